```python
import math
import jax, jax.numpy as jnp
from jax import lax
import numpy as np

D_MODEL = 2048
BATCH = 8
SEQ = 2048
DEPTH = 2

N_MEM = 256
N_MIXERS = 2
EXPAND = 2
INNER = EXPAND * D_MODEL
XA_HEADS = 4
XA_WIDTH = INNER // 4
XA_DIM = XA_WIDTH // XA_HEADS
MIX_WIDTH = INNER - XA_WIDTH
DN_HEAD_DIM = 128
DN_V_HEADS = MIX_WIDTH // DN_HEAD_DIM
DN_QK_HEADS = DN_V_HEADS // 2
DN_QK_WIDTH = DN_QK_HEADS * DN_HEAD_DIM
DN_CONV = 4
DN_CHUNK = 64
DN_MIX_COLS = 2 * DN_QK_WIDTH + MIX_WIDTH + 2 * DN_V_HEADS
DN_PROJ = DN_MIX_COLS + XA_WIDTH + INNER
SB_HEAD_DIM = 128
SB_HEADS = MIX_WIDTH // SB_HEAD_DIM
SB_BLOCK = 128
SB_MIX_COLS = 3 * MIX_WIDTH
SB_PROJ = SB_MIX_COLS + XA_WIDTH + INNER
EPS = 1e-6

kernel_name = "hybrid_deltanet_stickbreaking_memxattn"


def rms_norm(x, g):
    xf = x.astype(jnp.float32)
    y = xf * lax.rsqrt(jnp.mean(xf * xf, axis=-1, keepdims=True) + EPS)
    return (y * g.astype(jnp.float32)).astype(x.dtype)


def l2_norm(x):
    xf = x.astype(jnp.float32)
    return (xf * lax.rsqrt(jnp.sum(xf * xf, axis=-1, keepdims=True) + EPS)).astype(x.dtype)


def causal_depthwise_conv(x, w):
    c = x.shape[-1]
    return lax.conv_general_dilated(
        x, w[:, None, :].astype(x.dtype), window_strides=(1,),
        padding=((w.shape[0] - 1, 0),), dimension_numbers=("NWC", "WIO", "NWC"),
        feature_group_count=c)


def gated_delta_rule(q, k, v, g, beta):
    f32 = jnp.float32
    B, H, S, dk = q.shape
    dv = v.shape[-1]
    C, N = DN_CHUNK, S // DN_CHUNK
    q, k, v = (t.astype(f32).reshape(B, H, N, C, t.shape[-1]) for t in (q, k, v))
    g = g.astype(f32).reshape(B, H, N, C)
    beta = beta.astype(f32).reshape(B, H, N, C)
    gc = jnp.cumsum(g, axis=-1)
    idx = jnp.arange(C)
    incl = idx[:, None] >= idx[None, :]
    strict = idx[:, None] > idx[None, :]
    decay = jnp.exp(jnp.where(incl, gc[..., :, None] - gc[..., None, :], -jnp.inf))
    kk = jnp.einsum("bhncd,bhnjd->bhncj", k, k)
    m = jnp.where(strict, beta[..., :, None] * kk * decay, 0.0) + jnp.eye(C, dtype=f32)
    rhs = jnp.concatenate([v * beta[..., None], k * (beta * jnp.exp(gc))[..., None]], axis=-1)
    sol = lax.linalg.triangular_solve(m, rhs, left_side=True, lower=True, unit_diagonal=True)
    u0, w = sol[..., :dv], sol[..., dv:]
    qk = jnp.einsum("bhncd,bhnjd->bhncj", q, k) * decay
    q_dec = q * jnp.exp(gc)[..., None]
    k_dec = k * jnp.exp(gc[..., -1:] - gc)[..., None]
    chunk_decay = jnp.exp(gc[..., -1])

    def step(state, inp):
        u0_c, w_c, qk_c, qd_c, kd_c, cd_c = inp
        u = u0_c - jnp.einsum("bhcd,bhde->bhce", w_c, state)
        o = jnp.einsum("bhcd,bhde->bhce", qd_c, state) + jnp.einsum("bhcj,bhje->bhce", qk_c, u)
        state = cd_c[..., None, None] * state + jnp.einsum("bhcd,bhce->bhde", kd_c, u)
        return state, o

    xs = tuple(jnp.moveaxis(t, 2, 0) for t in (u0, w, qk, q_dec, k_dec, chunk_decay))
    _, o = lax.scan(step, jnp.zeros((B, H, dk, dv), f32), xs)
    return jnp.moveaxis(o, 0, 2).reshape(B, H, S, dv)


def deltanet_branch(p, conv_w, a_log, dt_bias, out_g):
    B, S, _ = p.shape
    c_qkv = 2 * DN_QK_WIDTH + MIX_WIDTH
    qkv = jax.nn.silu(causal_depthwise_conv(p[..., :c_qkv], conv_w))
    q = qkv[..., :DN_QK_WIDTH].reshape(B, S, DN_QK_HEADS, DN_HEAD_DIM)
    k = qkv[..., DN_QK_WIDTH:2 * DN_QK_WIDTH].reshape(B, S, DN_QK_HEADS, DN_HEAD_DIM)
    v = qkv[..., 2 * DN_QK_WIDTH:].reshape(B, S, DN_V_HEADS, DN_HEAD_DIM)
    a = p[..., c_qkv:c_qkv + DN_V_HEADS].astype(jnp.float32)
    b = p[..., c_qkv + DN_V_HEADS:].astype(jnp.float32)
    rep = DN_V_HEADS // DN_QK_HEADS
    q = jnp.repeat(l2_norm(q), rep, axis=2) * DN_HEAD_DIM ** -0.5
    k = jnp.repeat(l2_norm(k), rep, axis=2)
    g = -jnp.exp(a_log.astype(jnp.float32)) * jax.nn.softplus(a + dt_bias.astype(jnp.float32))
    beta = jax.nn.sigmoid(b)
    o = gated_delta_rule(q.transpose(0, 2, 1, 3), k.transpose(0, 2, 1, 3), v.transpose(0, 2, 1, 3),
                         g.transpose(0, 2, 1), beta.transpose(0, 2, 1))
    o = rms_norm(o.astype(p.dtype), out_g)
    return o.transpose(0, 2, 1, 3).reshape(B, S, MIX_WIDTH)


def stick_breaking_attention(q, k, v):
    B, H, S, d = q.shape
    scale = d ** -0.5
    outs = []
    for blk in range(S // SB_BLOCK):
        q0 = blk * SB_BLOCK
        kv_len = q0 + SB_BLOCK
        qb = q[:, :, q0:kv_len]
        kb, vb = k[:, :, :kv_len], v[:, :, :kv_len]
        z = jnp.einsum("bhtd,bhsd->bhts", qb, kb).astype(jnp.float32) * scale
        t_pos = q0 + jnp.arange(SB_BLOCK)
        s_pos = jnp.arange(kv_len)
        mask = s_pos[None, :] < t_pos[:, None]
        log_rest = jnp.where(mask, jax.nn.log_sigmoid(-z), 0.0)
        later = lax.cumsum(log_rest, axis=3, reverse=True) - log_rest
        wts = jnp.where(mask, jnp.exp(jax.nn.log_sigmoid(z) + later), 0.0)
        outs.append(jnp.einsum("bhts,bhsd->bhtd", wts.astype(v.dtype), vb))
    return jnp.concatenate(outs, axis=2)


def stick_breaking_branch(p, qn_g, kn_g):
    B, S, _ = p.shape
    q, k, v = (t.reshape(B, S, SB_HEADS, SB_HEAD_DIM) for t in jnp.split(p, 3, axis=-1))
    q, k = rms_norm(q, qn_g), rms_norm(k, kn_g)
    o = stick_breaking_attention(q.transpose(0, 2, 1, 3), k.transpose(0, 2, 1, 3), v.transpose(0, 2, 1, 3))
    return o.transpose(0, 2, 1, 3).reshape(B, S, MIX_WIDTH)


def memory_cross_attention(xq, mem_n, w_kv, qn_g, kn_g):
    B, S, _ = xq.shape
    q = rms_norm(xq.reshape(B, S, XA_HEADS, XA_DIM), qn_g)
    kv = jnp.einsum("bmd,de->bme", mem_n, w_kv)
    k = rms_norm(kv[..., :XA_WIDTH].reshape(B, -1, XA_HEADS, XA_DIM), kn_g)
    v = kv[..., XA_WIDTH:].reshape(B, -1, XA_HEADS, XA_DIM)
    s = jnp.einsum("bthd,bmhd->bhtm", q, k).astype(jnp.float32) * XA_DIM ** -0.5
    p = jax.nn.softmax(s, axis=-1).astype(v.dtype)
    return jnp.einsum("bhtm,bmhd->bthd", p, v).reshape(B, S, XA_WIDTH)


def _fwd_setup_inputs(seed: int = 0) -> dict:
    key = jax.random.key(seed)
    ks = jax.random.split(key, 20)
    f32 = jnp.float32
    n_dn = (DEPTH + N_MIXERS - 1) // N_MIXERS
    n_sb = DEPTH // N_MIXERS

    def dense(k, shape, fan_in):
        return jax.random.normal(k, shape, f32) * fan_in ** -0.5

    def gain(k, shape):
        return 1.0 + 0.02 * jax.random.normal(k, shape, f32)

    dt = jnp.exp(jax.random.uniform(ks[10], (n_dn, DN_V_HEADS), f32,
                                    minval=math.log(1e-3), maxval=math.log(1e-1)))
    return {
        "x": jax.random.normal(ks[0], (BATCH, SEQ, D_MODEL), f32),
        "mem": jax.random.normal(ks[1], (BATCH, N_MEM, D_MODEL), f32),
        "norm_g": gain(ks[2], (DEPTH, D_MODEL)),
        "mem_norm_g": gain(ks[3], (D_MODEL,)),
        "mem_w_kv": dense(ks[4], (DEPTH, D_MODEL, 2 * XA_WIDTH), D_MODEL),
        "xa_q_norm_g": gain(ks[5], (DEPTH, XA_DIM)),
        "xa_k_norm_g": gain(ks[6], (DEPTH, XA_DIM)),
        "w_out": dense(ks[7], (DEPTH, INNER, D_MODEL), INNER),
        "dn_w_in": dense(ks[8], (n_dn, D_MODEL, DN_PROJ), D_MODEL),
        "dn_conv_w": dense(ks[9], (n_dn, DN_CONV, 2 * DN_QK_WIDTH + MIX_WIDTH), DN_CONV),
        "dn_a_log": jnp.log(jax.random.uniform(ks[11], (n_dn, DN_V_HEADS), f32, minval=1.0, maxval=16.0)),
        "dn_dt_bias": dt + jnp.log(-jnp.expm1(-dt)),
        "dn_out_norm_g": gain(ks[12], (n_dn, DN_HEAD_DIM)),
        "sb_w_in": dense(ks[13], (n_sb, D_MODEL, SB_PROJ), D_MODEL),
        "sb_q_norm_g": gain(ks[14], (n_sb, SB_HEAD_DIM)),
        "sb_k_norm_g": gain(ks[15], (n_sb, SB_HEAD_DIM)),
    }


def _fwd_reference(x, mem, norm_g, mem_norm_g, mem_w_kv, xa_q_norm_g, xa_k_norm_g, w_out,
              dn_w_in, dn_conv_w, dn_a_log, dn_dt_bias, dn_out_norm_g,
              sb_w_in, sb_q_norm_g, sb_k_norm_g):
    mem_n = rms_norm(mem, mem_norm_g)
    for i in range(DEPTH):
        h = rms_norm(x, norm_g[i])
        j = i // N_MIXERS
        if i % N_MIXERS == 0:
            proj = jnp.einsum("bsd,de->bse", h, dn_w_in[j])
            mix = deltanet_branch(proj[..., :DN_MIX_COLS], dn_conv_w[j], dn_a_log[j],
                                  dn_dt_bias[j], dn_out_norm_g[j])
        else:
            proj = jnp.einsum("bsd,de->bse", h, sb_w_in[j])
            mix = stick_breaking_branch(proj[..., :SB_MIX_COLS], sb_q_norm_g[j], sb_k_norm_g[j])
        xq = proj[..., -(XA_WIDTH + INNER):-INNER]
        z = proj[..., -INNER:]
        xa = memory_cross_attention(xq, mem_n, mem_w_kv[i], xa_q_norm_g[i], xa_k_norm_g[i])
        y = jnp.concatenate([mix, xa], axis=-1) * jax.nn.silu(z)
        x = x + jnp.einsum("bse,ed->bsd", y, w_out[i])
    return x


import jax as _jax
import jax.numpy as _jnp

TWIN_FORMAT = 'train_step'
FWD_PARAMS = ['x', 'mem', 'norm_g', 'mem_norm_g', 'mem_w_kv', 'xa_q_norm_g', 'xa_k_norm_g', 'w_out', 'dn_w_in', 'dn_conv_w', 'dn_a_log', 'dn_dt_bias', 'dn_out_norm_g', 'sb_w_in', 'sb_q_norm_g', 'sb_k_norm_g']
TWIN_WEIGHTS = ['norm_g', 'mem_norm_g', 'mem_w_kv', 'xa_q_norm_g', 'xa_k_norm_g', 'w_out', 'dn_w_in', 'dn_conv_w', 'dn_a_log', 'dn_dt_bias', 'dn_out_norm_g', 'sb_w_in', 'sb_q_norm_g', 'sb_k_norm_g']
TWIN_DIFF_INPUT = 'x'
TWIN_INPUTS = ['x', 'mem', 'norm_g', 'mem_norm_g', 'mem_w_kv', 'xa_q_norm_g', 'xa_k_norm_g', 'w_out', 'dn_w_in', 'dn_conv_w', 'dn_a_log', 'dn_dt_bias', 'dn_out_norm_g', 'sb_w_in', 'sb_q_norm_g', 'sb_k_norm_g', 'loss_target', 'm_norm_g', 'm_mem_norm_g', 'm_mem_w_kv', 'm_xa_q_norm_g', 'm_xa_k_norm_g', 'm_w_out', 'm_dn_w_in', 'm_dn_conv_w', 'm_dn_a_log', 'm_dn_dt_bias', 'm_dn_out_norm_g', 'm_sb_w_in', 'm_sb_q_norm_g', 'm_sb_k_norm_g', 'v_norm_g', 'v_mem_norm_g', 'v_mem_w_kv', 'v_xa_q_norm_g', 'v_xa_k_norm_g', 'v_w_out', 'v_dn_w_in', 'v_dn_conv_w', 'v_dn_a_log', 'v_dn_dt_bias', 'v_dn_out_norm_g', 'v_sb_w_in', 'v_sb_q_norm_g', 'v_sb_k_norm_g']
TWIN_OUTPUTS = ['loss', 'grad_x', 'grad_norm_g', 'grad_mem_norm_g', 'grad_mem_w_kv', 'grad_xa_q_norm_g', 'grad_xa_k_norm_g', 'grad_w_out', 'grad_dn_w_in', 'grad_dn_conv_w', 'grad_dn_a_log', 'grad_dn_dt_bias', 'grad_dn_out_norm_g', 'grad_sb_w_in', 'grad_sb_q_norm_g', 'grad_sb_k_norm_g', 'delta_norm_g', 'delta_mem_norm_g', 'delta_mem_w_kv', 'delta_xa_q_norm_g', 'delta_xa_k_norm_g', 'delta_w_out', 'delta_dn_w_in', 'delta_dn_conv_w', 'delta_dn_a_log', 'delta_dn_dt_bias', 'delta_dn_out_norm_g', 'delta_sb_w_in', 'delta_sb_q_norm_g', 'delta_sb_k_norm_g', 'new_m_norm_g', 'new_m_mem_norm_g', 'new_m_mem_w_kv', 'new_m_xa_q_norm_g', 'new_m_xa_k_norm_g', 'new_m_w_out', 'new_m_dn_w_in', 'new_m_dn_conv_w', 'new_m_dn_a_log', 'new_m_dn_dt_bias', 'new_m_dn_out_norm_g', 'new_m_sb_w_in', 'new_m_sb_q_norm_g', 'new_m_sb_k_norm_g', 'new_v_norm_g', 'new_v_mem_norm_g', 'new_v_mem_w_kv', 'new_v_xa_q_norm_g', 'new_v_xa_k_norm_g', 'new_v_w_out', 'new_v_dn_w_in', 'new_v_dn_conv_w', 'new_v_dn_a_log', 'new_v_dn_dt_bias', 'new_v_dn_out_norm_g', 'new_v_sb_w_in', 'new_v_sb_q_norm_g', 'new_v_sb_k_norm_g']
TWIN_LEAF_KINDS = {'loss': 'loss', 'grad_x': 'grad_x', 'grad_norm_g': 'grad_w', 'grad_mem_norm_g': 'grad_w', 'grad_mem_w_kv': 'grad_w', 'grad_xa_q_norm_g': 'grad_w', 'grad_xa_k_norm_g': 'grad_w', 'grad_w_out': 'grad_w', 'grad_dn_w_in': 'grad_w', 'grad_dn_conv_w': 'grad_w', 'grad_dn_a_log': 'grad_w', 'grad_dn_dt_bias': 'grad_w', 'grad_dn_out_norm_g': 'grad_w', 'grad_sb_w_in': 'grad_w', 'grad_sb_q_norm_g': 'grad_w', 'grad_sb_k_norm_g': 'grad_w', 'delta_norm_g': 'delta_w', 'delta_mem_norm_g': 'delta_w', 'delta_mem_w_kv': 'delta_w', 'delta_xa_q_norm_g': 'delta_w', 'delta_xa_k_norm_g': 'delta_w', 'delta_w_out': 'delta_w', 'delta_dn_w_in': 'delta_w', 'delta_dn_conv_w': 'delta_w', 'delta_dn_a_log': 'delta_w', 'delta_dn_dt_bias': 'delta_w', 'delta_dn_out_norm_g': 'delta_w', 'delta_sb_w_in': 'delta_w', 'delta_sb_q_norm_g': 'delta_w', 'delta_sb_k_norm_g': 'delta_w', 'new_m_norm_g': 'new_m', 'new_m_mem_norm_g': 'new_m', 'new_m_mem_w_kv': 'new_m', 'new_m_xa_q_norm_g': 'new_m', 'new_m_xa_k_norm_g': 'new_m', 'new_m_w_out': 'new_m', 'new_m_dn_w_in': 'new_m', 'new_m_dn_conv_w': 'new_m', 'new_m_dn_a_log': 'new_m', 'new_m_dn_dt_bias': 'new_m', 'new_m_dn_out_norm_g': 'new_m', 'new_m_sb_w_in': 'new_m', 'new_m_sb_q_norm_g': 'new_m', 'new_m_sb_k_norm_g': 'new_m', 'new_v_norm_g': 'new_v', 'new_v_mem_norm_g': 'new_v', 'new_v_mem_w_kv': 'new_v', 'new_v_xa_q_norm_g': 'new_v', 'new_v_xa_k_norm_g': 'new_v', 'new_v_w_out': 'new_v', 'new_v_dn_w_in': 'new_v', 'new_v_dn_conv_w': 'new_v', 'new_v_dn_a_log': 'new_v', 'new_v_dn_dt_bias': 'new_v', 'new_v_dn_out_norm_g': 'new_v', 'new_v_sb_w_in': 'new_v', 'new_v_sb_q_norm_g': 'new_v', 'new_v_sb_k_norm_g': 'new_v'}


def _forward(args):
    return _fwd_reference(*[args[k] for k in FWD_PARAMS])


def _output_shape():
    out = _jax.eval_shape(lambda: _forward(_fwd_setup_inputs(0)))
    return out.shape, out.dtype

N_MICROBATCH = 1
ADAM_LR = 0.001
ADAM_B1 = 0.9
ADAM_B2 = 0.999
ADAM_EPS = 1e-08
ADAM_WD = 0.01
ADAM_STEP = 10
PER_EXAMPLE_BATCH_AXIS = {'x': 0, 'mem': 0, 'loss_target': 0}
SHARED_INPUTS = []
_WEIGHT_DTYPES = {'norm_g': _jnp.float32, 'mem_norm_g': _jnp.float32, 'mem_w_kv': _jnp.float32, 'xa_q_norm_g': _jnp.float32, 'xa_k_norm_g': _jnp.float32, 'w_out': _jnp.float32, 'dn_w_in': _jnp.float32, 'dn_conv_w': _jnp.float32, 'dn_a_log': _jnp.float32, 'dn_dt_bias': _jnp.float32, 'dn_out_norm_g': _jnp.float32, 'sb_w_in': _jnp.float32, 'sb_q_norm_g': _jnp.float32, 'sb_k_norm_g': _jnp.float32}
MOMENT_SCALE = {'norm_g': 2.278626e+00, 'mem_norm_g': 1.443826e-02, 'mem_w_kv': 5.451849e-03, 'xa_q_norm_g': 4.997048e-02, 'xa_k_norm_g': 5.010260e-02, 'w_out': 9.324615e-02, 'dn_w_in': 6.036541e-02, 'dn_conv_w': 8.851325e-02, 'dn_a_log': 4.637579e+00, 'dn_dt_bias': 4.385409e+00, 'dn_out_norm_g': 3.338114e+01, 'sb_w_in': 3.044979e-02, 'sb_q_norm_g': 2.090995e+00, 'sb_k_norm_g': 2.095941e+00}


def _to_microbatches(a, axis):
    t = _jnp.moveaxis(a, axis, 0)
    t = t.reshape((N_MICROBATCH, t.shape[0] // N_MICROBATCH) + t.shape[1:])
    return _jnp.moveaxis(t, 1, axis + 1)


def setup_inputs(seed: int = 0) -> dict:
    inp = _fwd_setup_inputs(seed)
    key = _jax.random.fold_in(_jax.random.key(seed), 7919)
    shape, _ = _output_shape()
    out = dict(inp)
    out["loss_target"] = _jax.random.normal(_jax.random.fold_in(key, 0), shape, _jnp.float32)
    for i, name in enumerate(TWIN_WEIGHTS):
        w = inp[name].astype(_jnp.float32)
        if MOMENT_SCALE is None:
            s = _jnp.sqrt(_jnp.mean(_jnp.square(w)) + 1e-30)
        else:
            s = MOMENT_SCALE[name]
        km, kv = _jax.random.split(_jax.random.fold_in(key, i + 1))
        out[name] = w
        out["m_" + name] = s * _jax.random.normal(km, w.shape, _jnp.float32)
        out["v_" + name] = (s * s) * _jax.random.uniform(kv, w.shape, _jnp.float32, 0.5, 1.5)
    if N_MICROBATCH > 1:
        for name, axis in PER_EXAMPLE_BATCH_AXIS.items():
            out[name] = _to_microbatches(out[name], axis)
    return {'x': out['x'], 'mem': out['mem'], 'norm_g': out['norm_g'], 'mem_norm_g': out['mem_norm_g'], 'mem_w_kv': out['mem_w_kv'], 'xa_q_norm_g': out['xa_q_norm_g'], 'xa_k_norm_g': out['xa_k_norm_g'], 'w_out': out['w_out'], 'dn_w_in': out['dn_w_in'], 'dn_conv_w': out['dn_conv_w'], 'dn_a_log': out['dn_a_log'], 'dn_dt_bias': out['dn_dt_bias'], 'dn_out_norm_g': out['dn_out_norm_g'], 'sb_w_in': out['sb_w_in'], 'sb_q_norm_g': out['sb_q_norm_g'], 'sb_k_norm_g': out['sb_k_norm_g'], 'loss_target': out['loss_target'], 'm_norm_g': out['m_norm_g'], 'm_mem_norm_g': out['m_mem_norm_g'], 'm_mem_w_kv': out['m_mem_w_kv'], 'm_xa_q_norm_g': out['m_xa_q_norm_g'], 'm_xa_k_norm_g': out['m_xa_k_norm_g'], 'm_w_out': out['m_w_out'], 'm_dn_w_in': out['m_dn_w_in'], 'm_dn_conv_w': out['m_dn_conv_w'], 'm_dn_a_log': out['m_dn_a_log'], 'm_dn_dt_bias': out['m_dn_dt_bias'], 'm_dn_out_norm_g': out['m_dn_out_norm_g'], 'm_sb_w_in': out['m_sb_w_in'], 'm_sb_q_norm_g': out['m_sb_q_norm_g'], 'm_sb_k_norm_g': out['m_sb_k_norm_g'], 'v_norm_g': out['v_norm_g'], 'v_mem_norm_g': out['v_mem_norm_g'], 'v_mem_w_kv': out['v_mem_w_kv'], 'v_xa_q_norm_g': out['v_xa_q_norm_g'], 'v_xa_k_norm_g': out['v_xa_k_norm_g'], 'v_w_out': out['v_w_out'], 'v_dn_w_in': out['v_dn_w_in'], 'v_dn_conv_w': out['v_dn_conv_w'], 'v_dn_a_log': out['v_dn_a_log'], 'v_dn_dt_bias': out['v_dn_dt_bias'], 'v_dn_out_norm_g': out['v_dn_out_norm_g'], 'v_sb_w_in': out['v_sb_w_in'], 'v_sb_q_norm_g': out['v_sb_q_norm_g'], 'v_sb_k_norm_g': out['v_sb_k_norm_g']}


def _loss(weights, diff, rest, loss_target):
    with _jax.named_scope("forward"):
        args = {**rest, TWIN_DIFF_INPUT: diff, **{k: w.astype(_WEIGHT_DTYPES[k]) for k, w in weights.items()}}
        y = _forward(args)
    with _jax.named_scope("loss_head"):
        err = _jnp.square(y.astype(_jnp.float32) - loss_target)
        return 0.5 * _jnp.sum(_jnp.mean(err, axis=-1)) if err.ndim else 0.5 * err


def _adamw(w, g, m, v):
    m = ADAM_B1 * m + (1.0 - ADAM_B1) * g
    v = ADAM_B2 * v + (1.0 - ADAM_B2) * _jnp.square(g)
    m_hat = m / (1.0 - ADAM_B1 ** ADAM_STEP)
    v_hat = v / (1.0 - ADAM_B2 ** ADAM_STEP)
    delta = -ADAM_LR * (m_hat / (_jnp.sqrt(v_hat) + ADAM_EPS) + ADAM_WD * w)
    return delta, m, v


def reference(x, mem, norm_g, mem_norm_g, mem_w_kv, xa_q_norm_g, xa_k_norm_g, w_out, dn_w_in, dn_conv_w, dn_a_log, dn_dt_bias, dn_out_norm_g, sb_w_in, sb_q_norm_g, sb_k_norm_g, loss_target, m_norm_g, m_mem_norm_g, m_mem_w_kv, m_xa_q_norm_g, m_xa_k_norm_g, m_w_out, m_dn_w_in, m_dn_conv_w, m_dn_a_log, m_dn_dt_bias, m_dn_out_norm_g, m_sb_w_in, m_sb_q_norm_g, m_sb_k_norm_g, v_norm_g, v_mem_norm_g, v_mem_w_kv, v_xa_q_norm_g, v_xa_k_norm_g, v_w_out, v_dn_w_in, v_dn_conv_w, v_dn_a_log, v_dn_dt_bias, v_dn_out_norm_g, v_sb_w_in, v_sb_q_norm_g, v_sb_k_norm_g):
    given = dict(x=x, mem=mem, norm_g=norm_g, mem_norm_g=mem_norm_g, mem_w_kv=mem_w_kv, xa_q_norm_g=xa_q_norm_g, xa_k_norm_g=xa_k_norm_g, w_out=w_out, dn_w_in=dn_w_in, dn_conv_w=dn_conv_w, dn_a_log=dn_a_log, dn_dt_bias=dn_dt_bias, dn_out_norm_g=dn_out_norm_g, sb_w_in=sb_w_in, sb_q_norm_g=sb_q_norm_g, sb_k_norm_g=sb_k_norm_g, loss_target=loss_target, m_norm_g=m_norm_g, m_mem_norm_g=m_mem_norm_g, m_mem_w_kv=m_mem_w_kv, m_xa_q_norm_g=m_xa_q_norm_g, m_xa_k_norm_g=m_xa_k_norm_g, m_w_out=m_w_out, m_dn_w_in=m_dn_w_in, m_dn_conv_w=m_dn_conv_w, m_dn_a_log=m_dn_a_log, m_dn_dt_bias=m_dn_dt_bias, m_dn_out_norm_g=m_dn_out_norm_g, m_sb_w_in=m_sb_w_in, m_sb_q_norm_g=m_sb_q_norm_g, m_sb_k_norm_g=m_sb_k_norm_g, v_norm_g=v_norm_g, v_mem_norm_g=v_mem_norm_g, v_mem_w_kv=v_mem_w_kv, v_xa_q_norm_g=v_xa_q_norm_g, v_xa_k_norm_g=v_xa_k_norm_g, v_w_out=v_w_out, v_dn_w_in=v_dn_w_in, v_dn_conv_w=v_dn_conv_w, v_dn_a_log=v_dn_a_log, v_dn_dt_bias=v_dn_dt_bias, v_dn_out_norm_g=v_dn_out_norm_g, v_sb_w_in=v_sb_w_in, v_sb_q_norm_g=v_sb_q_norm_g, v_sb_k_norm_g=v_sb_k_norm_g)
    weights = {n: given[n] for n in TWIN_WEIGHTS}
    shared = {n: given[n] for n in SHARED_INPUTS}
    per_example = {n: given[n] for n in ['x', 'mem']}
    grad_fn = _jax.value_and_grad(_loss, argnums=(0, 1))

    def one_microbatch(ex, loss_target):
        ex = dict(ex)
        diff = ex.pop(TWIN_DIFF_INPUT)
        return grad_fn(weights, diff, {**shared, **ex}, loss_target)

    if N_MICROBATCH == 1:
        loss, (grad_w, grad_x) = one_microbatch(per_example, given["loss_target"])
    else:
        def body(carry, xs):
            loss_sum, grad_sum = carry
            l_k, (gw_k, gx_k) = one_microbatch(xs[0], xs[1])
            with _jax.named_scope("update"):
                return (loss_sum + l_k, _jax.tree.map(_jnp.add, grad_sum, gw_k)), gx_k

        init = (_jnp.zeros((), _jnp.float32), _jax.tree.map(_jnp.zeros_like, weights))
        (loss, grad_w), grad_x = _jax.lax.scan(body, init, (per_example, given["loss_target"]))
    with _jax.named_scope("update"):
        delta_w, new_m, new_v = {}, {}, {}
        for n in TWIN_WEIGHTS:
            delta_w[n], new_m[n], new_v[n] = _adamw(weights[n], grad_w[n], given["m_" + n], given["v_" + n])
    return (loss, grad_x, *[grad_w[n] for n in TWIN_WEIGHTS], *[delta_w[n] for n in TWIN_WEIGHTS],
            *[new_m[n] for n in TWIN_WEIGHTS], *[new_v[n] for n in TWIN_WEIGHTS])
```

```python
import functools

import jax
import jax.numpy as jnp
from jax import lax
from jax.experimental import pallas as pl
from jax.experimental.pallas import tpu as pltpu

F32 = jnp.float32
BF16 = jnp.bfloat16

D_MODEL = 2048
SEQ = 2048
N_MEM = 256
INNER = 4096
XA_HEADS = 4
XA_WIDTH = 1024
XA_DIM = 256
MIX_WIDTH = 3072
HEAD_DIM = 128
DN_V_HEADS = 24
DN_QK_HEADS = 12
DN_QK_WIDTH = 1536
DN_CHUNK = 64
DN_QKV = 2 * DN_QK_WIDTH + MIX_WIDTH
DN_PROJ = 11312
SB_HEADS = 24
SB_PROJ = 14336
EPS = 1e-6
N_DEV = 8
DN_SHARD = DN_PROJ // N_DEV
DN_SHARD_PAD = 1536
LANE = 128
DN_COLS = 90 * LANE
DN_AB_BLK, DN_PAD_BLK, DN_XQ_BLK, DN_Z_BLK = 48, 49, 50, 58
SB_XQ_BLK, SB_Z_BLK = 72, 80

ADAM_LR, ADAM_B1, ADAM_B2, ADAM_EPS, ADAM_WD, ADAM_STEP = 0.001, 0.9, 0.999, 1e-08, 0.01, 10

VMEM_LIMIT = 56 * 1024 * 1024
MESH = pl.DeviceIdType.MESH

_NN = (((1,), (0,)), ((), ()))
_NT = (((1,), (1,)), ((), ()))
_TN = (((0,), (0,)), ((), ()))


def _params(*sem):
    return pltpu.CompilerParams(dimension_semantics=sem if sem else None, vmem_limit_bytes=VMEM_LIMIT)


def _dg(a, b, dims):
    return lax.dot_general(a.astype(BF16), b.astype(BF16), dims, preferred_element_type=F32)


@jax.custom_vjp
def mm(a, b):
    return _dg(a, b, _NN)


@jax.custom_vjp
def mm_nt(a, b):
    return _dg(a, b, _NT)


@jax.custom_vjp
def mm_tn(a, b):
    return _dg(a, b, _TN)


mm.defvjp(lambda a, b: (_dg(a, b, _NN), (a, b)), lambda r, g: (mm_nt(g, r[1]), mm_tn(r[0], g)))
mm_nt.defvjp(lambda a, b: (_dg(a, b, _NT), (a, b)), lambda r, g: (mm(g, r[1]), mm_tn(g, r[0])))
mm_tn.defvjp(lambda a, b: (_dg(a, b, _TN), (a, b)), lambda r, g: (mm_nt(r[1], g), mm(r[0], g)))


def _split3(x):
    hi = x.astype(BF16)
    r1 = x - hi.astype(F32)
    mid = r1.astype(BF16)
    lo = (r1 - mid.astype(F32)).astype(BF16)
    return hi, mid, lo


def _dg3(a, b, dims):
    ah, am, _ = _split3(a)
    bh, bm, _ = _split3(b)
    d = lambda x, y: lax.dot_general(x, y, dims, preferred_element_type=F32)
    return d(ah, bh) + (d(ah, bm) + d(am, bh))


def _dg_exact_rhs(a, b01, dims):
    ah, am, al = _split3(a)
    b = b01.astype(BF16)
    d = lambda x: lax.dot_general(x, b, dims, preferred_element_type=F32)
    return d(ah) + (d(am) + d(al))


def _dg_exact_lhs(a01, b, dims):
    bh, bm, bl = _split3(b)
    a = a01.astype(BF16)
    d = lambda y: lax.dot_general(a, y, dims, preferred_element_type=F32)
    return d(bh) + (d(bm) + d(bl))


@jax.custom_vjp
def mm3(a, b):
    return _dg3(a, b, _NN)


@jax.custom_vjp
def mm3_nt(a, b):
    return _dg3(a, b, _NT)


@jax.custom_vjp
def mm3_tn(a, b):
    return _dg3(a, b, _TN)


mm3.defvjp(lambda a, b: (_dg3(a, b, _NN), (a, b)), lambda r, g: (mm3_nt(g, r[1]), mm3_tn(r[0], g)))
mm3_nt.defvjp(lambda a, b: (_dg3(a, b, _NT), (a, b)), lambda r, g: (mm3(g, r[1]), mm3_tn(g, r[0])))
mm3_tn.defvjp(lambda a, b: (_dg3(a, b, _TN), (a, b)), lambda r, g: (mm3_nt(r[1], g), mm3(r[0], g)))


def _softplus(x):
    return jnp.maximum(x, 0.0) + jnp.log(1.0 + jnp.exp(-jnp.abs(x)))


def _log_sigmoid(x):
    return jnp.minimum(x, 0.0) - jnp.log(1.0 + jnp.exp(-jnp.abs(x)))


def _sigmoid(x):
    return 1.0 / (1.0 + jnp.exp(-x))


def _silu(x):
    return x * _sigmoid(x)


def _silu_grad(x):
    s = _sigmoid(x)
    return s * (1.0 + x * (1.0 - s))


@jax.custom_vjp
def mm01(a01, b):
    return _dg_exact_lhs(a01, b, _NN)


mm01.defvjp(lambda a, b: (_dg_exact_lhs(a, b, _NN), a),
            lambda a, g: (jnp.zeros_like(a), _dg_exact_lhs(a, g, _TN)))


def _lane_pick(x, idx):
    lane = lax.broadcasted_iota(jnp.int32, x.shape, 1)
    return jnp.sum(jnp.where(lane == idx, x, 0.0), axis=1, keepdims=True)


def _dn_chunk(qt, kt, v, ab, alog, dtb, h):
    C = DN_CHUNK
    g = -jnp.exp(_lane_pick(alog, h)) * _softplus(_lane_pick(ab, h) + _lane_pick(dtb, h))
    beta = _sigmoid(_lane_pick(ab, h + DN_V_HEADS))
    q = qt * lax.rsqrt(jnp.sum(qt * qt, axis=1, keepdims=True) + EPS) * (HEAD_DIM ** -0.5)
    k = kt * lax.rsqrt(jnp.sum(kt * kt, axis=1, keepdims=True) + EPS)
    row = lax.broadcasted_iota(jnp.int32, (C, C), 0)
    col = lax.broadcasted_iota(jnp.int32, (C, C), 1)
    lower = (row >= col).astype(F32)
    ones = jnp.ones((C, C), F32)
    g_wide = jnp.broadcast_to(g, (C, LANE))
    g_sq = jnp.broadcast_to(g, (C, C))
    gc = mm01(lower, g_wide)
    gc_i = mm01(lower, g_sq)
    gc_j = mm01(ones, jnp.where(row <= col, g_sq, 0.0))
    g_last = mm01(ones, g_wide)
    decay = jnp.exp(jnp.where(row >= col, gc_i - gc_j, -1e30))
    eg = jnp.exp(gc)
    kk = mm_nt(k, k)
    a_mat = jnp.where(row > col, jnp.broadcast_to(beta, (C, C)) * kk * decay, 0.0)
    eye = (row == col).astype(F32)
    y = -a_mat
    t = eye + y
    for _ in range(5):
        y = mm3(y, y)
        t = t + mm3(t, y)
    bb = jnp.broadcast_to(beta, (C, LANE))
    u0 = mm3(t, v * bb)
    w = mm3(t, k * (bb * eg))
    qk = mm_nt(q, k) * decay
    q_dec = q * eg
    k_dec = k * jnp.exp(g_last - gc)
    cd = jnp.exp(g_last)[:8, :]
    return u0, w, qk, q_dec, k_dec, cd


def _shift_rows(x, j, down):
    if j == 0:
        return x
    n = x.shape[0]
    r = lax.broadcasted_iota(jnp.int32, x.shape, 0)
    if down:
        return jnp.where(r >= j, pltpu.roll(x, j, 0), 0.0)
    return jnp.where(r < n - j, pltpu.roll(x, n - j, 0), 0.0)


def dn_conv_fwd(proj, conv_w):
    s = proj.shape[0]

    def body(x_ref, w_ref, o_ref):
        x = x_ref[...]
        w = w_ref[...]
        pre = x * w[3:4, :]
        for j in (1, 2, 3):
            pre = pre + _shift_rows(x, j, True) * w[3 - j:4 - j, :]
        o_ref[...] = _silu(pre)

    return pl.pallas_call(
        body, name="dn_conv_fwd", grid=(DN_QKV // LANE,),
        in_specs=[pl.BlockSpec((s, LANE), lambda j: (0, j)), pl.BlockSpec((8, LANE), lambda j: (0, j))],
        out_specs=pl.BlockSpec((s, LANE), lambda j: (0, j)),
        out_shape=jax.ShapeDtypeStruct((s, DN_QKV), F32), compiler_params=_params("parallel"),
    )(proj, conv_w)


def dn_conv_bwd(dact, proj, conv_w, dproj_in, dab, blk0, name):
    s = proj.shape[0]
    nblk = dact.shape[1] // LANE
    extra = 2 if dab is not None else 0

    def body(*refs):
        if dab is not None:
            da_ref, x_ref, w_ref, _, dab_ref, dp_ref, dw_ref = refs
        else:
            da_ref, x_ref, w_ref, _, dp_ref, dw_ref = refs
        j = pl.program_id(0)

        @pl.when(j < nblk)
        def _():
            x = x_ref[...]
            w = w_ref[...]
            xs = [_shift_rows(x, 3 - kk_, True) for kk_ in range(4)]
            pre = xs[0] * w[0:1, :]
            for kk_ in (1, 2, 3):
                pre = pre + xs[kk_] * w[kk_:kk_ + 1, :]
            dpre = da_ref[...] * _silu_grad(pre)
            dx = dpre * w[3:4, :]
            for jj in (1, 2, 3):
                dx = dx + _shift_rows(dpre, jj, False) * w[3 - jj:4 - jj, :]
            dp_ref[...] = dx
            rows = [jnp.sum(dpre * xs[kk_], axis=0, keepdims=True) for kk_ in range(4)]
            dw_ref[...] = jnp.concatenate(rows + [jnp.zeros((4, LANE), F32)], axis=0)

        if dab is not None:
            @pl.when(j == nblk)
            def _():
                dp_ref[...] = dab_ref[...]

            @pl.when(j == nblk + 1)
            def _():
                dp_ref[...] = jnp.zeros_like(dp_ref)

    cl = lambda j: jnp.minimum(j, nblk - 1)
    in_specs = [pl.BlockSpec((s, LANE), lambda j: (0, cl(j))),
                pl.BlockSpec((s, LANE), lambda j: (0, blk0 + cl(j))),
                pl.BlockSpec((8, LANE), lambda j: (0, blk0 + cl(j))),
                pl.BlockSpec(memory_space=pl.ANY)]
    args = [dact, proj, conv_w, dproj_in]
    if dab is not None:
        in_specs.append(pl.BlockSpec((s, LANE), lambda j: (0, 0)))
        args.append(dab)
    return pl.pallas_call(
        body, name=name, grid=(nblk + extra,), in_specs=in_specs,
        out_specs=[pl.BlockSpec((s, LANE), lambda j: (0, blk0 + j)), pl.BlockSpec((8, LANE), lambda j: (0, cl(j)))],
        out_shape=[jax.ShapeDtypeStruct(dproj_in.shape, F32), jax.ShapeDtypeStruct((8, dact.shape[1]), F32)],
        input_output_aliases={3: 0}, compiler_params=_params("arbitrary"),
    )(*args)


DN_GROUP = 4
DN_ROWS = DN_GROUP * DN_CHUNK


def dn_prep_fwd(act, proj, alog, dtb):
    s = act.shape[0]
    nc = s // DN_CHUNK
    C = DN_CHUNK

    def body(q_ref, k_ref, v_ref, ab_ref, al_ref, dt_ref, u0_ref, w_ref, qd_ref, kd_ref, qk_ref, cd_ref):
        qh = pl.program_id(0)
        al = al_ref[0:1, :]
        dt = dt_ref[0:1, :]
        for hv in range(2):
            for c in range(DN_GROUP):
                rs = slice(c * C, (c + 1) * C)
                cs = slice(hv * LANE, (hv + 1) * LANE)
                u0, w, qk, qd, kd, cd = _dn_chunk(q_ref[rs, :], k_ref[rs, :], v_ref[rs, cs], ab_ref[rs, :],
                                                  al, dt, 2 * qh + hv)
                u0_ref[rs, cs] = u0
                w_ref[rs, cs] = w
                qd_ref[rs, cs] = qd
                kd_ref[rs, cs] = kd
                qk_ref[hv, rs, :] = qk
                cd_ref[hv, c] = cd

    big = pl.BlockSpec((DN_ROWS, 2 * LANE), lambda h, g: (g, h))
    wide = jax.ShapeDtypeStruct((s, MIX_WIDTH), F32)
    return pl.pallas_call(
        body, name="dn_prep_fwd", grid=(DN_QK_HEADS, s // DN_ROWS),
        in_specs=[pl.BlockSpec((DN_ROWS, LANE), lambda h, g: (g, h)),
                  pl.BlockSpec((DN_ROWS, LANE), lambda h, g: (g, DN_QK_HEADS + h)),
                  pl.BlockSpec((DN_ROWS, 2 * LANE), lambda h, g: (g, DN_QK_HEADS + h)),
                  pl.BlockSpec((DN_ROWS, LANE), lambda h, g: (g, DN_AB_BLK)),
                  pl.BlockSpec((8, LANE), lambda h, g: (0, 0)), pl.BlockSpec((8, LANE), lambda h, g: (0, 0))],
        out_specs=[big, big, big, big,
                   pl.BlockSpec((2, DN_ROWS, C), lambda h, g: (h, g, 0)),
                   pl.BlockSpec((2, DN_GROUP, 8, LANE), lambda h, g: (h, g, 0, 0))],
        out_shape=[wide, wide, wide, wide, jax.ShapeDtypeStruct((DN_V_HEADS, s, C), F32),
                   jax.ShapeDtypeStruct((DN_V_HEADS, nc, 8, LANE), F32)],
        compiler_params=_params("parallel", "parallel"),
    )(act, act, act, proj, alog, dtb)


def dn_prep_bwd(act, proj, alog, dtb, du0, dw, dqd, dkd, dqk, dcd):
    s = act.shape[0]
    C = DN_CHUNK

    def body(q_ref, k_ref, v_ref, ab_ref, al_ref, dt_ref, du0_ref, dw_ref, dqd_ref, dkd_ref, dqk_ref, dcd_ref,
             dq_ref, dk_ref, dv_ref, dab_ref, dal_ref, ddt_ref):
        g_id = pl.program_id(0)
        qh = pl.program_id(1)
        al = al_ref[0:1, :]
        dt = dt_ref[0:1, :]

        @pl.when(qh == 0)
        def _():
            dab_ref[...] = jnp.zeros_like(dab_ref)

        @pl.when((qh == 0) & (g_id == 0))
        def _():
            dal_ref[...] = jnp.zeros_like(dal_ref)
            ddt_ref[...] = jnp.zeros_like(ddt_ref)

        for c in range(DN_GROUP):
            rs = slice(c * C, (c + 1) * C)
            dq_acc = jnp.zeros((C, LANE), F32)
            dk_acc = jnp.zeros((C, LANE), F32)
            for hv in range(2):
                cs = slice(hv * LANE, (hv + 1) * LANE)
                h = 2 * qh + hv
                f = lambda qt, kt, v, ab, a_, d_: _dn_chunk(qt, kt, v, ab, a_, d_, h)
                _, vjp = jax.vjp(f, q_ref[rs, :], k_ref[rs, :], v_ref[rs, cs], ab_ref[rs, :], al, dt)
                dq, dk, dv, dab, dal, ddt = vjp((du0_ref[rs, cs], dw_ref[rs, cs], dqk_ref[hv, rs, :], dqd_ref[rs, cs],
                                                 dkd_ref[rs, cs], dcd_ref[hv, c]))
                dq_acc = dq_acc + dq
                dk_acc = dk_acc + dk
                dv_ref[rs, cs] = dv
                dab_ref[rs, :] += dab
                dal_ref[0:1, :] += dal
                ddt_ref[0:1, :] += ddt
            dq_ref[rs, :] = dq_acc
            dk_ref[rs, :] = dk_acc

    big = pl.BlockSpec((DN_ROWS, 2 * LANE), lambda g, h: (g, h))
    one = pl.BlockSpec((DN_ROWS, LANE), lambda g, h: (g, h))
    small = pl.BlockSpec((8, LANE), lambda g, h: (0, 0))
    return pl.pallas_call(
        body, name="dn_prep_bwd", grid=(s // DN_ROWS, DN_QK_HEADS),
        in_specs=[one, pl.BlockSpec((DN_ROWS, LANE), lambda g, h: (g, DN_QK_HEADS + h)),
                  pl.BlockSpec((DN_ROWS, 2 * LANE), lambda g, h: (g, DN_QK_HEADS + h)),
                  pl.BlockSpec((DN_ROWS, LANE), lambda g, h: (g, DN_AB_BLK)), small, small,
                  big, big, big, big,
                  pl.BlockSpec((2, DN_ROWS, C), lambda g, h: (h, g, 0)),
                  pl.BlockSpec((2, DN_GROUP, 8, LANE), lambda g, h: (h, g, 0, 0))],
        out_specs=[one, one, big, pl.BlockSpec((DN_ROWS, LANE), lambda g, h: (g, 0)), small, small],
        out_shape=[jax.ShapeDtypeStruct((s, DN_QK_WIDTH), F32), jax.ShapeDtypeStruct((s, DN_QK_WIDTH), F32),
                   jax.ShapeDtypeStruct((s, MIX_WIDTH), F32), jax.ShapeDtypeStruct((s, LANE), F32),
                   jax.ShapeDtypeStruct((8, LANE), F32), jax.ShapeDtypeStruct((8, LANE), F32)],
        compiler_params=_params("arbitrary", "arbitrary"),
    )(act, act, act, proj, alog, dtb, du0, dw, dqd, dkd, dqk, dcd)


def dn_scan_fwd(u0, w, qd, kd, qk, cd, width):
    s = u0.shape[0]
    nc = s // DN_CHUNK
    C = DN_CHUNK

    def body(u0_ref, w_ref, qd_ref, kd_ref, qk_ref, cd_ref, o_ref, st_ref):
        def step(c, state):
            rs = pl.ds(pl.multiple_of(c * C, C), C)
            st_ref[0, c] = state
            u = u0_ref[rs, :] - _dg(w_ref[rs, :], state, _NN)
            o_ref[rs, :] = _dg(qd_ref[rs, :], state, _NN) + _dg(qk_ref[0, rs, :], u, _NN)
            return cd_ref[0, c][0:1, :] * state + _dg(kd_ref[rs, :], u, _TN)

        lax.fori_loop(0, nc, step, jnp.zeros((HEAD_DIM, HEAD_DIM), F32))

    col = pl.BlockSpec((s, LANE), lambda h: (0, h))
    return pl.pallas_call(
        body, name="dn_scan_fwd", grid=(DN_V_HEADS,),
        in_specs=[col, col, col, col, pl.BlockSpec((1, s, C), lambda h: (h, 0, 0)),
                  pl.BlockSpec((1, nc, 8, LANE), lambda h: (h, 0, 0, 0))],
        out_specs=[col, pl.BlockSpec((1, nc, HEAD_DIM, HEAD_DIM), lambda h: (h, 0, 0, 0))],
        out_shape=[jax.ShapeDtypeStruct((s, width), F32),
                   jax.ShapeDtypeStruct((DN_V_HEADS, nc, HEAD_DIM, HEAD_DIM), F32)],
        compiler_params=_params("parallel"),
    )(u0, w, qd, kd, qk, cd)


def dn_scan_bwd(do, u0, w, qd, kd, qk, cd, states):
    s = u0.shape[0]
    nc = s // DN_CHUNK
    C = DN_CHUNK

    def body(do_ref, u0_ref, w_ref, qd_ref, kd_ref, qk_ref, cd_ref, st_ref,
             du0_ref, dw_ref, dqd_ref, dkd_ref, dqk_ref, dcd_ref):
        def step(i, dstate):
            c = nc - 1 - i
            rs = pl.ds(pl.multiple_of(c * C, C), C)
            state = st_ref[0, c]
            g = do_ref[rs, :]
            w_c = w_ref[rs, :]
            kd_c = kd_ref[rs, :]
            qd_c = qd_ref[rs, :]
            qk_c = qk_ref[0, rs, :]
            cd_row = cd_ref[0, c][0:1, :]
            u = u0_ref[rs, :] - _dg(w_c, state, _NN)
            du = _dg(qk_c, g, _TN) + _dg(kd_c, dstate, _NN)
            du0_ref[rs, :] = du
            dw_ref[rs, :] = -_dg(du, state, _NT)
            dqd_ref[rs, :] = _dg(g, state, _NT)
            dkd_ref[rs, :] = _dg(u, dstate, _NT)
            dqk_ref[0, rs, :] = _dg(g, u, _NT)
            dcd_row = jnp.sum(state * dstate, axis=0, keepdims=True)
            dcd_ref[0, c] = jnp.concatenate([dcd_row, jnp.zeros((7, LANE), F32)], axis=0)
            return cd_row * dstate + _dg(qd_c, g, _TN) - _dg(w_c, du, _TN)

        lax.fori_loop(0, nc, step, jnp.zeros((HEAD_DIM, HEAD_DIM), F32))

    col = pl.BlockSpec((s, LANE), lambda h: (0, h))
    qk_spec = pl.BlockSpec((1, s, C), lambda h: (h, 0, 0))
    cd_spec = pl.BlockSpec((1, nc, 8, LANE), lambda h: (h, 0, 0, 0))
    wide = jax.ShapeDtypeStruct((s, MIX_WIDTH), F32)
    return pl.pallas_call(
        body, name="dn_scan_bwd", grid=(DN_V_HEADS,),
        in_specs=[col, col, col, col, col, qk_spec, cd_spec,
                  pl.BlockSpec((1, nc, HEAD_DIM, HEAD_DIM), lambda h: (h, 0, 0, 0))],
        out_specs=[col, col, col, col, qk_spec, cd_spec],
        out_shape=[wide, wide, wide, wide, jax.ShapeDtypeStruct((DN_V_HEADS, s, C), F32),
                   jax.ShapeDtypeStruct((DN_V_HEADS, nc, 8, LANE), F32)],
        compiler_params=_params("parallel"),
    )(do, u0, w, qd, kd, qk, cd, states)


SB_T = 256


def _sb_scores(q, kb, i, j):
    z = _dg(q, kb, _NT) * (HEAD_DIM ** -0.5)
    row = lax.broadcasted_iota(jnp.int32, (SB_T, SB_T), 0)
    col = lax.broadcasted_iota(jnp.int32, (SB_T, SB_T), 1)
    mask = (j * SB_T + col) < (i * SB_T + row)
    ls = _log_sigmoid(z)
    return mask, ls, jnp.where(mask, ls - z, 0.0)


def sb_fwd(qkn, proj, v_blk0, width):
    s = qkn.shape[0]

    def body(q_ref, k_ref, v_ref, o_ref, lt_ref):
        i = pl.program_id(1)
        q = q_ref[...]
        row = lax.broadcasted_iota(jnp.int32, (SB_T, SB_T), 0)
        col = lax.broadcasted_iota(jnp.int32, (SB_T, SB_T), 1)
        after = (row > col).astype(BF16)

        def step(jj, carry):
            run, acc = carry
            j = i - jj
            ks = pl.ds(pl.multiple_of(j * SB_T, SB_T), SB_T)
            mask, ls, lr = _sb_scores(q, k_ref[ks, :], i, j)
            later = _dg_exact_rhs(lr, after, _NN) + run
            wts = jnp.where(mask, jnp.exp(ls + later), 0.0)
            return run + jnp.sum(lr, axis=1, keepdims=True), acc + _dg(wts, v_ref[ks, :], _NN)

        run, acc = lax.fori_loop(0, i + 1, step, (jnp.zeros((SB_T, 1), F32), jnp.zeros((SB_T, HEAD_DIM), F32)))
        o_ref[...] = acc
        lt_ref[0] = run

    return pl.pallas_call(
        body, name="sb_fwd", grid=(SB_HEADS, s // SB_T),
        in_specs=[pl.BlockSpec((SB_T, LANE), lambda h, i: (i, h)),
                  pl.BlockSpec((s, LANE), lambda h, i: (0, SB_HEADS + h)),
                  pl.BlockSpec((s, LANE), lambda h, i: (0, v_blk0 + h))],
        out_specs=[pl.BlockSpec((SB_T, LANE), lambda h, i: (i, h)), pl.BlockSpec((1, SB_T, 1), lambda h, i: (h, i, 0))],
        out_shape=[jax.ShapeDtypeStruct((s, width), F32), jax.ShapeDtypeStruct((SB_HEADS, s, 1), F32)],
        compiler_params=_params("parallel", "parallel"),
    )(qkn, qkn, proj)


def sb_bwd(qkn, proj, v_blk0, do, ltot, dproj_in):
    s = qkn.shape[0]

    def body(q_ref, k_ref, v_ref, do_ref, lt_ref, _, dq_ref, dk_ref, dv_ref):
        i = pl.program_id(1)

        @pl.when(i == 0)
        def _():
            dk_ref[...] = jnp.zeros_like(dk_ref)
            dv_ref[...] = jnp.zeros_like(dv_ref)

        q = q_ref[...]
        g = do_ref[...]
        ltot = lt_ref[0]
        row = lax.broadcasted_iota(jnp.int32, (SB_T, SB_T), 0)
        col = lax.broadcasted_iota(jnp.int32, (SB_T, SB_T), 1)
        upto = (row <= col).astype(BF16)
        before = (row < col).astype(BF16)

        def step(j, carry):
            plr, pdl, dq = carry
            ks = pl.ds(pl.multiple_of(j * SB_T, SB_T), SB_T)
            kb = k_ref[ks, :]
            vb = v_ref[ks, :]
            mask, ls, lr = _sb_scores(q, kb, i, j)
            later = ltot - (_dg_exact_rhs(lr, upto, _NN) + plr)
            wts = jnp.where(mask, jnp.exp(ls + later), 0.0)
            dl = _dg(g, vb, _NT) * wts
            pre = _dg_exact_rhs(dl, before, _NN) + pdl
            sz = jnp.exp(ls)
            dz = jnp.where(mask, dl * (1.0 - sz) - sz * pre, 0.0) * (HEAD_DIM ** -0.5)
            dk_ref[ks, :] += _dg(dz, q, _TN)
            dv_ref[ks, :] += _dg(wts, g, _TN)
            return (plr + jnp.sum(lr, axis=1, keepdims=True), pdl + jnp.sum(dl, axis=1, keepdims=True),
                    dq + _dg(dz, kb, _NN))

        zero = jnp.zeros((SB_T, 1), F32)
        _, _, dq = lax.fori_loop(0, i + 1, step, (zero, zero, jnp.zeros((SB_T, HEAD_DIM), F32)))
        dq_ref[...] = dq

    tile = pl.BlockSpec((SB_T, LANE), lambda h, i: (i, h))
    colspec = pl.BlockSpec((s, LANE), lambda h, i: (0, h))
    return pl.pallas_call(
        body, name="sb_bwd", grid=(SB_HEADS, s // SB_T),
        in_specs=[tile, pl.BlockSpec((s, LANE), lambda h, i: (0, SB_HEADS + h)),
                  pl.BlockSpec((s, LANE), lambda h, i: (0, v_blk0 + h)), tile,
                  pl.BlockSpec((1, SB_T, 1), lambda h, i: (h, i, 0)), pl.BlockSpec(memory_space=pl.ANY)],
        out_specs=[tile, colspec, pl.BlockSpec((s, LANE), lambda h, i: (0, v_blk0 + h))],
        out_shape=[jax.ShapeDtypeStruct((s, MIX_WIDTH), F32), jax.ShapeDtypeStruct((s, MIX_WIDTH), F32),
                   jax.ShapeDtypeStruct(dproj_in.shape, F32)],
        input_output_aliases={5: 2}, compiler_params=_params("parallel", "arbitrary"),
    )(qkn, qkn, proj, do, ltot, dproj_in)


ROW_TILE = 256


def head_norm_fwd(x, x_blk0, nblk, gains, out_dtype, out_width, name):
    s = x.shape[0]
    per = nblk // gains.shape[0]

    def body(x_ref, g_ref, o_ref):
        xv = x_ref[...]
        r = lax.rsqrt(jnp.mean(xv * xv, axis=1, keepdims=True) + EPS)
        o_ref[...] = (xv * r * g_ref[0, 0:1, :]).astype(out_dtype)

    return pl.pallas_call(
        body, name=name, grid=(nblk, s // ROW_TILE),
        in_specs=[pl.BlockSpec((ROW_TILE, LANE), lambda j, t: (t, x_blk0 + j)),
                  pl.BlockSpec((1, 8, LANE), lambda j, t: (j // per, 0, 0))],
        out_specs=pl.BlockSpec((ROW_TILE, LANE), lambda j, t: (t, j)),
        out_shape=jax.ShapeDtypeStruct((s, out_width), out_dtype), compiler_params=_params("parallel", "parallel"),
    )(x, gains)


def head_norm_bwd(dy, dy_blk0, x, x_blk0, nblk, gain, dst, dst_blk0, name):
    s = x.shape[0]

    def body(*refs):
        if dst is not None:
            dy_ref, x_ref, g_ref, _, dx_ref, dg_ref = refs
        else:
            dy_ref, x_ref, g_ref, dx_ref, dg_ref = refs

        @pl.when((pl.program_id(0) == 0) & (pl.program_id(1) == 0))
        def _():
            dg_ref[...] = jnp.zeros_like(dg_ref)

        xv = x_ref[...]
        g = dy_ref[...]
        r = lax.rsqrt(jnp.mean(xv * xv, axis=1, keepdims=True) + EPS)
        gy = g * g_ref[0:1, :]
        dx_ref[...] = r * gy - xv * (r * r * r) * jnp.mean(gy * xv, axis=1, keepdims=True)
        dg_ref[0:1, :] += jnp.sum(g * xv * r, axis=0, keepdims=True)

    in_specs = [pl.BlockSpec((ROW_TILE, LANE), lambda j, t: (t, dy_blk0 + j)),
                pl.BlockSpec((ROW_TILE, LANE), lambda j, t: (t, x_blk0 + j)),
                pl.BlockSpec((8, LANE), lambda j, t: (0, 0))]
    args = [dy, x, gain]
    aliases = {}
    if dst is not None:
        in_specs.append(pl.BlockSpec(memory_space=pl.ANY))
        args.append(dst)
        aliases = {3: 0}
        out0 = jax.ShapeDtypeStruct(dst.shape, F32)
    else:
        out0 = jax.ShapeDtypeStruct((s, (dst_blk0 + nblk) * LANE), F32)
    return pl.pallas_call(
        body, name=name, grid=(nblk, s // ROW_TILE), in_specs=in_specs,
        out_specs=[pl.BlockSpec((ROW_TILE, LANE), lambda j, t: (t, dst_blk0 + j)),
                   pl.BlockSpec((8, LANE), lambda j, t: (0, 0))],
        out_shape=[out0, jax.ShapeDtypeStruct((8, LANE), F32)],
        input_output_aliases=aliases, compiler_params=_params("arbitrary", "arbitrary"),
    )(*args)


def _xa_head(xq, kraw, v, qg, kg):
    q = xq * lax.rsqrt(jnp.mean(xq * xq, axis=1, keepdims=True) + EPS) * qg
    k = kraw * lax.rsqrt(jnp.mean(kraw * kraw, axis=1, keepdims=True) + EPS) * kg
    sc = mm_nt(q, k) * (XA_DIM ** -0.5)
    e = jnp.exp(sc - lax.stop_gradient(jnp.max(sc, axis=1, keepdims=True)))
    return mm(e / jnp.sum(e, axis=1, keepdims=True), v)


def xa_fwd(proj, xq_blk0, kv, qg, kg, cat):
    s = proj.shape[0]
    n_mem = kv.shape[0]

    def body(xq_ref, k_ref, v_ref, qg_ref, kg_ref, _, o_ref):
        o_ref[...] = _xa_head(xq_ref[...], k_ref[...], v_ref[...], qg_ref[0:1, :], kg_ref[0:1, :])

    gain = pl.BlockSpec((8, XA_DIM), lambda h, t: (0, 0))
    return pl.pallas_call(
        body, name="xa_fwd", grid=(XA_HEADS, s // ROW_TILE),
        in_specs=[pl.BlockSpec((ROW_TILE, XA_DIM), lambda h, t: (t, xq_blk0 // 2 + h)),
                  pl.BlockSpec((n_mem, XA_DIM), lambda h, t: (0, h)),
                  pl.BlockSpec((n_mem, XA_DIM), lambda h, t: (0, XA_HEADS + h)), gain, gain,
                  pl.BlockSpec(memory_space=pl.ANY)],
        out_specs=pl.BlockSpec((ROW_TILE, XA_DIM), lambda h, t: (t, MIX_WIDTH // XA_DIM + h)),
        out_shape=jax.ShapeDtypeStruct(cat.shape, F32), input_output_aliases={5: 0},
        compiler_params=_params("parallel", "parallel"),
    )(proj, kv, kv, qg, kg, cat)


def xa_bwd(proj, xq_blk0, kv, qg, kg, dcat, dproj_in):
    s = proj.shape[0]
    n_mem = kv.shape[0]

    def body(xq_ref, k_ref, v_ref, qg_ref, kg_ref, do_ref, _, dxq_ref, dk_ref, dv_ref, dqg_ref, dkg_ref):
        h = pl.program_id(0)
        t = pl.program_id(1)

        @pl.when(t == 0)
        def _():
            dk_ref[...] = jnp.zeros_like(dk_ref)
            dv_ref[...] = jnp.zeros_like(dv_ref)

        @pl.when((t == 0) & (h == 0))
        def _():
            dqg_ref[...] = jnp.zeros_like(dqg_ref)
            dkg_ref[...] = jnp.zeros_like(dkg_ref)

        _, vjp = jax.vjp(_xa_head, xq_ref[...], k_ref[...], v_ref[...], qg_ref[0:1, :], kg_ref[0:1, :])
        dxq, dk, dv, dqg, dkg = vjp(do_ref[...])
        dxq_ref[...] = dxq
        dk_ref[...] += dk
        dv_ref[...] += dv
        dqg_ref[0:1, :] += dqg
        dkg_ref[0:1, :] += dkg

    gain = pl.BlockSpec((8, XA_DIM), lambda h, t: (0, 0))
    kspec = pl.BlockSpec((n_mem, XA_DIM), lambda h, t: (0, h))
    vspec = pl.BlockSpec((n_mem, XA_DIM), lambda h, t: (0, XA_HEADS + h))
    return pl.pallas_call(
        body, name="xa_bwd", grid=(XA_HEADS, s // ROW_TILE),
        in_specs=[pl.BlockSpec((ROW_TILE, XA_DIM), lambda h, t: (t, xq_blk0 // 2 + h)), kspec, vspec, gain, gain,
                  pl.BlockSpec((ROW_TILE, XA_DIM), lambda h, t: (t, MIX_WIDTH // XA_DIM + h)),
                  pl.BlockSpec(memory_space=pl.ANY)],
        out_specs=[pl.BlockSpec((ROW_TILE, XA_DIM), lambda h, t: (t, xq_blk0 // 2 + h)), kspec, kspec, gain, gain],
        out_shape=[jax.ShapeDtypeStruct(dproj_in.shape, F32), jax.ShapeDtypeStruct((n_mem, XA_WIDTH), F32),
                   jax.ShapeDtypeStruct((n_mem, XA_WIDTH), F32), jax.ShapeDtypeStruct((8, XA_DIM), F32),
                   jax.ShapeDtypeStruct((8, XA_DIM), F32)],
        input_output_aliases={6: 0}, compiler_params=_params("arbitrary", "arbitrary"),
    )(proj, kv, kv, qg, kg, dcat, dproj_in)


def gate_fwd(cat, proj, z_blk0):
    s = cat.shape[0]

    def body(c_ref, z_ref, y_ref):
        y_ref[...] = (c_ref[...] * _silu(z_ref[...])).astype(BF16)

    w = 2 * LANE
    return pl.pallas_call(
        body, name="gate_fwd", grid=(INNER // w, s // ROW_TILE),
        in_specs=[pl.BlockSpec((ROW_TILE, w), lambda j, t: (t, j)),
                  pl.BlockSpec((ROW_TILE, w), lambda j, t: (t, z_blk0 // 2 + j))],
        out_specs=pl.BlockSpec((ROW_TILE, w), lambda j, t: (t, j)),
        out_shape=jax.ShapeDtypeStruct((s, INNER), BF16), compiler_params=_params("parallel", "parallel"),
    )(cat, proj)


def gate_bwd(dy, cat, proj, z_blk0):
    s = cat.shape[0]

    def body(dy_ref, c_ref, z_ref, dc_ref, dz_ref):
        z = z_ref[...]
        g = dy_ref[...]
        dc_ref[...] = g * _silu(z)
        dz_ref[...] = g * c_ref[...] * _silu_grad(z)

    w = 2 * LANE
    tile = pl.BlockSpec((ROW_TILE, w), lambda j, t: (t, j))
    ztile = pl.BlockSpec((ROW_TILE, w), lambda j, t: (t, z_blk0 // 2 + j))
    return pl.pallas_call(
        body, name="gate_bwd", grid=(INNER // w, s // ROW_TILE), in_specs=[tile, tile, ztile],
        out_specs=[tile, ztile],
        out_shape=[jax.ShapeDtypeStruct((s, INNER), F32), jax.ShapeDtypeStruct(proj.shape, F32)],
        compiler_params=_params("parallel", "parallel"),
    )(dy, cat, proj)


NORM_ROWS = 256


def rms_fwd(x, gain):
    s, d = x.shape

    def body(x_ref, g_ref, o_ref):
        xv = x_ref[...]
        r = lax.rsqrt(jnp.mean(xv * xv, axis=1, keepdims=True) + EPS)
        o_ref[...] = (xv * r * g_ref[0:1, :]).astype(BF16)

    return pl.pallas_call(
        body, name="rms_fwd", grid=(s // NORM_ROWS,),
        in_specs=[pl.BlockSpec((NORM_ROWS, d), lambda t: (t, 0)), pl.BlockSpec((8, d), lambda t: (0, 0))],
        out_specs=pl.BlockSpec((NORM_ROWS, d), lambda t: (t, 0)),
        out_shape=jax.ShapeDtypeStruct((s, d), BF16), compiler_params=_params("parallel"),
    )(x, gain)


def rms_bwd(dh, x, gain, dres):
    s, d = x.shape

    def body(*refs):
        if dres is not None:
            dh_ref, x_ref, g_ref, dr_ref, dx_ref, dg_ref = refs
        else:
            dh_ref, x_ref, g_ref, dx_ref, dg_ref = refs

        @pl.when(pl.program_id(0) == 0)
        def _():
            dg_ref[...] = jnp.zeros_like(dg_ref)

        xv = x_ref[...]
        g = dh_ref[...]
        r = lax.rsqrt(jnp.mean(xv * xv, axis=1, keepdims=True) + EPS)
        gy = g * g_ref[0:1, :]
        dx = r * gy - xv * (r * r * r) * jnp.mean(gy * xv, axis=1, keepdims=True)
        dx_ref[...] = dx + dr_ref[...] if dres is not None else dx
        dg_ref[0:1, :] += jnp.sum(g * xv * r, axis=0, keepdims=True)

    tile = pl.BlockSpec((NORM_ROWS, d), lambda t: (t, 0))
    gspec = pl.BlockSpec((8, d), lambda t: (0, 0))
    args = [dh, x, gain] + ([dres] if dres is not None else [])
    return pl.pallas_call(
        body, name="rms_bwd", grid=(s // NORM_ROWS,),
        in_specs=[tile, tile, gspec] + ([tile] if dres is not None else []),
        out_specs=[tile, gspec],
        out_shape=[jax.ShapeDtypeStruct((s, d), F32), jax.ShapeDtypeStruct((8, d), F32)],
        compiler_params=_params("arbitrary"),
    )(*args)


def loss_fwd_bwd(y, target):
    s, d = y.shape

    def body(y_ref, t_ref, l_ref, dy_ref):
        @pl.when(pl.program_id(0) == 0)
        def _():
            l_ref[...] = jnp.zeros_like(l_ref)

        err = y_ref[...] - t_ref[...]
        dy_ref[...] = err * (1.0 / d)
        part = 0.5 * jnp.sum(jnp.mean(err * err, axis=1, keepdims=True), axis=0, keepdims=True)
        r = lax.broadcasted_iota(jnp.int32, (8, LANE), 0)
        c = lax.broadcasted_iota(jnp.int32, (8, LANE), 1)
        l_ref[...] += jnp.where((r == 0) & (c == 0), part, 0.0)

    tile = pl.BlockSpec((NORM_ROWS, d), lambda t: (t, 0))
    return pl.pallas_call(
        body, name="loss", grid=(s // NORM_ROWS,), in_specs=[tile, tile],
        out_specs=[pl.BlockSpec((8, LANE), lambda t: (0, 0)), tile],
        out_shape=[jax.ShapeDtypeStruct((8, LANE), F32), jax.ShapeDtypeStruct((s, d), F32)],
        compiler_params=_params("arbitrary"),
    )(y, target)


def matmul(a, b, mode, out_dtype, tm, tn, tk, name, add=None, b_blocked=False, out_blocks=None):
    if b_blocked:
        nb, _, width = b.shape
        bshape = (b.shape[1], nb * width)
    else:
        bshape = b.shape
    if mode == "tn":
        (kdim, m), n = a.shape, bshape[1]
    else:
        (m, kdim), n = a.shape, (bshape[1] if mode == "nn" else bshape[0])
    tm, tn, tk = min(tm, m), min(tn, n), min(tk, kdim)
    assert m % tm == 0 and n % tn == 0 and kdim % tk == 0, (name, m, n, kdim)
    nk = kdim // tk
    dims = {"nn": _NN, "nt": _NT, "tn": _TN}[mode]

    def body(*refs):
        if add is not None:
            a_ref, b_ref, add_ref, o_ref, acc_ref = refs
        else:
            a_ref, b_ref, o_ref, acc_ref = refs
        k = pl.program_id(2)

        @pl.when(k == 0)
        def _():
            acc_ref[...] = jnp.zeros_like(acc_ref)

        acc_ref[...] += _dg(a_ref[...], b_ref[...], dims)

        @pl.when(k == nk - 1)
        def _():
            r = acc_ref[...]
            if add is not None:
                r = r + add_ref[...]
            o_ref[...] = r.astype(out_dtype)

    a_spec = pl.BlockSpec((tk, tm), lambda i, j, k: (k, i)) if mode == "tn" else pl.BlockSpec((tm, tk), lambda i, j, k: (i, k))
    if b_blocked and mode == "nn":
        per = width // tn
        assert width % tn == 0
        b_spec = pl.BlockSpec((None, tk, tn), lambda i, j, k: (j // per, k, j % per))
    elif b_blocked and mode == "nt":
        per = width // tk
        assert width % tk == 0
        b_spec = pl.BlockSpec((None, tn, tk), lambda i, j, k: (k // per, j, k % per))
    elif mode == "nt":
        b_spec = pl.BlockSpec((tn, tk), lambda i, j, k: (j, k))
    else:
        assert not b_blocked
        b_spec = pl.BlockSpec((tk, tn), lambda i, j, k: (k, j))
    add_spec = pl.BlockSpec((tm, tn), lambda i, j, k: (i, j))
    if out_blocks is not None:
        operb = (n // out_blocks) // tn
        assert (n // out_blocks) % tn == 0 and add is None
        o_spec = pl.BlockSpec((None, tm, tn), lambda i, j, k: (j // operb, i, j % operb))
        out_shape = jax.ShapeDtypeStruct((out_blocks, m, n // out_blocks), out_dtype)
    else:
        o_spec = add_spec
        out_shape = jax.ShapeDtypeStruct((m, n), out_dtype)
    return pl.pallas_call(
        body, name=name, grid=(m // tm, n // tn, nk),
        in_specs=[a_spec, b_spec] + ([add_spec] if add is not None else []), out_specs=o_spec,
        out_shape=out_shape, scratch_shapes=[pltpu.VMEM((tm, tn), F32)],
        compiler_params=_params("parallel", "parallel", "arbitrary"),
    )(*([a, b] + ([add] if add is not None else [])))


_HBM = pl.BlockSpec(memory_space=pltpu.HBM)


def _me():
    return lax.axis_index("x"), lax.axis_index("y"), lax.axis_index("c")


def _flat(p):
    return 4 * p[0] + 2 * p[1] + p[2]


def _flip(p, r):
    return tuple((1 - v) if (r >> (2 - a)) & 1 else v for a, v in enumerate(p))


def all_gather(shards):
    n = len(shards)

    def body(*refs):
        srcs, outs = refs[:n], refs[n:2 * n]
        send_sems, recv_sems, local_sems = refs[2 * n:]
        me = _me()
        x, y, c = me
        sibling = (x, y, 1 - c)
        chips = [(1 - x, y), (x, 1 - y), (1 - x, 1 - y)]

        def copy(a, k, block, to, src=None):
            dst = outs[a].at[_flat(block)]
            return pltpu.make_async_remote_copy(src_ref=dst if src is None else src, dst_ref=dst,
                                                send_sem=send_sems.at[a, k], recv_sem=recv_sems.at[a, k],
                                                device_id=to, device_id_type=MESH)

        mine = [pltpu.make_async_copy(srcs[a], outs[a].at[_flat(me)], local_sems.at[a]) for a in range(n)]
        for cp in mine:
            cp.start()
        sent = []
        for a in range(n):
            sent.append(copy(a, 0, me, sibling, src=srcs[a]))
            sent += [copy(a, 1 + j, me, (*chip, c), src=srcs[a]) for j, chip in enumerate(chips)]
        for cp in sent:
            cp.start()
        for j, chip in enumerate(chips):
            for a in range(n):
                copy(a, 1 + j, (*chip, c), me).wait_recv()
                fwd = copy(a, 4 + j, (*chip, c), sibling)
                fwd.start()
                sent.append(fwd)
        for a in range(n):
            copy(a, 0, sibling, me).wait_recv()
            for j, chip in enumerate(chips):
                copy(a, 4 + j, (*chip, 1 - c), me).wait_recv()
        for cp in sent:
            cp.wait_send()
        for cp in mine:
            cp.wait()

    return pl.pallas_call(
        body, name="all_gather", in_specs=[_HBM] * n, out_specs=[_HBM] * n,
        out_shape=[jax.ShapeDtypeStruct((N_DEV,) + s.shape, s.dtype) for s in shards],
        scratch_shapes=[pltpu.SemaphoreType.DMA((n, 7)), pltpu.SemaphoreType.DMA((n, 7)), pltpu.SemaphoreType.DMA((n,))],
    )(*shards)


def reduce_scatter_send(srcs, slots, recv_shapes):
    n = len(srcs)
    nr = len(recv_shapes)

    def body(*refs):
        src_refs, recv_refs = refs[:n], refs[n:n + nr]
        send_sems, recv_sems, local_sems = refs[n + nr:]
        me = _me()

        def copy(a, r):
            peer = _flip(me, r)
            ri, layer = slots[a]
            return pltpu.make_async_remote_copy(src_ref=src_refs[a].at[_flat(peer)],
                                                dst_ref=recv_refs[ri].at[_flat(me), layer],
                                                send_sem=send_sems.at[a, r - 1], recv_sem=recv_sems.at[a, r - 1],
                                                device_id=peer, device_id_type=MESH)

        def arrival(a, r):
            peer = _flip(me, r)
            ri, layer = slots[a]
            land = recv_refs[ri].at[_flat(peer), layer]
            return pltpu.make_async_remote_copy(src_ref=land, dst_ref=land, send_sem=send_sems.at[a, r - 1],
                                                recv_sem=recv_sems.at[a, r - 1], device_id=peer, device_id_type=MESH)

        mine = [pltpu.make_async_copy(src_refs[a].at[_flat(me)], recv_refs[slots[a][0]].at[_flat(me), slots[a][1]],
                                      local_sems.at[a]) for a in range(n)]
        for cp in mine:
            cp.start()
        sent = [copy(a, r) for r in range(1, N_DEV) for a in range(n)]
        for cp in sent:
            cp.start()
        for r in range(1, N_DEV):
            for a in range(n):
                arrival(a, r).wait_recv()
        for cp in sent:
            cp.wait_send()
        for cp in mine:
            cp.wait()

    return pl.pallas_call(
        body, name="reduce_scatter_send", in_specs=[_HBM] * n, out_specs=[_HBM] * nr,
        out_shape=[jax.ShapeDtypeStruct(s, BF16) for s in recv_shapes],
        scratch_shapes=[pltpu.SemaphoreType.DMA((n, 7)), pltpu.SemaphoreType.DMA((n, 7)), pltpu.SemaphoreType.DMA((n,))],
    )(*srcs)


SMALL_ROWS = 24


def all_reduce_small(pack):
    def body(p_ref, o_ref, buf, send_sems, recv_sems):
        me = _me()
        buf[_flat(me)] = p_ref[...]
        sent = []
        for r in range(1, N_DEV):
            peer = _flip(me, r)
            cp = pltpu.make_async_remote_copy(src_ref=p_ref, dst_ref=buf.at[_flat(me)], send_sem=send_sems.at[r - 1],
                                              recv_sem=recv_sems.at[r - 1], device_id=peer, device_id_type=MESH)
            cp.start()
            sent.append(cp)
        for r in range(1, N_DEV):
            peer = _flip(me, r)
            land = buf.at[_flat(peer)]
            pltpu.make_async_remote_copy(src_ref=land, dst_ref=land, send_sem=send_sems.at[r - 1],
                                         recv_sem=recv_sems.at[r - 1], device_id=peer, device_id_type=MESH).wait_recv()
        for cp in sent:
            cp.wait_send()
        acc = buf[0]
        for d in range(1, N_DEV):
            acc = acc + buf[d]
        o_ref[...] = acc

    vm = pl.BlockSpec(memory_space=pltpu.VMEM)
    return pl.pallas_call(
        body, name="all_reduce_small", in_specs=[vm], out_specs=vm,
        out_shape=jax.ShapeDtypeStruct(pack.shape, F32),
        scratch_shapes=[pltpu.VMEM((N_DEV,) + pack.shape, F32), pltpu.SemaphoreType.DMA((7,)),
                        pltpu.SemaphoreType.DMA((7,))],
    )(pack)


def _adamw(w, g, m, v):
    m = ADAM_B1 * m + (1.0 - ADAM_B1) * g
    v = ADAM_B2 * v + (1.0 - ADAM_B2) * (g * g)
    m_hat = m / (1.0 - ADAM_B1 ** ADAM_STEP)
    v_hat = v / (1.0 - ADAM_B2 ** ADAM_STEP)
    delta = -ADAM_LR * (m_hat / (jnp.sqrt(v_hat) + ADAM_EPS) + ADAM_WD * w)
    return delta, m, v


ADAM_ROWS = 128


def reduce_adamw(recv, w, m, v, name):
    nl, rows, cols = w.shape
    cp = recv.shape[3]

    def body(r_ref, w_ref, m_ref, v_ref, g_ref, d_ref, mo_ref, vo_ref):
        g = r_ref[0, 0].astype(F32)
        for dev in range(1, N_DEV):
            g = g + r_ref[dev, 0].astype(F32)
        if cp != cols:
            g = g[:, :cols]
        delta, m_new, v_new = _adamw(w_ref[0], g, m_ref[0], v_ref[0])
        g_ref[0] = g
        d_ref[0] = delta
        mo_ref[0] = m_new
        vo_ref[0] = v_new

    tile = pl.BlockSpec((1, ADAM_ROWS, cols), lambda l, t: (l, t, 0))
    out = jax.ShapeDtypeStruct(w.shape, F32)
    return pl.pallas_call(
        body, name=name, grid=(nl, rows // ADAM_ROWS),
        in_specs=[pl.BlockSpec((N_DEV, 1, ADAM_ROWS, cp), lambda l, t: (0, l, t, 0)), tile, tile, tile],
        out_specs=[tile, tile, tile, tile], out_shape=[out, out, out, out],
        compiler_params=_params("parallel", "parallel"),
    )(recv, w, m, v)


def adamw_small(g, w, m, v):
    def body(g_ref, w_ref, m_ref, v_ref, d_ref, mo_ref, vo_ref):
        d_ref[...], mo_ref[...], vo_ref[...] = _adamw(w_ref[...], g_ref[...], m_ref[...], v_ref[...])

    out = jax.ShapeDtypeStruct(g.shape, F32)
    return pl.pallas_call(body, name="adamw_small", out_shape=[out, out, out])(g, w, m, v)


def _row8(v):
    return jnp.pad(v.reshape(1, -1).astype(F32), ((0, 7), (0, 0)))


def _row8_lanes(v, width=LANE):
    return jnp.pad(v.reshape(1, -1).astype(F32), ((0, 7), (0, width - v.size)))


def _pack_rows(parts):
    rows = []
    for p in parts:
        p = p.reshape(-1).astype(F32)
        nrow = -(-p.size // D_MODEL)
        rows.append(jnp.pad(p, (0, nrow * D_MODEL - p.size)).reshape(nrow, D_MODEL))
    out = jnp.concatenate(rows, axis=0)
    return jnp.pad(out, ((0, SMALL_ROWS - out.shape[0]), (0, 0)))


def _unpack_rows(pack, shapes):
    out, r = [], 0
    for shp in shapes:
        size = 1
        for d in shp:
            size *= d
        nrow = -(-size // D_MODEL)
        out.append(pack[r:r + nrow].reshape(-1)[:size].reshape(shp))
        r += nrow
    return out


def _dn_weight_layout(gathered):
    full = jnp.transpose(gathered[:, :, :DN_SHARD], (1, 0, 2)).reshape(D_MODEL, DN_PROJ)
    n_ab = 2 * DN_V_HEADS
    zeros = jnp.zeros((D_MODEL, 2 * LANE - n_ab), full.dtype)
    return jnp.concatenate([full[:, :DN_QKV + n_ab], zeros, full[:, DN_QKV + n_ab:]], axis=1)


def _dn_grad_blocks(dw):
    n_ab = 2 * DN_V_HEADS
    full = jnp.concatenate([dw[:, :DN_QKV + n_ab], dw[:, DN_QKV + 2 * LANE:]], axis=1)
    blocks = jnp.transpose(full.reshape(D_MODEL, N_DEV, DN_SHARD), (1, 0, 2))
    return jnp.pad(blocks, ((0, 0), (0, 0), (0, DN_SHARD_PAD - DN_SHARD)))


def kernel(x, mem, norm_g, mem_norm_g, mem_w_kv, xa_q_norm_g, xa_k_norm_g, w_out, dn_w_in, dn_conv_w, dn_a_log, dn_dt_bias, dn_out_norm_g, sb_w_in, sb_q_norm_g, sb_k_norm_g, loss_target, m_norm_g, m_mem_norm_g, m_mem_w_kv, m_xa_q_norm_g, m_xa_k_norm_g, m_w_out, m_dn_w_in, m_dn_conv_w, m_dn_a_log, m_dn_dt_bias, m_dn_out_norm_g, m_sb_w_in, m_sb_q_norm_g, m_sb_k_norm_g, v_norm_g, v_mem_norm_g, v_mem_w_kv, v_xa_q_norm_g, v_xa_k_norm_g, v_w_out, v_dn_w_in, v_dn_conv_w, v_dn_a_log, v_dn_dt_bias, v_dn_out_norm_g, v_sb_w_in, v_sb_q_norm_g, v_sb_k_norm_g):
    x0, memv, target = x[0], mem[0], loss_target[0]
    my_dev = 4 * lax.axis_index("x") + 2 * lax.axis_index("y") + lax.axis_index("c")

    dn_shard = jnp.pad(dn_w_in[0].astype(BF16), ((0, 0), (0, DN_SHARD_PAD - DN_SHARD)))
    w_out_b = w_out.astype(BF16)
    w_kv_b = mem_w_kv.astype(BF16)
    conv_shard = jnp.pad(dn_conv_w[0], ((0, 4), (0, 0)))
    g_dn, g_sb, g_wo0, g_wo1, g_kv0, g_kv1, g_conv = all_gather(
        [dn_shard, sb_w_in[0].astype(BF16), w_out_b[0], w_out_b[1], w_kv_b[0], w_kv_b[1], conv_shard])
    w_dn = _dn_weight_layout(g_dn)
    w_sb = g_sb
    w_o = [g_wo0.reshape(INNER, D_MODEL), g_wo1.reshape(INNER, D_MODEL)]
    w_kv = [g_kv0.reshape(D_MODEL, 2 * XA_WIDTH), g_kv1.reshape(D_MODEL, 2 * XA_WIDTH)]
    conv_w = jnp.transpose(g_conv, (1, 0, 2)).reshape(8, DN_QKV)

    ng = [_row8(norm_g[0]), _row8(norm_g[1])]
    mem_g = _row8(mem_norm_g)
    xqg = [_row8(xa_q_norm_g[0]), _row8(xa_q_norm_g[1])]
    xkg = [_row8(xa_k_norm_g[0]), _row8(xa_k_norm_g[1])]
    alog, dtb = _row8_lanes(dn_a_log[0]), _row8_lanes(dn_dt_bias[0])
    out_g, sbq_g, sbk_g = _row8(dn_out_norm_g[0]), _row8(sb_q_norm_g[0]), _row8(sb_k_norm_g[0])

    mem_n = rms_fwd(memv, mem_g)
    kv = [matmul(mem_n, w_kv[i], "nn", F32, 256, 1024, 2048, f"kv{i}") for i in range(2)]

    h0 = rms_fwd(x0, ng[0])
    proj0 = matmul(h0, w_dn, "nn", F32, 1024, 1152, 2048, "proj_dn")
    act = dn_conv_fwd(proj0, conv_w)
    u0, w_, qd, kd, qk, cd = dn_prep_fwd(act, proj0, alog, dtb)
    o_raw, states = dn_scan_fwd(u0, w_, qd, kd, qk, cd, MIX_WIDTH)
    cat0 = head_norm_fwd(o_raw, 0, DN_V_HEADS, out_g[None], F32, INNER, "dn_out_norm")
    cat0 = xa_fwd(proj0, DN_XQ_BLK, kv[0], xqg[0], xkg[0], cat0)
    y0 = gate_fwd(cat0, proj0, DN_Z_BLK)
    x1 = matmul(y0, w_o[0], "nn", F32, 1024, 1024, 2048, "out_proj0", add=x0)

    h1 = rms_fwd(x1, ng[1])
    proj1 = matmul(h1, w_sb, "nn", F32, 1024, 896, 2048, "proj_sb", b_blocked=True)
    qkn = head_norm_fwd(proj1, 0, 2 * SB_HEADS, jnp.stack([sbq_g, sbk_g]), BF16, 2 * MIX_WIDTH, "sb_qk_norm")
    cat1, ltot = sb_fwd(qkn, proj1, 2 * SB_HEADS, INNER)
    cat1 = xa_fwd(proj1, SB_XQ_BLK, kv[1], xqg[1], xkg[1], cat1)
    y1 = gate_fwd(cat1, proj1, SB_Z_BLK)
    x2 = matmul(y1, w_o[1], "nn", F32, 1024, 1024, 2048, "out_proj1", add=x1)
    loss_part, dx2 = loss_fwd_bwd(x2, target)

    dy1 = matmul(dx2, w_o[1], "nt", F32, 1024, 1024, 2048, "d_y1")
    dw_o1 = matmul(y1, dx2, "tn", BF16, 1024, 1024, 1024, "d_w_out1")
    dcat1, dproj1 = gate_bwd(dy1, cat1, proj1, SB_Z_BLK)
    dproj1, dxk1, dxv1, dxqg1, dxkg1 = xa_bwd(proj1, SB_XQ_BLK, kv[1], xqg[1], xkg[1], dcat1, dproj1)
    dqn, dkn, dproj1 = sb_bwd(qkn, proj1, 2 * SB_HEADS, dcat1, ltot, dproj1)
    dproj1, d_sbq = head_norm_bwd(dqn, 0, proj1, 0, SB_HEADS, sbq_g, dproj1, 0, "sb_q_norm_bwd")
    dproj1, d_sbk = head_norm_bwd(dkn, 0, proj1, SB_HEADS, SB_HEADS, sbk_g, dproj1, SB_HEADS, "sb_k_norm_bwd")
    dw_sb = matmul(h1, dproj1, "tn", BF16, 1024, 896, 1024, "d_w_sb", out_blocks=N_DEV)
    dh1 = matmul(dproj1, w_sb, "nt", F32, 1024, 1024, 896, "d_h1", b_blocked=True)
    dx1, d_ng1 = rms_bwd(dh1, x1, ng[1], dx2)

    dy0 = matmul(dx1, w_o[0], "nt", F32, 1024, 1024, 2048, "d_y0")
    dw_o0 = matmul(y0, dx1, "tn", BF16, 1024, 1024, 1024, "d_w_out0")
    dcat0, dproj0 = gate_bwd(dy0, cat0, proj0, DN_Z_BLK)
    dproj0, dxk0, dxv0, dxqg0, dxkg0 = xa_bwd(proj0, DN_XQ_BLK, kv[0], xqg[0], xkg[0], dcat0, dproj0)
    do_raw, d_outg = head_norm_bwd(dcat0, 0, o_raw, 0, DN_V_HEADS, out_g, None, 0, "dn_out_norm_bwd")
    du0, dw_, dqd, dkd, dqk, dcd = dn_scan_bwd(do_raw, u0, w_, qd, kd, qk, cd, states)
    dq_a, dk_a, dv_a, dab, d_alog, d_dtb = dn_prep_bwd(act, proj0, alog, dtb, du0, dw_, dqd, dkd, dqk, dcd)
    dproj0, dcw_q = dn_conv_bwd(dq_a, proj0, conv_w, dproj0, None, 0, "dn_conv_bwd_q")
    dproj0, dcw_k = dn_conv_bwd(dk_a, proj0, conv_w, dproj0, None, DN_QK_HEADS, "dn_conv_bwd_k")
    dproj0, dcw_v = dn_conv_bwd(dv_a, proj0, conv_w, dproj0, dab, 2 * DN_QK_HEADS, "dn_conv_bwd_v")
    dw_dn = matmul(h0, dproj0, "tn", BF16, 1024, 1152, 1024, "d_w_dn")
    dh0 = matmul(dproj0, w_dn, "nt", F32, 1024, 1024, 1152, "d_h0")
    grad_x, d_ng0 = rms_bwd(dh0, x0, ng[0], dx1)

    dkv = [jnp.concatenate([dxk0, dxv0], axis=1), jnp.concatenate([dxk1, dxv1], axis=1)]
    dw_kv = [matmul(mem_n, dkv[i], "tn", BF16, 1024, 1024, 256, f"d_w_kv{i}") for i in range(2)]
    dmem_n = matmul(dkv[0], w_kv[0], "nt", F32, 256, 1024, 2048, "d_mem_n0")
    dmem_n = matmul(dkv[1], w_kv[1], "nt", F32, 256, 1024, 2048, "d_mem_n1", add=dmem_n)
    _, d_memg = rms_bwd(dmem_n, memv, mem_g, None)

    recv_dn, recv_sb, recv_wo, recv_kv = reduce_scatter_send(
        [_dn_grad_blocks(dw_dn), dw_sb, dw_o0.reshape(N_DEV, INNER // N_DEV, D_MODEL),
         dw_o1.reshape(N_DEV, INNER // N_DEV, D_MODEL), dw_kv[0].reshape(N_DEV, D_MODEL // N_DEV, 2 * XA_WIDTH),
         dw_kv[1].reshape(N_DEV, D_MODEL // N_DEV, 2 * XA_WIDTH)],
        [(0, 0), (1, 0), (2, 0), (2, 1), (3, 0), (3, 1)],
        [(N_DEV, 1, D_MODEL, DN_SHARD_PAD), (N_DEV, 1, D_MODEL, SB_PROJ // N_DEV),
         (N_DEV, 2, INNER // N_DEV, D_MODEL), (N_DEV, 2, D_MODEL // N_DEV, 2 * XA_WIDTH)])
    big = {
        "dn_w_in": reduce_adamw(recv_dn, dn_w_in, m_dn_w_in, v_dn_w_in, "adamw_dn_w_in"),
        "sb_w_in": reduce_adamw(recv_sb, sb_w_in, m_sb_w_in, v_sb_w_in, "adamw_sb_w_in"),
        "w_out": reduce_adamw(recv_wo, w_out, m_w_out, v_w_out, "adamw_w_out"),
        "mem_w_kv": reduce_adamw(recv_kv, mem_w_kv, m_mem_w_kv, v_mem_w_kv, "adamw_mem_w_kv"),
    }

    dconv = jnp.concatenate([dcw_q, dcw_k, dcw_v], axis=1)[:4]
    small_shapes = [(2, D_MODEL), (D_MODEL,), (2, XA_DIM), (2, XA_DIM), (4, DN_QKV), (1, DN_V_HEADS),
                    (1, DN_V_HEADS), (1, HEAD_DIM), (1, HEAD_DIM), (1, HEAD_DIM), (1,)]
    pack = _pack_rows([jnp.stack([d_ng0[0], d_ng1[0]]), d_memg[0], jnp.stack([dxqg0[0], dxqg1[0]]),
                       jnp.stack([dxkg0[0], dxkg1[0]]), dconv, d_alog[0, :DN_V_HEADS], d_dtb[0, :DN_V_HEADS],
                       d_outg[0], d_sbq[0], d_sbk[0], loss_part[0, :1]])
    total = all_reduce_small(pack)
    (g_norm, g_memn, g_xq, g_xk, g_conv_full, g_alog, g_dtb, g_outn, g_sbq, g_sbk, loss1) = _unpack_rows(total, small_shapes)
    conv_cols = DN_QKV // N_DEV
    g_conv = lax.dynamic_slice(g_conv_full, (0, my_dev * conv_cols), (4, conv_cols))[None]
    names = ["norm_g", "mem_norm_g", "xa_q_norm_g", "xa_k_norm_g", "dn_conv_w", "dn_a_log", "dn_dt_bias",
             "dn_out_norm_g", "sb_q_norm_g", "sb_k_norm_g"]
    grads = [g_norm, g_memn, g_xq, g_xk, g_conv, g_alog, g_dtb, g_outn, g_sbq, g_sbk]
    ws = [norm_g, mem_norm_g, xa_q_norm_g, xa_k_norm_g, dn_conv_w, dn_a_log, dn_dt_bias, dn_out_norm_g, sb_q_norm_g,
          sb_k_norm_g]
    ms = [m_norm_g, m_mem_norm_g, m_xa_q_norm_g, m_xa_k_norm_g, m_dn_conv_w, m_dn_a_log, m_dn_dt_bias,
          m_dn_out_norm_g, m_sb_q_norm_g, m_sb_k_norm_g]
    vs = [v_norm_g, v_mem_norm_g, v_xa_q_norm_g, v_xa_k_norm_g, v_dn_conv_w, v_dn_a_log, v_dn_dt_bias,
          v_dn_out_norm_g, v_sb_q_norm_g, v_sb_k_norm_g]
    shapes = [w.shape for w in ws]
    d_p, m_p, v_p = adamw_small(_pack_rows(grads), _pack_rows(ws), _pack_rows(ms), _pack_rows(vs))
    small = dict(zip(names, zip(grads, _unpack_rows(d_p, shapes), _unpack_rows(m_p, shapes), _unpack_rows(v_p, shapes))))

    order = ["norm_g", "mem_norm_g", "mem_w_kv", "xa_q_norm_g", "xa_k_norm_g", "w_out", "dn_w_in", "dn_conv_w",
             "dn_a_log", "dn_dt_bias", "dn_out_norm_g", "sb_w_in", "sb_q_norm_g", "sb_k_norm_g"]
    res = {**big, **small}
    outs = [loss1.reshape(()), grad_x[None]]
    for k in range(4):
        outs += [res[n][k] for n in order]
    return tuple(outs)
```

```python
import functools

import jax
import jax.numpy as jnp
from jax import lax
from jax.experimental import pallas as pl
from jax.experimental.pallas import tpu as pltpu

F32 = jnp.float32
BF16 = jnp.bfloat16

D_MODEL = 2048
SEQ = 2048
N_MEM = 256
INNER = 4096
XA_HEADS = 4
XA_WIDTH = 1024
XA_DIM = 256
MIX_WIDTH = 3072
HEAD_DIM = 128
DN_V_HEADS = 24
DN_QK_HEADS = 12
DN_QK_WIDTH = 1536
DN_CHUNK = 64
DN_QKV = 2 * DN_QK_WIDTH + MIX_WIDTH
DN_PROJ = 11312
SB_HEADS = 24
SB_PROJ = 14336
EPS = 1e-6
N_DEV = 8
DN_SHARD = DN_PROJ // N_DEV
DN_SHARD_PAD = 1536
LANE = 128
DN_COLS = 90 * LANE
DN_AB_BLK, DN_PAD_BLK, DN_XQ_BLK, DN_Z_BLK = 48, 49, 50, 58
SB_XQ_BLK, SB_Z_BLK = 72, 80

ADAM_LR, ADAM_B1, ADAM_B2, ADAM_EPS, ADAM_WD, ADAM_STEP = 0.001, 0.9, 0.999, 1e-08, 0.01, 10

VMEM_LIMIT = 56 * 1024 * 1024
MESH = pl.DeviceIdType.MESH

_NN, _NT, _TN = "nn", "nt", "tn"


def _dims(mode, rank):
    lhs, rhs = {"nn": (1, 0), "nt": (1, 1), "tn": (0, 0)}[mode]
    if rank == 2:
        return (((lhs,), (rhs,)), ((), ()))
    return (((lhs + 1,), (rhs + 1,)), ((0,), (0,)))


def _params(*sem):
    return pltpu.CompilerParams(dimension_semantics=sem if sem else None, vmem_limit_bytes=VMEM_LIMIT)


def _dot(a, b, mode):
    return lax.dot_general(a, b, _dims(mode, a.ndim), preferred_element_type=F32)


def _dg(a, b, dims):
    return _dot(a.astype(BF16), b.astype(BF16), dims)


@jax.custom_vjp
def mm(a, b):
    return _dg(a, b, _NN)


@jax.custom_vjp
def mm_nt(a, b):
    return _dg(a, b, _NT)


@jax.custom_vjp
def mm_tn(a, b):
    return _dg(a, b, _TN)


mm.defvjp(lambda a, b: (_dg(a, b, _NN), (a, b)), lambda r, g: (mm_nt(g, r[1]), mm_tn(r[0], g)))
mm_nt.defvjp(lambda a, b: (_dg(a, b, _NT), (a, b)), lambda r, g: (mm(g, r[1]), mm_tn(g, r[0])))
mm_tn.defvjp(lambda a, b: (_dg(a, b, _TN), (a, b)), lambda r, g: (mm_nt(r[1], g), mm(r[0], g)))


def _split3(x):
    hi = x.astype(BF16)
    r1 = x - hi.astype(F32)
    mid = r1.astype(BF16)
    lo = (r1 - mid.astype(F32)).astype(BF16)
    return hi, mid, lo


def _dg3(a, b, dims):
    ah, am, _ = _split3(a)
    bh, bm, _ = _split3(b)
    return _dot(ah, bh, dims) + (_dot(ah, bm, dims) + _dot(am, bh, dims))


def _dg_exact_rhs(a, b01, dims):
    ah, am, _ = _split3(a)
    b = b01.astype(BF16)
    return _dot(ah, b, dims) + _dot(am, b, dims)


def _dg_exact_lhs(a01, b, dims):
    bh, bm, bl = _split3(b)
    a = a01.astype(BF16)
    return _dot(a, bh, dims) + (_dot(a, bm, dims) + _dot(a, bl, dims))


@jax.custom_vjp
def mm3(a, b):
    return _dg3(a, b, _NN)


@jax.custom_vjp
def mm3_nt(a, b):
    return _dg3(a, b, _NT)


@jax.custom_vjp
def mm3_tn(a, b):
    return _dg3(a, b, _TN)


mm3.defvjp(lambda a, b: (_dg3(a, b, _NN), (a, b)), lambda r, g: (mm3_nt(g, r[1]), mm3_tn(r[0], g)))
mm3_nt.defvjp(lambda a, b: (_dg3(a, b, _NT), (a, b)), lambda r, g: (mm3(g, r[1]), mm3_tn(g, r[0])))
mm3_tn.defvjp(lambda a, b: (_dg3(a, b, _TN), (a, b)), lambda r, g: (mm3_nt(r[1], g), mm3(r[0], g)))


def _softplus(x):
    return jnp.maximum(x, 0.0) + jnp.log(1.0 + jnp.exp(-jnp.abs(x)))


def _log_sigmoid(x):
    return jnp.minimum(x, 0.0) - jnp.log(1.0 + jnp.exp(-jnp.abs(x)))


def _sigmoid(x):
    return 1.0 / (1.0 + jnp.exp(-x))


def _silu(x):
    return x * _sigmoid(x)


def _silu_grad(x):
    s = _sigmoid(x)
    return s * (1.0 + x * (1.0 - s))


@jax.custom_vjp
def mm01(a01, b):
    return _dg_exact_lhs(a01, b, _NN)


mm01.defvjp(lambda a, b: (_dg_exact_lhs(a, b, _NN), a),
            lambda a, g: (jnp.zeros_like(a), _dg_exact_lhs(a, g, _TN)))


def _lane_pick(x, idx):
    lane = lax.broadcasted_iota(jnp.int32, x.shape, x.ndim - 1)
    return jnp.sum(jnp.where(lane == idx, x, 0.0), axis=-1, keepdims=True)


def _dn_chunk(qt, kt, v, ab, alog, dtb, h):
    B, C = qt.shape[0], DN_CHUNK
    g = -jnp.exp(_lane_pick(alog, h)) * _softplus(_lane_pick(ab, h) + _lane_pick(dtb, h))
    beta = _sigmoid(_lane_pick(ab, h + DN_V_HEADS))
    q = qt * lax.rsqrt(jnp.sum(qt * qt, axis=-1, keepdims=True) + EPS) * (HEAD_DIM ** -0.5)
    k = kt * lax.rsqrt(jnp.sum(kt * kt, axis=-1, keepdims=True) + EPS)
    row = lax.broadcasted_iota(jnp.int32, (B, C, C), 1)
    col = lax.broadcasted_iota(jnp.int32, (B, C, C), 2)
    lower = (row >= col).astype(F32)
    ones = jnp.ones((B, C, C), F32)
    g_wide = jnp.broadcast_to(g, (B, C, LANE))
    g_sq = jnp.broadcast_to(g, (B, C, C))
    gc = mm01(lower, g_wide)
    gc_i = gc[:, :, :C]
    gc_j = mm01(ones, jnp.where(row <= col, g_sq, 0.0))
    g_last = jnp.broadcast_to(gc[:, C - 1:C, :], (B, C, LANE))
    decay = jnp.exp(jnp.where(row >= col, gc_i - gc_j, -1e30))
    eg = jnp.exp(gc)
    kk = mm_nt(k, k)
    a_mat = jnp.where(row > col, jnp.broadcast_to(beta, (B, C, C)) * kk * decay, 0.0)
    eye = (row == col).astype(F32)
    y = -a_mat
    t = eye + y
    for _ in range(5):
        y = mm(y, y)
        t = t + mm(t, y)
    bb = jnp.broadcast_to(beta, (B, C, LANE))
    u0 = mm3(t, v * bb)
    w = mm3(t, k * (bb * eg))
    qk = mm_nt(q, k) * decay
    q_dec = q * eg
    k_dec = k * jnp.exp(g_last - gc)
    cd = jnp.exp(g_last)[:, :8, :]
    return u0, w, qk, q_dec, k_dec, cd


def _shift_rows(x, j, down):
    if j == 0:
        return x
    n = x.shape[0]
    r = lax.broadcasted_iota(jnp.int32, x.shape, 0)
    if down:
        return jnp.where(r >= j, pltpu.roll(x, j, 0), 0.0)
    return jnp.where(r < n - j, pltpu.roll(x, n - j, 0), 0.0)


def dn_conv_fwd(proj, conv_w):
    s = proj.shape[0]

    def body(x_ref, w_ref, o_ref):
        x = x_ref[...]
        w = w_ref[...]
        pre = x * w[3:4, :]
        for j in (1, 2, 3):
            pre = pre + _shift_rows(x, j, True) * w[3 - j:4 - j, :]
        o_ref[...] = _silu(pre)

    return pl.pallas_call(
        body, name="dn_conv_fwd", grid=(DN_QKV // LANE,),
        in_specs=[pl.BlockSpec((s, LANE), lambda j: (0, j)), pl.BlockSpec((8, LANE), lambda j: (0, j))],
        out_specs=pl.BlockSpec((s, LANE), lambda j: (0, j)),
        out_shape=jax.ShapeDtypeStruct((s, DN_QKV), F32), compiler_params=_params("parallel"),
    )(proj, conv_w)


def dn_conv_bwd(dact, proj, conv_w, dproj_in, dab, blk0, name):
    s = proj.shape[0]
    nblk = dact.shape[1] // LANE
    extra = 2 if dab is not None else 0

    def body(*refs):
        if dab is not None:
            da_ref, x_ref, w_ref, _, dab_ref, dp_ref, dw_ref = refs
        else:
            da_ref, x_ref, w_ref, _, dp_ref, dw_ref = refs
        j = pl.program_id(0)

        @pl.when(j < nblk)
        def _():
            x = x_ref[...]
            w = w_ref[...]
            xs = [_shift_rows(x, 3 - kk_, True) for kk_ in range(4)]
            pre = xs[0] * w[0:1, :]
            for kk_ in (1, 2, 3):
                pre = pre + xs[kk_] * w[kk_:kk_ + 1, :]
            dpre = da_ref[...] * _silu_grad(pre)
            dx = dpre * w[3:4, :]
            for jj in (1, 2, 3):
                dx = dx + _shift_rows(dpre, jj, False) * w[3 - jj:4 - jj, :]
            dp_ref[...] = dx
            rows = [jnp.sum(dpre * xs[kk_], axis=0, keepdims=True) for kk_ in range(4)]
            dw_ref[...] = jnp.concatenate(rows + [jnp.zeros((4, LANE), F32)], axis=0)

        if dab is not None:
            @pl.when(j == nblk)
            def _():
                dp_ref[...] = dab_ref[...]

            @pl.when(j == nblk + 1)
            def _():
                dp_ref[...] = jnp.zeros_like(dp_ref)

    cl = lambda j: jnp.minimum(j, nblk - 1)
    in_specs = [pl.BlockSpec((s, LANE), lambda j: (0, cl(j))),
                pl.BlockSpec((s, LANE), lambda j: (0, blk0 + cl(j))),
                pl.BlockSpec((8, LANE), lambda j: (0, blk0 + cl(j))),
                pl.BlockSpec(memory_space=pl.ANY)]
    args = [dact, proj, conv_w, dproj_in]
    if dab is not None:
        in_specs.append(pl.BlockSpec((s, LANE), lambda j: (0, 0)))
        args.append(dab)
    return pl.pallas_call(
        body, name=name, grid=(nblk + extra,), in_specs=in_specs,
        out_specs=[pl.BlockSpec((s, LANE), lambda j: (0, blk0 + j)), pl.BlockSpec((8, LANE), lambda j: (0, cl(j)))],
        out_shape=[jax.ShapeDtypeStruct(dproj_in.shape, F32), jax.ShapeDtypeStruct((8, dact.shape[1]), F32)],
        input_output_aliases={3: 0}, compiler_params=_params("arbitrary"),
    )(*args)


DN_GROUP = 8
DN_ROWS = DN_GROUP * DN_CHUNK


def dn_prep_fwd(act, proj, alog, dtb):
    s = act.shape[0]
    nc = s // DN_CHUNK
    C = DN_CHUNK

    def body(q_ref, k_ref, v_ref, ab_ref, al_ref, dt_ref, u0_ref, w_ref, qd_ref, kd_ref, qk_ref, cd_ref):
        qh = pl.program_id(0)
        al = al_ref[0:1, :]
        dt = dt_ref[0:1, :]
        chunks = lambda x: x.reshape(DN_GROUP, C, x.shape[-1])
        rows = lambda x: x.reshape(DN_ROWS, x.shape[-1])
        qt, kt, ab = chunks(q_ref[...]), chunks(k_ref[...]), chunks(ab_ref[...])
        for hv in range(2):
            cs = slice(hv * LANE, (hv + 1) * LANE)
            u0, w, qk, qd, kd, cd = _dn_chunk(qt, kt, chunks(v_ref[:, cs]), ab, al, dt, 2 * qh + hv)
            u0_ref[:, cs] = rows(u0)
            w_ref[:, cs] = rows(w)
            qd_ref[:, cs] = rows(qd)
            kd_ref[:, cs] = rows(kd)
            qk_ref[hv] = rows(qk)
            cd_ref[hv] = cd

    big = pl.BlockSpec((DN_ROWS, 2 * LANE), lambda h, g: (g, h))
    wide = jax.ShapeDtypeStruct((s, MIX_WIDTH), F32)
    return pl.pallas_call(
        body, name="dn_prep_fwd", grid=(DN_QK_HEADS, s // DN_ROWS),
        in_specs=[pl.BlockSpec((DN_ROWS, LANE), lambda h, g: (g, h)),
                  pl.BlockSpec((DN_ROWS, LANE), lambda h, g: (g, DN_QK_HEADS + h)),
                  pl.BlockSpec((DN_ROWS, 2 * LANE), lambda h, g: (g, DN_QK_HEADS + h)),
                  pl.BlockSpec((DN_ROWS, LANE), lambda h, g: (g, DN_AB_BLK)),
                  pl.BlockSpec((8, LANE), lambda h, g: (0, 0)), pl.BlockSpec((8, LANE), lambda h, g: (0, 0))],
        out_specs=[big, big, big, big,
                   pl.BlockSpec((2, DN_ROWS, C), lambda h, g: (h, g, 0)),
                   pl.BlockSpec((2, DN_GROUP, 8, LANE), lambda h, g: (h, g, 0, 0))],
        out_shape=[wide, wide, wide, wide, jax.ShapeDtypeStruct((DN_V_HEADS, s, C), F32),
                   jax.ShapeDtypeStruct((DN_V_HEADS, nc, 8, LANE), F32)],
        compiler_params=_params("parallel", "parallel"),
    )(act, act, act, proj, alog, dtb)


def dn_prep_bwd(act, proj, alog, dtb, du0, dw, dqd, dkd, dqk, dcd):
    s = act.shape[0]
    C = DN_CHUNK

    def body(q_ref, k_ref, v_ref, ab_ref, al_ref, dt_ref, du0_ref, dw_ref, dqd_ref, dkd_ref, dqk_ref, dcd_ref,
             dq_ref, dk_ref, dv_ref, dab_ref, dal_ref, ddt_ref):
        g_id = pl.program_id(0)
        qh = pl.program_id(1)
        al = al_ref[0:1, :]
        dt = dt_ref[0:1, :]

        @pl.when(qh == 0)
        def _():
            dab_ref[...] = jnp.zeros_like(dab_ref)

        @pl.when((qh == 0) & (g_id == 0))
        def _():
            dal_ref[...] = jnp.zeros_like(dal_ref)
            ddt_ref[...] = jnp.zeros_like(ddt_ref)

        chunks = lambda x: x.reshape(DN_GROUP, C, x.shape[-1])
        rows = lambda x: x.reshape(DN_ROWS, x.shape[-1])
        qt, kt, ab = chunks(q_ref[...]), chunks(k_ref[...]), chunks(ab_ref[...])
        dq_acc = jnp.zeros((DN_ROWS, LANE), F32)
        dk_acc = jnp.zeros((DN_ROWS, LANE), F32)
        for hv in range(2):
            cs = slice(hv * LANE, (hv + 1) * LANE)
            h = 2 * qh + hv
            f = lambda qt_, kt_, v_, ab_, a_, d_: _dn_chunk(qt_, kt_, v_, ab_, a_, d_, h)
            _, vjp = jax.vjp(f, qt, kt, chunks(v_ref[:, cs]), ab, al, dt)
            dq, dk, dv, dab, dal, ddt = vjp((chunks(du0_ref[:, cs]), chunks(dw_ref[:, cs]), chunks(dqk_ref[hv]),
                                             chunks(dqd_ref[:, cs]), chunks(dkd_ref[:, cs]), dcd_ref[hv]))
            dq_acc = dq_acc + rows(dq)
            dk_acc = dk_acc + rows(dk)
            dv_ref[:, cs] = rows(dv)
            dab_ref[...] += rows(dab)
            dal_ref[0:1, :] += dal
            ddt_ref[0:1, :] += ddt
        dq_ref[...] = dq_acc
        dk_ref[...] = dk_acc

    big = pl.BlockSpec((DN_ROWS, 2 * LANE), lambda g, h: (g, h))
    one = pl.BlockSpec((DN_ROWS, LANE), lambda g, h: (g, h))
    small = pl.BlockSpec((8, LANE), lambda g, h: (0, 0))
    return pl.pallas_call(
        body, name="dn_prep_bwd", grid=(s // DN_ROWS, DN_QK_HEADS),
        in_specs=[one, pl.BlockSpec((DN_ROWS, LANE), lambda g, h: (g, DN_QK_HEADS + h)),
                  pl.BlockSpec((DN_ROWS, 2 * LANE), lambda g, h: (g, DN_QK_HEADS + h)),
                  pl.BlockSpec((DN_ROWS, LANE), lambda g, h: (g, DN_AB_BLK)), small, small,
                  big, big, big, big,
                  pl.BlockSpec((2, DN_ROWS, C), lambda g, h: (h, g, 0)),
                  pl.BlockSpec((2, DN_GROUP, 8, LANE), lambda g, h: (h, g, 0, 0))],
        out_specs=[one, one, big, pl.BlockSpec((DN_ROWS, LANE), lambda g, h: (g, 0)), small, small],
        out_shape=[jax.ShapeDtypeStruct((s, DN_QK_WIDTH), F32), jax.ShapeDtypeStruct((s, DN_QK_WIDTH), F32),
                   jax.ShapeDtypeStruct((s, MIX_WIDTH), F32), jax.ShapeDtypeStruct((s, LANE), F32),
                   jax.ShapeDtypeStruct((8, LANE), F32), jax.ShapeDtypeStruct((8, LANE), F32)],
        compiler_params=_params("arbitrary", "arbitrary"),
    )(act, act, act, proj, alog, dtb, du0, dw, dqd, dkd, dqk, dcd)


def dn_scan_fwd(u0, w, qd, kd, qk, cd, width):
    s = u0.shape[0]
    nc = s // DN_CHUNK
    C = DN_CHUNK

    def body(u0_ref, w_ref, qd_ref, kd_ref, qk_ref, cd_ref, o_ref, st_ref):
        def step(c, state):
            rs = pl.ds(pl.multiple_of(c * C, C), C)
            st_ref[0, c] = state
            u = u0_ref[rs, :] - _dg(w_ref[rs, :], state, _NN)
            o_ref[rs, :] = _dg(qd_ref[rs, :], state, _NN) + _dg(qk_ref[0, rs, :], u, _NN)
            return cd_ref[0, c][0:1, :] * state + _dg(kd_ref[rs, :], u, _TN)

        lax.fori_loop(0, nc, step, jnp.zeros((HEAD_DIM, HEAD_DIM), F32))

    col = pl.BlockSpec((s, LANE), lambda h: (0, h))
    return pl.pallas_call(
        body, name="dn_scan_fwd", grid=(DN_V_HEADS,),
        in_specs=[col, col, col, col, pl.BlockSpec((1, s, C), lambda h: (h, 0, 0)),
                  pl.BlockSpec((1, nc, 8, LANE), lambda h: (h, 0, 0, 0))],
        out_specs=[col, pl.BlockSpec((1, nc, HEAD_DIM, HEAD_DIM), lambda h: (h, 0, 0, 0))],
        out_shape=[jax.ShapeDtypeStruct((s, width), F32),
                   jax.ShapeDtypeStruct((DN_V_HEADS, nc, HEAD_DIM, HEAD_DIM), F32)],
        compiler_params=_params("parallel"),
    )(u0, w, qd, kd, qk, cd)


def dn_scan_bwd(do, u0, w, qd, kd, qk, cd, states):
    s = u0.shape[0]
    nc = s // DN_CHUNK
    C = DN_CHUNK

    def body(do_ref, u0_ref, w_ref, qd_ref, kd_ref, qk_ref, cd_ref, st_ref,
             du0_ref, dw_ref, dqd_ref, dkd_ref, dqk_ref, dcd_ref):
        def step(i, dstate):
            c = nc - 1 - i
            rs = pl.ds(pl.multiple_of(c * C, C), C)
            state = st_ref[0, c]
            g = do_ref[rs, :]
            w_c = w_ref[rs, :]
            kd_c = kd_ref[rs, :]
            qd_c = qd_ref[rs, :]
            qk_c = qk_ref[0, rs, :]
            cd_row = cd_ref[0, c][0:1, :]
            u = u0_ref[rs, :] - _dg(w_c, state, _NN)
            du = _dg(qk_c, g, _TN) + _dg(kd_c, dstate, _NN)
            du0_ref[rs, :] = du
            dw_ref[rs, :] = -_dg(du, state, _NT)
            dqd_ref[rs, :] = _dg(g, state, _NT)
            dkd_ref[rs, :] = _dg(u, dstate, _NT)
            dqk_ref[0, rs, :] = _dg(g, u, _NT)
            dcd_row = jnp.sum(state * dstate, axis=0, keepdims=True)
            dcd_ref[0, c] = jnp.concatenate([dcd_row, jnp.zeros((7, LANE), F32)], axis=0)
            return cd_row * dstate + _dg(qd_c, g, _TN) - _dg(w_c, du, _TN)

        lax.fori_loop(0, nc, step, jnp.zeros((HEAD_DIM, HEAD_DIM), F32))

    col = pl.BlockSpec((s, LANE), lambda h: (0, h))
    qk_spec = pl.BlockSpec((1, s, C), lambda h: (h, 0, 0))
    cd_spec = pl.BlockSpec((1, nc, 8, LANE), lambda h: (h, 0, 0, 0))
    wide = jax.ShapeDtypeStruct((s, MIX_WIDTH), F32)
    return pl.pallas_call(
        body, name="dn_scan_bwd", grid=(DN_V_HEADS,),
        in_specs=[col, col, col, col, col, qk_spec, cd_spec,
                  pl.BlockSpec((1, nc, HEAD_DIM, HEAD_DIM), lambda h: (h, 0, 0, 0))],
        out_specs=[col, col, col, col, qk_spec, cd_spec],
        out_shape=[wide, wide, wide, wide, jax.ShapeDtypeStruct((DN_V_HEADS, s, C), F32),
                   jax.ShapeDtypeStruct((DN_V_HEADS, nc, 8, LANE), F32)],
        compiler_params=_params("parallel"),
    )(do, u0, w, qd, kd, qk, cd, states)


SB_T = 256


def _sb_scores(q, kb, i, j):
    z = _dg(q, kb, _NT) * (HEAD_DIM ** -0.5)
    row = lax.broadcasted_iota(jnp.int32, (SB_T, SB_T), 0)
    col = lax.broadcasted_iota(jnp.int32, (SB_T, SB_T), 1)
    mask = (j * SB_T + col) < (i * SB_T + row)
    ls = _log_sigmoid(z)
    return mask, ls, jnp.where(mask, ls - z, 0.0)


def sb_fwd(qkn, proj, v_blk0, width):
    s = qkn.shape[0]

    def body(q_ref, k_ref, v_ref, o_ref, lt_ref):
        i = pl.program_id(1)
        q = q_ref[...]
        row = lax.broadcasted_iota(jnp.int32, (SB_T, SB_T), 0)
        col = lax.broadcasted_iota(jnp.int32, (SB_T, SB_T), 1)
        after = (row > col).astype(BF16)

        def step(jj, carry):
            run, acc = carry
            j = i - jj
            ks = pl.ds(pl.multiple_of(j * SB_T, SB_T), SB_T)
            mask, ls, lr = _sb_scores(q, k_ref[ks, :], i, j)
            later = _dg_exact_rhs(lr, after, _NN) + run
            wts = jnp.where(mask, jnp.exp(ls + later), 0.0)
            return run + jnp.sum(lr, axis=1, keepdims=True), acc + _dg(wts, v_ref[ks, :], _NN)

        run, acc = lax.fori_loop(0, i + 1, step, (jnp.zeros((SB_T, 1), F32), jnp.zeros((SB_T, HEAD_DIM), F32)))
        o_ref[...] = acc
        lt_ref[0] = run

    return pl.pallas_call(
        body, name="sb_fwd", grid=(SB_HEADS, s // SB_T),
        in_specs=[pl.BlockSpec((SB_T, LANE), lambda h, i: (i, h)),
                  pl.BlockSpec((s, LANE), lambda h, i: (0, SB_HEADS + h)),
                  pl.BlockSpec((s, LANE), lambda h, i: (0, v_blk0 + h))],
        out_specs=[pl.BlockSpec((SB_T, LANE), lambda h, i: (i, h)), pl.BlockSpec((1, SB_T, 1), lambda h, i: (h, i, 0))],
        out_shape=[jax.ShapeDtypeStruct((s, width), F32), jax.ShapeDtypeStruct((SB_HEADS, s, 1), F32)],
        compiler_params=_params("parallel", "parallel"),
    )(qkn, qkn, proj)


def sb_bwd(qkn, proj, v_blk0, do, ltot, dproj_in):
    s = qkn.shape[0]

    def body(q_ref, k_ref, v_ref, do_ref, lt_ref, _, dq_ref, dk_ref, dv_ref):
        i = pl.program_id(1)

        @pl.when(i == 0)
        def _():
            dk_ref[...] = jnp.zeros_like(dk_ref)
            dv_ref[...] = jnp.zeros_like(dv_ref)

        q = q_ref[...]
        g = do_ref[...]
        ltot = lt_ref[0]
        row = lax.broadcasted_iota(jnp.int32, (SB_T, SB_T), 0)
        col = lax.broadcasted_iota(jnp.int32, (SB_T, SB_T), 1)
        upto = (row <= col).astype(BF16)
        before = (row < col).astype(BF16)

        def step(j, carry):
            plr, pdl, dq = carry
            ks = pl.ds(pl.multiple_of(j * SB_T, SB_T), SB_T)
            kb = k_ref[ks, :]
            vb = v_ref[ks, :]
            mask, ls, lr = _sb_scores(q, kb, i, j)
            later = ltot - (_dg_exact_rhs(lr, upto, _NN) + plr)
            wts = jnp.where(mask, jnp.exp(ls + later), 0.0)
            dl = _dg(g, vb, _NT) * wts
            pre = _dg_exact_rhs(dl, before, _NN) + pdl
            sz = jnp.exp(ls)
            dz = jnp.where(mask, dl * (1.0 - sz) - sz * pre, 0.0) * (HEAD_DIM ** -0.5)
            dk_ref[ks, :] += _dg(dz, q, _TN)
            dv_ref[ks, :] += _dg(wts, g, _TN)
            return (plr + jnp.sum(lr, axis=1, keepdims=True), pdl + jnp.sum(dl, axis=1, keepdims=True),
                    dq + _dg(dz, kb, _NN))

        zero = jnp.zeros((SB_T, 1), F32)
        _, _, dq = lax.fori_loop(0, i + 1, step, (zero, zero, jnp.zeros((SB_T, HEAD_DIM), F32)))
        dq_ref[...] = dq

    tile = pl.BlockSpec((SB_T, LANE), lambda h, i: (i, h))
    colspec = pl.BlockSpec((s, LANE), lambda h, i: (0, h))
    return pl.pallas_call(
        body, name="sb_bwd", grid=(SB_HEADS, s // SB_T),
        in_specs=[tile, pl.BlockSpec((s, LANE), lambda h, i: (0, SB_HEADS + h)),
                  pl.BlockSpec((s, LANE), lambda h, i: (0, v_blk0 + h)), tile,
                  pl.BlockSpec((1, SB_T, 1), lambda h, i: (h, i, 0)), pl.BlockSpec(memory_space=pl.ANY)],
        out_specs=[tile, colspec, pl.BlockSpec((s, LANE), lambda h, i: (0, v_blk0 + h))],
        out_shape=[jax.ShapeDtypeStruct((s, MIX_WIDTH), F32), jax.ShapeDtypeStruct((s, MIX_WIDTH), F32),
                   jax.ShapeDtypeStruct(dproj_in.shape, F32)],
        input_output_aliases={5: 2}, compiler_params=_params("parallel", "arbitrary"),
    )(qkn, qkn, proj, do, ltot, dproj_in)


ROW_TILE = 256
HN_HEADS = 8


def head_norm_fwd(x, x_blk0, nblk, gains, out_dtype, out_width, name):
    s = x.shape[0]
    assert x_blk0 % HN_HEADS == 0 and nblk % (HN_HEADS * gains.shape[0]) == 0
    per = nblk // gains.shape[0] // HN_HEADS
    w = HN_HEADS * LANE

    def body(x_ref, g_ref, o_ref):
        gain = g_ref[0, 0:1, :]
        for j in range(HN_HEADS):
            cs = slice(j * LANE, (j + 1) * LANE)
            xv = x_ref[:, cs]
            r = lax.rsqrt(jnp.mean(xv * xv, axis=1, keepdims=True) + EPS)
            o_ref[:, cs] = (xv * r * gain).astype(out_dtype)

    return pl.pallas_call(
        body, name=name, grid=(nblk // HN_HEADS, s // ROW_TILE),
        in_specs=[pl.BlockSpec((ROW_TILE, w), lambda j, t: (t, x_blk0 // HN_HEADS + j)),
                  pl.BlockSpec((1, 8, LANE), lambda j, t: (j // per, 0, 0))],
        out_specs=pl.BlockSpec((ROW_TILE, w), lambda j, t: (t, j)),
        out_shape=jax.ShapeDtypeStruct((s, out_width), out_dtype), compiler_params=_params("parallel", "parallel"),
    )(x, gains)


def head_norm_bwd(dy, dy_blk0, x, x_blk0, nblk, gain, dst, dst_blk0, name):
    s = x.shape[0]

    def body(*refs):
        if dst is not None:
            dy_ref, x_ref, g_ref, _, dx_ref, dg_ref = refs
        else:
            dy_ref, x_ref, g_ref, dx_ref, dg_ref = refs

        @pl.when((pl.program_id(0) == 0) & (pl.program_id(1) == 0))
        def _():
            dg_ref[...] = jnp.zeros_like(dg_ref)

        gain = g_ref[0:1, :]
        dg = jnp.zeros((1, LANE), F32)
        for j in range(HN_HEADS):
            cs = slice(j * LANE, (j + 1) * LANE)
            xv = x_ref[:, cs]
            g = dy_ref[:, cs]
            r = lax.rsqrt(jnp.mean(xv * xv, axis=1, keepdims=True) + EPS)
            gy = g * gain
            dx_ref[:, cs] = r * gy - xv * (r * r * r) * jnp.mean(gy * xv, axis=1, keepdims=True)
            dg = dg + jnp.sum(g * xv * r, axis=0, keepdims=True)
        dg_ref[0:1, :] += dg

    assert dy_blk0 % HN_HEADS == 0 and x_blk0 % HN_HEADS == 0 and dst_blk0 % HN_HEADS == 0 and nblk % HN_HEADS == 0
    w = HN_HEADS * LANE
    in_specs = [pl.BlockSpec((ROW_TILE, w), lambda j, t: (t, dy_blk0 // HN_HEADS + j)),
                pl.BlockSpec((ROW_TILE, w), lambda j, t: (t, x_blk0 // HN_HEADS + j)),
                pl.BlockSpec((8, LANE), lambda j, t: (0, 0))]
    args = [dy, x, gain]
    aliases = {}
    if dst is not None:
        in_specs.append(pl.BlockSpec(memory_space=pl.ANY))
        args.append(dst)
        aliases = {3: 0}
        out0 = jax.ShapeDtypeStruct(dst.shape, F32)
    else:
        out0 = jax.ShapeDtypeStruct((s, (dst_blk0 + nblk) * LANE), F32)
    return pl.pallas_call(
        body, name=name, grid=(nblk // HN_HEADS, s // ROW_TILE), in_specs=in_specs,
        out_specs=[pl.BlockSpec((ROW_TILE, w), lambda j, t: (t, dst_blk0 // HN_HEADS + j)),
                   pl.BlockSpec((8, LANE), lambda j, t: (0, 0))],
        out_shape=[out0, jax.ShapeDtypeStruct((8, LANE), F32)],
        input_output_aliases=aliases, compiler_params=_params("arbitrary", "arbitrary"),
    )(*args)


def _xa_head(xq, kraw, v, qg, kg):
    q = xq * lax.rsqrt(jnp.mean(xq * xq, axis=1, keepdims=True) + EPS) * qg
    k = kraw * lax.rsqrt(jnp.mean(kraw * kraw, axis=1, keepdims=True) + EPS) * kg
    sc = mm_nt(q, k) * (XA_DIM ** -0.5)
    e = jnp.exp(sc - lax.stop_gradient(jnp.max(sc, axis=1, keepdims=True)))
    return mm(e / jnp.sum(e, axis=1, keepdims=True), v)


def xa_fwd(proj, xq_blk0, kv, qg, kg, cat):
    s = proj.shape[0]
    n_mem = kv.shape[0]

    def body(xq_ref, k_ref, v_ref, qg_ref, kg_ref, _, o_ref):
        o_ref[...] = _xa_head(xq_ref[...], k_ref[...], v_ref[...], qg_ref[0:1, :], kg_ref[0:1, :])

    gain = pl.BlockSpec((8, XA_DIM), lambda h, t: (0, 0))
    return pl.pallas_call(
        body, name="xa_fwd", grid=(XA_HEADS, s // ROW_TILE),
        in_specs=[pl.BlockSpec((ROW_TILE, XA_DIM), lambda h, t: (t, xq_blk0 // 2 + h)),
                  pl.BlockSpec((n_mem, XA_DIM), lambda h, t: (0, h)),
                  pl.BlockSpec((n_mem, XA_DIM), lambda h, t: (0, XA_HEADS + h)), gain, gain,
                  pl.BlockSpec(memory_space=pl.ANY)],
        out_specs=pl.BlockSpec((ROW_TILE, XA_DIM), lambda h, t: (t, MIX_WIDTH // XA_DIM + h)),
        out_shape=jax.ShapeDtypeStruct(cat.shape, F32), input_output_aliases={5: 0},
        compiler_params=_params("parallel", "parallel"),
    )(proj, kv, kv, qg, kg, cat)


def xa_bwd(proj, xq_blk0, kv, qg, kg, dcat, dproj_in):
    s = proj.shape[0]
    n_mem = kv.shape[0]

    def body(xq_ref, k_ref, v_ref, qg_ref, kg_ref, do_ref, _, dxq_ref, dk_ref, dv_ref, dqg_ref, dkg_ref):
        h = pl.program_id(0)
        t = pl.program_id(1)

        @pl.when(t == 0)
        def _():
            dk_ref[...] = jnp.zeros_like(dk_ref)
            dv_ref[...] = jnp.zeros_like(dv_ref)

        @pl.when((t == 0) & (h == 0))
        def _():
            dqg_ref[...] = jnp.zeros_like(dqg_ref)
            dkg_ref[...] = jnp.zeros_like(dkg_ref)

        _, vjp = jax.vjp(_xa_head, xq_ref[...], k_ref[...], v_ref[...], qg_ref[0:1, :], kg_ref[0:1, :])
        dxq, dk, dv, dqg, dkg = vjp(do_ref[...])
        dxq_ref[...] = dxq
        dk_ref[...] += dk
        dv_ref[...] += dv
        dqg_ref[0:1, :] += dqg
        dkg_ref[0:1, :] += dkg

    gain = pl.BlockSpec((8, XA_DIM), lambda h, t: (0, 0))
    kspec = pl.BlockSpec((n_mem, XA_DIM), lambda h, t: (0, h))
    vspec = pl.BlockSpec((n_mem, XA_DIM), lambda h, t: (0, XA_HEADS + h))
    return pl.pallas_call(
        body, name="xa_bwd", grid=(XA_HEADS, s // ROW_TILE),
        in_specs=[pl.BlockSpec((ROW_TILE, XA_DIM), lambda h, t: (t, xq_blk0 // 2 + h)), kspec, vspec, gain, gain,
                  pl.BlockSpec((ROW_TILE, XA_DIM), lambda h, t: (t, MIX_WIDTH // XA_DIM + h)),
                  pl.BlockSpec(memory_space=pl.ANY)],
        out_specs=[pl.BlockSpec((ROW_TILE, XA_DIM), lambda h, t: (t, xq_blk0 // 2 + h)), kspec, kspec, gain, gain],
        out_shape=[jax.ShapeDtypeStruct(dproj_in.shape, F32), jax.ShapeDtypeStruct((n_mem, XA_WIDTH), F32),
                   jax.ShapeDtypeStruct((n_mem, XA_WIDTH), F32), jax.ShapeDtypeStruct((8, XA_DIM), F32),
                   jax.ShapeDtypeStruct((8, XA_DIM), F32)],
        input_output_aliases={6: 0}, compiler_params=_params("arbitrary", "arbitrary"),
    )(proj, kv, kv, qg, kg, dcat, dproj_in)


GATE_ROWS = 1024


def gate_fwd(cat, proj, z_blk0):
    s = cat.shape[0]

    def body(c_ref, z_ref, y_ref):
        y_ref[...] = (c_ref[...] * _silu(z_ref[...])).astype(BF16)

    w = 2 * LANE
    rt = min(GATE_ROWS, s)
    return pl.pallas_call(
        body, name="gate_fwd", grid=(INNER // w, s // rt),
        in_specs=[pl.BlockSpec((rt, w), lambda j, t: (t, j)),
                  pl.BlockSpec((rt, w), lambda j, t: (t, z_blk0 // 2 + j))],
        out_specs=pl.BlockSpec((rt, w), lambda j, t: (t, j)),
        out_shape=jax.ShapeDtypeStruct((s, INNER), BF16), compiler_params=_params("parallel", "parallel"),
    )(cat, proj)


def gate_bwd(dy, cat, proj, z_blk0):
    s = cat.shape[0]

    def body(dy_ref, c_ref, z_ref, dc_ref, dz_ref):
        z = z_ref[...]
        g = dy_ref[...]
        dc_ref[...] = g * _silu(z)
        dz_ref[...] = g * c_ref[...] * _silu_grad(z)

    w = 2 * LANE
    rt = min(GATE_ROWS, s)
    tile = pl.BlockSpec((rt, w), lambda j, t: (t, j))
    ztile = pl.BlockSpec((rt, w), lambda j, t: (t, z_blk0 // 2 + j))
    return pl.pallas_call(
        body, name="gate_bwd", grid=(INNER // w, s // rt), in_specs=[tile, tile, ztile],
        out_specs=[tile, ztile],
        out_shape=[jax.ShapeDtypeStruct((s, INNER), F32), jax.ShapeDtypeStruct(proj.shape, F32)],
        compiler_params=_params("parallel", "parallel"),
    )(dy, cat, proj)


NORM_ROWS = 256


def rms_fwd(x, gain):
    s, d = x.shape

    def body(x_ref, g_ref, o_ref):
        xv = x_ref[...]
        r = lax.rsqrt(jnp.mean(xv * xv, axis=1, keepdims=True) + EPS)
        o_ref[...] = (xv * r * g_ref[0:1, :]).astype(BF16)

    return pl.pallas_call(
        body, name="rms_fwd", grid=(s // NORM_ROWS,),
        in_specs=[pl.BlockSpec((NORM_ROWS, d), lambda t: (t, 0)), pl.BlockSpec((8, d), lambda t: (0, 0))],
        out_specs=pl.BlockSpec((NORM_ROWS, d), lambda t: (t, 0)),
        out_shape=jax.ShapeDtypeStruct((s, d), BF16), compiler_params=_params("parallel"),
    )(x, gain)


def rms_bwd(dh, x, gain, dres):
    s, d = x.shape

    def body(*refs):
        if dres is not None:
            dh_ref, x_ref, g_ref, dr_ref, dx_ref, dg_ref = refs
        else:
            dh_ref, x_ref, g_ref, dx_ref, dg_ref = refs

        @pl.when(pl.program_id(0) == 0)
        def _():
            dg_ref[...] = jnp.zeros_like(dg_ref)

        xv = x_ref[...]
        g = dh_ref[...]
        r = lax.rsqrt(jnp.mean(xv * xv, axis=1, keepdims=True) + EPS)
        gy = g * g_ref[0:1, :]
        dx = r * gy - xv * (r * r * r) * jnp.mean(gy * xv, axis=1, keepdims=True)
        dx_ref[...] = dx + dr_ref[...] if dres is not None else dx
        dg_ref[0:1, :] += jnp.sum(g * xv * r, axis=0, keepdims=True)

    tile = pl.BlockSpec((NORM_ROWS, d), lambda t: (t, 0))
    gspec = pl.BlockSpec((8, d), lambda t: (0, 0))
    args = [dh, x, gain] + ([dres] if dres is not None else [])
    return pl.pallas_call(
        body, name="rms_bwd", grid=(s // NORM_ROWS,),
        in_specs=[tile, tile, gspec] + ([tile] if dres is not None else []),
        out_specs=[tile, gspec],
        out_shape=[jax.ShapeDtypeStruct((s, d), F32), jax.ShapeDtypeStruct((8, d), F32)],
        compiler_params=_params("arbitrary"),
    )(*args)


def loss_fwd_bwd(y, target):
    s, d = y.shape

    def body(y_ref, t_ref, l_ref, dy_ref):
        @pl.when(pl.program_id(0) == 0)
        def _():
            l_ref[...] = jnp.zeros_like(l_ref)

        err = y_ref[...] - t_ref[...]
        dy_ref[...] = err * (1.0 / d)
        part = 0.5 * jnp.sum(jnp.mean(err * err, axis=1, keepdims=True), axis=0, keepdims=True)
        r = lax.broadcasted_iota(jnp.int32, (8, LANE), 0)
        c = lax.broadcasted_iota(jnp.int32, (8, LANE), 1)
        l_ref[...] += jnp.where((r == 0) & (c == 0), part, 0.0)

    tile = pl.BlockSpec((NORM_ROWS, d), lambda t: (t, 0))
    return pl.pallas_call(
        body, name="loss", grid=(s // NORM_ROWS,), in_specs=[tile, tile],
        out_specs=[pl.BlockSpec((8, LANE), lambda t: (0, 0)), tile],
        out_shape=[jax.ShapeDtypeStruct((8, LANE), F32), jax.ShapeDtypeStruct((s, d), F32)],
        compiler_params=_params("arbitrary"),
    )(y, target)


def matmul(a, b, mode, out_dtype, tm, tn, tk, name, add=None, b_blocked=False, out_blocks=None):
    if b_blocked:
        nb, _, width = b.shape
        bshape = (b.shape[1], nb * width)
    else:
        bshape = b.shape
    if mode == "tn":
        (kdim, m), n = a.shape, bshape[1]
    else:
        (m, kdim), n = a.shape, (bshape[1] if mode == "nn" else bshape[0])
    tm, tn, tk = min(tm, m), min(tn, n), min(tk, kdim)
    assert m % tm == 0 and n % tn == 0 and kdim % tk == 0, (name, m, n, kdim)
    nk = kdim // tk
    dims = {"nn": _NN, "nt": _NT, "tn": _TN}[mode]

    def body(*refs):
        if add is not None:
            a_ref, b_ref, add_ref, o_ref, acc_ref = refs
        else:
            a_ref, b_ref, o_ref, acc_ref = refs
        k = pl.program_id(2)

        @pl.when(k == 0)
        def _():
            acc_ref[...] = jnp.zeros_like(acc_ref)

        acc_ref[...] += _dg(a_ref[...], b_ref[...], dims)

        @pl.when(k == nk - 1)
        def _():
            r = acc_ref[...]
            if add is not None:
                r = r + add_ref[...]
            o_ref[...] = r.astype(out_dtype)

    a_spec = pl.BlockSpec((tk, tm), lambda i, j, k: (k, i)) if mode == "tn" else pl.BlockSpec((tm, tk), lambda i, j, k: (i, k))
    if b_blocked and mode == "nn":
        per = width // tn
        assert width % tn == 0
        b_spec = pl.BlockSpec((None, tk, tn), lambda i, j, k: (j // per, k, j % per))
    elif b_blocked and mode == "nt":
        per = width // tk
        assert width % tk == 0
        b_spec = pl.BlockSpec((None, tn, tk), lambda i, j, k: (k // per, j, k % per))
    elif mode == "nt":
        b_spec = pl.BlockSpec((tn, tk), lambda i, j, k: (j, k))
    else:
        assert not b_blocked
        b_spec = pl.BlockSpec((tk, tn), lambda i, j, k: (k, j))
    add_spec = pl.BlockSpec((tm, tn), lambda i, j, k: (i, j))
    if out_blocks is not None:
        operb = (n // out_blocks) // tn
        assert (n // out_blocks) % tn == 0 and add is None
        o_spec = pl.BlockSpec((None, tm, tn), lambda i, j, k: (j // operb, i, j % operb))
        out_shape = jax.ShapeDtypeStruct((out_blocks, m, n // out_blocks), out_dtype)
    else:
        o_spec = add_spec
        out_shape = jax.ShapeDtypeStruct((m, n), out_dtype)
    return pl.pallas_call(
        body, name=name, grid=(m // tm, n // tn, nk),
        in_specs=[a_spec, b_spec] + ([add_spec] if add is not None else []), out_specs=o_spec,
        out_shape=out_shape, scratch_shapes=[pltpu.VMEM((tm, tn), F32)],
        compiler_params=_params("parallel", "parallel", "arbitrary"),
    )(*([a, b] + ([add] if add is not None else [])))


_HBM = pl.BlockSpec(memory_space=pltpu.HBM)


def _me():
    return lax.axis_index("x"), lax.axis_index("y"), lax.axis_index("c")


def _flat(p):
    return 4 * p[0] + 2 * p[1] + p[2]


def _flip(p, r):
    return tuple((1 - v) if (r >> (2 - a)) & 1 else v for a, v in enumerate(p))


def all_gather(shards):
    n = len(shards)

    def body(*refs):
        srcs, outs = refs[:n], refs[n:2 * n]
        send_sems, recv_sems, local_sems = refs[2 * n:]
        me = _me()
        x, y, c = me
        sibling = (x, y, 1 - c)
        chips = [(1 - x, y), (x, 1 - y), (1 - x, 1 - y)]

        def copy(a, k, block, to, src=None):
            dst = outs[a].at[_flat(block)]
            return pltpu.make_async_remote_copy(src_ref=dst if src is None else src, dst_ref=dst,
                                                send_sem=send_sems.at[a, k], recv_sem=recv_sems.at[a, k],
                                                device_id=to, device_id_type=MESH)

        mine = [pltpu.make_async_copy(srcs[a], outs[a].at[_flat(me)], local_sems.at[a]) for a in range(n)]
        for cp in mine:
            cp.start()
        sent = []
        for a in range(n):
            sent.append(copy(a, 0, me, sibling, src=srcs[a]))
            sent += [copy(a, 1 + j, me, (*chip, c), src=srcs[a]) for j, chip in enumerate(chips)]
        for cp in sent:
            cp.start()
        for j, chip in enumerate(chips):
            for a in range(n):
                copy(a, 1 + j, (*chip, c), me).wait_recv()
                fwd = copy(a, 4 + j, (*chip, c), sibling)
                fwd.start()
                sent.append(fwd)
        for a in range(n):
            copy(a, 0, sibling, me).wait_recv()
            for j, chip in enumerate(chips):
                copy(a, 4 + j, (*chip, 1 - c), me).wait_recv()
        for cp in sent:
            cp.wait_send()
        for cp in mine:
            cp.wait()

    return pl.pallas_call(
        body, name="all_gather", in_specs=[_HBM] * n, out_specs=[_HBM] * n,
        out_shape=[jax.ShapeDtypeStruct((N_DEV,) + s.shape, s.dtype) for s in shards],
        scratch_shapes=[pltpu.SemaphoreType.DMA((n, 7)), pltpu.SemaphoreType.DMA((n, 7)), pltpu.SemaphoreType.DMA((n,))],
    )(*shards)


def reduce_scatter_send(srcs, slots, recv_shapes):
    n = len(srcs)
    nr = len(recv_shapes)

    def body(*refs):
        src_refs, recv_refs = refs[:n], refs[n:n + nr]
        send_sems, recv_sems, local_sems = refs[n + nr:]
        me = _me()

        def copy(a, r):
            peer = _flip(me, r)
            ri, layer = slots[a]
            return pltpu.make_async_remote_copy(src_ref=src_refs[a].at[_flat(peer)],
                                                dst_ref=recv_refs[ri].at[_flat(me), layer],
                                                send_sem=send_sems.at[a, r - 1], recv_sem=recv_sems.at[a, r - 1],
                                                device_id=peer, device_id_type=MESH)

        def arrival(a, r):
            peer = _flip(me, r)
            ri, layer = slots[a]
            land = recv_refs[ri].at[_flat(peer), layer]
            return pltpu.make_async_remote_copy(src_ref=land, dst_ref=land, send_sem=send_sems.at[a, r - 1],
                                                recv_sem=recv_sems.at[a, r - 1], device_id=peer, device_id_type=MESH)

        mine = [pltpu.make_async_copy(src_refs[a].at[_flat(me)], recv_refs[slots[a][0]].at[_flat(me), slots[a][1]],
                                      local_sems.at[a]) for a in range(n)]
        for cp in mine:
            cp.start()
        sent = [copy(a, r) for r in range(1, N_DEV) for a in range(n)]
        for cp in sent:
            cp.start()
        for r in range(1, N_DEV):
            for a in range(n):
                arrival(a, r).wait_recv()
        for cp in sent:
            cp.wait_send()
        for cp in mine:
            cp.wait()

    return pl.pallas_call(
        body, name="reduce_scatter_send", in_specs=[_HBM] * n, out_specs=[_HBM] * nr,
        out_shape=[jax.ShapeDtypeStruct(s, BF16) for s in recv_shapes],
        scratch_shapes=[pltpu.SemaphoreType.DMA((n, 7)), pltpu.SemaphoreType.DMA((n, 7)), pltpu.SemaphoreType.DMA((n,))],
    )(*srcs)


SMALL_ROWS = 24


def all_reduce_small(pack):
    def body(p_ref, o_ref, buf, send_sems, recv_sems):
        me = _me()
        buf[_flat(me)] = p_ref[...]
        sent = []
        for r in range(1, N_DEV):
            peer = _flip(me, r)
            cp = pltpu.make_async_remote_copy(src_ref=p_ref, dst_ref=buf.at[_flat(me)], send_sem=send_sems.at[r - 1],
                                              recv_sem=recv_sems.at[r - 1], device_id=peer, device_id_type=MESH)
            cp.start()
            sent.append(cp)
        for r in range(1, N_DEV):
            peer = _flip(me, r)
            land = buf.at[_flat(peer)]
            pltpu.make_async_remote_copy(src_ref=land, dst_ref=land, send_sem=send_sems.at[r - 1],
                                         recv_sem=recv_sems.at[r - 1], device_id=peer, device_id_type=MESH).wait_recv()
        for cp in sent:
            cp.wait_send()
        acc = buf[0]
        for d in range(1, N_DEV):
            acc = acc + buf[d]
        o_ref[...] = acc

    vm = pl.BlockSpec(memory_space=pltpu.VMEM)
    return pl.pallas_call(
        body, name="all_reduce_small", in_specs=[vm], out_specs=vm,
        out_shape=jax.ShapeDtypeStruct(pack.shape, F32),
        scratch_shapes=[pltpu.VMEM((N_DEV,) + pack.shape, F32), pltpu.SemaphoreType.DMA((7,)),
                        pltpu.SemaphoreType.DMA((7,))],
    )(pack)


def _adamw(w, g, m, v):
    m = ADAM_B1 * m + (1.0 - ADAM_B1) * g
    v = ADAM_B2 * v + (1.0 - ADAM_B2) * (g * g)
    m_hat = m / (1.0 - ADAM_B1 ** ADAM_STEP)
    v_hat = v / (1.0 - ADAM_B2 ** ADAM_STEP)
    delta = -ADAM_LR * (m_hat / (jnp.sqrt(v_hat) + ADAM_EPS) + ADAM_WD * w)
    return delta, m, v


ADAM_ROWS = 128


def reduce_adamw(recv, w, m, v, name):
    nl, rows, cols = w.shape
    cp = recv.shape[3]

    def body(r_ref, w_ref, m_ref, v_ref, g_ref, d_ref, mo_ref, vo_ref):
        g = r_ref[0, 0].astype(F32)
        for dev in range(1, N_DEV):
            g = g + r_ref[dev, 0].astype(F32)
        if cp != cols:
            g = g[:, :cols]
        delta, m_new, v_new = _adamw(w_ref[0], g, m_ref[0], v_ref[0])
        g_ref[0] = g
        d_ref[0] = delta
        mo_ref[0] = m_new
        vo_ref[0] = v_new

    tile = pl.BlockSpec((1, ADAM_ROWS, cols), lambda l, t: (l, t, 0))
    out = jax.ShapeDtypeStruct(w.shape, F32)
    return pl.pallas_call(
        body, name=name, grid=(nl, rows // ADAM_ROWS),
        in_specs=[pl.BlockSpec((N_DEV, 1, ADAM_ROWS, cp), lambda l, t: (0, l, t, 0)), tile, tile, tile],
        out_specs=[tile, tile, tile, tile], out_shape=[out, out, out, out],
        compiler_params=_params("parallel", "parallel"),
    )(recv, w, m, v)


def adamw_small(g, w, m, v):
    def body(g_ref, w_ref, m_ref, v_ref, d_ref, mo_ref, vo_ref):
        d_ref[...], mo_ref[...], vo_ref[...] = _adamw(w_ref[...], g_ref[...], m_ref[...], v_ref[...])

    out = jax.ShapeDtypeStruct(g.shape, F32)
    return pl.pallas_call(body, name="adamw_small", out_shape=[out, out, out])(g, w, m, v)


def _row8(v):
    return jnp.pad(v.reshape(1, -1).astype(F32), ((0, 7), (0, 0)))


def _row8_lanes(v, width=LANE):
    return jnp.pad(v.reshape(1, -1).astype(F32), ((0, 7), (0, width - v.size)))


def _pack_rows(parts):
    rows = []
    for p in parts:
        p = p.reshape(-1).astype(F32)
        nrow = -(-p.size // D_MODEL)
        rows.append(jnp.pad(p, (0, nrow * D_MODEL - p.size)).reshape(nrow, D_MODEL))
    out = jnp.concatenate(rows, axis=0)
    return jnp.pad(out, ((0, SMALL_ROWS - out.shape[0]), (0, 0)))


def _unpack_rows(pack, shapes):
    out, r = [], 0
    for shp in shapes:
        size = 1
        for d in shp:
            size *= d
        nrow = -(-size // D_MODEL)
        out.append(pack[r:r + nrow].reshape(-1)[:size].reshape(shp))
        r += nrow
    return out


def _dn_weight_layout(gathered):
    full = jnp.transpose(gathered[:, :, :DN_SHARD], (1, 0, 2)).reshape(D_MODEL, DN_PROJ)
    n_ab = 2 * DN_V_HEADS
    zeros = jnp.zeros((D_MODEL, 2 * LANE - n_ab), full.dtype)
    return jnp.concatenate([full[:, :DN_QKV + n_ab], zeros, full[:, DN_QKV + n_ab:]], axis=1)


def _dn_grad_blocks(dw):
    n_ab = 2 * DN_V_HEADS
    full = jnp.concatenate([dw[:, :DN_QKV + n_ab], dw[:, DN_QKV + 2 * LANE:]], axis=1)
    blocks = jnp.transpose(full.reshape(D_MODEL, N_DEV, DN_SHARD), (1, 0, 2))
    return jnp.pad(blocks, ((0, 0), (0, 0), (0, DN_SHARD_PAD - DN_SHARD)))


def kernel(x, mem, norm_g, mem_norm_g, mem_w_kv, xa_q_norm_g, xa_k_norm_g, w_out, dn_w_in, dn_conv_w, dn_a_log, dn_dt_bias, dn_out_norm_g, sb_w_in, sb_q_norm_g, sb_k_norm_g, loss_target, m_norm_g, m_mem_norm_g, m_mem_w_kv, m_xa_q_norm_g, m_xa_k_norm_g, m_w_out, m_dn_w_in, m_dn_conv_w, m_dn_a_log, m_dn_dt_bias, m_dn_out_norm_g, m_sb_w_in, m_sb_q_norm_g, m_sb_k_norm_g, v_norm_g, v_mem_norm_g, v_mem_w_kv, v_xa_q_norm_g, v_xa_k_norm_g, v_w_out, v_dn_w_in, v_dn_conv_w, v_dn_a_log, v_dn_dt_bias, v_dn_out_norm_g, v_sb_w_in, v_sb_q_norm_g, v_sb_k_norm_g):
    x0, memv, target = x[0], mem[0], loss_target[0]
    my_dev = 4 * lax.axis_index("x") + 2 * lax.axis_index("y") + lax.axis_index("c")

    dn_shard = jnp.pad(dn_w_in[0].astype(BF16), ((0, 0), (0, DN_SHARD_PAD - DN_SHARD)))
    w_out_b = w_out.astype(BF16)
    w_kv_b = mem_w_kv.astype(BF16)
    conv_shard = jnp.pad(dn_conv_w[0], ((0, 4), (0, 0)))
    g_dn, g_sb, g_wo0, g_wo1, g_kv0, g_kv1, g_conv = all_gather(
        [dn_shard, sb_w_in[0].astype(BF16), w_out_b[0], w_out_b[1], w_kv_b[0], w_kv_b[1], conv_shard])
    w_dn = _dn_weight_layout(g_dn)
    w_sb = g_sb
    w_o = [g_wo0.reshape(INNER, D_MODEL), g_wo1.reshape(INNER, D_MODEL)]
    w_kv = [g_kv0.reshape(D_MODEL, 2 * XA_WIDTH), g_kv1.reshape(D_MODEL, 2 * XA_WIDTH)]
    conv_w = jnp.transpose(g_conv, (1, 0, 2)).reshape(8, DN_QKV)

    ng = [_row8(norm_g[0]), _row8(norm_g[1])]
    mem_g = _row8(mem_norm_g)
    xqg = [_row8(xa_q_norm_g[0]), _row8(xa_q_norm_g[1])]
    xkg = [_row8(xa_k_norm_g[0]), _row8(xa_k_norm_g[1])]
    alog, dtb = _row8_lanes(dn_a_log[0]), _row8_lanes(dn_dt_bias[0])
    out_g, sbq_g, sbk_g = _row8(dn_out_norm_g[0]), _row8(sb_q_norm_g[0]), _row8(sb_k_norm_g[0])

    mem_n = rms_fwd(memv, mem_g)
    kv = [matmul(mem_n, w_kv[i], "nn", F32, 256, 1024, 2048, f"kv{i}") for i in range(2)]

    h0 = rms_fwd(x0, ng[0])
    proj0 = matmul(h0, w_dn, "nn", F32, 1024, 1152, 2048, "proj_dn")
    act = dn_conv_fwd(proj0, conv_w)
    u0, w_, qd, kd, qk, cd = dn_prep_fwd(act, proj0, alog, dtb)
    o_raw, states = dn_scan_fwd(u0, w_, qd, kd, qk, cd, MIX_WIDTH)
    cat0 = head_norm_fwd(o_raw, 0, DN_V_HEADS, out_g[None], F32, INNER, "dn_out_norm")
    cat0 = xa_fwd(proj0, DN_XQ_BLK, kv[0], xqg[0], xkg[0], cat0)
    y0 = gate_fwd(cat0, proj0, DN_Z_BLK)
    x1 = matmul(y0, w_o[0], "nn", F32, 1024, 1024, 2048, "out_proj0", add=x0)

    h1 = rms_fwd(x1, ng[1])
    proj1 = matmul(h1, w_sb, "nn", F32, 1024, 896, 2048, "proj_sb", b_blocked=True)
    qkn = head_norm_fwd(proj1, 0, 2 * SB_HEADS, jnp.stack([sbq_g, sbk_g]), BF16, 2 * MIX_WIDTH, "sb_qk_norm")
    cat1, ltot = sb_fwd(qkn, proj1, 2 * SB_HEADS, INNER)
    cat1 = xa_fwd(proj1, SB_XQ_BLK, kv[1], xqg[1], xkg[1], cat1)
    y1 = gate_fwd(cat1, proj1, SB_Z_BLK)
    x2 = matmul(y1, w_o[1], "nn", F32, 1024, 1024, 2048, "out_proj1", add=x1)
    loss_part, dx2 = loss_fwd_bwd(x2, target)

    dy1 = matmul(dx2, w_o[1], "nt", F32, 1024, 1024, 2048, "d_y1")
    dw_o1 = matmul(y1, dx2, "tn", BF16, 1024, 1024, 1024, "d_w_out1")
    dcat1, dproj1 = gate_bwd(dy1, cat1, proj1, SB_Z_BLK)
    dproj1, dxk1, dxv1, dxqg1, dxkg1 = xa_bwd(proj1, SB_XQ_BLK, kv[1], xqg[1], xkg[1], dcat1, dproj1)
    dqn, dkn, dproj1 = sb_bwd(qkn, proj1, 2 * SB_HEADS, dcat1, ltot, dproj1)
    dproj1, d_sbq = head_norm_bwd(dqn, 0, proj1, 0, SB_HEADS, sbq_g, dproj1, 0, "sb_q_norm_bwd")
    dproj1, d_sbk = head_norm_bwd(dkn, 0, proj1, SB_HEADS, SB_HEADS, sbk_g, dproj1, SB_HEADS, "sb_k_norm_bwd")
    dw_sb = matmul(h1, dproj1, "tn", BF16, 1024, 896, 1024, "d_w_sb", out_blocks=N_DEV)
    dh1 = matmul(dproj1, w_sb, "nt", F32, 1024, 1024, 896, "d_h1", b_blocked=True)
    dx1, d_ng1 = rms_bwd(dh1, x1, ng[1], dx2)

    dy0 = matmul(dx1, w_o[0], "nt", F32, 1024, 1024, 2048, "d_y0")
    dw_o0 = matmul(y0, dx1, "tn", BF16, 1024, 1024, 1024, "d_w_out0")
    dcat0, dproj0 = gate_bwd(dy0, cat0, proj0, DN_Z_BLK)
    dproj0, dxk0, dxv0, dxqg0, dxkg0 = xa_bwd(proj0, DN_XQ_BLK, kv[0], xqg[0], xkg[0], dcat0, dproj0)
    do_raw, d_outg = head_norm_bwd(dcat0, 0, o_raw, 0, DN_V_HEADS, out_g, None, 0, "dn_out_norm_bwd")
    du0, dw_, dqd, dkd, dqk, dcd = dn_scan_bwd(do_raw, u0, w_, qd, kd, qk, cd, states)
    dq_a, dk_a, dv_a, dab, d_alog, d_dtb = dn_prep_bwd(act, proj0, alog, dtb, du0, dw_, dqd, dkd, dqk, dcd)
    dproj0, dcw_q = dn_conv_bwd(dq_a, proj0, conv_w, dproj0, None, 0, "dn_conv_bwd_q")
    dproj0, dcw_k = dn_conv_bwd(dk_a, proj0, conv_w, dproj0, None, DN_QK_HEADS, "dn_conv_bwd_k")
    dproj0, dcw_v = dn_conv_bwd(dv_a, proj0, conv_w, dproj0, dab, 2 * DN_QK_HEADS, "dn_conv_bwd_v")
    dw_dn = matmul(h0, dproj0, "tn", BF16, 1024, 1152, 1024, "d_w_dn")
    dh0 = matmul(dproj0, w_dn, "nt", F32, 1024, 1024, 1152, "d_h0")
    grad_x, d_ng0 = rms_bwd(dh0, x0, ng[0], dx1)

    dkv = [jnp.concatenate([dxk0, dxv0], axis=1), jnp.concatenate([dxk1, dxv1], axis=1)]
    dw_kv = [matmul(mem_n, dkv[i], "tn", BF16, 1024, 1024, 256, f"d_w_kv{i}") for i in range(2)]
    dmem_n = matmul(dkv[0], w_kv[0], "nt", F32, 256, 1024, 2048, "d_mem_n0")
    dmem_n = matmul(dkv[1], w_kv[1], "nt", F32, 256, 1024, 2048, "d_mem_n1", add=dmem_n)
    _, d_memg = rms_bwd(dmem_n, memv, mem_g, None)

    recv_dn, recv_sb, recv_wo, recv_kv = reduce_scatter_send(
        [_dn_grad_blocks(dw_dn), dw_sb, dw_o0.reshape(N_DEV, INNER // N_DEV, D_MODEL),
         dw_o1.reshape(N_DEV, INNER // N_DEV, D_MODEL), dw_kv[0].reshape(N_DEV, D_MODEL // N_DEV, 2 * XA_WIDTH),
         dw_kv[1].reshape(N_DEV, D_MODEL // N_DEV, 2 * XA_WIDTH)],
        [(0, 0), (1, 0), (2, 0), (2, 1), (3, 0), (3, 1)],
        [(N_DEV, 1, D_MODEL, DN_SHARD_PAD), (N_DEV, 1, D_MODEL, SB_PROJ // N_DEV),
         (N_DEV, 2, INNER // N_DEV, D_MODEL), (N_DEV, 2, D_MODEL // N_DEV, 2 * XA_WIDTH)])
    big = {
        "dn_w_in": reduce_adamw(recv_dn, dn_w_in, m_dn_w_in, v_dn_w_in, "adamw_dn_w_in"),
        "sb_w_in": reduce_adamw(recv_sb, sb_w_in, m_sb_w_in, v_sb_w_in, "adamw_sb_w_in"),
        "w_out": reduce_adamw(recv_wo, w_out, m_w_out, v_w_out, "adamw_w_out"),
        "mem_w_kv": reduce_adamw(recv_kv, mem_w_kv, m_mem_w_kv, v_mem_w_kv, "adamw_mem_w_kv"),
    }

    dconv = jnp.concatenate([dcw_q, dcw_k, dcw_v], axis=1)[:4]
    small_shapes = [(2, D_MODEL), (D_MODEL,), (2, XA_DIM), (2, XA_DIM), (4, DN_QKV), (1, DN_V_HEADS),
                    (1, DN_V_HEADS), (1, HEAD_DIM), (1, HEAD_DIM), (1, HEAD_DIM), (1,)]
    pack = _pack_rows([jnp.stack([d_ng0[0], d_ng1[0]]), d_memg[0], jnp.stack([dxqg0[0], dxqg1[0]]),
                       jnp.stack([dxkg0[0], dxkg1[0]]), dconv, d_alog[0, :DN_V_HEADS], d_dtb[0, :DN_V_HEADS],
                       d_outg[0], d_sbq[0], d_sbk[0], loss_part[0, :1]])
    total = all_reduce_small(pack)
    (g_norm, g_memn, g_xq, g_xk, g_conv_full, g_alog, g_dtb, g_outn, g_sbq, g_sbk, loss1) = _unpack_rows(total, small_shapes)
    conv_cols = DN_QKV // N_DEV
    g_conv = lax.dynamic_slice(g_conv_full, (0, my_dev * conv_cols), (4, conv_cols))[None]
    names = ["norm_g", "mem_norm_g", "xa_q_norm_g", "xa_k_norm_g", "dn_conv_w", "dn_a_log", "dn_dt_bias",
             "dn_out_norm_g", "sb_q_norm_g", "sb_k_norm_g"]
    grads = [g_norm, g_memn, g_xq, g_xk, g_conv, g_alog, g_dtb, g_outn, g_sbq, g_sbk]
    ws = [norm_g, mem_norm_g, xa_q_norm_g, xa_k_norm_g, dn_conv_w, dn_a_log, dn_dt_bias, dn_out_norm_g, sb_q_norm_g,
          sb_k_norm_g]
    ms = [m_norm_g, m_mem_norm_g, m_xa_q_norm_g, m_xa_k_norm_g, m_dn_conv_w, m_dn_a_log, m_dn_dt_bias,
          m_dn_out_norm_g, m_sb_q_norm_g, m_sb_k_norm_g]
    vs = [v_norm_g, v_mem_norm_g, v_xa_q_norm_g, v_xa_k_norm_g, v_dn_conv_w, v_dn_a_log, v_dn_dt_bias,
          v_dn_out_norm_g, v_sb_q_norm_g, v_sb_k_norm_g]
    shapes = [w.shape for w in ws]
    d_p, m_p, v_p = adamw_small(_pack_rows(grads), _pack_rows(ws), _pack_rows(ms), _pack_rows(vs))
    small = dict(zip(names, zip(grads, _unpack_rows(d_p, shapes), _unpack_rows(m_p, shapes), _unpack_rows(v_p, shapes))))

    order = ["norm_g", "mem_norm_g", "mem_w_kv", "xa_q_norm_g", "xa_k_norm_g", "w_out", "dn_w_in", "dn_conv_w",
             "dn_a_log", "dn_dt_bias", "dn_out_norm_g", "sb_w_in", "sb_q_norm_g", "sb_k_norm_g"]
    res = {**big, **small}
    outs = [loss1.reshape(()), grad_x[None]]
    for k in range(4):
        outs += [res[n][k] for n in order]
    return tuple(outs)
```

```python
import functools

import jax
import jax.numpy as jnp
from jax import lax
from jax.experimental import pallas as pl
from jax.experimental.pallas import tpu as pltpu

F32 = jnp.float32
BF16 = jnp.bfloat16

D_MODEL = 2048
SEQ = 2048
N_MEM = 256
INNER = 4096
XA_HEADS = 4
XA_WIDTH = 1024
XA_DIM = 256
MIX_WIDTH = 3072
HEAD_DIM = 128
DN_V_HEADS = 24
DN_QK_HEADS = 12
DN_QK_WIDTH = 1536
DN_CHUNK = 64
DN_QKV = 2 * DN_QK_WIDTH + MIX_WIDTH
DN_PROJ = 11312
SB_HEADS = 24
SB_PROJ = 14336
EPS = 1e-6
N_DEV = 8
DN_SHARD = DN_PROJ // N_DEV
DN_SHARD_PAD = 1536
LANE = 128
DN_COLS = 90 * LANE
DN_AB_BLK, DN_PAD_BLK, DN_XQ_BLK, DN_Z_BLK = 48, 49, 50, 58
SB_XQ_BLK, SB_Z_BLK = 72, 80

ADAM_LR, ADAM_B1, ADAM_B2, ADAM_EPS, ADAM_WD, ADAM_STEP = 0.001, 0.9, 0.999, 1e-08, 0.01, 10

VMEM_LIMIT = 56 * 1024 * 1024
MESH = pl.DeviceIdType.MESH

_NN, _NT, _TN = "nn", "nt", "tn"


def _dims(mode, rank):
    lhs, rhs = {"nn": (1, 0), "nt": (1, 1), "tn": (0, 0)}[mode]
    if rank == 2:
        return (((lhs,), (rhs,)), ((), ()))
    return (((lhs + 1,), (rhs + 1,)), ((0,), (0,)))


def _params(*sem):
    return pltpu.CompilerParams(dimension_semantics=sem if sem else None, vmem_limit_bytes=VMEM_LIMIT)


def _dot(a, b, mode):
    return lax.dot_general(a, b, _dims(mode, a.ndim), preferred_element_type=F32)


def _dg(a, b, dims):
    return _dot(a.astype(BF16), b.astype(BF16), dims)


@jax.custom_vjp
def mm(a, b):
    return _dg(a, b, _NN)


@jax.custom_vjp
def mm_nt(a, b):
    return _dg(a, b, _NT)


@jax.custom_vjp
def mm_tn(a, b):
    return _dg(a, b, _TN)


mm.defvjp(lambda a, b: (_dg(a, b, _NN), (a, b)), lambda r, g: (mm_nt(g, r[1]), mm_tn(r[0], g)))
mm_nt.defvjp(lambda a, b: (_dg(a, b, _NT), (a, b)), lambda r, g: (mm(g, r[1]), mm_tn(g, r[0])))
mm_tn.defvjp(lambda a, b: (_dg(a, b, _TN), (a, b)), lambda r, g: (mm_nt(r[1], g), mm(r[0], g)))


def _split3(x):
    hi = x.astype(BF16)
    r1 = x - hi.astype(F32)
    mid = r1.astype(BF16)
    lo = (r1 - mid.astype(F32)).astype(BF16)
    return hi, mid, lo


def _dg3(a, b, dims):
    ah, am, _ = _split3(a)
    bh, bm, _ = _split3(b)
    return _dot(ah, bh, dims) + (_dot(ah, bm, dims) + _dot(am, bh, dims))


def _dg_exact_rhs(a, b01, dims):
    ah, am, _ = _split3(a)
    b = b01.astype(BF16)
    return _dot(ah, b, dims) + _dot(am, b, dims)


def _dg_exact_lhs(a01, b, dims):
    bh, bm, bl = _split3(b)
    a = a01.astype(BF16)
    return _dot(a, bh, dims) + (_dot(a, bm, dims) + _dot(a, bl, dims))


@jax.custom_vjp
def mm3(a, b):
    return _dg3(a, b, _NN)


@jax.custom_vjp
def mm3_nt(a, b):
    return _dg3(a, b, _NT)


@jax.custom_vjp
def mm3_tn(a, b):
    return _dg3(a, b, _TN)


mm3.defvjp(lambda a, b: (_dg3(a, b, _NN), (a, b)), lambda r, g: (mm3_nt(g, r[1]), mm3_tn(r[0], g)))
mm3_nt.defvjp(lambda a, b: (_dg3(a, b, _NT), (a, b)), lambda r, g: (mm3(g, r[1]), mm3_tn(g, r[0])))
mm3_tn.defvjp(lambda a, b: (_dg3(a, b, _TN), (a, b)), lambda r, g: (mm3_nt(r[1], g), mm3(r[0], g)))


def _softplus(x):
    return jnp.maximum(x, 0.0) + jnp.log(1.0 + jnp.exp(-jnp.abs(x)))


def _log_sigmoid(x):
    return jnp.minimum(x, 0.0) - jnp.log(1.0 + jnp.exp(-jnp.abs(x)))


def _sigmoid(x):
    return 1.0 / (1.0 + jnp.exp(-x))


def _silu(x):
    return x * _sigmoid(x)


def _silu_grad(x):
    s = _sigmoid(x)
    return s * (1.0 + x * (1.0 - s))


@jax.custom_vjp
def mm01(a01, b):
    return _dg_exact_lhs(a01, b, _NN)


mm01.defvjp(lambda a, b: (_dg_exact_lhs(a, b, _NN), a),
            lambda a, g: (jnp.zeros_like(a), _dg_exact_lhs(a, g, _TN)))


def _lane_pick(x, idx):
    lane = lax.broadcasted_iota(jnp.int32, x.shape, x.ndim - 1)
    return jnp.sum(jnp.where(lane == idx, x, 0.0), axis=-1, keepdims=True)


def _dn_chunk(qt, kt, v, ab, alog, dtb, h):
    B, C = qt.shape[0], DN_CHUNK
    g = -jnp.exp(_lane_pick(alog, h)) * _softplus(_lane_pick(ab, h) + _lane_pick(dtb, h))
    beta = _sigmoid(_lane_pick(ab, h + DN_V_HEADS))
    q = qt * lax.rsqrt(jnp.sum(qt * qt, axis=-1, keepdims=True) + EPS) * (HEAD_DIM ** -0.5)
    k = kt * lax.rsqrt(jnp.sum(kt * kt, axis=-1, keepdims=True) + EPS)
    row = lax.broadcasted_iota(jnp.int32, (B, C, C), 1)
    col = lax.broadcasted_iota(jnp.int32, (B, C, C), 2)
    lower = (row >= col).astype(F32)
    ones = jnp.ones((B, C, C), F32)
    g_wide = jnp.broadcast_to(g, (B, C, LANE))
    g_sq = jnp.broadcast_to(g, (B, C, C))
    gc = mm01(lower, g_wide)
    gc_i = gc[:, :, :C]
    gc_j = mm01(ones, jnp.where(row <= col, g_sq, 0.0))
    g_last = jnp.broadcast_to(gc[:, C - 1:C, :], (B, C, LANE))
    decay = jnp.exp(jnp.where(row >= col, gc_i - gc_j, -1e30))
    eg = jnp.exp(gc)
    kk = mm_nt(k, k)
    a_mat = jnp.where(row > col, jnp.broadcast_to(beta, (B, C, C)) * kk * decay, 0.0)
    eye = (row == col).astype(F32)
    y = -a_mat
    t = eye + y
    for _ in range(5):
        y = mm(y, y)
        t = t + mm(t, y)
    bb = jnp.broadcast_to(beta, (B, C, LANE))
    u0 = mm3(t, v * bb)
    w = mm3(t, k * (bb * eg))
    qk = mm_nt(q, k) * decay
    q_dec = q * eg
    k_dec = k * jnp.exp(g_last - gc)
    cd = jnp.exp(g_last)[:, :8, :]
    return u0, w, qk, q_dec, k_dec, cd


def _shift_rows(x, j, down):
    if j == 0:
        return x
    n = x.shape[0]
    r = lax.broadcasted_iota(jnp.int32, x.shape, 0)
    if down:
        return jnp.where(r >= j, pltpu.roll(x, j, 0), 0.0)
    return jnp.where(r < n - j, pltpu.roll(x, n - j, 0), 0.0)


def dn_conv_fwd(proj, conv_w):
    s = proj.shape[0]

    def body(x_ref, w_ref, o_ref):
        x = x_ref[...]
        w = w_ref[...]
        pre = x * w[3:4, :]
        for j in (1, 2, 3):
            pre = pre + _shift_rows(x, j, True) * w[3 - j:4 - j, :]
        o_ref[...] = _silu(pre)

    return pl.pallas_call(
        body, name="dn_conv_fwd", grid=(DN_QKV // LANE,),
        in_specs=[pl.BlockSpec((s, LANE), lambda j: (0, j)), pl.BlockSpec((8, LANE), lambda j: (0, j))],
        out_specs=pl.BlockSpec((s, LANE), lambda j: (0, j)),
        out_shape=jax.ShapeDtypeStruct((s, DN_QKV), F32), compiler_params=_params("parallel"),
    )(proj, conv_w)


def dn_conv_bwd(dact, proj, conv_w, dproj_in, dab, blk0, name):
    s = proj.shape[0]
    nblk = dact.shape[1] // LANE
    extra = 2 if dab is not None else 0

    def body(*refs):
        if dab is not None:
            da_ref, x_ref, w_ref, _, dab_ref, dp_ref, dw_ref = refs
        else:
            da_ref, x_ref, w_ref, _, dp_ref, dw_ref = refs
        j = pl.program_id(0)

        @pl.when(j < nblk)
        def _():
            x = x_ref[...]
            w = w_ref[...]
            xs = [_shift_rows(x, 3 - kk_, True) for kk_ in range(4)]
            pre = xs[0] * w[0:1, :]
            for kk_ in (1, 2, 3):
                pre = pre + xs[kk_] * w[kk_:kk_ + 1, :]
            dpre = da_ref[...] * _silu_grad(pre)
            dx = dpre * w[3:4, :]
            for jj in (1, 2, 3):
                dx = dx + _shift_rows(dpre, jj, False) * w[3 - jj:4 - jj, :]
            dp_ref[...] = dx
            rows = [jnp.sum(dpre * xs[kk_], axis=0, keepdims=True) for kk_ in range(4)]
            dw_ref[...] = jnp.concatenate(rows + [jnp.zeros((4, LANE), F32)], axis=0)

        if dab is not None:
            @pl.when(j == nblk)
            def _():
                dp_ref[...] = dab_ref[...]

            @pl.when(j == nblk + 1)
            def _():
                dp_ref[...] = jnp.zeros_like(dp_ref)

    cl = lambda j: jnp.minimum(j, nblk - 1)
    in_specs = [pl.BlockSpec((s, LANE), lambda j: (0, cl(j))),
                pl.BlockSpec((s, LANE), lambda j: (0, blk0 + cl(j))),
                pl.BlockSpec((8, LANE), lambda j: (0, blk0 + cl(j))),
                pl.BlockSpec(memory_space=pl.ANY)]
    args = [dact, proj, conv_w, dproj_in]
    if dab is not None:
        in_specs.append(pl.BlockSpec((s, LANE), lambda j: (0, 0)))
        args.append(dab)
    return pl.pallas_call(
        body, name=name, grid=(nblk + extra,), in_specs=in_specs,
        out_specs=[pl.BlockSpec((s, LANE), lambda j: (0, blk0 + j)), pl.BlockSpec((8, LANE), lambda j: (0, cl(j)))],
        out_shape=[jax.ShapeDtypeStruct(dproj_in.shape, F32), jax.ShapeDtypeStruct((8, dact.shape[1]), F32)],
        input_output_aliases={3: 0}, compiler_params=_params("arbitrary"),
    )(*args)


DN_GROUP = 8
DN_ROWS = DN_GROUP * DN_CHUNK


def dn_prep_fwd(act, proj, alog, dtb, exchange=None):
    s = act.shape[0]
    nc = s // DN_CHUNK
    C = DN_CHUNK

    def body(q_ref, k_ref, v_ref, ab_ref, al_ref, dt_ref, u0_ref, w_ref, qd_ref, kd_ref, qk_ref, cd_ref):
        qh = pl.program_id(0)
        al = al_ref[0:1, :]
        dt = dt_ref[0:1, :]
        chunks = lambda x: x.reshape(DN_GROUP, C, x.shape[-1])
        rows = lambda x: x.reshape(DN_ROWS, x.shape[-1])
        qt, kt, ab = chunks(q_ref[...]), chunks(k_ref[...]), chunks(ab_ref[...])
        for hv in range(2):
            cs = slice(hv * LANE, (hv + 1) * LANE)
            u0, w, qk, qd, kd, cd = _dn_chunk(qt, kt, chunks(v_ref[:, cs]), ab, al, dt, 2 * qh + hv)
            u0_ref[:, cs] = rows(u0)
            w_ref[:, cs] = rows(w)
            qd_ref[:, cs] = rows(qd)
            kd_ref[:, cs] = rows(kd)
            qk_ref[hv] = rows(qk)
            cd_ref[hv] = cd

    big = pl.BlockSpec((DN_ROWS, 2 * LANE), lambda h, g: (g, h))
    wide = jax.ShapeDtypeStruct((s, MIX_WIDTH), F32)
    return hosted_call(
        body, exchange, name="dn_prep_fwd", grid=(DN_QK_HEADS, s // DN_ROWS),
        in_specs=[pl.BlockSpec((DN_ROWS, LANE), lambda h, g: (g, h)),
                  pl.BlockSpec((DN_ROWS, LANE), lambda h, g: (g, DN_QK_HEADS + h)),
                  pl.BlockSpec((DN_ROWS, 2 * LANE), lambda h, g: (g, DN_QK_HEADS + h)),
                  pl.BlockSpec((DN_ROWS, LANE), lambda h, g: (g, DN_AB_BLK)),
                  pl.BlockSpec((8, LANE), lambda h, g: (0, 0)), pl.BlockSpec((8, LANE), lambda h, g: (0, 0))],
        out_specs=[big, big, big, big,
                   pl.BlockSpec((2, DN_ROWS, C), lambda h, g: (h, g, 0)),
                   pl.BlockSpec((2, DN_GROUP, 8, LANE), lambda h, g: (h, g, 0, 0))],
        out_shape=[wide, wide, wide, wide, jax.ShapeDtypeStruct((DN_V_HEADS, s, C), F32),
                   jax.ShapeDtypeStruct((DN_V_HEADS, nc, 8, LANE), F32)],
        sem=("parallel", "parallel"),
    )(act, act, act, proj, alog, dtb)


def dn_prep_bwd(act, proj, alog, dtb, du0, dw, dqd, dkd, dqk, dcd, exchange=None):
    s = act.shape[0]
    C = DN_CHUNK

    def body(q_ref, k_ref, v_ref, ab_ref, al_ref, dt_ref, du0_ref, dw_ref, dqd_ref, dkd_ref, dqk_ref, dcd_ref,
             dq_ref, dk_ref, dv_ref, dab_ref, dal_ref, ddt_ref):
        g_id = pl.program_id(0)
        qh = pl.program_id(1)
        al = al_ref[0:1, :]
        dt = dt_ref[0:1, :]

        @pl.when(qh == 0)
        def _():
            dab_ref[...] = jnp.zeros_like(dab_ref)

        @pl.when((qh == 0) & (g_id == 0))
        def _():
            dal_ref[...] = jnp.zeros_like(dal_ref)
            ddt_ref[...] = jnp.zeros_like(ddt_ref)

        chunks = lambda x: x.reshape(DN_GROUP, C, x.shape[-1])
        rows = lambda x: x.reshape(DN_ROWS, x.shape[-1])
        qt, kt, ab = chunks(q_ref[...]), chunks(k_ref[...]), chunks(ab_ref[...])
        dq_acc = jnp.zeros((DN_ROWS, LANE), F32)
        dk_acc = jnp.zeros((DN_ROWS, LANE), F32)
        for hv in range(2):
            cs = slice(hv * LANE, (hv + 1) * LANE)
            h = 2 * qh + hv
            f = lambda qt_, kt_, v_, ab_, a_, d_: _dn_chunk(qt_, kt_, v_, ab_, a_, d_, h)
            _, vjp = jax.vjp(f, qt, kt, chunks(v_ref[:, cs]), ab, al, dt)
            dq, dk, dv, dab, dal, ddt = vjp((chunks(du0_ref[:, cs]), chunks(dw_ref[:, cs]), chunks(dqk_ref[hv]),
                                             chunks(dqd_ref[:, cs]), chunks(dkd_ref[:, cs]), dcd_ref[hv]))
            dq_acc = dq_acc + rows(dq)
            dk_acc = dk_acc + rows(dk)
            dv_ref[:, cs] = rows(dv)
            dab_ref[...] += rows(dab)
            dal_ref[0:1, :] += dal
            ddt_ref[0:1, :] += ddt
        dq_ref[...] = dq_acc
        dk_ref[...] = dk_acc

    big = pl.BlockSpec((DN_ROWS, 2 * LANE), lambda g, h: (g, h))
    one = pl.BlockSpec((DN_ROWS, LANE), lambda g, h: (g, h))
    small = pl.BlockSpec((8, LANE), lambda g, h: (0, 0))
    return hosted_call(
        body, exchange, name="dn_prep_bwd", grid=(s // DN_ROWS, DN_QK_HEADS),
        in_specs=[one, pl.BlockSpec((DN_ROWS, LANE), lambda g, h: (g, DN_QK_HEADS + h)),
                  pl.BlockSpec((DN_ROWS, 2 * LANE), lambda g, h: (g, DN_QK_HEADS + h)),
                  pl.BlockSpec((DN_ROWS, LANE), lambda g, h: (g, DN_AB_BLK)), small, small,
                  big, big, big, big,
                  pl.BlockSpec((2, DN_ROWS, C), lambda g, h: (h, g, 0)),
                  pl.BlockSpec((2, DN_GROUP, 8, LANE), lambda g, h: (h, g, 0, 0))],
        out_specs=[one, one, big, pl.BlockSpec((DN_ROWS, LANE), lambda g, h: (g, 0)), small, small],
        out_shape=[jax.ShapeDtypeStruct((s, DN_QK_WIDTH), F32), jax.ShapeDtypeStruct((s, DN_QK_WIDTH), F32),
                   jax.ShapeDtypeStruct((s, MIX_WIDTH), F32), jax.ShapeDtypeStruct((s, LANE), F32),
                   jax.ShapeDtypeStruct((8, LANE), F32), jax.ShapeDtypeStruct((8, LANE), F32)],
        sem=("arbitrary", "arbitrary"),
    )(act, act, act, proj, alog, dtb, du0, dw, dqd, dkd, dqk, dcd)


def dn_scan_fwd(u0, w, qd, kd, qk, cd, width, exchange=None):
    s = u0.shape[0]
    nc = s // DN_CHUNK
    C = DN_CHUNK

    def body(u0_ref, w_ref, qd_ref, kd_ref, qk_ref, cd_ref, o_ref, st_ref):
        def step(c, state):
            rs = pl.ds(pl.multiple_of(c * C, C), C)
            st_ref[0, c] = state
            u = u0_ref[rs, :] - _dg(w_ref[rs, :], state, _NN)
            o_ref[rs, :] = _dg(qd_ref[rs, :], state, _NN) + _dg(qk_ref[0, rs, :], u, _NN)
            return cd_ref[0, c][0:1, :] * state + _dg(kd_ref[rs, :], u, _TN)

        lax.fori_loop(0, nc, step, jnp.zeros((HEAD_DIM, HEAD_DIM), F32))

    col = pl.BlockSpec((s, LANE), lambda h: (0, h))
    return hosted_call(
        body, exchange, name="dn_scan_fwd", grid=(DN_V_HEADS,),
        in_specs=[col, col, col, col, pl.BlockSpec((1, s, C), lambda h: (h, 0, 0)),
                  pl.BlockSpec((1, nc, 8, LANE), lambda h: (h, 0, 0, 0))],
        out_specs=[col, pl.BlockSpec((1, nc, HEAD_DIM, HEAD_DIM), lambda h: (h, 0, 0, 0))],
        out_shape=[jax.ShapeDtypeStruct((s, width), F32),
                   jax.ShapeDtypeStruct((DN_V_HEADS, nc, HEAD_DIM, HEAD_DIM), F32)],
        sem=("parallel",),
    )(u0, w, qd, kd, qk, cd)


def dn_scan_bwd(do, u0, w, qd, kd, qk, cd, states, exchange=None):
    s = u0.shape[0]
    nc = s // DN_CHUNK
    C = DN_CHUNK

    def body(do_ref, u0_ref, w_ref, qd_ref, kd_ref, qk_ref, cd_ref, st_ref,
             du0_ref, dw_ref, dqd_ref, dkd_ref, dqk_ref, dcd_ref):
        def step(i, dstate):
            c = nc - 1 - i
            rs = pl.ds(pl.multiple_of(c * C, C), C)
            state = st_ref[0, c]
            g = do_ref[rs, :]
            w_c = w_ref[rs, :]
            kd_c = kd_ref[rs, :]
            qd_c = qd_ref[rs, :]
            qk_c = qk_ref[0, rs, :]
            cd_row = cd_ref[0, c][0:1, :]
            u = u0_ref[rs, :] - _dg(w_c, state, _NN)
            du = _dg(qk_c, g, _TN) + _dg(kd_c, dstate, _NN)
            du0_ref[rs, :] = du
            dw_ref[rs, :] = -_dg(du, state, _NT)
            dqd_ref[rs, :] = _dg(g, state, _NT)
            dkd_ref[rs, :] = _dg(u, dstate, _NT)
            dqk_ref[0, rs, :] = _dg(g, u, _NT)
            dcd_row = jnp.sum(state * dstate, axis=0, keepdims=True)
            dcd_ref[0, c] = jnp.concatenate([dcd_row, jnp.zeros((7, LANE), F32)], axis=0)
            return cd_row * dstate + _dg(qd_c, g, _TN) - _dg(w_c, du, _TN)

        lax.fori_loop(0, nc, step, jnp.zeros((HEAD_DIM, HEAD_DIM), F32))

    col = pl.BlockSpec((s, LANE), lambda h: (0, h))
    qk_spec = pl.BlockSpec((1, s, C), lambda h: (h, 0, 0))
    cd_spec = pl.BlockSpec((1, nc, 8, LANE), lambda h: (h, 0, 0, 0))
    wide = jax.ShapeDtypeStruct((s, MIX_WIDTH), F32)
    return hosted_call(
        body, exchange, name="dn_scan_bwd", grid=(DN_V_HEADS,),
        in_specs=[col, col, col, col, col, qk_spec, cd_spec,
                  pl.BlockSpec((1, nc, HEAD_DIM, HEAD_DIM), lambda h: (h, 0, 0, 0))],
        out_specs=[col, col, col, col, qk_spec, cd_spec],
        out_shape=[wide, wide, wide, wide, jax.ShapeDtypeStruct((DN_V_HEADS, s, C), F32),
                   jax.ShapeDtypeStruct((DN_V_HEADS, nc, 8, LANE), F32)],
        sem=("parallel",),
    )(do, u0, w, qd, kd, qk, cd, states)


SB_T = 256


def _sb_scores(q, kb, i, j):
    z = _dg(q, kb, _NT) * (HEAD_DIM ** -0.5)
    row = lax.broadcasted_iota(jnp.int32, (SB_T, SB_T), 0)
    col = lax.broadcasted_iota(jnp.int32, (SB_T, SB_T), 1)
    mask = (j * SB_T + col) < (i * SB_T + row)
    ls = _log_sigmoid(z)
    return mask, ls, jnp.where(mask, ls - z, 0.0)


def sb_fwd(qkn, proj, v_blk0, width):
    s = qkn.shape[0]

    def body(q_ref, k_ref, v_ref, o_ref, lt_ref):
        i = pl.program_id(1)
        q = q_ref[...]
        row = lax.broadcasted_iota(jnp.int32, (SB_T, SB_T), 0)
        col = lax.broadcasted_iota(jnp.int32, (SB_T, SB_T), 1)
        after = (row > col).astype(BF16)

        def step(jj, carry):
            run, acc = carry
            j = i - jj
            ks = pl.ds(pl.multiple_of(j * SB_T, SB_T), SB_T)
            mask, ls, lr = _sb_scores(q, k_ref[ks, :], i, j)
            later = _dg_exact_rhs(lr, after, _NN) + run
            wts = jnp.where(mask, jnp.exp(ls + later), 0.0)
            return run + jnp.sum(lr, axis=1, keepdims=True), acc + _dg(wts, v_ref[ks, :], _NN)

        run, acc = lax.fori_loop(0, i + 1, step, (jnp.zeros((SB_T, 1), F32), jnp.zeros((SB_T, HEAD_DIM), F32)))
        o_ref[...] = acc
        lt_ref[0] = run

    return pl.pallas_call(
        body, name="sb_fwd", grid=(SB_HEADS, s // SB_T),
        in_specs=[pl.BlockSpec((SB_T, LANE), lambda h, i: (i, h)),
                  pl.BlockSpec((s, LANE), lambda h, i: (0, SB_HEADS + h)),
                  pl.BlockSpec((s, LANE), lambda h, i: (0, v_blk0 + h))],
        out_specs=[pl.BlockSpec((SB_T, LANE), lambda h, i: (i, h)), pl.BlockSpec((1, SB_T, 1), lambda h, i: (h, i, 0))],
        out_shape=[jax.ShapeDtypeStruct((s, width), F32), jax.ShapeDtypeStruct((SB_HEADS, s, 1), F32)],
        compiler_params=_params("parallel", "parallel"),
    )(qkn, qkn, proj)


def sb_bwd(qkn, proj, v_blk0, do, ltot, dproj_in):
    s = qkn.shape[0]

    def body(q_ref, k_ref, v_ref, do_ref, lt_ref, _, dq_ref, dk_ref, dv_ref):
        i = pl.program_id(1)

        @pl.when(i == 0)
        def _():
            dk_ref[...] = jnp.zeros_like(dk_ref)
            dv_ref[...] = jnp.zeros_like(dv_ref)

        q = q_ref[...]
        g = do_ref[...]
        ltot = lt_ref[0]
        row = lax.broadcasted_iota(jnp.int32, (SB_T, SB_T), 0)
        col = lax.broadcasted_iota(jnp.int32, (SB_T, SB_T), 1)
        upto = (row <= col).astype(BF16)
        before = (row < col).astype(BF16)

        def step(j, carry):
            plr, pdl, dq = carry
            ks = pl.ds(pl.multiple_of(j * SB_T, SB_T), SB_T)
            kb = k_ref[ks, :]
            vb = v_ref[ks, :]
            mask, ls, lr = _sb_scores(q, kb, i, j)
            later = ltot - (_dg_exact_rhs(lr, upto, _NN) + plr)
            wts = jnp.where(mask, jnp.exp(ls + later), 0.0)
            dl = _dg(g, vb, _NT) * wts
            pre = _dg_exact_rhs(dl, before, _NN) + pdl
            sz = jnp.exp(ls)
            dz = jnp.where(mask, dl * (1.0 - sz) - sz * pre, 0.0) * (HEAD_DIM ** -0.5)
            dk_ref[ks, :] += _dg(dz, q, _TN)
            dv_ref[ks, :] += _dg(wts, g, _TN)
            return (plr + jnp.sum(lr, axis=1, keepdims=True), pdl + jnp.sum(dl, axis=1, keepdims=True),
                    dq + _dg(dz, kb, _NN))

        zero = jnp.zeros((SB_T, 1), F32)
        _, _, dq = lax.fori_loop(0, i + 1, step, (zero, zero, jnp.zeros((SB_T, HEAD_DIM), F32)))
        dq_ref[...] = dq

    tile = pl.BlockSpec((SB_T, LANE), lambda h, i: (i, h))
    colspec = pl.BlockSpec((s, LANE), lambda h, i: (0, h))
    return pl.pallas_call(
        body, name="sb_bwd", grid=(SB_HEADS, s // SB_T),
        in_specs=[tile, pl.BlockSpec((s, LANE), lambda h, i: (0, SB_HEADS + h)),
                  pl.BlockSpec((s, LANE), lambda h, i: (0, v_blk0 + h)), tile,
                  pl.BlockSpec((1, SB_T, 1), lambda h, i: (h, i, 0)), pl.BlockSpec(memory_space=pl.ANY)],
        out_specs=[tile, colspec, pl.BlockSpec((s, LANE), lambda h, i: (0, v_blk0 + h))],
        out_shape=[jax.ShapeDtypeStruct((s, MIX_WIDTH), F32), jax.ShapeDtypeStruct((s, MIX_WIDTH), F32),
                   jax.ShapeDtypeStruct(dproj_in.shape, F32)],
        input_output_aliases={5: 2}, compiler_params=_params("parallel", "arbitrary"),
    )(qkn, qkn, proj, do, ltot, dproj_in)


ROW_TILE = 256
HN_HEADS = 8


def head_norm_fwd(x, x_blk0, nblk, gains, out_dtype, out_width, name):
    s = x.shape[0]
    assert x_blk0 % HN_HEADS == 0 and nblk % (HN_HEADS * gains.shape[0]) == 0
    per = nblk // gains.shape[0] // HN_HEADS
    w = HN_HEADS * LANE

    def body(x_ref, g_ref, o_ref):
        gain = g_ref[0, 0:1, :]
        for j in range(HN_HEADS):
            cs = slice(j * LANE, (j + 1) * LANE)
            xv = x_ref[:, cs]
            r = lax.rsqrt(jnp.mean(xv * xv, axis=1, keepdims=True) + EPS)
            o_ref[:, cs] = (xv * r * gain).astype(out_dtype)

    return pl.pallas_call(
        body, name=name, grid=(nblk // HN_HEADS, s // ROW_TILE),
        in_specs=[pl.BlockSpec((ROW_TILE, w), lambda j, t: (t, x_blk0 // HN_HEADS + j)),
                  pl.BlockSpec((1, 8, LANE), lambda j, t: (j // per, 0, 0))],
        out_specs=pl.BlockSpec((ROW_TILE, w), lambda j, t: (t, j)),
        out_shape=jax.ShapeDtypeStruct((s, out_width), out_dtype), compiler_params=_params("parallel", "parallel"),
    )(x, gains)


def head_norm_bwd(dy, dy_blk0, x, x_blk0, nblk, gain, dst, dst_blk0, name):
    s = x.shape[0]

    def body(*refs):
        if dst is not None:
            dy_ref, x_ref, g_ref, _, dx_ref, dg_ref = refs
        else:
            dy_ref, x_ref, g_ref, dx_ref, dg_ref = refs

        @pl.when((pl.program_id(0) == 0) & (pl.program_id(1) == 0))
        def _():
            dg_ref[...] = jnp.zeros_like(dg_ref)

        gain = g_ref[0:1, :]
        dg = jnp.zeros((1, LANE), F32)
        for j in range(HN_HEADS):
            cs = slice(j * LANE, (j + 1) * LANE)
            xv = x_ref[:, cs]
            g = dy_ref[:, cs]
            r = lax.rsqrt(jnp.mean(xv * xv, axis=1, keepdims=True) + EPS)
            gy = g * gain
            dx_ref[:, cs] = r * gy - xv * (r * r * r) * jnp.mean(gy * xv, axis=1, keepdims=True)
            dg = dg + jnp.sum(g * xv * r, axis=0, keepdims=True)
        dg_ref[0:1, :] += dg

    assert dy_blk0 % HN_HEADS == 0 and x_blk0 % HN_HEADS == 0 and dst_blk0 % HN_HEADS == 0 and nblk % HN_HEADS == 0
    w = HN_HEADS * LANE
    in_specs = [pl.BlockSpec((ROW_TILE, w), lambda j, t: (t, dy_blk0 // HN_HEADS + j)),
                pl.BlockSpec((ROW_TILE, w), lambda j, t: (t, x_blk0 // HN_HEADS + j)),
                pl.BlockSpec((8, LANE), lambda j, t: (0, 0))]
    args = [dy, x, gain]
    aliases = {}
    if dst is not None:
        in_specs.append(pl.BlockSpec(memory_space=pl.ANY))
        args.append(dst)
        aliases = {3: 0}
        out0 = jax.ShapeDtypeStruct(dst.shape, F32)
    else:
        out0 = jax.ShapeDtypeStruct((s, (dst_blk0 + nblk) * LANE), F32)
    return pl.pallas_call(
        body, name=name, grid=(nblk // HN_HEADS, s // ROW_TILE), in_specs=in_specs,
        out_specs=[pl.BlockSpec((ROW_TILE, w), lambda j, t: (t, dst_blk0 // HN_HEADS + j)),
                   pl.BlockSpec((8, LANE), lambda j, t: (0, 0))],
        out_shape=[out0, jax.ShapeDtypeStruct((8, LANE), F32)],
        input_output_aliases=aliases, compiler_params=_params("arbitrary", "arbitrary"),
    )(*args)


def _xa_head(xq, kraw, v, qg, kg):
    q = xq * lax.rsqrt(jnp.mean(xq * xq, axis=1, keepdims=True) + EPS) * qg
    k = kraw * lax.rsqrt(jnp.mean(kraw * kraw, axis=1, keepdims=True) + EPS) * kg
    sc = mm_nt(q, k) * (XA_DIM ** -0.5)
    e = jnp.exp(sc - lax.stop_gradient(jnp.max(sc, axis=1, keepdims=True)))
    return mm(e / jnp.sum(e, axis=1, keepdims=True), v)


def xa_fwd(proj, xq_blk0, kv, qg, kg, cat):
    s = proj.shape[0]
    n_mem = kv.shape[0]

    def body(xq_ref, k_ref, v_ref, qg_ref, kg_ref, _, o_ref):
        o_ref[...] = _xa_head(xq_ref[...], k_ref[...], v_ref[...], qg_ref[0:1, :], kg_ref[0:1, :])

    gain = pl.BlockSpec((8, XA_DIM), lambda h, t: (0, 0))
    return pl.pallas_call(
        body, name="xa_fwd", grid=(XA_HEADS, s // ROW_TILE),
        in_specs=[pl.BlockSpec((ROW_TILE, XA_DIM), lambda h, t: (t, xq_blk0 // 2 + h)),
                  pl.BlockSpec((n_mem, XA_DIM), lambda h, t: (0, h)),
                  pl.BlockSpec((n_mem, XA_DIM), lambda h, t: (0, XA_HEADS + h)), gain, gain,
                  pl.BlockSpec(memory_space=pl.ANY)],
        out_specs=pl.BlockSpec((ROW_TILE, XA_DIM), lambda h, t: (t, MIX_WIDTH // XA_DIM + h)),
        out_shape=jax.ShapeDtypeStruct(cat.shape, F32), input_output_aliases={5: 0},
        compiler_params=_params("parallel", "parallel"),
    )(proj, kv, kv, qg, kg, cat)


def xa_bwd(proj, xq_blk0, kv, qg, kg, dcat, dproj_in):
    s = proj.shape[0]
    n_mem = kv.shape[0]

    def body(xq_ref, k_ref, v_ref, qg_ref, kg_ref, do_ref, _, dxq_ref, dk_ref, dv_ref, dqg_ref, dkg_ref):
        h = pl.program_id(0)
        t = pl.program_id(1)

        @pl.when(t == 0)
        def _():
            dk_ref[...] = jnp.zeros_like(dk_ref)
            dv_ref[...] = jnp.zeros_like(dv_ref)

        @pl.when((t == 0) & (h == 0))
        def _():
            dqg_ref[...] = jnp.zeros_like(dqg_ref)
            dkg_ref[...] = jnp.zeros_like(dkg_ref)

        _, vjp = jax.vjp(_xa_head, xq_ref[...], k_ref[...], v_ref[...], qg_ref[0:1, :], kg_ref[0:1, :])
        dxq, dk, dv, dqg, dkg = vjp(do_ref[...])
        dxq_ref[...] = dxq
        dk_ref[...] += dk
        dv_ref[...] += dv
        dqg_ref[0:1, :] += dqg
        dkg_ref[0:1, :] += dkg

    gain = pl.BlockSpec((8, XA_DIM), lambda h, t: (0, 0))
    kspec = pl.BlockSpec((n_mem, XA_DIM), lambda h, t: (0, h))
    vspec = pl.BlockSpec((n_mem, XA_DIM), lambda h, t: (0, XA_HEADS + h))
    return pl.pallas_call(
        body, name="xa_bwd", grid=(XA_HEADS, s // ROW_TILE),
        in_specs=[pl.BlockSpec((ROW_TILE, XA_DIM), lambda h, t: (t, xq_blk0 // 2 + h)), kspec, vspec, gain, gain,
                  pl.BlockSpec((ROW_TILE, XA_DIM), lambda h, t: (t, MIX_WIDTH // XA_DIM + h)),
                  pl.BlockSpec(memory_space=pl.ANY)],
        out_specs=[pl.BlockSpec((ROW_TILE, XA_DIM), lambda h, t: (t, xq_blk0 // 2 + h)), kspec, kspec, gain, gain],
        out_shape=[jax.ShapeDtypeStruct(dproj_in.shape, F32), jax.ShapeDtypeStruct((n_mem, XA_WIDTH), F32),
                   jax.ShapeDtypeStruct((n_mem, XA_WIDTH), F32), jax.ShapeDtypeStruct((8, XA_DIM), F32),
                   jax.ShapeDtypeStruct((8, XA_DIM), F32)],
        input_output_aliases={6: 0}, compiler_params=_params("arbitrary", "arbitrary"),
    )(proj, kv, kv, qg, kg, dcat, dproj_in)


GATE_ROWS = 1024


def gate_fwd(cat, proj, z_blk0):
    s = cat.shape[0]

    def body(c_ref, z_ref, y_ref):
        y_ref[...] = (c_ref[...] * _silu(z_ref[...])).astype(BF16)

    w = 2 * LANE
    rt = min(GATE_ROWS, s)
    return pl.pallas_call(
        body, name="gate_fwd", grid=(INNER // w, s // rt),
        in_specs=[pl.BlockSpec((rt, w), lambda j, t: (t, j)),
                  pl.BlockSpec((rt, w), lambda j, t: (t, z_blk0 // 2 + j))],
        out_specs=pl.BlockSpec((rt, w), lambda j, t: (t, j)),
        out_shape=jax.ShapeDtypeStruct((s, INNER), BF16), compiler_params=_params("parallel", "parallel"),
    )(cat, proj)


def gate_bwd(dy, cat, proj, z_blk0):
    s = cat.shape[0]

    def body(dy_ref, c_ref, z_ref, dc_ref, dz_ref):
        z = z_ref[...]
        g = dy_ref[...]
        dc_ref[...] = g * _silu(z)
        dz_ref[...] = g * c_ref[...] * _silu_grad(z)

    w = 2 * LANE
    rt = min(GATE_ROWS, s)
    tile = pl.BlockSpec((rt, w), lambda j, t: (t, j))
    ztile = pl.BlockSpec((rt, w), lambda j, t: (t, z_blk0 // 2 + j))
    return pl.pallas_call(
        body, name="gate_bwd", grid=(INNER // w, s // rt), in_specs=[tile, tile, ztile],
        out_specs=[tile, ztile],
        out_shape=[jax.ShapeDtypeStruct((s, INNER), F32), jax.ShapeDtypeStruct(proj.shape, F32)],
        compiler_params=_params("parallel", "parallel"),
    )(dy, cat, proj)


NORM_ROWS = 256


def rms_fwd(x, gain):
    s, d = x.shape

    def body(x_ref, g_ref, o_ref):
        xv = x_ref[...]
        r = lax.rsqrt(jnp.mean(xv * xv, axis=1, keepdims=True) + EPS)
        o_ref[...] = (xv * r * g_ref[0:1, :]).astype(BF16)

    return pl.pallas_call(
        body, name="rms_fwd", grid=(s // NORM_ROWS,),
        in_specs=[pl.BlockSpec((NORM_ROWS, d), lambda t: (t, 0)), pl.BlockSpec((8, d), lambda t: (0, 0))],
        out_specs=pl.BlockSpec((NORM_ROWS, d), lambda t: (t, 0)),
        out_shape=jax.ShapeDtypeStruct((s, d), BF16), compiler_params=_params("parallel"),
    )(x, gain)


def rms_bwd(dh, x, gain, dres):
    s, d = x.shape

    def body(*refs):
        if dres is not None:
            dh_ref, x_ref, g_ref, dr_ref, dx_ref, dg_ref = refs
        else:
            dh_ref, x_ref, g_ref, dx_ref, dg_ref = refs

        @pl.when(pl.program_id(0) == 0)
        def _():
            dg_ref[...] = jnp.zeros_like(dg_ref)

        xv = x_ref[...]
        g = dh_ref[...]
        r = lax.rsqrt(jnp.mean(xv * xv, axis=1, keepdims=True) + EPS)
        gy = g * g_ref[0:1, :]
        dx = r * gy - xv * (r * r * r) * jnp.mean(gy * xv, axis=1, keepdims=True)
        dx_ref[...] = dx + dr_ref[...] if dres is not None else dx
        dg_ref[0:1, :] += jnp.sum(g * xv * r, axis=0, keepdims=True)

    tile = pl.BlockSpec((NORM_ROWS, d), lambda t: (t, 0))
    gspec = pl.BlockSpec((8, d), lambda t: (0, 0))
    args = [dh, x, gain] + ([dres] if dres is not None else [])
    return pl.pallas_call(
        body, name="rms_bwd", grid=(s // NORM_ROWS,),
        in_specs=[tile, tile, gspec] + ([tile] if dres is not None else []),
        out_specs=[tile, gspec],
        out_shape=[jax.ShapeDtypeStruct((s, d), F32), jax.ShapeDtypeStruct((8, d), F32)],
        compiler_params=_params("arbitrary"),
    )(*args)


def loss_fwd_bwd(y, target):
    s, d = y.shape

    def body(y_ref, t_ref, l_ref, dy_ref):
        @pl.when(pl.program_id(0) == 0)
        def _():
            l_ref[...] = jnp.zeros_like(l_ref)

        err = y_ref[...] - t_ref[...]
        dy_ref[...] = err * (1.0 / d)
        part = 0.5 * jnp.sum(jnp.mean(err * err, axis=1, keepdims=True), axis=0, keepdims=True)
        r = lax.broadcasted_iota(jnp.int32, (8, LANE), 0)
        c = lax.broadcasted_iota(jnp.int32, (8, LANE), 1)
        l_ref[...] += jnp.where((r == 0) & (c == 0), part, 0.0)

    tile = pl.BlockSpec((NORM_ROWS, d), lambda t: (t, 0))
    return pl.pallas_call(
        body, name="loss", grid=(s // NORM_ROWS,), in_specs=[tile, tile],
        out_specs=[pl.BlockSpec((8, LANE), lambda t: (0, 0)), tile],
        out_shape=[jax.ShapeDtypeStruct((8, LANE), F32), jax.ShapeDtypeStruct((s, d), F32)],
        compiler_params=_params("arbitrary"),
    )(y, target)


def matmul(a, b, mode, out_dtype, tm, tn, tk, name, add=None, b_blocked=False, out_blocks=None, exchange=None):
    if b_blocked:
        nb, _, width = b.shape
        bshape = (b.shape[1], nb * width)
    else:
        bshape = b.shape
    if mode == "tn":
        (kdim, m), n = a.shape, bshape[1]
    else:
        (m, kdim), n = a.shape, (bshape[1] if mode == "nn" else bshape[0])
    tm, tn, tk = min(tm, m), min(tn, n), min(tk, kdim)
    assert m % tm == 0 and n % tn == 0 and kdim % tk == 0, (name, m, n, kdim)
    nk = kdim // tk
    dims = {"nn": _NN, "nt": _NT, "tn": _TN}[mode]

    def body(*refs):
        if add is not None:
            a_ref, b_ref, add_ref, o_ref, acc_ref = refs
        else:
            a_ref, b_ref, o_ref, acc_ref = refs
        k = pl.program_id(2)

        @pl.when(k == 0)
        def _():
            acc_ref[...] = jnp.zeros_like(acc_ref)

        acc_ref[...] += _dg(a_ref[...], b_ref[...], dims)

        @pl.when(k == nk - 1)
        def _():
            r = acc_ref[...]
            if add is not None:
                r = r + add_ref[...]
            o_ref[...] = r.astype(out_dtype)

    a_spec = pl.BlockSpec((tk, tm), lambda i, j, k: (k, i)) if mode == "tn" else pl.BlockSpec((tm, tk), lambda i, j, k: (i, k))
    if b_blocked and mode == "nn":
        per = width // tn
        assert width % tn == 0
        b_spec = pl.BlockSpec((None, tk, tn), lambda i, j, k: (j // per, k, j % per))
    elif b_blocked and mode == "nt":
        per = width // tk
        assert width % tk == 0
        b_spec = pl.BlockSpec((None, tn, tk), lambda i, j, k: (k // per, j, k % per))
    elif mode == "nt":
        b_spec = pl.BlockSpec((tn, tk), lambda i, j, k: (j, k))
    else:
        assert not b_blocked
        b_spec = pl.BlockSpec((tk, tn), lambda i, j, k: (k, j))
    add_spec = pl.BlockSpec((tm, tn), lambda i, j, k: (i, j))
    if out_blocks is not None:
        operb = (n // out_blocks) // tn
        assert (n // out_blocks) % tn == 0 and add is None
        o_spec = pl.BlockSpec((None, tm, tn), lambda i, j, k: (j // operb, i, j % operb))
        out_shape = jax.ShapeDtypeStruct((out_blocks, m, n // out_blocks), out_dtype)
    else:
        o_spec = add_spec
        out_shape = jax.ShapeDtypeStruct((m, n), out_dtype)
    res = hosted_call(
        body, exchange, name=name, grid=(m // tm, n // tn, nk),
        in_specs=[a_spec, b_spec] + ([add_spec] if add is not None else []), out_specs=[o_spec],
        out_shape=[out_shape], scratch_shapes=[pltpu.VMEM((tm, tn), F32)],
        sem=("parallel", "parallel", "arbitrary"),
    )(*([a, b] + ([add] if add is not None else [])))
    return res[0] if exchange is None else res


_HBM = pl.BlockSpec(memory_space=pltpu.HBM)


def _me():
    return lax.axis_index("x"), lax.axis_index("y"), lax.axis_index("c")


def _flat(p):
    return 4 * p[0] + 2 * p[1] + p[2]


def _flip(p, r):
    return tuple((1 - v) if (r >> (2 - a)) & 1 else v for a, v in enumerate(p))


class Exchange:
    def __init__(self, srcs, out_shapes, sems, start, finish):
        self.srcs, self.out_shapes, self.sems = list(srcs), list(out_shapes), list(sems)
        self.start, self.finish = start, finish


def hosted_call(body, exchange, *, name, grid, in_specs, out_specs, out_shape, scratch_shapes=(),
                input_output_aliases=None, sem=()):
    in_specs, out_specs, out_shape = list(in_specs), list(out_specs), list(out_shape)
    scratch_shapes = list(scratch_shapes)
    aliases = input_output_aliases or {}
    if exchange is None:
        call = pl.pallas_call(body, name=name, grid=grid, in_specs=in_specs, out_specs=out_specs, out_shape=out_shape,
                              scratch_shapes=scratch_shapes, input_output_aliases=aliases, compiler_params=_params(*sem))
        return lambda *args: list(call(*args))
    ni, no, ns = len(in_specs), len(out_specs), len(scratch_shapes)
    xi, xo = len(exchange.srcs), len(exchange.out_shapes)

    def wrapped(*refs):
        ins, refs = refs[:ni], refs[ni:]
        xin, refs = refs[:xi], refs[xi:]
        outs, refs = refs[:no], refs[no:]
        xout, refs = refs[:xo], refs[xo:]
        scr, xsem = refs[:ns], refs[ns:]
        first = functools.reduce(lambda p, q: p & q, [pl.program_id(d) == 0 for d in range(len(grid))])
        last = functools.reduce(lambda p, q: p & q, [pl.program_id(d) == grid[d] - 1 for d in range(len(grid))])

        @pl.when(first)
        def _():
            exchange.start(xin, xout, xsem)

        body(*ins, *outs, *scr)

        @pl.when(last)
        def _():
            exchange.finish(xin, xout, xsem)

    call = pl.pallas_call(
        wrapped, name=name, grid=grid, in_specs=in_specs + [_HBM] * xi, out_specs=out_specs + [_HBM] * xo,
        out_shape=out_shape + exchange.out_shapes, scratch_shapes=scratch_shapes + exchange.sems,
        input_output_aliases=aliases, compiler_params=_params(*(("arbitrary",) * len(grid))))
    return lambda *args: list(call(*args, *exchange.srcs))


def run_exchange(exchange, name):
    xi, xo = len(exchange.srcs), len(exchange.out_shapes)

    def body(*refs):
        exchange.start(refs[:xi], refs[xi:xi + xo], refs[xi + xo:])
        exchange.finish(refs[:xi], refs[xi:xi + xo], refs[xi + xo:])

    return list(pl.pallas_call(body, name=name, in_specs=[_HBM] * xi, out_specs=[_HBM] * xo,
                               out_shape=exchange.out_shapes, scratch_shapes=exchange.sems)(*exchange.srcs))


def gather_exchange(shards):
    n = len(shards)

    def parts(srcs, outs, sems):
        send_sems, recv_sems, local_sems = sems
        me = _me()
        x, y, c = me
        chips = [(1 - x, y), (x, 1 - y), (1 - x, 1 - y)]

        def copy(a, k, block, to, src=None):
            dst = outs[a].at[_flat(block)]
            return pltpu.make_async_remote_copy(src_ref=dst if src is None else src, dst_ref=dst,
                                                send_sem=send_sems.at[a, k], recv_sem=recv_sems.at[a, k],
                                                device_id=to, device_id_type=MESH)

        mine = [pltpu.make_async_copy(srcs[a], outs[a].at[_flat(me)], local_sems.at[a]) for a in range(n)]
        own = []
        for a in range(n):
            own.append(copy(a, 0, me, (x, y, 1 - c), src=srcs[a]))
            own += [copy(a, 1 + j, me, (*chip, c), src=srcs[a]) for j, chip in enumerate(chips)]
        return me, chips, copy, mine, own

    def start(srcs, outs, sems):
        _, _, _, mine, own = parts(srcs, outs, sems)
        for cp in mine + own:
            cp.start()

    def finish(srcs, outs, sems):
        me, chips, copy, mine, own = parts(srcs, outs, sems)
        x, y, c = me
        passed = []
        for j, chip in enumerate(chips):
            for a in range(n):
                copy(a, 1 + j, (*chip, c), me).wait_recv()
                fwd = copy(a, 4 + j, (*chip, c), (x, y, 1 - c))
                fwd.start()
                passed.append(fwd)
        for a in range(n):
            copy(a, 0, (x, y, 1 - c), me).wait_recv()
            for j, chip in enumerate(chips):
                copy(a, 4 + j, (*chip, 1 - c), me).wait_recv()
        for cp in own + passed:
            cp.wait_send()
        for cp in mine:
            cp.wait()

    dma = pltpu.SemaphoreType.DMA
    return Exchange(shards, [jax.ShapeDtypeStruct((N_DEV,) + s.shape, s.dtype) for s in shards],
                    [dma((n, 7)), dma((n, 7)), dma((n,))], start, finish)


def pair_exchange(srcs):
    n = len(srcs)

    def copies(srcs_, outs, sems):
        send_sems, recv_sems = sems
        x, y, c = _me()
        return [pltpu.make_async_remote_copy(src_ref=srcs_[a].at[:, 1 - c], dst_ref=outs[a], send_sem=send_sems.at[a],
                                             recv_sem=recv_sems.at[a], device_id=(x, y, 1 - c), device_id_type=MESH)
                for a in range(n)]

    def start(srcs_, outs, sems):
        for cp in copies(srcs_, outs, sems):
            cp.start()

    def finish(srcs_, outs, sems):
        for cp in copies(srcs_, outs, sems):
            cp.wait()

    dma = pltpu.SemaphoreType.DMA
    return Exchange(srcs, [jax.ShapeDtypeStruct((4,) + s.shape[2:], s.dtype) for s in srcs], [dma((n,)), dma((n,))],
                    start, finish)


def pair_sum(src, half, name):
    _, _, rows, cols = src.shape
    tr = min(rows, 256)

    def body(x_ref, h_ref, o_ref):
        c = lax.axis_index("c")
        o_ref[0] = (x_ref[0, c].astype(F32) + h_ref[0].astype(F32)).astype(BF16)

    return pl.pallas_call(
        body, name=name, grid=(4, rows // tr),
        in_specs=[pl.BlockSpec((1, 2, tr, cols), lambda ch, t: (ch, 0, t, 0)),
                  pl.BlockSpec((1, tr, cols), lambda ch, t: (ch, t, 0))],
        out_specs=pl.BlockSpec((1, tr, cols), lambda ch, t: (ch, t, 0)),
        out_shape=jax.ShapeDtypeStruct(half.shape, BF16), compiler_params=_params("parallel", "parallel"),
    )(src, half)


def chip_exchange(parts, slots, recv_shapes):
    n = len(parts)

    def plan(srcs, outs, sems):
        send_sems, recv_sems, local_sems = sems
        x, y, c = _me()
        chip = 2 * x + y
        mine = [pltpu.make_async_copy(srcs[a].at[chip], outs[slots[a][0]].at[chip, slots[a][1]], local_sems.at[a])
                for a in range(n)]
        sends, arrivals = [], []
        for r in (1, 2, 3):
            px = (1 - x) if r & 2 else x
            py = (1 - y) if r & 1 else y
            for a in range(n):
                ri, layer = slots[a]
                sends.append(pltpu.make_async_remote_copy(
                    src_ref=srcs[a].at[2 * px + py], dst_ref=outs[ri].at[chip, layer], send_sem=send_sems.at[a, r - 1],
                    recv_sem=recv_sems.at[a, r - 1], device_id=(px, py, c), device_id_type=MESH))
                land = outs[ri].at[2 * px + py, layer]
                arrivals.append(pltpu.make_async_remote_copy(
                    src_ref=land, dst_ref=land, send_sem=send_sems.at[a, r - 1], recv_sem=recv_sems.at[a, r - 1],
                    device_id=(px, py, c), device_id_type=MESH))
        return mine, sends, arrivals

    def start(srcs, outs, sems):
        mine, sends, _ = plan(srcs, outs, sems)
        for cp in mine + sends:
            cp.start()

    def finish(srcs, outs, sems):
        mine, sends, arrivals = plan(srcs, outs, sems)
        for cp in arrivals:
            cp.wait_recv()
        for cp in sends:
            cp.wait_send()
        for cp in mine:
            cp.wait()

    dma = pltpu.SemaphoreType.DMA
    return Exchange(parts, [jax.ShapeDtypeStruct(s, BF16) for s in recv_shapes],
                    [dma((n, 3)), dma((n, 3)), dma((n,))], start, finish)


SMALL_ROWS = 24


def all_reduce_small(pack):
    def body(p_ref, o_ref, buf, send_sems, recv_sems):
        me = _me()
        buf[_flat(me)] = p_ref[...]
        sent = []
        for r in range(1, N_DEV):
            peer = _flip(me, r)
            cp = pltpu.make_async_remote_copy(src_ref=p_ref, dst_ref=buf.at[_flat(me)], send_sem=send_sems.at[r - 1],
                                              recv_sem=recv_sems.at[r - 1], device_id=peer, device_id_type=MESH)
            cp.start()
            sent.append(cp)
        for r in range(1, N_DEV):
            peer = _flip(me, r)
            land = buf.at[_flat(peer)]
            pltpu.make_async_remote_copy(src_ref=land, dst_ref=land, send_sem=send_sems.at[r - 1],
                                         recv_sem=recv_sems.at[r - 1], device_id=peer, device_id_type=MESH).wait_recv()
        for cp in sent:
            cp.wait_send()
        acc = buf[0]
        for d in range(1, N_DEV):
            acc = acc + buf[d]
        o_ref[...] = acc

    vm = pl.BlockSpec(memory_space=pltpu.VMEM)
    return pl.pallas_call(
        body, name="all_reduce_small", in_specs=[vm], out_specs=vm,
        out_shape=jax.ShapeDtypeStruct(pack.shape, F32),
        scratch_shapes=[pltpu.VMEM((N_DEV,) + pack.shape, F32), pltpu.SemaphoreType.DMA((7,)),
                        pltpu.SemaphoreType.DMA((7,))],
    )(pack)


def _adamw(w, g, m, v):
    m = ADAM_B1 * m + (1.0 - ADAM_B1) * g
    v = ADAM_B2 * v + (1.0 - ADAM_B2) * (g * g)
    m_hat = m / (1.0 - ADAM_B1 ** ADAM_STEP)
    v_hat = v / (1.0 - ADAM_B2 ** ADAM_STEP)
    delta = -ADAM_LR * (m_hat / (jnp.sqrt(v_hat) + ADAM_EPS) + ADAM_WD * w)
    return delta, m, v


ADAM_ROWS = 128


def reduce_adamw(recv, w, m, v, name):
    nl, rows, cols = w.shape
    nslot, cp = recv.shape[0], recv.shape[3]

    def body(r_ref, w_ref, m_ref, v_ref, g_ref, d_ref, mo_ref, vo_ref):
        g = r_ref[0, 0].astype(F32)
        for slot in range(1, nslot):
            g = g + r_ref[slot, 0].astype(F32)
        if cp != cols:
            g = g[:, :cols]
        delta, m_new, v_new = _adamw(w_ref[0], g, m_ref[0], v_ref[0])
        g_ref[0] = g
        d_ref[0] = delta
        mo_ref[0] = m_new
        vo_ref[0] = v_new

    tile = pl.BlockSpec((1, ADAM_ROWS, cols), lambda l, t: (l, t, 0))
    out = jax.ShapeDtypeStruct(w.shape, F32)
    return pl.pallas_call(
        body, name=name, grid=(nl, rows // ADAM_ROWS),
        in_specs=[pl.BlockSpec((nslot, 1, ADAM_ROWS, cp), lambda l, t: (0, l, t, 0)), tile, tile, tile],
        out_specs=[tile, tile, tile, tile], out_shape=[out, out, out, out],
        compiler_params=_params("parallel", "parallel"),
    )(recv, w, m, v)


def adamw_small(g, w, m, v):
    def body(g_ref, w_ref, m_ref, v_ref, d_ref, mo_ref, vo_ref):
        d_ref[...], mo_ref[...], vo_ref[...] = _adamw(w_ref[...], g_ref[...], m_ref[...], v_ref[...])

    out = jax.ShapeDtypeStruct(g.shape, F32)
    return pl.pallas_call(body, name="adamw_small", out_shape=[out, out, out])(g, w, m, v)


def _row8(v):
    return jnp.pad(v.reshape(1, -1).astype(F32), ((0, 7), (0, 0)))


def _row8_lanes(v, width=LANE):
    return jnp.pad(v.reshape(1, -1).astype(F32), ((0, 7), (0, width - v.size)))


def _pack_rows(parts):
    rows = []
    for p in parts:
        p = p.reshape(-1).astype(F32)
        nrow = -(-p.size // D_MODEL)
        rows.append(jnp.pad(p, (0, nrow * D_MODEL - p.size)).reshape(nrow, D_MODEL))
    out = jnp.concatenate(rows, axis=0)
    return jnp.pad(out, ((0, SMALL_ROWS - out.shape[0]), (0, 0)))


def _unpack_rows(pack, shapes):
    out, r = [], 0
    for shp in shapes:
        size = 1
        for d in shp:
            size *= d
        nrow = -(-size // D_MODEL)
        out.append(pack[r:r + nrow].reshape(-1)[:size].reshape(shp))
        r += nrow
    return out


def _dn_weight_layout(gathered):
    full = jnp.transpose(gathered[:, :, :DN_SHARD], (1, 0, 2)).reshape(D_MODEL, DN_PROJ)
    n_ab = 2 * DN_V_HEADS
    zeros = jnp.zeros((D_MODEL, 2 * LANE - n_ab), full.dtype)
    return jnp.concatenate([full[:, :DN_QKV + n_ab], zeros, full[:, DN_QKV + n_ab:]], axis=1)


def _dn_grad_blocks(dw):
    n_ab = 2 * DN_V_HEADS
    full = jnp.concatenate([dw[:, :DN_QKV + n_ab], dw[:, DN_QKV + 2 * LANE:]], axis=1)
    blocks = jnp.transpose(full.reshape(D_MODEL, N_DEV, DN_SHARD), (1, 0, 2))
    return jnp.pad(blocks, ((0, 0), (0, 0), (0, DN_SHARD_PAD - DN_SHARD)))


def kernel(x, mem, norm_g, mem_norm_g, mem_w_kv, xa_q_norm_g, xa_k_norm_g, w_out, dn_w_in, dn_conv_w, dn_a_log, dn_dt_bias, dn_out_norm_g, sb_w_in, sb_q_norm_g, sb_k_norm_g, loss_target, m_norm_g, m_mem_norm_g, m_mem_w_kv, m_xa_q_norm_g, m_xa_k_norm_g, m_w_out, m_dn_w_in, m_dn_conv_w, m_dn_a_log, m_dn_dt_bias, m_dn_out_norm_g, m_sb_w_in, m_sb_q_norm_g, m_sb_k_norm_g, v_norm_g, v_mem_norm_g, v_mem_w_kv, v_xa_q_norm_g, v_xa_k_norm_g, v_w_out, v_dn_w_in, v_dn_conv_w, v_dn_a_log, v_dn_dt_bias, v_dn_out_norm_g, v_sb_w_in, v_sb_q_norm_g, v_sb_k_norm_g):
    x0, memv, target = x[0], mem[0], loss_target[0]
    my_dev = 4 * lax.axis_index("x") + 2 * lax.axis_index("y") + lax.axis_index("c")

    dn_shard = jnp.pad(dn_w_in[0].astype(BF16), ((0, 0), (0, DN_SHARD_PAD - DN_SHARD)))
    w_out_b = w_out.astype(BF16)
    w_kv_b = mem_w_kv.astype(BF16)
    conv_shard = jnp.pad(dn_conv_w[0], ((0, 4), (0, 0)))
    g_dn, g_kv0, g_conv = run_exchange(gather_exchange([dn_shard, w_kv_b[0], conv_shard]), "gather_first")
    w_dn = _dn_weight_layout(g_dn)
    conv_w = jnp.transpose(g_conv, (1, 0, 2)).reshape(8, DN_QKV)

    ng = [_row8(norm_g[0]), _row8(norm_g[1])]
    mem_g = _row8(mem_norm_g)
    xqg = [_row8(xa_q_norm_g[0]), _row8(xa_q_norm_g[1])]
    xkg = [_row8(xa_k_norm_g[0]), _row8(xa_k_norm_g[1])]
    alog, dtb = _row8_lanes(dn_a_log[0]), _row8_lanes(dn_dt_bias[0])
    out_g, sbq_g, sbk_g = _row8(dn_out_norm_g[0]), _row8(sb_q_norm_g[0]), _row8(sb_k_norm_g[0])

    mem_n = rms_fwd(memv, mem_g)
    w_kv = [g_kv0.reshape(D_MODEL, 2 * XA_WIDTH), None]
    kv = [matmul(mem_n, w_kv[0], "nn", F32, 256, 1024, 2048, "kv0"), None]

    h0 = rms_fwd(x0, ng[0])
    proj0 = matmul(h0, w_dn, "nn", F32, 1024, 1152, 2048, "proj_dn")
    act = dn_conv_fwd(proj0, conv_w)
    u0, w_, qd, kd, qk, cd, g_wo0, g_wo1, g_kv1 = dn_prep_fwd(
        act, proj0, alog, dtb, exchange=gather_exchange([w_out_b[0], w_out_b[1], w_kv_b[1]]))
    o_raw, states, w_sb = dn_scan_fwd(u0, w_, qd, kd, qk, cd, MIX_WIDTH,
                                      exchange=gather_exchange([sb_w_in[0].astype(BF16)]))
    w_o = [g_wo0.reshape(INNER, D_MODEL), g_wo1.reshape(INNER, D_MODEL)]
    w_kv[1] = g_kv1.reshape(D_MODEL, 2 * XA_WIDTH)
    kv[1] = matmul(mem_n, w_kv[1], "nn", F32, 256, 1024, 2048, "kv1")
    cat0 = head_norm_fwd(o_raw, 0, DN_V_HEADS, out_g[None], F32, INNER, "dn_out_norm")
    cat0 = xa_fwd(proj0, DN_XQ_BLK, kv[0], xqg[0], xkg[0], cat0)
    y0 = gate_fwd(cat0, proj0, DN_Z_BLK)
    x1 = matmul(y0, w_o[0], "nn", F32, 1024, 1024, 2048, "out_proj0", add=x0)

    h1 = rms_fwd(x1, ng[1])
    proj1 = matmul(h1, w_sb, "nn", F32, 1024, 896, 2048, "proj_sb", b_blocked=True)
    qkn = head_norm_fwd(proj1, 0, 2 * SB_HEADS, jnp.stack([sbq_g, sbk_g]), BF16, 2 * MIX_WIDTH, "sb_qk_norm")
    cat1, ltot = sb_fwd(qkn, proj1, 2 * SB_HEADS, INNER)
    cat1 = xa_fwd(proj1, SB_XQ_BLK, kv[1], xqg[1], xkg[1], cat1)
    y1 = gate_fwd(cat1, proj1, SB_Z_BLK)
    x2 = matmul(y1, w_o[1], "nn", F32, 1024, 1024, 2048, "out_proj1", add=x1)
    loss_part, dx2 = loss_fwd_bwd(x2, target)

    dy1 = matmul(dx2, w_o[1], "nt", F32, 1024, 1024, 2048, "d_y1")
    dw_o1 = matmul(y1, dx2, "tn", BF16, 1024, 1024, 1024, "d_w_out1")
    dcat1, dproj1 = gate_bwd(dy1, cat1, proj1, SB_Z_BLK)
    dproj1, dxk1, dxv1, dxqg1, dxkg1 = xa_bwd(proj1, SB_XQ_BLK, kv[1], xqg[1], xkg[1], dcat1, dproj1)
    dqn, dkn, dproj1 = sb_bwd(qkn, proj1, 2 * SB_HEADS, dcat1, ltot, dproj1)
    dproj1, d_sbq = head_norm_bwd(dqn, 0, proj1, 0, SB_HEADS, sbq_g, dproj1, 0, "sb_q_norm_bwd")
    dproj1, d_sbk = head_norm_bwd(dkn, 0, proj1, SB_HEADS, SB_HEADS, sbk_g, dproj1, SB_HEADS, "sb_k_norm_bwd")
    dw_sb = matmul(h1, dproj1, "tn", BF16, 1024, 896, 1024, "d_w_sb", out_blocks=N_DEV)
    dh1 = matmul(dproj1, w_sb, "nt", F32, 1024, 1024, 896, "d_h1", b_blocked=True)
    dx1, d_ng1 = rms_bwd(dh1, x1, ng[1], dx2)

    dy0 = matmul(dx1, w_o[0], "nt", F32, 1024, 1024, 2048, "d_y0")
    dw_o0 = matmul(y0, dx1, "tn", BF16, 1024, 1024, 1024, "d_w_out0")
    dcat0, dproj0 = gate_bwd(dy0, cat0, proj0, DN_Z_BLK)
    dproj0, dxk0, dxv0, dxqg0, dxkg0 = xa_bwd(proj0, DN_XQ_BLK, kv[0], xqg[0], xkg[0], dcat0, dproj0)

    dkv = [jnp.concatenate([dxk0, dxv0], axis=1), jnp.concatenate([dxk1, dxv1], axis=1)]
    dw_kv = [matmul(mem_n, dkv[i], "tn", BF16, 1024, 1024, 256, f"d_w_kv{i}") for i in range(2)]
    dmem_n = matmul(dkv[0], w_kv[0], "nt", F32, 256, 1024, 2048, "d_mem_n0")
    dmem_n = matmul(dkv[1], w_kv[1], "nt", F32, 256, 1024, 2048, "d_mem_n1", add=dmem_n)
    _, d_memg = rms_bwd(dmem_n, memv, mem_g, None)

    by_owner = lambda g, rows: g.reshape(4, 2, rows, g.size // (N_DEV * rows))
    grads = [dw_sb.reshape(4, 2, D_MODEL, SB_PROJ // N_DEV), by_owner(dw_o0, INNER // N_DEV),
             by_owner(dw_o1, INNER // N_DEV), by_owner(dw_kv[0], D_MODEL // N_DEV), by_owner(dw_kv[1], D_MODEL // N_DEV)]
    do_raw, d_outg = head_norm_bwd(dcat0, 0, o_raw, 0, DN_V_HEADS, out_g, None, 0, "dn_out_norm_bwd")
    du0, dw_, dqd, dkd, dqk, dcd, *halves = dn_scan_bwd(do_raw, u0, w_, qd, kd, qk, cd, states,
                                                         exchange=pair_exchange(grads))
    sums = [pair_sum(g, h, f"pair_sum{i}") for i, (g, h) in enumerate(zip(grads, halves))]
    to_chips = chip_exchange(sums, [(0, 0), (1, 0), (1, 1), (2, 0), (2, 1)],
                             [(4, 1, D_MODEL, SB_PROJ // N_DEV), (4, 2, INNER // N_DEV, D_MODEL),
                              (4, 2, D_MODEL // N_DEV, 2 * XA_WIDTH)])
    dq_a, dk_a, dv_a, dab, d_alog, d_dtb, recv_sb, recv_wo, recv_kv = dn_prep_bwd(
        act, proj0, alog, dtb, du0, dw_, dqd, dkd, dqk, dcd, exchange=to_chips)
    dproj0, dcw_q = dn_conv_bwd(dq_a, proj0, conv_w, dproj0, None, 0, "dn_conv_bwd_q")
    dproj0, dcw_k = dn_conv_bwd(dk_a, proj0, conv_w, dproj0, None, DN_QK_HEADS, "dn_conv_bwd_k")
    dproj0, dcw_v = dn_conv_bwd(dv_a, proj0, conv_w, dproj0, dab, 2 * DN_QK_HEADS, "dn_conv_bwd_v")
    dw_dn = matmul(h0, dproj0, "tn", BF16, 1024, 1152, 1024, "d_w_dn")
    dn_grad = _dn_grad_blocks(dw_dn).reshape(4, 2, D_MODEL, DN_SHARD_PAD)
    dh0, dn_half = matmul(dproj0, w_dn, "nt", F32, 1024, 1024, 1152, "d_h0", exchange=pair_exchange([dn_grad]))
    grad_x, d_ng0 = rms_bwd(dh0, x0, ng[0], dx1)
    dn_sum = pair_sum(dn_grad, dn_half, "pair_sum_dn")
    recv_dn, = run_exchange(chip_exchange([dn_sum], [(0, 0)], [(4, 1, D_MODEL, DN_SHARD_PAD)]), "reduce_dn")

    big = {
        "dn_w_in": reduce_adamw(recv_dn, dn_w_in, m_dn_w_in, v_dn_w_in, "adamw_dn_w_in"),
        "sb_w_in": reduce_adamw(recv_sb, sb_w_in, m_sb_w_in, v_sb_w_in, "adamw_sb_w_in"),
        "w_out": reduce_adamw(recv_wo, w_out, m_w_out, v_w_out, "adamw_w_out"),
        "mem_w_kv": reduce_adamw(recv_kv, mem_w_kv, m_mem_w_kv, v_mem_w_kv, "adamw_mem_w_kv"),
    }

    dconv = jnp.concatenate([dcw_q, dcw_k, dcw_v], axis=1)[:4]
    small_shapes = [(2, D_MODEL), (D_MODEL,), (2, XA_DIM), (2, XA_DIM), (4, DN_QKV), (1, DN_V_HEADS),
                    (1, DN_V_HEADS), (1, HEAD_DIM), (1, HEAD_DIM), (1, HEAD_DIM), (1,)]
    pack = _pack_rows([jnp.stack([d_ng0[0], d_ng1[0]]), d_memg[0], jnp.stack([dxqg0[0], dxqg1[0]]),
                       jnp.stack([dxkg0[0], dxkg1[0]]), dconv, d_alog[0, :DN_V_HEADS], d_dtb[0, :DN_V_HEADS],
                       d_outg[0], d_sbq[0], d_sbk[0], loss_part[0, :1]])
    total = all_reduce_small(pack)
    (g_norm, g_memn, g_xq, g_xk, g_conv_full, g_alog, g_dtb, g_outn, g_sbq, g_sbk, loss1) = _unpack_rows(total, small_shapes)
    conv_cols = DN_QKV // N_DEV
    g_conv = lax.dynamic_slice(g_conv_full, (0, my_dev * conv_cols), (4, conv_cols))[None]
    names = ["norm_g", "mem_norm_g", "xa_q_norm_g", "xa_k_norm_g", "dn_conv_w", "dn_a_log", "dn_dt_bias",
             "dn_out_norm_g", "sb_q_norm_g", "sb_k_norm_g"]
    grads = [g_norm, g_memn, g_xq, g_xk, g_conv, g_alog, g_dtb, g_outn, g_sbq, g_sbk]
    ws = [norm_g, mem_norm_g, xa_q_norm_g, xa_k_norm_g, dn_conv_w, dn_a_log, dn_dt_bias, dn_out_norm_g, sb_q_norm_g,
          sb_k_norm_g]
    ms = [m_norm_g, m_mem_norm_g, m_xa_q_norm_g, m_xa_k_norm_g, m_dn_conv_w, m_dn_a_log, m_dn_dt_bias,
          m_dn_out_norm_g, m_sb_q_norm_g, m_sb_k_norm_g]
    vs = [v_norm_g, v_mem_norm_g, v_xa_q_norm_g, v_xa_k_norm_g, v_dn_conv_w, v_dn_a_log, v_dn_dt_bias,
          v_dn_out_norm_g, v_sb_q_norm_g, v_sb_k_norm_g]
    shapes = [w.shape for w in ws]
    d_p, m_p, v_p = adamw_small(_pack_rows(grads), _pack_rows(ws), _pack_rows(ms), _pack_rows(vs))
    small = dict(zip(names, zip(grads, _unpack_rows(d_p, shapes), _unpack_rows(m_p, shapes), _unpack_rows(v_p, shapes))))

    order = ["norm_g", "mem_norm_g", "mem_w_kv", "xa_q_norm_g", "xa_k_norm_g", "w_out", "dn_w_in", "dn_conv_w",
             "dn_a_log", "dn_dt_bias", "dn_out_norm_g", "sb_w_in", "sb_q_norm_g", "sb_k_norm_g"]
    res = {**big, **small}
    outs = [loss1.reshape(()), grad_x[None]]
    for k in range(4):
        outs += [res[n][k] for n in order]
    return tuple(outs)
```

```python
import functools

import jax
import jax.numpy as jnp
from jax import lax
from jax.experimental import pallas as pl
from jax.experimental.pallas import tpu as pltpu

F32 = jnp.float32
BF16 = jnp.bfloat16

D_MODEL = 2048
SEQ = 2048
N_MEM = 256
INNER = 4096
XA_HEADS = 4
XA_WIDTH = 1024
XA_DIM = 256
MIX_WIDTH = 3072
HEAD_DIM = 128
DN_V_HEADS = 24
DN_QK_HEADS = 12
DN_QK_WIDTH = 1536
DN_CHUNK = 64
DN_QKV = 2 * DN_QK_WIDTH + MIX_WIDTH
DN_PROJ = 11312
SB_HEADS = 24
SB_PROJ = 14336
EPS = 1e-6
N_DEV = 8
DN_SHARD = DN_PROJ // N_DEV
DN_SHARD_PAD = 1536
LANE = 128
DN_COLS = 90 * LANE
DN_AB_BLK, DN_PAD_BLK, DN_XQ_BLK, DN_Z_BLK = 48, 49, 50, 58
SB_XQ_BLK, SB_Z_BLK = 72, 80

ADAM_LR, ADAM_B1, ADAM_B2, ADAM_EPS, ADAM_WD, ADAM_STEP = 0.001, 0.9, 0.999, 1e-08, 0.01, 10

VMEM_LIMIT = 56 * 1024 * 1024
MESH = pl.DeviceIdType.MESH

_NN, _NT, _TN = "nn", "nt", "tn"


def _dims(mode, rank):
    lhs, rhs = {"nn": (1, 0), "nt": (1, 1), "tn": (0, 0)}[mode]
    if rank == 2:
        return (((lhs,), (rhs,)), ((), ()))
    return (((lhs + 1,), (rhs + 1,)), ((0,), (0,)))


def _params(*sem):
    return pltpu.CompilerParams(dimension_semantics=sem if sem else None, vmem_limit_bytes=VMEM_LIMIT)


def _dot(a, b, mode):
    return lax.dot_general(a, b, _dims(mode, a.ndim), preferred_element_type=F32)


def _dg(a, b, dims):
    return _dot(a.astype(BF16), b.astype(BF16), dims)


@jax.custom_vjp
def mm(a, b):
    return _dg(a, b, _NN)


@jax.custom_vjp
def mm_nt(a, b):
    return _dg(a, b, _NT)


@jax.custom_vjp
def mm_tn(a, b):
    return _dg(a, b, _TN)


mm.defvjp(lambda a, b: (_dg(a, b, _NN), (a, b)), lambda r, g: (mm_nt(g, r[1]), mm_tn(r[0], g)))
mm_nt.defvjp(lambda a, b: (_dg(a, b, _NT), (a, b)), lambda r, g: (mm(g, r[1]), mm_tn(g, r[0])))
mm_tn.defvjp(lambda a, b: (_dg(a, b, _TN), (a, b)), lambda r, g: (mm_nt(r[1], g), mm(r[0], g)))


def _split3(x):
    hi = x.astype(BF16)
    r1 = x - hi.astype(F32)
    mid = r1.astype(BF16)
    lo = (r1 - mid.astype(F32)).astype(BF16)
    return hi, mid, lo


def _dg3(a, b, dims):
    ah, am, _ = _split3(a)
    bh, bm, _ = _split3(b)
    return _dot(ah, bh, dims) + (_dot(ah, bm, dims) + _dot(am, bh, dims))


def _dg_exact_rhs(a, b01, dims):
    ah, am, _ = _split3(a)
    b = b01.astype(BF16)
    return _dot(ah, b, dims) + _dot(am, b, dims)


def _dg_exact_lhs(a01, b, dims):
    bh, bm, bl = _split3(b)
    a = a01.astype(BF16)
    return _dot(a, bh, dims) + (_dot(a, bm, dims) + _dot(a, bl, dims))


@jax.custom_vjp
def mm3(a, b):
    return _dg3(a, b, _NN)


@jax.custom_vjp
def mm3_nt(a, b):
    return _dg3(a, b, _NT)


@jax.custom_vjp
def mm3_tn(a, b):
    return _dg3(a, b, _TN)


mm3.defvjp(lambda a, b: (_dg3(a, b, _NN), (a, b)), lambda r, g: (mm3_nt(g, r[1]), mm3_tn(r[0], g)))
mm3_nt.defvjp(lambda a, b: (_dg3(a, b, _NT), (a, b)), lambda r, g: (mm3(g, r[1]), mm3_tn(g, r[0])))
mm3_tn.defvjp(lambda a, b: (_dg3(a, b, _TN), (a, b)), lambda r, g: (mm3_nt(r[1], g), mm3(r[0], g)))


def _softplus(x):
    return jnp.maximum(x, 0.0) + jnp.log(1.0 + jnp.exp(-jnp.abs(x)))


def _log_sigmoid(x):
    return jnp.minimum(x, 0.0) - jnp.log(1.0 + jnp.exp(-jnp.abs(x)))


def _sigmoid(x):
    return 1.0 / (1.0 + jnp.exp(-x))


def _silu(x):
    return x * _sigmoid(x)


def _silu_grad(x):
    s = _sigmoid(x)
    return s * (1.0 + x * (1.0 - s))


@jax.custom_vjp
def mm01(a01, b):
    return _dg_exact_lhs(a01, b, _NN)


mm01.defvjp(lambda a, b: (_dg_exact_lhs(a, b, _NN), a),
            lambda a, g: (jnp.zeros_like(a), _dg_exact_lhs(a, g, _TN)))


def _lane_pick(x, idx):
    lane = lax.broadcasted_iota(jnp.int32, x.shape, x.ndim - 1)
    return jnp.sum(jnp.where(lane == idx, x, 0.0), axis=-1, keepdims=True)


def _dn_chunk(qt, kt, v, ab, alog, dtb, h):
    B, C = qt.shape[0], DN_CHUNK
    g = -jnp.exp(_lane_pick(alog, h)) * _softplus(_lane_pick(ab, h) + _lane_pick(dtb, h))
    beta = _sigmoid(_lane_pick(ab, h + DN_V_HEADS))
    q = qt * lax.rsqrt(jnp.sum(qt * qt, axis=-1, keepdims=True) + EPS) * (HEAD_DIM ** -0.5)
    k = kt * lax.rsqrt(jnp.sum(kt * kt, axis=-1, keepdims=True) + EPS)
    row = lax.broadcasted_iota(jnp.int32, (B, C, C), 1)
    col = lax.broadcasted_iota(jnp.int32, (B, C, C), 2)
    lower = (row >= col).astype(F32)
    ones = jnp.ones((B, C, C), F32)
    g_wide = jnp.broadcast_to(g, (B, C, LANE))
    g_sq = jnp.broadcast_to(g, (B, C, C))
    gc = mm01(lower, g_wide)
    gc_i = gc[:, :, :C]
    gc_j = mm01(ones, jnp.where(row <= col, g_sq, 0.0))
    g_last = jnp.broadcast_to(gc[:, C - 1:C, :], (B, C, LANE))
    decay = jnp.exp(jnp.where(row >= col, gc_i - gc_j, -1e30))
    eg = jnp.exp(gc)
    kk = mm_nt(k, k)
    a_mat = jnp.where(row > col, jnp.broadcast_to(beta, (B, C, C)) * kk * decay, 0.0)
    eye = (row == col).astype(F32)
    y = -a_mat
    t = eye + y
    for _ in range(5):
        y = mm(y, y)
        t = t + mm(t, y)
    bb = jnp.broadcast_to(beta, (B, C, LANE))
    u0 = mm3(t, v * bb)
    w = mm3(t, k * (bb * eg))
    qk = mm_nt(q, k) * decay
    q_dec = q * eg
    k_dec = k * jnp.exp(g_last - gc)
    cd = jnp.exp(g_last)[:, :8, :]
    return u0, w, qk, q_dec, k_dec, cd


def _shift_rows(x, j, down):
    if j == 0:
        return x
    n = x.shape[0]
    r = lax.broadcasted_iota(jnp.int32, x.shape, 0)
    if down:
        return jnp.where(r >= j, pltpu.roll(x, j, 0), 0.0)
    return jnp.where(r < n - j, pltpu.roll(x, n - j, 0), 0.0)


def dn_conv_fwd(proj, conv_w):
    s = proj.shape[0]

    def body(x_ref, w_ref, o_ref):
        x = x_ref[...]
        w = w_ref[...]
        pre = x * w[3:4, :]
        for j in (1, 2, 3):
            pre = pre + _shift_rows(x, j, True) * w[3 - j:4 - j, :]
        o_ref[...] = _silu(pre)

    return pl.pallas_call(
        body, name="dn_conv_fwd", grid=(DN_QKV // LANE,),
        in_specs=[pl.BlockSpec((s, LANE), lambda j: (0, j)), pl.BlockSpec((8, LANE), lambda j: (0, j))],
        out_specs=pl.BlockSpec((s, LANE), lambda j: (0, j)),
        out_shape=jax.ShapeDtypeStruct((s, DN_QKV), F32), compiler_params=_params("parallel"),
    )(proj, conv_w)


def dn_conv_bwd(dact, proj, conv_w, dproj_in, dab, blk0, name):
    s = proj.shape[0]
    nblk = dact.shape[1] // LANE
    extra = 2 if dab is not None else 0

    def body(*refs):
        if dab is not None:
            da_ref, x_ref, w_ref, _, dab_ref, dp_ref, dw_ref = refs
        else:
            da_ref, x_ref, w_ref, _, dp_ref, dw_ref = refs
        j = pl.program_id(0)

        @pl.when(j < nblk)
        def _():
            x = x_ref[...]
            w = w_ref[...]
            xs = [_shift_rows(x, 3 - kk_, True) for kk_ in range(4)]
            pre = xs[0] * w[0:1, :]
            for kk_ in (1, 2, 3):
                pre = pre + xs[kk_] * w[kk_:kk_ + 1, :]
            dpre = da_ref[...] * _silu_grad(pre)
            dx = dpre * w[3:4, :]
            for jj in (1, 2, 3):
                dx = dx + _shift_rows(dpre, jj, False) * w[3 - jj:4 - jj, :]
            dp_ref[...] = dx
            rows = [jnp.sum(dpre * xs[kk_], axis=0, keepdims=True) for kk_ in range(4)]
            dw_ref[...] = jnp.concatenate(rows + [jnp.zeros((4, LANE), F32)], axis=0)

        if dab is not None:
            @pl.when(j == nblk)
            def _():
                dp_ref[...] = dab_ref[...]

            @pl.when(j == nblk + 1)
            def _():
                dp_ref[...] = jnp.zeros_like(dp_ref)

    cl = lambda j: jnp.minimum(j, nblk - 1)
    in_specs = [pl.BlockSpec((s, LANE), lambda j: (0, cl(j))),
                pl.BlockSpec((s, LANE), lambda j: (0, blk0 + cl(j))),
                pl.BlockSpec((8, LANE), lambda j: (0, blk0 + cl(j))),
                pl.BlockSpec(memory_space=pl.ANY)]
    args = [dact, proj, conv_w, dproj_in]
    if dab is not None:
        in_specs.append(pl.BlockSpec((s, LANE), lambda j: (0, 0)))
        args.append(dab)
    return pl.pallas_call(
        body, name=name, grid=(nblk + extra,), in_specs=in_specs,
        out_specs=[pl.BlockSpec((s, LANE), lambda j: (0, blk0 + j)), pl.BlockSpec((8, LANE), lambda j: (0, cl(j)))],
        out_shape=[jax.ShapeDtypeStruct(dproj_in.shape, F32), jax.ShapeDtypeStruct((8, dact.shape[1]), F32)],
        input_output_aliases={3: 0}, compiler_params=_params("arbitrary"),
    )(*args)


DN_GROUP = 8
DN_ROWS = DN_GROUP * DN_CHUNK


def dn_prep_fwd(act, proj, alog, dtb, exchange=None):
    s = act.shape[0]
    nc = s // DN_CHUNK
    C = DN_CHUNK

    def body(q_ref, k_ref, v_ref, ab_ref, al_ref, dt_ref, u0_ref, w_ref, qd_ref, kd_ref, qk_ref, cd_ref):
        qh = pl.program_id(0)
        al = al_ref[0:1, :]
        dt = dt_ref[0:1, :]
        chunks = lambda x: x.reshape(DN_GROUP, C, x.shape[-1])
        rows = lambda x: x.reshape(DN_ROWS, x.shape[-1])
        qt, kt, ab = chunks(q_ref[...]), chunks(k_ref[...]), chunks(ab_ref[...])
        for hv in range(2):
            cs = slice(hv * LANE, (hv + 1) * LANE)
            u0, w, qk, qd, kd, cd = _dn_chunk(qt, kt, chunks(v_ref[:, cs]), ab, al, dt, 2 * qh + hv)
            u0_ref[:, cs] = rows(u0)
            w_ref[:, cs] = rows(w)
            qd_ref[:, cs] = rows(qd)
            kd_ref[:, cs] = rows(kd)
            qk_ref[hv] = rows(qk)
            cd_ref[hv] = cd

    big = pl.BlockSpec((DN_ROWS, 2 * LANE), lambda h, g: (g, h))
    wide = jax.ShapeDtypeStruct((s, MIX_WIDTH), F32)
    return hosted_call(
        body, exchange, name="dn_prep_fwd", grid=(DN_QK_HEADS, s // DN_ROWS),
        in_specs=[pl.BlockSpec((DN_ROWS, LANE), lambda h, g: (g, h)),
                  pl.BlockSpec((DN_ROWS, LANE), lambda h, g: (g, DN_QK_HEADS + h)),
                  pl.BlockSpec((DN_ROWS, 2 * LANE), lambda h, g: (g, DN_QK_HEADS + h)),
                  pl.BlockSpec((DN_ROWS, LANE), lambda h, g: (g, DN_AB_BLK)),
                  pl.BlockSpec((8, LANE), lambda h, g: (0, 0)), pl.BlockSpec((8, LANE), lambda h, g: (0, 0))],
        out_specs=[big, big, big, big,
                   pl.BlockSpec((2, DN_ROWS, C), lambda h, g: (h, g, 0)),
                   pl.BlockSpec((2, DN_GROUP, 8, LANE), lambda h, g: (h, g, 0, 0))],
        out_shape=[wide, wide, wide, wide, jax.ShapeDtypeStruct((DN_V_HEADS, s, C), F32),
                   jax.ShapeDtypeStruct((DN_V_HEADS, nc, 8, LANE), F32)],
        sem=("parallel", "parallel"),
    )(act, act, act, proj, alog, dtb)


def dn_prep_bwd(act, proj, alog, dtb, du0, dw, dqd, dkd, dqk, dcd, exchange=None):
    s = act.shape[0]
    C = DN_CHUNK

    def body(q_ref, k_ref, v_ref, ab_ref, al_ref, dt_ref, du0_ref, dw_ref, dqd_ref, dkd_ref, dqk_ref, dcd_ref,
             dq_ref, dk_ref, dv_ref, dab_ref, dal_ref, ddt_ref):
        g_id = pl.program_id(0)
        qh = pl.program_id(1)
        al = al_ref[0:1, :]
        dt = dt_ref[0:1, :]

        @pl.when(qh == 0)
        def _():
            dab_ref[...] = jnp.zeros_like(dab_ref)

        @pl.when((qh == 0) & (g_id == 0))
        def _():
            dal_ref[...] = jnp.zeros_like(dal_ref)
            ddt_ref[...] = jnp.zeros_like(ddt_ref)

        chunks = lambda x: x.reshape(DN_GROUP, C, x.shape[-1])
        rows = lambda x: x.reshape(DN_ROWS, x.shape[-1])
        qt, kt, ab = chunks(q_ref[...]), chunks(k_ref[...]), chunks(ab_ref[...])
        dq_acc = jnp.zeros((DN_ROWS, LANE), F32)
        dk_acc = jnp.zeros((DN_ROWS, LANE), F32)
        for hv in range(2):
            cs = slice(hv * LANE, (hv + 1) * LANE)
            h = 2 * qh + hv
            f = lambda qt_, kt_, v_, ab_, a_, d_: _dn_chunk(qt_, kt_, v_, ab_, a_, d_, h)
            _, vjp = jax.vjp(f, qt, kt, chunks(v_ref[:, cs]), ab, al, dt)
            dq, dk, dv, dab, dal, ddt = vjp((chunks(du0_ref[:, cs]), chunks(dw_ref[:, cs]), chunks(dqk_ref[hv]),
                                             chunks(dqd_ref[:, cs]), chunks(dkd_ref[:, cs]), dcd_ref[hv]))
            dq_acc = dq_acc + rows(dq)
            dk_acc = dk_acc + rows(dk)
            dv_ref[:, cs] = rows(dv)
            dab_ref[...] += rows(dab)
            dal_ref[0:1, :] += dal
            ddt_ref[0:1, :] += ddt
        dq_ref[...] = dq_acc
        dk_ref[...] = dk_acc

    big = pl.BlockSpec((DN_ROWS, 2 * LANE), lambda g, h: (g, h))
    one = pl.BlockSpec((DN_ROWS, LANE), lambda g, h: (g, h))
    small = pl.BlockSpec((8, LANE), lambda g, h: (0, 0))
    return hosted_call(
        body, exchange, name="dn_prep_bwd", grid=(s // DN_ROWS, DN_QK_HEADS),
        in_specs=[one, pl.BlockSpec((DN_ROWS, LANE), lambda g, h: (g, DN_QK_HEADS + h)),
                  pl.BlockSpec((DN_ROWS, 2 * LANE), lambda g, h: (g, DN_QK_HEADS + h)),
                  pl.BlockSpec((DN_ROWS, LANE), lambda g, h: (g, DN_AB_BLK)), small, small,
                  big, big, big, big,
                  pl.BlockSpec((2, DN_ROWS, C), lambda g, h: (h, g, 0)),
                  pl.BlockSpec((2, DN_GROUP, 8, LANE), lambda g, h: (h, g, 0, 0))],
        out_specs=[one, one, big, pl.BlockSpec((DN_ROWS, LANE), lambda g, h: (g, 0)), small, small],
        out_shape=[jax.ShapeDtypeStruct((s, DN_QK_WIDTH), F32), jax.ShapeDtypeStruct((s, DN_QK_WIDTH), F32),
                   jax.ShapeDtypeStruct((s, MIX_WIDTH), F32), jax.ShapeDtypeStruct((s, LANE), F32),
                   jax.ShapeDtypeStruct((8, LANE), F32), jax.ShapeDtypeStruct((8, LANE), F32)],
        sem=("arbitrary", "arbitrary"),
    )(act, act, act, proj, alog, dtb, du0, dw, dqd, dkd, dqk, dcd)


def dn_scan_fwd(u0, w, qd, kd, qk, cd, width, exchange=None):
    s = u0.shape[0]
    nc = s // DN_CHUNK
    C = DN_CHUNK

    def body(u0_ref, w_ref, qd_ref, kd_ref, qk_ref, cd_ref, o_ref, st_ref):
        def step(c, state):
            rs = pl.ds(pl.multiple_of(c * C, C), C)
            st_ref[0, c] = state
            u = u0_ref[rs, :] - _dg(w_ref[rs, :], state, _NN)
            o_ref[rs, :] = _dg(qd_ref[rs, :], state, _NN) + _dg(qk_ref[0, rs, :], u, _NN)
            return cd_ref[0, c][0:1, :] * state + _dg(kd_ref[rs, :], u, _TN)

        lax.fori_loop(0, nc, step, jnp.zeros((HEAD_DIM, HEAD_DIM), F32))

    col = pl.BlockSpec((s, LANE), lambda h: (0, h))
    return hosted_call(
        body, exchange, name="dn_scan_fwd", grid=(DN_V_HEADS,),
        in_specs=[col, col, col, col, pl.BlockSpec((1, s, C), lambda h: (h, 0, 0)),
                  pl.BlockSpec((1, nc, 8, LANE), lambda h: (h, 0, 0, 0))],
        out_specs=[col, pl.BlockSpec((1, nc, HEAD_DIM, HEAD_DIM), lambda h: (h, 0, 0, 0))],
        out_shape=[jax.ShapeDtypeStruct((s, width), F32),
                   jax.ShapeDtypeStruct((DN_V_HEADS, nc, HEAD_DIM, HEAD_DIM), F32)],
        sem=("parallel",),
    )(u0, w, qd, kd, qk, cd)


def dn_scan_bwd(do, u0, w, qd, kd, qk, cd, states, exchange=None):
    s = u0.shape[0]
    nc = s // DN_CHUNK
    C = DN_CHUNK

    def body(do_ref, u0_ref, w_ref, qd_ref, kd_ref, qk_ref, cd_ref, st_ref,
             du0_ref, dw_ref, dqd_ref, dkd_ref, dqk_ref, dcd_ref):
        def step(i, dstate):
            c = nc - 1 - i
            rs = pl.ds(pl.multiple_of(c * C, C), C)
            state = st_ref[0, c]
            g = do_ref[rs, :]
            w_c = w_ref[rs, :]
            kd_c = kd_ref[rs, :]
            qd_c = qd_ref[rs, :]
            qk_c = qk_ref[0, rs, :]
            cd_row = cd_ref[0, c][0:1, :]
            u = u0_ref[rs, :] - _dg(w_c, state, _NN)
            du = _dg(qk_c, g, _TN) + _dg(kd_c, dstate, _NN)
            du0_ref[rs, :] = du
            dw_ref[rs, :] = -_dg(du, state, _NT)
            dqd_ref[rs, :] = _dg(g, state, _NT)
            dkd_ref[rs, :] = _dg(u, dstate, _NT)
            dqk_ref[0, rs, :] = _dg(g, u, _NT)
            dcd_row = jnp.sum(state * dstate, axis=0, keepdims=True)
            dcd_ref[0, c] = jnp.concatenate([dcd_row, jnp.zeros((7, LANE), F32)], axis=0)
            return cd_row * dstate + _dg(qd_c, g, _TN) - _dg(w_c, du, _TN)

        lax.fori_loop(0, nc, step, jnp.zeros((HEAD_DIM, HEAD_DIM), F32))

    col = pl.BlockSpec((s, LANE), lambda h: (0, h))
    qk_spec = pl.BlockSpec((1, s, C), lambda h: (h, 0, 0))
    cd_spec = pl.BlockSpec((1, nc, 8, LANE), lambda h: (h, 0, 0, 0))
    wide = jax.ShapeDtypeStruct((s, MIX_WIDTH), F32)
    return hosted_call(
        body, exchange, name="dn_scan_bwd", grid=(DN_V_HEADS,),
        in_specs=[col, col, col, col, col, qk_spec, cd_spec,
                  pl.BlockSpec((1, nc, HEAD_DIM, HEAD_DIM), lambda h: (h, 0, 0, 0))],
        out_specs=[col, col, col, col, qk_spec, cd_spec],
        out_shape=[wide, wide, wide, wide, jax.ShapeDtypeStruct((DN_V_HEADS, s, C), F32),
                   jax.ShapeDtypeStruct((DN_V_HEADS, nc, 8, LANE), F32)],
        sem=("parallel",),
    )(do, u0, w, qd, kd, qk, cd, states)


SB_T = 256
SB_PAIR = 2


def _sb_scores(q, kbs, diff, lims):
    zs = [_dg(q, kb, _NT) * (HEAD_DIM ** -0.5) for kb in kbs]
    masks = [diff < lim for lim in lims]
    lss = [_log_sigmoid(z) for z in zs]
    lrs = [jnp.where(m, ls - z, 0.0) for m, ls, z in zip(masks, lss, zs)]
    return masks, lss, lrs


def _sb_diff():
    return lax.broadcasted_iota(jnp.int32, (SB_T, SB_T), 1) - lax.broadcasted_iota(jnp.int32, (SB_T, SB_T), 0)


def _sb_loop(n_tiles, step, carry):
    pairs = n_tiles // SB_PAIR
    carry = lax.fori_loop(0, pairs, lambda p, c: step(p * SB_PAIR, SB_PAIR, c), carry)
    return lax.fori_loop(pairs * SB_PAIR, n_tiles, lambda t, c: step(t, 1, c), carry)


def sb_fwd(qkn, proj, v_blk0, width, exchange=None):
    s = qkn.shape[0]

    def body(q_ref, k_ref, v_ref, o_ref, lt_ref):
        i = pl.program_id(1)
        q = q_ref[...]
        diff = _sb_diff()
        after = (diff < 0).astype(BF16)

        def step(first, n, carry):
            run, acc = carry
            tiles = [first + t for t in range(n)]
            kss = [pl.ds(pl.multiple_of((i - t) * SB_T, SB_T), SB_T) for t in tiles]
            masks, lss, lrs = _sb_scores(q, [k_ref[ks, :] for ks in kss], diff, [t * SB_T for t in tiles])
            within = [_dg_exact_rhs(lr, after, _NN) for lr in lrs]
            sums = [jnp.sum(lr, axis=1, keepdims=True) for lr in lrs]
            for t in range(n):
                wts = jnp.where(masks[t], jnp.exp(lss[t] + (within[t] + run)), 0.0)
                acc = acc + _dg(wts, v_ref[kss[t], :], _NN)
                run = run + sums[t]
            return run, acc

        run, acc = _sb_loop(i + 1, step, (jnp.zeros((SB_T, 1), F32), jnp.zeros((SB_T, HEAD_DIM), F32)))
        o_ref[...] = acc
        lt_ref[0] = run

    return hosted_call(
        body, exchange, name="sb_fwd", grid=(SB_HEADS, s // SB_T),
        in_specs=[pl.BlockSpec((SB_T, LANE), lambda h, i: (i, h)),
                  pl.BlockSpec((s, LANE), lambda h, i: (0, SB_HEADS + h)),
                  pl.BlockSpec((s, LANE), lambda h, i: (0, v_blk0 + h))],
        out_specs=[pl.BlockSpec((SB_T, LANE), lambda h, i: (i, h)), pl.BlockSpec((1, SB_T, 1), lambda h, i: (h, i, 0))],
        out_shape=[jax.ShapeDtypeStruct((s, width), F32), jax.ShapeDtypeStruct((SB_HEADS, s, 1), F32)],
        sem=("parallel", "parallel"),
    )(qkn, qkn, proj)


def sb_bwd(qkn, proj, v_blk0, do, ltot, dproj_in):
    s = qkn.shape[0]

    def body(q_ref, k_ref, v_ref, do_ref, lt_ref, _, dq_ref, dk_ref, dv_ref):
        i = pl.program_id(1)

        @pl.when(i == 0)
        def _():
            dk_ref[...] = jnp.zeros_like(dk_ref)
            dv_ref[...] = jnp.zeros_like(dv_ref)

        q = q_ref[...]
        g = do_ref[...]
        ltot = lt_ref[0]
        diff = _sb_diff()
        upto = (diff >= 0).astype(BF16)
        before = (diff > 0).astype(BF16)

        def step(first, n, carry):
            plr, pdl, dq = carry
            tiles = [first + t for t in range(n)]
            kss = [pl.ds(pl.multiple_of(j * SB_T, SB_T), SB_T) for j in tiles]
            kbs = [k_ref[ks, :] for ks in kss]
            vbs = [v_ref[ks, :] for ks in kss]
            masks, lss, lrs = _sb_scores(q, kbs, diff, [(i - j) * SB_T for j in tiles])
            dwts = [_dg(g, vb, _NT) for vb in vbs]
            within = [_dg_exact_rhs(lr, upto, _NN) for lr in lrs]
            wtss, dls = [], []
            for t in range(n):
                wts = jnp.where(masks[t], jnp.exp(lss[t] + (ltot - (within[t] + plr))), 0.0)
                plr = plr + jnp.sum(lrs[t], axis=1, keepdims=True)
                wtss.append(wts)
                dls.append(dwts[t] * wts)
            dwithin = [_dg_exact_rhs(dl, before, _NN) for dl in dls]
            for t in range(n):
                sz = jnp.exp(lss[t])
                dz = jnp.where(masks[t], dls[t] * (1.0 - sz) - sz * (dwithin[t] + pdl), 0.0) * (HEAD_DIM ** -0.5)
                pdl = pdl + jnp.sum(dls[t], axis=1, keepdims=True)
                dk_ref[kss[t], :] += _dg(dz, q, _TN)
                dv_ref[kss[t], :] += _dg(wtss[t], g, _TN)
                dq = dq + _dg(dz, kbs[t], _NN)
            return plr, pdl, dq

        zero = jnp.zeros((SB_T, 1), F32)
        _, _, dq = _sb_loop(i + 1, step, (zero, zero, jnp.zeros((SB_T, HEAD_DIM), F32)))
        dq_ref[...] = dq

    tile = pl.BlockSpec((SB_T, LANE), lambda h, i: (i, h))
    colspec = pl.BlockSpec((s, LANE), lambda h, i: (0, h))
    return pl.pallas_call(
        body, name="sb_bwd", grid=(SB_HEADS, s // SB_T),
        in_specs=[tile, pl.BlockSpec((s, LANE), lambda h, i: (0, SB_HEADS + h)),
                  pl.BlockSpec((s, LANE), lambda h, i: (0, v_blk0 + h)), tile,
                  pl.BlockSpec((1, SB_T, 1), lambda h, i: (h, i, 0)), pl.BlockSpec(memory_space=pl.ANY)],
        out_specs=[tile, colspec, pl.BlockSpec((s, LANE), lambda h, i: (0, v_blk0 + h))],
        out_shape=[jax.ShapeDtypeStruct((s, MIX_WIDTH), F32), jax.ShapeDtypeStruct((s, MIX_WIDTH), F32),
                   jax.ShapeDtypeStruct(dproj_in.shape, F32)],
        input_output_aliases={5: 2}, compiler_params=_params("parallel", "arbitrary"),
    )(qkn, qkn, proj, do, ltot, dproj_in)


ROW_TILE = 256
HN_HEADS = 8


def head_norm_fwd(x, x_blk0, nblk, gains, out_dtype, out_width, name):
    s = x.shape[0]
    assert x_blk0 % HN_HEADS == 0 and nblk % (HN_HEADS * gains.shape[0]) == 0
    per = nblk // gains.shape[0] // HN_HEADS
    w = HN_HEADS * LANE

    def body(x_ref, g_ref, o_ref):
        gain = g_ref[0, 0:1, :]
        for j in range(HN_HEADS):
            cs = slice(j * LANE, (j + 1) * LANE)
            xv = x_ref[:, cs]
            r = lax.rsqrt(jnp.mean(xv * xv, axis=1, keepdims=True) + EPS)
            o_ref[:, cs] = (xv * r * gain).astype(out_dtype)

    return pl.pallas_call(
        body, name=name, grid=(nblk // HN_HEADS, s // ROW_TILE),
        in_specs=[pl.BlockSpec((ROW_TILE, w), lambda j, t: (t, x_blk0 // HN_HEADS + j)),
                  pl.BlockSpec((1, 8, LANE), lambda j, t: (j // per, 0, 0))],
        out_specs=pl.BlockSpec((ROW_TILE, w), lambda j, t: (t, j)),
        out_shape=jax.ShapeDtypeStruct((s, out_width), out_dtype), compiler_params=_params("parallel", "parallel"),
    )(x, gains)


def head_norm_bwd(dy, dy_blk0, x, x_blk0, nblk, gain, dst, dst_blk0, name):
    s = x.shape[0]

    def body(*refs):
        if dst is not None:
            dy_ref, x_ref, g_ref, _, dx_ref, dg_ref = refs
        else:
            dy_ref, x_ref, g_ref, dx_ref, dg_ref = refs

        @pl.when((pl.program_id(0) == 0) & (pl.program_id(1) == 0))
        def _():
            dg_ref[...] = jnp.zeros_like(dg_ref)

        gain = g_ref[0:1, :]
        dg = jnp.zeros((1, LANE), F32)
        for j in range(HN_HEADS):
            cs = slice(j * LANE, (j + 1) * LANE)
            xv = x_ref[:, cs]
            g = dy_ref[:, cs]
            r = lax.rsqrt(jnp.mean(xv * xv, axis=1, keepdims=True) + EPS)
            gy = g * gain
            dx_ref[:, cs] = r * gy - xv * (r * r * r) * jnp.mean(gy * xv, axis=1, keepdims=True)
            dg = dg + jnp.sum(g * xv * r, axis=0, keepdims=True)
        dg_ref[0:1, :] += dg

    assert dy_blk0 % HN_HEADS == 0 and x_blk0 % HN_HEADS == 0 and dst_blk0 % HN_HEADS == 0 and nblk % HN_HEADS == 0
    w = HN_HEADS * LANE
    in_specs = [pl.BlockSpec((ROW_TILE, w), lambda j, t: (t, dy_blk0 // HN_HEADS + j)),
                pl.BlockSpec((ROW_TILE, w), lambda j, t: (t, x_blk0 // HN_HEADS + j)),
                pl.BlockSpec((8, LANE), lambda j, t: (0, 0))]
    args = [dy, x, gain]
    aliases = {}
    if dst is not None:
        in_specs.append(pl.BlockSpec(memory_space=pl.ANY))
        args.append(dst)
        aliases = {3: 0}
        out0 = jax.ShapeDtypeStruct(dst.shape, F32)
    else:
        out0 = jax.ShapeDtypeStruct((s, (dst_blk0 + nblk) * LANE), F32)
    return pl.pallas_call(
        body, name=name, grid=(nblk // HN_HEADS, s // ROW_TILE), in_specs=in_specs,
        out_specs=[pl.BlockSpec((ROW_TILE, w), lambda j, t: (t, dst_blk0 // HN_HEADS + j)),
                   pl.BlockSpec((8, LANE), lambda j, t: (0, 0))],
        out_shape=[out0, jax.ShapeDtypeStruct((8, LANE), F32)],
        input_output_aliases=aliases, compiler_params=_params("arbitrary", "arbitrary"),
    )(*args)


def _xa_head(xq, kraw, v, qg, kg):
    q = xq * lax.rsqrt(jnp.mean(xq * xq, axis=1, keepdims=True) + EPS) * qg
    k = kraw * lax.rsqrt(jnp.mean(kraw * kraw, axis=1, keepdims=True) + EPS) * kg
    sc = mm_nt(q, k) * (XA_DIM ** -0.5)
    e = jnp.exp(sc - lax.stop_gradient(jnp.max(sc, axis=1, keepdims=True)))
    return mm(e / jnp.sum(e, axis=1, keepdims=True), v)


def xa_fwd(proj, xq_blk0, kv, qg, kg, cat):
    s = proj.shape[0]
    n_mem = kv.shape[0]

    def body(xq_ref, k_ref, v_ref, qg_ref, kg_ref, _, o_ref):
        o_ref[...] = _xa_head(xq_ref[...], k_ref[...], v_ref[...], qg_ref[0:1, :], kg_ref[0:1, :])

    gain = pl.BlockSpec((8, XA_DIM), lambda h, t: (0, 0))
    return pl.pallas_call(
        body, name="xa_fwd", grid=(XA_HEADS, s // ROW_TILE),
        in_specs=[pl.BlockSpec((ROW_TILE, XA_DIM), lambda h, t: (t, xq_blk0 // 2 + h)),
                  pl.BlockSpec((n_mem, XA_DIM), lambda h, t: (0, h)),
                  pl.BlockSpec((n_mem, XA_DIM), lambda h, t: (0, XA_HEADS + h)), gain, gain,
                  pl.BlockSpec(memory_space=pl.ANY)],
        out_specs=pl.BlockSpec((ROW_TILE, XA_DIM), lambda h, t: (t, MIX_WIDTH // XA_DIM + h)),
        out_shape=jax.ShapeDtypeStruct(cat.shape, F32), input_output_aliases={5: 0},
        compiler_params=_params("parallel", "parallel"),
    )(proj, kv, kv, qg, kg, cat)


def xa_bwd(proj, xq_blk0, kv, qg, kg, dcat, dproj_in):
    s = proj.shape[0]
    n_mem = kv.shape[0]

    def body(xq_ref, k_ref, v_ref, qg_ref, kg_ref, do_ref, _, dxq_ref, dk_ref, dv_ref, dqg_ref, dkg_ref):
        h = pl.program_id(0)
        t = pl.program_id(1)

        @pl.when(t == 0)
        def _():
            dk_ref[...] = jnp.zeros_like(dk_ref)
            dv_ref[...] = jnp.zeros_like(dv_ref)

        @pl.when((t == 0) & (h == 0))
        def _():
            dqg_ref[...] = jnp.zeros_like(dqg_ref)
            dkg_ref[...] = jnp.zeros_like(dkg_ref)

        _, vjp = jax.vjp(_xa_head, xq_ref[...], k_ref[...], v_ref[...], qg_ref[0:1, :], kg_ref[0:1, :])
        dxq, dk, dv, dqg, dkg = vjp(do_ref[...])
        dxq_ref[...] = dxq
        dk_ref[...] += dk
        dv_ref[...] += dv
        dqg_ref[0:1, :] += dqg
        dkg_ref[0:1, :] += dkg

    gain = pl.BlockSpec((8, XA_DIM), lambda h, t: (0, 0))
    kspec = pl.BlockSpec((n_mem, XA_DIM), lambda h, t: (0, h))
    vspec = pl.BlockSpec((n_mem, XA_DIM), lambda h, t: (0, XA_HEADS + h))
    return pl.pallas_call(
        body, name="xa_bwd", grid=(XA_HEADS, s // ROW_TILE),
        in_specs=[pl.BlockSpec((ROW_TILE, XA_DIM), lambda h, t: (t, xq_blk0 // 2 + h)), kspec, vspec, gain, gain,
                  pl.BlockSpec((ROW_TILE, XA_DIM), lambda h, t: (t, MIX_WIDTH // XA_DIM + h)),
                  pl.BlockSpec(memory_space=pl.ANY)],
        out_specs=[pl.BlockSpec((ROW_TILE, XA_DIM), lambda h, t: (t, xq_blk0 // 2 + h)), kspec, kspec, gain, gain],
        out_shape=[jax.ShapeDtypeStruct(dproj_in.shape, F32), jax.ShapeDtypeStruct((n_mem, XA_WIDTH), F32),
                   jax.ShapeDtypeStruct((n_mem, XA_WIDTH), F32), jax.ShapeDtypeStruct((8, XA_DIM), F32),
                   jax.ShapeDtypeStruct((8, XA_DIM), F32)],
        input_output_aliases={6: 0}, compiler_params=_params("arbitrary", "arbitrary"),
    )(proj, kv, kv, qg, kg, dcat, dproj_in)


GATE_ROWS = 1024


def gate_fwd(cat, proj, z_blk0):
    s = cat.shape[0]

    def body(c_ref, z_ref, y_ref):
        y_ref[...] = (c_ref[...] * _silu(z_ref[...])).astype(BF16)

    w = 2 * LANE
    rt = min(GATE_ROWS, s)
    return pl.pallas_call(
        body, name="gate_fwd", grid=(INNER // w, s // rt),
        in_specs=[pl.BlockSpec((rt, w), lambda j, t: (t, j)),
                  pl.BlockSpec((rt, w), lambda j, t: (t, z_blk0 // 2 + j))],
        out_specs=pl.BlockSpec((rt, w), lambda j, t: (t, j)),
        out_shape=jax.ShapeDtypeStruct((s, INNER), BF16), compiler_params=_params("parallel", "parallel"),
    )(cat, proj)


def gate_bwd(dy, cat, proj, z_blk0):
    s = cat.shape[0]

    def body(dy_ref, c_ref, z_ref, dc_ref, dz_ref):
        z = z_ref[...]
        g = dy_ref[...]
        dc_ref[...] = g * _silu(z)
        dz_ref[...] = g * c_ref[...] * _silu_grad(z)

    w = 2 * LANE
    rt = min(GATE_ROWS, s)
    tile = pl.BlockSpec((rt, w), lambda j, t: (t, j))
    ztile = pl.BlockSpec((rt, w), lambda j, t: (t, z_blk0 // 2 + j))
    return pl.pallas_call(
        body, name="gate_bwd", grid=(INNER // w, s // rt), in_specs=[tile, tile, ztile],
        out_specs=[tile, ztile],
        out_shape=[jax.ShapeDtypeStruct((s, INNER), F32), jax.ShapeDtypeStruct(proj.shape, F32)],
        compiler_params=_params("parallel", "parallel"),
    )(dy, cat, proj)


NORM_ROWS = 256


def rms_fwd(x, gain):
    s, d = x.shape

    def body(x_ref, g_ref, o_ref):
        xv = x_ref[...]
        r = lax.rsqrt(jnp.mean(xv * xv, axis=1, keepdims=True) + EPS)
        o_ref[...] = (xv * r * g_ref[0:1, :]).astype(BF16)

    return pl.pallas_call(
        body, name="rms_fwd", grid=(s // NORM_ROWS,),
        in_specs=[pl.BlockSpec((NORM_ROWS, d), lambda t: (t, 0)), pl.BlockSpec((8, d), lambda t: (0, 0))],
        out_specs=pl.BlockSpec((NORM_ROWS, d), lambda t: (t, 0)),
        out_shape=jax.ShapeDtypeStruct((s, d), BF16), compiler_params=_params("parallel"),
    )(x, gain)


def rms_bwd(dh, x, gain, dres):
    s, d = x.shape

    def body(*refs):
        if dres is not None:
            dh_ref, x_ref, g_ref, dr_ref, dx_ref, dg_ref = refs
        else:
            dh_ref, x_ref, g_ref, dx_ref, dg_ref = refs

        @pl.when(pl.program_id(0) == 0)
        def _():
            dg_ref[...] = jnp.zeros_like(dg_ref)

        xv = x_ref[...]
        g = dh_ref[...]
        r = lax.rsqrt(jnp.mean(xv * xv, axis=1, keepdims=True) + EPS)
        gy = g * g_ref[0:1, :]
        dx = r * gy - xv * (r * r * r) * jnp.mean(gy * xv, axis=1, keepdims=True)
        dx_ref[...] = dx + dr_ref[...] if dres is not None else dx
        dg_ref[0:1, :] += jnp.sum(g * xv * r, axis=0, keepdims=True)

    tile = pl.BlockSpec((NORM_ROWS, d), lambda t: (t, 0))
    gspec = pl.BlockSpec((8, d), lambda t: (0, 0))
    args = [dh, x, gain] + ([dres] if dres is not None else [])
    return pl.pallas_call(
        body, name="rms_bwd", grid=(s // NORM_ROWS,),
        in_specs=[tile, tile, gspec] + ([tile] if dres is not None else []),
        out_specs=[tile, gspec],
        out_shape=[jax.ShapeDtypeStruct((s, d), F32), jax.ShapeDtypeStruct((8, d), F32)],
        compiler_params=_params("arbitrary"),
    )(*args)


def loss_fwd_bwd(y, target):
    s, d = y.shape

    def body(y_ref, t_ref, l_ref, dy_ref):
        @pl.when(pl.program_id(0) == 0)
        def _():
            l_ref[...] = jnp.zeros_like(l_ref)

        err = y_ref[...] - t_ref[...]
        dy_ref[...] = err * (1.0 / d)
        part = 0.5 * jnp.sum(jnp.mean(err * err, axis=1, keepdims=True), axis=0, keepdims=True)
        r = lax.broadcasted_iota(jnp.int32, (8, LANE), 0)
        c = lax.broadcasted_iota(jnp.int32, (8, LANE), 1)
        l_ref[...] += jnp.where((r == 0) & (c == 0), part, 0.0)

    tile = pl.BlockSpec((NORM_ROWS, d), lambda t: (t, 0))
    return pl.pallas_call(
        body, name="loss", grid=(s // NORM_ROWS,), in_specs=[tile, tile],
        out_specs=[pl.BlockSpec((8, LANE), lambda t: (0, 0)), tile],
        out_shape=[jax.ShapeDtypeStruct((8, LANE), F32), jax.ShapeDtypeStruct((s, d), F32)],
        compiler_params=_params("arbitrary"),
    )(y, target)


def matmul(a, b, mode, out_dtype, tm, tn, tk, name, add=None, b_blocked=False, out_blocks=None, exchange=None):
    if b_blocked:
        nb, _, width = b.shape
        bshape = (b.shape[1], nb * width)
    else:
        bshape = b.shape
    if mode == "tn":
        (kdim, m), n = a.shape, bshape[1]
    else:
        (m, kdim), n = a.shape, (bshape[1] if mode == "nn" else bshape[0])
    tm, tn, tk = min(tm, m), min(tn, n), min(tk, kdim)
    assert m % tm == 0 and n % tn == 0 and kdim % tk == 0, (name, m, n, kdim)
    nk = kdim // tk
    dims = {"nn": _NN, "nt": _NT, "tn": _TN}[mode]

    def body(*refs):
        if add is not None:
            a_ref, b_ref, add_ref, o_ref, acc_ref = refs
        else:
            a_ref, b_ref, o_ref, acc_ref = refs
        k = pl.program_id(2)

        @pl.when(k == 0)
        def _():
            acc_ref[...] = jnp.zeros_like(acc_ref)

        acc_ref[...] += _dg(a_ref[...], b_ref[...], dims)

        @pl.when(k == nk - 1)
        def _():
            r = acc_ref[...]
            if add is not None:
                r = r + add_ref[...]
            o_ref[...] = r.astype(out_dtype)

    a_spec = pl.BlockSpec((tk, tm), lambda i, j, k: (k, i)) if mode == "tn" else pl.BlockSpec((tm, tk), lambda i, j, k: (i, k))
    if b_blocked and mode == "nn":
        per = width // tn
        assert width % tn == 0
        b_spec = pl.BlockSpec((None, tk, tn), lambda i, j, k: (j // per, k, j % per))
    elif b_blocked and mode == "nt":
        per = width // tk
        assert width % tk == 0
        b_spec = pl.BlockSpec((None, tn, tk), lambda i, j, k: (k // per, j, k % per))
    elif mode == "nt":
        b_spec = pl.BlockSpec((tn, tk), lambda i, j, k: (j, k))
    else:
        assert not b_blocked
        b_spec = pl.BlockSpec((tk, tn), lambda i, j, k: (k, j))
    add_spec = pl.BlockSpec((tm, tn), lambda i, j, k: (i, j))
    if out_blocks is not None:
        operb = (n // out_blocks) // tn
        assert (n // out_blocks) % tn == 0 and add is None
        o_spec = pl.BlockSpec((None, tm, tn), lambda i, j, k: (j // operb, i, j % operb))
        out_shape = jax.ShapeDtypeStruct((out_blocks, m, n // out_blocks), out_dtype)
    else:
        o_spec = add_spec
        out_shape = jax.ShapeDtypeStruct((m, n), out_dtype)
    res = hosted_call(
        body, exchange, name=name, grid=(m // tm, n // tn, nk),
        in_specs=[a_spec, b_spec] + ([add_spec] if add is not None else []), out_specs=[o_spec],
        out_shape=[out_shape], scratch_shapes=[pltpu.VMEM((tm, tn), F32)],
        sem=("parallel", "parallel", "arbitrary"),
    )(*([a, b] + ([add] if add is not None else [])))
    return res[0] if exchange is None else res


_HBM = pl.BlockSpec(memory_space=pltpu.HBM)


def _me():
    return lax.axis_index("x"), lax.axis_index("y"), lax.axis_index("c")


def _flat(p):
    return 4 * p[0] + 2 * p[1] + p[2]


def _flip(p, r):
    return tuple((1 - v) if (r >> (2 - a)) & 1 else v for a, v in enumerate(p))


class Exchange:
    def __init__(self, srcs, out_shapes, sems, start, finish):
        self.srcs, self.out_shapes, self.sems = list(srcs), list(out_shapes), list(sems)
        self.start, self.finish = start, finish


def hosted_call(body, exchange, *, name, grid, in_specs, out_specs, out_shape, scratch_shapes=(),
                input_output_aliases=None, sem=()):
    in_specs, out_specs, out_shape = list(in_specs), list(out_specs), list(out_shape)
    scratch_shapes = list(scratch_shapes)
    aliases = input_output_aliases or {}
    if exchange is None:
        call = pl.pallas_call(body, name=name, grid=grid, in_specs=in_specs, out_specs=out_specs, out_shape=out_shape,
                              scratch_shapes=scratch_shapes, input_output_aliases=aliases, compiler_params=_params(*sem))
        return lambda *args: list(call(*args))
    ni, no, ns = len(in_specs), len(out_specs), len(scratch_shapes)
    xi, xo = len(exchange.srcs), len(exchange.out_shapes)

    def wrapped(*refs):
        ins, refs = refs[:ni], refs[ni:]
        xin, refs = refs[:xi], refs[xi:]
        outs, refs = refs[:no], refs[no:]
        xout, refs = refs[:xo], refs[xo:]
        scr, xsem = refs[:ns], refs[ns:]
        first = functools.reduce(lambda p, q: p & q, [pl.program_id(d) == 0 for d in range(len(grid))])
        last = functools.reduce(lambda p, q: p & q, [pl.program_id(d) == grid[d] - 1 for d in range(len(grid))])

        @pl.when(first)
        def _():
            exchange.start(xin, xout, xsem)

        body(*ins, *outs, *scr)

        @pl.when(last)
        def _():
            exchange.finish(xin, xout, xsem)

    call = pl.pallas_call(
        wrapped, name=name, grid=grid, in_specs=in_specs + [_HBM] * xi, out_specs=out_specs + [_HBM] * xo,
        out_shape=out_shape + exchange.out_shapes, scratch_shapes=scratch_shapes + exchange.sems,
        input_output_aliases=aliases, compiler_params=_params(*(("arbitrary",) * len(grid))))
    return lambda *args: list(call(*args, *exchange.srcs))


def run_exchange(exchange, name):
    xi, xo = len(exchange.srcs), len(exchange.out_shapes)

    def body(*refs):
        exchange.start(refs[:xi], refs[xi:xi + xo], refs[xi + xo:])
        exchange.finish(refs[:xi], refs[xi:xi + xo], refs[xi + xo:])

    return list(pl.pallas_call(body, name=name, in_specs=[_HBM] * xi, out_specs=[_HBM] * xo,
                               out_shape=exchange.out_shapes, scratch_shapes=exchange.sems)(*exchange.srcs))


def gather_exchange(shards):
    n = len(shards)

    def parts(srcs, outs, sems):
        send_sems, recv_sems, local_sems = sems
        me = _me()
        x, y, c = me
        chips = [(1 - x, y), (x, 1 - y), (1 - x, 1 - y)]

        def copy(a, k, block, to, src=None):
            dst = outs[a].at[_flat(block)]
            return pltpu.make_async_remote_copy(src_ref=dst if src is None else src, dst_ref=dst,
                                                send_sem=send_sems.at[a, k], recv_sem=recv_sems.at[a, k],
                                                device_id=to, device_id_type=MESH)

        mine = [pltpu.make_async_copy(srcs[a], outs[a].at[_flat(me)], local_sems.at[a]) for a in range(n)]
        own = []
        for a in range(n):
            own.append(copy(a, 0, me, (x, y, 1 - c), src=srcs[a]))
            own += [copy(a, 1 + j, me, (*chip, c), src=srcs[a]) for j, chip in enumerate(chips)]
        return me, chips, copy, mine, own

    def start(srcs, outs, sems):
        _, _, _, mine, own = parts(srcs, outs, sems)
        for cp in mine + own:
            cp.start()

    def finish(srcs, outs, sems):
        me, chips, copy, mine, own = parts(srcs, outs, sems)
        x, y, c = me
        passed = []
        for j, chip in enumerate(chips):
            for a in range(n):
                copy(a, 1 + j, (*chip, c), me).wait_recv()
                fwd = copy(a, 4 + j, (*chip, c), (x, y, 1 - c))
                fwd.start()
                passed.append(fwd)
        for a in range(n):
            copy(a, 0, (x, y, 1 - c), me).wait_recv()
            for j, chip in enumerate(chips):
                copy(a, 4 + j, (*chip, 1 - c), me).wait_recv()
        for cp in own + passed:
            cp.wait_send()
        for cp in mine:
            cp.wait()

    dma = pltpu.SemaphoreType.DMA
    return Exchange(shards, [jax.ShapeDtypeStruct((N_DEV,) + s.shape, s.dtype) for s in shards],
                    [dma((n, 7)), dma((n, 7)), dma((n,))], start, finish)


def pair_exchange(srcs):
    n = len(srcs)

    def copies(srcs_, outs, sems):
        send_sems, recv_sems = sems
        x, y, c = _me()
        return [pltpu.make_async_remote_copy(src_ref=srcs_[a].at[:, 1 - c], dst_ref=outs[a], send_sem=send_sems.at[a],
                                             recv_sem=recv_sems.at[a], device_id=(x, y, 1 - c), device_id_type=MESH)
                for a in range(n)]

    def start(srcs_, outs, sems):
        for cp in copies(srcs_, outs, sems):
            cp.start()

    def finish(srcs_, outs, sems):
        for cp in copies(srcs_, outs, sems):
            cp.wait()

    dma = pltpu.SemaphoreType.DMA
    return Exchange(srcs, [jax.ShapeDtypeStruct((4,) + s.shape[2:], s.dtype) for s in srcs], [dma((n,)), dma((n,))],
                    start, finish)


def pair_sum(src, half, name):
    _, _, rows, cols = src.shape
    tr = min(rows, 256)

    def body(x_ref, h_ref, o_ref):
        c = lax.axis_index("c")
        o_ref[0] = (x_ref[0, c].astype(F32) + h_ref[0].astype(F32)).astype(BF16)

    return pl.pallas_call(
        body, name=name, grid=(4, rows // tr),
        in_specs=[pl.BlockSpec((1, 2, tr, cols), lambda ch, t: (ch, 0, t, 0)),
                  pl.BlockSpec((1, tr, cols), lambda ch, t: (ch, t, 0))],
        out_specs=pl.BlockSpec((1, tr, cols), lambda ch, t: (ch, t, 0)),
        out_shape=jax.ShapeDtypeStruct(half.shape, BF16), compiler_params=_params("parallel", "parallel"),
    )(src, half)


def chip_exchange(parts, slots, recv_shapes):
    n = len(parts)

    def plan(srcs, outs, sems):
        send_sems, recv_sems, local_sems = sems
        x, y, c = _me()
        chip = 2 * x + y
        mine = [pltpu.make_async_copy(srcs[a].at[chip], outs[slots[a][0]].at[chip, slots[a][1]], local_sems.at[a])
                for a in range(n)]
        sends, arrivals = [], []
        for r in (1, 2, 3):
            px = (1 - x) if r & 2 else x
            py = (1 - y) if r & 1 else y
            for a in range(n):
                ri, layer = slots[a]
                sends.append(pltpu.make_async_remote_copy(
                    src_ref=srcs[a].at[2 * px + py], dst_ref=outs[ri].at[chip, layer], send_sem=send_sems.at[a, r - 1],
                    recv_sem=recv_sems.at[a, r - 1], device_id=(px, py, c), device_id_type=MESH))
                land = outs[ri].at[2 * px + py, layer]
                arrivals.append(pltpu.make_async_remote_copy(
                    src_ref=land, dst_ref=land, send_sem=send_sems.at[a, r - 1], recv_sem=recv_sems.at[a, r - 1],
                    device_id=(px, py, c), device_id_type=MESH))
        return mine, sends, arrivals

    def start(srcs, outs, sems):
        mine, sends, _ = plan(srcs, outs, sems)
        for cp in mine + sends:
            cp.start()

    def finish(srcs, outs, sems):
        mine, sends, arrivals = plan(srcs, outs, sems)
        for cp in arrivals:
            cp.wait_recv()
        for cp in sends:
            cp.wait_send()
        for cp in mine:
            cp.wait()

    dma = pltpu.SemaphoreType.DMA
    return Exchange(parts, [jax.ShapeDtypeStruct(s, BF16) for s in recv_shapes],
                    [dma((n, 3)), dma((n, 3)), dma((n,))], start, finish)


SMALL_ROWS = 24


def all_reduce_small(pack):
    def body(p_ref, o_ref, buf, send_sems, recv_sems):
        me = _me()
        buf[_flat(me)] = p_ref[...]
        sent = []
        for r in range(1, N_DEV):
            peer = _flip(me, r)
            cp = pltpu.make_async_remote_copy(src_ref=p_ref, dst_ref=buf.at[_flat(me)], send_sem=send_sems.at[r - 1],
                                              recv_sem=recv_sems.at[r - 1], device_id=peer, device_id_type=MESH)
            cp.start()
            sent.append(cp)
        for r in range(1, N_DEV):
            peer = _flip(me, r)
            land = buf.at[_flat(peer)]
            pltpu.make_async_remote_copy(src_ref=land, dst_ref=land, send_sem=send_sems.at[r - 1],
                                         recv_sem=recv_sems.at[r - 1], device_id=peer, device_id_type=MESH).wait_recv()
        for cp in sent:
            cp.wait_send()
        acc = buf[0]
        for d in range(1, N_DEV):
            acc = acc + buf[d]
        o_ref[...] = acc

    vm = pl.BlockSpec(memory_space=pltpu.VMEM)
    return pl.pallas_call(
        body, name="all_reduce_small", in_specs=[vm], out_specs=vm,
        out_shape=jax.ShapeDtypeStruct(pack.shape, F32),
        scratch_shapes=[pltpu.VMEM((N_DEV,) + pack.shape, F32), pltpu.SemaphoreType.DMA((7,)),
                        pltpu.SemaphoreType.DMA((7,))],
    )(pack)


def _adamw(w, g, m, v):
    m = ADAM_B1 * m + (1.0 - ADAM_B1) * g
    v = ADAM_B2 * v + (1.0 - ADAM_B2) * (g * g)
    m_hat = m / (1.0 - ADAM_B1 ** ADAM_STEP)
    v_hat = v / (1.0 - ADAM_B2 ** ADAM_STEP)
    delta = -ADAM_LR * (m_hat / (jnp.sqrt(v_hat) + ADAM_EPS) + ADAM_WD * w)
    return delta, m, v


ADAM_ROWS = 128


def reduce_adamw(recv, w, m, v, name):
    nl, rows, cols = w.shape
    nslot, cp = recv.shape[0], recv.shape[3]

    def body(r_ref, w_ref, m_ref, v_ref, g_ref, d_ref, mo_ref, vo_ref):
        g = r_ref[0, 0].astype(F32)
        for slot in range(1, nslot):
            g = g + r_ref[slot, 0].astype(F32)
        if cp != cols:
            g = g[:, :cols]
        delta, m_new, v_new = _adamw(w_ref[0], g, m_ref[0], v_ref[0])
        g_ref[0] = g
        d_ref[0] = delta
        mo_ref[0] = m_new
        vo_ref[0] = v_new

    tile = pl.BlockSpec((1, ADAM_ROWS, cols), lambda l, t: (l, t, 0))
    out = jax.ShapeDtypeStruct(w.shape, F32)
    return pl.pallas_call(
        body, name=name, grid=(nl, rows // ADAM_ROWS),
        in_specs=[pl.BlockSpec((nslot, 1, ADAM_ROWS, cp), lambda l, t: (0, l, t, 0)), tile, tile, tile],
        out_specs=[tile, tile, tile, tile], out_shape=[out, out, out, out],
        compiler_params=_params("parallel", "parallel"),
    )(recv, w, m, v)


def adamw_small(g, w, m, v):
    def body(g_ref, w_ref, m_ref, v_ref, d_ref, mo_ref, vo_ref):
        d_ref[...], mo_ref[...], vo_ref[...] = _adamw(w_ref[...], g_ref[...], m_ref[...], v_ref[...])

    out = jax.ShapeDtypeStruct(g.shape, F32)
    return pl.pallas_call(body, name="adamw_small", out_shape=[out, out, out])(g, w, m, v)


def _row8(v):
    return jnp.pad(v.reshape(1, -1).astype(F32), ((0, 7), (0, 0)))


def _row8_lanes(v, width=LANE):
    return jnp.pad(v.reshape(1, -1).astype(F32), ((0, 7), (0, width - v.size)))


def _pack_rows(parts):
    rows = []
    for p in parts:
        p = p.reshape(-1).astype(F32)
        nrow = -(-p.size // D_MODEL)
        rows.append(jnp.pad(p, (0, nrow * D_MODEL - p.size)).reshape(nrow, D_MODEL))
    out = jnp.concatenate(rows, axis=0)
    return jnp.pad(out, ((0, SMALL_ROWS - out.shape[0]), (0, 0)))


def _unpack_rows(pack, shapes):
    out, r = [], 0
    for shp in shapes:
        size = 1
        for d in shp:
            size *= d
        nrow = -(-size // D_MODEL)
        out.append(pack[r:r + nrow].reshape(-1)[:size].reshape(shp))
        r += nrow
    return out


def _dn_weight_layout(gathered):
    full = jnp.transpose(gathered[:, :, :DN_SHARD], (1, 0, 2)).reshape(D_MODEL, DN_PROJ)
    n_ab = 2 * DN_V_HEADS
    zeros = jnp.zeros((D_MODEL, 2 * LANE - n_ab), full.dtype)
    return jnp.concatenate([full[:, :DN_QKV + n_ab], zeros, full[:, DN_QKV + n_ab:]], axis=1)


def _dn_grad_blocks(dw):
    n_ab = 2 * DN_V_HEADS
    full = jnp.concatenate([dw[:, :DN_QKV + n_ab], dw[:, DN_QKV + 2 * LANE:]], axis=1)
    blocks = jnp.transpose(full.reshape(D_MODEL, N_DEV, DN_SHARD), (1, 0, 2))
    return jnp.pad(blocks, ((0, 0), (0, 0), (0, DN_SHARD_PAD - DN_SHARD)))


def kernel(x, mem, norm_g, mem_norm_g, mem_w_kv, xa_q_norm_g, xa_k_norm_g, w_out, dn_w_in, dn_conv_w, dn_a_log, dn_dt_bias, dn_out_norm_g, sb_w_in, sb_q_norm_g, sb_k_norm_g, loss_target, m_norm_g, m_mem_norm_g, m_mem_w_kv, m_xa_q_norm_g, m_xa_k_norm_g, m_w_out, m_dn_w_in, m_dn_conv_w, m_dn_a_log, m_dn_dt_bias, m_dn_out_norm_g, m_sb_w_in, m_sb_q_norm_g, m_sb_k_norm_g, v_norm_g, v_mem_norm_g, v_mem_w_kv, v_xa_q_norm_g, v_xa_k_norm_g, v_w_out, v_dn_w_in, v_dn_conv_w, v_dn_a_log, v_dn_dt_bias, v_dn_out_norm_g, v_sb_w_in, v_sb_q_norm_g, v_sb_k_norm_g):
    x0, memv, target = x[0], mem[0], loss_target[0]
    my_dev = 4 * lax.axis_index("x") + 2 * lax.axis_index("y") + lax.axis_index("c")

    dn_shard = jnp.pad(dn_w_in[0].astype(BF16), ((0, 0), (0, DN_SHARD_PAD - DN_SHARD)))
    w_out_b = w_out.astype(BF16)
    w_kv_b = mem_w_kv.astype(BF16)
    conv_shard = jnp.pad(dn_conv_w[0], ((0, 4), (0, 0)))
    g_dn, g_conv = run_exchange(gather_exchange([dn_shard, conv_shard]), "gather_first")
    w_dn = _dn_weight_layout(g_dn)
    conv_w = jnp.transpose(g_conv, (1, 0, 2)).reshape(8, DN_QKV)

    ng = [_row8(norm_g[0]), _row8(norm_g[1])]
    mem_g = _row8(mem_norm_g)
    xqg = [_row8(xa_q_norm_g[0]), _row8(xa_q_norm_g[1])]
    xkg = [_row8(xa_k_norm_g[0]), _row8(xa_k_norm_g[1])]
    alog, dtb = _row8_lanes(dn_a_log[0]), _row8_lanes(dn_dt_bias[0])
    out_g, sbq_g, sbk_g = _row8(dn_out_norm_g[0]), _row8(sb_q_norm_g[0]), _row8(sb_k_norm_g[0])

    mem_n = rms_fwd(memv, mem_g)
    h0 = rms_fwd(x0, ng[0])
    proj0 = matmul(h0, w_dn, "nn", F32, 1024, 1152, 2048, "proj_dn")
    act = dn_conv_fwd(proj0, conv_w)
    u0, w_, qd, kd, qk, cd, g_wo0, g_kv0 = dn_prep_fwd(
        act, proj0, alog, dtb, exchange=gather_exchange([w_out_b[0], w_kv_b[0]]))
    o_raw, states, w_sb = dn_scan_fwd(u0, w_, qd, kd, qk, cd, MIX_WIDTH,
                                      exchange=gather_exchange([sb_w_in[0].astype(BF16)]))
    w_o = [g_wo0.reshape(INNER, D_MODEL), None]
    w_kv = [g_kv0.reshape(D_MODEL, 2 * XA_WIDTH), None]
    kv = [matmul(mem_n, w_kv[0], "nn", F32, 256, 1024, 2048, "kv0"), None]
    cat0 = head_norm_fwd(o_raw, 0, DN_V_HEADS, out_g[None], F32, INNER, "dn_out_norm")
    cat0 = xa_fwd(proj0, DN_XQ_BLK, kv[0], xqg[0], xkg[0], cat0)
    y0 = gate_fwd(cat0, proj0, DN_Z_BLK)
    x1 = matmul(y0, w_o[0], "nn", F32, 1024, 1024, 2048, "out_proj0", add=x0)

    h1 = rms_fwd(x1, ng[1])
    proj1 = matmul(h1, w_sb, "nn", F32, 1024, 896, 2048, "proj_sb", b_blocked=True)
    qkn = head_norm_fwd(proj1, 0, 2 * SB_HEADS, jnp.stack([sbq_g, sbk_g]), BF16, 2 * MIX_WIDTH, "sb_qk_norm")
    cat1, ltot, g_wo1, g_kv1 = sb_fwd(qkn, proj1, 2 * SB_HEADS, INNER,
                                      exchange=gather_exchange([w_out_b[1], w_kv_b[1]]))
    w_o[1] = g_wo1.reshape(INNER, D_MODEL)
    w_kv[1] = g_kv1.reshape(D_MODEL, 2 * XA_WIDTH)
    kv[1] = matmul(mem_n, w_kv[1], "nn", F32, 256, 1024, 2048, "kv1")
    cat1 = xa_fwd(proj1, SB_XQ_BLK, kv[1], xqg[1], xkg[1], cat1)
    y1 = gate_fwd(cat1, proj1, SB_Z_BLK)
    x2 = matmul(y1, w_o[1], "nn", F32, 1024, 1024, 2048, "out_proj1", add=x1)
    loss_part, dx2 = loss_fwd_bwd(x2, target)

    dy1 = matmul(dx2, w_o[1], "nt", F32, 1024, 1024, 2048, "d_y1")
    dw_o1 = matmul(y1, dx2, "tn", BF16, 1024, 1024, 1024, "d_w_out1")
    dcat1, dproj1 = gate_bwd(dy1, cat1, proj1, SB_Z_BLK)
    dproj1, dxk1, dxv1, dxqg1, dxkg1 = xa_bwd(proj1, SB_XQ_BLK, kv[1], xqg[1], xkg[1], dcat1, dproj1)
    dqn, dkn, dproj1 = sb_bwd(qkn, proj1, 2 * SB_HEADS, dcat1, ltot, dproj1)
    dproj1, d_sbq = head_norm_bwd(dqn, 0, proj1, 0, SB_HEADS, sbq_g, dproj1, 0, "sb_q_norm_bwd")
    dproj1, d_sbk = head_norm_bwd(dkn, 0, proj1, SB_HEADS, SB_HEADS, sbk_g, dproj1, SB_HEADS, "sb_k_norm_bwd")
    dw_sb = matmul(h1, dproj1, "tn", BF16, 1024, 896, 1024, "d_w_sb", out_blocks=N_DEV)
    dh1 = matmul(dproj1, w_sb, "nt", F32, 1024, 1024, 896, "d_h1", b_blocked=True)
    dx1, d_ng1 = rms_bwd(dh1, x1, ng[1], dx2)

    dy0 = matmul(dx1, w_o[0], "nt", F32, 1024, 1024, 2048, "d_y0")
    dw_o0 = matmul(y0, dx1, "tn", BF16, 1024, 1024, 1024, "d_w_out0")
    dcat0, dproj0 = gate_bwd(dy0, cat0, proj0, DN_Z_BLK)
    dproj0, dxk0, dxv0, dxqg0, dxkg0 = xa_bwd(proj0, DN_XQ_BLK, kv[0], xqg[0], xkg[0], dcat0, dproj0)

    dkv = [jnp.concatenate([dxk0, dxv0], axis=1), jnp.concatenate([dxk1, dxv1], axis=1)]
    dw_kv = [matmul(mem_n, dkv[i], "tn", BF16, 1024, 1024, 256, f"d_w_kv{i}") for i in range(2)]
    dmem_n = matmul(dkv[0], w_kv[0], "nt", F32, 256, 1024, 2048, "d_mem_n0")
    dmem_n = matmul(dkv[1], w_kv[1], "nt", F32, 256, 1024, 2048, "d_mem_n1", add=dmem_n)
    _, d_memg = rms_bwd(dmem_n, memv, mem_g, None)

    by_owner = lambda g, rows: g.reshape(4, 2, rows, g.size // (N_DEV * rows))
    grads = [dw_sb.reshape(4, 2, D_MODEL, SB_PROJ // N_DEV), by_owner(dw_o0, INNER // N_DEV),
             by_owner(dw_o1, INNER // N_DEV), by_owner(dw_kv[0], D_MODEL // N_DEV), by_owner(dw_kv[1], D_MODEL // N_DEV)]
    do_raw, d_outg = head_norm_bwd(dcat0, 0, o_raw, 0, DN_V_HEADS, out_g, None, 0, "dn_out_norm_bwd")
    du0, dw_, dqd, dkd, dqk, dcd, *halves = dn_scan_bwd(do_raw, u0, w_, qd, kd, qk, cd, states,
                                                         exchange=pair_exchange(grads))
    sums = [pair_sum(g, h, f"pair_sum{i}") for i, (g, h) in enumerate(zip(grads, halves))]
    to_chips = chip_exchange(sums, [(0, 0), (1, 0), (1, 1), (2, 0), (2, 1)],
                             [(4, 1, D_MODEL, SB_PROJ // N_DEV), (4, 2, INNER // N_DEV, D_MODEL),
                              (4, 2, D_MODEL // N_DEV, 2 * XA_WIDTH)])
    dq_a, dk_a, dv_a, dab, d_alog, d_dtb, recv_sb, recv_wo, recv_kv = dn_prep_bwd(
        act, proj0, alog, dtb, du0, dw_, dqd, dkd, dqk, dcd, exchange=to_chips)
    dproj0, dcw_q = dn_conv_bwd(dq_a, proj0, conv_w, dproj0, None, 0, "dn_conv_bwd_q")
    dproj0, dcw_k = dn_conv_bwd(dk_a, proj0, conv_w, dproj0, None, DN_QK_HEADS, "dn_conv_bwd_k")
    dproj0, dcw_v = dn_conv_bwd(dv_a, proj0, conv_w, dproj0, dab, 2 * DN_QK_HEADS, "dn_conv_bwd_v")
    dw_dn = matmul(h0, dproj0, "tn", BF16, 1024, 1152, 1024, "d_w_dn")
    dn_grad = _dn_grad_blocks(dw_dn).reshape(4, 2, D_MODEL, DN_SHARD_PAD)
    dh0, dn_half = matmul(dproj0, w_dn, "nt", F32, 1024, 1024, 1152, "d_h0", exchange=pair_exchange([dn_grad]))
    grad_x, d_ng0 = rms_bwd(dh0, x0, ng[0], dx1)
    dn_sum = pair_sum(dn_grad, dn_half, "pair_sum_dn")
    recv_dn, = run_exchange(chip_exchange([dn_sum], [(0, 0)], [(4, 1, D_MODEL, DN_SHARD_PAD)]), "reduce_dn")

    big = {
        "dn_w_in": reduce_adamw(recv_dn, dn_w_in, m_dn_w_in, v_dn_w_in, "adamw_dn_w_in"),
        "sb_w_in": reduce_adamw(recv_sb, sb_w_in, m_sb_w_in, v_sb_w_in, "adamw_sb_w_in"),
        "w_out": reduce_adamw(recv_wo, w_out, m_w_out, v_w_out, "adamw_w_out"),
        "mem_w_kv": reduce_adamw(recv_kv, mem_w_kv, m_mem_w_kv, v_mem_w_kv, "adamw_mem_w_kv"),
    }

    dconv = jnp.concatenate([dcw_q, dcw_k, dcw_v], axis=1)[:4]
    small_shapes = [(2, D_MODEL), (D_MODEL,), (2, XA_DIM), (2, XA_DIM), (4, DN_QKV), (1, DN_V_HEADS),
                    (1, DN_V_HEADS), (1, HEAD_DIM), (1, HEAD_DIM), (1, HEAD_DIM), (1,)]
    pack = _pack_rows([jnp.stack([d_ng0[0], d_ng1[0]]), d_memg[0], jnp.stack([dxqg0[0], dxqg1[0]]),
                       jnp.stack([dxkg0[0], dxkg1[0]]), dconv, d_alog[0, :DN_V_HEADS], d_dtb[0, :DN_V_HEADS],
                       d_outg[0], d_sbq[0], d_sbk[0], loss_part[0, :1]])
    total = all_reduce_small(pack)
    (g_norm, g_memn, g_xq, g_xk, g_conv_full, g_alog, g_dtb, g_outn, g_sbq, g_sbk, loss1) = _unpack_rows(total, small_shapes)
    conv_cols = DN_QKV // N_DEV
    g_conv = lax.dynamic_slice(g_conv_full, (0, my_dev * conv_cols), (4, conv_cols))[None]
    names = ["norm_g", "mem_norm_g", "xa_q_norm_g", "xa_k_norm_g", "dn_conv_w", "dn_a_log", "dn_dt_bias",
             "dn_out_norm_g", "sb_q_norm_g", "sb_k_norm_g"]
    grads = [g_norm, g_memn, g_xq, g_xk, g_conv, g_alog, g_dtb, g_outn, g_sbq, g_sbk]
    ws = [norm_g, mem_norm_g, xa_q_norm_g, xa_k_norm_g, dn_conv_w, dn_a_log, dn_dt_bias, dn_out_norm_g, sb_q_norm_g,
          sb_k_norm_g]
    ms = [m_norm_g, m_mem_norm_g, m_xa_q_norm_g, m_xa_k_norm_g, m_dn_conv_w, m_dn_a_log, m_dn_dt_bias,
          m_dn_out_norm_g, m_sb_q_norm_g, m_sb_k_norm_g]
    vs = [v_norm_g, v_mem_norm_g, v_xa_q_norm_g, v_xa_k_norm_g, v_dn_conv_w, v_dn_a_log, v_dn_dt_bias,
          v_dn_out_norm_g, v_sb_q_norm_g, v_sb_k_norm_g]
    shapes = [w.shape for w in ws]
    d_p, m_p, v_p = adamw_small(_pack_rows(grads), _pack_rows(ws), _pack_rows(ms), _pack_rows(vs))
    small = dict(zip(names, zip(grads, _unpack_rows(d_p, shapes), _unpack_rows(m_p, shapes), _unpack_rows(v_p, shapes))))

    order = ["norm_g", "mem_norm_g", "mem_w_kv", "xa_q_norm_g", "xa_k_norm_g", "w_out", "dn_w_in", "dn_conv_w",
             "dn_a_log", "dn_dt_bias", "dn_out_norm_g", "sb_w_in", "sb_q_norm_g", "sb_k_norm_g"]
    res = {**big, **small}
    outs = [loss1.reshape(()), grad_x[None]]
    for k in range(4):
        outs += [res[n][k] for n in order]
    return tuple(outs)
```

```python
import functools

import jax
import jax.numpy as jnp
from jax import lax
from jax.experimental import pallas as pl
from jax.experimental.pallas import tpu as pltpu

F32 = jnp.float32
BF16 = jnp.bfloat16

D_MODEL = 2048
SEQ = 2048
N_MEM = 256
INNER = 4096
XA_HEADS = 4
XA_WIDTH = 1024
XA_DIM = 256
MIX_WIDTH = 3072
HEAD_DIM = 128
DN_V_HEADS = 24
DN_QK_HEADS = 12
DN_QK_WIDTH = 1536
DN_CHUNK = 64
DN_QKV = 2 * DN_QK_WIDTH + MIX_WIDTH
DN_PROJ = 11312
SB_HEADS = 24
SB_PROJ = 14336
EPS = 1e-6
N_DEV = 8
DN_SHARD = DN_PROJ // N_DEV
DN_SHARD_PAD = 1536
LANE = 128
DN_COLS = 90 * LANE
DN_AB_BLK, DN_PAD_BLK, DN_XQ_BLK, DN_Z_BLK = 48, 49, 50, 58
SB_XQ_BLK, SB_Z_BLK = 72, 80

ADAM_LR, ADAM_B1, ADAM_B2, ADAM_EPS, ADAM_WD, ADAM_STEP = 0.001, 0.9, 0.999, 1e-08, 0.01, 10

VMEM_LIMIT = 56 * 1024 * 1024
MESH = pl.DeviceIdType.MESH

_NN, _NT, _TN = "nn", "nt", "tn"


def _dims(mode, rank):
    lhs, rhs = {"nn": (1, 0), "nt": (1, 1), "tn": (0, 0)}[mode]
    if rank == 2:
        return (((lhs,), (rhs,)), ((), ()))
    return (((lhs + 1,), (rhs + 1,)), ((0,), (0,)))


def _params(*sem):
    return pltpu.CompilerParams(dimension_semantics=sem if sem else None, vmem_limit_bytes=VMEM_LIMIT)


def _dot(a, b, mode):
    return lax.dot_general(a, b, _dims(mode, a.ndim), preferred_element_type=F32)


def _dg(a, b, dims):
    return _dot(a.astype(BF16), b.astype(BF16), dims)


@jax.custom_vjp
def mm(a, b):
    return _dg(a, b, _NN)


@jax.custom_vjp
def mm_nt(a, b):
    return _dg(a, b, _NT)


@jax.custom_vjp
def mm_tn(a, b):
    return _dg(a, b, _TN)


mm.defvjp(lambda a, b: (_dg(a, b, _NN), (a, b)), lambda r, g: (mm_nt(g, r[1]), mm_tn(r[0], g)))
mm_nt.defvjp(lambda a, b: (_dg(a, b, _NT), (a, b)), lambda r, g: (mm(g, r[1]), mm_tn(g, r[0])))
mm_tn.defvjp(lambda a, b: (_dg(a, b, _TN), (a, b)), lambda r, g: (mm_nt(r[1], g), mm(r[0], g)))


def _split3(x):
    hi = x.astype(BF16)
    r1 = x - hi.astype(F32)
    mid = r1.astype(BF16)
    lo = (r1 - mid.astype(F32)).astype(BF16)
    return hi, mid, lo


def _dg3(a, b, dims):
    ah, am, _ = _split3(a)
    bh, bm, _ = _split3(b)
    return _dot(ah, bh, dims) + (_dot(ah, bm, dims) + _dot(am, bh, dims))


def _dg_exact_rhs(a, b01, dims):
    ah, am, _ = _split3(a)
    b = b01.astype(BF16)
    return _dot(ah, b, dims) + _dot(am, b, dims)


def _dg_exact_lhs(a01, b, dims):
    bh, bm, bl = _split3(b)
    a = a01.astype(BF16)
    return _dot(a, bh, dims) + (_dot(a, bm, dims) + _dot(a, bl, dims))


@jax.custom_vjp
def mm3(a, b):
    return _dg3(a, b, _NN)


@jax.custom_vjp
def mm3_nt(a, b):
    return _dg3(a, b, _NT)


@jax.custom_vjp
def mm3_tn(a, b):
    return _dg3(a, b, _TN)


mm3.defvjp(lambda a, b: (_dg3(a, b, _NN), (a, b)), lambda r, g: (mm3_nt(g, r[1]), mm3_tn(r[0], g)))
mm3_nt.defvjp(lambda a, b: (_dg3(a, b, _NT), (a, b)), lambda r, g: (mm3(g, r[1]), mm3_tn(g, r[0])))
mm3_tn.defvjp(lambda a, b: (_dg3(a, b, _TN), (a, b)), lambda r, g: (mm3_nt(r[1], g), mm3(r[0], g)))


def _softplus(x):
    return jnp.maximum(x, 0.0) + jnp.log(1.0 + jnp.exp(-jnp.abs(x)))


def _log_sigmoid(x):
    return jnp.minimum(x, 0.0) - jnp.log(1.0 + jnp.exp(-jnp.abs(x)))


def _sigmoid(x):
    return 1.0 / (1.0 + jnp.exp(-x))


def _silu(x):
    return x * _sigmoid(x)


def _silu_grad(x):
    s = _sigmoid(x)
    return s * (1.0 + x * (1.0 - s))


@jax.custom_vjp
def mm01(a01, b):
    return _dg_exact_lhs(a01, b, _NN)


mm01.defvjp(lambda a, b: (_dg_exact_lhs(a, b, _NN), a),
            lambda a, g: (jnp.zeros_like(a), _dg_exact_lhs(a, g, _TN)))


def _lane_pick(x, idx):
    lane = lax.broadcasted_iota(jnp.int32, x.shape, x.ndim - 1)
    return jnp.sum(jnp.where(lane == idx, x, 0.0), axis=-1, keepdims=True)


def _dn_chunk(qt, kt, v, ab, alog, dtb, h):
    B, C = qt.shape[0], DN_CHUNK
    g = -jnp.exp(_lane_pick(alog, h)) * _softplus(_lane_pick(ab, h) + _lane_pick(dtb, h))
    beta = _sigmoid(_lane_pick(ab, h + DN_V_HEADS))
    q = qt * lax.rsqrt(jnp.sum(qt * qt, axis=-1, keepdims=True) + EPS) * (HEAD_DIM ** -0.5)
    k = kt * lax.rsqrt(jnp.sum(kt * kt, axis=-1, keepdims=True) + EPS)
    row = lax.broadcasted_iota(jnp.int32, (B, C, C), 1)
    col = lax.broadcasted_iota(jnp.int32, (B, C, C), 2)
    lower = (row >= col).astype(F32)
    ones = jnp.ones((B, C, C), F32)
    g_wide = jnp.broadcast_to(g, (B, C, LANE))
    g_sq = jnp.broadcast_to(g, (B, C, C))
    gc = mm01(lower, g_wide)
    gc_i = gc[:, :, :C]
    gc_j = mm01(ones, jnp.where(row <= col, g_sq, 0.0))
    g_last = jnp.broadcast_to(gc[:, C - 1:C, :], (B, C, LANE))
    decay = jnp.exp(jnp.where(row >= col, gc_i - gc_j, -1e30))
    eg = jnp.exp(gc)
    kk = mm_nt(k, k)
    a_mat = jnp.where(row > col, jnp.broadcast_to(beta, (B, C, C)) * kk * decay, 0.0)
    eye = (row == col).astype(F32)
    y = -a_mat
    t = eye + y
    for _ in range(5):
        y = mm(y, y)
        t = t + mm(t, y)
    bb = jnp.broadcast_to(beta, (B, C, LANE))
    u0 = mm3(t, v * bb)
    w = mm3(t, k * (bb * eg))
    qk = mm_nt(q, k) * decay
    q_dec = q * eg
    k_dec = k * jnp.exp(g_last - gc)
    cd = jnp.exp(g_last)[:, :8, :]
    return u0, w, qk, q_dec, k_dec, cd


def _shift_rows(x, j, down):
    if j == 0:
        return x
    n = x.shape[0]
    r = lax.broadcasted_iota(jnp.int32, x.shape, 0)
    if down:
        return jnp.where(r >= j, pltpu.roll(x, j, 0), 0.0)
    return jnp.where(r < n - j, pltpu.roll(x, n - j, 0), 0.0)


def dn_conv_fwd(proj, conv_w):
    s = proj.shape[0]

    def body(x_ref, w_ref, o_ref):
        x = x_ref[...]
        w = w_ref[...]
        pre = x * w[3:4, :]
        for j in (1, 2, 3):
            pre = pre + _shift_rows(x, j, True) * w[3 - j:4 - j, :]
        o_ref[...] = _silu(pre)

    return pl.pallas_call(
        body, name="dn_conv_fwd", grid=(DN_QKV // LANE,),
        in_specs=[pl.BlockSpec((s, LANE), lambda j: (0, j)), pl.BlockSpec((8, LANE), lambda j: (0, j))],
        out_specs=pl.BlockSpec((s, LANE), lambda j: (0, j)),
        out_shape=jax.ShapeDtypeStruct((s, DN_QKV), F32), compiler_params=_params("parallel"),
    )(proj, conv_w)


def dn_conv_bwd(dact, proj, conv_w, dproj_in, dab, blk0, name):
    s = proj.shape[0]
    nblk = dact.shape[1] // LANE
    extra = 2 if dab is not None else 0

    def body(*refs):
        if dab is not None:
            da_ref, x_ref, w_ref, _, dab_ref, dp_ref, dw_ref = refs
        else:
            da_ref, x_ref, w_ref, _, dp_ref, dw_ref = refs
        j = pl.program_id(0)

        @pl.when(j < nblk)
        def _():
            x = x_ref[...]
            w = w_ref[...]
            xs = [_shift_rows(x, 3 - kk_, True) for kk_ in range(4)]
            pre = xs[0] * w[0:1, :]
            for kk_ in (1, 2, 3):
                pre = pre + xs[kk_] * w[kk_:kk_ + 1, :]
            dpre = da_ref[...] * _silu_grad(pre)
            dx = dpre * w[3:4, :]
            for jj in (1, 2, 3):
                dx = dx + _shift_rows(dpre, jj, False) * w[3 - jj:4 - jj, :]
            dp_ref[...] = dx
            rows = [jnp.sum(dpre * xs[kk_], axis=0, keepdims=True) for kk_ in range(4)]
            dw_ref[...] = jnp.concatenate(rows + [jnp.zeros((4, LANE), F32)], axis=0)

        if dab is not None:
            @pl.when(j == nblk)
            def _():
                dp_ref[...] = dab_ref[...]

            @pl.when(j == nblk + 1)
            def _():
                dp_ref[...] = jnp.zeros_like(dp_ref)

    cl = lambda j: jnp.minimum(j, nblk - 1)
    in_specs = [pl.BlockSpec((s, LANE), lambda j: (0, cl(j))),
                pl.BlockSpec((s, LANE), lambda j: (0, blk0 + cl(j))),
                pl.BlockSpec((8, LANE), lambda j: (0, blk0 + cl(j))),
                pl.BlockSpec(memory_space=pl.ANY)]
    args = [dact, proj, conv_w, dproj_in]
    if dab is not None:
        in_specs.append(pl.BlockSpec((s, LANE), lambda j: (0, 0)))
        args.append(dab)
    return pl.pallas_call(
        body, name=name, grid=(nblk + extra,), in_specs=in_specs,
        out_specs=[pl.BlockSpec((s, LANE), lambda j: (0, blk0 + j)), pl.BlockSpec((8, LANE), lambda j: (0, cl(j)))],
        out_shape=[jax.ShapeDtypeStruct(dproj_in.shape, F32), jax.ShapeDtypeStruct((8, dact.shape[1]), F32)],
        input_output_aliases={3: 0}, compiler_params=_params("arbitrary"),
    )(*args)


DN_GROUP = 8
DN_ROWS = DN_GROUP * DN_CHUNK


def dn_prep_fwd(act, proj, alog, dtb, exchange=None):
    s = act.shape[0]
    nc = s // DN_CHUNK
    C = DN_CHUNK

    def body(q_ref, k_ref, v_ref, ab_ref, al_ref, dt_ref, u0_ref, w_ref, qd_ref, kd_ref, qk_ref, cd_ref):
        qh = pl.program_id(0)
        al = al_ref[0:1, :]
        dt = dt_ref[0:1, :]
        chunks = lambda x: x.reshape(DN_GROUP, C, x.shape[-1])
        rows = lambda x: x.reshape(DN_ROWS, x.shape[-1])
        qt, kt, ab = chunks(q_ref[...]), chunks(k_ref[...]), chunks(ab_ref[...])
        for hv in range(2):
            cs = slice(hv * LANE, (hv + 1) * LANE)
            u0, w, qk, qd, kd, cd = _dn_chunk(qt, kt, chunks(v_ref[:, cs]), ab, al, dt, 2 * qh + hv)
            u0_ref[:, cs] = rows(u0)
            w_ref[:, cs] = rows(w)
            qd_ref[:, cs] = rows(qd)
            kd_ref[:, cs] = rows(kd)
            qk_ref[hv] = rows(qk)
            cd_ref[hv] = cd

    big = pl.BlockSpec((DN_ROWS, 2 * LANE), lambda h, g: (g, h))
    wide = jax.ShapeDtypeStruct((s, MIX_WIDTH), F32)
    return hosted_call(
        body, exchange, name="dn_prep_fwd", grid=(DN_QK_HEADS, s // DN_ROWS),
        in_specs=[pl.BlockSpec((DN_ROWS, LANE), lambda h, g: (g, h)),
                  pl.BlockSpec((DN_ROWS, LANE), lambda h, g: (g, DN_QK_HEADS + h)),
                  pl.BlockSpec((DN_ROWS, 2 * LANE), lambda h, g: (g, DN_QK_HEADS + h)),
                  pl.BlockSpec((DN_ROWS, LANE), lambda h, g: (g, DN_AB_BLK)),
                  pl.BlockSpec((8, LANE), lambda h, g: (0, 0)), pl.BlockSpec((8, LANE), lambda h, g: (0, 0))],
        out_specs=[big, big, big, big,
                   pl.BlockSpec((2, DN_ROWS, C), lambda h, g: (h, g, 0)),
                   pl.BlockSpec((2, DN_GROUP, 8, LANE), lambda h, g: (h, g, 0, 0))],
        out_shape=[wide, wide, wide, wide, jax.ShapeDtypeStruct((DN_V_HEADS, s, C), F32),
                   jax.ShapeDtypeStruct((DN_V_HEADS, nc, 8, LANE), F32)],
        sem=("parallel", "parallel"),
    )(act, act, act, proj, alog, dtb)


def dn_prep_bwd(act, proj, alog, dtb, du0, dw, dqd, dkd, dqk, dcd, exchange=None):
    s = act.shape[0]
    C = DN_CHUNK

    def body(q_ref, k_ref, v_ref, ab_ref, al_ref, dt_ref, du0_ref, dw_ref, dqd_ref, dkd_ref, dqk_ref, dcd_ref,
             dq_ref, dk_ref, dv_ref, dab_ref, dal_ref, ddt_ref):
        g_id = pl.program_id(0)
        qh = pl.program_id(1)
        al = al_ref[0:1, :]
        dt = dt_ref[0:1, :]

        @pl.when(qh == 0)
        def _():
            dab_ref[...] = jnp.zeros_like(dab_ref)

        @pl.when((qh == 0) & (g_id == 0))
        def _():
            dal_ref[...] = jnp.zeros_like(dal_ref)
            ddt_ref[...] = jnp.zeros_like(ddt_ref)

        chunks = lambda x: x.reshape(DN_GROUP, C, x.shape[-1])
        rows = lambda x: x.reshape(DN_ROWS, x.shape[-1])
        qt, kt, ab = chunks(q_ref[...]), chunks(k_ref[...]), chunks(ab_ref[...])
        dq_acc = jnp.zeros((DN_ROWS, LANE), F32)
        dk_acc = jnp.zeros((DN_ROWS, LANE), F32)
        for hv in range(2):
            cs = slice(hv * LANE, (hv + 1) * LANE)
            h = 2 * qh + hv
            f = lambda qt_, kt_, v_, ab_, a_, d_: _dn_chunk(qt_, kt_, v_, ab_, a_, d_, h)
            _, vjp = jax.vjp(f, qt, kt, chunks(v_ref[:, cs]), ab, al, dt)
            dq, dk, dv, dab, dal, ddt = vjp((chunks(du0_ref[:, cs]), chunks(dw_ref[:, cs]), chunks(dqk_ref[hv]),
                                             chunks(dqd_ref[:, cs]), chunks(dkd_ref[:, cs]), dcd_ref[hv]))
            dq_acc = dq_acc + rows(dq)
            dk_acc = dk_acc + rows(dk)
            dv_ref[:, cs] = rows(dv)
            dab_ref[...] += rows(dab)
            dal_ref[0:1, :] += dal
            ddt_ref[0:1, :] += ddt
        dq_ref[...] = dq_acc
        dk_ref[...] = dk_acc

    big = pl.BlockSpec((DN_ROWS, 2 * LANE), lambda g, h: (g, h))
    one = pl.BlockSpec((DN_ROWS, LANE), lambda g, h: (g, h))
    small = pl.BlockSpec((8, LANE), lambda g, h: (0, 0))
    return hosted_call(
        body, exchange, name="dn_prep_bwd", grid=(s // DN_ROWS, DN_QK_HEADS),
        in_specs=[one, pl.BlockSpec((DN_ROWS, LANE), lambda g, h: (g, DN_QK_HEADS + h)),
                  pl.BlockSpec((DN_ROWS, 2 * LANE), lambda g, h: (g, DN_QK_HEADS + h)),
                  pl.BlockSpec((DN_ROWS, LANE), lambda g, h: (g, DN_AB_BLK)), small, small,
                  big, big, big, big,
                  pl.BlockSpec((2, DN_ROWS, C), lambda g, h: (h, g, 0)),
                  pl.BlockSpec((2, DN_GROUP, 8, LANE), lambda g, h: (h, g, 0, 0))],
        out_specs=[one, one, big, pl.BlockSpec((DN_ROWS, LANE), lambda g, h: (g, 0)), small, small],
        out_shape=[jax.ShapeDtypeStruct((s, DN_QK_WIDTH), F32), jax.ShapeDtypeStruct((s, DN_QK_WIDTH), F32),
                   jax.ShapeDtypeStruct((s, MIX_WIDTH), F32), jax.ShapeDtypeStruct((s, LANE), F32),
                   jax.ShapeDtypeStruct((8, LANE), F32), jax.ShapeDtypeStruct((8, LANE), F32)],
        sem=("arbitrary", "arbitrary"),
    )(act, act, act, proj, alog, dtb, du0, dw, dqd, dkd, dqk, dcd)


def dn_scan_fwd(u0, w, qd, kd, qk, cd, width, exchange=None):
    s = u0.shape[0]
    nc = s // DN_CHUNK
    C = DN_CHUNK

    def body(u0_ref, w_ref, qd_ref, kd_ref, qk_ref, cd_ref, o_ref, st_ref):
        def step(c, state):
            rs = pl.ds(pl.multiple_of(c * C, C), C)
            st_ref[0, c] = state
            u = u0_ref[rs, :] - _dg(w_ref[rs, :], state, _NN)
            o_ref[rs, :] = _dg(qd_ref[rs, :], state, _NN) + _dg(qk_ref[0, rs, :], u, _NN)
            return cd_ref[0, c][0:1, :] * state + _dg(kd_ref[rs, :], u, _TN)

        lax.fori_loop(0, nc, step, jnp.zeros((HEAD_DIM, HEAD_DIM), F32))

    col = pl.BlockSpec((s, LANE), lambda h: (0, h))
    return hosted_call(
        body, exchange, name="dn_scan_fwd", grid=(DN_V_HEADS,),
        in_specs=[col, col, col, col, pl.BlockSpec((1, s, C), lambda h: (h, 0, 0)),
                  pl.BlockSpec((1, nc, 8, LANE), lambda h: (h, 0, 0, 0))],
        out_specs=[col, pl.BlockSpec((1, nc, HEAD_DIM, HEAD_DIM), lambda h: (h, 0, 0, 0))],
        out_shape=[jax.ShapeDtypeStruct((s, width), F32),
                   jax.ShapeDtypeStruct((DN_V_HEADS, nc, HEAD_DIM, HEAD_DIM), F32)],
        sem=("parallel",),
    )(u0, w, qd, kd, qk, cd)


def dn_scan_bwd(do, u0, w, qd, kd, qk, cd, states, exchange=None):
    s = u0.shape[0]
    nc = s // DN_CHUNK
    C = DN_CHUNK

    def body(do_ref, u0_ref, w_ref, qd_ref, kd_ref, qk_ref, cd_ref, st_ref,
             du0_ref, dw_ref, dqd_ref, dkd_ref, dqk_ref, dcd_ref):
        def step(i, dstate):
            c = nc - 1 - i
            rs = pl.ds(pl.multiple_of(c * C, C), C)
            state = st_ref[0, c]
            g = do_ref[rs, :]
            w_c = w_ref[rs, :]
            kd_c = kd_ref[rs, :]
            qd_c = qd_ref[rs, :]
            qk_c = qk_ref[0, rs, :]
            cd_row = cd_ref[0, c][0:1, :]
            u = u0_ref[rs, :] - _dg(w_c, state, _NN)
            du = _dg(qk_c, g, _TN) + _dg(kd_c, dstate, _NN)
            du0_ref[rs, :] = du
            dw_ref[rs, :] = -_dg(du, state, _NT)
            dqd_ref[rs, :] = _dg(g, state, _NT)
            dkd_ref[rs, :] = _dg(u, dstate, _NT)
            dqk_ref[0, rs, :] = _dg(g, u, _NT)
            dcd_row = jnp.sum(state * dstate, axis=0, keepdims=True)
            dcd_ref[0, c] = jnp.concatenate([dcd_row, jnp.zeros((7, LANE), F32)], axis=0)
            return cd_row * dstate + _dg(qd_c, g, _TN) - _dg(w_c, du, _TN)

        lax.fori_loop(0, nc, step, jnp.zeros((HEAD_DIM, HEAD_DIM), F32))

    col = pl.BlockSpec((s, LANE), lambda h: (0, h))
    qk_spec = pl.BlockSpec((1, s, C), lambda h: (h, 0, 0))
    cd_spec = pl.BlockSpec((1, nc, 8, LANE), lambda h: (h, 0, 0, 0))
    wide = jax.ShapeDtypeStruct((s, MIX_WIDTH), F32)
    return hosted_call(
        body, exchange, name="dn_scan_bwd", grid=(DN_V_HEADS,),
        in_specs=[col, col, col, col, col, qk_spec, cd_spec,
                  pl.BlockSpec((1, nc, HEAD_DIM, HEAD_DIM), lambda h: (h, 0, 0, 0))],
        out_specs=[col, col, col, col, qk_spec, cd_spec],
        out_shape=[wide, wide, wide, wide, jax.ShapeDtypeStruct((DN_V_HEADS, s, C), F32),
                   jax.ShapeDtypeStruct((DN_V_HEADS, nc, 8, LANE), F32)],
        sem=("parallel",),
    )(do, u0, w, qd, kd, qk, cd, states)


SB_T = 256
SB_PAIR = 2


def _sb_scores(q, kbs, diff, lims):
    zs = [_dg(q, kb, _NT) * (HEAD_DIM ** -0.5) for kb in kbs]
    masks = [diff < lim for lim in lims]
    lss = [_log_sigmoid(z) for z in zs]
    lrs = [jnp.where(m, ls - z, 0.0) for m, ls, z in zip(masks, lss, zs)]
    return masks, lss, lrs


def _sb_diff():
    return lax.broadcasted_iota(jnp.int32, (SB_T, SB_T), 1) - lax.broadcasted_iota(jnp.int32, (SB_T, SB_T), 0)


def _sb_loop(n_tiles, step, carry):
    pairs = n_tiles // SB_PAIR
    carry = lax.fori_loop(0, pairs, lambda p, c: step(p * SB_PAIR, SB_PAIR, c), carry)
    return lax.fori_loop(pairs * SB_PAIR, n_tiles, lambda t, c: step(t, 1, c), carry)


def sb_fwd(qkn, proj, v_blk0, width, exchange=None):
    s = qkn.shape[0]

    def body(q_ref, k_ref, v_ref, o_ref, lt_ref):
        i = pl.program_id(1)
        q = q_ref[...]
        diff = _sb_diff()
        after = (diff < 0).astype(BF16)

        def step(first, n, carry):
            run, acc = carry
            tiles = [first + t for t in range(n)]
            kss = [pl.ds(pl.multiple_of((i - t) * SB_T, SB_T), SB_T) for t in tiles]
            masks, lss, lrs = _sb_scores(q, [k_ref[ks, :] for ks in kss], diff, [t * SB_T for t in tiles])
            within = [_dg_exact_rhs(lr, after, _NN) for lr in lrs]
            sums = [jnp.sum(lr, axis=1, keepdims=True) for lr in lrs]
            for t in range(n):
                wts = jnp.where(masks[t], jnp.exp(lss[t] + (within[t] + run)), 0.0)
                acc = acc + _dg(wts, v_ref[kss[t], :], _NN)
                run = run + sums[t]
            return run, acc

        run, acc = _sb_loop(i + 1, step, (jnp.zeros((SB_T, 1), F32), jnp.zeros((SB_T, HEAD_DIM), F32)))
        o_ref[...] = acc
        lt_ref[0] = run

    return hosted_call(
        body, exchange, name="sb_fwd", grid=(SB_HEADS, s // SB_T),
        in_specs=[pl.BlockSpec((SB_T, LANE), lambda h, i: (i, h)),
                  pl.BlockSpec((s, LANE), lambda h, i: (0, SB_HEADS + h)),
                  pl.BlockSpec((s, LANE), lambda h, i: (0, v_blk0 + h))],
        out_specs=[pl.BlockSpec((SB_T, LANE), lambda h, i: (i, h)), pl.BlockSpec((1, SB_T, 1), lambda h, i: (h, i, 0))],
        out_shape=[jax.ShapeDtypeStruct((s, width), F32), jax.ShapeDtypeStruct((SB_HEADS, s, 1), F32)],
        sem=("parallel", "parallel"),
    )(qkn, qkn, proj)


def sb_bwd(qkn, proj, v_blk0, do, ltot, dproj_in):
    s = qkn.shape[0]

    def body(q_ref, k_ref, v_ref, do_ref, lt_ref, _, dq_ref, dk_ref, dv_ref):
        i = pl.program_id(1)

        @pl.when(i == 0)
        def _():
            dk_ref[...] = jnp.zeros_like(dk_ref)
            dv_ref[...] = jnp.zeros_like(dv_ref)

        q = q_ref[...]
        g = do_ref[...]
        ltot = lt_ref[0]
        diff = _sb_diff()
        upto = (diff >= 0).astype(BF16)
        before = (diff > 0).astype(BF16)

        def step(first, n, carry):
            plr, pdl, dq = carry
            tiles = [first + t for t in range(n)]
            kss = [pl.ds(pl.multiple_of(j * SB_T, SB_T), SB_T) for j in tiles]
            kbs = [k_ref[ks, :] for ks in kss]
            vbs = [v_ref[ks, :] for ks in kss]
            masks, lss, lrs = _sb_scores(q, kbs, diff, [(i - j) * SB_T for j in tiles])
            dwts = [_dg(g, vb, _NT) for vb in vbs]
            within = [_dg_exact_rhs(lr, upto, _NN) for lr in lrs]
            wtss, dls = [], []
            for t in range(n):
                wts = jnp.where(masks[t], jnp.exp(lss[t] + (ltot - (within[t] + plr))), 0.0)
                plr = plr + jnp.sum(lrs[t], axis=1, keepdims=True)
                wtss.append(wts)
                dls.append(dwts[t] * wts)
            dwithin = [_dg_exact_rhs(dl, before, _NN) for dl in dls]
            for t in range(n):
                sz = jnp.exp(lss[t])
                dz = jnp.where(masks[t], dls[t] * (1.0 - sz) - sz * (dwithin[t] + pdl), 0.0) * (HEAD_DIM ** -0.5)
                pdl = pdl + jnp.sum(dls[t], axis=1, keepdims=True)
                dk_ref[kss[t], :] += _dg(dz, q, _TN)
                dv_ref[kss[t], :] += _dg(wtss[t], g, _TN)
                dq = dq + _dg(dz, kbs[t], _NN)
            return plr, pdl, dq

        zero = jnp.zeros((SB_T, 1), F32)
        _, _, dq = _sb_loop(i + 1, step, (zero, zero, jnp.zeros((SB_T, HEAD_DIM), F32)))
        dq_ref[...] = dq

    tile = pl.BlockSpec((SB_T, LANE), lambda h, i: (i, h))
    colspec = pl.BlockSpec((s, LANE), lambda h, i: (0, h))
    return pl.pallas_call(
        body, name="sb_bwd", grid=(SB_HEADS, s // SB_T),
        in_specs=[tile, pl.BlockSpec((s, LANE), lambda h, i: (0, SB_HEADS + h)),
                  pl.BlockSpec((s, LANE), lambda h, i: (0, v_blk0 + h)), tile,
                  pl.BlockSpec((1, SB_T, 1), lambda h, i: (h, i, 0)), pl.BlockSpec(memory_space=pl.ANY)],
        out_specs=[tile, colspec, pl.BlockSpec((s, LANE), lambda h, i: (0, v_blk0 + h))],
        out_shape=[jax.ShapeDtypeStruct((s, MIX_WIDTH), F32), jax.ShapeDtypeStruct((s, MIX_WIDTH), F32),
                   jax.ShapeDtypeStruct(dproj_in.shape, F32)],
        input_output_aliases={5: 2}, compiler_params=_params("parallel", "arbitrary"),
    )(qkn, qkn, proj, do, ltot, dproj_in)


ROW_TILE = 256
HN_HEADS = 8


def head_norm_fwd(x, x_blk0, nblk, gains, out_dtype, out_width, name):
    s = x.shape[0]
    assert x_blk0 % HN_HEADS == 0 and nblk % (HN_HEADS * gains.shape[0]) == 0
    per = nblk // gains.shape[0] // HN_HEADS
    w = HN_HEADS * LANE

    def body(x_ref, g_ref, o_ref):
        gain = g_ref[0, 0:1, :]
        for j in range(HN_HEADS):
            cs = slice(j * LANE, (j + 1) * LANE)
            xv = x_ref[:, cs]
            r = lax.rsqrt(jnp.mean(xv * xv, axis=1, keepdims=True) + EPS)
            o_ref[:, cs] = (xv * r * gain).astype(out_dtype)

    return pl.pallas_call(
        body, name=name, grid=(nblk // HN_HEADS, s // ROW_TILE),
        in_specs=[pl.BlockSpec((ROW_TILE, w), lambda j, t: (t, x_blk0 // HN_HEADS + j)),
                  pl.BlockSpec((1, 8, LANE), lambda j, t: (j // per, 0, 0))],
        out_specs=pl.BlockSpec((ROW_TILE, w), lambda j, t: (t, j)),
        out_shape=jax.ShapeDtypeStruct((s, out_width), out_dtype), compiler_params=_params("parallel", "parallel"),
    )(x, gains)


def head_norm_bwd(dy, dy_blk0, x, x_blk0, nblk, gain, dst, dst_blk0, name):
    s = x.shape[0]

    def body(*refs):
        if dst is not None:
            dy_ref, x_ref, g_ref, _, dx_ref, dg_ref = refs
        else:
            dy_ref, x_ref, g_ref, dx_ref, dg_ref = refs

        @pl.when((pl.program_id(0) == 0) & (pl.program_id(1) == 0))
        def _():
            dg_ref[...] = jnp.zeros_like(dg_ref)

        gain = g_ref[0:1, :]
        dg = jnp.zeros((1, LANE), F32)
        for j in range(HN_HEADS):
            cs = slice(j * LANE, (j + 1) * LANE)
            xv = x_ref[:, cs]
            g = dy_ref[:, cs]
            r = lax.rsqrt(jnp.mean(xv * xv, axis=1, keepdims=True) + EPS)
            gy = g * gain
            dx_ref[:, cs] = r * gy - xv * (r * r * r) * jnp.mean(gy * xv, axis=1, keepdims=True)
            dg = dg + jnp.sum(g * xv * r, axis=0, keepdims=True)
        dg_ref[0:1, :] += dg

    assert dy_blk0 % HN_HEADS == 0 and x_blk0 % HN_HEADS == 0 and dst_blk0 % HN_HEADS == 0 and nblk % HN_HEADS == 0
    w = HN_HEADS * LANE
    in_specs = [pl.BlockSpec((ROW_TILE, w), lambda j, t: (t, dy_blk0 // HN_HEADS + j)),
                pl.BlockSpec((ROW_TILE, w), lambda j, t: (t, x_blk0 // HN_HEADS + j)),
                pl.BlockSpec((8, LANE), lambda j, t: (0, 0))]
    args = [dy, x, gain]
    aliases = {}
    if dst is not None:
        in_specs.append(pl.BlockSpec(memory_space=pl.ANY))
        args.append(dst)
        aliases = {3: 0}
        out0 = jax.ShapeDtypeStruct(dst.shape, F32)
    else:
        out0 = jax.ShapeDtypeStruct((s, (dst_blk0 + nblk) * LANE), F32)
    return pl.pallas_call(
        body, name=name, grid=(nblk // HN_HEADS, s // ROW_TILE), in_specs=in_specs,
        out_specs=[pl.BlockSpec((ROW_TILE, w), lambda j, t: (t, dst_blk0 // HN_HEADS + j)),
                   pl.BlockSpec((8, LANE), lambda j, t: (0, 0))],
        out_shape=[out0, jax.ShapeDtypeStruct((8, LANE), F32)],
        input_output_aliases=aliases, compiler_params=_params("arbitrary", "arbitrary"),
    )(*args)


def _xa_head(xq, kraw, v, qg, kg):
    q = xq * lax.rsqrt(jnp.mean(xq * xq, axis=1, keepdims=True) + EPS) * qg
    k = kraw * lax.rsqrt(jnp.mean(kraw * kraw, axis=1, keepdims=True) + EPS) * kg
    sc = mm_nt(q, k) * (XA_DIM ** -0.5)
    e = jnp.exp(sc - lax.stop_gradient(jnp.max(sc, axis=1, keepdims=True)))
    return mm(e / jnp.sum(e, axis=1, keepdims=True), v)


def xa_fwd(proj, xq_blk0, kv, qg, kg, cat):
    s = proj.shape[0]
    n_mem = kv.shape[0]

    def body(xq_ref, k_ref, v_ref, qg_ref, kg_ref, _, o_ref):
        o_ref[...] = _xa_head(xq_ref[...], k_ref[...], v_ref[...], qg_ref[0:1, :], kg_ref[0:1, :])

    gain = pl.BlockSpec((8, XA_DIM), lambda h, t: (0, 0))
    return pl.pallas_call(
        body, name="xa_fwd", grid=(XA_HEADS, s // ROW_TILE),
        in_specs=[pl.BlockSpec((ROW_TILE, XA_DIM), lambda h, t: (t, xq_blk0 // 2 + h)),
                  pl.BlockSpec((n_mem, XA_DIM), lambda h, t: (0, h)),
                  pl.BlockSpec((n_mem, XA_DIM), lambda h, t: (0, XA_HEADS + h)), gain, gain,
                  pl.BlockSpec(memory_space=pl.ANY)],
        out_specs=pl.BlockSpec((ROW_TILE, XA_DIM), lambda h, t: (t, MIX_WIDTH // XA_DIM + h)),
        out_shape=jax.ShapeDtypeStruct(cat.shape, F32), input_output_aliases={5: 0},
        compiler_params=_params("parallel", "parallel"),
    )(proj, kv, kv, qg, kg, cat)


def xa_bwd(proj, xq_blk0, kv, qg, kg, dcat, dproj_in):
    s = proj.shape[0]
    n_mem = kv.shape[0]

    def body(xq_ref, k_ref, v_ref, qg_ref, kg_ref, do_ref, _, dxq_ref, dk_ref, dv_ref, dqg_ref, dkg_ref):
        h = pl.program_id(0)
        t = pl.program_id(1)

        @pl.when(t == 0)
        def _():
            dk_ref[...] = jnp.zeros_like(dk_ref)
            dv_ref[...] = jnp.zeros_like(dv_ref)

        @pl.when((t == 0) & (h == 0))
        def _():
            dqg_ref[...] = jnp.zeros_like(dqg_ref)
            dkg_ref[...] = jnp.zeros_like(dkg_ref)

        _, vjp = jax.vjp(_xa_head, xq_ref[...], k_ref[...], v_ref[...], qg_ref[0:1, :], kg_ref[0:1, :])
        dxq, dk, dv, dqg, dkg = vjp(do_ref[...])
        dxq_ref[...] = dxq
        dk_ref[...] += dk
        dv_ref[...] += dv
        dqg_ref[0:1, :] += dqg
        dkg_ref[0:1, :] += dkg

    gain = pl.BlockSpec((8, XA_DIM), lambda h, t: (0, 0))
    kspec = pl.BlockSpec((n_mem, XA_DIM), lambda h, t: (0, h))
    vspec = pl.BlockSpec((n_mem, XA_DIM), lambda h, t: (0, XA_HEADS + h))
    return pl.pallas_call(
        body, name="xa_bwd", grid=(XA_HEADS, s // ROW_TILE),
        in_specs=[pl.BlockSpec((ROW_TILE, XA_DIM), lambda h, t: (t, xq_blk0 // 2 + h)), kspec, vspec, gain, gain,
                  pl.BlockSpec((ROW_TILE, XA_DIM), lambda h, t: (t, MIX_WIDTH // XA_DIM + h)),
                  pl.BlockSpec(memory_space=pl.ANY)],
        out_specs=[pl.BlockSpec((ROW_TILE, XA_DIM), lambda h, t: (t, xq_blk0 // 2 + h)), kspec, kspec, gain, gain],
        out_shape=[jax.ShapeDtypeStruct(dproj_in.shape, F32), jax.ShapeDtypeStruct((n_mem, XA_WIDTH), F32),
                   jax.ShapeDtypeStruct((n_mem, XA_WIDTH), F32), jax.ShapeDtypeStruct((8, XA_DIM), F32),
                   jax.ShapeDtypeStruct((8, XA_DIM), F32)],
        input_output_aliases={6: 0}, compiler_params=_params("arbitrary", "arbitrary"),
    )(proj, kv, kv, qg, kg, dcat, dproj_in)


GATE_ROWS = 1024


def gate_fwd(cat, proj, z_blk0):
    s = cat.shape[0]

    def body(c_ref, z_ref, y_ref):
        y_ref[...] = (c_ref[...] * _silu(z_ref[...])).astype(BF16)

    w = 2 * LANE
    rt = min(GATE_ROWS, s)
    return pl.pallas_call(
        body, name="gate_fwd", grid=(INNER // w, s // rt),
        in_specs=[pl.BlockSpec((rt, w), lambda j, t: (t, j)),
                  pl.BlockSpec((rt, w), lambda j, t: (t, z_blk0 // 2 + j))],
        out_specs=pl.BlockSpec((rt, w), lambda j, t: (t, j)),
        out_shape=jax.ShapeDtypeStruct((s, INNER), BF16), compiler_params=_params("parallel", "parallel"),
    )(cat, proj)


def gate_bwd(dy, cat, proj, z_blk0):
    s = cat.shape[0]

    def body(dy_ref, c_ref, z_ref, dc_ref, dz_ref):
        z = z_ref[...]
        g = dy_ref[...]
        dc_ref[...] = g * _silu(z)
        dz_ref[...] = g * c_ref[...] * _silu_grad(z)

    w = 2 * LANE
    rt = min(GATE_ROWS, s)
    tile = pl.BlockSpec((rt, w), lambda j, t: (t, j))
    ztile = pl.BlockSpec((rt, w), lambda j, t: (t, z_blk0 // 2 + j))
    return pl.pallas_call(
        body, name="gate_bwd", grid=(INNER // w, s // rt), in_specs=[tile, tile, ztile],
        out_specs=[tile, ztile],
        out_shape=[jax.ShapeDtypeStruct((s, INNER), F32), jax.ShapeDtypeStruct(proj.shape, F32)],
        compiler_params=_params("parallel", "parallel"),
    )(dy, cat, proj)


NORM_ROWS = 256


def rms_fwd(x, gain):
    s, d = x.shape

    def body(x_ref, g_ref, o_ref):
        xv = x_ref[...]
        r = lax.rsqrt(jnp.mean(xv * xv, axis=1, keepdims=True) + EPS)
        o_ref[...] = (xv * r * g_ref[0:1, :]).astype(BF16)

    return pl.pallas_call(
        body, name="rms_fwd", grid=(s // NORM_ROWS,),
        in_specs=[pl.BlockSpec((NORM_ROWS, d), lambda t: (t, 0)), pl.BlockSpec((8, d), lambda t: (0, 0))],
        out_specs=pl.BlockSpec((NORM_ROWS, d), lambda t: (t, 0)),
        out_shape=jax.ShapeDtypeStruct((s, d), BF16), compiler_params=_params("parallel"),
    )(x, gain)


def rms_bwd(dh, x, gain, dres):
    s, d = x.shape

    def body(*refs):
        if dres is not None:
            dh_ref, x_ref, g_ref, dr_ref, dx_ref, dg_ref = refs
        else:
            dh_ref, x_ref, g_ref, dx_ref, dg_ref = refs

        @pl.when(pl.program_id(0) == 0)
        def _():
            dg_ref[...] = jnp.zeros_like(dg_ref)

        xv = x_ref[...]
        g = dh_ref[...]
        r = lax.rsqrt(jnp.mean(xv * xv, axis=1, keepdims=True) + EPS)
        gy = g * g_ref[0:1, :]
        dx = r * gy - xv * (r * r * r) * jnp.mean(gy * xv, axis=1, keepdims=True)
        dx_ref[...] = dx + dr_ref[...] if dres is not None else dx
        dg_ref[0:1, :] += jnp.sum(g * xv * r, axis=0, keepdims=True)

    tile = pl.BlockSpec((NORM_ROWS, d), lambda t: (t, 0))
    gspec = pl.BlockSpec((8, d), lambda t: (0, 0))
    args = [dh, x, gain] + ([dres] if dres is not None else [])
    return pl.pallas_call(
        body, name="rms_bwd", grid=(s // NORM_ROWS,),
        in_specs=[tile, tile, gspec] + ([tile] if dres is not None else []),
        out_specs=[tile, gspec],
        out_shape=[jax.ShapeDtypeStruct((s, d), F32), jax.ShapeDtypeStruct((8, d), F32)],
        compiler_params=_params("arbitrary"),
    )(*args)


def loss_fwd_bwd(y, target):
    s, d = y.shape

    def body(y_ref, t_ref, l_ref, dy_ref):
        @pl.when(pl.program_id(0) == 0)
        def _():
            l_ref[...] = jnp.zeros_like(l_ref)

        err = y_ref[...] - t_ref[...]
        dy_ref[...] = err * (1.0 / d)
        part = 0.5 * jnp.sum(jnp.mean(err * err, axis=1, keepdims=True), axis=0, keepdims=True)
        r = lax.broadcasted_iota(jnp.int32, (8, LANE), 0)
        c = lax.broadcasted_iota(jnp.int32, (8, LANE), 1)
        l_ref[...] += jnp.where((r == 0) & (c == 0), part, 0.0)

    tile = pl.BlockSpec((NORM_ROWS, d), lambda t: (t, 0))
    return pl.pallas_call(
        body, name="loss", grid=(s // NORM_ROWS,), in_specs=[tile, tile],
        out_specs=[pl.BlockSpec((8, LANE), lambda t: (0, 0)), tile],
        out_shape=[jax.ShapeDtypeStruct((8, LANE), F32), jax.ShapeDtypeStruct((s, d), F32)],
        compiler_params=_params("arbitrary"),
    )(y, target)


def matmul(a, b, mode, out_dtype, tm, tn, tk, name, add=None, b_blocked=False, out_blocks=None, exchange=None):
    if b_blocked:
        nb, _, width = b.shape
        bshape = (b.shape[1], nb * width)
    else:
        bshape = b.shape
    if mode == "tn":
        (kdim, m), n = a.shape, bshape[1]
    else:
        (m, kdim), n = a.shape, (bshape[1] if mode == "nn" else bshape[0])
    tm, tn, tk = min(tm, m), min(tn, n), min(tk, kdim)
    assert m % tm == 0 and n % tn == 0 and kdim % tk == 0, (name, m, n, kdim)
    nk = kdim // tk
    dims = {"nn": _NN, "nt": _NT, "tn": _TN}[mode]

    def body(*refs):
        if add is not None:
            a_ref, b_ref, add_ref, o_ref, acc_ref = refs
        else:
            a_ref, b_ref, o_ref, acc_ref = refs
        k = pl.program_id(2)

        @pl.when(k == 0)
        def _():
            acc_ref[...] = jnp.zeros_like(acc_ref)

        acc_ref[...] += _dg(a_ref[...], b_ref[...], dims)

        @pl.when(k == nk - 1)
        def _():
            r = acc_ref[...]
            if add is not None:
                r = r + add_ref[...]
            o_ref[...] = r.astype(out_dtype)

    a_spec = pl.BlockSpec((tk, tm), lambda i, j, k: (k, i)) if mode == "tn" else pl.BlockSpec((tm, tk), lambda i, j, k: (i, k))
    if b_blocked and mode == "nn":
        per = width // tn
        assert width % tn == 0
        b_spec = pl.BlockSpec((None, tk, tn), lambda i, j, k: (j // per, k, j % per))
    elif b_blocked and mode == "nt":
        per = width // tk
        assert width % tk == 0
        b_spec = pl.BlockSpec((None, tn, tk), lambda i, j, k: (k // per, j, k % per))
    elif mode == "nt":
        b_spec = pl.BlockSpec((tn, tk), lambda i, j, k: (j, k))
    else:
        assert not b_blocked
        b_spec = pl.BlockSpec((tk, tn), lambda i, j, k: (k, j))
    add_spec = pl.BlockSpec((tm, tn), lambda i, j, k: (i, j))
    if out_blocks is not None:
        operb = (n // out_blocks) // tn
        assert (n // out_blocks) % tn == 0 and add is None
        o_spec = pl.BlockSpec((None, tm, tn), lambda i, j, k: (j // operb, i, j % operb))
        out_shape = jax.ShapeDtypeStruct((out_blocks, m, n // out_blocks), out_dtype)
    else:
        o_spec = add_spec
        out_shape = jax.ShapeDtypeStruct((m, n), out_dtype)
    res = hosted_call(
        body, exchange, name=name, grid=(m // tm, n // tn, nk),
        in_specs=[a_spec, b_spec] + ([add_spec] if add is not None else []), out_specs=[o_spec],
        out_shape=[out_shape], scratch_shapes=[pltpu.VMEM((tm, tn), F32)],
        sem=("parallel", "parallel", "arbitrary"),
    )(*([a, b] + ([add] if add is not None else [])))
    return res[0] if exchange is None else res


_HBM = pl.BlockSpec(memory_space=pltpu.HBM)


def _me():
    return lax.axis_index("x"), lax.axis_index("y"), lax.axis_index("c")


def _flat(p):
    return 4 * p[0] + 2 * p[1] + p[2]


def _flip(p, r):
    return tuple((1 - v) if (r >> (2 - a)) & 1 else v for a, v in enumerate(p))


class Exchange:
    def __init__(self, srcs, out_shapes, sems, start, finish):
        self.srcs, self.out_shapes, self.sems = list(srcs), list(out_shapes), list(sems)
        self.start, self.finish = start, finish


def hosted_call(body, exchange, *, name, grid, in_specs, out_specs, out_shape, scratch_shapes=(),
                input_output_aliases=None, sem=()):
    in_specs, out_specs, out_shape = list(in_specs), list(out_specs), list(out_shape)
    scratch_shapes = list(scratch_shapes)
    aliases = input_output_aliases or {}
    if exchange is None:
        call = pl.pallas_call(body, name=name, grid=grid, in_specs=in_specs, out_specs=out_specs, out_shape=out_shape,
                              scratch_shapes=scratch_shapes, input_output_aliases=aliases, compiler_params=_params(*sem))
        return lambda *args: list(call(*args))
    ni, no, ns = len(in_specs), len(out_specs), len(scratch_shapes)
    xi, xo = len(exchange.srcs), len(exchange.out_shapes)

    def wrapped(*refs):
        ins, refs = refs[:ni], refs[ni:]
        xin, refs = refs[:xi], refs[xi:]
        outs, refs = refs[:no], refs[no:]
        xout, refs = refs[:xo], refs[xo:]
        scr, xsem = refs[:ns], refs[ns:]
        first = functools.reduce(lambda p, q: p & q, [pl.program_id(d) == 0 for d in range(len(grid))])
        last = functools.reduce(lambda p, q: p & q, [pl.program_id(d) == grid[d] - 1 for d in range(len(grid))])

        @pl.when(first)
        def _():
            exchange.start(xin, xout, xsem)

        body(*ins, *outs, *scr)

        @pl.when(last)
        def _():
            exchange.finish(xin, xout, xsem)

    call = pl.pallas_call(
        wrapped, name=name, grid=grid, in_specs=in_specs + [_HBM] * xi, out_specs=out_specs + [_HBM] * xo,
        out_shape=out_shape + exchange.out_shapes, scratch_shapes=scratch_shapes + exchange.sems,
        input_output_aliases=aliases, compiler_params=_params(*(("arbitrary",) * len(grid))))
    return lambda *args: list(call(*args, *exchange.srcs))


def run_exchange(exchange, name):
    xi, xo = len(exchange.srcs), len(exchange.out_shapes)

    def body(*refs):
        exchange.start(refs[:xi], refs[xi:xi + xo], refs[xi + xo:])
        exchange.finish(refs[:xi], refs[xi:xi + xo], refs[xi + xo:])

    return list(pl.pallas_call(body, name=name, in_specs=[_HBM] * xi, out_specs=[_HBM] * xo,
                               out_shape=exchange.out_shapes, scratch_shapes=exchange.sems)(*exchange.srcs))


def gather_exchange(shards):
    n = len(shards)

    def parts(srcs, outs, sems):
        send_sems, recv_sems, local_sems = sems
        me = _me()
        x, y, c = me
        chips = [(1 - x, y), (x, 1 - y), (1 - x, 1 - y)]

        def copy(a, k, block, to, src=None):
            dst = outs[a].at[_flat(block)]
            return pltpu.make_async_remote_copy(src_ref=dst if src is None else src, dst_ref=dst,
                                                send_sem=send_sems.at[a, k], recv_sem=recv_sems.at[a, k],
                                                device_id=to, device_id_type=MESH)

        mine = [pltpu.make_async_copy(srcs[a], outs[a].at[_flat(me)], local_sems.at[a]) for a in range(n)]
        own = []
        for a in range(n):
            own.append(copy(a, 0, me, (x, y, 1 - c), src=srcs[a]))
            own += [copy(a, 1 + j, me, (*chip, c), src=srcs[a]) for j, chip in enumerate(chips)]
        return me, chips, copy, mine, own

    def start(srcs, outs, sems):
        _, _, _, mine, own = parts(srcs, outs, sems)
        for cp in mine + own:
            cp.start()

    def finish(srcs, outs, sems):
        me, chips, copy, mine, own = parts(srcs, outs, sems)
        x, y, c = me
        passed = []
        for j, chip in enumerate(chips):
            for a in range(n):
                copy(a, 1 + j, (*chip, c), me).wait_recv()
                fwd = copy(a, 4 + j, (*chip, c), (x, y, 1 - c))
                fwd.start()
                passed.append(fwd)
        for a in range(n):
            copy(a, 0, (x, y, 1 - c), me).wait_recv()
            for j, chip in enumerate(chips):
                copy(a, 4 + j, (*chip, 1 - c), me).wait_recv()
        for cp in own + passed:
            cp.wait_send()
        for cp in mine:
            cp.wait()

    dma = pltpu.SemaphoreType.DMA
    return Exchange(shards, [jax.ShapeDtypeStruct((N_DEV,) + s.shape, s.dtype) for s in shards],
                    [dma((n, 7)), dma((n, 7)), dma((n,))], start, finish)


def pair_exchange(srcs):
    n = len(srcs)

    def copies(srcs_, outs, sems):
        send_sems, recv_sems = sems
        x, y, c = _me()
        return [pltpu.make_async_remote_copy(src_ref=srcs_[a].at[:, 1 - c], dst_ref=outs[a], send_sem=send_sems.at[a],
                                             recv_sem=recv_sems.at[a], device_id=(x, y, 1 - c), device_id_type=MESH)
                for a in range(n)]

    def start(srcs_, outs, sems):
        for cp in copies(srcs_, outs, sems):
            cp.start()

    def finish(srcs_, outs, sems):
        for cp in copies(srcs_, outs, sems):
            cp.wait()

    dma = pltpu.SemaphoreType.DMA
    return Exchange(srcs, [jax.ShapeDtypeStruct((4,) + s.shape[2:], s.dtype) for s in srcs], [dma((n,)), dma((n,))],
                    start, finish)


def pair_sum(src, half, name):
    _, _, rows, cols = src.shape
    tr = min(rows, 256)

    def body(x_ref, h_ref, o_ref):
        c = lax.axis_index("c")
        o_ref[0] = (x_ref[0, c].astype(F32) + h_ref[0].astype(F32)).astype(BF16)

    return pl.pallas_call(
        body, name=name, grid=(4, rows // tr),
        in_specs=[pl.BlockSpec((1, 2, tr, cols), lambda ch, t: (ch, 0, t, 0)),
                  pl.BlockSpec((1, tr, cols), lambda ch, t: (ch, t, 0))],
        out_specs=pl.BlockSpec((1, tr, cols), lambda ch, t: (ch, t, 0)),
        out_shape=jax.ShapeDtypeStruct(half.shape, BF16), compiler_params=_params("parallel", "parallel"),
    )(src, half)


def chip_exchange(parts, slots, recv_shapes):
    n = len(parts)

    def plan(srcs, outs, sems):
        send_sems, recv_sems, local_sems = sems
        x, y, c = _me()
        chip = 2 * x + y
        mine = [pltpu.make_async_copy(srcs[a].at[chip], outs[slots[a][0]].at[chip, slots[a][1]], local_sems.at[a])
                for a in range(n)]
        sends, arrivals = [], []
        for r in (1, 2, 3):
            px = (1 - x) if r & 2 else x
            py = (1 - y) if r & 1 else y
            for a in range(n):
                ri, layer = slots[a]
                sends.append(pltpu.make_async_remote_copy(
                    src_ref=srcs[a].at[2 * px + py], dst_ref=outs[ri].at[chip, layer], send_sem=send_sems.at[a, r - 1],
                    recv_sem=recv_sems.at[a, r - 1], device_id=(px, py, c), device_id_type=MESH))
                land = outs[ri].at[2 * px + py, layer]
                arrivals.append(pltpu.make_async_remote_copy(
                    src_ref=land, dst_ref=land, send_sem=send_sems.at[a, r - 1], recv_sem=recv_sems.at[a, r - 1],
                    device_id=(px, py, c), device_id_type=MESH))
        return mine, sends, arrivals

    def start(srcs, outs, sems):
        mine, sends, _ = plan(srcs, outs, sems)
        for cp in mine + sends:
            cp.start()

    def finish(srcs, outs, sems):
        mine, sends, arrivals = plan(srcs, outs, sems)
        for cp in arrivals:
            cp.wait_recv()
        for cp in sends:
            cp.wait_send()
        for cp in mine:
            cp.wait()

    dma = pltpu.SemaphoreType.DMA
    return Exchange(parts, [jax.ShapeDtypeStruct(s, BF16) for s in recv_shapes],
                    [dma((n, 3)), dma((n, 3)), dma((n,))], start, finish)


SMALL_ROWS = 24


def all_reduce_small(pack):
    def body(p_ref, o_ref, buf, send_sems, recv_sems):
        me = _me()
        buf[_flat(me)] = p_ref[...]
        sent = []
        for r in range(1, N_DEV):
            peer = _flip(me, r)
            cp = pltpu.make_async_remote_copy(src_ref=p_ref, dst_ref=buf.at[_flat(me)], send_sem=send_sems.at[r - 1],
                                              recv_sem=recv_sems.at[r - 1], device_id=peer, device_id_type=MESH)
            cp.start()
            sent.append(cp)
        for r in range(1, N_DEV):
            peer = _flip(me, r)
            land = buf.at[_flat(peer)]
            pltpu.make_async_remote_copy(src_ref=land, dst_ref=land, send_sem=send_sems.at[r - 1],
                                         recv_sem=recv_sems.at[r - 1], device_id=peer, device_id_type=MESH).wait_recv()
        for cp in sent:
            cp.wait_send()
        acc = buf[0]
        for d in range(1, N_DEV):
            acc = acc + buf[d]
        o_ref[...] = acc

    vm = pl.BlockSpec(memory_space=pltpu.VMEM)
    return pl.pallas_call(
        body, name="all_reduce_small", in_specs=[vm], out_specs=vm,
        out_shape=jax.ShapeDtypeStruct(pack.shape, F32),
        scratch_shapes=[pltpu.VMEM((N_DEV,) + pack.shape, F32), pltpu.SemaphoreType.DMA((7,)),
                        pltpu.SemaphoreType.DMA((7,))],
    )(pack)


def _adamw(w, g, m, v):
    m = ADAM_B1 * m + (1.0 - ADAM_B1) * g
    v = ADAM_B2 * v + (1.0 - ADAM_B2) * (g * g)
    m_hat = m / (1.0 - ADAM_B1 ** ADAM_STEP)
    v_hat = v / (1.0 - ADAM_B2 ** ADAM_STEP)
    delta = -ADAM_LR * (m_hat / (jnp.sqrt(v_hat) + ADAM_EPS) + ADAM_WD * w)
    return delta, m, v


ADAM_ROWS = 128


def reduce_adamw(recv, w, m, v, name):
    nl, rows, cols = w.shape
    nslot, cp = recv.shape[0], recv.shape[3]

    def body(r_ref, w_ref, m_ref, v_ref, g_ref, d_ref, mo_ref, vo_ref):
        g = r_ref[0, 0].astype(F32)
        for slot in range(1, nslot):
            g = g + r_ref[slot, 0].astype(F32)
        if cp != cols:
            g = g[:, :cols]
        delta, m_new, v_new = _adamw(w_ref[0], g, m_ref[0], v_ref[0])
        g_ref[0] = g
        d_ref[0] = delta
        mo_ref[0] = m_new
        vo_ref[0] = v_new

    tile = pl.BlockSpec((1, ADAM_ROWS, cols), lambda l, t: (l, t, 0))
    out = jax.ShapeDtypeStruct(w.shape, F32)
    return pl.pallas_call(
        body, name=name, grid=(nl, rows // ADAM_ROWS),
        in_specs=[pl.BlockSpec((nslot, 1, ADAM_ROWS, cp), lambda l, t: (0, l, t, 0)), tile, tile, tile],
        out_specs=[tile, tile, tile, tile], out_shape=[out, out, out, out],
        compiler_params=_params("parallel", "parallel"),
    )(recv, w, m, v)


def reduce_adamw_t(recv, w_t, m_t, v_t, name):
    cols, rows = w_t.shape
    nslot, cp = recv.shape[0], recv.shape[3]
    tr = 256

    def body(r_ref, w_ref, m_ref, v_ref, g_ref, d_ref, mo_ref, vo_ref):
        g = r_ref[0, 0].astype(F32)
        for slot in range(1, nslot):
            g = g + r_ref[slot, 0].astype(F32)
        g_ref[...] = g.T[:cols, :]
        delta, m_new, v_new = _adamw(w_ref[...], g_ref[...], m_ref[...], v_ref[...])
        d_ref[...] = delta
        mo_ref[...] = m_new
        vo_ref[...] = v_new

    tile = pl.BlockSpec((cols, tr), lambda t: (0, t))
    out = jax.ShapeDtypeStruct((cols, rows), F32)
    return pl.pallas_call(
        body, name=name, grid=(rows // tr,),
        in_specs=[pl.BlockSpec((nslot, 1, tr, cp), lambda t: (0, 0, t, 0)), tile, tile, tile],
        out_specs=[tile, tile, tile, tile], out_shape=[out, out, out, out], compiler_params=_params("parallel"),
    )(recv, w_t, m_t, v_t)


def adamw_small(g, w, m, v):
    def body(g_ref, w_ref, m_ref, v_ref, d_ref, mo_ref, vo_ref):
        d_ref[...], mo_ref[...], vo_ref[...] = _adamw(w_ref[...], g_ref[...], m_ref[...], v_ref[...])

    out = jax.ShapeDtypeStruct(g.shape, F32)
    return pl.pallas_call(body, name="adamw_small", out_shape=[out, out, out])(g, w, m, v)


def _row8(v):
    return jnp.pad(v.reshape(1, -1).astype(F32), ((0, 7), (0, 0)))


def _row8_lanes(v, width=LANE):
    return jnp.pad(v.reshape(1, -1).astype(F32), ((0, 7), (0, width - v.size)))


def _pack_rows(parts):
    rows = []
    for p in parts:
        p = p.reshape(-1).astype(F32)
        nrow = -(-p.size // D_MODEL)
        rows.append(jnp.pad(p, (0, nrow * D_MODEL - p.size)).reshape(nrow, D_MODEL))
    out = jnp.concatenate(rows, axis=0)
    return jnp.pad(out, ((0, SMALL_ROWS - out.shape[0]), (0, 0)))


def _unpack_rows(pack, shapes):
    out, r = [], 0
    for shp in shapes:
        size = 1
        for d in shp:
            size *= d
        nrow = -(-size // D_MODEL)
        out.append(pack[r:r + nrow].reshape(-1)[:size].reshape(shp))
        r += nrow
    return out


def _dn_weight_layout(gathered):
    split = DN_QKV + 2 * DN_V_HEADS
    pieces = []
    for d in range(N_DEV):
        lo, hi = d * DN_SHARD, (d + 1) * DN_SHARD
        if lo < split < hi:
            pieces += [gathered[d, :, :split - lo], jnp.zeros((D_MODEL, DN_COLS - DN_PROJ), gathered.dtype),
                       gathered[d, :, split - lo:DN_SHARD]]
        else:
            pieces.append(gathered[d, :, :DN_SHARD])
    return jnp.concatenate(pieces, axis=1)


def _dn_grad_blocks(dw):
    split = DN_QKV + 2 * DN_V_HEADS
    gap = DN_COLS - DN_PROJ
    local = lambda c: c if c <= split else c + gap
    zeros = jnp.zeros((D_MODEL, DN_SHARD_PAD - DN_SHARD), dw.dtype)
    blocks = []
    for d in range(N_DEV):
        lo, hi = d * DN_SHARD, (d + 1) * DN_SHARD
        if lo < split < hi:
            parts = [dw[:, lo:split], dw[:, split + gap:hi + gap]]
        else:
            parts = [dw[:, local(lo):local(lo) + DN_SHARD]]
        blocks.append(jnp.concatenate(parts + [zeros], axis=1))
    return jnp.stack(blocks)


def kernel(x, mem, norm_g, mem_norm_g, mem_w_kv, xa_q_norm_g, xa_k_norm_g, w_out, dn_w_in, dn_conv_w, dn_a_log, dn_dt_bias, dn_out_norm_g, sb_w_in, sb_q_norm_g, sb_k_norm_g, loss_target, m_norm_g, m_mem_norm_g, m_mem_w_kv, m_xa_q_norm_g, m_xa_k_norm_g, m_w_out, m_dn_w_in, m_dn_conv_w, m_dn_a_log, m_dn_dt_bias, m_dn_out_norm_g, m_sb_w_in, m_sb_q_norm_g, m_sb_k_norm_g, v_norm_g, v_mem_norm_g, v_mem_w_kv, v_xa_q_norm_g, v_xa_k_norm_g, v_w_out, v_dn_w_in, v_dn_conv_w, v_dn_a_log, v_dn_dt_bias, v_dn_out_norm_g, v_sb_w_in, v_sb_q_norm_g, v_sb_k_norm_g):
    x0, memv, target = x[0], mem[0], loss_target[0]
    my_dev = 4 * lax.axis_index("x") + 2 * lax.axis_index("y") + lax.axis_index("c")

    dn_shard = jnp.pad(dn_w_in[0].astype(BF16), ((0, 0), (0, DN_SHARD_PAD - DN_SHARD)))
    w_out_b = [w_out[i].astype(BF16) for i in range(2)]
    w_kv_b = [mem_w_kv[i].astype(BF16) for i in range(2)]
    conv_shard = jnp.pad(dn_conv_w[0], ((0, 4), (0, 0)))
    g_dn, g_conv = run_exchange(gather_exchange([dn_shard, conv_shard]), "gather_first")
    w_dn = _dn_weight_layout(g_dn)
    conv_w = jnp.transpose(g_conv, (1, 0, 2)).reshape(8, DN_QKV)

    ng = [_row8(norm_g[0]), _row8(norm_g[1])]
    mem_g = _row8(mem_norm_g)
    xqg = [_row8(xa_q_norm_g[0]), _row8(xa_q_norm_g[1])]
    xkg = [_row8(xa_k_norm_g[0]), _row8(xa_k_norm_g[1])]
    alog, dtb = _row8_lanes(dn_a_log[0]), _row8_lanes(dn_dt_bias[0])
    out_g, sbq_g, sbk_g = _row8(dn_out_norm_g[0]), _row8(sb_q_norm_g[0]), _row8(sb_k_norm_g[0])

    mem_n = rms_fwd(memv, mem_g)
    h0 = rms_fwd(x0, ng[0])
    proj0 = matmul(h0, w_dn, "nn", F32, 1024, 1152, 2048, "proj_dn")
    act = dn_conv_fwd(proj0, conv_w)
    u0, w_, qd, kd, qk, cd, g_wo0, g_kv0 = dn_prep_fwd(
        act, proj0, alog, dtb, exchange=gather_exchange([w_out_b[0], w_kv_b[0]]))
    o_raw, states, w_sb = dn_scan_fwd(u0, w_, qd, kd, qk, cd, MIX_WIDTH,
                                      exchange=gather_exchange([sb_w_in[0].astype(BF16)]))
    w_o = [g_wo0.reshape(INNER, D_MODEL), None]
    w_kv = [g_kv0.reshape(D_MODEL, 2 * XA_WIDTH), None]
    kv = [matmul(mem_n, w_kv[0], "nn", F32, 256, 1024, 2048, "kv0"), None]
    cat0 = head_norm_fwd(o_raw, 0, DN_V_HEADS, out_g[None], F32, INNER, "dn_out_norm")
    cat0 = xa_fwd(proj0, DN_XQ_BLK, kv[0], xqg[0], xkg[0], cat0)
    y0 = gate_fwd(cat0, proj0, DN_Z_BLK)
    x1 = matmul(y0, w_o[0], "nn", F32, 1024, 1024, 2048, "out_proj0", add=x0)

    h1 = rms_fwd(x1, ng[1])
    proj1 = matmul(h1, w_sb, "nn", F32, 1024, 896, 2048, "proj_sb", b_blocked=True)
    qkn = head_norm_fwd(proj1, 0, 2 * SB_HEADS, jnp.stack([sbq_g, sbk_g]), BF16, 2 * MIX_WIDTH, "sb_qk_norm")
    cat1, ltot, g_wo1, g_kv1 = sb_fwd(qkn, proj1, 2 * SB_HEADS, INNER,
                                      exchange=gather_exchange([w_out_b[1], w_kv_b[1]]))
    w_o[1] = g_wo1.reshape(INNER, D_MODEL)
    w_kv[1] = g_kv1.reshape(D_MODEL, 2 * XA_WIDTH)
    kv[1] = matmul(mem_n, w_kv[1], "nn", F32, 256, 1024, 2048, "kv1")
    cat1 = xa_fwd(proj1, SB_XQ_BLK, kv[1], xqg[1], xkg[1], cat1)
    y1 = gate_fwd(cat1, proj1, SB_Z_BLK)
    x2 = matmul(y1, w_o[1], "nn", F32, 1024, 1024, 2048, "out_proj1", add=x1)
    loss_part, dx2 = loss_fwd_bwd(x2, target)

    dy1 = matmul(dx2, w_o[1], "nt", F32, 1024, 1024, 2048, "d_y1")
    dw_o1 = matmul(y1, dx2, "tn", BF16, 1024, 1024, 1024, "d_w_out1")
    dcat1, dproj1 = gate_bwd(dy1, cat1, proj1, SB_Z_BLK)
    dproj1, dxk1, dxv1, dxqg1, dxkg1 = xa_bwd(proj1, SB_XQ_BLK, kv[1], xqg[1], xkg[1], dcat1, dproj1)
    dqn, dkn, dproj1 = sb_bwd(qkn, proj1, 2 * SB_HEADS, dcat1, ltot, dproj1)
    dproj1, d_sbq = head_norm_bwd(dqn, 0, proj1, 0, SB_HEADS, sbq_g, dproj1, 0, "sb_q_norm_bwd")
    dproj1, d_sbk = head_norm_bwd(dkn, 0, proj1, SB_HEADS, SB_HEADS, sbk_g, dproj1, SB_HEADS, "sb_k_norm_bwd")
    dw_sb = matmul(h1, dproj1, "tn", BF16, 1024, 896, 1024, "d_w_sb", out_blocks=N_DEV)
    dh1 = matmul(dproj1, w_sb, "nt", F32, 1024, 1024, 896, "d_h1", b_blocked=True)
    dx1, d_ng1 = rms_bwd(dh1, x1, ng[1], dx2)

    dy0 = matmul(dx1, w_o[0], "nt", F32, 1024, 1024, 2048, "d_y0")
    dw_o0 = matmul(y0, dx1, "tn", BF16, 1024, 1024, 1024, "d_w_out0")
    dcat0, dproj0 = gate_bwd(dy0, cat0, proj0, DN_Z_BLK)
    dproj0, dxk0, dxv0, dxqg0, dxkg0 = xa_bwd(proj0, DN_XQ_BLK, kv[0], xqg[0], xkg[0], dcat0, dproj0)

    dkv = [jnp.concatenate([dxk0, dxv0], axis=1), jnp.concatenate([dxk1, dxv1], axis=1)]
    dw_kv = [matmul(mem_n, dkv[i], "tn", BF16, 1024, 1024, 256, f"d_w_kv{i}") for i in range(2)]
    dmem_n = matmul(dkv[0], w_kv[0], "nt", F32, 256, 1024, 2048, "d_mem_n0")
    dmem_n = matmul(dkv[1], w_kv[1], "nt", F32, 256, 1024, 2048, "d_mem_n1", add=dmem_n)
    _, d_memg = rms_bwd(dmem_n, memv, mem_g, None)

    by_owner = lambda g, rows: g.reshape(4, 2, rows, g.size // (N_DEV * rows))
    grads = [dw_sb.reshape(4, 2, D_MODEL, SB_PROJ // N_DEV), by_owner(dw_o0, INNER // N_DEV),
             by_owner(dw_o1, INNER // N_DEV), by_owner(dw_kv[0], D_MODEL // N_DEV), by_owner(dw_kv[1], D_MODEL // N_DEV)]
    do_raw, d_outg = head_norm_bwd(dcat0, 0, o_raw, 0, DN_V_HEADS, out_g, None, 0, "dn_out_norm_bwd")
    du0, dw_, dqd, dkd, dqk, dcd, *halves = dn_scan_bwd(do_raw, u0, w_, qd, kd, qk, cd, states,
                                                         exchange=pair_exchange(grads))
    sums = [pair_sum(g, h, f"pair_sum{i}") for i, (g, h) in enumerate(zip(grads, halves))]
    to_chips = chip_exchange(sums, [(0, 0), (1, 0), (1, 1), (2, 0), (2, 1)],
                             [(4, 1, D_MODEL, SB_PROJ // N_DEV), (4, 2, INNER // N_DEV, D_MODEL),
                              (4, 2, D_MODEL // N_DEV, 2 * XA_WIDTH)])
    dq_a, dk_a, dv_a, dab, d_alog, d_dtb, recv_sb, recv_wo, recv_kv = dn_prep_bwd(
        act, proj0, alog, dtb, du0, dw_, dqd, dkd, dqk, dcd, exchange=to_chips)
    dproj0, dcw_q = dn_conv_bwd(dq_a, proj0, conv_w, dproj0, None, 0, "dn_conv_bwd_q")
    dproj0, dcw_k = dn_conv_bwd(dk_a, proj0, conv_w, dproj0, None, DN_QK_HEADS, "dn_conv_bwd_k")
    dproj0, dcw_v = dn_conv_bwd(dv_a, proj0, conv_w, dproj0, dab, 2 * DN_QK_HEADS, "dn_conv_bwd_v")
    dw_dn = matmul(h0, dproj0, "tn", BF16, 1024, 1152, 1024, "d_w_dn")
    dn_grad = _dn_grad_blocks(dw_dn).reshape(4, 2, D_MODEL, DN_SHARD_PAD)
    dh0, dn_half = matmul(dproj0, w_dn, "nt", F32, 1024, 1024, 1152, "d_h0", exchange=pair_exchange([dn_grad]))
    grad_x, d_ng0 = rms_bwd(dh0, x0, ng[0], dx1)
    dn_sum = pair_sum(dn_grad, dn_half, "pair_sum_dn")
    recv_dn, = run_exchange(chip_exchange([dn_sum], [(0, 0)], [(4, 1, D_MODEL, DN_SHARD_PAD)]), "reduce_dn")

    big = {
        "dn_w_in": tuple(jnp.transpose(a)[None] for a in reduce_adamw_t(
            recv_dn, jnp.transpose(dn_w_in[0]), jnp.transpose(m_dn_w_in[0]), jnp.transpose(v_dn_w_in[0]),
            "adamw_dn_w_in")),
        "sb_w_in": reduce_adamw(recv_sb, sb_w_in, m_sb_w_in, v_sb_w_in, "adamw_sb_w_in"),
        "w_out": reduce_adamw(recv_wo, w_out, m_w_out, v_w_out, "adamw_w_out"),
        "mem_w_kv": reduce_adamw(recv_kv, mem_w_kv, m_mem_w_kv, v_mem_w_kv, "adamw_mem_w_kv"),
    }

    dconv = jnp.concatenate([dcw_q, dcw_k, dcw_v], axis=1)[:4]
    small_shapes = [(2, D_MODEL), (D_MODEL,), (2, XA_DIM), (2, XA_DIM), (4, DN_QKV), (1, DN_V_HEADS),
                    (1, DN_V_HEADS), (1, HEAD_DIM), (1, HEAD_DIM), (1, HEAD_DIM), (1,)]
    pack = _pack_rows([jnp.stack([d_ng0[0], d_ng1[0]]), d_memg[0], jnp.stack([dxqg0[0], dxqg1[0]]),
                       jnp.stack([dxkg0[0], dxkg1[0]]), dconv, d_alog[0, :DN_V_HEADS], d_dtb[0, :DN_V_HEADS],
                       d_outg[0], d_sbq[0], d_sbk[0], loss_part[0, :1]])
    total = all_reduce_small(pack)
    (g_norm, g_memn, g_xq, g_xk, g_conv_full, g_alog, g_dtb, g_outn, g_sbq, g_sbk, loss1) = _unpack_rows(total, small_shapes)
    conv_cols = DN_QKV // N_DEV
    g_conv = lax.dynamic_slice(g_conv_full, (0, my_dev * conv_cols), (4, conv_cols))[None]
    names = ["norm_g", "mem_norm_g", "xa_q_norm_g", "xa_k_norm_g", "dn_conv_w", "dn_a_log", "dn_dt_bias",
             "dn_out_norm_g", "sb_q_norm_g", "sb_k_norm_g"]
    grads = [g_norm, g_memn, g_xq, g_xk, g_conv, g_alog, g_dtb, g_outn, g_sbq, g_sbk]
    ws = [norm_g, mem_norm_g, xa_q_norm_g, xa_k_norm_g, dn_conv_w, dn_a_log, dn_dt_bias, dn_out_norm_g, sb_q_norm_g,
          sb_k_norm_g]
    ms = [m_norm_g, m_mem_norm_g, m_xa_q_norm_g, m_xa_k_norm_g, m_dn_conv_w, m_dn_a_log, m_dn_dt_bias,
          m_dn_out_norm_g, m_sb_q_norm_g, m_sb_k_norm_g]
    vs = [v_norm_g, v_mem_norm_g, v_xa_q_norm_g, v_xa_k_norm_g, v_dn_conv_w, v_dn_a_log, v_dn_dt_bias,
          v_dn_out_norm_g, v_sb_q_norm_g, v_sb_k_norm_g]
    shapes = [w.shape for w in ws]
    d_p, m_p, v_p = adamw_small(_pack_rows(grads), _pack_rows(ws), _pack_rows(ms), _pack_rows(vs))
    small = dict(zip(names, zip(grads, _unpack_rows(d_p, shapes), _unpack_rows(m_p, shapes), _unpack_rows(v_p, shapes))))

    order = ["norm_g", "mem_norm_g", "mem_w_kv", "xa_q_norm_g", "xa_k_norm_g", "w_out", "dn_w_in", "dn_conv_w",
             "dn_a_log", "dn_dt_bias", "dn_out_norm_g", "sb_w_in", "sb_q_norm_g", "sb_k_norm_g"]
    res = {**big, **small}
    outs = [loss1.reshape(()), grad_x[None]]
    for k in range(4):
        outs += [res[n][k] for n in order]
    return tuple(outs)
```

```python
import functools

import jax
import jax.numpy as jnp
from jax import lax
from jax.experimental import pallas as pl
from jax.experimental.pallas import tpu as pltpu

F32 = jnp.float32
BF16 = jnp.bfloat16

D_MODEL = 2048
SEQ = 2048
N_MEM = 256
INNER = 4096
XA_HEADS = 4
XA_WIDTH = 1024
XA_DIM = 256
MIX_WIDTH = 3072
HEAD_DIM = 128
DN_V_HEADS = 24
DN_QK_HEADS = 12
DN_QK_WIDTH = 1536
DN_CHUNK = 64
DN_QKV = 2 * DN_QK_WIDTH + MIX_WIDTH
DN_PROJ = 11312
SB_HEADS = 24
SB_PROJ = 14336
EPS = 1e-6
N_DEV = 8
DN_SHARD = DN_PROJ // N_DEV
DN_SHARD_PAD = 1536
LANE = 128
DN_COLS = 90 * LANE
DN_AB_BLK, DN_PAD_BLK, DN_XQ_BLK, DN_Z_BLK = 48, 49, 50, 58
SB_XQ_BLK, SB_Z_BLK = 72, 80

ADAM_LR, ADAM_B1, ADAM_B2, ADAM_EPS, ADAM_WD, ADAM_STEP = 0.001, 0.9, 0.999, 1e-08, 0.01, 10

VMEM_LIMIT = 56 * 1024 * 1024
MESH = pl.DeviceIdType.MESH

_NN, _NT, _TN = "nn", "nt", "tn"


def _dims(mode, rank):
    lhs, rhs = {"nn": (1, 0), "nt": (1, 1), "tn": (0, 0)}[mode]
    if rank == 2:
        return (((lhs,), (rhs,)), ((), ()))
    return (((lhs + 1,), (rhs + 1,)), ((0,), (0,)))


def _params(*sem):
    return pltpu.CompilerParams(dimension_semantics=sem if sem else None, vmem_limit_bytes=VMEM_LIMIT)


def _dot(a, b, mode):
    return lax.dot_general(a, b, _dims(mode, a.ndim), preferred_element_type=F32)


def _dg(a, b, dims):
    return _dot(a.astype(BF16), b.astype(BF16), dims)


@jax.custom_vjp
def mm(a, b):
    return _dg(a, b, _NN)


@jax.custom_vjp
def mm_nt(a, b):
    return _dg(a, b, _NT)


@jax.custom_vjp
def mm_tn(a, b):
    return _dg(a, b, _TN)


mm.defvjp(lambda a, b: (_dg(a, b, _NN), (a, b)), lambda r, g: (mm_nt(g, r[1]), mm_tn(r[0], g)))
mm_nt.defvjp(lambda a, b: (_dg(a, b, _NT), (a, b)), lambda r, g: (mm(g, r[1]), mm_tn(g, r[0])))
mm_tn.defvjp(lambda a, b: (_dg(a, b, _TN), (a, b)), lambda r, g: (mm_nt(r[1], g), mm(r[0], g)))


def _split3(x):
    hi = x.astype(BF16)
    r1 = x - hi.astype(F32)
    mid = r1.astype(BF16)
    lo = (r1 - mid.astype(F32)).astype(BF16)
    return hi, mid, lo


def _dg3(a, b, dims):
    ah, am, _ = _split3(a)
    bh, bm, _ = _split3(b)
    return _dot(ah, bh, dims) + (_dot(ah, bm, dims) + _dot(am, bh, dims))


def _dg_exact_rhs(a, b01, dims):
    ah, am, _ = _split3(a)
    b = b01.astype(BF16)
    return _dot(ah, b, dims) + _dot(am, b, dims)


def _dg_exact_lhs(a01, b, dims):
    bh, bm, bl = _split3(b)
    a = a01.astype(BF16)
    return _dot(a, bh, dims) + (_dot(a, bm, dims) + _dot(a, bl, dims))


@jax.custom_vjp
def mm3(a, b):
    return _dg3(a, b, _NN)


@jax.custom_vjp
def mm3_nt(a, b):
    return _dg3(a, b, _NT)


@jax.custom_vjp
def mm3_tn(a, b):
    return _dg3(a, b, _TN)


mm3.defvjp(lambda a, b: (_dg3(a, b, _NN), (a, b)), lambda r, g: (mm3_nt(g, r[1]), mm3_tn(r[0], g)))
mm3_nt.defvjp(lambda a, b: (_dg3(a, b, _NT), (a, b)), lambda r, g: (mm3(g, r[1]), mm3_tn(g, r[0])))
mm3_tn.defvjp(lambda a, b: (_dg3(a, b, _TN), (a, b)), lambda r, g: (mm3_nt(r[1], g), mm3(r[0], g)))


def _softplus(x):
    return jnp.maximum(x, 0.0) + jnp.log(1.0 + jnp.exp(-jnp.abs(x)))


def _log_sigmoid(x):
    return jnp.minimum(x, 0.0) - jnp.log(1.0 + jnp.exp(-jnp.abs(x)))


def _sigmoid(x):
    return 1.0 / (1.0 + jnp.exp(-x))


def _silu(x):
    return x * _sigmoid(x)


def _silu_grad(x):
    s = _sigmoid(x)
    return s * (1.0 + x * (1.0 - s))


@jax.custom_vjp
def mm01(a01, b):
    return _dg_exact_lhs(a01, b, _NN)


mm01.defvjp(lambda a, b: (_dg_exact_lhs(a, b, _NN), a),
            lambda a, g: (jnp.zeros_like(a), _dg_exact_lhs(a, g, _TN)))


def _lane_pick(x, idx):
    lane = lax.broadcasted_iota(jnp.int32, x.shape, x.ndim - 1)
    return jnp.sum(jnp.where(lane == idx, x, 0.0), axis=-1, keepdims=True)


def _dn_chunk(qt, kt, v, ab, alog, dtb, h):
    B, C = qt.shape[0], DN_CHUNK
    g = -jnp.exp(_lane_pick(alog, h)) * _softplus(_lane_pick(ab, h) + _lane_pick(dtb, h))
    beta = _sigmoid(_lane_pick(ab, h + DN_V_HEADS))
    q = qt * lax.rsqrt(jnp.sum(qt * qt, axis=-1, keepdims=True) + EPS) * (HEAD_DIM ** -0.5)
    k = kt * lax.rsqrt(jnp.sum(kt * kt, axis=-1, keepdims=True) + EPS)
    row = lax.broadcasted_iota(jnp.int32, (B, C, C), 1)
    col = lax.broadcasted_iota(jnp.int32, (B, C, C), 2)
    lower = (row >= col).astype(F32)
    ones = jnp.ones((B, C, C), F32)
    g_wide = jnp.broadcast_to(g, (B, C, LANE))
    g_sq = jnp.broadcast_to(g, (B, C, C))
    gc = mm01(lower, g_wide)
    gc_i = gc[:, :, :C]
    gc_j = mm01(ones, jnp.where(row <= col, g_sq, 0.0))
    g_last = jnp.broadcast_to(gc[:, C - 1:C, :], (B, C, LANE))
    decay = jnp.exp(jnp.where(row >= col, gc_i - gc_j, -1e30))
    eg = jnp.exp(gc)
    kk = mm_nt(k, k)
    a_mat = jnp.where(row > col, jnp.broadcast_to(beta, (B, C, C)) * kk * decay, 0.0)
    eye = (row == col).astype(F32)
    y = -a_mat
    t = eye + y
    for _ in range(5):
        y = mm(y, y)
        t = t + mm(t, y)
    bb = jnp.broadcast_to(beta, (B, C, LANE))
    u0 = mm3(t, v * bb)
    w = mm3(t, k * (bb * eg))
    qk = mm_nt(q, k) * decay
    q_dec = q * eg
    k_dec = k * jnp.exp(g_last - gc)
    cd = jnp.exp(g_last)[:, :8, :]
    return u0, w, qk, q_dec, k_dec, cd


def _shift_rows(x, j, down):
    if j == 0:
        return x
    n = x.shape[0]
    r = lax.broadcasted_iota(jnp.int32, x.shape, 0)
    if down:
        return jnp.where(r >= j, pltpu.roll(x, j, 0), 0.0)
    return jnp.where(r < n - j, pltpu.roll(x, n - j, 0), 0.0)


def dn_conv_fwd(proj, conv_w):
    s = proj.shape[0]

    def body(x_ref, w_ref, o_ref):
        x = x_ref[...]
        w = w_ref[...]
        pre = x * w[3:4, :]
        for j in (1, 2, 3):
            pre = pre + _shift_rows(x, j, True) * w[3 - j:4 - j, :]
        o_ref[...] = _silu(pre)

    return pl.pallas_call(
        body, name="dn_conv_fwd", grid=(DN_QKV // LANE,),
        in_specs=[pl.BlockSpec((s, LANE), lambda j: (0, j)), pl.BlockSpec((8, LANE), lambda j: (0, j))],
        out_specs=pl.BlockSpec((s, LANE), lambda j: (0, j)),
        out_shape=jax.ShapeDtypeStruct((s, DN_QKV), F32), compiler_params=_params("parallel"),
    )(proj, conv_w)


def dn_conv_bwd(dact, proj, conv_w, dproj_in, dab, blk0, name):
    s = proj.shape[0]
    nblk = dact.shape[1] // LANE
    extra = 2 if dab is not None else 0

    def body(*refs):
        if dab is not None:
            da_ref, x_ref, w_ref, _, dab_ref, dp_ref, dw_ref = refs
        else:
            da_ref, x_ref, w_ref, _, dp_ref, dw_ref = refs
        j = pl.program_id(0)

        @pl.when(j < nblk)
        def _():
            x = x_ref[...]
            w = w_ref[...]
            xs = [_shift_rows(x, 3 - kk_, True) for kk_ in range(4)]
            pre = xs[0] * w[0:1, :]
            for kk_ in (1, 2, 3):
                pre = pre + xs[kk_] * w[kk_:kk_ + 1, :]
            dpre = da_ref[...] * _silu_grad(pre)
            dx = dpre * w[3:4, :]
            for jj in (1, 2, 3):
                dx = dx + _shift_rows(dpre, jj, False) * w[3 - jj:4 - jj, :]
            dp_ref[...] = dx
            rows = [jnp.sum(dpre * xs[kk_], axis=0, keepdims=True) for kk_ in range(4)]
            dw_ref[...] = jnp.concatenate(rows + [jnp.zeros((4, LANE), F32)], axis=0)

        if dab is not None:
            @pl.when(j == nblk)
            def _():
                dp_ref[...] = dab_ref[...]

            @pl.when(j == nblk + 1)
            def _():
                dp_ref[...] = jnp.zeros_like(dp_ref)

    cl = lambda j: jnp.minimum(j, nblk - 1)
    in_specs = [pl.BlockSpec((s, LANE), lambda j: (0, cl(j))),
                pl.BlockSpec((s, LANE), lambda j: (0, blk0 + cl(j))),
                pl.BlockSpec((8, LANE), lambda j: (0, blk0 + cl(j))),
                pl.BlockSpec(memory_space=pl.ANY)]
    args = [dact, proj, conv_w, dproj_in]
    if dab is not None:
        in_specs.append(pl.BlockSpec((s, LANE), lambda j: (0, 0)))
        args.append(dab)
    return pl.pallas_call(
        body, name=name, grid=(nblk + extra,), in_specs=in_specs,
        out_specs=[pl.BlockSpec((s, LANE), lambda j: (0, blk0 + j)), pl.BlockSpec((8, LANE), lambda j: (0, cl(j)))],
        out_shape=[jax.ShapeDtypeStruct(dproj_in.shape, F32), jax.ShapeDtypeStruct((8, dact.shape[1]), F32)],
        input_output_aliases={3: 0}, compiler_params=_params("arbitrary"),
    )(*args)


DN_GROUP = 8
DN_ROWS = DN_GROUP * DN_CHUNK


def dn_prep_fwd(act, proj, alog, dtb, exchange=None):
    s = act.shape[0]
    nc = s // DN_CHUNK
    C = DN_CHUNK

    def body(q_ref, k_ref, v_ref, ab_ref, al_ref, dt_ref, u0_ref, w_ref, qd_ref, kd_ref, qk_ref, cd_ref):
        qh = pl.program_id(0)
        al = al_ref[0:1, :]
        dt = dt_ref[0:1, :]
        chunks = lambda x: x.reshape(DN_GROUP, C, x.shape[-1])
        rows = lambda x: x.reshape(DN_ROWS, x.shape[-1])
        qt, kt, ab = chunks(q_ref[...]), chunks(k_ref[...]), chunks(ab_ref[...])
        for hv in range(2):
            cs = slice(hv * LANE, (hv + 1) * LANE)
            u0, w, qk, qd, kd, cd = _dn_chunk(qt, kt, chunks(v_ref[:, cs]), ab, al, dt, 2 * qh + hv)
            u0_ref[:, cs] = rows(u0)
            w_ref[:, cs] = rows(w)
            qd_ref[:, cs] = rows(qd)
            kd_ref[:, cs] = rows(kd)
            qk_ref[hv] = rows(qk)
            cd_ref[hv] = cd

    big = pl.BlockSpec((DN_ROWS, 2 * LANE), lambda h, g: (g, h))
    wide = jax.ShapeDtypeStruct((s, MIX_WIDTH), F32)
    return hosted_call(
        body, exchange, name="dn_prep_fwd", grid=(DN_QK_HEADS, s // DN_ROWS),
        in_specs=[pl.BlockSpec((DN_ROWS, LANE), lambda h, g: (g, h)),
                  pl.BlockSpec((DN_ROWS, LANE), lambda h, g: (g, DN_QK_HEADS + h)),
                  pl.BlockSpec((DN_ROWS, 2 * LANE), lambda h, g: (g, DN_QK_HEADS + h)),
                  pl.BlockSpec((DN_ROWS, LANE), lambda h, g: (g, DN_AB_BLK)),
                  pl.BlockSpec((8, LANE), lambda h, g: (0, 0)), pl.BlockSpec((8, LANE), lambda h, g: (0, 0))],
        out_specs=[big, big, big, big,
                   pl.BlockSpec((2, DN_ROWS, C), lambda h, g: (h, g, 0)),
                   pl.BlockSpec((2, DN_GROUP, 8, LANE), lambda h, g: (h, g, 0, 0))],
        out_shape=[wide, wide, wide, wide, jax.ShapeDtypeStruct((DN_V_HEADS, s, C), F32),
                   jax.ShapeDtypeStruct((DN_V_HEADS, nc, 8, LANE), F32)],
        sem=("parallel", "parallel"),
    )(act, act, act, proj, alog, dtb)


def dn_prep_bwd(act, proj, alog, dtb, du0, dw, dqd, dkd, dqk, dcd, exchange=None):
    s = act.shape[0]
    C = DN_CHUNK

    def body(q_ref, k_ref, v_ref, ab_ref, al_ref, dt_ref, du0_ref, dw_ref, dqd_ref, dkd_ref, dqk_ref, dcd_ref,
             dq_ref, dk_ref, dv_ref, dab_ref, dal_ref, ddt_ref):
        g_id = pl.program_id(0)
        qh = pl.program_id(1)
        al = al_ref[0:1, :]
        dt = dt_ref[0:1, :]

        @pl.when(qh == 0)
        def _():
            dab_ref[...] = jnp.zeros_like(dab_ref)

        @pl.when((qh == 0) & (g_id == 0))
        def _():
            dal_ref[...] = jnp.zeros_like(dal_ref)
            ddt_ref[...] = jnp.zeros_like(ddt_ref)

        chunks = lambda x: x.reshape(DN_GROUP, C, x.shape[-1])
        rows = lambda x: x.reshape(DN_ROWS, x.shape[-1])
        qt, kt, ab = chunks(q_ref[...]), chunks(k_ref[...]), chunks(ab_ref[...])
        dq_acc = jnp.zeros((DN_ROWS, LANE), F32)
        dk_acc = jnp.zeros((DN_ROWS, LANE), F32)
        for hv in range(2):
            cs = slice(hv * LANE, (hv + 1) * LANE)
            h = 2 * qh + hv
            f = lambda qt_, kt_, v_, ab_, a_, d_: _dn_chunk(qt_, kt_, v_, ab_, a_, d_, h)
            _, vjp = jax.vjp(f, qt, kt, chunks(v_ref[:, cs]), ab, al, dt)
            dq, dk, dv, dab, dal, ddt = vjp((chunks(du0_ref[:, cs]), chunks(dw_ref[:, cs]), chunks(dqk_ref[hv]),
                                             chunks(dqd_ref[:, cs]), chunks(dkd_ref[:, cs]), dcd_ref[hv]))
            dq_acc = dq_acc + rows(dq)
            dk_acc = dk_acc + rows(dk)
            dv_ref[:, cs] = rows(dv)
            dab_ref[...] += rows(dab)
            dal_ref[0:1, :] += dal
            ddt_ref[0:1, :] += ddt
        dq_ref[...] = dq_acc
        dk_ref[...] = dk_acc

    big = pl.BlockSpec((DN_ROWS, 2 * LANE), lambda g, h: (g, h))
    one = pl.BlockSpec((DN_ROWS, LANE), lambda g, h: (g, h))
    small = pl.BlockSpec((8, LANE), lambda g, h: (0, 0))
    return hosted_call(
        body, exchange, name="dn_prep_bwd", grid=(s // DN_ROWS, DN_QK_HEADS),
        in_specs=[one, pl.BlockSpec((DN_ROWS, LANE), lambda g, h: (g, DN_QK_HEADS + h)),
                  pl.BlockSpec((DN_ROWS, 2 * LANE), lambda g, h: (g, DN_QK_HEADS + h)),
                  pl.BlockSpec((DN_ROWS, LANE), lambda g, h: (g, DN_AB_BLK)), small, small,
                  big, big, big, big,
                  pl.BlockSpec((2, DN_ROWS, C), lambda g, h: (h, g, 0)),
                  pl.BlockSpec((2, DN_GROUP, 8, LANE), lambda g, h: (h, g, 0, 0))],
        out_specs=[one, one, big, pl.BlockSpec((DN_ROWS, LANE), lambda g, h: (g, 0)), small, small],
        out_shape=[jax.ShapeDtypeStruct((s, DN_QK_WIDTH), F32), jax.ShapeDtypeStruct((s, DN_QK_WIDTH), F32),
                   jax.ShapeDtypeStruct((s, MIX_WIDTH), F32), jax.ShapeDtypeStruct((s, LANE), F32),
                   jax.ShapeDtypeStruct((8, LANE), F32), jax.ShapeDtypeStruct((8, LANE), F32)],
        sem=("arbitrary", "arbitrary"),
    )(act, act, act, proj, alog, dtb, du0, dw, dqd, dkd, dqk, dcd)


DN_SCAN_HEADS = 2


def dn_scan_fwd(u0, w, qd, kd, qk, cd, width, exchange=None):
    s = u0.shape[0]
    nc = s // DN_CHUNK
    C = DN_CHUNK

    nh = DN_SCAN_HEADS
    heads = range(nh)
    cols = [slice(h * LANE, (h + 1) * LANE) for h in heads]

    def body(u0_ref, w_ref, qd_ref, kd_ref, qk_ref, cd_ref, o_ref, st_ref):
        def step(c, states):
            rs = pl.ds(pl.multiple_of(c * C, C), C)
            for h in heads:
                st_ref[h, c] = states[h]
            ws = [_dg(w_ref[rs, cols[h]], states[h], _NN) for h in heads]
            us = [u0_ref[rs, cols[h]] - ws[h] for h in heads]
            os_ = [_dg(qd_ref[rs, cols[h]], states[h], _NN) for h in heads]
            for h in heads:
                o_ref[rs, cols[h]] = os_[h] + _dg(qk_ref[h, rs, :], us[h], _NN)
            return tuple(cd_ref[h, c][0:1, :] * states[h] + _dg(kd_ref[rs, cols[h]], us[h], _TN) for h in heads)

        lax.fori_loop(0, nc, step, tuple(jnp.zeros((HEAD_DIM, HEAD_DIM), F32) for _ in heads))

    col = pl.BlockSpec((s, nh * LANE), lambda h: (0, h))
    return hosted_call(
        body, exchange, name="dn_scan_fwd", grid=(DN_V_HEADS // nh,),
        in_specs=[col, col, col, col, pl.BlockSpec((nh, s, C), lambda h: (h, 0, 0)),
                  pl.BlockSpec((nh, nc, 8, LANE), lambda h: (h, 0, 0, 0))],
        out_specs=[col, pl.BlockSpec((nh, nc, HEAD_DIM, HEAD_DIM), lambda h: (h, 0, 0, 0))],
        out_shape=[jax.ShapeDtypeStruct((s, width), F32),
                   jax.ShapeDtypeStruct((DN_V_HEADS, nc, HEAD_DIM, HEAD_DIM), F32)],
        sem=("parallel",),
    )(u0, w, qd, kd, qk, cd)


def dn_scan_bwd(do, u0, w, qd, kd, qk, cd, states, exchange=None):
    s = u0.shape[0]
    nc = s // DN_CHUNK
    C = DN_CHUNK

    nh = DN_SCAN_HEADS
    heads = range(nh)
    cols = [slice(h * LANE, (h + 1) * LANE) for h in heads]

    def body(do_ref, u0_ref, w_ref, qd_ref, kd_ref, qk_ref, cd_ref, st_ref,
             du0_ref, dw_ref, dqd_ref, dkd_ref, dqk_ref, dcd_ref):
        def step(i, dstates):
            c = nc - 1 - i
            rs = pl.ds(pl.multiple_of(c * C, C), C)
            states = [st_ref[h, c] for h in heads]
            gs = [do_ref[rs, cols[h]] for h in heads]
            w_cs = [w_ref[rs, cols[h]] for h in heads]
            qd_cs = [qd_ref[rs, cols[h]] for h in heads]
            cd_rows = [cd_ref[h, c][0:1, :] for h in heads]
            us = [u0_ref[rs, cols[h]] - _dg(w_cs[h], states[h], _NN) for h in heads]
            dus = [_dg(qk_ref[h, rs, :], gs[h], _TN) + _dg(kd_ref[rs, cols[h]], dstates[h], _NN) for h in heads]
            for h in heads:
                du0_ref[rs, cols[h]] = dus[h]
                dw_ref[rs, cols[h]] = -_dg(dus[h], states[h], _NT)
                dqd_ref[rs, cols[h]] = _dg(gs[h], states[h], _NT)
                dkd_ref[rs, cols[h]] = _dg(us[h], dstates[h], _NT)
                dqk_ref[h, rs, :] = _dg(gs[h], us[h], _NT)
                dcd_row = jnp.sum(states[h] * dstates[h], axis=0, keepdims=True)
                dcd_ref[h, c] = jnp.concatenate([dcd_row, jnp.zeros((7, LANE), F32)], axis=0)
            return tuple(cd_rows[h] * dstates[h] + _dg(qd_cs[h], gs[h], _TN) - _dg(w_cs[h], dus[h], _TN)
                         for h in heads)

        lax.fori_loop(0, nc, step, tuple(jnp.zeros((HEAD_DIM, HEAD_DIM), F32) for _ in heads))

    col = pl.BlockSpec((s, nh * LANE), lambda h: (0, h))
    qk_spec = pl.BlockSpec((nh, s, C), lambda h: (h, 0, 0))
    cd_spec = pl.BlockSpec((nh, nc, 8, LANE), lambda h: (h, 0, 0, 0))
    wide = jax.ShapeDtypeStruct((s, MIX_WIDTH), F32)
    return hosted_call(
        body, exchange, name="dn_scan_bwd", grid=(DN_V_HEADS // nh,),
        in_specs=[col, col, col, col, col, qk_spec, cd_spec,
                  pl.BlockSpec((nh, nc, HEAD_DIM, HEAD_DIM), lambda h: (h, 0, 0, 0))],
        out_specs=[col, col, col, col, qk_spec, cd_spec],
        out_shape=[wide, wide, wide, wide, jax.ShapeDtypeStruct((DN_V_HEADS, s, C), F32),
                   jax.ShapeDtypeStruct((DN_V_HEADS, nc, 8, LANE), F32)],
        sem=("parallel",),
    )(do, u0, w, qd, kd, qk, cd, states)


SB_T = 256
SB_GROUPS = (4, 2, 1)


def _sb_scores(q, kbs, diff, lims):
    zs = [_dg(q, kb, _NT) * (HEAD_DIM ** -0.5) for kb in kbs]
    masks = [diff < lim for lim in lims]
    lss = [_log_sigmoid(z) for z in zs]
    lrs = [jnp.where(m, ls - z, 0.0) for m, ls, z in zip(masks, lss, zs)]
    return masks, lss, lrs


def _sb_diff():
    return lax.broadcasted_iota(jnp.int32, (SB_T, SB_T), 1) - lax.broadcasted_iota(jnp.int32, (SB_T, SB_T), 0)


def _sb_loop(n_tiles, step, carry):
    done = 0
    for size in SB_GROUPS:
        groups = (n_tiles - done) // size
        carry = lax.fori_loop(0, groups, lambda p, c, s=size, d=done: step(d + p * s, s, c), carry)
        done = done + groups * size
    return carry


def sb_fwd(qkn, proj, v_blk0, width, exchange=None):
    s = qkn.shape[0]

    def body(q_ref, k_ref, v_ref, o_ref, lt_ref):
        i = pl.program_id(1)
        q = q_ref[...]
        diff = _sb_diff()
        after = (diff < 0).astype(BF16)

        def step(first, n, carry):
            run, acc = carry
            tiles = [first + t for t in range(n)]
            kss = [pl.ds(pl.multiple_of((i - t) * SB_T, SB_T), SB_T) for t in tiles]
            masks, lss, lrs = _sb_scores(q, [k_ref[ks, :] for ks in kss], diff, [t * SB_T for t in tiles])
            within = [_dg_exact_rhs(lr, after, _NN) for lr in lrs]
            sums = [jnp.sum(lr, axis=1, keepdims=True) for lr in lrs]
            for t in range(n):
                wts = jnp.where(masks[t], jnp.exp(lss[t] + (within[t] + run)), 0.0)
                acc = acc + _dg(wts, v_ref[kss[t], :], _NN)
                run = run + sums[t]
            return run, acc

        run, acc = _sb_loop(i + 1, step, (jnp.zeros((SB_T, 1), F32), jnp.zeros((SB_T, HEAD_DIM), F32)))
        o_ref[...] = acc
        lt_ref[0] = run

    return hosted_call(
        body, exchange, name="sb_fwd", grid=(SB_HEADS, s // SB_T),
        in_specs=[pl.BlockSpec((SB_T, LANE), lambda h, i: (i, h)),
                  pl.BlockSpec((s, LANE), lambda h, i: (0, SB_HEADS + h)),
                  pl.BlockSpec((s, LANE), lambda h, i: (0, v_blk0 + h))],
        out_specs=[pl.BlockSpec((SB_T, LANE), lambda h, i: (i, h)), pl.BlockSpec((1, SB_T, 1), lambda h, i: (h, i, 0))],
        out_shape=[jax.ShapeDtypeStruct((s, width), F32), jax.ShapeDtypeStruct((SB_HEADS, s, 1), F32)],
        sem=("parallel", "parallel"),
    )(qkn, qkn, proj)


def sb_bwd(qkn, proj, v_blk0, do, ltot, dproj_in):
    s = qkn.shape[0]

    def body(q_ref, k_ref, v_ref, do_ref, lt_ref, _, dq_ref, dk_ref, dv_ref):
        i = pl.program_id(1)

        @pl.when(i == 0)
        def _():
            dk_ref[...] = jnp.zeros_like(dk_ref)
            dv_ref[...] = jnp.zeros_like(dv_ref)

        q = q_ref[...]
        g = do_ref[...]
        ltot = lt_ref[0]
        diff = _sb_diff()
        upto = (diff >= 0).astype(BF16)
        before = (diff > 0).astype(BF16)

        def step(first, n, carry):
            plr, pdl, dq = carry
            tiles = [first + t for t in range(n)]
            kss = [pl.ds(pl.multiple_of(j * SB_T, SB_T), SB_T) for j in tiles]
            kbs = [k_ref[ks, :] for ks in kss]
            vbs = [v_ref[ks, :] for ks in kss]
            masks, lss, lrs = _sb_scores(q, kbs, diff, [(i - j) * SB_T for j in tiles])
            dwts = [_dg(g, vb, _NT) for vb in vbs]
            within = [_dg_exact_rhs(lr, upto, _NN) for lr in lrs]
            wtss, dls = [], []
            for t in range(n):
                wts = jnp.where(masks[t], jnp.exp(lss[t] + (ltot - (within[t] + plr))), 0.0)
                plr = plr + jnp.sum(lrs[t], axis=1, keepdims=True)
                wtss.append(wts)
                dls.append(dwts[t] * wts)
            dwithin = [_dg_exact_rhs(dl, before, _NN) for dl in dls]
            for t in range(n):
                sz = jnp.exp(lss[t])
                dz = jnp.where(masks[t], dls[t] * (1.0 - sz) - sz * (dwithin[t] + pdl), 0.0) * (HEAD_DIM ** -0.5)
                pdl = pdl + jnp.sum(dls[t], axis=1, keepdims=True)
                dk_ref[kss[t], :] += _dg(dz, q, _TN)
                dv_ref[kss[t], :] += _dg(wtss[t], g, _TN)
                dq = dq + _dg(dz, kbs[t], _NN)
            return plr, pdl, dq

        zero = jnp.zeros((SB_T, 1), F32)
        _, _, dq = _sb_loop(i + 1, step, (zero, zero, jnp.zeros((SB_T, HEAD_DIM), F32)))
        dq_ref[...] = dq

    tile = pl.BlockSpec((SB_T, LANE), lambda h, i: (i, h))
    colspec = pl.BlockSpec((s, LANE), lambda h, i: (0, h))
    return pl.pallas_call(
        body, name="sb_bwd", grid=(SB_HEADS, s // SB_T),
        in_specs=[tile, pl.BlockSpec((s, LANE), lambda h, i: (0, SB_HEADS + h)),
                  pl.BlockSpec((s, LANE), lambda h, i: (0, v_blk0 + h)), tile,
                  pl.BlockSpec((1, SB_T, 1), lambda h, i: (h, i, 0)), pl.BlockSpec(memory_space=pl.ANY)],
        out_specs=[tile, colspec, pl.BlockSpec((s, LANE), lambda h, i: (0, v_blk0 + h))],
        out_shape=[jax.ShapeDtypeStruct((s, MIX_WIDTH), F32), jax.ShapeDtypeStruct((s, MIX_WIDTH), F32),
                   jax.ShapeDtypeStruct(dproj_in.shape, F32)],
        input_output_aliases={5: 2}, compiler_params=_params("parallel", "arbitrary"),
    )(qkn, qkn, proj, do, ltot, dproj_in)


ROW_TILE = 256
HN_HEADS = 8


def head_norm_fwd(x, x_blk0, nblk, gains, out_dtype, out_width, name):
    s = x.shape[0]
    assert x_blk0 % HN_HEADS == 0 and nblk % (HN_HEADS * gains.shape[0]) == 0
    per = nblk // gains.shape[0] // HN_HEADS
    w = HN_HEADS * LANE

    def body(x_ref, g_ref, o_ref):
        gain = g_ref[0, 0:1, :]
        for j in range(HN_HEADS):
            cs = slice(j * LANE, (j + 1) * LANE)
            xv = x_ref[:, cs]
            r = lax.rsqrt(jnp.mean(xv * xv, axis=1, keepdims=True) + EPS)
            o_ref[:, cs] = (xv * r * gain).astype(out_dtype)

    return pl.pallas_call(
        body, name=name, grid=(nblk // HN_HEADS, s // ROW_TILE),
        in_specs=[pl.BlockSpec((ROW_TILE, w), lambda j, t: (t, x_blk0 // HN_HEADS + j)),
                  pl.BlockSpec((1, 8, LANE), lambda j, t: (j // per, 0, 0))],
        out_specs=pl.BlockSpec((ROW_TILE, w), lambda j, t: (t, j)),
        out_shape=jax.ShapeDtypeStruct((s, out_width), out_dtype), compiler_params=_params("parallel", "parallel"),
    )(x, gains)


def head_norm_bwd(dy, dy_blk0, x, x_blk0, nblk, gain, dst, dst_blk0, name):
    s = x.shape[0]

    def body(*refs):
        if dst is not None:
            dy_ref, x_ref, g_ref, _, dx_ref, dg_ref = refs
        else:
            dy_ref, x_ref, g_ref, dx_ref, dg_ref = refs

        @pl.when((pl.program_id(0) == 0) & (pl.program_id(1) == 0))
        def _():
            dg_ref[...] = jnp.zeros_like(dg_ref)

        gain = g_ref[0:1, :]
        dg = jnp.zeros((1, LANE), F32)
        for j in range(HN_HEADS):
            cs = slice(j * LANE, (j + 1) * LANE)
            xv = x_ref[:, cs]
            g = dy_ref[:, cs]
            r = lax.rsqrt(jnp.mean(xv * xv, axis=1, keepdims=True) + EPS)
            gy = g * gain
            dx_ref[:, cs] = r * gy - xv * (r * r * r) * jnp.mean(gy * xv, axis=1, keepdims=True)
            dg = dg + jnp.sum(g * xv * r, axis=0, keepdims=True)
        dg_ref[0:1, :] += dg

    assert dy_blk0 % HN_HEADS == 0 and x_blk0 % HN_HEADS == 0 and dst_blk0 % HN_HEADS == 0 and nblk % HN_HEADS == 0
    w = HN_HEADS * LANE
    in_specs = [pl.BlockSpec((ROW_TILE, w), lambda j, t: (t, dy_blk0 // HN_HEADS + j)),
                pl.BlockSpec((ROW_TILE, w), lambda j, t: (t, x_blk0 // HN_HEADS + j)),
                pl.BlockSpec((8, LANE), lambda j, t: (0, 0))]
    args = [dy, x, gain]
    aliases = {}
    if dst is not None:
        in_specs.append(pl.BlockSpec(memory_space=pl.ANY))
        args.append(dst)
        aliases = {3: 0}
        out0 = jax.ShapeDtypeStruct(dst.shape, F32)
    else:
        out0 = jax.ShapeDtypeStruct((s, (dst_blk0 + nblk) * LANE), F32)
    return pl.pallas_call(
        body, name=name, grid=(nblk // HN_HEADS, s // ROW_TILE), in_specs=in_specs,
        out_specs=[pl.BlockSpec((ROW_TILE, w), lambda j, t: (t, dst_blk0 // HN_HEADS + j)),
                   pl.BlockSpec((8, LANE), lambda j, t: (0, 0))],
        out_shape=[out0, jax.ShapeDtypeStruct((8, LANE), F32)],
        input_output_aliases=aliases, compiler_params=_params("arbitrary", "arbitrary"),
    )(*args)


def _xa_head(xq, kraw, v, qg, kg):
    q = xq * lax.rsqrt(jnp.mean(xq * xq, axis=1, keepdims=True) + EPS) * qg
    k = kraw * lax.rsqrt(jnp.mean(kraw * kraw, axis=1, keepdims=True) + EPS) * kg
    sc = mm_nt(q, k) * (XA_DIM ** -0.5)
    e = jnp.exp(sc - lax.stop_gradient(jnp.max(sc, axis=1, keepdims=True)))
    return mm(e / jnp.sum(e, axis=1, keepdims=True), v)


def xa_fwd(proj, xq_blk0, kv, qg, kg, cat):
    s = proj.shape[0]
    n_mem = kv.shape[0]

    def body(xq_ref, k_ref, v_ref, qg_ref, kg_ref, _, o_ref):
        o_ref[...] = _xa_head(xq_ref[...], k_ref[...], v_ref[...], qg_ref[0:1, :], kg_ref[0:1, :])

    gain = pl.BlockSpec((8, XA_DIM), lambda h, t: (0, 0))
    return pl.pallas_call(
        body, name="xa_fwd", grid=(XA_HEADS, s // ROW_TILE),
        in_specs=[pl.BlockSpec((ROW_TILE, XA_DIM), lambda h, t: (t, xq_blk0 // 2 + h)),
                  pl.BlockSpec((n_mem, XA_DIM), lambda h, t: (0, h)),
                  pl.BlockSpec((n_mem, XA_DIM), lambda h, t: (0, XA_HEADS + h)), gain, gain,
                  pl.BlockSpec(memory_space=pl.ANY)],
        out_specs=pl.BlockSpec((ROW_TILE, XA_DIM), lambda h, t: (t, MIX_WIDTH // XA_DIM + h)),
        out_shape=jax.ShapeDtypeStruct(cat.shape, F32), input_output_aliases={5: 0},
        compiler_params=_params("parallel", "parallel"),
    )(proj, kv, kv, qg, kg, cat)


def xa_bwd(proj, xq_blk0, kv, qg, kg, dcat, dproj_in):
    s = proj.shape[0]
    n_mem = kv.shape[0]

    def body(xq_ref, k_ref, v_ref, qg_ref, kg_ref, do_ref, _, dxq_ref, dk_ref, dv_ref, dqg_ref, dkg_ref):
        h = pl.program_id(0)
        t = pl.program_id(1)

        @pl.when(t == 0)
        def _():
            dk_ref[...] = jnp.zeros_like(dk_ref)
            dv_ref[...] = jnp.zeros_like(dv_ref)

        @pl.when((t == 0) & (h == 0))
        def _():
            dqg_ref[...] = jnp.zeros_like(dqg_ref)
            dkg_ref[...] = jnp.zeros_like(dkg_ref)

        _, vjp = jax.vjp(_xa_head, xq_ref[...], k_ref[...], v_ref[...], qg_ref[0:1, :], kg_ref[0:1, :])
        dxq, dk, dv, dqg, dkg = vjp(do_ref[...])
        dxq_ref[...] = dxq
        dk_ref[...] += dk
        dv_ref[...] += dv
        dqg_ref[0:1, :] += dqg
        dkg_ref[0:1, :] += dkg

    gain = pl.BlockSpec((8, XA_DIM), lambda h, t: (0, 0))
    kspec = pl.BlockSpec((n_mem, XA_DIM), lambda h, t: (0, h))
    vspec = pl.BlockSpec((n_mem, XA_DIM), lambda h, t: (0, XA_HEADS + h))
    return pl.pallas_call(
        body, name="xa_bwd", grid=(XA_HEADS, s // ROW_TILE),
        in_specs=[pl.BlockSpec((ROW_TILE, XA_DIM), lambda h, t: (t, xq_blk0 // 2 + h)), kspec, vspec, gain, gain,
                  pl.BlockSpec((ROW_TILE, XA_DIM), lambda h, t: (t, MIX_WIDTH // XA_DIM + h)),
                  pl.BlockSpec(memory_space=pl.ANY)],
        out_specs=[pl.BlockSpec((ROW_TILE, XA_DIM), lambda h, t: (t, xq_blk0 // 2 + h)), kspec, kspec, gain, gain],
        out_shape=[jax.ShapeDtypeStruct(dproj_in.shape, F32), jax.ShapeDtypeStruct((n_mem, XA_WIDTH), F32),
                   jax.ShapeDtypeStruct((n_mem, XA_WIDTH), F32), jax.ShapeDtypeStruct((8, XA_DIM), F32),
                   jax.ShapeDtypeStruct((8, XA_DIM), F32)],
        input_output_aliases={6: 0}, compiler_params=_params("arbitrary", "arbitrary"),
    )(proj, kv, kv, qg, kg, dcat, dproj_in)


GATE_ROWS = 1024


def gate_fwd(cat, proj, z_blk0):
    s = cat.shape[0]

    def body(c_ref, z_ref, y_ref):
        y_ref[...] = (c_ref[...] * _silu(z_ref[...])).astype(BF16)

    w = 2 * LANE
    rt = min(GATE_ROWS, s)
    return pl.pallas_call(
        body, name="gate_fwd", grid=(INNER // w, s // rt),
        in_specs=[pl.BlockSpec((rt, w), lambda j, t: (t, j)),
                  pl.BlockSpec((rt, w), lambda j, t: (t, z_blk0 // 2 + j))],
        out_specs=pl.BlockSpec((rt, w), lambda j, t: (t, j)),
        out_shape=jax.ShapeDtypeStruct((s, INNER), BF16), compiler_params=_params("parallel", "parallel"),
    )(cat, proj)


def gate_bwd(dy, cat, proj, z_blk0):
    s = cat.shape[0]

    def body(dy_ref, c_ref, z_ref, dc_ref, dz_ref):
        z = z_ref[...]
        g = dy_ref[...]
        dc_ref[...] = g * _silu(z)
        dz_ref[...] = g * c_ref[...] * _silu_grad(z)

    w = 2 * LANE
    rt = min(GATE_ROWS, s)
    tile = pl.BlockSpec((rt, w), lambda j, t: (t, j))
    ztile = pl.BlockSpec((rt, w), lambda j, t: (t, z_blk0 // 2 + j))
    return pl.pallas_call(
        body, name="gate_bwd", grid=(INNER // w, s // rt), in_specs=[tile, tile, ztile],
        out_specs=[tile, ztile],
        out_shape=[jax.ShapeDtypeStruct((s, INNER), F32), jax.ShapeDtypeStruct(proj.shape, F32)],
        compiler_params=_params("parallel", "parallel"),
    )(dy, cat, proj)


NORM_ROWS = 256


def rms_fwd(x, gain):
    s, d = x.shape

    def body(x_ref, g_ref, o_ref):
        xv = x_ref[...]
        r = lax.rsqrt(jnp.mean(xv * xv, axis=1, keepdims=True) + EPS)
        o_ref[...] = (xv * r * g_ref[0:1, :]).astype(BF16)

    return pl.pallas_call(
        body, name="rms_fwd", grid=(s // NORM_ROWS,),
        in_specs=[pl.BlockSpec((NORM_ROWS, d), lambda t: (t, 0)), pl.BlockSpec((8, d), lambda t: (0, 0))],
        out_specs=pl.BlockSpec((NORM_ROWS, d), lambda t: (t, 0)),
        out_shape=jax.ShapeDtypeStruct((s, d), BF16), compiler_params=_params("parallel"),
    )(x, gain)


def rms_bwd(dh, x, gain, dres):
    s, d = x.shape

    def body(*refs):
        if dres is not None:
            dh_ref, x_ref, g_ref, dr_ref, dx_ref, dg_ref = refs
        else:
            dh_ref, x_ref, g_ref, dx_ref, dg_ref = refs

        @pl.when(pl.program_id(0) == 0)
        def _():
            dg_ref[...] = jnp.zeros_like(dg_ref)

        xv = x_ref[...]
        g = dh_ref[...]
        r = lax.rsqrt(jnp.mean(xv * xv, axis=1, keepdims=True) + EPS)
        gy = g * g_ref[0:1, :]
        dx = r * gy - xv * (r * r * r) * jnp.mean(gy * xv, axis=1, keepdims=True)
        dx_ref[...] = dx + dr_ref[...] if dres is not None else dx
        dg_ref[0:1, :] += jnp.sum(g * xv * r, axis=0, keepdims=True)

    tile = pl.BlockSpec((NORM_ROWS, d), lambda t: (t, 0))
    gspec = pl.BlockSpec((8, d), lambda t: (0, 0))
    args = [dh, x, gain] + ([dres] if dres is not None else [])
    return pl.pallas_call(
        body, name="rms_bwd", grid=(s // NORM_ROWS,),
        in_specs=[tile, tile, gspec] + ([tile] if dres is not None else []),
        out_specs=[tile, gspec],
        out_shape=[jax.ShapeDtypeStruct((s, d), F32), jax.ShapeDtypeStruct((8, d), F32)],
        compiler_params=_params("arbitrary"),
    )(*args)


def loss_fwd_bwd(y, target):
    s, d = y.shape

    def body(y_ref, t_ref, l_ref, dy_ref):
        @pl.when(pl.program_id(0) == 0)
        def _():
            l_ref[...] = jnp.zeros_like(l_ref)

        err = y_ref[...] - t_ref[...]
        dy_ref[...] = err * (1.0 / d)
        part = 0.5 * jnp.sum(jnp.mean(err * err, axis=1, keepdims=True), axis=0, keepdims=True)
        r = lax.broadcasted_iota(jnp.int32, (8, LANE), 0)
        c = lax.broadcasted_iota(jnp.int32, (8, LANE), 1)
        l_ref[...] += jnp.where((r == 0) & (c == 0), part, 0.0)

    tile = pl.BlockSpec((NORM_ROWS, d), lambda t: (t, 0))
    return pl.pallas_call(
        body, name="loss", grid=(s // NORM_ROWS,), in_specs=[tile, tile],
        out_specs=[pl.BlockSpec((8, LANE), lambda t: (0, 0)), tile],
        out_shape=[jax.ShapeDtypeStruct((8, LANE), F32), jax.ShapeDtypeStruct((s, d), F32)],
        compiler_params=_params("arbitrary"),
    )(y, target)


def matmul(a, b, mode, out_dtype, tm, tn, tk, name, add=None, b_blocked=False, out_blocks=None, exchange=None):
    if b_blocked:
        nb, _, width = b.shape
        bshape = (b.shape[1], nb * width)
    else:
        bshape = b.shape
    if mode == "tn":
        (kdim, m), n = a.shape, bshape[1]
    else:
        (m, kdim), n = a.shape, (bshape[1] if mode == "nn" else bshape[0])
    tm, tn, tk = min(tm, m), min(tn, n), min(tk, kdim)
    assert m % tm == 0 and n % tn == 0 and kdim % tk == 0, (name, m, n, kdim)
    nk = kdim // tk
    dims = {"nn": _NN, "nt": _NT, "tn": _TN}[mode]

    def body(*refs):
        if add is not None:
            a_ref, b_ref, add_ref, o_ref, acc_ref = refs
        else:
            a_ref, b_ref, o_ref, acc_ref = refs
        k = pl.program_id(2)

        @pl.when(k == 0)
        def _():
            acc_ref[...] = jnp.zeros_like(acc_ref)

        acc_ref[...] += _dg(a_ref[...], b_ref[...], dims)

        @pl.when(k == nk - 1)
        def _():
            r = acc_ref[...]
            if add is not None:
                r = r + add_ref[...]
            o_ref[...] = r.astype(out_dtype)

    a_spec = pl.BlockSpec((tk, tm), lambda i, j, k: (k, i)) if mode == "tn" else pl.BlockSpec((tm, tk), lambda i, j, k: (i, k))
    if b_blocked and mode == "nn":
        per = width // tn
        assert width % tn == 0
        b_spec = pl.BlockSpec((None, tk, tn), lambda i, j, k: (j // per, k, j % per))
    elif b_blocked and mode == "nt":
        per = width // tk
        assert width % tk == 0
        b_spec = pl.BlockSpec((None, tn, tk), lambda i, j, k: (k // per, j, k % per))
    elif mode == "nt":
        b_spec = pl.BlockSpec((tn, tk), lambda i, j, k: (j, k))
    else:
        assert not b_blocked
        b_spec = pl.BlockSpec((tk, tn), lambda i, j, k: (k, j))
    add_spec = pl.BlockSpec((tm, tn), lambda i, j, k: (i, j))
    if out_blocks is not None:
        operb = (n // out_blocks) // tn
        assert (n // out_blocks) % tn == 0 and add is None
        o_spec = pl.BlockSpec((None, tm, tn), lambda i, j, k: (j // operb, i, j % operb))
        out_shape = jax.ShapeDtypeStruct((out_blocks, m, n // out_blocks), out_dtype)
    else:
        o_spec = add_spec
        out_shape = jax.ShapeDtypeStruct((m, n), out_dtype)
    res = hosted_call(
        body, exchange, name=name, grid=(m // tm, n // tn, nk),
        in_specs=[a_spec, b_spec] + ([add_spec] if add is not None else []), out_specs=[o_spec],
        out_shape=[out_shape], scratch_shapes=[pltpu.VMEM((tm, tn), F32)],
        sem=("parallel", "parallel", "arbitrary"),
    )(*([a, b] + ([add] if add is not None else [])))
    return res[0] if exchange is None else res


_HBM = pl.BlockSpec(memory_space=pltpu.HBM)


def _me():
    return lax.axis_index("x"), lax.axis_index("y"), lax.axis_index("c")


def _flat(p):
    return 4 * p[0] + 2 * p[1] + p[2]


def _flip(p, r):
    return tuple((1 - v) if (r >> (2 - a)) & 1 else v for a, v in enumerate(p))


class Exchange:
    def __init__(self, srcs, out_shapes, sems, start, finish, alias=None):
        self.srcs, self.out_shapes, self.sems = list(srcs), list(out_shapes), list(sems)
        self.start, self.finish, self.alias = start, finish, dict(alias or {})


def hosted_call(body, exchange, *, name, grid, in_specs, out_specs, out_shape, scratch_shapes=(),
                input_output_aliases=None, sem=()):
    in_specs, out_specs, out_shape = list(in_specs), list(out_specs), list(out_shape)
    scratch_shapes = list(scratch_shapes)
    aliases = input_output_aliases or {}
    if exchange is None:
        call = pl.pallas_call(body, name=name, grid=grid, in_specs=in_specs, out_specs=out_specs, out_shape=out_shape,
                              scratch_shapes=scratch_shapes, input_output_aliases=aliases, compiler_params=_params(*sem))
        return lambda *args: list(call(*args))
    ni, no, ns = len(in_specs), len(out_specs), len(scratch_shapes)
    xi, xo = len(exchange.srcs), len(exchange.out_shapes)

    def wrapped(*refs):
        ins, refs = refs[:ni], refs[ni:]
        xin, refs = refs[:xi], refs[xi:]
        outs, refs = refs[:no], refs[no:]
        xout, refs = refs[:xo], refs[xo:]
        scr, xsem = refs[:ns], refs[ns:]
        first = functools.reduce(lambda p, q: p & q, [pl.program_id(d) == 0 for d in range(len(grid))])
        last = functools.reduce(lambda p, q: p & q, [pl.program_id(d) == grid[d] - 1 for d in range(len(grid))])

        @pl.when(first)
        def _():
            exchange.start(xin, xout, xsem)

        body(*ins, *outs, *scr)

        @pl.when(last)
        def _():
            exchange.finish(xin, xout, xsem)

    call = pl.pallas_call(
        wrapped, name=name, grid=grid, in_specs=in_specs + [_HBM] * xi, out_specs=out_specs + [_HBM] * xo,
        out_shape=out_shape + exchange.out_shapes, scratch_shapes=scratch_shapes + exchange.sems,
        input_output_aliases={**aliases, **{ni + i: no + j for i, j in exchange.alias.items()}},
        compiler_params=_params(*(("arbitrary",) * len(grid))))
    return lambda *args: list(call(*args, *exchange.srcs))


def run_exchange(exchange, name):
    xi, xo = len(exchange.srcs), len(exchange.out_shapes)

    def body(*refs):
        exchange.start(refs[:xi], refs[xi:xi + xo], refs[xi + xo:])
        exchange.finish(refs[:xi], refs[xi:xi + xo], refs[xi + xo:])

    return list(pl.pallas_call(body, name=name, in_specs=[_HBM] * xi, out_specs=[_HBM] * xo,
                               out_shape=exchange.out_shapes, scratch_shapes=exchange.sems,
                               input_output_aliases=exchange.alias)(*exchange.srcs))


def gather_exchange(shards, rows=None, into=None):
    n = len(shards)

    def parts(srcs, outs, sems):
        send_sems, recv_sems, local_sems = sems
        me = _me()
        x, y, c = me
        chips = [(1 - x, y), (x, 1 - y), (1 - x, 1 - y)]

        def place(a, block):
            dst = outs[a].at[_flat(block)]
            return dst if rows is None else dst.at[pl.ds(rows[0], shards[a].shape[0])]

        def copy(a, k, block, to, src=None):
            dst = place(a, block)
            return pltpu.make_async_remote_copy(src_ref=dst if src is None else src, dst_ref=dst,
                                                send_sem=send_sems.at[a, k], recv_sem=recv_sems.at[a, k],
                                                device_id=to, device_id_type=MESH)

        mine = [pltpu.make_async_copy(srcs[a], place(a, me), local_sems.at[a]) for a in range(n)]
        own = []
        for a in range(n):
            own.append(copy(a, 0, me, (x, y, 1 - c), src=srcs[a]))
            own += [copy(a, 1 + j, me, (*chip, c), src=srcs[a]) for j, chip in enumerate(chips)]
        return me, chips, copy, mine, own

    def start(srcs, outs, sems):
        _, _, _, mine, own = parts(srcs, outs, sems)
        for cp in mine + own:
            cp.start()

    def finish(srcs, outs, sems):
        me, chips, copy, mine, own = parts(srcs, outs, sems)
        x, y, c = me
        passed = []
        for j, chip in enumerate(chips):
            for a in range(n):
                copy(a, 1 + j, (*chip, c), me).wait_recv()
                fwd = copy(a, 4 + j, (*chip, c), (x, y, 1 - c))
                fwd.start()
                passed.append(fwd)
        for a in range(n):
            copy(a, 0, (x, y, 1 - c), me).wait_recv()
            for j, chip in enumerate(chips):
                copy(a, 4 + j, (*chip, 1 - c), me).wait_recv()
        for cp in own + passed:
            cp.wait_send()
        for cp in mine:
            cp.wait()

    dma = pltpu.SemaphoreType.DMA
    full = lambda s: s.shape if rows is None else (rows[1],) + s.shape[1:]
    return Exchange(list(shards) + list(into or []), [jax.ShapeDtypeStruct((N_DEV,) + full(s), s.dtype) for s in shards],
                    [dma((n, 7)), dma((n, 7)), dma((n,))], start, finish,
                    alias={n + a: a for a in range(n)} if into else None)


def pair_exchange(srcs):
    n = len(srcs)

    def copies(srcs_, outs, sems):
        send_sems, recv_sems = sems
        x, y, c = _me()
        return [pltpu.make_async_remote_copy(src_ref=srcs_[a].at[:, 1 - c], dst_ref=outs[a], send_sem=send_sems.at[a],
                                             recv_sem=recv_sems.at[a], device_id=(x, y, 1 - c), device_id_type=MESH)
                for a in range(n)]

    def start(srcs_, outs, sems):
        for cp in copies(srcs_, outs, sems):
            cp.start()

    def finish(srcs_, outs, sems):
        for cp in copies(srcs_, outs, sems):
            cp.wait()

    dma = pltpu.SemaphoreType.DMA
    return Exchange(srcs, [jax.ShapeDtypeStruct((4,) + s.shape[2:], s.dtype) for s in srcs], [dma((n,)), dma((n,))],
                    start, finish)


def pair_sum(src, half, name):
    _, _, rows, cols = src.shape
    tr = min(rows, 256)

    def body(x_ref, h_ref, o_ref):
        c = lax.axis_index("c")
        o_ref[0] = (x_ref[0, c].astype(F32) + h_ref[0].astype(F32)).astype(BF16)

    return pl.pallas_call(
        body, name=name, grid=(4, rows // tr),
        in_specs=[pl.BlockSpec((1, 2, tr, cols), lambda ch, t: (ch, 0, t, 0)),
                  pl.BlockSpec((1, tr, cols), lambda ch, t: (ch, t, 0))],
        out_specs=pl.BlockSpec((1, tr, cols), lambda ch, t: (ch, t, 0)),
        out_shape=jax.ShapeDtypeStruct(half.shape, BF16), compiler_params=_params("parallel", "parallel"),
    )(src, half)


def chip_exchange(parts, slots, recv_shapes):
    n = len(parts)

    def plan(srcs, outs, sems):
        send_sems, recv_sems, local_sems = sems
        x, y, c = _me()
        chip = 2 * x + y
        mine = [pltpu.make_async_copy(srcs[a].at[chip], outs[slots[a][0]].at[chip, slots[a][1]], local_sems.at[a])
                for a in range(n)]
        sends, arrivals = [], []
        for r in (1, 2, 3):
            px = (1 - x) if r & 2 else x
            py = (1 - y) if r & 1 else y
            for a in range(n):
                ri, layer = slots[a]
                sends.append(pltpu.make_async_remote_copy(
                    src_ref=srcs[a].at[2 * px + py], dst_ref=outs[ri].at[chip, layer], send_sem=send_sems.at[a, r - 1],
                    recv_sem=recv_sems.at[a, r - 1], device_id=(px, py, c), device_id_type=MESH))
                land = outs[ri].at[2 * px + py, layer]
                arrivals.append(pltpu.make_async_remote_copy(
                    src_ref=land, dst_ref=land, send_sem=send_sems.at[a, r - 1], recv_sem=recv_sems.at[a, r - 1],
                    device_id=(px, py, c), device_id_type=MESH))
        return mine, sends, arrivals

    def start(srcs, outs, sems):
        mine, sends, _ = plan(srcs, outs, sems)
        for cp in mine + sends:
            cp.start()

    def finish(srcs, outs, sems):
        mine, sends, arrivals = plan(srcs, outs, sems)
        for cp in arrivals:
            cp.wait_recv()
        for cp in sends:
            cp.wait_send()
        for cp in mine:
            cp.wait()

    dma = pltpu.SemaphoreType.DMA
    return Exchange(parts, [jax.ShapeDtypeStruct(s, BF16) for s in recv_shapes],
                    [dma((n, 3)), dma((n, 3)), dma((n,))], start, finish)


SMALL_ROWS = 24


def all_reduce_small(pack):
    def body(p_ref, o_ref, buf, send_sems, recv_sems):
        me = _me()
        buf[_flat(me)] = p_ref[...]
        sent = []
        for r in range(1, N_DEV):
            peer = _flip(me, r)
            cp = pltpu.make_async_remote_copy(src_ref=p_ref, dst_ref=buf.at[_flat(me)], send_sem=send_sems.at[r - 1],
                                              recv_sem=recv_sems.at[r - 1], device_id=peer, device_id_type=MESH)
            cp.start()
            sent.append(cp)
        for r in range(1, N_DEV):
            peer = _flip(me, r)
            land = buf.at[_flat(peer)]
            pltpu.make_async_remote_copy(src_ref=land, dst_ref=land, send_sem=send_sems.at[r - 1],
                                         recv_sem=recv_sems.at[r - 1], device_id=peer, device_id_type=MESH).wait_recv()
        for cp in sent:
            cp.wait_send()
        acc = buf[0]
        for d in range(1, N_DEV):
            acc = acc + buf[d]
        o_ref[...] = acc

    vm = pl.BlockSpec(memory_space=pltpu.VMEM)
    return pl.pallas_call(
        body, name="all_reduce_small", in_specs=[vm], out_specs=vm,
        out_shape=jax.ShapeDtypeStruct(pack.shape, F32),
        scratch_shapes=[pltpu.VMEM((N_DEV,) + pack.shape, F32), pltpu.SemaphoreType.DMA((7,)),
                        pltpu.SemaphoreType.DMA((7,))],
    )(pack)


def _adamw(w, g, m, v):
    m = ADAM_B1 * m + (1.0 - ADAM_B1) * g
    v = ADAM_B2 * v + (1.0 - ADAM_B2) * (g * g)
    m_hat = m / (1.0 - ADAM_B1 ** ADAM_STEP)
    v_hat = v / (1.0 - ADAM_B2 ** ADAM_STEP)
    delta = -ADAM_LR * (m_hat / (jnp.sqrt(v_hat) + ADAM_EPS) + ADAM_WD * w)
    return delta, m, v


ADAM_ROWS = 128


def reduce_adamw(recv, w, m, v, name):
    nl, rows, cols = w.shape
    nslot, cp = recv.shape[0], recv.shape[3]

    def body(r_ref, w_ref, m_ref, v_ref, g_ref, d_ref, mo_ref, vo_ref):
        g = r_ref[0, 0].astype(F32)
        for slot in range(1, nslot):
            g = g + r_ref[slot, 0].astype(F32)
        if cp != cols:
            g = g[:, :cols]
        delta, m_new, v_new = _adamw(w_ref[0], g, m_ref[0], v_ref[0])
        g_ref[0] = g
        d_ref[0] = delta
        mo_ref[0] = m_new
        vo_ref[0] = v_new

    tile = pl.BlockSpec((1, ADAM_ROWS, cols), lambda l, t: (l, t, 0))
    out = jax.ShapeDtypeStruct(w.shape, F32)
    return pl.pallas_call(
        body, name=name, grid=(nl, rows // ADAM_ROWS),
        in_specs=[pl.BlockSpec((nslot, 1, ADAM_ROWS, cp), lambda l, t: (0, l, t, 0)), tile, tile, tile],
        out_specs=[tile, tile, tile, tile], out_shape=[out, out, out, out],
        compiler_params=_params("parallel", "parallel"),
    )(recv, w, m, v)


def reduce_adamw_t(recv, w_t, m_t, v_t, name):
    cols, rows = w_t.shape
    nslot, cp = recv.shape[0], recv.shape[3]
    tr = 256

    def body(r_ref, w_ref, m_ref, v_ref, g_ref, d_ref, mo_ref, vo_ref):
        g = r_ref[0, 0].astype(F32)
        for slot in range(1, nslot):
            g = g + r_ref[slot, 0].astype(F32)
        g_ref[...] = g.T[:cols, :]
        delta, m_new, v_new = _adamw(w_ref[...], g_ref[...], m_ref[...], v_ref[...])
        d_ref[...] = delta
        mo_ref[...] = m_new
        vo_ref[...] = v_new

    tile = pl.BlockSpec((cols, tr), lambda t: (0, t))
    out = jax.ShapeDtypeStruct((cols, rows), F32)
    return pl.pallas_call(
        body, name=name, grid=(rows // tr,),
        in_specs=[pl.BlockSpec((nslot, 1, tr, cp), lambda t: (0, 0, t, 0)), tile, tile, tile],
        out_specs=[tile, tile, tile, tile], out_shape=[out, out, out, out], compiler_params=_params("parallel"),
    )(recv, w_t, m_t, v_t)


def adamw_small(g, w, m, v):
    def body(g_ref, w_ref, m_ref, v_ref, d_ref, mo_ref, vo_ref):
        d_ref[...], mo_ref[...], vo_ref[...] = _adamw(w_ref[...], g_ref[...], m_ref[...], v_ref[...])

    out = jax.ShapeDtypeStruct(g.shape, F32)
    return pl.pallas_call(body, name="adamw_small", out_shape=[out, out, out])(g, w, m, v)


def _row8(v):
    return jnp.pad(v.reshape(1, -1).astype(F32), ((0, 7), (0, 0)))


def _row8_lanes(v, width=LANE):
    return jnp.pad(v.reshape(1, -1).astype(F32), ((0, 7), (0, width - v.size)))


def _pack_rows(parts):
    rows = []
    for p in parts:
        p = p.reshape(-1).astype(F32)
        nrow = -(-p.size // D_MODEL)
        rows.append(jnp.pad(p, (0, nrow * D_MODEL - p.size)).reshape(nrow, D_MODEL))
    out = jnp.concatenate(rows, axis=0)
    return jnp.pad(out, ((0, SMALL_ROWS - out.shape[0]), (0, 0)))


def _unpack_rows(pack, shapes):
    out, r = [], 0
    for shp in shapes:
        size = 1
        for d in shp:
            size *= d
        nrow = -(-size // D_MODEL)
        out.append(pack[r:r + nrow].reshape(-1)[:size].reshape(shp))
        r += nrow
    return out


def _dn_weight_layout(gathered):
    split = DN_QKV + 2 * DN_V_HEADS
    pieces = []
    for d in range(N_DEV):
        lo, hi = d * DN_SHARD, (d + 1) * DN_SHARD
        if lo < split < hi:
            pieces += [gathered[d, :, :split - lo], jnp.zeros((D_MODEL, DN_COLS - DN_PROJ), gathered.dtype),
                       gathered[d, :, split - lo:DN_SHARD]]
        else:
            pieces.append(gathered[d, :, :DN_SHARD])
    return jnp.concatenate(pieces, axis=1)


def _dn_grad_blocks(dw):
    split = DN_QKV + 2 * DN_V_HEADS
    gap = DN_COLS - DN_PROJ
    local = lambda c: c if c <= split else c + gap
    zeros = jnp.zeros((D_MODEL, DN_SHARD_PAD - DN_SHARD), dw.dtype)
    blocks = []
    for d in range(N_DEV):
        lo, hi = d * DN_SHARD, (d + 1) * DN_SHARD
        if lo < split < hi:
            parts = [dw[:, lo:split], dw[:, split + gap:hi + gap]]
        else:
            parts = [dw[:, local(lo):local(lo) + DN_SHARD]]
        blocks.append(jnp.concatenate(parts + [zeros], axis=1))
    return jnp.stack(blocks)


def kernel(x, mem, norm_g, mem_norm_g, mem_w_kv, xa_q_norm_g, xa_k_norm_g, w_out, dn_w_in, dn_conv_w, dn_a_log, dn_dt_bias, dn_out_norm_g, sb_w_in, sb_q_norm_g, sb_k_norm_g, loss_target, m_norm_g, m_mem_norm_g, m_mem_w_kv, m_xa_q_norm_g, m_xa_k_norm_g, m_w_out, m_dn_w_in, m_dn_conv_w, m_dn_a_log, m_dn_dt_bias, m_dn_out_norm_g, m_sb_w_in, m_sb_q_norm_g, m_sb_k_norm_g, v_norm_g, v_mem_norm_g, v_mem_w_kv, v_xa_q_norm_g, v_xa_k_norm_g, v_w_out, v_dn_w_in, v_dn_conv_w, v_dn_a_log, v_dn_dt_bias, v_dn_out_norm_g, v_sb_w_in, v_sb_q_norm_g, v_sb_k_norm_g):
    x0, memv, target = x[0], mem[0], loss_target[0]
    my_dev = 4 * lax.axis_index("x") + 2 * lax.axis_index("y") + lax.axis_index("c")

    dn_shard = jnp.pad(dn_w_in[0].astype(BF16), ((0, 0), (0, DN_SHARD_PAD - DN_SHARD)))
    w_out_b = [w_out[i].astype(BF16) for i in range(2)]
    w_kv_b = [mem_w_kv[i].astype(BF16) for i in range(2)]
    conv_shard = jnp.pad(dn_conv_w[0], ((0, 4), (0, 0)))
    g_dn, g_conv = run_exchange(gather_exchange([dn_shard, conv_shard]), "gather_first")
    w_dn = _dn_weight_layout(g_dn)
    conv_w = jnp.transpose(g_conv, (1, 0, 2)).reshape(8, DN_QKV)

    ng = [_row8(norm_g[0]), _row8(norm_g[1])]
    mem_g = _row8(mem_norm_g)
    xqg = [_row8(xa_q_norm_g[0]), _row8(xa_q_norm_g[1])]
    xkg = [_row8(xa_k_norm_g[0]), _row8(xa_k_norm_g[1])]
    alog, dtb = _row8_lanes(dn_a_log[0]), _row8_lanes(dn_dt_bias[0])
    out_g, sbq_g, sbk_g = _row8(dn_out_norm_g[0]), _row8(sb_q_norm_g[0]), _row8(sb_k_norm_g[0])

    mem_n = rms_fwd(memv, mem_g)
    h0 = rms_fwd(x0, ng[0])
    proj0, g_wo0, g_kv0 = matmul(h0, w_dn, "nn", F32, 1024, 1152, 2048, "proj_dn",
                                 exchange=gather_exchange([w_out_b[0], w_kv_b[0]]))
    act = dn_conv_fwd(proj0, conv_w)
    sb_shard, half = sb_w_in[0].astype(BF16), D_MODEL // 2
    u0, w_, qd, kd, qk, cd, w_sb = dn_prep_fwd(
        act, proj0, alog, dtb, exchange=gather_exchange([sb_shard[:half]], rows=(0, D_MODEL)))
    o_raw, states, w_sb = dn_scan_fwd(
        u0, w_, qd, kd, qk, cd, MIX_WIDTH,
        exchange=gather_exchange([sb_shard[half:]], rows=(half, D_MODEL), into=[w_sb]))
    w_o = [g_wo0.reshape(INNER, D_MODEL), None]
    w_kv = [g_kv0.reshape(D_MODEL, 2 * XA_WIDTH), None]
    kv = [matmul(mem_n, w_kv[0], "nn", F32, 256, 1024, 2048, "kv0"), None]
    cat0 = head_norm_fwd(o_raw, 0, DN_V_HEADS, out_g[None], F32, INNER, "dn_out_norm")
    cat0 = xa_fwd(proj0, DN_XQ_BLK, kv[0], xqg[0], xkg[0], cat0)
    y0 = gate_fwd(cat0, proj0, DN_Z_BLK)
    x1 = matmul(y0, w_o[0], "nn", F32, 1024, 1024, 2048, "out_proj0", add=x0)

    h1 = rms_fwd(x1, ng[1])
    proj1 = matmul(h1, w_sb, "nn", F32, 1024, 896, 2048, "proj_sb", b_blocked=True)
    qkn = head_norm_fwd(proj1, 0, 2 * SB_HEADS, jnp.stack([sbq_g, sbk_g]), BF16, 2 * MIX_WIDTH, "sb_qk_norm")
    cat1, ltot, g_wo1, g_kv1 = sb_fwd(qkn, proj1, 2 * SB_HEADS, INNER,
                                      exchange=gather_exchange([w_out_b[1], w_kv_b[1]]))
    w_o[1] = g_wo1.reshape(INNER, D_MODEL)
    w_kv[1] = g_kv1.reshape(D_MODEL, 2 * XA_WIDTH)
    kv[1] = matmul(mem_n, w_kv[1], "nn", F32, 256, 1024, 2048, "kv1")
    cat1 = xa_fwd(proj1, SB_XQ_BLK, kv[1], xqg[1], xkg[1], cat1)
    y1 = gate_fwd(cat1, proj1, SB_Z_BLK)
    x2 = matmul(y1, w_o[1], "nn", F32, 1024, 1024, 2048, "out_proj1", add=x1)
    loss_part, dx2 = loss_fwd_bwd(x2, target)

    dy1 = matmul(dx2, w_o[1], "nt", F32, 1024, 1024, 2048, "d_y1")
    dw_o1 = matmul(y1, dx2, "tn", BF16, 1024, 1024, 1024, "d_w_out1")
    dcat1, dproj1 = gate_bwd(dy1, cat1, proj1, SB_Z_BLK)
    dproj1, dxk1, dxv1, dxqg1, dxkg1 = xa_bwd(proj1, SB_XQ_BLK, kv[1], xqg[1], xkg[1], dcat1, dproj1)
    dqn, dkn, dproj1 = sb_bwd(qkn, proj1, 2 * SB_HEADS, dcat1, ltot, dproj1)
    dproj1, d_sbq = head_norm_bwd(dqn, 0, proj1, 0, SB_HEADS, sbq_g, dproj1, 0, "sb_q_norm_bwd")
    dproj1, d_sbk = head_norm_bwd(dkn, 0, proj1, SB_HEADS, SB_HEADS, sbk_g, dproj1, SB_HEADS, "sb_k_norm_bwd")
    dw_sb = matmul(h1, dproj1, "tn", BF16, 1024, 896, 1024, "d_w_sb", out_blocks=N_DEV)
    dh1 = matmul(dproj1, w_sb, "nt", F32, 1024, 1024, 896, "d_h1", b_blocked=True)
    dx1, d_ng1 = rms_bwd(dh1, x1, ng[1], dx2)

    dy0 = matmul(dx1, w_o[0], "nt", F32, 1024, 1024, 2048, "d_y0")
    dw_o0 = matmul(y0, dx1, "tn", BF16, 1024, 1024, 1024, "d_w_out0")
    dcat0, dproj0 = gate_bwd(dy0, cat0, proj0, DN_Z_BLK)
    dproj0, dxk0, dxv0, dxqg0, dxkg0 = xa_bwd(proj0, DN_XQ_BLK, kv[0], xqg[0], xkg[0], dcat0, dproj0)

    dkv = [jnp.concatenate([dxk0, dxv0], axis=1), jnp.concatenate([dxk1, dxv1], axis=1)]
    dw_kv = [matmul(mem_n, dkv[i], "tn", BF16, 1024, 1024, 256, f"d_w_kv{i}") for i in range(2)]
    dmem_n = matmul(dkv[0], w_kv[0], "nt", F32, 256, 1024, 2048, "d_mem_n0")
    dmem_n = matmul(dkv[1], w_kv[1], "nt", F32, 256, 1024, 2048, "d_mem_n1", add=dmem_n)
    _, d_memg = rms_bwd(dmem_n, memv, mem_g, None)

    by_owner = lambda g, rows: g.reshape(4, 2, rows, g.size // (N_DEV * rows))
    grads = [dw_sb.reshape(4, 2, D_MODEL, SB_PROJ // N_DEV), by_owner(dw_o0, INNER // N_DEV),
             by_owner(dw_o1, INNER // N_DEV), by_owner(dw_kv[0], D_MODEL // N_DEV), by_owner(dw_kv[1], D_MODEL // N_DEV)]
    do_raw, d_outg = head_norm_bwd(dcat0, 0, o_raw, 0, DN_V_HEADS, out_g, None, 0, "dn_out_norm_bwd")
    du0, dw_, dqd, dkd, dqk, dcd, *halves = dn_scan_bwd(do_raw, u0, w_, qd, kd, qk, cd, states,
                                                         exchange=pair_exchange(grads))
    sums = [pair_sum(g, h, f"pair_sum{i}") for i, (g, h) in enumerate(zip(grads, halves))]
    to_chips = chip_exchange(sums, [(0, 0), (1, 0), (1, 1), (2, 0), (2, 1)],
                             [(4, 1, D_MODEL, SB_PROJ // N_DEV), (4, 2, INNER // N_DEV, D_MODEL),
                              (4, 2, D_MODEL // N_DEV, 2 * XA_WIDTH)])
    dq_a, dk_a, dv_a, dab, d_alog, d_dtb, recv_sb, recv_wo, recv_kv = dn_prep_bwd(
        act, proj0, alog, dtb, du0, dw_, dqd, dkd, dqk, dcd, exchange=to_chips)
    dproj0, dcw_q = dn_conv_bwd(dq_a, proj0, conv_w, dproj0, None, 0, "dn_conv_bwd_q")
    dproj0, dcw_k = dn_conv_bwd(dk_a, proj0, conv_w, dproj0, None, DN_QK_HEADS, "dn_conv_bwd_k")
    dproj0, dcw_v = dn_conv_bwd(dv_a, proj0, conv_w, dproj0, dab, 2 * DN_QK_HEADS, "dn_conv_bwd_v")
    dw_dn = matmul(h0, dproj0, "tn", BF16, 1024, 1152, 1024, "d_w_dn")
    dn_grad = _dn_grad_blocks(dw_dn).reshape(4, 2, D_MODEL, DN_SHARD_PAD)
    dh0, dn_half = matmul(dproj0, w_dn, "nt", F32, 1024, 1024, 1152, "d_h0", exchange=pair_exchange([dn_grad]))
    grad_x, d_ng0 = rms_bwd(dh0, x0, ng[0], dx1)
    dn_sum = pair_sum(dn_grad, dn_half, "pair_sum_dn")
    recv_dn, = run_exchange(chip_exchange([dn_sum], [(0, 0)], [(4, 1, D_MODEL, DN_SHARD_PAD)]), "reduce_dn")

    big = {
        "dn_w_in": tuple(jnp.transpose(a)[None] for a in reduce_adamw_t(
            recv_dn, jnp.transpose(dn_w_in[0]), jnp.transpose(m_dn_w_in[0]), jnp.transpose(v_dn_w_in[0]),
            "adamw_dn_w_in")),
        "sb_w_in": reduce_adamw(recv_sb, sb_w_in, m_sb_w_in, v_sb_w_in, "adamw_sb_w_in"),
        "w_out": reduce_adamw(recv_wo, w_out, m_w_out, v_w_out, "adamw_w_out"),
        "mem_w_kv": reduce_adamw(recv_kv, mem_w_kv, m_mem_w_kv, v_mem_w_kv, "adamw_mem_w_kv"),
    }

    dconv = jnp.concatenate([dcw_q, dcw_k, dcw_v], axis=1)[:4]
    small_shapes = [(2, D_MODEL), (D_MODEL,), (2, XA_DIM), (2, XA_DIM), (4, DN_QKV), (1, DN_V_HEADS),
                    (1, DN_V_HEADS), (1, HEAD_DIM), (1, HEAD_DIM), (1, HEAD_DIM), (1,)]
    pack = _pack_rows([jnp.stack([d_ng0[0], d_ng1[0]]), d_memg[0], jnp.stack([dxqg0[0], dxqg1[0]]),
                       jnp.stack([dxkg0[0], dxkg1[0]]), dconv, d_alog[0, :DN_V_HEADS], d_dtb[0, :DN_V_HEADS],
                       d_outg[0], d_sbq[0], d_sbk[0], loss_part[0, :1]])
    total = all_reduce_small(pack)
    (g_norm, g_memn, g_xq, g_xk, g_conv_full, g_alog, g_dtb, g_outn, g_sbq, g_sbk, loss1) = _unpack_rows(total, small_shapes)
    conv_cols = DN_QKV // N_DEV
    g_conv = lax.dynamic_slice(g_conv_full, (0, my_dev * conv_cols), (4, conv_cols))[None]
    names = ["norm_g", "mem_norm_g", "xa_q_norm_g", "xa_k_norm_g", "dn_conv_w", "dn_a_log", "dn_dt_bias",
             "dn_out_norm_g", "sb_q_norm_g", "sb_k_norm_g"]
    grads = [g_norm, g_memn, g_xq, g_xk, g_conv, g_alog, g_dtb, g_outn, g_sbq, g_sbk]
    ws = [norm_g, mem_norm_g, xa_q_norm_g, xa_k_norm_g, dn_conv_w, dn_a_log, dn_dt_bias, dn_out_norm_g, sb_q_norm_g,
          sb_k_norm_g]
    ms = [m_norm_g, m_mem_norm_g, m_xa_q_norm_g, m_xa_k_norm_g, m_dn_conv_w, m_dn_a_log, m_dn_dt_bias,
          m_dn_out_norm_g, m_sb_q_norm_g, m_sb_k_norm_g]
    vs = [v_norm_g, v_mem_norm_g, v_xa_q_norm_g, v_xa_k_norm_g, v_dn_conv_w, v_dn_a_log, v_dn_dt_bias,
          v_dn_out_norm_g, v_sb_q_norm_g, v_sb_k_norm_g]
    shapes = [w.shape for w in ws]
    d_p, m_p, v_p = adamw_small(_pack_rows(grads), _pack_rows(ws), _pack_rows(ms), _pack_rows(vs))
    small = dict(zip(names, zip(grads, _unpack_rows(d_p, shapes), _unpack_rows(m_p, shapes), _unpack_rows(v_p, shapes))))

    order = ["norm_g", "mem_norm_g", "mem_w_kv", "xa_q_norm_g", "xa_k_norm_g", "w_out", "dn_w_in", "dn_conv_w",
             "dn_a_log", "dn_dt_bias", "dn_out_norm_g", "sb_w_in", "sb_q_norm_g", "sb_k_norm_g"]
    res = {**big, **small}
    outs = [loss1.reshape(()), grad_x[None]]
    for k in range(4):
        outs += [res[n][k] for n in order]
    return tuple(outs)
```

```python
import functools

import jax
import jax.numpy as jnp
from jax import lax
from jax.experimental import pallas as pl
from jax.experimental.pallas import tpu as pltpu

F32 = jnp.float32
BF16 = jnp.bfloat16

D_MODEL = 2048
SEQ = 2048
N_MEM = 256
INNER = 4096
XA_HEADS = 4
XA_WIDTH = 1024
XA_DIM = 256
MIX_WIDTH = 3072
HEAD_DIM = 128
DN_V_HEADS = 24
DN_QK_HEADS = 12
DN_QK_WIDTH = 1536
DN_CHUNK = 64
DN_QKV = 2 * DN_QK_WIDTH + MIX_WIDTH
DN_PROJ = 11312
SB_HEADS = 24
SB_PROJ = 14336
EPS = 1e-6
N_DEV = 8
DN_SHARD = DN_PROJ // N_DEV
DN_SHARD_PAD = 1536
LANE = 128
DN_COLS = 90 * LANE
DN_AB_BLK, DN_PAD_BLK, DN_XQ_BLK, DN_Z_BLK = 48, 49, 50, 58
SB_XQ_BLK, SB_Z_BLK = 72, 80

ADAM_LR, ADAM_B1, ADAM_B2, ADAM_EPS, ADAM_WD, ADAM_STEP = 0.001, 0.9, 0.999, 1e-08, 0.01, 10

VMEM_LIMIT = 56 * 1024 * 1024
MESH = pl.DeviceIdType.MESH

_NN, _NT, _TN = "nn", "nt", "tn"


def _dims(mode, rank):
    lhs, rhs = {"nn": (1, 0), "nt": (1, 1), "tn": (0, 0)}[mode]
    if rank == 2:
        return (((lhs,), (rhs,)), ((), ()))
    return (((lhs + 1,), (rhs + 1,)), ((0,), (0,)))


def _params(*sem):
    return pltpu.CompilerParams(dimension_semantics=sem if sem else None, vmem_limit_bytes=VMEM_LIMIT)


def _dot(a, b, mode):
    return lax.dot_general(a, b, _dims(mode, a.ndim), preferred_element_type=F32)


def _dg(a, b, dims):
    return _dot(a.astype(BF16), b.astype(BF16), dims)


@jax.custom_vjp
def mm(a, b):
    return _dg(a, b, _NN)


@jax.custom_vjp
def mm_nt(a, b):
    return _dg(a, b, _NT)


@jax.custom_vjp
def mm_tn(a, b):
    return _dg(a, b, _TN)


mm.defvjp(lambda a, b: (_dg(a, b, _NN), (a, b)), lambda r, g: (mm_nt(g, r[1]), mm_tn(r[0], g)))
mm_nt.defvjp(lambda a, b: (_dg(a, b, _NT), (a, b)), lambda r, g: (mm(g, r[1]), mm_tn(g, r[0])))
mm_tn.defvjp(lambda a, b: (_dg(a, b, _TN), (a, b)), lambda r, g: (mm_nt(r[1], g), mm(r[0], g)))


def _split3(x):
    hi = x.astype(BF16)
    r1 = x - hi.astype(F32)
    mid = r1.astype(BF16)
    lo = (r1 - mid.astype(F32)).astype(BF16)
    return hi, mid, lo


def _dg3(a, b, dims):
    ah, am, _ = _split3(a)
    bh, bm, _ = _split3(b)
    return _dot(ah, bh, dims) + (_dot(ah, bm, dims) + _dot(am, bh, dims))


def _dg_exact_rhs(a, b01, dims):
    ah, am, _ = _split3(a)
    b = b01.astype(BF16)
    return _dot(ah, b, dims) + _dot(am, b, dims)


def _dg_exact_lhs(a01, b, dims):
    bh, bm, bl = _split3(b)
    a = a01.astype(BF16)
    return _dot(a, bh, dims) + (_dot(a, bm, dims) + _dot(a, bl, dims))


@jax.custom_vjp
def mm3(a, b):
    return _dg3(a, b, _NN)


@jax.custom_vjp
def mm3_nt(a, b):
    return _dg3(a, b, _NT)


@jax.custom_vjp
def mm3_tn(a, b):
    return _dg3(a, b, _TN)


mm3.defvjp(lambda a, b: (_dg3(a, b, _NN), (a, b)), lambda r, g: (mm3_nt(g, r[1]), mm3_tn(r[0], g)))
mm3_nt.defvjp(lambda a, b: (_dg3(a, b, _NT), (a, b)), lambda r, g: (mm3(g, r[1]), mm3_tn(g, r[0])))
mm3_tn.defvjp(lambda a, b: (_dg3(a, b, _TN), (a, b)), lambda r, g: (mm3_nt(r[1], g), mm3(r[0], g)))


def _softplus(x):
    return jnp.maximum(x, 0.0) + jnp.log(1.0 + jnp.exp(-jnp.abs(x)))


def _log_sigmoid(x):
    return jnp.minimum(x, 0.0) - jnp.log(1.0 + jnp.exp(-jnp.abs(x)))


def _sigmoid(x):
    return 1.0 / (1.0 + jnp.exp(-x))


def _silu(x):
    return x * _sigmoid(x)


def _silu_grad(x):
    s = _sigmoid(x)
    return s * (1.0 + x * (1.0 - s))


@jax.custom_vjp
def mm01(a01, b):
    return _dg_exact_lhs(a01, b, _NN)


mm01.defvjp(lambda a, b: (_dg_exact_lhs(a, b, _NN), a),
            lambda a, g: (jnp.zeros_like(a), _dg_exact_lhs(a, g, _TN)))


def _lane_pick(x, idx):
    lane = lax.broadcasted_iota(jnp.int32, x.shape, x.ndim - 1)
    return jnp.sum(jnp.where(lane == idx, x, 0.0), axis=-1, keepdims=True)


def _dn_chunk(qt, kt, v, ab, alog, dtb, h):
    B, C = qt.shape[0], DN_CHUNK
    g = -jnp.exp(_lane_pick(alog, h)) * _softplus(_lane_pick(ab, h) + _lane_pick(dtb, h))
    beta = _sigmoid(_lane_pick(ab, h + DN_V_HEADS))
    q = qt * lax.rsqrt(jnp.sum(qt * qt, axis=-1, keepdims=True) + EPS) * (HEAD_DIM ** -0.5)
    k = kt * lax.rsqrt(jnp.sum(kt * kt, axis=-1, keepdims=True) + EPS)
    row = lax.broadcasted_iota(jnp.int32, (B, C, C), 1)
    col = lax.broadcasted_iota(jnp.int32, (B, C, C), 2)
    lower = (row >= col).astype(F32)
    ones = jnp.ones((B, C, C), F32)
    g_wide = jnp.broadcast_to(g, (B, C, LANE))
    g_sq = jnp.broadcast_to(g, (B, C, C))
    gc = mm01(lower, g_wide)
    gc_i = gc[:, :, :C]
    gc_j = mm01(ones, jnp.where(row <= col, g_sq, 0.0))
    g_last = jnp.broadcast_to(gc[:, C - 1:C, :], (B, C, LANE))
    decay = jnp.exp(jnp.where(row >= col, gc_i - gc_j, -1e30))
    eg = jnp.exp(gc)
    kk = mm_nt(k, k)
    a_mat = jnp.where(row > col, jnp.broadcast_to(beta, (B, C, C)) * kk * decay, 0.0)
    eye = (row == col).astype(F32)
    y = -a_mat
    t = eye + y
    for _ in range(5):
        y = mm(y, y)
        t = t + mm(t, y)
    bb = jnp.broadcast_to(beta, (B, C, LANE))
    u0 = mm3(t, v * bb)
    w = mm3(t, k * (bb * eg))
    qk = mm_nt(q, k) * decay
    q_dec = q * eg
    k_dec = k * jnp.exp(g_last - gc)
    cd = jnp.exp(g_last)[:, :8, :]
    return u0, w, qk, q_dec, k_dec, cd


def _shift_rows(x, j, down):
    if j == 0:
        return x
    n = x.shape[0]
    r = lax.broadcasted_iota(jnp.int32, x.shape, 0)
    if down:
        return jnp.where(r >= j, pltpu.roll(x, j, 0), 0.0)
    return jnp.where(r < n - j, pltpu.roll(x, n - j, 0), 0.0)


def dn_conv_fwd(proj, conv_w, exchange=None):
    s = proj.shape[0]

    def body(x_ref, w_ref, o_ref):
        x = x_ref[...]
        w = w_ref[...]
        pre = x * w[3:4, :]
        for j in (1, 2, 3):
            pre = pre + _shift_rows(x, j, True) * w[3 - j:4 - j, :]
        o_ref[...] = _silu(pre)

    return hosted_call(
        body, exchange, name="dn_conv_fwd", grid=(DN_QKV // LANE,),
        in_specs=[pl.BlockSpec((s, LANE), lambda j: (0, j)), pl.BlockSpec((8, LANE), lambda j: (0, j))],
        out_specs=[pl.BlockSpec((s, LANE), lambda j: (0, j))],
        out_shape=[jax.ShapeDtypeStruct((s, DN_QKV), F32)], sem=("parallel",),
    )(proj, conv_w)


def dn_conv_bwd(dact, proj, conv_w, dproj_in, dab, blk0, name):
    s = proj.shape[0]
    nblk = dact.shape[1] // LANE
    extra = 2 if dab is not None else 0

    def body(*refs):
        if dab is not None:
            da_ref, x_ref, w_ref, _, dab_ref, dp_ref, dw_ref = refs
        else:
            da_ref, x_ref, w_ref, _, dp_ref, dw_ref = refs
        j = pl.program_id(0)

        @pl.when(j < nblk)
        def _():
            x = x_ref[...]
            w = w_ref[...]
            xs = [_shift_rows(x, 3 - kk_, True) for kk_ in range(4)]
            pre = xs[0] * w[0:1, :]
            for kk_ in (1, 2, 3):
                pre = pre + xs[kk_] * w[kk_:kk_ + 1, :]
            dpre = da_ref[...] * _silu_grad(pre)
            dx = dpre * w[3:4, :]
            for jj in (1, 2, 3):
                dx = dx + _shift_rows(dpre, jj, False) * w[3 - jj:4 - jj, :]
            dp_ref[...] = dx
            rows = [jnp.sum(dpre * xs[kk_], axis=0, keepdims=True) for kk_ in range(4)]
            dw_ref[...] = jnp.concatenate(rows + [jnp.zeros((4, LANE), F32)], axis=0)

        if dab is not None:
            @pl.when(j == nblk)
            def _():
                dp_ref[...] = dab_ref[...]

            @pl.when(j == nblk + 1)
            def _():
                dp_ref[...] = jnp.zeros_like(dp_ref)

    cl = lambda j: jnp.minimum(j, nblk - 1)
    in_specs = [pl.BlockSpec((s, LANE), lambda j: (0, cl(j))),
                pl.BlockSpec((s, LANE), lambda j: (0, blk0 + cl(j))),
                pl.BlockSpec((8, LANE), lambda j: (0, blk0 + cl(j))),
                pl.BlockSpec(memory_space=pl.ANY)]
    args = [dact, proj, conv_w, dproj_in]
    if dab is not None:
        in_specs.append(pl.BlockSpec((s, LANE), lambda j: (0, 0)))
        args.append(dab)
    return pl.pallas_call(
        body, name=name, grid=(nblk + extra,), in_specs=in_specs,
        out_specs=[pl.BlockSpec((s, LANE), lambda j: (0, blk0 + j)), pl.BlockSpec((8, LANE), lambda j: (0, cl(j)))],
        out_shape=[jax.ShapeDtypeStruct(dproj_in.shape, F32), jax.ShapeDtypeStruct((8, dact.shape[1]), F32)],
        input_output_aliases={3: 0}, compiler_params=_params("arbitrary"),
    )(*args)


DN_GROUP = 8
DN_ROWS = DN_GROUP * DN_CHUNK


def dn_prep_fwd(act, proj, alog, dtb, exchange=None):
    s = act.shape[0]
    nc = s // DN_CHUNK
    C = DN_CHUNK

    def body(q_ref, k_ref, v_ref, ab_ref, al_ref, dt_ref, u0_ref, w_ref, qd_ref, kd_ref, qk_ref, cd_ref):
        qh = pl.program_id(0)
        al = al_ref[0:1, :]
        dt = dt_ref[0:1, :]
        chunks = lambda x: x.reshape(DN_GROUP, C, x.shape[-1])
        rows = lambda x: x.reshape(DN_ROWS, x.shape[-1])
        qt, kt, ab = chunks(q_ref[...]), chunks(k_ref[...]), chunks(ab_ref[...])
        for hv in range(2):
            cs = slice(hv * LANE, (hv + 1) * LANE)
            u0, w, qk, qd, kd, cd = _dn_chunk(qt, kt, chunks(v_ref[:, cs]), ab, al, dt, 2 * qh + hv)
            u0_ref[:, cs] = rows(u0)
            w_ref[:, cs] = rows(w)
            qd_ref[:, cs] = rows(qd)
            kd_ref[:, cs] = rows(kd)
            qk_ref[hv] = rows(qk)
            cd_ref[hv] = cd

    big = pl.BlockSpec((DN_ROWS, 2 * LANE), lambda h, g: (g, h))
    wide = jax.ShapeDtypeStruct((s, MIX_WIDTH), F32)
    return hosted_call(
        body, exchange, name="dn_prep_fwd", grid=(DN_QK_HEADS, s // DN_ROWS),
        in_specs=[pl.BlockSpec((DN_ROWS, LANE), lambda h, g: (g, h)),
                  pl.BlockSpec((DN_ROWS, LANE), lambda h, g: (g, DN_QK_HEADS + h)),
                  pl.BlockSpec((DN_ROWS, 2 * LANE), lambda h, g: (g, DN_QK_HEADS + h)),
                  pl.BlockSpec((DN_ROWS, LANE), lambda h, g: (g, DN_AB_BLK)),
                  pl.BlockSpec((8, LANE), lambda h, g: (0, 0)), pl.BlockSpec((8, LANE), lambda h, g: (0, 0))],
        out_specs=[big, big, big, big,
                   pl.BlockSpec((2, DN_ROWS, C), lambda h, g: (h, g, 0)),
                   pl.BlockSpec((2, DN_GROUP, 8, LANE), lambda h, g: (h, g, 0, 0))],
        out_shape=[wide, wide, wide, wide, jax.ShapeDtypeStruct((DN_V_HEADS, s, C), F32),
                   jax.ShapeDtypeStruct((DN_V_HEADS, nc, 8, LANE), F32)],
        sem=("parallel", "parallel"),
    )(act, act, act, proj, alog, dtb)


def dn_prep_bwd(act, proj, alog, dtb, du0, dw, dqd, dkd, dqk, dcd, exchange=None):
    s = act.shape[0]
    C = DN_CHUNK

    def body(q_ref, k_ref, v_ref, ab_ref, al_ref, dt_ref, du0_ref, dw_ref, dqd_ref, dkd_ref, dqk_ref, dcd_ref,
             dq_ref, dk_ref, dv_ref, dab_ref, dal_ref, ddt_ref):
        g_id = pl.program_id(0)
        qh = pl.program_id(1)
        al = al_ref[0:1, :]
        dt = dt_ref[0:1, :]

        @pl.when(qh == 0)
        def _():
            dab_ref[...] = jnp.zeros_like(dab_ref)

        @pl.when((qh == 0) & (g_id == 0))
        def _():
            dal_ref[...] = jnp.zeros_like(dal_ref)
            ddt_ref[...] = jnp.zeros_like(ddt_ref)

        chunks = lambda x: x.reshape(DN_GROUP, C, x.shape[-1])
        rows = lambda x: x.reshape(DN_ROWS, x.shape[-1])
        qt, kt, ab = chunks(q_ref[...]), chunks(k_ref[...]), chunks(ab_ref[...])
        dq_acc = jnp.zeros((DN_ROWS, LANE), F32)
        dk_acc = jnp.zeros((DN_ROWS, LANE), F32)
        for hv in range(2):
            cs = slice(hv * LANE, (hv + 1) * LANE)
            h = 2 * qh + hv
            f = lambda qt_, kt_, v_, ab_, a_, d_: _dn_chunk(qt_, kt_, v_, ab_, a_, d_, h)
            _, vjp = jax.vjp(f, qt, kt, chunks(v_ref[:, cs]), ab, al, dt)
            dq, dk, dv, dab, dal, ddt = vjp((chunks(du0_ref[:, cs]), chunks(dw_ref[:, cs]), chunks(dqk_ref[hv]),
                                             chunks(dqd_ref[:, cs]), chunks(dkd_ref[:, cs]), dcd_ref[hv]))
            dq_acc = dq_acc + rows(dq)
            dk_acc = dk_acc + rows(dk)
            dv_ref[:, cs] = rows(dv)
            dab_ref[...] += rows(dab)
            dal_ref[0:1, :] += dal
            ddt_ref[0:1, :] += ddt
        dq_ref[...] = dq_acc
        dk_ref[...] = dk_acc

    big = pl.BlockSpec((DN_ROWS, 2 * LANE), lambda g, h: (g, h))
    one = pl.BlockSpec((DN_ROWS, LANE), lambda g, h: (g, h))
    small = pl.BlockSpec((8, LANE), lambda g, h: (0, 0))
    return hosted_call(
        body, exchange, name="dn_prep_bwd", grid=(s // DN_ROWS, DN_QK_HEADS),
        in_specs=[one, pl.BlockSpec((DN_ROWS, LANE), lambda g, h: (g, DN_QK_HEADS + h)),
                  pl.BlockSpec((DN_ROWS, 2 * LANE), lambda g, h: (g, DN_QK_HEADS + h)),
                  pl.BlockSpec((DN_ROWS, LANE), lambda g, h: (g, DN_AB_BLK)), small, small,
                  big, big, big, big,
                  pl.BlockSpec((2, DN_ROWS, C), lambda g, h: (h, g, 0)),
                  pl.BlockSpec((2, DN_GROUP, 8, LANE), lambda g, h: (h, g, 0, 0))],
        out_specs=[one, one, big, pl.BlockSpec((DN_ROWS, LANE), lambda g, h: (g, 0)), small, small],
        out_shape=[jax.ShapeDtypeStruct((s, DN_QK_WIDTH), F32), jax.ShapeDtypeStruct((s, DN_QK_WIDTH), F32),
                   jax.ShapeDtypeStruct((s, MIX_WIDTH), F32), jax.ShapeDtypeStruct((s, LANE), F32),
                   jax.ShapeDtypeStruct((8, LANE), F32), jax.ShapeDtypeStruct((8, LANE), F32)],
        sem=("arbitrary", "arbitrary"),
    )(act, act, act, proj, alog, dtb, du0, dw, dqd, dkd, dqk, dcd)


DN_SCAN_HEADS = 2


def dn_scan_fwd(u0, w, qd, kd, qk, cd, width, exchange=None):
    s = u0.shape[0]
    nc = s // DN_CHUNK
    C = DN_CHUNK

    nh = DN_SCAN_HEADS
    heads = range(nh)
    cols = [slice(h * LANE, (h + 1) * LANE) for h in heads]

    def body(u0_ref, w_ref, qd_ref, kd_ref, qk_ref, cd_ref, o_ref, st_ref):
        def step(c, states):
            rs = pl.ds(pl.multiple_of(c * C, C), C)
            for h in heads:
                st_ref[h, c] = states[h]
            ws = [_dg(w_ref[rs, cols[h]], states[h], _NN) for h in heads]
            us = [u0_ref[rs, cols[h]] - ws[h] for h in heads]
            os_ = [_dg(qd_ref[rs, cols[h]], states[h], _NN) for h in heads]
            for h in heads:
                o_ref[rs, cols[h]] = os_[h] + _dg(qk_ref[h, rs, :], us[h], _NN)
            return tuple(cd_ref[h, c][0:1, :] * states[h] + _dg(kd_ref[rs, cols[h]], us[h], _TN) for h in heads)

        lax.fori_loop(0, nc, step, tuple(jnp.zeros((HEAD_DIM, HEAD_DIM), F32) for _ in heads))

    col = pl.BlockSpec((s, nh * LANE), lambda h: (0, h))
    return hosted_call(
        body, exchange, name="dn_scan_fwd", grid=(DN_V_HEADS // nh,),
        in_specs=[col, col, col, col, pl.BlockSpec((nh, s, C), lambda h: (h, 0, 0)),
                  pl.BlockSpec((nh, nc, 8, LANE), lambda h: (h, 0, 0, 0))],
        out_specs=[col, pl.BlockSpec((nh, nc, HEAD_DIM, HEAD_DIM), lambda h: (h, 0, 0, 0))],
        out_shape=[jax.ShapeDtypeStruct((s, width), F32),
                   jax.ShapeDtypeStruct((DN_V_HEADS, nc, HEAD_DIM, HEAD_DIM), F32)],
        sem=("parallel",),
    )(u0, w, qd, kd, qk, cd)


def dn_scan_bwd(do, u0, w, qd, kd, qk, cd, states, exchange=None):
    s = u0.shape[0]
    nc = s // DN_CHUNK
    C = DN_CHUNK

    nh = DN_SCAN_HEADS
    heads = range(nh)
    cols = [slice(h * LANE, (h + 1) * LANE) for h in heads]

    def body(do_ref, u0_ref, w_ref, qd_ref, kd_ref, qk_ref, cd_ref, st_ref,
             du0_ref, dw_ref, dqd_ref, dkd_ref, dqk_ref, dcd_ref):
        def step(i, dstates):
            c = nc - 1 - i
            rs = pl.ds(pl.multiple_of(c * C, C), C)
            states = [st_ref[h, c] for h in heads]
            gs = [do_ref[rs, cols[h]] for h in heads]
            w_cs = [w_ref[rs, cols[h]] for h in heads]
            qd_cs = [qd_ref[rs, cols[h]] for h in heads]
            cd_rows = [cd_ref[h, c][0:1, :] for h in heads]
            us = [u0_ref[rs, cols[h]] - _dg(w_cs[h], states[h], _NN) for h in heads]
            dus = [_dg(qk_ref[h, rs, :], gs[h], _TN) + _dg(kd_ref[rs, cols[h]], dstates[h], _NN) for h in heads]
            for h in heads:
                du0_ref[rs, cols[h]] = dus[h]
                dw_ref[rs, cols[h]] = -_dg(dus[h], states[h], _NT)
                dqd_ref[rs, cols[h]] = _dg(gs[h], states[h], _NT)
                dkd_ref[rs, cols[h]] = _dg(us[h], dstates[h], _NT)
                dqk_ref[h, rs, :] = _dg(gs[h], us[h], _NT)
                dcd_row = jnp.sum(states[h] * dstates[h], axis=0, keepdims=True)
                dcd_ref[h, c] = jnp.concatenate([dcd_row, jnp.zeros((7, LANE), F32)], axis=0)
            return tuple(cd_rows[h] * dstates[h] + _dg(qd_cs[h], gs[h], _TN) - _dg(w_cs[h], dus[h], _TN)
                         for h in heads)

        lax.fori_loop(0, nc, step, tuple(jnp.zeros((HEAD_DIM, HEAD_DIM), F32) for _ in heads))

    col = pl.BlockSpec((s, nh * LANE), lambda h: (0, h))
    qk_spec = pl.BlockSpec((nh, s, C), lambda h: (h, 0, 0))
    cd_spec = pl.BlockSpec((nh, nc, 8, LANE), lambda h: (h, 0, 0, 0))
    wide = jax.ShapeDtypeStruct((s, MIX_WIDTH), F32)
    return hosted_call(
        body, exchange, name="dn_scan_bwd", grid=(DN_V_HEADS // nh,),
        in_specs=[col, col, col, col, col, qk_spec, cd_spec,
                  pl.BlockSpec((nh, nc, HEAD_DIM, HEAD_DIM), lambda h: (h, 0, 0, 0))],
        out_specs=[col, col, col, col, qk_spec, cd_spec],
        out_shape=[wide, wide, wide, wide, jax.ShapeDtypeStruct((DN_V_HEADS, s, C), F32),
                   jax.ShapeDtypeStruct((DN_V_HEADS, nc, 8, LANE), F32)],
        sem=("parallel",),
    )(do, u0, w, qd, kd, qk, cd, states)


SB_T = 256
SB_GROUPS = (4, 2, 1)


def _sb_scores(q, kbs, diff, lims):
    zs = [_dg(q, kb, _NT) * (HEAD_DIM ** -0.5) for kb in kbs]
    masks = [diff < lim for lim in lims]
    lss = [_log_sigmoid(z) for z in zs]
    lrs = [jnp.where(m, ls - z, 0.0) for m, ls, z in zip(masks, lss, zs)]
    return masks, lss, lrs


def _sb_diff():
    return lax.broadcasted_iota(jnp.int32, (SB_T, SB_T), 1) - lax.broadcasted_iota(jnp.int32, (SB_T, SB_T), 0)


def _sb_loop(n_tiles, step, carry):
    done = 0
    for size in SB_GROUPS:
        groups = (n_tiles - done) // size
        carry = lax.fori_loop(0, groups, lambda p, c, s=size, d=done: step(d + p * s, s, c), carry)
        done = done + groups * size
    return carry


def sb_fwd(qkn, proj, v_blk0, width, exchange=None):
    s = qkn.shape[0]

    def body(q_ref, k_ref, v_ref, o_ref, lt_ref):
        i = pl.program_id(1)
        q = q_ref[...]
        diff = _sb_diff()
        after = (diff < 0).astype(BF16)

        def step(first, n, carry):
            run, acc = carry
            tiles = [first + t for t in range(n)]
            kss = [pl.ds(pl.multiple_of((i - t) * SB_T, SB_T), SB_T) for t in tiles]
            masks, lss, lrs = _sb_scores(q, [k_ref[ks, :] for ks in kss], diff, [t * SB_T for t in tiles])
            within = [_dg_exact_rhs(lr, after, _NN) for lr in lrs]
            sums = [jnp.sum(lr, axis=1, keepdims=True) for lr in lrs]
            for t in range(n):
                wts = jnp.where(masks[t], jnp.exp(lss[t] + (within[t] + run)), 0.0)
                acc = acc + _dg(wts, v_ref[kss[t], :], _NN)
                run = run + sums[t]
            return run, acc

        run, acc = _sb_loop(i + 1, step, (jnp.zeros((SB_T, 1), F32), jnp.zeros((SB_T, HEAD_DIM), F32)))
        o_ref[...] = acc
        lt_ref[0] = run

    return hosted_call(
        body, exchange, name="sb_fwd", grid=(SB_HEADS, s // SB_T),
        in_specs=[pl.BlockSpec((SB_T, LANE), lambda h, i: (i, h)),
                  pl.BlockSpec((s, LANE), lambda h, i: (0, SB_HEADS + h)),
                  pl.BlockSpec((s, LANE), lambda h, i: (0, v_blk0 + h))],
        out_specs=[pl.BlockSpec((SB_T, LANE), lambda h, i: (i, h)), pl.BlockSpec((1, SB_T, 1), lambda h, i: (h, i, 0))],
        out_shape=[jax.ShapeDtypeStruct((s, width), F32), jax.ShapeDtypeStruct((SB_HEADS, s, 1), F32)],
        sem=("parallel", "parallel"),
    )(qkn, qkn, proj)


def sb_bwd(qkn, proj, v_blk0, do, ltot, dproj_in):
    s = qkn.shape[0]

    def body(q_ref, k_ref, v_ref, do_ref, lt_ref, _, dq_ref, dk_ref, dv_ref):
        i = pl.program_id(1)

        @pl.when(i == 0)
        def _():
            dk_ref[...] = jnp.zeros_like(dk_ref)
            dv_ref[...] = jnp.zeros_like(dv_ref)

        q = q_ref[...]
        g = do_ref[...]
        ltot = lt_ref[0]
        diff = _sb_diff()
        upto = (diff >= 0).astype(BF16)
        before = (diff > 0).astype(BF16)

        def step(first, n, carry):
            plr, pdl, dq = carry
            tiles = [first + t for t in range(n)]
            kss = [pl.ds(pl.multiple_of(j * SB_T, SB_T), SB_T) for j in tiles]
            kbs = [k_ref[ks, :] for ks in kss]
            vbs = [v_ref[ks, :] for ks in kss]
            masks, lss, lrs = _sb_scores(q, kbs, diff, [(i - j) * SB_T for j in tiles])
            dwts = [_dg(g, vb, _NT) for vb in vbs]
            within = [_dg_exact_rhs(lr, upto, _NN) for lr in lrs]
            wtss, dls = [], []
            for t in range(n):
                wts = jnp.where(masks[t], jnp.exp(lss[t] + (ltot - (within[t] + plr))), 0.0)
                plr = plr + jnp.sum(lrs[t], axis=1, keepdims=True)
                wtss.append(wts)
                dls.append(dwts[t] * wts)
            dwithin = [_dg_exact_rhs(dl, before, _NN) for dl in dls]
            for t in range(n):
                sz = jnp.exp(lss[t])
                dz = jnp.where(masks[t], dls[t] * (1.0 - sz) - sz * (dwithin[t] + pdl), 0.0) * (HEAD_DIM ** -0.5)
                pdl = pdl + jnp.sum(dls[t], axis=1, keepdims=True)
                dk_ref[kss[t], :] += _dg(dz, q, _TN)
                dv_ref[kss[t], :] += _dg(wtss[t], g, _TN)
                dq = dq + _dg(dz, kbs[t], _NN)
            return plr, pdl, dq

        zero = jnp.zeros((SB_T, 1), F32)
        _, _, dq = _sb_loop(i + 1, step, (zero, zero, jnp.zeros((SB_T, HEAD_DIM), F32)))
        dq_ref[...] = dq

    tile = pl.BlockSpec((SB_T, LANE), lambda h, i: (i, h))
    colspec = pl.BlockSpec((s, LANE), lambda h, i: (0, h))
    return pl.pallas_call(
        body, name="sb_bwd", grid=(SB_HEADS, s // SB_T),
        in_specs=[tile, pl.BlockSpec((s, LANE), lambda h, i: (0, SB_HEADS + h)),
                  pl.BlockSpec((s, LANE), lambda h, i: (0, v_blk0 + h)), tile,
                  pl.BlockSpec((1, SB_T, 1), lambda h, i: (h, i, 0)), pl.BlockSpec(memory_space=pl.ANY)],
        out_specs=[tile, colspec, pl.BlockSpec((s, LANE), lambda h, i: (0, v_blk0 + h))],
        out_shape=[jax.ShapeDtypeStruct((s, MIX_WIDTH), F32), jax.ShapeDtypeStruct((s, MIX_WIDTH), F32),
                   jax.ShapeDtypeStruct(dproj_in.shape, F32)],
        input_output_aliases={5: 2}, compiler_params=_params("parallel", "arbitrary"),
    )(qkn, qkn, proj, do, ltot, dproj_in)


ROW_TILE = 256
HN_HEADS = 8


def head_norm_fwd(x, x_blk0, nblk, gains, out_dtype, out_width, name):
    s = x.shape[0]
    assert x_blk0 % HN_HEADS == 0 and nblk % (HN_HEADS * gains.shape[0]) == 0
    per = nblk // gains.shape[0] // HN_HEADS
    w = HN_HEADS * LANE

    def body(x_ref, g_ref, o_ref):
        gain = g_ref[0, 0:1, :]
        for j in range(HN_HEADS):
            cs = slice(j * LANE, (j + 1) * LANE)
            xv = x_ref[:, cs]
            r = lax.rsqrt(jnp.mean(xv * xv, axis=1, keepdims=True) + EPS)
            o_ref[:, cs] = (xv * r * gain).astype(out_dtype)

    return pl.pallas_call(
        body, name=name, grid=(nblk // HN_HEADS, s // ROW_TILE),
        in_specs=[pl.BlockSpec((ROW_TILE, w), lambda j, t: (t, x_blk0 // HN_HEADS + j)),
                  pl.BlockSpec((1, 8, LANE), lambda j, t: (j // per, 0, 0))],
        out_specs=pl.BlockSpec((ROW_TILE, w), lambda j, t: (t, j)),
        out_shape=jax.ShapeDtypeStruct((s, out_width), out_dtype), compiler_params=_params("parallel", "parallel"),
    )(x, gains)


def head_norm_bwd(dy, dy_blk0, x, x_blk0, nblk, gain, dst, dst_blk0, name):
    s = x.shape[0]

    def body(*refs):
        if dst is not None:
            dy_ref, x_ref, g_ref, _, dx_ref, dg_ref = refs
        else:
            dy_ref, x_ref, g_ref, dx_ref, dg_ref = refs

        @pl.when((pl.program_id(0) == 0) & (pl.program_id(1) == 0))
        def _():
            dg_ref[...] = jnp.zeros_like(dg_ref)

        gain = g_ref[0:1, :]
        dg = jnp.zeros((1, LANE), F32)
        for j in range(HN_HEADS):
            cs = slice(j * LANE, (j + 1) * LANE)
            xv = x_ref[:, cs]
            g = dy_ref[:, cs]
            r = lax.rsqrt(jnp.mean(xv * xv, axis=1, keepdims=True) + EPS)
            gy = g * gain
            dx_ref[:, cs] = r * gy - xv * (r * r * r) * jnp.mean(gy * xv, axis=1, keepdims=True)
            dg = dg + jnp.sum(g * xv * r, axis=0, keepdims=True)
        dg_ref[0:1, :] += dg

    assert dy_blk0 % HN_HEADS == 0 and x_blk0 % HN_HEADS == 0 and dst_blk0 % HN_HEADS == 0 and nblk % HN_HEADS == 0
    w = HN_HEADS * LANE
    in_specs = [pl.BlockSpec((ROW_TILE, w), lambda j, t: (t, dy_blk0 // HN_HEADS + j)),
                pl.BlockSpec((ROW_TILE, w), lambda j, t: (t, x_blk0 // HN_HEADS + j)),
                pl.BlockSpec((8, LANE), lambda j, t: (0, 0))]
    args = [dy, x, gain]
    aliases = {}
    if dst is not None:
        in_specs.append(pl.BlockSpec(memory_space=pl.ANY))
        args.append(dst)
        aliases = {3: 0}
        out0 = jax.ShapeDtypeStruct(dst.shape, F32)
    else:
        out0 = jax.ShapeDtypeStruct((s, (dst_blk0 + nblk) * LANE), F32)
    return pl.pallas_call(
        body, name=name, grid=(nblk // HN_HEADS, s // ROW_TILE), in_specs=in_specs,
        out_specs=[pl.BlockSpec((ROW_TILE, w), lambda j, t: (t, dst_blk0 // HN_HEADS + j)),
                   pl.BlockSpec((8, LANE), lambda j, t: (0, 0))],
        out_shape=[out0, jax.ShapeDtypeStruct((8, LANE), F32)],
        input_output_aliases=aliases, compiler_params=_params("arbitrary", "arbitrary"),
    )(*args)


def _xa_head(xq, kraw, v, qg, kg):
    q = xq * lax.rsqrt(jnp.mean(xq * xq, axis=1, keepdims=True) + EPS) * qg
    k = kraw * lax.rsqrt(jnp.mean(kraw * kraw, axis=1, keepdims=True) + EPS) * kg
    sc = mm_nt(q, k) * (XA_DIM ** -0.5)
    e = jnp.exp(sc - lax.stop_gradient(jnp.max(sc, axis=1, keepdims=True)))
    return mm(e / jnp.sum(e, axis=1, keepdims=True), v)


def xa_fwd(proj, xq_blk0, kv, qg, kg, cat):
    s = proj.shape[0]
    n_mem = kv.shape[0]

    def body(xq_ref, k_ref, v_ref, qg_ref, kg_ref, _, o_ref):
        o_ref[...] = _xa_head(xq_ref[...], k_ref[...], v_ref[...], qg_ref[0:1, :], kg_ref[0:1, :])

    gain = pl.BlockSpec((8, XA_DIM), lambda h, t: (0, 0))
    return pl.pallas_call(
        body, name="xa_fwd", grid=(XA_HEADS, s // ROW_TILE),
        in_specs=[pl.BlockSpec((ROW_TILE, XA_DIM), lambda h, t: (t, xq_blk0 // 2 + h)),
                  pl.BlockSpec((n_mem, XA_DIM), lambda h, t: (0, h)),
                  pl.BlockSpec((n_mem, XA_DIM), lambda h, t: (0, XA_HEADS + h)), gain, gain,
                  pl.BlockSpec(memory_space=pl.ANY)],
        out_specs=pl.BlockSpec((ROW_TILE, XA_DIM), lambda h, t: (t, MIX_WIDTH // XA_DIM + h)),
        out_shape=jax.ShapeDtypeStruct(cat.shape, F32), input_output_aliases={5: 0},
        compiler_params=_params("parallel", "parallel"),
    )(proj, kv, kv, qg, kg, cat)


def xa_bwd(proj, xq_blk0, kv, qg, kg, dcat, dproj_in):
    s = proj.shape[0]
    n_mem = kv.shape[0]

    def body(xq_ref, k_ref, v_ref, qg_ref, kg_ref, do_ref, _, dxq_ref, dk_ref, dv_ref, dqg_ref, dkg_ref):
        h = pl.program_id(0)
        t = pl.program_id(1)

        @pl.when(t == 0)
        def _():
            dk_ref[...] = jnp.zeros_like(dk_ref)
            dv_ref[...] = jnp.zeros_like(dv_ref)

        @pl.when((t == 0) & (h == 0))
        def _():
            dqg_ref[...] = jnp.zeros_like(dqg_ref)
            dkg_ref[...] = jnp.zeros_like(dkg_ref)

        _, vjp = jax.vjp(_xa_head, xq_ref[...], k_ref[...], v_ref[...], qg_ref[0:1, :], kg_ref[0:1, :])
        dxq, dk, dv, dqg, dkg = vjp(do_ref[...])
        dxq_ref[...] = dxq
        dk_ref[...] += dk
        dv_ref[...] += dv
        dqg_ref[0:1, :] += dqg
        dkg_ref[0:1, :] += dkg

    gain = pl.BlockSpec((8, XA_DIM), lambda h, t: (0, 0))
    kspec = pl.BlockSpec((n_mem, XA_DIM), lambda h, t: (0, h))
    vspec = pl.BlockSpec((n_mem, XA_DIM), lambda h, t: (0, XA_HEADS + h))
    return pl.pallas_call(
        body, name="xa_bwd", grid=(XA_HEADS, s // ROW_TILE),
        in_specs=[pl.BlockSpec((ROW_TILE, XA_DIM), lambda h, t: (t, xq_blk0 // 2 + h)), kspec, vspec, gain, gain,
                  pl.BlockSpec((ROW_TILE, XA_DIM), lambda h, t: (t, MIX_WIDTH // XA_DIM + h)),
                  pl.BlockSpec(memory_space=pl.ANY)],
        out_specs=[pl.BlockSpec((ROW_TILE, XA_DIM), lambda h, t: (t, xq_blk0 // 2 + h)), kspec, kspec, gain, gain],
        out_shape=[jax.ShapeDtypeStruct(dproj_in.shape, F32), jax.ShapeDtypeStruct((n_mem, XA_WIDTH), F32),
                   jax.ShapeDtypeStruct((n_mem, XA_WIDTH), F32), jax.ShapeDtypeStruct((8, XA_DIM), F32),
                   jax.ShapeDtypeStruct((8, XA_DIM), F32)],
        input_output_aliases={6: 0}, compiler_params=_params("arbitrary", "arbitrary"),
    )(proj, kv, kv, qg, kg, dcat, dproj_in)


GATE_ROWS = 1024


def gate_fwd(cat, proj, z_blk0):
    s = cat.shape[0]

    def body(c_ref, z_ref, y_ref):
        y_ref[...] = (c_ref[...] * _silu(z_ref[...])).astype(BF16)

    w = 2 * LANE
    rt = min(GATE_ROWS, s)
    return pl.pallas_call(
        body, name="gate_fwd", grid=(INNER // w, s // rt),
        in_specs=[pl.BlockSpec((rt, w), lambda j, t: (t, j)),
                  pl.BlockSpec((rt, w), lambda j, t: (t, z_blk0 // 2 + j))],
        out_specs=pl.BlockSpec((rt, w), lambda j, t: (t, j)),
        out_shape=jax.ShapeDtypeStruct((s, INNER), BF16), compiler_params=_params("parallel", "parallel"),
    )(cat, proj)


def gate_bwd(dy, cat, proj, z_blk0):
    s = cat.shape[0]

    def body(dy_ref, c_ref, z_ref, dc_ref, dz_ref):
        z = z_ref[...]
        g = dy_ref[...]
        dc_ref[...] = g * _silu(z)
        dz_ref[...] = g * c_ref[...] * _silu_grad(z)

    w = 2 * LANE
    rt = min(GATE_ROWS, s)
    tile = pl.BlockSpec((rt, w), lambda j, t: (t, j))
    ztile = pl.BlockSpec((rt, w), lambda j, t: (t, z_blk0 // 2 + j))
    return pl.pallas_call(
        body, name="gate_bwd", grid=(INNER // w, s // rt), in_specs=[tile, tile, ztile],
        out_specs=[tile, ztile],
        out_shape=[jax.ShapeDtypeStruct((s, INNER), F32), jax.ShapeDtypeStruct(proj.shape, F32)],
        compiler_params=_params("parallel", "parallel"),
    )(dy, cat, proj)


NORM_ROWS = 256


def rms_fwd(x, gain):
    s, d = x.shape

    def body(x_ref, g_ref, o_ref):
        xv = x_ref[...]
        r = lax.rsqrt(jnp.mean(xv * xv, axis=1, keepdims=True) + EPS)
        o_ref[...] = (xv * r * g_ref[0:1, :]).astype(BF16)

    return pl.pallas_call(
        body, name="rms_fwd", grid=(s // NORM_ROWS,),
        in_specs=[pl.BlockSpec((NORM_ROWS, d), lambda t: (t, 0)), pl.BlockSpec((8, d), lambda t: (0, 0))],
        out_specs=pl.BlockSpec((NORM_ROWS, d), lambda t: (t, 0)),
        out_shape=jax.ShapeDtypeStruct((s, d), BF16), compiler_params=_params("parallel"),
    )(x, gain)


def rms_bwd(dh, x, gain, dres):
    s, d = x.shape

    def body(*refs):
        if dres is not None:
            dh_ref, x_ref, g_ref, dr_ref, dx_ref, dg_ref = refs
        else:
            dh_ref, x_ref, g_ref, dx_ref, dg_ref = refs

        @pl.when(pl.program_id(0) == 0)
        def _():
            dg_ref[...] = jnp.zeros_like(dg_ref)

        xv = x_ref[...]
        g = dh_ref[...]
        r = lax.rsqrt(jnp.mean(xv * xv, axis=1, keepdims=True) + EPS)
        gy = g * g_ref[0:1, :]
        dx = r * gy - xv * (r * r * r) * jnp.mean(gy * xv, axis=1, keepdims=True)
        dx_ref[...] = dx + dr_ref[...] if dres is not None else dx
        dg_ref[0:1, :] += jnp.sum(g * xv * r, axis=0, keepdims=True)

    tile = pl.BlockSpec((NORM_ROWS, d), lambda t: (t, 0))
    gspec = pl.BlockSpec((8, d), lambda t: (0, 0))
    args = [dh, x, gain] + ([dres] if dres is not None else [])
    return pl.pallas_call(
        body, name="rms_bwd", grid=(s // NORM_ROWS,),
        in_specs=[tile, tile, gspec] + ([tile] if dres is not None else []),
        out_specs=[tile, gspec],
        out_shape=[jax.ShapeDtypeStruct((s, d), F32), jax.ShapeDtypeStruct((8, d), F32)],
        compiler_params=_params("arbitrary"),
    )(*args)


def loss_fwd_bwd(y, target):
    s, d = y.shape

    def body(y_ref, t_ref, l_ref, dy_ref):
        @pl.when(pl.program_id(0) == 0)
        def _():
            l_ref[...] = jnp.zeros_like(l_ref)

        err = y_ref[...] - t_ref[...]
        dy_ref[...] = err * (1.0 / d)
        part = 0.5 * jnp.sum(jnp.mean(err * err, axis=1, keepdims=True), axis=0, keepdims=True)
        r = lax.broadcasted_iota(jnp.int32, (8, LANE), 0)
        c = lax.broadcasted_iota(jnp.int32, (8, LANE), 1)
        l_ref[...] += jnp.where((r == 0) & (c == 0), part, 0.0)

    tile = pl.BlockSpec((NORM_ROWS, d), lambda t: (t, 0))
    return pl.pallas_call(
        body, name="loss", grid=(s // NORM_ROWS,), in_specs=[tile, tile],
        out_specs=[pl.BlockSpec((8, LANE), lambda t: (0, 0)), tile],
        out_shape=[jax.ShapeDtypeStruct((8, LANE), F32), jax.ShapeDtypeStruct((s, d), F32)],
        compiler_params=_params("arbitrary"),
    )(y, target)


def matmul(a, b, mode, out_dtype, tm, tn, tk, name, add=None, b_blocked=False, out_blocks=None, exchange=None):
    if b_blocked:
        nb, _, width = b.shape
        bshape = (b.shape[1], nb * width)
    else:
        bshape = b.shape
    if mode == "tn":
        (kdim, m), n = a.shape, bshape[1]
    else:
        (m, kdim), n = a.shape, (bshape[1] if mode == "nn" else bshape[0])
    tm, tn, tk = min(tm, m), min(tn, n), min(tk, kdim)
    assert m % tm == 0 and n % tn == 0 and kdim % tk == 0, (name, m, n, kdim)
    nk = kdim // tk
    dims = {"nn": _NN, "nt": _NT, "tn": _TN}[mode]

    def body(*refs):
        if add is not None:
            a_ref, b_ref, add_ref, o_ref, acc_ref = refs
        else:
            a_ref, b_ref, o_ref, acc_ref = refs
        k = pl.program_id(2)

        @pl.when(k == 0)
        def _():
            acc_ref[...] = jnp.zeros_like(acc_ref)

        acc_ref[...] += _dg(a_ref[...], b_ref[...], dims)

        @pl.when(k == nk - 1)
        def _():
            r = acc_ref[...]
            if add is not None:
                r = r + add_ref[...]
            o_ref[...] = r.astype(out_dtype)

    a_spec = pl.BlockSpec((tk, tm), lambda i, j, k: (k, i)) if mode == "tn" else pl.BlockSpec((tm, tk), lambda i, j, k: (i, k))
    if b_blocked and mode == "nn":
        per = width // tn
        assert width % tn == 0
        b_spec = pl.BlockSpec((None, tk, tn), lambda i, j, k: (j // per, k, j % per))
    elif b_blocked and mode == "nt":
        per = width // tk
        assert width % tk == 0
        b_spec = pl.BlockSpec((None, tn, tk), lambda i, j, k: (k // per, j, k % per))
    elif mode == "nt":
        b_spec = pl.BlockSpec((tn, tk), lambda i, j, k: (j, k))
    else:
        assert not b_blocked
        b_spec = pl.BlockSpec((tk, tn), lambda i, j, k: (k, j))
    add_spec = pl.BlockSpec((tm, tn), lambda i, j, k: (i, j))
    if out_blocks is not None:
        operb = (n // out_blocks) // tn
        assert (n // out_blocks) % tn == 0 and add is None
        o_spec = pl.BlockSpec((None, tm, tn), lambda i, j, k: (j // operb, i, j % operb))
        out_shape = jax.ShapeDtypeStruct((out_blocks, m, n // out_blocks), out_dtype)
    else:
        o_spec = add_spec
        out_shape = jax.ShapeDtypeStruct((m, n), out_dtype)
    res = hosted_call(
        body, exchange, name=name, grid=(m // tm, n // tn, nk),
        in_specs=[a_spec, b_spec] + ([add_spec] if add is not None else []), out_specs=[o_spec],
        out_shape=[out_shape], scratch_shapes=[pltpu.VMEM((tm, tn), F32)],
        sem=("parallel", "parallel", "arbitrary"),
    )(*([a, b] + ([add] if add is not None else [])))
    return res[0] if exchange is None else res


_HBM = pl.BlockSpec(memory_space=pltpu.HBM)


def _me():
    return lax.axis_index("x"), lax.axis_index("y"), lax.axis_index("c")


def _flat(p):
    return 4 * p[0] + 2 * p[1] + p[2]


def _flip(p, r):
    return tuple((1 - v) if (r >> (2 - a)) & 1 else v for a, v in enumerate(p))


class Exchange:
    def __init__(self, srcs, out_shapes, sems, start, finish, alias=None):
        self.srcs, self.out_shapes, self.sems = list(srcs), list(out_shapes), list(sems)
        self.start, self.finish, self.alias = start, finish, dict(alias or {})


def hosted_call(body, exchange, *, name, grid, in_specs, out_specs, out_shape, scratch_shapes=(),
                input_output_aliases=None, sem=()):
    in_specs, out_specs, out_shape = list(in_specs), list(out_specs), list(out_shape)
    scratch_shapes = list(scratch_shapes)
    aliases = input_output_aliases or {}
    if exchange is None:
        call = pl.pallas_call(body, name=name, grid=grid, in_specs=in_specs, out_specs=out_specs, out_shape=out_shape,
                              scratch_shapes=scratch_shapes, input_output_aliases=aliases, compiler_params=_params(*sem))
        return lambda *args: list(call(*args))
    ni, no, ns = len(in_specs), len(out_specs), len(scratch_shapes)
    xi, xo = len(exchange.srcs), len(exchange.out_shapes)

    def wrapped(*refs):
        ins, refs = refs[:ni], refs[ni:]
        xin, refs = refs[:xi], refs[xi:]
        outs, refs = refs[:no], refs[no:]
        xout, refs = refs[:xo], refs[xo:]
        scr, xsem = refs[:ns], refs[ns:]
        first = functools.reduce(lambda p, q: p & q, [pl.program_id(d) == 0 for d in range(len(grid))])
        last = functools.reduce(lambda p, q: p & q, [pl.program_id(d) == grid[d] - 1 for d in range(len(grid))])

        @pl.when(first)
        def _():
            exchange.start(xin, xout, xsem)

        body(*ins, *outs, *scr)

        @pl.when(last)
        def _():
            exchange.finish(xin, xout, xsem)

    call = pl.pallas_call(
        wrapped, name=name, grid=grid, in_specs=in_specs + [_HBM] * xi, out_specs=out_specs + [_HBM] * xo,
        out_shape=out_shape + exchange.out_shapes, scratch_shapes=scratch_shapes + exchange.sems,
        input_output_aliases={**aliases, **{ni + i: no + j for i, j in exchange.alias.items()}},
        compiler_params=_params(*(("arbitrary",) * len(grid))))
    return lambda *args: list(call(*args, *exchange.srcs))


def run_exchange(exchange, name):
    xi, xo = len(exchange.srcs), len(exchange.out_shapes)

    def body(*refs):
        exchange.start(refs[:xi], refs[xi:xi + xo], refs[xi + xo:])
        exchange.finish(refs[:xi], refs[xi:xi + xo], refs[xi + xo:])

    return list(pl.pallas_call(body, name=name, in_specs=[_HBM] * xi, out_specs=[_HBM] * xo,
                               out_shape=exchange.out_shapes, scratch_shapes=exchange.sems,
                               input_output_aliases=exchange.alias)(*exchange.srcs))


def gather_exchange(shards, rows=None, into=None):
    n = len(shards)

    def parts(srcs, outs, sems):
        send_sems, recv_sems, local_sems = sems
        me = _me()
        x, y, c = me
        chips = [(1 - x, y), (x, 1 - y), (1 - x, 1 - y)]

        def place(a, block):
            dst = outs[a].at[_flat(block)]
            return dst if rows is None else dst.at[pl.ds(rows[0], shards[a].shape[0])]

        def copy(a, k, block, to, src=None):
            dst = place(a, block)
            return pltpu.make_async_remote_copy(src_ref=dst if src is None else src, dst_ref=dst,
                                                send_sem=send_sems.at[a, k], recv_sem=recv_sems.at[a, k],
                                                device_id=to, device_id_type=MESH)

        mine = [pltpu.make_async_copy(srcs[a], place(a, me), local_sems.at[a]) for a in range(n)]
        own = []
        for a in range(n):
            own.append(copy(a, 0, me, (x, y, 1 - c), src=srcs[a]))
            own += [copy(a, 1 + j, me, (*chip, c), src=srcs[a]) for j, chip in enumerate(chips)]
        return me, chips, copy, mine, own

    def start(srcs, outs, sems):
        _, _, _, mine, own = parts(srcs, outs, sems)
        for cp in mine + own:
            cp.start()

    def finish(srcs, outs, sems):
        me, chips, copy, mine, own = parts(srcs, outs, sems)
        x, y, c = me
        passed = []
        for j, chip in enumerate(chips):
            for a in range(n):
                copy(a, 1 + j, (*chip, c), me).wait_recv()
                fwd = copy(a, 4 + j, (*chip, c), (x, y, 1 - c))
                fwd.start()
                passed.append(fwd)
        for a in range(n):
            copy(a, 0, (x, y, 1 - c), me).wait_recv()
            for j, chip in enumerate(chips):
                copy(a, 4 + j, (*chip, 1 - c), me).wait_recv()
        for cp in own + passed:
            cp.wait_send()
        for cp in mine:
            cp.wait()

    dma = pltpu.SemaphoreType.DMA
    full = lambda s: s.shape if rows is None else (rows[1],) + s.shape[1:]
    return Exchange(list(shards) + list(into or []), [jax.ShapeDtypeStruct((N_DEV,) + full(s), s.dtype) for s in shards],
                    [dma((n, 7)), dma((n, 7)), dma((n,))], start, finish,
                    alias={n + a: a for a in range(n)} if into else None)


def pair_exchange(srcs):
    n = len(srcs)

    def copies(srcs_, outs, sems):
        send_sems, recv_sems = sems
        x, y, c = _me()
        return [pltpu.make_async_remote_copy(src_ref=srcs_[a].at[:, 1 - c], dst_ref=outs[a], send_sem=send_sems.at[a],
                                             recv_sem=recv_sems.at[a], device_id=(x, y, 1 - c), device_id_type=MESH)
                for a in range(n)]

    def start(srcs_, outs, sems):
        for cp in copies(srcs_, outs, sems):
            cp.start()

    def finish(srcs_, outs, sems):
        for cp in copies(srcs_, outs, sems):
            cp.wait()

    dma = pltpu.SemaphoreType.DMA
    return Exchange(srcs, [jax.ShapeDtypeStruct((4,) + s.shape[2:], s.dtype) for s in srcs], [dma((n,)), dma((n,))],
                    start, finish)


def pair_sum(src, half, name):
    _, _, rows, cols = src.shape
    tr = min(rows, 256)

    def body(x_ref, h_ref, o_ref):
        c = lax.axis_index("c")
        o_ref[0] = (x_ref[0, c].astype(F32) + h_ref[0].astype(F32)).astype(BF16)

    return pl.pallas_call(
        body, name=name, grid=(4, rows // tr),
        in_specs=[pl.BlockSpec((1, 2, tr, cols), lambda ch, t: (ch, 0, t, 0)),
                  pl.BlockSpec((1, tr, cols), lambda ch, t: (ch, t, 0))],
        out_specs=pl.BlockSpec((1, tr, cols), lambda ch, t: (ch, t, 0)),
        out_shape=jax.ShapeDtypeStruct(half.shape, BF16), compiler_params=_params("parallel", "parallel"),
    )(src, half)


def chip_exchange(parts, slots, recv_shapes):
    n = len(parts)

    def plan(srcs, outs, sems):
        send_sems, recv_sems, local_sems = sems
        x, y, c = _me()
        chip = 2 * x + y
        mine = [pltpu.make_async_copy(srcs[a].at[chip], outs[slots[a][0]].at[chip, slots[a][1]], local_sems.at[a])
                for a in range(n)]
        sends, arrivals = [], []
        for r in (1, 2, 3):
            px = (1 - x) if r & 2 else x
            py = (1 - y) if r & 1 else y
            for a in range(n):
                ri, layer = slots[a]
                sends.append(pltpu.make_async_remote_copy(
                    src_ref=srcs[a].at[2 * px + py], dst_ref=outs[ri].at[chip, layer], send_sem=send_sems.at[a, r - 1],
                    recv_sem=recv_sems.at[a, r - 1], device_id=(px, py, c), device_id_type=MESH))
                land = outs[ri].at[2 * px + py, layer]
                arrivals.append(pltpu.make_async_remote_copy(
                    src_ref=land, dst_ref=land, send_sem=send_sems.at[a, r - 1], recv_sem=recv_sems.at[a, r - 1],
                    device_id=(px, py, c), device_id_type=MESH))
        return mine, sends, arrivals

    def start(srcs, outs, sems):
        mine, sends, _ = plan(srcs, outs, sems)
        for cp in mine + sends:
            cp.start()

    def finish(srcs, outs, sems):
        mine, sends, arrivals = plan(srcs, outs, sems)
        for cp in arrivals:
            cp.wait_recv()
        for cp in sends:
            cp.wait_send()
        for cp in mine:
            cp.wait()

    dma = pltpu.SemaphoreType.DMA
    return Exchange(parts, [jax.ShapeDtypeStruct(s, BF16) for s in recv_shapes],
                    [dma((n, 3)), dma((n, 3)), dma((n,))], start, finish)


SMALL_ROWS = 24


def all_reduce_small(pack):
    def body(p_ref, o_ref, buf, send_sems, recv_sems):
        me = _me()
        buf[_flat(me)] = p_ref[...]
        sent = []
        for r in range(1, N_DEV):
            peer = _flip(me, r)
            cp = pltpu.make_async_remote_copy(src_ref=p_ref, dst_ref=buf.at[_flat(me)], send_sem=send_sems.at[r - 1],
                                              recv_sem=recv_sems.at[r - 1], device_id=peer, device_id_type=MESH)
            cp.start()
            sent.append(cp)
        for r in range(1, N_DEV):
            peer = _flip(me, r)
            land = buf.at[_flat(peer)]
            pltpu.make_async_remote_copy(src_ref=land, dst_ref=land, send_sem=send_sems.at[r - 1],
                                         recv_sem=recv_sems.at[r - 1], device_id=peer, device_id_type=MESH).wait_recv()
        for cp in sent:
            cp.wait_send()
        acc = buf[0]
        for d in range(1, N_DEV):
            acc = acc + buf[d]
        o_ref[...] = acc

    vm = pl.BlockSpec(memory_space=pltpu.VMEM)
    return pl.pallas_call(
        body, name="all_reduce_small", in_specs=[vm], out_specs=vm,
        out_shape=jax.ShapeDtypeStruct(pack.shape, F32),
        scratch_shapes=[pltpu.VMEM((N_DEV,) + pack.shape, F32), pltpu.SemaphoreType.DMA((7,)),
                        pltpu.SemaphoreType.DMA((7,))],
    )(pack)


def _adamw(w, g, m, v):
    m = ADAM_B1 * m + (1.0 - ADAM_B1) * g
    v = ADAM_B2 * v + (1.0 - ADAM_B2) * (g * g)
    m_hat = m / (1.0 - ADAM_B1 ** ADAM_STEP)
    v_hat = v / (1.0 - ADAM_B2 ** ADAM_STEP)
    delta = -ADAM_LR * (m_hat / (jnp.sqrt(v_hat) + ADAM_EPS) + ADAM_WD * w)
    return delta, m, v


ADAM_ROWS = 128


def reduce_adamw(recv, w, m, v, name):
    nl, rows, cols = w.shape
    nslot, cp = recv.shape[0], recv.shape[3]

    def body(r_ref, w_ref, m_ref, v_ref, g_ref, d_ref, mo_ref, vo_ref):
        g = r_ref[0, 0].astype(F32)
        for slot in range(1, nslot):
            g = g + r_ref[slot, 0].astype(F32)
        if cp != cols:
            g = g[:, :cols]
        delta, m_new, v_new = _adamw(w_ref[0], g, m_ref[0], v_ref[0])
        g_ref[0] = g
        d_ref[0] = delta
        mo_ref[0] = m_new
        vo_ref[0] = v_new

    tile = pl.BlockSpec((1, ADAM_ROWS, cols), lambda l, t: (l, t, 0))
    out = jax.ShapeDtypeStruct(w.shape, F32)
    return pl.pallas_call(
        body, name=name, grid=(nl, rows // ADAM_ROWS),
        in_specs=[pl.BlockSpec((nslot, 1, ADAM_ROWS, cp), lambda l, t: (0, l, t, 0)), tile, tile, tile],
        out_specs=[tile, tile, tile, tile], out_shape=[out, out, out, out],
        compiler_params=_params("parallel", "parallel"),
    )(recv, w, m, v)


def reduce_adamw_t(recv, w_t, m_t, v_t, name):
    cols, rows = w_t.shape
    nslot, cp = recv.shape[0], recv.shape[3]
    tr = 256

    def body(r_ref, w_ref, m_ref, v_ref, g_ref, d_ref, mo_ref, vo_ref):
        g = r_ref[0, 0].astype(F32)
        for slot in range(1, nslot):
            g = g + r_ref[slot, 0].astype(F32)
        g_ref[...] = g.T[:cols, :]
        delta, m_new, v_new = _adamw(w_ref[...], g_ref[...], m_ref[...], v_ref[...])
        d_ref[...] = delta
        mo_ref[...] = m_new
        vo_ref[...] = v_new

    tile = pl.BlockSpec((cols, tr), lambda t: (0, t))
    out = jax.ShapeDtypeStruct((cols, rows), F32)
    return pl.pallas_call(
        body, name=name, grid=(rows // tr,),
        in_specs=[pl.BlockSpec((nslot, 1, tr, cp), lambda t: (0, 0, t, 0)), tile, tile, tile],
        out_specs=[tile, tile, tile, tile], out_shape=[out, out, out, out], compiler_params=_params("parallel"),
    )(recv, w_t, m_t, v_t)


def adamw_small(g, w, m, v):
    def body(g_ref, w_ref, m_ref, v_ref, d_ref, mo_ref, vo_ref):
        d_ref[...], mo_ref[...], vo_ref[...] = _adamw(w_ref[...], g_ref[...], m_ref[...], v_ref[...])

    out = jax.ShapeDtypeStruct(g.shape, F32)
    return pl.pallas_call(body, name="adamw_small", out_shape=[out, out, out])(g, w, m, v)


def _row8(v):
    return jnp.pad(v.reshape(1, -1).astype(F32), ((0, 7), (0, 0)))


def _row8_lanes(v, width=LANE):
    return jnp.pad(v.reshape(1, -1).astype(F32), ((0, 7), (0, width - v.size)))


def _pack_rows(parts):
    rows = []
    for p in parts:
        p = p.reshape(-1).astype(F32)
        nrow = -(-p.size // D_MODEL)
        rows.append(jnp.pad(p, (0, nrow * D_MODEL - p.size)).reshape(nrow, D_MODEL))
    out = jnp.concatenate(rows, axis=0)
    return jnp.pad(out, ((0, SMALL_ROWS - out.shape[0]), (0, 0)))


def _unpack_rows(pack, shapes):
    out, r = [], 0
    for shp in shapes:
        size = 1
        for d in shp:
            size *= d
        nrow = -(-size // D_MODEL)
        out.append(pack[r:r + nrow].reshape(-1)[:size].reshape(shp))
        r += nrow
    return out


def _dn_weight_layout(gathered):
    split = DN_QKV + 2 * DN_V_HEADS
    pieces = []
    for d in range(N_DEV):
        lo, hi = d * DN_SHARD, (d + 1) * DN_SHARD
        if lo < split < hi:
            pieces += [gathered[d, :, :split - lo], jnp.zeros((D_MODEL, DN_COLS - DN_PROJ), gathered.dtype),
                       gathered[d, :, split - lo:DN_SHARD]]
        else:
            pieces.append(gathered[d, :, :DN_SHARD])
    return jnp.concatenate(pieces, axis=1)


def _dn_grad_blocks(dw):
    split = DN_QKV + 2 * DN_V_HEADS
    gap = DN_COLS - DN_PROJ
    local = lambda c: c if c <= split else c + gap
    zeros = jnp.zeros((D_MODEL, DN_SHARD_PAD - DN_SHARD), dw.dtype)
    blocks = []
    for d in range(N_DEV):
        lo, hi = d * DN_SHARD, (d + 1) * DN_SHARD
        if lo < split < hi:
            parts = [dw[:, lo:split], dw[:, split + gap:hi + gap]]
        else:
            parts = [dw[:, local(lo):local(lo) + DN_SHARD]]
        blocks.append(jnp.concatenate(parts + [zeros], axis=1))
    return jnp.stack(blocks)


def kernel(x, mem, norm_g, mem_norm_g, mem_w_kv, xa_q_norm_g, xa_k_norm_g, w_out, dn_w_in, dn_conv_w, dn_a_log, dn_dt_bias, dn_out_norm_g, sb_w_in, sb_q_norm_g, sb_k_norm_g, loss_target, m_norm_g, m_mem_norm_g, m_mem_w_kv, m_xa_q_norm_g, m_xa_k_norm_g, m_w_out, m_dn_w_in, m_dn_conv_w, m_dn_a_log, m_dn_dt_bias, m_dn_out_norm_g, m_sb_w_in, m_sb_q_norm_g, m_sb_k_norm_g, v_norm_g, v_mem_norm_g, v_mem_w_kv, v_xa_q_norm_g, v_xa_k_norm_g, v_w_out, v_dn_w_in, v_dn_conv_w, v_dn_a_log, v_dn_dt_bias, v_dn_out_norm_g, v_sb_w_in, v_sb_q_norm_g, v_sb_k_norm_g):
    x0, memv, target = x[0], mem[0], loss_target[0]
    my_dev = 4 * lax.axis_index("x") + 2 * lax.axis_index("y") + lax.axis_index("c")

    dn_shard = jnp.pad(dn_w_in[0].astype(BF16), ((0, 0), (0, DN_SHARD_PAD - DN_SHARD)))
    w_out_b = [w_out[i].astype(BF16) for i in range(2)]
    w_kv_b = [mem_w_kv[i].astype(BF16) for i in range(2)]
    conv_shard = jnp.pad(dn_conv_w[0], ((0, 4), (0, 0)))
    g_dn, g_conv = run_exchange(gather_exchange([dn_shard, conv_shard]), "gather_first")
    w_dn = _dn_weight_layout(g_dn)
    conv_w = jnp.transpose(g_conv, (1, 0, 2)).reshape(8, DN_QKV)

    ng = [_row8(norm_g[0]), _row8(norm_g[1])]
    mem_g = _row8(mem_norm_g)
    xqg = [_row8(xa_q_norm_g[0]), _row8(xa_q_norm_g[1])]
    xkg = [_row8(xa_k_norm_g[0]), _row8(xa_k_norm_g[1])]
    alog, dtb = _row8_lanes(dn_a_log[0]), _row8_lanes(dn_dt_bias[0])
    out_g, sbq_g, sbk_g = _row8(dn_out_norm_g[0]), _row8(sb_q_norm_g[0]), _row8(sb_k_norm_g[0])

    mem_n = rms_fwd(memv, mem_g)
    h0 = rms_fwd(x0, ng[0])
    proj0, g_wo0 = matmul(h0, w_dn, "nn", F32, 1024, 1152, 2048, "proj_dn", exchange=gather_exchange([w_out_b[0]]))
    act, g_kv0 = dn_conv_fwd(proj0, conv_w, exchange=gather_exchange([w_kv_b[0]]))
    sb_shard, half = sb_w_in[0].astype(BF16), 9 * D_MODEL // 16
    u0, w_, qd, kd, qk, cd, w_sb = dn_prep_fwd(
        act, proj0, alog, dtb, exchange=gather_exchange([sb_shard[:half]], rows=(0, D_MODEL)))
    o_raw, states, w_sb = dn_scan_fwd(
        u0, w_, qd, kd, qk, cd, MIX_WIDTH,
        exchange=gather_exchange([sb_shard[half:]], rows=(half, D_MODEL), into=[w_sb]))
    w_o = [g_wo0.reshape(INNER, D_MODEL), None]
    w_kv = [g_kv0.reshape(D_MODEL, 2 * XA_WIDTH), None]
    kv = [matmul(mem_n, w_kv[0], "nn", F32, 256, 1024, 2048, "kv0"), None]
    cat0 = head_norm_fwd(o_raw, 0, DN_V_HEADS, out_g[None], F32, INNER, "dn_out_norm")
    cat0 = xa_fwd(proj0, DN_XQ_BLK, kv[0], xqg[0], xkg[0], cat0)
    y0 = gate_fwd(cat0, proj0, DN_Z_BLK)
    x1 = matmul(y0, w_o[0], "nn", F32, 1024, 1024, 2048, "out_proj0", add=x0)

    h1 = rms_fwd(x1, ng[1])
    proj1 = matmul(h1, w_sb, "nn", F32, 1024, 896, 2048, "proj_sb", b_blocked=True)
    qkn = head_norm_fwd(proj1, 0, 2 * SB_HEADS, jnp.stack([sbq_g, sbk_g]), BF16, 2 * MIX_WIDTH, "sb_qk_norm")
    cat1, ltot, g_wo1, g_kv1 = sb_fwd(qkn, proj1, 2 * SB_HEADS, INNER,
                                      exchange=gather_exchange([w_out_b[1], w_kv_b[1]]))
    w_o[1] = g_wo1.reshape(INNER, D_MODEL)
    w_kv[1] = g_kv1.reshape(D_MODEL, 2 * XA_WIDTH)
    kv[1] = matmul(mem_n, w_kv[1], "nn", F32, 256, 1024, 2048, "kv1")
    cat1 = xa_fwd(proj1, SB_XQ_BLK, kv[1], xqg[1], xkg[1], cat1)
    y1 = gate_fwd(cat1, proj1, SB_Z_BLK)
    x2 = matmul(y1, w_o[1], "nn", F32, 1024, 1024, 2048, "out_proj1", add=x1)
    loss_part, dx2 = loss_fwd_bwd(x2, target)

    dy1 = matmul(dx2, w_o[1], "nt", F32, 1024, 1024, 2048, "d_y1")
    dw_o1 = matmul(y1, dx2, "tn", BF16, 1024, 1024, 1024, "d_w_out1")
    dcat1, dproj1 = gate_bwd(dy1, cat1, proj1, SB_Z_BLK)
    dproj1, dxk1, dxv1, dxqg1, dxkg1 = xa_bwd(proj1, SB_XQ_BLK, kv[1], xqg[1], xkg[1], dcat1, dproj1)
    dqn, dkn, dproj1 = sb_bwd(qkn, proj1, 2 * SB_HEADS, dcat1, ltot, dproj1)
    dproj1, d_sbq = head_norm_bwd(dqn, 0, proj1, 0, SB_HEADS, sbq_g, dproj1, 0, "sb_q_norm_bwd")
    dproj1, d_sbk = head_norm_bwd(dkn, 0, proj1, SB_HEADS, SB_HEADS, sbk_g, dproj1, SB_HEADS, "sb_k_norm_bwd")
    dw_sb = matmul(h1, dproj1, "tn", BF16, 1024, 896, 1024, "d_w_sb", out_blocks=N_DEV)
    dh1 = matmul(dproj1, w_sb, "nt", F32, 1024, 1024, 896, "d_h1", b_blocked=True)
    dx1, d_ng1 = rms_bwd(dh1, x1, ng[1], dx2)

    dy0 = matmul(dx1, w_o[0], "nt", F32, 1024, 1024, 2048, "d_y0")
    dw_o0 = matmul(y0, dx1, "tn", BF16, 1024, 1024, 1024, "d_w_out0")
    dcat0, dproj0 = gate_bwd(dy0, cat0, proj0, DN_Z_BLK)
    dproj0, dxk0, dxv0, dxqg0, dxkg0 = xa_bwd(proj0, DN_XQ_BLK, kv[0], xqg[0], xkg[0], dcat0, dproj0)

    dkv = [jnp.concatenate([dxk0, dxv0], axis=1), jnp.concatenate([dxk1, dxv1], axis=1)]
    dw_kv = [matmul(mem_n, dkv[i], "tn", BF16, 1024, 1024, 256, f"d_w_kv{i}") for i in range(2)]
    dmem_n = matmul(dkv[0], w_kv[0], "nt", F32, 256, 1024, 2048, "d_mem_n0")
    dmem_n = matmul(dkv[1], w_kv[1], "nt", F32, 256, 1024, 2048, "d_mem_n1", add=dmem_n)
    _, d_memg = rms_bwd(dmem_n, memv, mem_g, None)

    by_owner = lambda g, rows: g.reshape(4, 2, rows, g.size // (N_DEV * rows))
    grads = [dw_sb.reshape(4, 2, D_MODEL, SB_PROJ // N_DEV), by_owner(dw_o0, INNER // N_DEV),
             by_owner(dw_o1, INNER // N_DEV), by_owner(dw_kv[0], D_MODEL // N_DEV), by_owner(dw_kv[1], D_MODEL // N_DEV)]
    do_raw, d_outg = head_norm_bwd(dcat0, 0, o_raw, 0, DN_V_HEADS, out_g, None, 0, "dn_out_norm_bwd")
    du0, dw_, dqd, dkd, dqk, dcd, *halves = dn_scan_bwd(do_raw, u0, w_, qd, kd, qk, cd, states,
                                                         exchange=pair_exchange(grads))
    sums = [pair_sum(g, h, f"pair_sum{i}") for i, (g, h) in enumerate(zip(grads, halves))]
    to_chips = chip_exchange(sums, [(0, 0), (1, 0), (1, 1), (2, 0), (2, 1)],
                             [(4, 1, D_MODEL, SB_PROJ // N_DEV), (4, 2, INNER // N_DEV, D_MODEL),
                              (4, 2, D_MODEL // N_DEV, 2 * XA_WIDTH)])
    dq_a, dk_a, dv_a, dab, d_alog, d_dtb, recv_sb, recv_wo, recv_kv = dn_prep_bwd(
        act, proj0, alog, dtb, du0, dw_, dqd, dkd, dqk, dcd, exchange=to_chips)
    dproj0, dcw_q = dn_conv_bwd(dq_a, proj0, conv_w, dproj0, None, 0, "dn_conv_bwd_q")
    dproj0, dcw_k = dn_conv_bwd(dk_a, proj0, conv_w, dproj0, None, DN_QK_HEADS, "dn_conv_bwd_k")
    dproj0, dcw_v = dn_conv_bwd(dv_a, proj0, conv_w, dproj0, dab, 2 * DN_QK_HEADS, "dn_conv_bwd_v")
    dw_dn = matmul(h0, dproj0, "tn", BF16, 1024, 1152, 1024, "d_w_dn")
    dn_grad = _dn_grad_blocks(dw_dn).reshape(4, 2, D_MODEL, DN_SHARD_PAD)
    dn_half, = run_exchange(pair_exchange([dn_grad]), "pair_dn")
    dn_sum = pair_sum(dn_grad, dn_half, "pair_sum_dn")
    dh0, recv_dn = matmul(dproj0, w_dn, "nt", F32, 1024, 1024, 1152, "d_h0",
                          exchange=chip_exchange([dn_sum], [(0, 0)], [(4, 1, D_MODEL, DN_SHARD_PAD)]))
    grad_x, d_ng0 = rms_bwd(dh0, x0, ng[0], dx1)

    big = {
        "dn_w_in": tuple(jnp.transpose(a)[None] for a in reduce_adamw_t(
            recv_dn, jnp.transpose(dn_w_in[0]), jnp.transpose(m_dn_w_in[0]), jnp.transpose(v_dn_w_in[0]),
            "adamw_dn_w_in")),
        "sb_w_in": reduce_adamw(recv_sb, sb_w_in, m_sb_w_in, v_sb_w_in, "adamw_sb_w_in"),
        "w_out": reduce_adamw(recv_wo, w_out, m_w_out, v_w_out, "adamw_w_out"),
        "mem_w_kv": reduce_adamw(recv_kv, mem_w_kv, m_mem_w_kv, v_mem_w_kv, "adamw_mem_w_kv"),
    }

    dconv = jnp.concatenate([dcw_q, dcw_k, dcw_v], axis=1)[:4]
    small_shapes = [(2, D_MODEL), (D_MODEL,), (2, XA_DIM), (2, XA_DIM), (4, DN_QKV), (1, DN_V_HEADS),
                    (1, DN_V_HEADS), (1, HEAD_DIM), (1, HEAD_DIM), (1, HEAD_DIM), (1,)]
    pack = _pack_rows([jnp.stack([d_ng0[0], d_ng1[0]]), d_memg[0], jnp.stack([dxqg0[0], dxqg1[0]]),
                       jnp.stack([dxkg0[0], dxkg1[0]]), dconv, d_alog[0, :DN_V_HEADS], d_dtb[0, :DN_V_HEADS],
                       d_outg[0], d_sbq[0], d_sbk[0], loss_part[0, :1]])
    total = all_reduce_small(pack)
    (g_norm, g_memn, g_xq, g_xk, g_conv_full, g_alog, g_dtb, g_outn, g_sbq, g_sbk, loss1) = _unpack_rows(total, small_shapes)
    conv_cols = DN_QKV // N_DEV
    g_conv = lax.dynamic_slice(g_conv_full, (0, my_dev * conv_cols), (4, conv_cols))[None]
    names = ["norm_g", "mem_norm_g", "xa_q_norm_g", "xa_k_norm_g", "dn_conv_w", "dn_a_log", "dn_dt_bias",
             "dn_out_norm_g", "sb_q_norm_g", "sb_k_norm_g"]
    grads = [g_norm, g_memn, g_xq, g_xk, g_conv, g_alog, g_dtb, g_outn, g_sbq, g_sbk]
    ws = [norm_g, mem_norm_g, xa_q_norm_g, xa_k_norm_g, dn_conv_w, dn_a_log, dn_dt_bias, dn_out_norm_g, sb_q_norm_g,
          sb_k_norm_g]
    ms = [m_norm_g, m_mem_norm_g, m_xa_q_norm_g, m_xa_k_norm_g, m_dn_conv_w, m_dn_a_log, m_dn_dt_bias,
          m_dn_out_norm_g, m_sb_q_norm_g, m_sb_k_norm_g]
    vs = [v_norm_g, v_mem_norm_g, v_xa_q_norm_g, v_xa_k_norm_g, v_dn_conv_w, v_dn_a_log, v_dn_dt_bias,
          v_dn_out_norm_g, v_sb_q_norm_g, v_sb_k_norm_g]
    shapes = [w.shape for w in ws]
    d_p, m_p, v_p = adamw_small(_pack_rows(grads), _pack_rows(ws), _pack_rows(ms), _pack_rows(vs))
    small = dict(zip(names, zip(grads, _unpack_rows(d_p, shapes), _unpack_rows(m_p, shapes), _unpack_rows(v_p, shapes))))

    order = ["norm_g", "mem_norm_g", "mem_w_kv", "xa_q_norm_g", "xa_k_norm_g", "w_out", "dn_w_in", "dn_conv_w",
             "dn_a_log", "dn_dt_bias", "dn_out_norm_g", "sb_w_in", "sb_q_norm_g", "sb_k_norm_g"]
    res = {**big, **small}
    outs = [loss1.reshape(()), grad_x[None]]
    for k in range(4):
        outs += [res[n][k] for n in order]
    return tuple(outs)
```

```python
import functools

import jax
import jax.numpy as jnp
from jax import lax
from jax.experimental import pallas as pl
from jax.experimental.pallas import tpu as pltpu

F32 = jnp.float32
BF16 = jnp.bfloat16

D_MODEL = 2048
SEQ = 2048
N_MEM = 256
INNER = 4096
XA_HEADS = 4
XA_WIDTH = 1024
XA_DIM = 256
MIX_WIDTH = 3072
HEAD_DIM = 128
DN_V_HEADS = 24
DN_QK_HEADS = 12
DN_QK_WIDTH = 1536
DN_CHUNK = 64
DN_QKV = 2 * DN_QK_WIDTH + MIX_WIDTH
DN_PROJ = 11312
SB_HEADS = 24
SB_PROJ = 14336
EPS = 1e-6
N_DEV = 8
DN_SHARD = DN_PROJ // N_DEV
DN_SHARD_PAD = 1536
LANE = 128
DN_COLS = 90 * LANE
DN_AB_BLK, DN_PAD_BLK, DN_XQ_BLK, DN_Z_BLK = 48, 49, 50, 58
SB_XQ_BLK, SB_Z_BLK = 72, 80

ADAM_LR, ADAM_B1, ADAM_B2, ADAM_EPS, ADAM_WD, ADAM_STEP = 0.001, 0.9, 0.999, 1e-08, 0.01, 10

VMEM_LIMIT = 56 * 1024 * 1024
MESH = pl.DeviceIdType.MESH

_NN, _NT, _TN = "nn", "nt", "tn"


def _dims(mode, rank):
    lhs, rhs = {"nn": (1, 0), "nt": (1, 1), "tn": (0, 0)}[mode]
    if rank == 2:
        return (((lhs,), (rhs,)), ((), ()))
    return (((lhs + 1,), (rhs + 1,)), ((0,), (0,)))


def _params(*sem):
    return pltpu.CompilerParams(dimension_semantics=sem if sem else None, vmem_limit_bytes=VMEM_LIMIT)


def _dot(a, b, mode):
    return lax.dot_general(a, b, _dims(mode, a.ndim), preferred_element_type=F32)


def _dg(a, b, dims):
    return _dot(a.astype(BF16), b.astype(BF16), dims)


@jax.custom_vjp
def mm(a, b):
    return _dg(a, b, _NN)


@jax.custom_vjp
def mm_nt(a, b):
    return _dg(a, b, _NT)


@jax.custom_vjp
def mm_tn(a, b):
    return _dg(a, b, _TN)


mm.defvjp(lambda a, b: (_dg(a, b, _NN), (a, b)), lambda r, g: (mm_nt(g, r[1]), mm_tn(r[0], g)))
mm_nt.defvjp(lambda a, b: (_dg(a, b, _NT), (a, b)), lambda r, g: (mm(g, r[1]), mm_tn(g, r[0])))
mm_tn.defvjp(lambda a, b: (_dg(a, b, _TN), (a, b)), lambda r, g: (mm_nt(r[1], g), mm(r[0], g)))


def _split3(x):
    hi = x.astype(BF16)
    r1 = x - hi.astype(F32)
    mid = r1.astype(BF16)
    lo = (r1 - mid.astype(F32)).astype(BF16)
    return hi, mid, lo


def _dg3(a, b, dims):
    ah, am, _ = _split3(a)
    bh, bm, _ = _split3(b)
    return _dot(ah, bh, dims) + (_dot(ah, bm, dims) + _dot(am, bh, dims))


def _dg_exact_rhs(a, b01, dims):
    ah, am, _ = _split3(a)
    b = b01.astype(BF16)
    return _dot(ah, b, dims) + _dot(am, b, dims)


def _dg_exact_lhs(a01, b, dims):
    bh, bm, bl = _split3(b)
    a = a01.astype(BF16)
    return _dot(a, bh, dims) + (_dot(a, bm, dims) + _dot(a, bl, dims))


@jax.custom_vjp
def mm3(a, b):
    return _dg3(a, b, _NN)


@jax.custom_vjp
def mm3_nt(a, b):
    return _dg3(a, b, _NT)


@jax.custom_vjp
def mm3_tn(a, b):
    return _dg3(a, b, _TN)


mm3.defvjp(lambda a, b: (_dg3(a, b, _NN), (a, b)), lambda r, g: (mm3_nt(g, r[1]), mm3_tn(r[0], g)))
mm3_nt.defvjp(lambda a, b: (_dg3(a, b, _NT), (a, b)), lambda r, g: (mm3(g, r[1]), mm3_tn(g, r[0])))
mm3_tn.defvjp(lambda a, b: (_dg3(a, b, _TN), (a, b)), lambda r, g: (mm3_nt(r[1], g), mm3(r[0], g)))


def _softplus(x):
    return jnp.maximum(x, 0.0) + jnp.log(1.0 + jnp.exp(-jnp.abs(x)))


def _log_sigmoid(x):
    return jnp.minimum(x, 0.0) - jnp.log(1.0 + jnp.exp(-jnp.abs(x)))


def _sigmoid(x):
    return 1.0 / (1.0 + jnp.exp(-x))


def _silu(x):
    return x * _sigmoid(x)


def _silu_grad(x):
    s = _sigmoid(x)
    return s * (1.0 + x * (1.0 - s))


@jax.custom_vjp
def mm01(a01, b):
    return _dg_exact_lhs(a01, b, _NN)


mm01.defvjp(lambda a, b: (_dg_exact_lhs(a, b, _NN), a),
            lambda a, g: (jnp.zeros_like(a), _dg_exact_lhs(a, g, _TN)))


def _lane_pick(x, idx):
    lane = lax.broadcasted_iota(jnp.int32, x.shape, x.ndim - 1)
    return jnp.sum(jnp.where(lane == idx, x, 0.0), axis=-1, keepdims=True)


def _dn_chunk(qt, kt, v, ab, alog, dtb, h):
    B, C = qt.shape[0], DN_CHUNK
    g = -jnp.exp(_lane_pick(alog, h)) * _softplus(_lane_pick(ab, h) + _lane_pick(dtb, h))
    beta = _sigmoid(_lane_pick(ab, h + DN_V_HEADS))
    q = qt * lax.rsqrt(jnp.sum(qt * qt, axis=-1, keepdims=True) + EPS) * (HEAD_DIM ** -0.5)
    k = kt * lax.rsqrt(jnp.sum(kt * kt, axis=-1, keepdims=True) + EPS)
    row = lax.broadcasted_iota(jnp.int32, (B, C, C), 1)
    col = lax.broadcasted_iota(jnp.int32, (B, C, C), 2)
    lower = (row >= col).astype(F32)
    ones = jnp.ones((B, C, C), F32)
    g_wide = jnp.broadcast_to(g, (B, C, LANE))
    g_sq = jnp.broadcast_to(g, (B, C, C))
    gc = mm01(lower, g_wide)
    gc_i = gc[:, :, :C]
    gc_j = mm01(ones, jnp.where(row <= col, g_sq, 0.0))
    g_last = jnp.broadcast_to(gc[:, C - 1:C, :], (B, C, LANE))
    decay = jnp.exp(jnp.where(row >= col, gc_i - gc_j, -1e30))
    eg = jnp.exp(gc)
    kk = mm_nt(k, k)
    a_mat = jnp.where(row > col, jnp.broadcast_to(beta, (B, C, C)) * kk * decay, 0.0)
    eye = (row == col).astype(F32)
    y = -a_mat
    t = eye + y
    for _ in range(5):
        y = mm(y, y)
        t = t + mm(t, y)
    bb = jnp.broadcast_to(beta, (B, C, LANE))
    u0 = mm3(t, v * bb)
    w = mm3(t, k * (bb * eg))
    qk = mm_nt(q, k) * decay
    q_dec = q * eg
    k_dec = k * jnp.exp(g_last - gc)
    cd = jnp.exp(g_last)[:, :8, :]
    return u0, w, qk, q_dec, k_dec, cd


def _shift_rows(x, j, down):
    if j == 0:
        return x
    n = x.shape[0]
    r = lax.broadcasted_iota(jnp.int32, x.shape, 0)
    if down:
        return jnp.where(r >= j, pltpu.roll(x, j, 0), 0.0)
    return jnp.where(r < n - j, pltpu.roll(x, n - j, 0), 0.0)


def dn_conv_fwd(proj, conv_w, exchange=None):
    s = proj.shape[0]

    def body(x_ref, w_ref, o_ref):
        x = x_ref[...]
        w = w_ref[...]
        pre = x * w[3:4, :]
        for j in (1, 2, 3):
            pre = pre + _shift_rows(x, j, True) * w[3 - j:4 - j, :]
        o_ref[...] = _silu(pre)

    return hosted_call(
        body, exchange, name="dn_conv_fwd", grid=(DN_QKV // LANE,),
        in_specs=[pl.BlockSpec((s, LANE), lambda j: (0, j)), pl.BlockSpec((8, LANE), lambda j: (0, j))],
        out_specs=[pl.BlockSpec((s, LANE), lambda j: (0, j))],
        out_shape=[jax.ShapeDtypeStruct((s, DN_QKV), F32)], sem=("parallel",),
    )(proj, conv_w)


def dn_conv_bwd(dact, proj, conv_w, dproj_in, dab, blk0, name):
    s = proj.shape[0]
    nblk = dact.shape[1] // LANE
    extra = 2 if dab is not None else 0

    def body(*refs):
        if dab is not None:
            da_ref, x_ref, w_ref, _, dab_ref, dp_ref, dw_ref = refs
        else:
            da_ref, x_ref, w_ref, _, dp_ref, dw_ref = refs
        j = pl.program_id(0)

        @pl.when(j < nblk)
        def _():
            x = x_ref[...]
            w = w_ref[...]
            xs = [_shift_rows(x, 3 - kk_, True) for kk_ in range(4)]
            pre = xs[0] * w[0:1, :]
            for kk_ in (1, 2, 3):
                pre = pre + xs[kk_] * w[kk_:kk_ + 1, :]
            dpre = da_ref[...] * _silu_grad(pre)
            dx = dpre * w[3:4, :]
            for jj in (1, 2, 3):
                dx = dx + _shift_rows(dpre, jj, False) * w[3 - jj:4 - jj, :]
            dp_ref[...] = dx
            rows = [jnp.sum(dpre * xs[kk_], axis=0, keepdims=True) for kk_ in range(4)]
            dw_ref[...] = jnp.concatenate(rows + [jnp.zeros((4, LANE), F32)], axis=0)

        if dab is not None:
            @pl.when(j == nblk)
            def _():
                dp_ref[...] = dab_ref[...]

            @pl.when(j == nblk + 1)
            def _():
                dp_ref[...] = jnp.zeros_like(dp_ref)

    cl = lambda j: jnp.minimum(j, nblk - 1)
    in_specs = [pl.BlockSpec((s, LANE), lambda j: (0, cl(j))),
                pl.BlockSpec((s, LANE), lambda j: (0, blk0 + cl(j))),
                pl.BlockSpec((8, LANE), lambda j: (0, blk0 + cl(j))),
                pl.BlockSpec(memory_space=pl.ANY)]
    args = [dact, proj, conv_w, dproj_in]
    if dab is not None:
        in_specs.append(pl.BlockSpec((s, LANE), lambda j: (0, 0)))
        args.append(dab)
    return pl.pallas_call(
        body, name=name, grid=(nblk + extra,), in_specs=in_specs,
        out_specs=[pl.BlockSpec((s, LANE), lambda j: (0, blk0 + j)), pl.BlockSpec((8, LANE), lambda j: (0, cl(j)))],
        out_shape=[jax.ShapeDtypeStruct(dproj_in.shape, F32), jax.ShapeDtypeStruct((8, dact.shape[1]), F32)],
        input_output_aliases={3: 0}, compiler_params=_params("arbitrary"),
    )(*args)


DN_GROUP = 8
DN_ROWS = DN_GROUP * DN_CHUNK


def dn_prep_fwd(act, proj, alog, dtb, exchange=None):
    s = act.shape[0]
    nc = s // DN_CHUNK
    C = DN_CHUNK

    def body(q_ref, k_ref, v_ref, ab_ref, al_ref, dt_ref, u0_ref, w_ref, qd_ref, kd_ref, qk_ref, cd_ref):
        qh = pl.program_id(0)
        al = al_ref[0:1, :]
        dt = dt_ref[0:1, :]
        chunks = lambda x: x.reshape(DN_GROUP, C, x.shape[-1])
        rows = lambda x: x.reshape(DN_ROWS, x.shape[-1])
        qt, kt, ab = chunks(q_ref[...]), chunks(k_ref[...]), chunks(ab_ref[...])
        for hv in range(2):
            cs = slice(hv * LANE, (hv + 1) * LANE)
            u0, w, qk, qd, kd, cd = _dn_chunk(qt, kt, chunks(v_ref[:, cs]), ab, al, dt, 2 * qh + hv)
            u0_ref[:, cs] = rows(u0)
            w_ref[:, cs] = rows(w)
            qd_ref[:, cs] = rows(qd)
            kd_ref[:, cs] = rows(kd)
            qk_ref[hv] = rows(qk)
            cd_ref[hv] = cd

    big = pl.BlockSpec((DN_ROWS, 2 * LANE), lambda h, g: (g, h))
    wide = jax.ShapeDtypeStruct((s, MIX_WIDTH), F32)
    return hosted_call(
        body, exchange, name="dn_prep_fwd", grid=(DN_QK_HEADS, s // DN_ROWS),
        in_specs=[pl.BlockSpec((DN_ROWS, LANE), lambda h, g: (g, h)),
                  pl.BlockSpec((DN_ROWS, LANE), lambda h, g: (g, DN_QK_HEADS + h)),
                  pl.BlockSpec((DN_ROWS, 2 * LANE), lambda h, g: (g, DN_QK_HEADS + h)),
                  pl.BlockSpec((DN_ROWS, LANE), lambda h, g: (g, DN_AB_BLK)),
                  pl.BlockSpec((8, LANE), lambda h, g: (0, 0)), pl.BlockSpec((8, LANE), lambda h, g: (0, 0))],
        out_specs=[big, big, big, big,
                   pl.BlockSpec((2, DN_ROWS, C), lambda h, g: (h, g, 0)),
                   pl.BlockSpec((2, DN_GROUP, 8, LANE), lambda h, g: (h, g, 0, 0))],
        out_shape=[wide, wide, wide, wide, jax.ShapeDtypeStruct((DN_V_HEADS, s, C), F32),
                   jax.ShapeDtypeStruct((DN_V_HEADS, nc, 8, LANE), F32)],
        sem=("parallel", "parallel"),
    )(act, act, act, proj, alog, dtb)


def dn_prep_bwd(act, proj, alog, dtb, du0, dw, dqd, dkd, dqk, dcd, exchange=None):
    s = act.shape[0]
    C = DN_CHUNK

    def body(q_ref, k_ref, v_ref, ab_ref, al_ref, dt_ref, du0_ref, dw_ref, dqd_ref, dkd_ref, dqk_ref, dcd_ref,
             dq_ref, dk_ref, dv_ref, dab_ref, dal_ref, ddt_ref):
        g_id = pl.program_id(0)
        qh = pl.program_id(1)
        al = al_ref[0:1, :]
        dt = dt_ref[0:1, :]

        @pl.when(qh == 0)
        def _():
            dab_ref[...] = jnp.zeros_like(dab_ref)

        @pl.when((qh == 0) & (g_id == 0))
        def _():
            dal_ref[...] = jnp.zeros_like(dal_ref)
            ddt_ref[...] = jnp.zeros_like(ddt_ref)

        chunks = lambda x: x.reshape(DN_GROUP, C, x.shape[-1])
        rows = lambda x: x.reshape(DN_ROWS, x.shape[-1])
        qt, kt, ab = chunks(q_ref[...]), chunks(k_ref[...]), chunks(ab_ref[...])
        dq_acc = jnp.zeros((DN_ROWS, LANE), F32)
        dk_acc = jnp.zeros((DN_ROWS, LANE), F32)
        for hv in range(2):
            cs = slice(hv * LANE, (hv + 1) * LANE)
            h = 2 * qh + hv
            f = lambda qt_, kt_, v_, ab_, a_, d_: _dn_chunk(qt_, kt_, v_, ab_, a_, d_, h)
            _, vjp = jax.vjp(f, qt, kt, chunks(v_ref[:, cs]), ab, al, dt)
            dq, dk, dv, dab, dal, ddt = vjp((chunks(du0_ref[:, cs]), chunks(dw_ref[:, cs]), chunks(dqk_ref[hv]),
                                             chunks(dqd_ref[:, cs]), chunks(dkd_ref[:, cs]), dcd_ref[hv]))
            dq_acc = dq_acc + rows(dq)
            dk_acc = dk_acc + rows(dk)
            dv_ref[:, cs] = rows(dv)
            dab_ref[...] += rows(dab)
            dal_ref[0:1, :] += dal
            ddt_ref[0:1, :] += ddt
        dq_ref[...] = dq_acc
        dk_ref[...] = dk_acc

    big = pl.BlockSpec((DN_ROWS, 2 * LANE), lambda g, h: (g, h))
    one = pl.BlockSpec((DN_ROWS, LANE), lambda g, h: (g, h))
    small = pl.BlockSpec((8, LANE), lambda g, h: (0, 0))
    return hosted_call(
        body, exchange, name="dn_prep_bwd", grid=(s // DN_ROWS, DN_QK_HEADS),
        in_specs=[one, pl.BlockSpec((DN_ROWS, LANE), lambda g, h: (g, DN_QK_HEADS + h)),
                  pl.BlockSpec((DN_ROWS, 2 * LANE), lambda g, h: (g, DN_QK_HEADS + h)),
                  pl.BlockSpec((DN_ROWS, LANE), lambda g, h: (g, DN_AB_BLK)), small, small,
                  big, big, big, big,
                  pl.BlockSpec((2, DN_ROWS, C), lambda g, h: (h, g, 0)),
                  pl.BlockSpec((2, DN_GROUP, 8, LANE), lambda g, h: (h, g, 0, 0))],
        out_specs=[one, one, big, pl.BlockSpec((DN_ROWS, LANE), lambda g, h: (g, 0)), small, small],
        out_shape=[jax.ShapeDtypeStruct((s, DN_QK_WIDTH), F32), jax.ShapeDtypeStruct((s, DN_QK_WIDTH), F32),
                   jax.ShapeDtypeStruct((s, MIX_WIDTH), F32), jax.ShapeDtypeStruct((s, LANE), F32),
                   jax.ShapeDtypeStruct((8, LANE), F32), jax.ShapeDtypeStruct((8, LANE), F32)],
        sem=("arbitrary", "arbitrary"),
    )(act, act, act, proj, alog, dtb, du0, dw, dqd, dkd, dqk, dcd)


DN_SCAN_HEADS = 2


def dn_scan_fwd(u0, w, qd, kd, qk, cd, width, exchange=None):
    s = u0.shape[0]
    nc = s // DN_CHUNK
    C = DN_CHUNK

    nh = DN_SCAN_HEADS
    heads = range(nh)
    cols = [slice(h * LANE, (h + 1) * LANE) for h in heads]

    def body(u0_ref, w_ref, qd_ref, kd_ref, qk_ref, cd_ref, o_ref, st_ref):
        def step(c, states):
            rs = pl.ds(pl.multiple_of(c * C, C), C)
            for h in heads:
                st_ref[h, c] = states[h]
            ws = [_dg(w_ref[rs, cols[h]], states[h], _NN) for h in heads]
            us = [u0_ref[rs, cols[h]] - ws[h] for h in heads]
            os_ = [_dg(qd_ref[rs, cols[h]], states[h], _NN) for h in heads]
            for h in heads:
                o_ref[rs, cols[h]] = os_[h] + _dg(qk_ref[h, rs, :], us[h], _NN)
            return tuple(cd_ref[h, c][0:1, :] * states[h] + _dg(kd_ref[rs, cols[h]], us[h], _TN) for h in heads)

        lax.fori_loop(0, nc, step, tuple(jnp.zeros((HEAD_DIM, HEAD_DIM), F32) for _ in heads))

    col = pl.BlockSpec((s, nh * LANE), lambda h: (0, h))
    return hosted_call(
        body, exchange, name="dn_scan_fwd", grid=(DN_V_HEADS // nh,),
        in_specs=[col, col, col, col, pl.BlockSpec((nh, s, C), lambda h: (h, 0, 0)),
                  pl.BlockSpec((nh, nc, 8, LANE), lambda h: (h, 0, 0, 0))],
        out_specs=[col, pl.BlockSpec((nh, nc, HEAD_DIM, HEAD_DIM), lambda h: (h, 0, 0, 0))],
        out_shape=[jax.ShapeDtypeStruct((s, width), F32),
                   jax.ShapeDtypeStruct((DN_V_HEADS, nc, HEAD_DIM, HEAD_DIM), F32)],
        sem=("parallel",),
    )(u0, w, qd, kd, qk, cd)


def dn_scan_bwd(do, u0, w, qd, kd, qk, cd, states, exchange=None):
    s = u0.shape[0]
    nc = s // DN_CHUNK
    C = DN_CHUNK

    nh = DN_SCAN_HEADS
    heads = range(nh)
    cols = [slice(h * LANE, (h + 1) * LANE) for h in heads]

    def body(do_ref, u0_ref, w_ref, qd_ref, kd_ref, qk_ref, cd_ref, st_ref,
             du0_ref, dw_ref, dqd_ref, dkd_ref, dqk_ref, dcd_ref):
        def step(i, dstates):
            c = nc - 1 - i
            rs = pl.ds(pl.multiple_of(c * C, C), C)
            states = [st_ref[h, c] for h in heads]
            gs = [do_ref[rs, cols[h]] for h in heads]
            w_cs = [w_ref[rs, cols[h]] for h in heads]
            qd_cs = [qd_ref[rs, cols[h]] for h in heads]
            cd_rows = [cd_ref[h, c][0:1, :] for h in heads]
            us = [u0_ref[rs, cols[h]] - _dg(w_cs[h], states[h], _NN) for h in heads]
            dus = [_dg(qk_ref[h, rs, :], gs[h], _TN) + _dg(kd_ref[rs, cols[h]], dstates[h], _NN) for h in heads]
            for h in heads:
                du0_ref[rs, cols[h]] = dus[h]
                dw_ref[rs, cols[h]] = -_dg(dus[h], states[h], _NT)
                dqd_ref[rs, cols[h]] = _dg(gs[h], states[h], _NT)
                dkd_ref[rs, cols[h]] = _dg(us[h], dstates[h], _NT)
                dqk_ref[h, rs, :] = _dg(gs[h], us[h], _NT)
                dcd_row = jnp.sum(states[h] * dstates[h], axis=0, keepdims=True)
                dcd_ref[h, c] = jnp.concatenate([dcd_row, jnp.zeros((7, LANE), F32)], axis=0)
            return tuple(cd_rows[h] * dstates[h] + _dg(qd_cs[h], gs[h], _TN) - _dg(w_cs[h], dus[h], _TN)
                         for h in heads)

        lax.fori_loop(0, nc, step, tuple(jnp.zeros((HEAD_DIM, HEAD_DIM), F32) for _ in heads))

    col = pl.BlockSpec((s, nh * LANE), lambda h: (0, h))
    qk_spec = pl.BlockSpec((nh, s, C), lambda h: (h, 0, 0))
    cd_spec = pl.BlockSpec((nh, nc, 8, LANE), lambda h: (h, 0, 0, 0))
    wide = jax.ShapeDtypeStruct((s, MIX_WIDTH), F32)
    return hosted_call(
        body, exchange, name="dn_scan_bwd", grid=(DN_V_HEADS // nh,),
        in_specs=[col, col, col, col, col, qk_spec, cd_spec,
                  pl.BlockSpec((nh, nc, HEAD_DIM, HEAD_DIM), lambda h: (h, 0, 0, 0))],
        out_specs=[col, col, col, col, qk_spec, cd_spec],
        out_shape=[wide, wide, wide, wide, jax.ShapeDtypeStruct((DN_V_HEADS, s, C), F32),
                   jax.ShapeDtypeStruct((DN_V_HEADS, nc, 8, LANE), F32)],
        sem=("parallel",),
    )(do, u0, w, qd, kd, qk, cd, states)


SB_T = 256
SB_GROUPS = (4, 2, 1)


def _sb_scores(q, kbs, diff, lims):
    zs = [_dg(q, kb, _NT) * (HEAD_DIM ** -0.5) for kb in kbs]
    masks = [diff < lim for lim in lims]
    lss = [_log_sigmoid(z) for z in zs]
    lrs = [jnp.where(m, ls - z, 0.0) for m, ls, z in zip(masks, lss, zs)]
    return masks, lss, lrs


def _sb_diff():
    return lax.broadcasted_iota(jnp.int32, (SB_T, SB_T), 1) - lax.broadcasted_iota(jnp.int32, (SB_T, SB_T), 0)


def _sb_loop(n_tiles, step, carry):
    done = 0
    for size in SB_GROUPS:
        groups = (n_tiles - done) // size
        carry = lax.fori_loop(0, groups, lambda p, c, s=size, d=done: step(d + p * s, s, c), carry)
        done = done + groups * size
    return carry


def sb_fwd(qkn, proj, v_blk0, width, exchange=None):
    s = qkn.shape[0]

    def body(q_ref, k_ref, v_ref, o_ref, lt_ref):
        i = pl.program_id(1)
        q = q_ref[...]
        diff = _sb_diff()
        after = (diff < 0).astype(BF16)

        def step(first, n, carry):
            run, acc = carry
            tiles = [first + t for t in range(n)]
            kss = [pl.ds(pl.multiple_of((i - t) * SB_T, SB_T), SB_T) for t in tiles]
            masks, lss, lrs = _sb_scores(q, [k_ref[ks, :] for ks in kss], diff, [t * SB_T for t in tiles])
            within = [_dg_exact_rhs(lr, after, _NN) for lr in lrs]
            sums = [jnp.sum(lr, axis=1, keepdims=True) for lr in lrs]
            for t in range(n):
                wts = jnp.where(masks[t], jnp.exp(lss[t] + (within[t] + run)), 0.0)
                acc = acc + _dg(wts, v_ref[kss[t], :], _NN)
                run = run + sums[t]
            return run, acc

        run, acc = _sb_loop(i + 1, step, (jnp.zeros((SB_T, 1), F32), jnp.zeros((SB_T, HEAD_DIM), F32)))
        o_ref[...] = acc
        lt_ref[0] = run

    return hosted_call(
        body, exchange, name="sb_fwd", grid=(SB_HEADS, s // SB_T),
        in_specs=[pl.BlockSpec((SB_T, LANE), lambda h, i: (i, h)),
                  pl.BlockSpec((s, LANE), lambda h, i: (0, SB_HEADS + h)),
                  pl.BlockSpec((s, LANE), lambda h, i: (0, v_blk0 + h))],
        out_specs=[pl.BlockSpec((SB_T, LANE), lambda h, i: (i, h)), pl.BlockSpec((1, SB_T, 1), lambda h, i: (h, i, 0))],
        out_shape=[jax.ShapeDtypeStruct((s, width), F32), jax.ShapeDtypeStruct((SB_HEADS, s, 1), F32)],
        sem=("parallel", "parallel"),
    )(qkn, qkn, proj)


def sb_bwd(qkn, proj, v_blk0, do, ltot, dproj_in):
    s = qkn.shape[0]

    def body(q_ref, k_ref, v_ref, do_ref, lt_ref, _, dq_ref, dk_ref, dv_ref):
        i = pl.program_id(1)

        @pl.when(i == 0)
        def _():
            dk_ref[...] = jnp.zeros_like(dk_ref)
            dv_ref[...] = jnp.zeros_like(dv_ref)

        q = q_ref[...]
        g = do_ref[...]
        ltot = lt_ref[0]
        diff = _sb_diff()
        upto = (diff >= 0).astype(BF16)
        before = (diff > 0).astype(BF16)

        def step(first, n, carry):
            plr, pdl, dq = carry
            tiles = [first + t for t in range(n)]
            kss = [pl.ds(pl.multiple_of(j * SB_T, SB_T), SB_T) for j in tiles]
            kbs = [k_ref[ks, :] for ks in kss]
            vbs = [v_ref[ks, :] for ks in kss]
            masks, lss, lrs = _sb_scores(q, kbs, diff, [(i - j) * SB_T for j in tiles])
            dwts = [_dg(g, vb, _NT) for vb in vbs]
            within = [_dg_exact_rhs(lr, upto, _NN) for lr in lrs]
            wtss, dls = [], []
            for t in range(n):
                wts = jnp.where(masks[t], jnp.exp(lss[t] + (ltot - (within[t] + plr))), 0.0)
                plr = plr + jnp.sum(lrs[t], axis=1, keepdims=True)
                wtss.append(wts)
                dls.append(dwts[t] * wts)
            dwithin = [_dg_exact_rhs(dl, before, _NN) for dl in dls]
            for t in range(n):
                sz = jnp.exp(lss[t])
                dz = jnp.where(masks[t], dls[t] * (1.0 - sz) - sz * (dwithin[t] + pdl), 0.0) * (HEAD_DIM ** -0.5)
                pdl = pdl + jnp.sum(dls[t], axis=1, keepdims=True)
                dk_ref[kss[t], :] += _dg(dz, q, _TN)
                dv_ref[kss[t], :] += _dg(wtss[t], g, _TN)
                dq = dq + _dg(dz, kbs[t], _NN)
            return plr, pdl, dq

        zero = jnp.zeros((SB_T, 1), F32)
        _, _, dq = _sb_loop(i + 1, step, (zero, zero, jnp.zeros((SB_T, HEAD_DIM), F32)))
        dq_ref[...] = dq

    tile = pl.BlockSpec((SB_T, LANE), lambda h, i: (i, h))
    colspec = pl.BlockSpec((s, LANE), lambda h, i: (0, h))
    return pl.pallas_call(
        body, name="sb_bwd", grid=(SB_HEADS, s // SB_T),
        in_specs=[tile, pl.BlockSpec((s, LANE), lambda h, i: (0, SB_HEADS + h)),
                  pl.BlockSpec((s, LANE), lambda h, i: (0, v_blk0 + h)), tile,
                  pl.BlockSpec((1, SB_T, 1), lambda h, i: (h, i, 0)), pl.BlockSpec(memory_space=pl.ANY)],
        out_specs=[tile, colspec, pl.BlockSpec((s, LANE), lambda h, i: (0, v_blk0 + h))],
        out_shape=[jax.ShapeDtypeStruct((s, MIX_WIDTH), F32), jax.ShapeDtypeStruct((s, MIX_WIDTH), F32),
                   jax.ShapeDtypeStruct(dproj_in.shape, F32)],
        input_output_aliases={5: 2}, compiler_params=_params("parallel", "arbitrary"),
    )(qkn, qkn, proj, do, ltot, dproj_in)


ROW_TILE = 256
HN_HEADS = 8


def head_norm_fwd(x, x_blk0, nblk, gains, out_dtype, out_width, name):
    s = x.shape[0]
    assert x_blk0 % HN_HEADS == 0 and nblk % (HN_HEADS * gains.shape[0]) == 0
    per = nblk // gains.shape[0] // HN_HEADS
    w = HN_HEADS * LANE

    def body(x_ref, g_ref, o_ref):
        gain = g_ref[0, 0:1, :]
        for j in range(HN_HEADS):
            cs = slice(j * LANE, (j + 1) * LANE)
            xv = x_ref[:, cs]
            r = lax.rsqrt(jnp.mean(xv * xv, axis=1, keepdims=True) + EPS)
            o_ref[:, cs] = (xv * r * gain).astype(out_dtype)

    return pl.pallas_call(
        body, name=name, grid=(nblk // HN_HEADS, s // ROW_TILE),
        in_specs=[pl.BlockSpec((ROW_TILE, w), lambda j, t: (t, x_blk0 // HN_HEADS + j)),
                  pl.BlockSpec((1, 8, LANE), lambda j, t: (j // per, 0, 0))],
        out_specs=pl.BlockSpec((ROW_TILE, w), lambda j, t: (t, j)),
        out_shape=jax.ShapeDtypeStruct((s, out_width), out_dtype), compiler_params=_params("parallel", "parallel"),
    )(x, gains)


def head_norm_bwd(dy, dy_blk0, x, x_blk0, nblk, gain, dst, dst_blk0, name):
    s = x.shape[0]

    def body(*refs):
        if dst is not None:
            dy_ref, x_ref, g_ref, _, dx_ref, dg_ref = refs
        else:
            dy_ref, x_ref, g_ref, dx_ref, dg_ref = refs

        @pl.when((pl.program_id(0) == 0) & (pl.program_id(1) == 0))
        def _():
            dg_ref[...] = jnp.zeros_like(dg_ref)

        gain = g_ref[0:1, :]
        dg = jnp.zeros((1, LANE), F32)
        for j in range(HN_HEADS):
            cs = slice(j * LANE, (j + 1) * LANE)
            xv = x_ref[:, cs]
            g = dy_ref[:, cs]
            r = lax.rsqrt(jnp.mean(xv * xv, axis=1, keepdims=True) + EPS)
            gy = g * gain
            dx_ref[:, cs] = r * gy - xv * (r * r * r) * jnp.mean(gy * xv, axis=1, keepdims=True)
            dg = dg + jnp.sum(g * xv * r, axis=0, keepdims=True)
        dg_ref[0:1, :] += dg

    assert dy_blk0 % HN_HEADS == 0 and x_blk0 % HN_HEADS == 0 and dst_blk0 % HN_HEADS == 0 and nblk % HN_HEADS == 0
    w = HN_HEADS * LANE
    in_specs = [pl.BlockSpec((ROW_TILE, w), lambda j, t: (t, dy_blk0 // HN_HEADS + j)),
                pl.BlockSpec((ROW_TILE, w), lambda j, t: (t, x_blk0 // HN_HEADS + j)),
                pl.BlockSpec((8, LANE), lambda j, t: (0, 0))]
    args = [dy, x, gain]
    aliases = {}
    if dst is not None:
        in_specs.append(pl.BlockSpec(memory_space=pl.ANY))
        args.append(dst)
        aliases = {3: 0}
        out0 = jax.ShapeDtypeStruct(dst.shape, F32)
    else:
        out0 = jax.ShapeDtypeStruct((s, (dst_blk0 + nblk) * LANE), F32)
    return pl.pallas_call(
        body, name=name, grid=(nblk // HN_HEADS, s // ROW_TILE), in_specs=in_specs,
        out_specs=[pl.BlockSpec((ROW_TILE, w), lambda j, t: (t, dst_blk0 // HN_HEADS + j)),
                   pl.BlockSpec((8, LANE), lambda j, t: (0, 0))],
        out_shape=[out0, jax.ShapeDtypeStruct((8, LANE), F32)],
        input_output_aliases=aliases, compiler_params=_params("arbitrary", "arbitrary"),
    )(*args)


def _xa_head(xq, kraw, v, qg, kg):
    q = xq * lax.rsqrt(jnp.mean(xq * xq, axis=1, keepdims=True) + EPS) * qg
    k = kraw * lax.rsqrt(jnp.mean(kraw * kraw, axis=1, keepdims=True) + EPS) * kg
    sc = mm_nt(q, k) * (XA_DIM ** -0.5)
    e = jnp.exp(sc - lax.stop_gradient(jnp.max(sc, axis=1, keepdims=True)))
    return mm(e / jnp.sum(e, axis=1, keepdims=True), v)


def xa_fwd(proj, xq_blk0, kv, qg, kg, cat):
    s = proj.shape[0]
    n_mem = kv.shape[0]

    def body(xq_ref, k_ref, v_ref, qg_ref, kg_ref, _, o_ref):
        o_ref[...] = _xa_head(xq_ref[...], k_ref[...], v_ref[...], qg_ref[0:1, :], kg_ref[0:1, :])

    gain = pl.BlockSpec((8, XA_DIM), lambda h, t: (0, 0))
    return pl.pallas_call(
        body, name="xa_fwd", grid=(XA_HEADS, s // ROW_TILE),
        in_specs=[pl.BlockSpec((ROW_TILE, XA_DIM), lambda h, t: (t, xq_blk0 // 2 + h)),
                  pl.BlockSpec((n_mem, XA_DIM), lambda h, t: (0, h)),
                  pl.BlockSpec((n_mem, XA_DIM), lambda h, t: (0, XA_HEADS + h)), gain, gain,
                  pl.BlockSpec(memory_space=pl.ANY)],
        out_specs=pl.BlockSpec((ROW_TILE, XA_DIM), lambda h, t: (t, MIX_WIDTH // XA_DIM + h)),
        out_shape=jax.ShapeDtypeStruct(cat.shape, F32), input_output_aliases={5: 0},
        compiler_params=_params("parallel", "parallel"),
    )(proj, kv, kv, qg, kg, cat)


def xa_bwd(proj, xq_blk0, kv, qg, kg, dcat, dproj_in):
    s = proj.shape[0]
    n_mem = kv.shape[0]

    def body(xq_ref, k_ref, v_ref, qg_ref, kg_ref, do_ref, _, dxq_ref, dk_ref, dv_ref, dqg_ref, dkg_ref):
        h = pl.program_id(0)
        t = pl.program_id(1)

        @pl.when(t == 0)
        def _():
            dk_ref[...] = jnp.zeros_like(dk_ref)
            dv_ref[...] = jnp.zeros_like(dv_ref)

        @pl.when((t == 0) & (h == 0))
        def _():
            dqg_ref[...] = jnp.zeros_like(dqg_ref)
            dkg_ref[...] = jnp.zeros_like(dkg_ref)

        _, vjp = jax.vjp(_xa_head, xq_ref[...], k_ref[...], v_ref[...], qg_ref[0:1, :], kg_ref[0:1, :])
        dxq, dk, dv, dqg, dkg = vjp(do_ref[...])
        dxq_ref[...] = dxq
        dk_ref[...] += dk
        dv_ref[...] += dv
        dqg_ref[0:1, :] += dqg
        dkg_ref[0:1, :] += dkg

    gain = pl.BlockSpec((8, XA_DIM), lambda h, t: (0, 0))
    kspec = pl.BlockSpec((n_mem, XA_DIM), lambda h, t: (0, h))
    vspec = pl.BlockSpec((n_mem, XA_DIM), lambda h, t: (0, XA_HEADS + h))
    return pl.pallas_call(
        body, name="xa_bwd", grid=(XA_HEADS, s // ROW_TILE),
        in_specs=[pl.BlockSpec((ROW_TILE, XA_DIM), lambda h, t: (t, xq_blk0 // 2 + h)), kspec, vspec, gain, gain,
                  pl.BlockSpec((ROW_TILE, XA_DIM), lambda h, t: (t, MIX_WIDTH // XA_DIM + h)),
                  pl.BlockSpec(memory_space=pl.ANY)],
        out_specs=[pl.BlockSpec((ROW_TILE, XA_DIM), lambda h, t: (t, xq_blk0 // 2 + h)), kspec, kspec, gain, gain],
        out_shape=[jax.ShapeDtypeStruct(dproj_in.shape, F32), jax.ShapeDtypeStruct((n_mem, XA_WIDTH), F32),
                   jax.ShapeDtypeStruct((n_mem, XA_WIDTH), F32), jax.ShapeDtypeStruct((8, XA_DIM), F32),
                   jax.ShapeDtypeStruct((8, XA_DIM), F32)],
        input_output_aliases={6: 0}, compiler_params=_params("arbitrary", "arbitrary"),
    )(proj, kv, kv, qg, kg, dcat, dproj_in)


GATE_ROWS = 1024


def gate_fwd(cat, proj, z_blk0):
    s = cat.shape[0]

    def body(c_ref, z_ref, y_ref):
        y_ref[...] = (c_ref[...] * _silu(z_ref[...])).astype(BF16)

    w = 2 * LANE
    rt = min(GATE_ROWS, s)
    return pl.pallas_call(
        body, name="gate_fwd", grid=(INNER // w, s // rt),
        in_specs=[pl.BlockSpec((rt, w), lambda j, t: (t, j)),
                  pl.BlockSpec((rt, w), lambda j, t: (t, z_blk0 // 2 + j))],
        out_specs=pl.BlockSpec((rt, w), lambda j, t: (t, j)),
        out_shape=jax.ShapeDtypeStruct((s, INNER), BF16), compiler_params=_params("parallel", "parallel"),
    )(cat, proj)


def gate_bwd(dy, cat, proj, z_blk0):
    s = cat.shape[0]

    def body(dy_ref, c_ref, z_ref, dc_ref, dz_ref):
        z = z_ref[...]
        g = dy_ref[...]
        dc_ref[...] = g * _silu(z)
        dz_ref[...] = g * c_ref[...] * _silu_grad(z)

    w = 2 * LANE
    rt = min(GATE_ROWS, s)
    tile = pl.BlockSpec((rt, w), lambda j, t: (t, j))
    ztile = pl.BlockSpec((rt, w), lambda j, t: (t, z_blk0 // 2 + j))
    return pl.pallas_call(
        body, name="gate_bwd", grid=(INNER // w, s // rt), in_specs=[tile, tile, ztile],
        out_specs=[tile, ztile],
        out_shape=[jax.ShapeDtypeStruct((s, INNER), F32), jax.ShapeDtypeStruct(proj.shape, F32)],
        compiler_params=_params("parallel", "parallel"),
    )(dy, cat, proj)


NORM_ROWS = 256


def rms_fwd(x, gain, with_transpose=False):
    s, d = x.shape

    def body(x_ref, g_ref, o_ref, *t_ref):
        xv = x_ref[...]
        r = lax.rsqrt(jnp.mean(xv * xv, axis=1, keepdims=True) + EPS)
        y = xv * r * g_ref[0:1, :]
        o_ref[...] = y.astype(BF16)
        if with_transpose:
            t_ref[0][...] = y.T.astype(BF16)

    tile = pl.BlockSpec((NORM_ROWS, d), lambda t: (t, 0))
    res = pl.pallas_call(
        body, name="rms_fwd", grid=(s // NORM_ROWS,),
        in_specs=[tile, pl.BlockSpec((8, d), lambda t: (0, 0))],
        out_specs=[tile] + ([pl.BlockSpec((d, NORM_ROWS), lambda t: (0, t))] if with_transpose else []),
        out_shape=[jax.ShapeDtypeStruct((s, d), BF16)] + ([jax.ShapeDtypeStruct((d, s), BF16)] if with_transpose else []),
        compiler_params=_params("parallel"),
    )(x, gain)
    return res if with_transpose else res[0]


def rms_bwd(dh, x, gain, dres):
    s, d = x.shape

    def body(*refs):
        if dres is not None:
            dh_ref, x_ref, g_ref, dr_ref, dx_ref, dg_ref = refs
        else:
            dh_ref, x_ref, g_ref, dx_ref, dg_ref = refs

        @pl.when(pl.program_id(0) == 0)
        def _():
            dg_ref[...] = jnp.zeros_like(dg_ref)

        xv = x_ref[...]
        g = dh_ref[...]
        r = lax.rsqrt(jnp.mean(xv * xv, axis=1, keepdims=True) + EPS)
        gy = g * g_ref[0:1, :]
        dx = r * gy - xv * (r * r * r) * jnp.mean(gy * xv, axis=1, keepdims=True)
        dx_ref[...] = dx + dr_ref[...] if dres is not None else dx
        dg_ref[0:1, :] += jnp.sum(g * xv * r, axis=0, keepdims=True)

    tile = pl.BlockSpec((NORM_ROWS, d), lambda t: (t, 0))
    gspec = pl.BlockSpec((8, d), lambda t: (0, 0))
    args = [dh, x, gain] + ([dres] if dres is not None else [])
    return pl.pallas_call(
        body, name="rms_bwd", grid=(s // NORM_ROWS,),
        in_specs=[tile, tile, gspec] + ([tile] if dres is not None else []),
        out_specs=[tile, gspec],
        out_shape=[jax.ShapeDtypeStruct((s, d), F32), jax.ShapeDtypeStruct((8, d), F32)],
        compiler_params=_params("arbitrary"),
    )(*args)


def loss_fwd_bwd(y, target):
    s, d = y.shape

    def body(y_ref, t_ref, l_ref, dy_ref):
        @pl.when(pl.program_id(0) == 0)
        def _():
            l_ref[...] = jnp.zeros_like(l_ref)

        err = y_ref[...] - t_ref[...]
        dy_ref[...] = err * (1.0 / d)
        part = 0.5 * jnp.sum(jnp.mean(err * err, axis=1, keepdims=True), axis=0, keepdims=True)
        r = lax.broadcasted_iota(jnp.int32, (8, LANE), 0)
        c = lax.broadcasted_iota(jnp.int32, (8, LANE), 1)
        l_ref[...] += jnp.where((r == 0) & (c == 0), part, 0.0)

    tile = pl.BlockSpec((NORM_ROWS, d), lambda t: (t, 0))
    return pl.pallas_call(
        body, name="loss", grid=(s // NORM_ROWS,), in_specs=[tile, tile],
        out_specs=[pl.BlockSpec((8, LANE), lambda t: (0, 0)), tile],
        out_shape=[jax.ShapeDtypeStruct((8, LANE), F32), jax.ShapeDtypeStruct((s, d), F32)],
        compiler_params=_params("arbitrary"),
    )(y, target)


def matmul(a, b, mode, out_dtype, tm, tn, tk, name, add=None, b_blocked=False, out_blocks=None, exchange=None):
    if b_blocked:
        nb, _, width = b.shape
        bshape = (b.shape[1], nb * width)
    else:
        bshape = b.shape
    if mode == "tn":
        (kdim, m), n = a.shape, bshape[1]
    else:
        (m, kdim), n = a.shape, (bshape[1] if mode == "nn" else bshape[0])
    tm, tn, tk = min(tm, m), min(tn, n), min(tk, kdim)
    assert m % tm == 0 and n % tn == 0 and kdim % tk == 0, (name, m, n, kdim)
    nk = kdim // tk
    dims = {"nn": _NN, "nt": _NT, "tn": _TN}[mode]

    def body(*refs):
        if add is not None:
            a_ref, b_ref, add_ref, o_ref, acc_ref = refs
        else:
            a_ref, b_ref, o_ref, acc_ref = refs
        k = pl.program_id(2)

        @pl.when(k == 0)
        def _():
            acc_ref[...] = jnp.zeros_like(acc_ref)

        acc_ref[...] += _dg(a_ref[...], b_ref[...], dims)

        @pl.when(k == nk - 1)
        def _():
            r = acc_ref[...]
            if add is not None:
                r = r + add_ref[...]
            o_ref[...] = r.astype(out_dtype)

    a_spec = pl.BlockSpec((tk, tm), lambda i, j, k: (k, i)) if mode == "tn" else pl.BlockSpec((tm, tk), lambda i, j, k: (i, k))
    if b_blocked and mode == "nn":
        per = width // tn
        assert width % tn == 0
        b_spec = pl.BlockSpec((None, tk, tn), lambda i, j, k: (j // per, k, j % per))
    elif b_blocked and mode == "nt":
        per = width // tk
        assert width % tk == 0
        b_spec = pl.BlockSpec((None, tn, tk), lambda i, j, k: (k // per, j, k % per))
    elif mode == "nt":
        b_spec = pl.BlockSpec((tn, tk), lambda i, j, k: (j, k))
    else:
        assert not b_blocked
        b_spec = pl.BlockSpec((tk, tn), lambda i, j, k: (k, j))
    add_spec = pl.BlockSpec((tm, tn), lambda i, j, k: (i, j))
    if out_blocks is not None:
        operb = (n // out_blocks) // tn
        assert (n // out_blocks) % tn == 0 and add is None
        o_spec = pl.BlockSpec((None, tm, tn), lambda i, j, k: (j // operb, i, j % operb))
        out_shape = jax.ShapeDtypeStruct((out_blocks, m, n // out_blocks), out_dtype)
    else:
        o_spec = add_spec
        out_shape = jax.ShapeDtypeStruct((m, n), out_dtype)
    res = hosted_call(
        body, exchange, name=name, grid=(m // tm, n // tn, nk),
        in_specs=[a_spec, b_spec] + ([add_spec] if add is not None else []), out_specs=[o_spec],
        out_shape=[out_shape], scratch_shapes=[pltpu.VMEM((tm, tn), F32)],
        sem=("parallel", "parallel", "arbitrary"),
    )(*([a, b] + ([add] if add is not None else [])))
    return res[0] if exchange is None else res


_HBM = pl.BlockSpec(memory_space=pltpu.HBM)


def _me():
    return lax.axis_index("x"), lax.axis_index("y"), lax.axis_index("c")


def _flat(p):
    return 4 * p[0] + 2 * p[1] + p[2]


def _flip(p, r):
    return tuple((1 - v) if (r >> (2 - a)) & 1 else v for a, v in enumerate(p))


class Exchange:
    def __init__(self, srcs, out_shapes, sems, start, finish, alias=None):
        self.srcs, self.out_shapes, self.sems = list(srcs), list(out_shapes), list(sems)
        self.start, self.finish, self.alias = start, finish, dict(alias or {})


def hosted_call(body, exchange, *, name, grid, in_specs, out_specs, out_shape, scratch_shapes=(),
                input_output_aliases=None, sem=()):
    in_specs, out_specs, out_shape = list(in_specs), list(out_specs), list(out_shape)
    scratch_shapes = list(scratch_shapes)
    aliases = input_output_aliases or {}
    if exchange is None:
        call = pl.pallas_call(body, name=name, grid=grid, in_specs=in_specs, out_specs=out_specs, out_shape=out_shape,
                              scratch_shapes=scratch_shapes, input_output_aliases=aliases, compiler_params=_params(*sem))
        return lambda *args: list(call(*args))
    ni, no, ns = len(in_specs), len(out_specs), len(scratch_shapes)
    xi, xo = len(exchange.srcs), len(exchange.out_shapes)

    def wrapped(*refs):
        ins, refs = refs[:ni], refs[ni:]
        xin, refs = refs[:xi], refs[xi:]
        outs, refs = refs[:no], refs[no:]
        xout, refs = refs[:xo], refs[xo:]
        scr, xsem = refs[:ns], refs[ns:]
        first = functools.reduce(lambda p, q: p & q, [pl.program_id(d) == 0 for d in range(len(grid))])
        last = functools.reduce(lambda p, q: p & q, [pl.program_id(d) == grid[d] - 1 for d in range(len(grid))])

        @pl.when(first)
        def _():
            exchange.start(xin, xout, xsem)

        body(*ins, *outs, *scr)

        @pl.when(last)
        def _():
            exchange.finish(xin, xout, xsem)

    call = pl.pallas_call(
        wrapped, name=name, grid=grid, in_specs=in_specs + [_HBM] * xi, out_specs=out_specs + [_HBM] * xo,
        out_shape=out_shape + exchange.out_shapes, scratch_shapes=scratch_shapes + exchange.sems,
        input_output_aliases={**aliases, **{ni + i: no + j for i, j in exchange.alias.items()}},
        compiler_params=_params(*(("arbitrary",) * len(grid))))
    return lambda *args: list(call(*args, *exchange.srcs))


def run_exchange(exchange, name):
    xi, xo = len(exchange.srcs), len(exchange.out_shapes)

    def body(*refs):
        exchange.start(refs[:xi], refs[xi:xi + xo], refs[xi + xo:])
        exchange.finish(refs[:xi], refs[xi:xi + xo], refs[xi + xo:])

    return list(pl.pallas_call(body, name=name, in_specs=[_HBM] * xi, out_specs=[_HBM] * xo,
                               out_shape=exchange.out_shapes, scratch_shapes=exchange.sems,
                               input_output_aliases=exchange.alias)(*exchange.srcs))


def gather_exchange(shards, rows=None, into=None):
    n = len(shards)

    def parts(srcs, outs, sems):
        send_sems, recv_sems, local_sems = sems
        me = _me()
        x, y, c = me
        chips = [(1 - x, y), (x, 1 - y), (1 - x, 1 - y)]

        def place(a, block):
            dst = outs[a].at[_flat(block)]
            return dst if rows is None else dst.at[pl.ds(rows[0], shards[a].shape[0])]

        def copy(a, k, block, to, src=None):
            dst = place(a, block)
            return pltpu.make_async_remote_copy(src_ref=dst if src is None else src, dst_ref=dst,
                                                send_sem=send_sems.at[a, k], recv_sem=recv_sems.at[a, k],
                                                device_id=to, device_id_type=MESH)

        mine = [pltpu.make_async_copy(srcs[a], place(a, me), local_sems.at[a]) for a in range(n)]
        own = []
        for a in range(n):
            own.append(copy(a, 0, me, (x, y, 1 - c), src=srcs[a]))
            own += [copy(a, 1 + j, me, (*chip, c), src=srcs[a]) for j, chip in enumerate(chips)]
        return me, chips, copy, mine, own

    def start(srcs, outs, sems):
        _, _, _, mine, own = parts(srcs, outs, sems)
        for cp in mine + own:
            cp.start()

    def finish(srcs, outs, sems):
        me, chips, copy, mine, own = parts(srcs, outs, sems)
        x, y, c = me
        passed = []
        for j, chip in enumerate(chips):
            for a in range(n):
                copy(a, 1 + j, (*chip, c), me).wait_recv()
                fwd = copy(a, 4 + j, (*chip, c), (x, y, 1 - c))
                fwd.start()
                passed.append(fwd)
        for a in range(n):
            copy(a, 0, (x, y, 1 - c), me).wait_recv()
            for j, chip in enumerate(chips):
                copy(a, 4 + j, (*chip, 1 - c), me).wait_recv()
        for cp in own + passed:
            cp.wait_send()
        for cp in mine:
            cp.wait()

    dma = pltpu.SemaphoreType.DMA
    full = lambda s: s.shape if rows is None else (rows[1],) + s.shape[1:]
    return Exchange(list(shards) + list(into or []), [jax.ShapeDtypeStruct((N_DEV,) + full(s), s.dtype) for s in shards],
                    [dma((n, 7)), dma((n, 7)), dma((n,))], start, finish,
                    alias={n + a: a for a in range(n)} if into else None)


def pair_exchange(srcs):
    n = len(srcs)

    def copies(srcs_, outs, sems):
        send_sems, recv_sems = sems
        x, y, c = _me()
        return [pltpu.make_async_remote_copy(src_ref=srcs_[a].at[:, 1 - c], dst_ref=outs[a], send_sem=send_sems.at[a],
                                             recv_sem=recv_sems.at[a], device_id=(x, y, 1 - c), device_id_type=MESH)
                for a in range(n)]

    def start(srcs_, outs, sems):
        for cp in copies(srcs_, outs, sems):
            cp.start()

    def finish(srcs_, outs, sems):
        for cp in copies(srcs_, outs, sems):
            cp.wait()

    dma = pltpu.SemaphoreType.DMA
    return Exchange(srcs, [jax.ShapeDtypeStruct((4,) + s.shape[2:], s.dtype) for s in srcs], [dma((n,)), dma((n,))],
                    start, finish)


def pair_sum(src, half, name):
    _, _, rows, cols = src.shape
    tr = min(rows, 256)

    def body(x_ref, h_ref, o_ref):
        c = lax.axis_index("c")
        o_ref[0] = (x_ref[0, c].astype(F32) + h_ref[0].astype(F32)).astype(BF16)

    return pl.pallas_call(
        body, name=name, grid=(4, rows // tr),
        in_specs=[pl.BlockSpec((1, 2, tr, cols), lambda ch, t: (ch, 0, t, 0)),
                  pl.BlockSpec((1, tr, cols), lambda ch, t: (ch, t, 0))],
        out_specs=pl.BlockSpec((1, tr, cols), lambda ch, t: (ch, t, 0)),
        out_shape=jax.ShapeDtypeStruct(half.shape, BF16), compiler_params=_params("parallel", "parallel"),
    )(src, half)


def merge_exchanges(a, b):
    ns, no, nm = len(a.srcs), len(a.out_shapes), len(a.sems)

    def start(srcs, outs, sems):
        a.start(srcs[:ns], outs[:no], sems[:nm])
        b.start(srcs[ns:], outs[no:], sems[nm:])

    def finish(srcs, outs, sems):
        a.finish(srcs[:ns], outs[:no], sems[:nm])
        b.finish(srcs[ns:], outs[no:], sems[nm:])

    alias = {**a.alias, **{ns + i: no + j for i, j in b.alias.items()}}
    return Exchange(a.srcs + b.srcs, a.out_shapes + b.out_shapes, a.sems + b.sems, start, finish, alias)


def chip_exchange(parts, slots, recv_shapes, rows=None, into=None):
    n = len(parts)
    win = (lambda ref: ref) if rows is None else (lambda ref: ref.at[pl.ds(rows[0], rows[1])])

    def plan(srcs, outs, sems):
        send_sems, recv_sems, local_sems = sems
        x, y, c = _me()
        chip = 2 * x + y
        mine = [pltpu.make_async_copy(win(srcs[a].at[chip]), win(outs[slots[a][0]].at[chip, slots[a][1]]),
                                      local_sems.at[a]) for a in range(n)]
        sends, arrivals = [], []
        for r in (1, 2, 3):
            px = (1 - x) if r & 2 else x
            py = (1 - y) if r & 1 else y
            for a in range(n):
                ri, layer = slots[a]
                sends.append(pltpu.make_async_remote_copy(
                    src_ref=win(srcs[a].at[2 * px + py]), dst_ref=win(outs[ri].at[chip, layer]),
                    send_sem=send_sems.at[a, r - 1],
                    recv_sem=recv_sems.at[a, r - 1], device_id=(px, py, c), device_id_type=MESH))
                land = win(outs[ri].at[2 * px + py, layer])
                arrivals.append(pltpu.make_async_remote_copy(
                    src_ref=land, dst_ref=land, send_sem=send_sems.at[a, r - 1], recv_sem=recv_sems.at[a, r - 1],
                    device_id=(px, py, c), device_id_type=MESH))
        return mine, sends, arrivals

    def start(srcs, outs, sems):
        mine, sends, _ = plan(srcs, outs, sems)
        for cp in mine + sends:
            cp.start()

    def finish(srcs, outs, sems):
        mine, sends, arrivals = plan(srcs, outs, sems)
        for cp in arrivals:
            cp.wait_recv()
        for cp in sends:
            cp.wait_send()
        for cp in mine:
            cp.wait()

    dma = pltpu.SemaphoreType.DMA
    return Exchange(list(parts) + list(into or []), [jax.ShapeDtypeStruct(s, BF16) for s in recv_shapes],
                    [dma((n, 3)), dma((n, 3)), dma((n,))], start, finish,
                    alias={n + k: k for k in range(len(recv_shapes))} if into else None)


SMALL_ROWS = 24


def all_reduce_small(pack):
    def body(p_ref, o_ref, buf, send_sems, recv_sems):
        me = _me()
        buf[_flat(me)] = p_ref[...]
        sent = []
        for r in range(1, N_DEV):
            peer = _flip(me, r)
            cp = pltpu.make_async_remote_copy(src_ref=p_ref, dst_ref=buf.at[_flat(me)], send_sem=send_sems.at[r - 1],
                                              recv_sem=recv_sems.at[r - 1], device_id=peer, device_id_type=MESH)
            cp.start()
            sent.append(cp)
        for r in range(1, N_DEV):
            peer = _flip(me, r)
            land = buf.at[_flat(peer)]
            pltpu.make_async_remote_copy(src_ref=land, dst_ref=land, send_sem=send_sems.at[r - 1],
                                         recv_sem=recv_sems.at[r - 1], device_id=peer, device_id_type=MESH).wait_recv()
        for cp in sent:
            cp.wait_send()
        acc = buf[0]
        for d in range(1, N_DEV):
            acc = acc + buf[d]
        o_ref[...] = acc

    vm = pl.BlockSpec(memory_space=pltpu.VMEM)
    return pl.pallas_call(
        body, name="all_reduce_small", in_specs=[vm], out_specs=vm,
        out_shape=jax.ShapeDtypeStruct(pack.shape, F32),
        scratch_shapes=[pltpu.VMEM((N_DEV,) + pack.shape, F32), pltpu.SemaphoreType.DMA((7,)),
                        pltpu.SemaphoreType.DMA((7,))],
    )(pack)


def _adamw(w, g, m, v):
    m = ADAM_B1 * m + (1.0 - ADAM_B1) * g
    v = ADAM_B2 * v + (1.0 - ADAM_B2) * (g * g)
    m_hat = m / (1.0 - ADAM_B1 ** ADAM_STEP)
    v_hat = v / (1.0 - ADAM_B2 ** ADAM_STEP)
    delta = -ADAM_LR * (m_hat / (jnp.sqrt(v_hat) + ADAM_EPS) + ADAM_WD * w)
    return delta, m, v


ADAM_ROWS = 128


def reduce_adamw(recv, w, m, v, name):
    nl, rows, cols = w.shape
    nslot, cp = recv.shape[0], recv.shape[3]

    def body(r_ref, w_ref, m_ref, v_ref, g_ref, d_ref, mo_ref, vo_ref):
        g = r_ref[0, 0].astype(F32)
        for slot in range(1, nslot):
            g = g + r_ref[slot, 0].astype(F32)
        if cp != cols:
            g = g[:, :cols]
        delta, m_new, v_new = _adamw(w_ref[0], g, m_ref[0], v_ref[0])
        g_ref[0] = g
        d_ref[0] = delta
        mo_ref[0] = m_new
        vo_ref[0] = v_new

    tile = pl.BlockSpec((1, ADAM_ROWS, cols), lambda l, t: (l, t, 0))
    out = jax.ShapeDtypeStruct(w.shape, F32)
    return pl.pallas_call(
        body, name=name, grid=(nl, rows // ADAM_ROWS),
        in_specs=[pl.BlockSpec((nslot, 1, ADAM_ROWS, cp), lambda l, t: (0, l, t, 0)), tile, tile, tile],
        out_specs=[tile, tile, tile, tile], out_shape=[out, out, out, out],
        compiler_params=_params("parallel", "parallel"),
    )(recv, w, m, v)


def reduce_adamw_t(recv, w_t, m_t, v_t, name):
    cols, rows = w_t.shape
    nslot, cp = recv.shape[0], recv.shape[3]
    tr = 256

    def body(r_ref, w_ref, m_ref, v_ref, g_ref, d_ref, mo_ref, vo_ref):
        g = r_ref[0, 0].astype(F32)
        for slot in range(1, nslot):
            g = g + r_ref[slot, 0].astype(F32)
        g_ref[...] = g.T[:cols, :]
        delta, m_new, v_new = _adamw(w_ref[...], g_ref[...], m_ref[...], v_ref[...])
        d_ref[...] = delta
        mo_ref[...] = m_new
        vo_ref[...] = v_new

    tile = pl.BlockSpec((cols, tr), lambda t: (0, t))
    out = jax.ShapeDtypeStruct((cols, rows), F32)
    return pl.pallas_call(
        body, name=name, grid=(rows // tr,),
        in_specs=[pl.BlockSpec((nslot, 1, tr, cp), lambda t: (0, 0, t, 0)), tile, tile, tile],
        out_specs=[tile, tile, tile, tile], out_shape=[out, out, out, out], compiler_params=_params("parallel"),
    )(recv, w_t, m_t, v_t)


def adamw_small(g, w, m, v):
    def body(g_ref, w_ref, m_ref, v_ref, d_ref, mo_ref, vo_ref):
        d_ref[...], mo_ref[...], vo_ref[...] = _adamw(w_ref[...], g_ref[...], m_ref[...], v_ref[...])

    out = jax.ShapeDtypeStruct(g.shape, F32)
    return pl.pallas_call(body, name="adamw_small", out_shape=[out, out, out])(g, w, m, v)


def _row8(v):
    return jnp.pad(v.reshape(1, -1).astype(F32), ((0, 7), (0, 0)))


def _row8_lanes(v, width=LANE):
    return jnp.pad(v.reshape(1, -1).astype(F32), ((0, 7), (0, width - v.size)))


def _pack_rows(parts):
    rows = []
    for p in parts:
        p = p.reshape(-1).astype(F32)
        nrow = -(-p.size // D_MODEL)
        rows.append(jnp.pad(p, (0, nrow * D_MODEL - p.size)).reshape(nrow, D_MODEL))
    out = jnp.concatenate(rows, axis=0)
    return jnp.pad(out, ((0, SMALL_ROWS - out.shape[0]), (0, 0)))


def _unpack_rows(pack, shapes):
    out, r = [], 0
    for shp in shapes:
        size = 1
        for d in shp:
            size *= d
        nrow = -(-size // D_MODEL)
        out.append(pack[r:r + nrow].reshape(-1)[:size].reshape(shp))
        r += nrow
    return out


def _dn_weight_layout(gathered):
    split = DN_QKV + 2 * DN_V_HEADS
    pieces = []
    for d in range(N_DEV):
        lo, hi = d * DN_SHARD, (d + 1) * DN_SHARD
        if lo < split < hi:
            pieces += [gathered[d, :, :split - lo], jnp.zeros((D_MODEL, DN_COLS - DN_PROJ), gathered.dtype),
                       gathered[d, :, split - lo:DN_SHARD]]
        else:
            pieces.append(gathered[d, :, :DN_SHARD])
    return jnp.concatenate(pieces, axis=1)


def _dn_grad_blocks(dw):
    split = DN_QKV + 2 * DN_V_HEADS
    gap = DN_COLS - DN_PROJ
    local = lambda c: c if c <= split else c + gap
    zeros = jnp.zeros((D_MODEL, DN_SHARD_PAD - DN_SHARD), dw.dtype)
    blocks = []
    for d in range(N_DEV):
        lo, hi = d * DN_SHARD, (d + 1) * DN_SHARD
        if lo < split < hi:
            parts = [dw[:, lo:split], dw[:, split + gap:hi + gap]]
        else:
            parts = [dw[:, local(lo):local(lo) + DN_SHARD]]
        blocks.append(jnp.concatenate(parts + [zeros], axis=1))
    return jnp.stack(blocks)


def kernel(x, mem, norm_g, mem_norm_g, mem_w_kv, xa_q_norm_g, xa_k_norm_g, w_out, dn_w_in, dn_conv_w, dn_a_log, dn_dt_bias, dn_out_norm_g, sb_w_in, sb_q_norm_g, sb_k_norm_g, loss_target, m_norm_g, m_mem_norm_g, m_mem_w_kv, m_xa_q_norm_g, m_xa_k_norm_g, m_w_out, m_dn_w_in, m_dn_conv_w, m_dn_a_log, m_dn_dt_bias, m_dn_out_norm_g, m_sb_w_in, m_sb_q_norm_g, m_sb_k_norm_g, v_norm_g, v_mem_norm_g, v_mem_w_kv, v_xa_q_norm_g, v_xa_k_norm_g, v_w_out, v_dn_w_in, v_dn_conv_w, v_dn_a_log, v_dn_dt_bias, v_dn_out_norm_g, v_sb_w_in, v_sb_q_norm_g, v_sb_k_norm_g):
    x0, memv, target = x[0], mem[0], loss_target[0]
    my_dev = 4 * lax.axis_index("x") + 2 * lax.axis_index("y") + lax.axis_index("c")

    dn_shard = jnp.pad(dn_w_in[0].astype(BF16), ((0, 0), (0, DN_SHARD_PAD - DN_SHARD)))
    w_out_b = [w_out[i].astype(BF16) for i in range(2)]
    w_kv_b = [mem_w_kv[i].astype(BF16) for i in range(2)]
    conv_shard = jnp.pad(dn_conv_w[0], ((0, 4), (0, 0)))
    g_dn, g_conv = run_exchange(gather_exchange([dn_shard, conv_shard]), "gather_first")
    w_dn = _dn_weight_layout(g_dn)
    conv_w = jnp.transpose(g_conv, (1, 0, 2)).reshape(8, DN_QKV)

    ng = [_row8(norm_g[0]), _row8(norm_g[1])]
    mem_g = _row8(mem_norm_g)
    xqg = [_row8(xa_q_norm_g[0]), _row8(xa_q_norm_g[1])]
    xkg = [_row8(xa_k_norm_g[0]), _row8(xa_k_norm_g[1])]
    alog, dtb = _row8_lanes(dn_a_log[0]), _row8_lanes(dn_dt_bias[0])
    out_g, sbq_g, sbk_g = _row8(dn_out_norm_g[0]), _row8(sb_q_norm_g[0]), _row8(sb_k_norm_g[0])

    mem_n = rms_fwd(memv, mem_g)
    h0, h0_t = rms_fwd(x0, ng[0], with_transpose=True)
    proj0, g_wo0 = matmul(h0, w_dn, "nn", F32, 1024, 1152, 2048, "proj_dn", exchange=gather_exchange([w_out_b[0]]))
    act, g_kv0 = dn_conv_fwd(proj0, conv_w, exchange=gather_exchange([w_kv_b[0]]))
    sb_shard, half = sb_w_in[0].astype(BF16), 9 * D_MODEL // 16
    u0, w_, qd, kd, qk, cd, w_sb = dn_prep_fwd(
        act, proj0, alog, dtb, exchange=gather_exchange([sb_shard[:half]], rows=(0, D_MODEL)))
    o_raw, states, w_sb = dn_scan_fwd(
        u0, w_, qd, kd, qk, cd, MIX_WIDTH,
        exchange=gather_exchange([sb_shard[half:]], rows=(half, D_MODEL), into=[w_sb]))
    w_o = [g_wo0.reshape(INNER, D_MODEL), None]
    w_kv = [g_kv0.reshape(D_MODEL, 2 * XA_WIDTH), None]
    kv = [matmul(mem_n, w_kv[0], "nn", F32, 256, 1024, 2048, "kv0"), None]
    cat0 = head_norm_fwd(o_raw, 0, DN_V_HEADS, out_g[None], F32, INNER, "dn_out_norm")
    cat0 = xa_fwd(proj0, DN_XQ_BLK, kv[0], xqg[0], xkg[0], cat0)
    y0 = gate_fwd(cat0, proj0, DN_Z_BLK)
    x1 = matmul(y0, w_o[0], "nn", F32, 1024, 1024, 2048, "out_proj0", add=x0)

    h1, h1_t = rms_fwd(x1, ng[1], with_transpose=True)
    proj1 = matmul(h1, w_sb, "nn", F32, 1024, 896, 2048, "proj_sb", b_blocked=True)
    qkn = head_norm_fwd(proj1, 0, 2 * SB_HEADS, jnp.stack([sbq_g, sbk_g]), BF16, 2 * MIX_WIDTH, "sb_qk_norm")
    cat1, ltot, g_wo1, g_kv1 = sb_fwd(qkn, proj1, 2 * SB_HEADS, INNER,
                                      exchange=gather_exchange([w_out_b[1], w_kv_b[1]]))
    w_o[1] = g_wo1.reshape(INNER, D_MODEL)
    w_kv[1] = g_kv1.reshape(D_MODEL, 2 * XA_WIDTH)
    kv[1] = matmul(mem_n, w_kv[1], "nn", F32, 256, 1024, 2048, "kv1")
    cat1 = xa_fwd(proj1, SB_XQ_BLK, kv[1], xqg[1], xkg[1], cat1)
    y1 = gate_fwd(cat1, proj1, SB_Z_BLK)
    x2 = matmul(y1, w_o[1], "nn", F32, 1024, 1024, 2048, "out_proj1", add=x1)
    loss_part, dx2 = loss_fwd_bwd(x2, target)

    dy1 = matmul(dx2, w_o[1], "nt", F32, 1024, 1024, 2048, "d_y1")
    dw_o1 = matmul(y1, dx2, "tn", BF16, 1024, 1024, 1024, "d_w_out1")
    dcat1, dproj1 = gate_bwd(dy1, cat1, proj1, SB_Z_BLK)
    dproj1, dxk1, dxv1, dxqg1, dxkg1 = xa_bwd(proj1, SB_XQ_BLK, kv[1], xqg[1], xkg[1], dcat1, dproj1)
    dqn, dkn, dproj1 = sb_bwd(qkn, proj1, 2 * SB_HEADS, dcat1, ltot, dproj1)
    dproj1, d_sbq = head_norm_bwd(dqn, 0, proj1, 0, SB_HEADS, sbq_g, dproj1, 0, "sb_q_norm_bwd")
    dproj1, d_sbk = head_norm_bwd(dkn, 0, proj1, SB_HEADS, SB_HEADS, sbk_g, dproj1, SB_HEADS, "sb_k_norm_bwd")
    dw_sb = matmul(h1_t, dproj1, "nn", BF16, 1024, 896, 1024, "d_w_sb", out_blocks=N_DEV)
    dh1 = matmul(dproj1, w_sb, "nt", F32, 1024, 1024, 896, "d_h1", b_blocked=True)
    dx1, d_ng1 = rms_bwd(dh1, x1, ng[1], dx2)

    by_owner = lambda g, rows: g.reshape(4, 2, rows, g.size // (N_DEV * rows))
    sb_grad = dw_sb.reshape(4, 2, D_MODEL, SB_PROJ // N_DEV)
    dy0, sb_half = matmul(dx1, w_o[0], "nt", F32, 1024, 1024, 2048, "d_y0", exchange=pair_exchange([sb_grad]))
    sb_sum = pair_sum(sb_grad, sb_half, "pair_sum_sb")
    dw_o0 = matmul(y0, dx1, "tn", BF16, 1024, 1024, 1024, "d_w_out0")
    dcat0, dproj0 = gate_bwd(dy0, cat0, proj0, DN_Z_BLK)
    dproj0, dxk0, dxv0, dxqg0, dxkg0 = xa_bwd(proj0, DN_XQ_BLK, kv[0], xqg[0], xkg[0], dcat0, dproj0)

    dkv = [jnp.concatenate([dxk0, dxv0], axis=1), jnp.concatenate([dxk1, dxv1], axis=1)]
    dw_kv = [matmul(mem_n, dkv[i], "tn", BF16, 1024, 1024, 256, f"d_w_kv{i}") for i in range(2)]
    dmem_n = matmul(dkv[0], w_kv[0], "nt", F32, 256, 1024, 2048, "d_mem_n0")
    dmem_n = matmul(dkv[1], w_kv[1], "nt", F32, 256, 1024, 2048, "d_mem_n1", add=dmem_n)
    _, d_memg = rms_bwd(dmem_n, memv, mem_g, None)

    grads = [by_owner(dw_o0, INNER // N_DEV), by_owner(dw_o1, INNER // N_DEV),
             by_owner(dw_kv[0], D_MODEL // N_DEV), by_owner(dw_kv[1], D_MODEL // N_DEV)]
    sb_recv_shape = [(4, 1, D_MODEL, SB_PROJ // N_DEV)]
    sb_first = D_MODEL // 2
    do_raw, d_outg = head_norm_bwd(dcat0, 0, o_raw, 0, DN_V_HEADS, out_g, None, 0, "dn_out_norm_bwd")
    du0, dw_, dqd, dkd, dqk, dcd, *halves, recv_sb = dn_scan_bwd(
        do_raw, u0, w_, qd, kd, qk, cd, states,
        exchange=merge_exchanges(pair_exchange(grads),
                                 chip_exchange([sb_sum], [(0, 0)], sb_recv_shape, rows=(0, sb_first))))
    sums = [pair_sum(g, h, f"pair_sum{i}") for i, (g, h) in enumerate(zip(grads, halves))]
    to_chips = merge_exchanges(
        chip_exchange(sums, [(0, 0), (0, 1), (1, 0), (1, 1)],
                      [(4, 2, INNER // N_DEV, D_MODEL), (4, 2, D_MODEL // N_DEV, 2 * XA_WIDTH)]),
        chip_exchange([sb_sum], [(0, 0)], sb_recv_shape, rows=(sb_first, D_MODEL - sb_first), into=[recv_sb]))
    dq_a, dk_a, dv_a, dab, d_alog, d_dtb, recv_wo, recv_kv, recv_sb = dn_prep_bwd(
        act, proj0, alog, dtb, du0, dw_, dqd, dkd, dqk, dcd, exchange=to_chips)
    dproj0, dcw_q = dn_conv_bwd(dq_a, proj0, conv_w, dproj0, None, 0, "dn_conv_bwd_q")
    dproj0, dcw_k = dn_conv_bwd(dk_a, proj0, conv_w, dproj0, None, DN_QK_HEADS, "dn_conv_bwd_k")
    dproj0, dcw_v = dn_conv_bwd(dv_a, proj0, conv_w, dproj0, dab, 2 * DN_QK_HEADS, "dn_conv_bwd_v")
    dw_dn = matmul(h0_t, dproj0, "nn", BF16, 1024, 1152, 1024, "d_w_dn")
    dn_grad = _dn_grad_blocks(dw_dn).reshape(4, 2, D_MODEL, DN_SHARD_PAD)
    dn_half, = run_exchange(pair_exchange([dn_grad]), "pair_dn")
    dn_sum = pair_sum(dn_grad, dn_half, "pair_sum_dn")
    dh0, recv_dn = matmul(dproj0, w_dn, "nt", F32, 1024, 1024, 1152, "d_h0",
                          exchange=chip_exchange([dn_sum], [(0, 0)], [(4, 1, D_MODEL, DN_SHARD_PAD)]))
    grad_x, d_ng0 = rms_bwd(dh0, x0, ng[0], dx1)

    big = {
        "dn_w_in": tuple(jnp.transpose(a)[None] for a in reduce_adamw_t(
            recv_dn, jnp.transpose(dn_w_in[0]), jnp.transpose(m_dn_w_in[0]), jnp.transpose(v_dn_w_in[0]),
            "adamw_dn_w_in")),
        "sb_w_in": reduce_adamw(recv_sb, sb_w_in, m_sb_w_in, v_sb_w_in, "adamw_sb_w_in"),
        "w_out": reduce_adamw(recv_wo, w_out, m_w_out, v_w_out, "adamw_w_out"),
        "mem_w_kv": reduce_adamw(recv_kv, mem_w_kv, m_mem_w_kv, v_mem_w_kv, "adamw_mem_w_kv"),
    }

    dconv = jnp.concatenate([dcw_q, dcw_k, dcw_v], axis=1)[:4]
    small_shapes = [(2, D_MODEL), (D_MODEL,), (2, XA_DIM), (2, XA_DIM), (4, DN_QKV), (1, DN_V_HEADS),
                    (1, DN_V_HEADS), (1, HEAD_DIM), (1, HEAD_DIM), (1, HEAD_DIM), (1,)]
    pack = _pack_rows([jnp.stack([d_ng0[0], d_ng1[0]]), d_memg[0], jnp.stack([dxqg0[0], dxqg1[0]]),
                       jnp.stack([dxkg0[0], dxkg1[0]]), dconv, d_alog[0, :DN_V_HEADS], d_dtb[0, :DN_V_HEADS],
                       d_outg[0], d_sbq[0], d_sbk[0], loss_part[0, :1]])
    total = all_reduce_small(pack)
    (g_norm, g_memn, g_xq, g_xk, g_conv_full, g_alog, g_dtb, g_outn, g_sbq, g_sbk, loss1) = _unpack_rows(total, small_shapes)
    conv_cols = DN_QKV // N_DEV
    g_conv = lax.dynamic_slice(g_conv_full, (0, my_dev * conv_cols), (4, conv_cols))[None]
    names = ["norm_g", "mem_norm_g", "xa_q_norm_g", "xa_k_norm_g", "dn_conv_w", "dn_a_log", "dn_dt_bias",
             "dn_out_norm_g", "sb_q_norm_g", "sb_k_norm_g"]
    grads = [g_norm, g_memn, g_xq, g_xk, g_conv, g_alog, g_dtb, g_outn, g_sbq, g_sbk]
    ws = [norm_g, mem_norm_g, xa_q_norm_g, xa_k_norm_g, dn_conv_w, dn_a_log, dn_dt_bias, dn_out_norm_g, sb_q_norm_g,
          sb_k_norm_g]
    ms = [m_norm_g, m_mem_norm_g, m_xa_q_norm_g, m_xa_k_norm_g, m_dn_conv_w, m_dn_a_log, m_dn_dt_bias,
          m_dn_out_norm_g, m_sb_q_norm_g, m_sb_k_norm_g]
    vs = [v_norm_g, v_mem_norm_g, v_xa_q_norm_g, v_xa_k_norm_g, v_dn_conv_w, v_dn_a_log, v_dn_dt_bias,
          v_dn_out_norm_g, v_sb_q_norm_g, v_sb_k_norm_g]
    shapes = [w.shape for w in ws]
    d_p, m_p, v_p = adamw_small(_pack_rows(grads), _pack_rows(ws), _pack_rows(ms), _pack_rows(vs))
    small = dict(zip(names, zip(grads, _unpack_rows(d_p, shapes), _unpack_rows(m_p, shapes), _unpack_rows(v_p, shapes))))

    order = ["norm_g", "mem_norm_g", "mem_w_kv", "xa_q_norm_g", "xa_k_norm_g", "w_out", "dn_w_in", "dn_conv_w",
             "dn_a_log", "dn_dt_bias", "dn_out_norm_g", "sb_w_in", "sb_q_norm_g", "sb_k_norm_g"]
    res = {**big, **small}
    outs = [loss1.reshape(()), grad_x[None]]
    for k in range(4):
        outs += [res[n][k] for n in order]
    return tuple(outs)
```

```python
import functools

import jax
import jax.numpy as jnp
from jax import lax
from jax.experimental import pallas as pl
from jax.experimental.pallas import tpu as pltpu

F32 = jnp.float32
BF16 = jnp.bfloat16

D_MODEL = 2048
SEQ = 2048
N_MEM = 256
INNER = 4096
XA_HEADS = 4
XA_WIDTH = 1024
XA_DIM = 256
MIX_WIDTH = 3072
HEAD_DIM = 128
DN_V_HEADS = 24
DN_QK_HEADS = 12
DN_QK_WIDTH = 1536
DN_CHUNK = 64
DN_QKV = 2 * DN_QK_WIDTH + MIX_WIDTH
DN_PROJ = 11312
SB_HEADS = 24
SB_PROJ = 14336
EPS = 1e-6
N_DEV = 8
DN_SHARD = DN_PROJ // N_DEV
DN_SHARD_PAD = 1536
LANE = 128
DN_COLS = 90 * LANE
DN_AB_BLK, DN_PAD_BLK, DN_XQ_BLK, DN_Z_BLK = 48, 49, 50, 58
SB_XQ_BLK, SB_Z_BLK = 72, 80

ADAM_LR, ADAM_B1, ADAM_B2, ADAM_EPS, ADAM_WD, ADAM_STEP = 0.001, 0.9, 0.999, 1e-08, 0.01, 10

VMEM_LIMIT = 56 * 1024 * 1024
MESH = pl.DeviceIdType.MESH

_NN, _NT, _TN = "nn", "nt", "tn"


def _dims(mode, rank):
    lhs, rhs = {"nn": (1, 0), "nt": (1, 1), "tn": (0, 0)}[mode]
    if rank == 2:
        return (((lhs,), (rhs,)), ((), ()))
    return (((lhs + 1,), (rhs + 1,)), ((0,), (0,)))


def _params(*sem):
    return pltpu.CompilerParams(dimension_semantics=sem if sem else None, vmem_limit_bytes=VMEM_LIMIT)


def _dot(a, b, mode):
    return lax.dot_general(a, b, _dims(mode, a.ndim), preferred_element_type=F32)


def _dg(a, b, dims):
    return _dot(a.astype(BF16), b.astype(BF16), dims)


@jax.custom_vjp
def mm(a, b):
    return _dg(a, b, _NN)


@jax.custom_vjp
def mm_nt(a, b):
    return _dg(a, b, _NT)


@jax.custom_vjp
def mm_tn(a, b):
    return _dg(a, b, _TN)


mm.defvjp(lambda a, b: (_dg(a, b, _NN), (a, b)), lambda r, g: (mm_nt(g, r[1]), mm_tn(r[0], g)))
mm_nt.defvjp(lambda a, b: (_dg(a, b, _NT), (a, b)), lambda r, g: (mm(g, r[1]), mm_tn(g, r[0])))
mm_tn.defvjp(lambda a, b: (_dg(a, b, _TN), (a, b)), lambda r, g: (mm_nt(r[1], g), mm(r[0], g)))


def _split3(x):
    hi = x.astype(BF16)
    r1 = x - hi.astype(F32)
    mid = r1.astype(BF16)
    lo = (r1 - mid.astype(F32)).astype(BF16)
    return hi, mid, lo


def _dg3(a, b, dims):
    ah, am, _ = _split3(a)
    bh, bm, _ = _split3(b)
    return _dot(ah, bh, dims) + (_dot(ah, bm, dims) + _dot(am, bh, dims))


def _dg_exact_rhs(a, b01, dims):
    ah, am, _ = _split3(a)
    b = b01.astype(BF16)
    return _dot(ah, b, dims) + _dot(am, b, dims)


def _dg_exact_lhs(a01, b, dims):
    bh, bm, bl = _split3(b)
    a = a01.astype(BF16)
    return _dot(a, bh, dims) + (_dot(a, bm, dims) + _dot(a, bl, dims))


@jax.custom_vjp
def mm3(a, b):
    return _dg3(a, b, _NN)


@jax.custom_vjp
def mm3_nt(a, b):
    return _dg3(a, b, _NT)


@jax.custom_vjp
def mm3_tn(a, b):
    return _dg3(a, b, _TN)


mm3.defvjp(lambda a, b: (_dg3(a, b, _NN), (a, b)), lambda r, g: (mm3_nt(g, r[1]), mm3_tn(r[0], g)))
mm3_nt.defvjp(lambda a, b: (_dg3(a, b, _NT), (a, b)), lambda r, g: (mm3(g, r[1]), mm3_tn(g, r[0])))
mm3_tn.defvjp(lambda a, b: (_dg3(a, b, _TN), (a, b)), lambda r, g: (mm3_nt(r[1], g), mm3(r[0], g)))


def _softplus(x):
    return jnp.maximum(x, 0.0) + jnp.log(1.0 + jnp.exp(-jnp.abs(x)))


def _log_sigmoid(x):
    return jnp.minimum(x, 0.0) - jnp.log(1.0 + jnp.exp(-jnp.abs(x)))


def _sigmoid(x):
    return 1.0 / (1.0 + jnp.exp(-x))


def _silu(x):
    return x * _sigmoid(x)


def _silu_grad(x):
    s = _sigmoid(x)
    return s * (1.0 + x * (1.0 - s))


@jax.custom_vjp
def mm01(a01, b):
    return _dg_exact_lhs(a01, b, _NN)


mm01.defvjp(lambda a, b: (_dg_exact_lhs(a, b, _NN), a),
            lambda a, g: (jnp.zeros_like(a), _dg_exact_lhs(a, g, _TN)))


def _lane_pick(x, idx):
    lane = lax.broadcasted_iota(jnp.int32, x.shape, x.ndim - 1)
    return jnp.sum(jnp.where(lane == idx, x, 0.0), axis=-1, keepdims=True)


def _dn_chunk(qt, kt, v, ab, alog, dtb, h):
    B, C = qt.shape[0], DN_CHUNK
    g = -jnp.exp(_lane_pick(alog, h)) * _softplus(_lane_pick(ab, h) + _lane_pick(dtb, h))
    beta = _sigmoid(_lane_pick(ab, h + DN_V_HEADS))
    q = qt * lax.rsqrt(jnp.sum(qt * qt, axis=-1, keepdims=True) + EPS) * (HEAD_DIM ** -0.5)
    k = kt * lax.rsqrt(jnp.sum(kt * kt, axis=-1, keepdims=True) + EPS)
    row = lax.broadcasted_iota(jnp.int32, (B, C, C), 1)
    col = lax.broadcasted_iota(jnp.int32, (B, C, C), 2)
    lower = (row >= col).astype(F32)
    ones = jnp.ones((B, C, C), F32)
    g_wide = jnp.broadcast_to(g, (B, C, LANE))
    g_sq = jnp.broadcast_to(g, (B, C, C))
    gc = mm01(lower, g_wide)
    gc_i = gc[:, :, :C]
    gc_j = mm01(ones, jnp.where(row <= col, g_sq, 0.0))
    g_last = jnp.broadcast_to(gc[:, C - 1:C, :], (B, C, LANE))
    decay = jnp.exp(jnp.where(row >= col, gc_i - gc_j, -1e30))
    eg = jnp.exp(gc)
    kk = mm_nt(k, k)
    a_mat = jnp.where(row > col, jnp.broadcast_to(beta, (B, C, C)) * kk * decay, 0.0)
    eye = (row == col).astype(F32)
    y = -a_mat
    t = eye + y
    for _ in range(5):
        y = mm(y, y)
        t = t + mm(t, y)
    bb = jnp.broadcast_to(beta, (B, C, LANE))
    u0 = mm(t, v * bb)
    w = mm(t, k * (bb * eg))
    qk = mm_nt(q, k) * decay
    q_dec = q * eg
    k_dec = k * jnp.exp(g_last - gc)
    cd = jnp.exp(g_last)[:, :8, :]
    return u0, w, qk, q_dec, k_dec, cd


def _shift_rows(x, j, down):
    if j == 0:
        return x
    n = x.shape[0]
    r = lax.broadcasted_iota(jnp.int32, x.shape, 0)
    if down:
        return jnp.where(r >= j, pltpu.roll(x, j, 0), 0.0)
    return jnp.where(r < n - j, pltpu.roll(x, n - j, 0), 0.0)


def dn_conv_fwd(proj, conv_w, exchange=None):
    s = proj.shape[0]

    def body(x_ref, w_ref, o_ref):
        x = x_ref[...]
        w = w_ref[...]
        pre = x * w[3:4, :]
        for j in (1, 2, 3):
            pre = pre + _shift_rows(x, j, True) * w[3 - j:4 - j, :]
        o_ref[...] = _silu(pre)

    return hosted_call(
        body, exchange, name="dn_conv_fwd", grid=(DN_QKV // LANE,),
        in_specs=[pl.BlockSpec((s, LANE), lambda j: (0, j)), pl.BlockSpec((8, LANE), lambda j: (0, j))],
        out_specs=[pl.BlockSpec((s, LANE), lambda j: (0, j))],
        out_shape=[jax.ShapeDtypeStruct((s, DN_QKV), F32)], sem=("parallel",),
    )(proj, conv_w)


def dn_conv_bwd(dact, proj, conv_w, dproj_in, dab, blk0, name):
    s = proj.shape[0]
    nblk = dact.shape[1] // LANE
    extra = 2 if dab is not None else 0

    def body(*refs):
        if dab is not None:
            da_ref, x_ref, w_ref, _, dab_ref, dp_ref, dw_ref = refs
        else:
            da_ref, x_ref, w_ref, _, dp_ref, dw_ref = refs
        j = pl.program_id(0)

        @pl.when(j < nblk)
        def _():
            x = x_ref[...]
            w = w_ref[...]
            xs = [_shift_rows(x, 3 - kk_, True) for kk_ in range(4)]
            pre = xs[0] * w[0:1, :]
            for kk_ in (1, 2, 3):
                pre = pre + xs[kk_] * w[kk_:kk_ + 1, :]
            dpre = da_ref[...] * _silu_grad(pre)
            dx = dpre * w[3:4, :]
            for jj in (1, 2, 3):
                dx = dx + _shift_rows(dpre, jj, False) * w[3 - jj:4 - jj, :]
            dp_ref[...] = dx
            rows = [jnp.sum(dpre * xs[kk_], axis=0, keepdims=True) for kk_ in range(4)]
            dw_ref[...] = jnp.concatenate(rows + [jnp.zeros((4, LANE), F32)], axis=0)

        if dab is not None:
            @pl.when(j == nblk)
            def _():
                dp_ref[...] = dab_ref[...]

            @pl.when(j == nblk + 1)
            def _():
                dp_ref[...] = jnp.zeros_like(dp_ref)

    cl = lambda j: jnp.minimum(j, nblk - 1)
    in_specs = [pl.BlockSpec((s, LANE), lambda j: (0, cl(j))),
                pl.BlockSpec((s, LANE), lambda j: (0, blk0 + cl(j))),
                pl.BlockSpec((8, LANE), lambda j: (0, blk0 + cl(j))),
                pl.BlockSpec(memory_space=pl.ANY)]
    args = [dact, proj, conv_w, dproj_in]
    if dab is not None:
        in_specs.append(pl.BlockSpec((s, LANE), lambda j: (0, 0)))
        args.append(dab)
    return pl.pallas_call(
        body, name=name, grid=(nblk + extra,), in_specs=in_specs,
        out_specs=[pl.BlockSpec((s, LANE), lambda j: (0, blk0 + j)), pl.BlockSpec((8, LANE), lambda j: (0, cl(j)))],
        out_shape=[jax.ShapeDtypeStruct(dproj_in.shape, F32), jax.ShapeDtypeStruct((8, dact.shape[1]), F32)],
        input_output_aliases={3: 0}, compiler_params=_params("arbitrary"),
    )(*args)


DN_GROUP = 8
DN_ROWS = DN_GROUP * DN_CHUNK


def dn_prep_fwd(act, proj, alog, dtb, exchange=None):
    s = act.shape[0]
    nc = s // DN_CHUNK
    C = DN_CHUNK

    def body(q_ref, k_ref, v_ref, ab_ref, al_ref, dt_ref, u0_ref, w_ref, qd_ref, kd_ref, qk_ref, cd_ref):
        qh = pl.program_id(0)
        al = al_ref[0:1, :]
        dt = dt_ref[0:1, :]
        chunks = lambda x: x.reshape(DN_GROUP, C, x.shape[-1])
        rows = lambda x: x.reshape(DN_ROWS, x.shape[-1])
        qt, kt, ab = chunks(q_ref[...]), chunks(k_ref[...]), chunks(ab_ref[...])
        for hv in range(2):
            cs = slice(hv * LANE, (hv + 1) * LANE)
            u0, w, qk, qd, kd, cd = _dn_chunk(qt, kt, chunks(v_ref[:, cs]), ab, al, dt, 2 * qh + hv)
            u0_ref[:, cs] = rows(u0)
            w_ref[:, cs] = rows(w)
            qd_ref[:, cs] = rows(qd)
            kd_ref[:, cs] = rows(kd)
            qk_ref[hv] = rows(qk)
            cd_ref[hv] = cd

    big = pl.BlockSpec((DN_ROWS, 2 * LANE), lambda h, g: (g, h))
    wide = jax.ShapeDtypeStruct((s, MIX_WIDTH), F32)
    return hosted_call(
        body, exchange, name="dn_prep_fwd", grid=(DN_QK_HEADS, s // DN_ROWS),
        in_specs=[pl.BlockSpec((DN_ROWS, LANE), lambda h, g: (g, h)),
                  pl.BlockSpec((DN_ROWS, LANE), lambda h, g: (g, DN_QK_HEADS + h)),
                  pl.BlockSpec((DN_ROWS, 2 * LANE), lambda h, g: (g, DN_QK_HEADS + h)),
                  pl.BlockSpec((DN_ROWS, LANE), lambda h, g: (g, DN_AB_BLK)),
                  pl.BlockSpec((8, LANE), lambda h, g: (0, 0)), pl.BlockSpec((8, LANE), lambda h, g: (0, 0))],
        out_specs=[big, big, big, big,
                   pl.BlockSpec((2, DN_ROWS, C), lambda h, g: (h, g, 0)),
                   pl.BlockSpec((2, DN_GROUP, 8, LANE), lambda h, g: (h, g, 0, 0))],
        out_shape=[wide, wide, wide, wide, jax.ShapeDtypeStruct((DN_V_HEADS, s, C), F32),
                   jax.ShapeDtypeStruct((DN_V_HEADS, nc, 8, LANE), F32)],
        sem=("parallel", "parallel"),
    )(act, act, act, proj, alog, dtb)


def dn_prep_bwd(act, proj, alog, dtb, du0, dw, dqd, dkd, dqk, dcd, exchange=None):
    s = act.shape[0]
    C = DN_CHUNK

    def body(q_ref, k_ref, v_ref, ab_ref, al_ref, dt_ref, du0_ref, dw_ref, dqd_ref, dkd_ref, dqk_ref, dcd_ref,
             dq_ref, dk_ref, dv_ref, dab_ref, dal_ref, ddt_ref):
        g_id = pl.program_id(0)
        qh = pl.program_id(1)
        al = al_ref[0:1, :]
        dt = dt_ref[0:1, :]

        @pl.when(qh == 0)
        def _():
            dab_ref[...] = jnp.zeros_like(dab_ref)

        @pl.when((qh == 0) & (g_id == 0))
        def _():
            dal_ref[...] = jnp.zeros_like(dal_ref)
            ddt_ref[...] = jnp.zeros_like(ddt_ref)

        chunks = lambda x: x.reshape(DN_GROUP, C, x.shape[-1])
        rows = lambda x: x.reshape(DN_ROWS, x.shape[-1])
        qt, kt, ab = chunks(q_ref[...]), chunks(k_ref[...]), chunks(ab_ref[...])
        dq_acc = jnp.zeros((DN_ROWS, LANE), F32)
        dk_acc = jnp.zeros((DN_ROWS, LANE), F32)
        for hv in range(2):
            cs = slice(hv * LANE, (hv + 1) * LANE)
            h = 2 * qh + hv
            f = lambda qt_, kt_, v_, ab_, a_, d_: _dn_chunk(qt_, kt_, v_, ab_, a_, d_, h)
            _, vjp = jax.vjp(f, qt, kt, chunks(v_ref[:, cs]), ab, al, dt)
            dq, dk, dv, dab, dal, ddt = vjp((chunks(du0_ref[:, cs]), chunks(dw_ref[:, cs]), chunks(dqk_ref[hv]),
                                             chunks(dqd_ref[:, cs]), chunks(dkd_ref[:, cs]), dcd_ref[hv]))
            dq_acc = dq_acc + rows(dq)
            dk_acc = dk_acc + rows(dk)
            dv_ref[:, cs] = rows(dv)
            dab_ref[...] += rows(dab)
            dal_ref[0:1, :] += dal
            ddt_ref[0:1, :] += ddt
        dq_ref[...] = dq_acc
        dk_ref[...] = dk_acc

    big = pl.BlockSpec((DN_ROWS, 2 * LANE), lambda g, h: (g, h))
    one = pl.BlockSpec((DN_ROWS, LANE), lambda g, h: (g, h))
    small = pl.BlockSpec((8, LANE), lambda g, h: (0, 0))
    return hosted_call(
        body, exchange, name="dn_prep_bwd", grid=(s // DN_ROWS, DN_QK_HEADS),
        in_specs=[one, pl.BlockSpec((DN_ROWS, LANE), lambda g, h: (g, DN_QK_HEADS + h)),
                  pl.BlockSpec((DN_ROWS, 2 * LANE), lambda g, h: (g, DN_QK_HEADS + h)),
                  pl.BlockSpec((DN_ROWS, LANE), lambda g, h: (g, DN_AB_BLK)), small, small,
                  big, big, big, big,
                  pl.BlockSpec((2, DN_ROWS, C), lambda g, h: (h, g, 0)),
                  pl.BlockSpec((2, DN_GROUP, 8, LANE), lambda g, h: (h, g, 0, 0))],
        out_specs=[one, one, big, pl.BlockSpec((DN_ROWS, LANE), lambda g, h: (g, 0)), small, small],
        out_shape=[jax.ShapeDtypeStruct((s, DN_QK_WIDTH), F32), jax.ShapeDtypeStruct((s, DN_QK_WIDTH), F32),
                   jax.ShapeDtypeStruct((s, MIX_WIDTH), F32), jax.ShapeDtypeStruct((s, LANE), F32),
                   jax.ShapeDtypeStruct((8, LANE), F32), jax.ShapeDtypeStruct((8, LANE), F32)],
        sem=("arbitrary", "arbitrary"),
    )(act, act, act, proj, alog, dtb, du0, dw, dqd, dkd, dqk, dcd)


DN_SCAN_HEADS = 2


def dn_scan_fwd(u0, w, qd, kd, qk, cd, width, exchange=None):
    s = u0.shape[0]
    nc = s // DN_CHUNK
    C = DN_CHUNK

    nh = DN_SCAN_HEADS
    heads = range(nh)
    cols = [slice(h * LANE, (h + 1) * LANE) for h in heads]

    def body(u0_ref, w_ref, qd_ref, kd_ref, qk_ref, cd_ref, o_ref, st_ref):
        def step(c, states):
            rs = pl.ds(pl.multiple_of(c * C, C), C)
            for h in heads:
                st_ref[h, c] = states[h]
            ws = [_dg(w_ref[rs, cols[h]], states[h], _NN) for h in heads]
            us = [u0_ref[rs, cols[h]] - ws[h] for h in heads]
            os_ = [_dg(qd_ref[rs, cols[h]], states[h], _NN) for h in heads]
            for h in heads:
                o_ref[rs, cols[h]] = os_[h] + _dg(qk_ref[h, rs, :], us[h], _NN)
            return tuple(cd_ref[h, c][0:1, :] * states[h] + _dg(kd_ref[rs, cols[h]], us[h], _TN) for h in heads)

        lax.fori_loop(0, nc, step, tuple(jnp.zeros((HEAD_DIM, HEAD_DIM), F32) for _ in heads))

    col = pl.BlockSpec((s, nh * LANE), lambda h: (0, h))
    return hosted_call(
        body, exchange, name="dn_scan_fwd", grid=(DN_V_HEADS // nh,),
        in_specs=[col, col, col, col, pl.BlockSpec((nh, s, C), lambda h: (h, 0, 0)),
                  pl.BlockSpec((nh, nc, 8, LANE), lambda h: (h, 0, 0, 0))],
        out_specs=[col, pl.BlockSpec((nh, nc, HEAD_DIM, HEAD_DIM), lambda h: (h, 0, 0, 0))],
        out_shape=[jax.ShapeDtypeStruct((s, width), F32),
                   jax.ShapeDtypeStruct((DN_V_HEADS, nc, HEAD_DIM, HEAD_DIM), F32)],
        sem=("parallel",),
    )(u0, w, qd, kd, qk, cd)


def dn_scan_bwd(do, u0, w, qd, kd, qk, cd, states, exchange=None):
    s = u0.shape[0]
    nc = s // DN_CHUNK
    C = DN_CHUNK

    nh = DN_SCAN_HEADS
    heads = range(nh)
    cols = [slice(h * LANE, (h + 1) * LANE) for h in heads]

    def body(do_ref, u0_ref, w_ref, qd_ref, kd_ref, qk_ref, cd_ref, st_ref,
             du0_ref, dw_ref, dqd_ref, dkd_ref, dqk_ref, dcd_ref):
        def step(i, dstates):
            c = nc - 1 - i
            rs = pl.ds(pl.multiple_of(c * C, C), C)
            states = [st_ref[h, c] for h in heads]
            gs = [do_ref[rs, cols[h]] for h in heads]
            w_cs = [w_ref[rs, cols[h]] for h in heads]
            qd_cs = [qd_ref[rs, cols[h]] for h in heads]
            cd_rows = [cd_ref[h, c][0:1, :] for h in heads]
            us = [u0_ref[rs, cols[h]] - _dg(w_cs[h], states[h], _NN) for h in heads]
            dus = [_dg(qk_ref[h, rs, :], gs[h], _TN) + _dg(kd_ref[rs, cols[h]], dstates[h], _NN) for h in heads]
            for h in heads:
                du0_ref[rs, cols[h]] = dus[h]
                dw_ref[rs, cols[h]] = -_dg(dus[h], states[h], _NT)
                dqd_ref[rs, cols[h]] = _dg(gs[h], states[h], _NT)
                dkd_ref[rs, cols[h]] = _dg(us[h], dstates[h], _NT)
                dqk_ref[h, rs, :] = _dg(gs[h], us[h], _NT)
                dcd_row = jnp.sum(states[h] * dstates[h], axis=0, keepdims=True)
                dcd_ref[h, c] = jnp.concatenate([dcd_row, jnp.zeros((7, LANE), F32)], axis=0)
            return tuple(cd_rows[h] * dstates[h] + _dg(qd_cs[h], gs[h], _TN) - _dg(w_cs[h], dus[h], _TN)
                         for h in heads)

        lax.fori_loop(0, nc, step, tuple(jnp.zeros((HEAD_DIM, HEAD_DIM), F32) for _ in heads))

    col = pl.BlockSpec((s, nh * LANE), lambda h: (0, h))
    qk_spec = pl.BlockSpec((nh, s, C), lambda h: (h, 0, 0))
    cd_spec = pl.BlockSpec((nh, nc, 8, LANE), lambda h: (h, 0, 0, 0))
    wide = jax.ShapeDtypeStruct((s, MIX_WIDTH), F32)
    return hosted_call(
        body, exchange, name="dn_scan_bwd", grid=(DN_V_HEADS // nh,),
        in_specs=[col, col, col, col, col, qk_spec, cd_spec,
                  pl.BlockSpec((nh, nc, HEAD_DIM, HEAD_DIM), lambda h: (h, 0, 0, 0))],
        out_specs=[col, col, col, col, qk_spec, cd_spec],
        out_shape=[wide, wide, wide, wide, jax.ShapeDtypeStruct((DN_V_HEADS, s, C), F32),
                   jax.ShapeDtypeStruct((DN_V_HEADS, nc, 8, LANE), F32)],
        sem=("parallel",),
    )(do, u0, w, qd, kd, qk, cd, states)


SB_T = 256
SB_GROUPS = (4, 2, 1)


def _sb_scores(q, kbs, diff, lims):
    zs = [_dg(q, kb, _NT) * (HEAD_DIM ** -0.5) for kb in kbs]
    masks = [diff < lim for lim in lims]
    lss = [_log_sigmoid(z) for z in zs]
    lrs = [jnp.where(m, ls - z, 0.0) for m, ls, z in zip(masks, lss, zs)]
    return masks, lss, lrs


def _sb_diff():
    return lax.broadcasted_iota(jnp.int32, (SB_T, SB_T), 1) - lax.broadcasted_iota(jnp.int32, (SB_T, SB_T), 0)


def _sb_loop(n_tiles, step, carry):
    done = 0
    for size in SB_GROUPS:
        groups = (n_tiles - done) // size
        carry = lax.fori_loop(0, groups, lambda p, c, s=size, d=done: step(d + p * s, s, c), carry)
        done = done + groups * size
    return carry


def sb_fwd(qkn, proj, v_blk0, width, exchange=None):
    s = qkn.shape[0]

    def body(q_ref, k_ref, v_ref, o_ref, lt_ref):
        i = pl.program_id(1)
        q = q_ref[...]
        diff = _sb_diff()
        after = (diff < 0).astype(BF16)

        def step(first, n, carry):
            run, acc = carry
            tiles = [first + t for t in range(n)]
            kss = [pl.ds(pl.multiple_of((i - t) * SB_T, SB_T), SB_T) for t in tiles]
            masks, lss, lrs = _sb_scores(q, [k_ref[ks, :] for ks in kss], diff, [t * SB_T for t in tiles])
            within = [_dg_exact_rhs(lr, after, _NN) for lr in lrs]
            sums = [jnp.sum(lr, axis=1, keepdims=True) for lr in lrs]
            for t in range(n):
                wts = jnp.where(masks[t], jnp.exp(lss[t] + (within[t] + run)), 0.0)
                acc = acc + _dg(wts, v_ref[kss[t], :], _NN)
                run = run + sums[t]
            return run, acc

        run, acc = _sb_loop(i + 1, step, (jnp.zeros((SB_T, 1), F32), jnp.zeros((SB_T, HEAD_DIM), F32)))
        o_ref[...] = acc
        lt_ref[0] = run

    return hosted_call(
        body, exchange, name="sb_fwd", grid=(SB_HEADS, s // SB_T),
        in_specs=[pl.BlockSpec((SB_T, LANE), lambda h, i: (i, h)),
                  pl.BlockSpec((s, LANE), lambda h, i: (0, SB_HEADS + h)),
                  pl.BlockSpec((s, LANE), lambda h, i: (0, v_blk0 + h))],
        out_specs=[pl.BlockSpec((SB_T, LANE), lambda h, i: (i, h)), pl.BlockSpec((1, SB_T, 1), lambda h, i: (h, i, 0))],
        out_shape=[jax.ShapeDtypeStruct((s, width), F32), jax.ShapeDtypeStruct((SB_HEADS, s, 1), F32)],
        sem=("parallel", "parallel"),
    )(qkn, qkn, proj)


def sb_bwd(qkn, proj, v_blk0, do, ltot, dproj_in):
    s = qkn.shape[0]

    def body(q_ref, k_ref, v_ref, do_ref, lt_ref, _, dq_ref, dk_ref, dv_ref):
        i = pl.program_id(1)

        @pl.when(i == 0)
        def _():
            dk_ref[...] = jnp.zeros_like(dk_ref)
            dv_ref[...] = jnp.zeros_like(dv_ref)

        q = q_ref[...]
        g = do_ref[...]
        ltot = lt_ref[0]
        diff = _sb_diff()
        upto = (diff >= 0).astype(BF16)
        before = (diff > 0).astype(BF16)

        def step(first, n, carry):
            plr, pdl, dq = carry
            tiles = [first + t for t in range(n)]
            kss = [pl.ds(pl.multiple_of(j * SB_T, SB_T), SB_T) for j in tiles]
            kbs = [k_ref[ks, :] for ks in kss]
            vbs = [v_ref[ks, :] for ks in kss]
            masks, lss, lrs = _sb_scores(q, kbs, diff, [(i - j) * SB_T for j in tiles])
            dwts = [_dg(g, vb, _NT) for vb in vbs]
            within = [_dg_exact_rhs(lr, upto, _NN) for lr in lrs]
            wtss, dls = [], []
            for t in range(n):
                wts = jnp.where(masks[t], jnp.exp(lss[t] + (ltot - (within[t] + plr))), 0.0)
                plr = plr + jnp.sum(lrs[t], axis=1, keepdims=True)
                wtss.append(wts)
                dls.append(dwts[t] * wts)
            dwithin = [_dg_exact_rhs(dl, before, _NN) for dl in dls]
            for t in range(n):
                sz = jnp.exp(lss[t])
                dz = jnp.where(masks[t], dls[t] * (1.0 - sz) - sz * (dwithin[t] + pdl), 0.0) * (HEAD_DIM ** -0.5)
                pdl = pdl + jnp.sum(dls[t], axis=1, keepdims=True)
                dk_ref[kss[t], :] += _dg(dz, q, _TN)
                dv_ref[kss[t], :] += _dg(wtss[t], g, _TN)
                dq = dq + _dg(dz, kbs[t], _NN)
            return plr, pdl, dq

        zero = jnp.zeros((SB_T, 1), F32)
        _, _, dq = _sb_loop(i + 1, step, (zero, zero, jnp.zeros((SB_T, HEAD_DIM), F32)))
        dq_ref[...] = dq

    tile = pl.BlockSpec((SB_T, LANE), lambda h, i: (i, h))
    colspec = pl.BlockSpec((s, LANE), lambda h, i: (0, h))
    return pl.pallas_call(
        body, name="sb_bwd", grid=(SB_HEADS, s // SB_T),
        in_specs=[tile, pl.BlockSpec((s, LANE), lambda h, i: (0, SB_HEADS + h)),
                  pl.BlockSpec((s, LANE), lambda h, i: (0, v_blk0 + h)), tile,
                  pl.BlockSpec((1, SB_T, 1), lambda h, i: (h, i, 0)), pl.BlockSpec(memory_space=pl.ANY)],
        out_specs=[tile, colspec, pl.BlockSpec((s, LANE), lambda h, i: (0, v_blk0 + h))],
        out_shape=[jax.ShapeDtypeStruct((s, MIX_WIDTH), F32), jax.ShapeDtypeStruct((s, MIX_WIDTH), F32),
                   jax.ShapeDtypeStruct(dproj_in.shape, F32)],
        input_output_aliases={5: 2}, compiler_params=_params("parallel", "arbitrary"),
    )(qkn, qkn, proj, do, ltot, dproj_in)


ROW_TILE = 256
HN_HEADS = 8


def head_norm_fwd(x, x_blk0, nblk, gains, out_dtype, out_width, name):
    s = x.shape[0]
    assert x_blk0 % HN_HEADS == 0 and nblk % (HN_HEADS * gains.shape[0]) == 0
    per = nblk // gains.shape[0] // HN_HEADS
    w = HN_HEADS * LANE

    def body(x_ref, g_ref, o_ref):
        gain = g_ref[0, 0:1, :]
        for j in range(HN_HEADS):
            cs = slice(j * LANE, (j + 1) * LANE)
            xv = x_ref[:, cs]
            r = lax.rsqrt(jnp.mean(xv * xv, axis=1, keepdims=True) + EPS)
            o_ref[:, cs] = (xv * r * gain).astype(out_dtype)

    return pl.pallas_call(
        body, name=name, grid=(nblk // HN_HEADS, s // ROW_TILE),
        in_specs=[pl.BlockSpec((ROW_TILE, w), lambda j, t: (t, x_blk0 // HN_HEADS + j)),
                  pl.BlockSpec((1, 8, LANE), lambda j, t: (j // per, 0, 0))],
        out_specs=pl.BlockSpec((ROW_TILE, w), lambda j, t: (t, j)),
        out_shape=jax.ShapeDtypeStruct((s, out_width), out_dtype), compiler_params=_params("parallel", "parallel"),
    )(x, gains)


def head_norm_bwd(dy, dy_blk0, x, x_blk0, nblk, gain, dst, dst_blk0, name):
    s = x.shape[0]

    def body(*refs):
        if dst is not None:
            dy_ref, x_ref, g_ref, _, dx_ref, dg_ref = refs
        else:
            dy_ref, x_ref, g_ref, dx_ref, dg_ref = refs

        @pl.when((pl.program_id(0) == 0) & (pl.program_id(1) == 0))
        def _():
            dg_ref[...] = jnp.zeros_like(dg_ref)

        gain = g_ref[0:1, :]
        dg = jnp.zeros((1, LANE), F32)
        for j in range(HN_HEADS):
            cs = slice(j * LANE, (j + 1) * LANE)
            xv = x_ref[:, cs]
            g = dy_ref[:, cs]
            r = lax.rsqrt(jnp.mean(xv * xv, axis=1, keepdims=True) + EPS)
            gy = g * gain
            dx_ref[:, cs] = r * gy - xv * (r * r * r) * jnp.mean(gy * xv, axis=1, keepdims=True)
            dg = dg + jnp.sum(g * xv * r, axis=0, keepdims=True)
        dg_ref[0:1, :] += dg

    assert dy_blk0 % HN_HEADS == 0 and x_blk0 % HN_HEADS == 0 and dst_blk0 % HN_HEADS == 0 and nblk % HN_HEADS == 0
    w = HN_HEADS * LANE
    in_specs = [pl.BlockSpec((ROW_TILE, w), lambda j, t: (t, dy_blk0 // HN_HEADS + j)),
                pl.BlockSpec((ROW_TILE, w), lambda j, t: (t, x_blk0 // HN_HEADS + j)),
                pl.BlockSpec((8, LANE), lambda j, t: (0, 0))]
    args = [dy, x, gain]
    aliases = {}
    if dst is not None:
        in_specs.append(pl.BlockSpec(memory_space=pl.ANY))
        args.append(dst)
        aliases = {3: 0}
        out0 = jax.ShapeDtypeStruct(dst.shape, F32)
    else:
        out0 = jax.ShapeDtypeStruct((s, (dst_blk0 + nblk) * LANE), F32)
    return pl.pallas_call(
        body, name=name, grid=(nblk // HN_HEADS, s // ROW_TILE), in_specs=in_specs,
        out_specs=[pl.BlockSpec((ROW_TILE, w), lambda j, t: (t, dst_blk0 // HN_HEADS + j)),
                   pl.BlockSpec((8, LANE), lambda j, t: (0, 0))],
        out_shape=[out0, jax.ShapeDtypeStruct((8, LANE), F32)],
        input_output_aliases=aliases, compiler_params=_params("arbitrary", "arbitrary"),
    )(*args)


def _xa_head(xq, kraw, v, qg, kg):
    q = xq * lax.rsqrt(jnp.mean(xq * xq, axis=1, keepdims=True) + EPS) * qg
    k = kraw * lax.rsqrt(jnp.mean(kraw * kraw, axis=1, keepdims=True) + EPS) * kg
    sc = mm_nt(q, k) * (XA_DIM ** -0.5)
    e = jnp.exp(sc - lax.stop_gradient(jnp.max(sc, axis=1, keepdims=True)))
    return mm(e / jnp.sum(e, axis=1, keepdims=True), v)


def xa_fwd(proj, xq_blk0, kv, qg, kg, cat):
    s = proj.shape[0]
    n_mem = kv.shape[0]

    def body(xq_ref, k_ref, v_ref, qg_ref, kg_ref, _, o_ref):
        o_ref[...] = _xa_head(xq_ref[...], k_ref[...], v_ref[...], qg_ref[0:1, :], kg_ref[0:1, :])

    gain = pl.BlockSpec((8, XA_DIM), lambda h, t: (0, 0))
    return pl.pallas_call(
        body, name="xa_fwd", grid=(XA_HEADS, s // ROW_TILE),
        in_specs=[pl.BlockSpec((ROW_TILE, XA_DIM), lambda h, t: (t, xq_blk0 // 2 + h)),
                  pl.BlockSpec((n_mem, XA_DIM), lambda h, t: (0, h)),
                  pl.BlockSpec((n_mem, XA_DIM), lambda h, t: (0, XA_HEADS + h)), gain, gain,
                  pl.BlockSpec(memory_space=pl.ANY)],
        out_specs=pl.BlockSpec((ROW_TILE, XA_DIM), lambda h, t: (t, MIX_WIDTH // XA_DIM + h)),
        out_shape=jax.ShapeDtypeStruct(cat.shape, F32), input_output_aliases={5: 0},
        compiler_params=_params("parallel", "parallel"),
    )(proj, kv, kv, qg, kg, cat)


def xa_bwd(proj, xq_blk0, kv, qg, kg, dcat, dproj_in):
    s = proj.shape[0]
    n_mem = kv.shape[0]

    def body(xq_ref, k_ref, v_ref, qg_ref, kg_ref, do_ref, _, dxq_ref, dk_ref, dv_ref, dqg_ref, dkg_ref):
        h = pl.program_id(0)
        t = pl.program_id(1)

        @pl.when(t == 0)
        def _():
            dk_ref[...] = jnp.zeros_like(dk_ref)
            dv_ref[...] = jnp.zeros_like(dv_ref)

        @pl.when((t == 0) & (h == 0))
        def _():
            dqg_ref[...] = jnp.zeros_like(dqg_ref)
            dkg_ref[...] = jnp.zeros_like(dkg_ref)

        _, vjp = jax.vjp(_xa_head, xq_ref[...], k_ref[...], v_ref[...], qg_ref[0:1, :], kg_ref[0:1, :])
        dxq, dk, dv, dqg, dkg = vjp(do_ref[...])
        dxq_ref[...] = dxq
        dk_ref[...] += dk
        dv_ref[...] += dv
        dqg_ref[0:1, :] += dqg
        dkg_ref[0:1, :] += dkg

    gain = pl.BlockSpec((8, XA_DIM), lambda h, t: (0, 0))
    kspec = pl.BlockSpec((n_mem, XA_DIM), lambda h, t: (0, h))
    vspec = pl.BlockSpec((n_mem, XA_DIM), lambda h, t: (0, XA_HEADS + h))
    return pl.pallas_call(
        body, name="xa_bwd", grid=(XA_HEADS, s // ROW_TILE),
        in_specs=[pl.BlockSpec((ROW_TILE, XA_DIM), lambda h, t: (t, xq_blk0 // 2 + h)), kspec, vspec, gain, gain,
                  pl.BlockSpec((ROW_TILE, XA_DIM), lambda h, t: (t, MIX_WIDTH // XA_DIM + h)),
                  pl.BlockSpec(memory_space=pl.ANY)],
        out_specs=[pl.BlockSpec((ROW_TILE, XA_DIM), lambda h, t: (t, xq_blk0 // 2 + h)), kspec, kspec, gain, gain],
        out_shape=[jax.ShapeDtypeStruct(dproj_in.shape, F32), jax.ShapeDtypeStruct((n_mem, XA_WIDTH), F32),
                   jax.ShapeDtypeStruct((n_mem, XA_WIDTH), F32), jax.ShapeDtypeStruct((8, XA_DIM), F32),
                   jax.ShapeDtypeStruct((8, XA_DIM), F32)],
        input_output_aliases={6: 0}, compiler_params=_params("arbitrary", "arbitrary"),
    )(proj, kv, kv, qg, kg, dcat, dproj_in)


GATE_ROWS = 1024


def gate_fwd(cat, proj, z_blk0):
    s = cat.shape[0]

    def body(c_ref, z_ref, y_ref):
        y_ref[...] = (c_ref[...] * _silu(z_ref[...])).astype(BF16)

    w = 2 * LANE
    rt = min(GATE_ROWS, s)
    return pl.pallas_call(
        body, name="gate_fwd", grid=(INNER // w, s // rt),
        in_specs=[pl.BlockSpec((rt, w), lambda j, t: (t, j)),
                  pl.BlockSpec((rt, w), lambda j, t: (t, z_blk0 // 2 + j))],
        out_specs=pl.BlockSpec((rt, w), lambda j, t: (t, j)),
        out_shape=jax.ShapeDtypeStruct((s, INNER), BF16), compiler_params=_params("parallel", "parallel"),
    )(cat, proj)


def gate_bwd(dy, cat, proj, z_blk0):
    s = cat.shape[0]

    def body(dy_ref, c_ref, z_ref, dc_ref, dz_ref):
        z = z_ref[...]
        g = dy_ref[...]
        dc_ref[...] = g * _silu(z)
        dz_ref[...] = g * c_ref[...] * _silu_grad(z)

    w = 2 * LANE
    rt = min(GATE_ROWS, s)
    tile = pl.BlockSpec((rt, w), lambda j, t: (t, j))
    ztile = pl.BlockSpec((rt, w), lambda j, t: (t, z_blk0 // 2 + j))
    return pl.pallas_call(
        body, name="gate_bwd", grid=(INNER // w, s // rt), in_specs=[tile, tile, ztile],
        out_specs=[tile, ztile],
        out_shape=[jax.ShapeDtypeStruct((s, INNER), F32), jax.ShapeDtypeStruct(proj.shape, F32)],
        compiler_params=_params("parallel", "parallel"),
    )(dy, cat, proj)


NORM_ROWS = 256


def rms_fwd(x, gain, with_transpose=False):
    s, d = x.shape

    def body(x_ref, g_ref, o_ref, *t_ref):
        xv = x_ref[...]
        r = lax.rsqrt(jnp.mean(xv * xv, axis=1, keepdims=True) + EPS)
        y = xv * r * g_ref[0:1, :]
        o_ref[...] = y.astype(BF16)
        if with_transpose:
            t_ref[0][...] = y.T.astype(BF16)

    tile = pl.BlockSpec((NORM_ROWS, d), lambda t: (t, 0))
    res = pl.pallas_call(
        body, name="rms_fwd", grid=(s // NORM_ROWS,),
        in_specs=[tile, pl.BlockSpec((8, d), lambda t: (0, 0))],
        out_specs=[tile] + ([pl.BlockSpec((d, NORM_ROWS), lambda t: (0, t))] if with_transpose else []),
        out_shape=[jax.ShapeDtypeStruct((s, d), BF16)] + ([jax.ShapeDtypeStruct((d, s), BF16)] if with_transpose else []),
        compiler_params=_params("parallel"),
    )(x, gain)
    return res if with_transpose else res[0]


def rms_bwd(dh, x, gain, dres):
    s, d = x.shape

    def body(*refs):
        if dres is not None:
            dh_ref, x_ref, g_ref, dr_ref, dx_ref, dg_ref = refs
        else:
            dh_ref, x_ref, g_ref, dx_ref, dg_ref = refs

        @pl.when(pl.program_id(0) == 0)
        def _():
            dg_ref[...] = jnp.zeros_like(dg_ref)

        xv = x_ref[...]
        g = dh_ref[...]
        r = lax.rsqrt(jnp.mean(xv * xv, axis=1, keepdims=True) + EPS)
        gy = g * g_ref[0:1, :]
        dx = r * gy - xv * (r * r * r) * jnp.mean(gy * xv, axis=1, keepdims=True)
        dx_ref[...] = dx + dr_ref[...] if dres is not None else dx
        dg_ref[0:1, :] += jnp.sum(g * xv * r, axis=0, keepdims=True)

    tile = pl.BlockSpec((NORM_ROWS, d), lambda t: (t, 0))
    gspec = pl.BlockSpec((8, d), lambda t: (0, 0))
    args = [dh, x, gain] + ([dres] if dres is not None else [])
    return pl.pallas_call(
        body, name="rms_bwd", grid=(s // NORM_ROWS,),
        in_specs=[tile, tile, gspec] + ([tile] if dres is not None else []),
        out_specs=[tile, gspec],
        out_shape=[jax.ShapeDtypeStruct((s, d), F32), jax.ShapeDtypeStruct((8, d), F32)],
        compiler_params=_params("arbitrary"),
    )(*args)


def loss_fwd_bwd(y, target):
    s, d = y.shape

    def body(y_ref, t_ref, l_ref, dy_ref):
        @pl.when(pl.program_id(0) == 0)
        def _():
            l_ref[...] = jnp.zeros_like(l_ref)

        err = y_ref[...] - t_ref[...]
        dy_ref[...] = err * (1.0 / d)
        part = 0.5 * jnp.sum(jnp.mean(err * err, axis=1, keepdims=True), axis=0, keepdims=True)
        r = lax.broadcasted_iota(jnp.int32, (8, LANE), 0)
        c = lax.broadcasted_iota(jnp.int32, (8, LANE), 1)
        l_ref[...] += jnp.where((r == 0) & (c == 0), part, 0.0)

    tile = pl.BlockSpec((NORM_ROWS, d), lambda t: (t, 0))
    return pl.pallas_call(
        body, name="loss", grid=(s // NORM_ROWS,), in_specs=[tile, tile],
        out_specs=[pl.BlockSpec((8, LANE), lambda t: (0, 0)), tile],
        out_shape=[jax.ShapeDtypeStruct((8, LANE), F32), jax.ShapeDtypeStruct((s, d), F32)],
        compiler_params=_params("arbitrary"),
    )(y, target)


def matmul(a, b, mode, out_dtype, tm, tn, tk, name, add=None, b_blocked=False, out_blocks=None, exchange=None):
    if b_blocked:
        nb, _, width = b.shape
        bshape = (b.shape[1], nb * width)
    else:
        bshape = b.shape
    if mode == "tn":
        (kdim, m), n = a.shape, bshape[1]
    else:
        (m, kdim), n = a.shape, (bshape[1] if mode == "nn" else bshape[0])
    tm, tn, tk = min(tm, m), min(tn, n), min(tk, kdim)
    assert m % tm == 0 and n % tn == 0 and kdim % tk == 0, (name, m, n, kdim)
    nk = kdim // tk
    dims = {"nn": _NN, "nt": _NT, "tn": _TN}[mode]

    def body(*refs):
        if add is not None:
            a_ref, b_ref, add_ref, o_ref, acc_ref = refs
        else:
            a_ref, b_ref, o_ref, acc_ref = refs
        k = pl.program_id(2)

        @pl.when(k == 0)
        def _():
            acc_ref[...] = jnp.zeros_like(acc_ref)

        acc_ref[...] += _dg(a_ref[...], b_ref[...], dims)

        @pl.when(k == nk - 1)
        def _():
            r = acc_ref[...]
            if add is not None:
                r = r + add_ref[...]
            o_ref[...] = r.astype(out_dtype)

    a_spec = pl.BlockSpec((tk, tm), lambda i, j, k: (k, i)) if mode == "tn" else pl.BlockSpec((tm, tk), lambda i, j, k: (i, k))
    if b_blocked and mode == "nn":
        per = width // tn
        assert width % tn == 0
        b_spec = pl.BlockSpec((None, tk, tn), lambda i, j, k: (j // per, k, j % per))
    elif b_blocked and mode == "nt":
        per = width // tk
        assert width % tk == 0
        b_spec = pl.BlockSpec((None, tn, tk), lambda i, j, k: (k // per, j, k % per))
    elif mode == "nt":
        b_spec = pl.BlockSpec((tn, tk), lambda i, j, k: (j, k))
    else:
        assert not b_blocked
        b_spec = pl.BlockSpec((tk, tn), lambda i, j, k: (k, j))
    add_spec = pl.BlockSpec((tm, tn), lambda i, j, k: (i, j))
    if out_blocks is not None:
        operb = (n // out_blocks) // tn
        assert (n // out_blocks) % tn == 0 and add is None
        o_spec = pl.BlockSpec((None, tm, tn), lambda i, j, k: (j // operb, i, j % operb))
        out_shape = jax.ShapeDtypeStruct((out_blocks, m, n // out_blocks), out_dtype)
    else:
        o_spec = add_spec
        out_shape = jax.ShapeDtypeStruct((m, n), out_dtype)
    res = hosted_call(
        body, exchange, name=name, grid=(m // tm, n // tn, nk),
        in_specs=[a_spec, b_spec] + ([add_spec] if add is not None else []), out_specs=[o_spec],
        out_shape=[out_shape], scratch_shapes=[pltpu.VMEM((tm, tn), F32)],
        sem=("parallel", "parallel", "arbitrary"),
    )(*([a, b] + ([add] if add is not None else [])))
    return res[0] if exchange is None else res


_HBM = pl.BlockSpec(memory_space=pltpu.HBM)


def _me():
    return lax.axis_index("x"), lax.axis_index("y"), lax.axis_index("c")


def _flat(p):
    return 4 * p[0] + 2 * p[1] + p[2]


def _flip(p, r):
    return tuple((1 - v) if (r >> (2 - a)) & 1 else v for a, v in enumerate(p))


class Exchange:
    def __init__(self, srcs, out_shapes, sems, start, finish, alias=None):
        self.srcs, self.out_shapes, self.sems = list(srcs), list(out_shapes), list(sems)
        self.start, self.finish, self.alias = start, finish, dict(alias or {})


def hosted_call(body, exchange, *, name, grid, in_specs, out_specs, out_shape, scratch_shapes=(),
                input_output_aliases=None, sem=()):
    in_specs, out_specs, out_shape = list(in_specs), list(out_specs), list(out_shape)
    scratch_shapes = list(scratch_shapes)
    aliases = input_output_aliases or {}
    if exchange is None:
        call = pl.pallas_call(body, name=name, grid=grid, in_specs=in_specs, out_specs=out_specs, out_shape=out_shape,
                              scratch_shapes=scratch_shapes, input_output_aliases=aliases, compiler_params=_params(*sem))
        return lambda *args: list(call(*args))
    ni, no, ns = len(in_specs), len(out_specs), len(scratch_shapes)
    xi, xo = len(exchange.srcs), len(exchange.out_shapes)

    def wrapped(*refs):
        ins, refs = refs[:ni], refs[ni:]
        xin, refs = refs[:xi], refs[xi:]
        outs, refs = refs[:no], refs[no:]
        xout, refs = refs[:xo], refs[xo:]
        scr, xsem = refs[:ns], refs[ns:]
        first = functools.reduce(lambda p, q: p & q, [pl.program_id(d) == 0 for d in range(len(grid))])
        last = functools.reduce(lambda p, q: p & q, [pl.program_id(d) == grid[d] - 1 for d in range(len(grid))])

        @pl.when(first)
        def _():
            exchange.start(xin, xout, xsem)

        body(*ins, *outs, *scr)

        @pl.when(last)
        def _():
            exchange.finish(xin, xout, xsem)

    call = pl.pallas_call(
        wrapped, name=name, grid=grid, in_specs=in_specs + [_HBM] * xi, out_specs=out_specs + [_HBM] * xo,
        out_shape=out_shape + exchange.out_shapes, scratch_shapes=scratch_shapes + exchange.sems,
        input_output_aliases={**aliases, **{ni + i: no + j for i, j in exchange.alias.items()}},
        compiler_params=_params(*(("arbitrary",) * len(grid))))
    return lambda *args: list(call(*args, *exchange.srcs))


def run_exchange(exchange, name):
    xi, xo = len(exchange.srcs), len(exchange.out_shapes)

    def body(*refs):
        exchange.start(refs[:xi], refs[xi:xi + xo], refs[xi + xo:])
        exchange.finish(refs[:xi], refs[xi:xi + xo], refs[xi + xo:])

    return list(pl.pallas_call(body, name=name, in_specs=[_HBM] * xi, out_specs=[_HBM] * xo,
                               out_shape=exchange.out_shapes, scratch_shapes=exchange.sems,
                               input_output_aliases=exchange.alias)(*exchange.srcs))


def gather_exchange(shards, rows=None, into=None):
    n = len(shards)

    def parts(srcs, outs, sems):
        send_sems, recv_sems, local_sems = sems
        me = _me()
        x, y, c = me
        chips = [(1 - x, y), (x, 1 - y), (1 - x, 1 - y)]

        def place(a, block):
            dst = outs[a].at[_flat(block)]
            return dst if rows is None else dst.at[pl.ds(rows[0], shards[a].shape[0])]

        def copy(a, k, block, to, src=None):
            dst = place(a, block)
            return pltpu.make_async_remote_copy(src_ref=dst if src is None else src, dst_ref=dst,
                                                send_sem=send_sems.at[a, k], recv_sem=recv_sems.at[a, k],
                                                device_id=to, device_id_type=MESH)

        mine = [pltpu.make_async_copy(srcs[a], place(a, me), local_sems.at[a]) for a in range(n)]
        own = []
        for a in range(n):
            own.append(copy(a, 0, me, (x, y, 1 - c), src=srcs[a]))
            own += [copy(a, 1 + j, me, (*chip, c), src=srcs[a]) for j, chip in enumerate(chips)]
        return me, chips, copy, mine, own

    def start(srcs, outs, sems):
        _, _, _, mine, own = parts(srcs, outs, sems)
        for cp in mine + own:
            cp.start()

    def finish(srcs, outs, sems):
        me, chips, copy, mine, own = parts(srcs, outs, sems)
        x, y, c = me
        passed = []
        for j, chip in enumerate(chips):
            for a in range(n):
                copy(a, 1 + j, (*chip, c), me).wait_recv()
                fwd = copy(a, 4 + j, (*chip, c), (x, y, 1 - c))
                fwd.start()
                passed.append(fwd)
        for a in range(n):
            copy(a, 0, (x, y, 1 - c), me).wait_recv()
            for j, chip in enumerate(chips):
                copy(a, 4 + j, (*chip, 1 - c), me).wait_recv()
        for cp in own + passed:
            cp.wait_send()
        for cp in mine:
            cp.wait()

    dma = pltpu.SemaphoreType.DMA
    full = lambda s: s.shape if rows is None else (rows[1],) + s.shape[1:]
    return Exchange(list(shards) + list(into or []), [jax.ShapeDtypeStruct((N_DEV,) + full(s), s.dtype) for s in shards],
                    [dma((n, 7)), dma((n, 7)), dma((n,))], start, finish,
                    alias={n + a: a for a in range(n)} if into else None)


def pair_exchange(srcs):
    n = len(srcs)

    def copies(srcs_, outs, sems):
        send_sems, recv_sems = sems
        x, y, c = _me()
        return [pltpu.make_async_remote_copy(src_ref=srcs_[a].at[:, 1 - c], dst_ref=outs[a], send_sem=send_sems.at[a],
                                             recv_sem=recv_sems.at[a], device_id=(x, y, 1 - c), device_id_type=MESH)
                for a in range(n)]

    def start(srcs_, outs, sems):
        for cp in copies(srcs_, outs, sems):
            cp.start()

    def finish(srcs_, outs, sems):
        for cp in copies(srcs_, outs, sems):
            cp.wait()

    dma = pltpu.SemaphoreType.DMA
    return Exchange(srcs, [jax.ShapeDtypeStruct((4,) + s.shape[2:], s.dtype) for s in srcs], [dma((n,)), dma((n,))],
                    start, finish)


def pair_sum(src, half, name):
    _, _, rows, cols = src.shape
    tr = min(rows, 256)

    def body(x_ref, h_ref, o_ref):
        c = lax.axis_index("c")
        o_ref[0] = (x_ref[0, c].astype(F32) + h_ref[0].astype(F32)).astype(BF16)

    return pl.pallas_call(
        body, name=name, grid=(4, rows // tr),
        in_specs=[pl.BlockSpec((1, 2, tr, cols), lambda ch, t: (ch, 0, t, 0)),
                  pl.BlockSpec((1, tr, cols), lambda ch, t: (ch, t, 0))],
        out_specs=pl.BlockSpec((1, tr, cols), lambda ch, t: (ch, t, 0)),
        out_shape=jax.ShapeDtypeStruct(half.shape, BF16), compiler_params=_params("parallel", "parallel"),
    )(src, half)


def merge_exchanges(a, b):
    ns, no, nm = len(a.srcs), len(a.out_shapes), len(a.sems)

    def start(srcs, outs, sems):
        a.start(srcs[:ns], outs[:no], sems[:nm])
        b.start(srcs[ns:], outs[no:], sems[nm:])

    def finish(srcs, outs, sems):
        a.finish(srcs[:ns], outs[:no], sems[:nm])
        b.finish(srcs[ns:], outs[no:], sems[nm:])

    alias = {**a.alias, **{ns + i: no + j for i, j in b.alias.items()}}
    return Exchange(a.srcs + b.srcs, a.out_shapes + b.out_shapes, a.sems + b.sems, start, finish, alias)


def chip_exchange(parts, slots, recv_shapes, rows=None, into=None):
    n = len(parts)
    win = (lambda ref: ref) if rows is None else (lambda ref: ref.at[pl.ds(rows[0], rows[1])])

    def plan(srcs, outs, sems):
        send_sems, recv_sems, local_sems = sems
        x, y, c = _me()
        chip = 2 * x + y
        mine = [pltpu.make_async_copy(win(srcs[a].at[chip]), win(outs[slots[a][0]].at[chip, slots[a][1]]),
                                      local_sems.at[a]) for a in range(n)]
        sends, arrivals = [], []
        for r in (1, 2, 3):
            px = (1 - x) if r & 2 else x
            py = (1 - y) if r & 1 else y
            for a in range(n):
                ri, layer = slots[a]
                sends.append(pltpu.make_async_remote_copy(
                    src_ref=win(srcs[a].at[2 * px + py]), dst_ref=win(outs[ri].at[chip, layer]),
                    send_sem=send_sems.at[a, r - 1],
                    recv_sem=recv_sems.at[a, r - 1], device_id=(px, py, c), device_id_type=MESH))
                land = win(outs[ri].at[2 * px + py, layer])
                arrivals.append(pltpu.make_async_remote_copy(
                    src_ref=land, dst_ref=land, send_sem=send_sems.at[a, r - 1], recv_sem=recv_sems.at[a, r - 1],
                    device_id=(px, py, c), device_id_type=MESH))
        return mine, sends, arrivals

    def start(srcs, outs, sems):
        mine, sends, _ = plan(srcs, outs, sems)
        for cp in mine + sends:
            cp.start()

    def finish(srcs, outs, sems):
        mine, sends, arrivals = plan(srcs, outs, sems)
        for cp in arrivals:
            cp.wait_recv()
        for cp in sends:
            cp.wait_send()
        for cp in mine:
            cp.wait()

    dma = pltpu.SemaphoreType.DMA
    return Exchange(list(parts) + list(into or []), [jax.ShapeDtypeStruct(s, BF16) for s in recv_shapes],
                    [dma((n, 3)), dma((n, 3)), dma((n,))], start, finish,
                    alias={n + k: k for k in range(len(recv_shapes))} if into else None)


SMALL_ROWS = 24


def all_reduce_small(pack):
    def body(p_ref, o_ref, buf, send_sems, recv_sems):
        me = _me()
        buf[_flat(me)] = p_ref[...]
        sent = []
        for r in range(1, N_DEV):
            peer = _flip(me, r)
            cp = pltpu.make_async_remote_copy(src_ref=p_ref, dst_ref=buf.at[_flat(me)], send_sem=send_sems.at[r - 1],
                                              recv_sem=recv_sems.at[r - 1], device_id=peer, device_id_type=MESH)
            cp.start()
            sent.append(cp)
        for r in range(1, N_DEV):
            peer = _flip(me, r)
            land = buf.at[_flat(peer)]
            pltpu.make_async_remote_copy(src_ref=land, dst_ref=land, send_sem=send_sems.at[r - 1],
                                         recv_sem=recv_sems.at[r - 1], device_id=peer, device_id_type=MESH).wait_recv()
        for cp in sent:
            cp.wait_send()
        acc = buf[0]
        for d in range(1, N_DEV):
            acc = acc + buf[d]
        o_ref[...] = acc

    vm = pl.BlockSpec(memory_space=pltpu.VMEM)
    return pl.pallas_call(
        body, name="all_reduce_small", in_specs=[vm], out_specs=vm,
        out_shape=jax.ShapeDtypeStruct(pack.shape, F32),
        scratch_shapes=[pltpu.VMEM((N_DEV,) + pack.shape, F32), pltpu.SemaphoreType.DMA((7,)),
                        pltpu.SemaphoreType.DMA((7,))],
    )(pack)


def _adamw(w, g, m, v):
    m = ADAM_B1 * m + (1.0 - ADAM_B1) * g
    v = ADAM_B2 * v + (1.0 - ADAM_B2) * (g * g)
    m_hat = m / (1.0 - ADAM_B1 ** ADAM_STEP)
    v_hat = v / (1.0 - ADAM_B2 ** ADAM_STEP)
    delta = -ADAM_LR * (m_hat / (jnp.sqrt(v_hat) + ADAM_EPS) + ADAM_WD * w)
    return delta, m, v


ADAM_ROWS = 128


def reduce_adamw(recv, w, m, v, name):
    nl, rows, cols = w.shape
    nslot, cp = recv.shape[0], recv.shape[3]

    def body(r_ref, w_ref, m_ref, v_ref, g_ref, d_ref, mo_ref, vo_ref):
        g = r_ref[0, 0].astype(F32)
        for slot in range(1, nslot):
            g = g + r_ref[slot, 0].astype(F32)
        if cp != cols:
            g = g[:, :cols]
        delta, m_new, v_new = _adamw(w_ref[0], g, m_ref[0], v_ref[0])
        g_ref[0] = g
        d_ref[0] = delta
        mo_ref[0] = m_new
        vo_ref[0] = v_new

    tile = pl.BlockSpec((1, ADAM_ROWS, cols), lambda l, t: (l, t, 0))
    out = jax.ShapeDtypeStruct(w.shape, F32)
    return pl.pallas_call(
        body, name=name, grid=(nl, rows // ADAM_ROWS),
        in_specs=[pl.BlockSpec((nslot, 1, ADAM_ROWS, cp), lambda l, t: (0, l, t, 0)), tile, tile, tile],
        out_specs=[tile, tile, tile, tile], out_shape=[out, out, out, out],
        compiler_params=_params("parallel", "parallel"),
    )(recv, w, m, v)


def reduce_adamw_t(recv, w_t, m_t, v_t, name):
    cols, rows = w_t.shape
    nslot, cp = recv.shape[0], recv.shape[3]
    tr = 256

    def body(r_ref, w_ref, m_ref, v_ref, g_ref, d_ref, mo_ref, vo_ref):
        g = r_ref[0, 0].astype(F32)
        for slot in range(1, nslot):
            g = g + r_ref[slot, 0].astype(F32)
        g_ref[...] = g.T[:cols, :]
        delta, m_new, v_new = _adamw(w_ref[...], g_ref[...], m_ref[...], v_ref[...])
        d_ref[...] = delta
        mo_ref[...] = m_new
        vo_ref[...] = v_new

    tile = pl.BlockSpec((cols, tr), lambda t: (0, t))
    out = jax.ShapeDtypeStruct((cols, rows), F32)
    return pl.pallas_call(
        body, name=name, grid=(rows // tr,),
        in_specs=[pl.BlockSpec((nslot, 1, tr, cp), lambda t: (0, 0, t, 0)), tile, tile, tile],
        out_specs=[tile, tile, tile, tile], out_shape=[out, out, out, out], compiler_params=_params("parallel"),
    )(recv, w_t, m_t, v_t)


def adamw_small(g, w, m, v):
    def body(g_ref, w_ref, m_ref, v_ref, d_ref, mo_ref, vo_ref):
        d_ref[...], mo_ref[...], vo_ref[...] = _adamw(w_ref[...], g_ref[...], m_ref[...], v_ref[...])

    out = jax.ShapeDtypeStruct(g.shape, F32)
    return pl.pallas_call(body, name="adamw_small", out_shape=[out, out, out])(g, w, m, v)


def _row8(v):
    return jnp.pad(v.reshape(1, -1).astype(F32), ((0, 7), (0, 0)))


def _row8_lanes(v, width=LANE):
    return jnp.pad(v.reshape(1, -1).astype(F32), ((0, 7), (0, width - v.size)))


def _pack_rows(parts):
    rows = []
    for p in parts:
        p = p.reshape(-1).astype(F32)
        nrow = -(-p.size // D_MODEL)
        rows.append(jnp.pad(p, (0, nrow * D_MODEL - p.size)).reshape(nrow, D_MODEL))
    out = jnp.concatenate(rows, axis=0)
    return jnp.pad(out, ((0, SMALL_ROWS - out.shape[0]), (0, 0)))


def _unpack_rows(pack, shapes):
    out, r = [], 0
    for shp in shapes:
        size = 1
        for d in shp:
            size *= d
        nrow = -(-size // D_MODEL)
        out.append(pack[r:r + nrow].reshape(-1)[:size].reshape(shp))
        r += nrow
    return out


def _dn_weight_layout(gathered):
    split = DN_QKV + 2 * DN_V_HEADS
    pieces = []
    for d in range(N_DEV):
        lo, hi = d * DN_SHARD, (d + 1) * DN_SHARD
        if lo < split < hi:
            pieces += [gathered[d, :, :split - lo], jnp.zeros((D_MODEL, DN_COLS - DN_PROJ), gathered.dtype),
                       gathered[d, :, split - lo:DN_SHARD]]
        else:
            pieces.append(gathered[d, :, :DN_SHARD])
    return jnp.concatenate(pieces, axis=1)


def _dn_grad_blocks(dw):
    split = DN_QKV + 2 * DN_V_HEADS
    gap = DN_COLS - DN_PROJ
    local = lambda c: c if c <= split else c + gap
    zeros = jnp.zeros((D_MODEL, DN_SHARD_PAD - DN_SHARD), dw.dtype)
    blocks = []
    for d in range(N_DEV):
        lo, hi = d * DN_SHARD, (d + 1) * DN_SHARD
        if lo < split < hi:
            parts = [dw[:, lo:split], dw[:, split + gap:hi + gap]]
        else:
            parts = [dw[:, local(lo):local(lo) + DN_SHARD]]
        blocks.append(jnp.concatenate(parts + [zeros], axis=1))
    return jnp.stack(blocks)


def kernel(x, mem, norm_g, mem_norm_g, mem_w_kv, xa_q_norm_g, xa_k_norm_g, w_out, dn_w_in, dn_conv_w, dn_a_log, dn_dt_bias, dn_out_norm_g, sb_w_in, sb_q_norm_g, sb_k_norm_g, loss_target, m_norm_g, m_mem_norm_g, m_mem_w_kv, m_xa_q_norm_g, m_xa_k_norm_g, m_w_out, m_dn_w_in, m_dn_conv_w, m_dn_a_log, m_dn_dt_bias, m_dn_out_norm_g, m_sb_w_in, m_sb_q_norm_g, m_sb_k_norm_g, v_norm_g, v_mem_norm_g, v_mem_w_kv, v_xa_q_norm_g, v_xa_k_norm_g, v_w_out, v_dn_w_in, v_dn_conv_w, v_dn_a_log, v_dn_dt_bias, v_dn_out_norm_g, v_sb_w_in, v_sb_q_norm_g, v_sb_k_norm_g):
    x0, memv, target = x[0], mem[0], loss_target[0]
    my_dev = 4 * lax.axis_index("x") + 2 * lax.axis_index("y") + lax.axis_index("c")

    dn_shard = jnp.pad(dn_w_in[0].astype(BF16), ((0, 0), (0, DN_SHARD_PAD - DN_SHARD)))
    w_out_b = [w_out[i].astype(BF16) for i in range(2)]
    w_kv_b = [mem_w_kv[i].astype(BF16) for i in range(2)]
    conv_shard = jnp.pad(dn_conv_w[0], ((0, 4), (0, 0)))
    g_dn, g_conv = run_exchange(gather_exchange([dn_shard, conv_shard]), "gather_first")
    w_dn = _dn_weight_layout(g_dn)
    conv_w = jnp.transpose(g_conv, (1, 0, 2)).reshape(8, DN_QKV)

    ng = [_row8(norm_g[0]), _row8(norm_g[1])]
    mem_g = _row8(mem_norm_g)
    xqg = [_row8(xa_q_norm_g[0]), _row8(xa_q_norm_g[1])]
    xkg = [_row8(xa_k_norm_g[0]), _row8(xa_k_norm_g[1])]
    alog, dtb = _row8_lanes(dn_a_log[0]), _row8_lanes(dn_dt_bias[0])
    out_g, sbq_g, sbk_g = _row8(dn_out_norm_g[0]), _row8(sb_q_norm_g[0]), _row8(sb_k_norm_g[0])

    mem_n = rms_fwd(memv, mem_g)
    h0, h0_t = rms_fwd(x0, ng[0], with_transpose=True)
    proj0, g_wo0 = matmul(h0, w_dn, "nn", F32, 1024, 1152, 2048, "proj_dn", exchange=gather_exchange([w_out_b[0]]))
    act, g_kv0 = dn_conv_fwd(proj0, conv_w, exchange=gather_exchange([w_kv_b[0]]))
    sb_shard, half = sb_w_in[0].astype(BF16), 9 * D_MODEL // 16
    u0, w_, qd, kd, qk, cd, w_sb = dn_prep_fwd(
        act, proj0, alog, dtb, exchange=gather_exchange([sb_shard[:half]], rows=(0, D_MODEL)))
    o_raw, states, w_sb = dn_scan_fwd(
        u0, w_, qd, kd, qk, cd, MIX_WIDTH,
        exchange=gather_exchange([sb_shard[half:]], rows=(half, D_MODEL), into=[w_sb]))
    w_o = [g_wo0.reshape(INNER, D_MODEL), None]
    w_kv = [g_kv0.reshape(D_MODEL, 2 * XA_WIDTH), None]
    kv = [matmul(mem_n, w_kv[0], "nn", F32, 256, 1024, 2048, "kv0"), None]
    cat0 = head_norm_fwd(o_raw, 0, DN_V_HEADS, out_g[None], F32, INNER, "dn_out_norm")
    cat0 = xa_fwd(proj0, DN_XQ_BLK, kv[0], xqg[0], xkg[0], cat0)
    y0 = gate_fwd(cat0, proj0, DN_Z_BLK)
    x1 = matmul(y0, w_o[0], "nn", F32, 1024, 1024, 2048, "out_proj0", add=x0)

    h1, h1_t = rms_fwd(x1, ng[1], with_transpose=True)
    proj1 = matmul(h1, w_sb, "nn", F32, 1024, 896, 2048, "proj_sb", b_blocked=True)
    qkn = head_norm_fwd(proj1, 0, 2 * SB_HEADS, jnp.stack([sbq_g, sbk_g]), BF16, 2 * MIX_WIDTH, "sb_qk_norm")
    cat1, ltot, g_wo1, g_kv1 = sb_fwd(qkn, proj1, 2 * SB_HEADS, INNER,
                                      exchange=gather_exchange([w_out_b[1], w_kv_b[1]]))
    w_o[1] = g_wo1.reshape(INNER, D_MODEL)
    w_kv[1] = g_kv1.reshape(D_MODEL, 2 * XA_WIDTH)
    kv[1] = matmul(mem_n, w_kv[1], "nn", F32, 256, 1024, 2048, "kv1")
    cat1 = xa_fwd(proj1, SB_XQ_BLK, kv[1], xqg[1], xkg[1], cat1)
    y1 = gate_fwd(cat1, proj1, SB_Z_BLK)
    x2 = matmul(y1, w_o[1], "nn", F32, 1024, 1024, 2048, "out_proj1", add=x1)
    loss_part, dx2 = loss_fwd_bwd(x2, target)

    dy1 = matmul(dx2, w_o[1], "nt", F32, 1024, 1024, 2048, "d_y1")
    dw_o1 = matmul(y1, dx2, "tn", BF16, 1024, 1024, 2048, "d_w_out1")
    dcat1, dproj1 = gate_bwd(dy1, cat1, proj1, SB_Z_BLK)
    dproj1, dxk1, dxv1, dxqg1, dxkg1 = xa_bwd(proj1, SB_XQ_BLK, kv[1], xqg[1], xkg[1], dcat1, dproj1)
    dqn, dkn, dproj1 = sb_bwd(qkn, proj1, 2 * SB_HEADS, dcat1, ltot, dproj1)
    dproj1, d_sbq = head_norm_bwd(dqn, 0, proj1, 0, SB_HEADS, sbq_g, dproj1, 0, "sb_q_norm_bwd")
    dproj1, d_sbk = head_norm_bwd(dkn, 0, proj1, SB_HEADS, SB_HEADS, sbk_g, dproj1, SB_HEADS, "sb_k_norm_bwd")
    dw_sb = matmul(h1_t, dproj1, "nn", BF16, 1024, 896, 2048, "d_w_sb", out_blocks=N_DEV)
    dh1 = matmul(dproj1, w_sb, "nt", F32, 1024, 1024, 1792, "d_h1", b_blocked=True)
    dx1, d_ng1 = rms_bwd(dh1, x1, ng[1], dx2)

    by_owner = lambda g, rows: g.reshape(4, 2, rows, g.size // (N_DEV * rows))
    sb_grad = dw_sb.reshape(4, 2, D_MODEL, SB_PROJ // N_DEV)
    dy0, sb_half = matmul(dx1, w_o[0], "nt", F32, 1024, 1024, 2048, "d_y0", exchange=pair_exchange([sb_grad]))
    sb_sum = pair_sum(sb_grad, sb_half, "pair_sum_sb")
    dw_o0 = matmul(y0, dx1, "tn", BF16, 1024, 1024, 2048, "d_w_out0")
    dcat0, dproj0 = gate_bwd(dy0, cat0, proj0, DN_Z_BLK)
    dproj0, dxk0, dxv0, dxqg0, dxkg0 = xa_bwd(proj0, DN_XQ_BLK, kv[0], xqg[0], xkg[0], dcat0, dproj0)

    dkv = [jnp.concatenate([dxk0, dxv0], axis=1), jnp.concatenate([dxk1, dxv1], axis=1)]
    dw_kv = [matmul(mem_n, dkv[i], "tn", BF16, 1024, 1024, 256, f"d_w_kv{i}") for i in range(2)]
    dmem_n = matmul(dkv[0], w_kv[0], "nt", F32, 256, 1024, 2048, "d_mem_n0")
    dmem_n = matmul(dkv[1], w_kv[1], "nt", F32, 256, 1024, 2048, "d_mem_n1", add=dmem_n)
    _, d_memg = rms_bwd(dmem_n, memv, mem_g, None)

    grads = [by_owner(dw_o0, INNER // N_DEV), by_owner(dw_o1, INNER // N_DEV),
             by_owner(dw_kv[0], D_MODEL // N_DEV), by_owner(dw_kv[1], D_MODEL // N_DEV)]
    sb_recv_shape = [(4, 1, D_MODEL, SB_PROJ // N_DEV)]
    sb_first = D_MODEL // 2
    do_raw, d_outg = head_norm_bwd(dcat0, 0, o_raw, 0, DN_V_HEADS, out_g, None, 0, "dn_out_norm_bwd")
    du0, dw_, dqd, dkd, dqk, dcd, *halves, recv_sb = dn_scan_bwd(
        do_raw, u0, w_, qd, kd, qk, cd, states,
        exchange=merge_exchanges(pair_exchange(grads),
                                 chip_exchange([sb_sum], [(0, 0)], sb_recv_shape, rows=(0, sb_first))))
    sums = [pair_sum(g, h, f"pair_sum{i}") for i, (g, h) in enumerate(zip(grads, halves))]
    to_chips = merge_exchanges(
        chip_exchange(sums, [(0, 0), (0, 1), (1, 0), (1, 1)],
                      [(4, 2, INNER // N_DEV, D_MODEL), (4, 2, D_MODEL // N_DEV, 2 * XA_WIDTH)]),
        chip_exchange([sb_sum], [(0, 0)], sb_recv_shape, rows=(sb_first, D_MODEL - sb_first), into=[recv_sb]))
    dq_a, dk_a, dv_a, dab, d_alog, d_dtb, recv_wo, recv_kv, recv_sb = dn_prep_bwd(
        act, proj0, alog, dtb, du0, dw_, dqd, dkd, dqk, dcd, exchange=to_chips)
    dproj0, dcw_q = dn_conv_bwd(dq_a, proj0, conv_w, dproj0, None, 0, "dn_conv_bwd_q")
    dproj0, dcw_k = dn_conv_bwd(dk_a, proj0, conv_w, dproj0, None, DN_QK_HEADS, "dn_conv_bwd_k")
    dproj0, dcw_v = dn_conv_bwd(dv_a, proj0, conv_w, dproj0, dab, 2 * DN_QK_HEADS, "dn_conv_bwd_v")
    dw_dn = matmul(h0_t, dproj0, "nn", BF16, 1024, 1152, 2048, "d_w_dn")
    dn_grad = _dn_grad_blocks(dw_dn).reshape(4, 2, D_MODEL, DN_SHARD_PAD)
    dn_half, = run_exchange(pair_exchange([dn_grad]), "pair_dn")
    dn_sum = pair_sum(dn_grad, dn_half, "pair_sum_dn")
    dh0, recv_dn = matmul(dproj0, w_dn, "nt", F32, 1024, 1024, 2304, "d_h0",
                          exchange=chip_exchange([dn_sum], [(0, 0)], [(4, 1, D_MODEL, DN_SHARD_PAD)]))
    grad_x, d_ng0 = rms_bwd(dh0, x0, ng[0], dx1)

    big = {
        "dn_w_in": tuple(jnp.transpose(a)[None] for a in reduce_adamw_t(
            recv_dn, jnp.transpose(dn_w_in[0]), jnp.transpose(m_dn_w_in[0]), jnp.transpose(v_dn_w_in[0]),
            "adamw_dn_w_in")),
        "sb_w_in": reduce_adamw(recv_sb, sb_w_in, m_sb_w_in, v_sb_w_in, "adamw_sb_w_in"),
        "w_out": reduce_adamw(recv_wo, w_out, m_w_out, v_w_out, "adamw_w_out"),
        "mem_w_kv": reduce_adamw(recv_kv, mem_w_kv, m_mem_w_kv, v_mem_w_kv, "adamw_mem_w_kv"),
    }

    dconv = jnp.concatenate([dcw_q, dcw_k, dcw_v], axis=1)[:4]
    small_shapes = [(2, D_MODEL), (D_MODEL,), (2, XA_DIM), (2, XA_DIM), (4, DN_QKV), (1, DN_V_HEADS),
                    (1, DN_V_HEADS), (1, HEAD_DIM), (1, HEAD_DIM), (1, HEAD_DIM), (1,)]
    pack = _pack_rows([jnp.stack([d_ng0[0], d_ng1[0]]), d_memg[0], jnp.stack([dxqg0[0], dxqg1[0]]),
                       jnp.stack([dxkg0[0], dxkg1[0]]), dconv, d_alog[0, :DN_V_HEADS], d_dtb[0, :DN_V_HEADS],
                       d_outg[0], d_sbq[0], d_sbk[0], loss_part[0, :1]])
    total = all_reduce_small(pack)
    (g_norm, g_memn, g_xq, g_xk, g_conv_full, g_alog, g_dtb, g_outn, g_sbq, g_sbk, loss1) = _unpack_rows(total, small_shapes)
    conv_cols = DN_QKV // N_DEV
    g_conv = lax.dynamic_slice(g_conv_full, (0, my_dev * conv_cols), (4, conv_cols))[None]
    names = ["norm_g", "mem_norm_g", "xa_q_norm_g", "xa_k_norm_g", "dn_conv_w", "dn_a_log", "dn_dt_bias",
             "dn_out_norm_g", "sb_q_norm_g", "sb_k_norm_g"]
    grads = [g_norm, g_memn, g_xq, g_xk, g_conv, g_alog, g_dtb, g_outn, g_sbq, g_sbk]
    ws = [norm_g, mem_norm_g, xa_q_norm_g, xa_k_norm_g, dn_conv_w, dn_a_log, dn_dt_bias, dn_out_norm_g, sb_q_norm_g,
          sb_k_norm_g]
    ms = [m_norm_g, m_mem_norm_g, m_xa_q_norm_g, m_xa_k_norm_g, m_dn_conv_w, m_dn_a_log, m_dn_dt_bias,
          m_dn_out_norm_g, m_sb_q_norm_g, m_sb_k_norm_g]
    vs = [v_norm_g, v_mem_norm_g, v_xa_q_norm_g, v_xa_k_norm_g, v_dn_conv_w, v_dn_a_log, v_dn_dt_bias,
          v_dn_out_norm_g, v_sb_q_norm_g, v_sb_k_norm_g]
    shapes = [w.shape for w in ws]
    d_p, m_p, v_p = adamw_small(_pack_rows(grads), _pack_rows(ws), _pack_rows(ms), _pack_rows(vs))
    small = dict(zip(names, zip(grads, _unpack_rows(d_p, shapes), _unpack_rows(m_p, shapes), _unpack_rows(v_p, shapes))))

    order = ["norm_g", "mem_norm_g", "mem_w_kv", "xa_q_norm_g", "xa_k_norm_g", "w_out", "dn_w_in", "dn_conv_w",
             "dn_a_log", "dn_dt_bias", "dn_out_norm_g", "sb_w_in", "sb_q_norm_g", "sb_k_norm_g"]
    res = {**big, **small}
    outs = [loss1.reshape(()), grad_x[None]]
    for k in range(4):
        outs += [res[n][k] for n in order]
    return tuple(outs)
```

```python
import functools

import jax
import jax.numpy as jnp
from jax import lax
from jax.experimental import pallas as pl
from jax.experimental.pallas import tpu as pltpu

F32 = jnp.float32
BF16 = jnp.bfloat16

D_MODEL = 2048
SEQ = 2048
N_MEM = 256
INNER = 4096
XA_HEADS = 4
XA_WIDTH = 1024
XA_DIM = 256
MIX_WIDTH = 3072
HEAD_DIM = 128
DN_V_HEADS = 24
DN_QK_HEADS = 12
DN_QK_WIDTH = 1536
DN_CHUNK = 64
DN_QKV = 2 * DN_QK_WIDTH + MIX_WIDTH
DN_PROJ = 11312
SB_HEADS = 24
SB_PROJ = 14336
EPS = 1e-6
N_DEV = 8
DN_SHARD = DN_PROJ // N_DEV
DN_SHARD_PAD = 1536
LANE = 128
DN_COLS = 90 * LANE
DN_AB_BLK, DN_PAD_BLK, DN_XQ_BLK, DN_Z_BLK = 48, 49, 50, 58
SB_XQ_BLK, SB_Z_BLK = 72, 80

ADAM_LR, ADAM_B1, ADAM_B2, ADAM_EPS, ADAM_WD, ADAM_STEP = 0.001, 0.9, 0.999, 1e-08, 0.01, 10

VMEM_LIMIT = 56 * 1024 * 1024
MESH = pl.DeviceIdType.MESH

_NN, _NT, _TN = "nn", "nt", "tn"


def _dims(mode, rank):
    lhs, rhs = {"nn": (1, 0), "nt": (1, 1), "tn": (0, 0)}[mode]
    if rank == 2:
        return (((lhs,), (rhs,)), ((), ()))
    return (((lhs + 1,), (rhs + 1,)), ((0,), (0,)))


def _params(*sem):
    return pltpu.CompilerParams(dimension_semantics=sem if sem else None, vmem_limit_bytes=VMEM_LIMIT)


def _dot(a, b, mode):
    return lax.dot_general(a, b, _dims(mode, a.ndim), preferred_element_type=F32)


def _dg(a, b, dims):
    return _dot(a.astype(BF16), b.astype(BF16), dims)


@jax.custom_vjp
def mm(a, b):
    return _dg(a, b, _NN)


@jax.custom_vjp
def mm_nt(a, b):
    return _dg(a, b, _NT)


@jax.custom_vjp
def mm_tn(a, b):
    return _dg(a, b, _TN)


mm.defvjp(lambda a, b: (_dg(a, b, _NN), (a, b)), lambda r, g: (mm_nt(g, r[1]), mm_tn(r[0], g)))
mm_nt.defvjp(lambda a, b: (_dg(a, b, _NT), (a, b)), lambda r, g: (mm(g, r[1]), mm_tn(g, r[0])))
mm_tn.defvjp(lambda a, b: (_dg(a, b, _TN), (a, b)), lambda r, g: (mm_nt(r[1], g), mm(r[0], g)))


def _split3(x):
    hi = x.astype(BF16)
    r1 = x - hi.astype(F32)
    mid = r1.astype(BF16)
    lo = (r1 - mid.astype(F32)).astype(BF16)
    return hi, mid, lo


def _dg3(a, b, dims):
    ah, am, _ = _split3(a)
    bh, bm, _ = _split3(b)
    return _dot(ah, bh, dims) + (_dot(ah, bm, dims) + _dot(am, bh, dims))


def _dg_exact_rhs(a, b01, dims):
    ah, am, _ = _split3(a)
    b = b01.astype(BF16)
    return _dot(ah, b, dims) + _dot(am, b, dims)


def _dg_exact_lhs(a01, b, dims):
    bh, bm, bl = _split3(b)
    a = a01.astype(BF16)
    return _dot(a, bh, dims) + (_dot(a, bm, dims) + _dot(a, bl, dims))


@jax.custom_vjp
def mm3(a, b):
    return _dg3(a, b, _NN)


@jax.custom_vjp
def mm3_nt(a, b):
    return _dg3(a, b, _NT)


@jax.custom_vjp
def mm3_tn(a, b):
    return _dg3(a, b, _TN)


mm3.defvjp(lambda a, b: (_dg3(a, b, _NN), (a, b)), lambda r, g: (mm3_nt(g, r[1]), mm3_tn(r[0], g)))
mm3_nt.defvjp(lambda a, b: (_dg3(a, b, _NT), (a, b)), lambda r, g: (mm3(g, r[1]), mm3_tn(g, r[0])))
mm3_tn.defvjp(lambda a, b: (_dg3(a, b, _TN), (a, b)), lambda r, g: (mm3_nt(r[1], g), mm3(r[0], g)))


def _softplus(x):
    return jnp.maximum(x, 0.0) + jnp.log(1.0 + jnp.exp(-jnp.abs(x)))


def _log_sigmoid(x):
    return jnp.minimum(x, 0.0) - jnp.log(1.0 + jnp.exp(-jnp.abs(x)))


def _sigmoid(x):
    return 1.0 / (1.0 + jnp.exp(-x))


def _silu(x):
    return x * _sigmoid(x)


def _silu_grad(x):
    s = _sigmoid(x)
    return s * (1.0 + x * (1.0 - s))


@jax.custom_vjp
def mm01(a01, b):
    return _dg_exact_lhs(a01, b, _NN)


mm01.defvjp(lambda a, b: (_dg_exact_lhs(a, b, _NN), a),
            lambda a, g: (jnp.zeros_like(a), _dg_exact_lhs(a, g, _TN)))


def _lane_pick(x, idx):
    lane = lax.broadcasted_iota(jnp.int32, x.shape, x.ndim - 1)
    return jnp.sum(jnp.where(lane == idx, x, 0.0), axis=-1, keepdims=True)


def _dn_chunk(qt, kt, v, ab, alog, dtb, h):
    B, C = qt.shape[0], DN_CHUNK
    g = -jnp.exp(_lane_pick(alog, h)) * _softplus(_lane_pick(ab, h) + _lane_pick(dtb, h))
    beta = _sigmoid(_lane_pick(ab, h + DN_V_HEADS))
    q = qt * lax.rsqrt(jnp.sum(qt * qt, axis=-1, keepdims=True) + EPS) * (HEAD_DIM ** -0.5)
    k = kt * lax.rsqrt(jnp.sum(kt * kt, axis=-1, keepdims=True) + EPS)
    row = lax.broadcasted_iota(jnp.int32, (B, C, C), 1)
    col = lax.broadcasted_iota(jnp.int32, (B, C, C), 2)
    lower = (row >= col).astype(F32)
    ones = jnp.ones((B, C, C), F32)
    g_wide = jnp.broadcast_to(g, (B, C, LANE))
    g_sq = jnp.broadcast_to(g, (B, C, C))
    gc = mm01(lower, g_wide)
    gc_i = gc[:, :, :C]
    gc_j = mm01(ones, jnp.where(row <= col, g_sq, 0.0))
    g_last = jnp.broadcast_to(gc[:, C - 1:C, :], (B, C, LANE))
    decay = jnp.exp(jnp.where(row >= col, gc_i - gc_j, -1e30))
    eg = jnp.exp(gc)
    kk = mm_nt(k, k)
    a_mat = jnp.where(row > col, jnp.broadcast_to(beta, (B, C, C)) * kk * decay, 0.0)
    eye = (row == col).astype(F32)
    y = -a_mat
    t = eye + y
    for _ in range(5):
        y = mm(y, y)
        t = t + mm(t, y)
    bb = jnp.broadcast_to(beta, (B, C, LANE))
    u0 = mm(t, v * bb)
    w = mm(t, k * (bb * eg))
    qk = mm_nt(q, k) * decay
    q_dec = q * eg
    k_dec = k * jnp.exp(g_last - gc)
    cd = jnp.exp(g_last)[:, :8, :]
    return u0, w, qk, q_dec, k_dec, cd


def _shift_rows(x, j, down):
    if j == 0:
        return x
    n = x.shape[0]
    r = lax.broadcasted_iota(jnp.int32, x.shape, 0)
    if down:
        return jnp.where(r >= j, pltpu.roll(x, j, 0), 0.0)
    return jnp.where(r < n - j, pltpu.roll(x, n - j, 0), 0.0)


def dn_conv_fwd(proj, conv_w, exchange=None):
    s = proj.shape[0]

    def body(x_ref, w_ref, o_ref):
        x = x_ref[...]
        w = w_ref[...]
        pre = x * w[3:4, :]
        for j in (1, 2, 3):
            pre = pre + _shift_rows(x, j, True) * w[3 - j:4 - j, :]
        o_ref[...] = _silu(pre)

    return hosted_call(
        body, exchange, name="dn_conv_fwd", grid=(DN_QKV // LANE,),
        in_specs=[pl.BlockSpec((s, LANE), lambda j: (0, j)), pl.BlockSpec((8, LANE), lambda j: (0, j))],
        out_specs=[pl.BlockSpec((s, LANE), lambda j: (0, j))],
        out_shape=[jax.ShapeDtypeStruct((s, DN_QKV), F32)], sem=("parallel",),
    )(proj, conv_w)


def dn_conv_bwd(dact, proj, conv_w, dproj_in, dab, blk0, name):
    s = proj.shape[0]
    nblk = dact.shape[1] // LANE
    extra = 2 if dab is not None else 0

    def body(*refs):
        if dab is not None:
            da_ref, x_ref, w_ref, _, dab_ref, dp_ref, dw_ref = refs
        else:
            da_ref, x_ref, w_ref, _, dp_ref, dw_ref = refs
        j = pl.program_id(0)

        @pl.when(j < nblk)
        def _():
            x = x_ref[...]
            w = w_ref[...]
            xs = [_shift_rows(x, 3 - kk_, True) for kk_ in range(4)]
            pre = xs[0] * w[0:1, :]
            for kk_ in (1, 2, 3):
                pre = pre + xs[kk_] * w[kk_:kk_ + 1, :]
            dpre = da_ref[...] * _silu_grad(pre)
            dx = dpre * w[3:4, :]
            for jj in (1, 2, 3):
                dx = dx + _shift_rows(dpre, jj, False) * w[3 - jj:4 - jj, :]
            dp_ref[...] = dx
            rows = [jnp.sum(dpre * xs[kk_], axis=0, keepdims=True) for kk_ in range(4)]
            dw_ref[...] = jnp.concatenate(rows + [jnp.zeros((4, LANE), F32)], axis=0)

        if dab is not None:
            @pl.when(j == nblk)
            def _():
                dp_ref[...] = dab_ref[...]

            @pl.when(j == nblk + 1)
            def _():
                dp_ref[...] = jnp.zeros_like(dp_ref)

    cl = lambda j: jnp.minimum(j, nblk - 1)
    in_specs = [pl.BlockSpec((s, LANE), lambda j: (0, cl(j))),
                pl.BlockSpec((s, LANE), lambda j: (0, blk0 + cl(j))),
                pl.BlockSpec((8, LANE), lambda j: (0, blk0 + cl(j))),
                pl.BlockSpec(memory_space=pl.ANY)]
    args = [dact, proj, conv_w, dproj_in]
    if dab is not None:
        in_specs.append(pl.BlockSpec((s, LANE), lambda j: (0, 0)))
        args.append(dab)
    return pl.pallas_call(
        body, name=name, grid=(nblk + extra,), in_specs=in_specs,
        out_specs=[pl.BlockSpec((s, LANE), lambda j: (0, blk0 + j)), pl.BlockSpec((8, LANE), lambda j: (0, cl(j)))],
        out_shape=[jax.ShapeDtypeStruct(dproj_in.shape, F32), jax.ShapeDtypeStruct((8, dact.shape[1]), F32)],
        input_output_aliases={3: 0}, compiler_params=_params("arbitrary"),
    )(*args)


DN_GROUP = 8
DN_ROWS = DN_GROUP * DN_CHUNK


def dn_prep_fwd(act, proj, alog, dtb, exchange=None):
    s = act.shape[0]
    nc = s // DN_CHUNK
    C = DN_CHUNK

    def body(q_ref, k_ref, v_ref, ab_ref, al_ref, dt_ref, u0_ref, w_ref, qd_ref, kd_ref, qk_ref, cd_ref):
        qh = pl.program_id(0)
        al = al_ref[0:1, :]
        dt = dt_ref[0:1, :]
        chunks = lambda x: x.reshape(DN_GROUP, C, x.shape[-1])
        rows = lambda x: x.reshape(DN_ROWS, x.shape[-1])
        qt, kt, ab = chunks(q_ref[...]), chunks(k_ref[...]), chunks(ab_ref[...])
        for hv in range(2):
            cs = slice(hv * LANE, (hv + 1) * LANE)
            u0, w, qk, qd, kd, cd = _dn_chunk(qt, kt, chunks(v_ref[:, cs]), ab, al, dt, 2 * qh + hv)
            u0_ref[:, cs] = rows(u0)
            w_ref[:, cs] = rows(w)
            qd_ref[:, cs] = rows(qd)
            kd_ref[:, cs] = rows(kd)
            qk_ref[hv] = rows(qk)
            cd_ref[hv] = cd

    big = pl.BlockSpec((DN_ROWS, 2 * LANE), lambda h, g: (g, h))
    wide = jax.ShapeDtypeStruct((s, MIX_WIDTH), F32)
    return hosted_call(
        body, exchange, name="dn_prep_fwd", grid=(DN_QK_HEADS, s // DN_ROWS),
        in_specs=[pl.BlockSpec((DN_ROWS, LANE), lambda h, g: (g, h)),
                  pl.BlockSpec((DN_ROWS, LANE), lambda h, g: (g, DN_QK_HEADS + h)),
                  pl.BlockSpec((DN_ROWS, 2 * LANE), lambda h, g: (g, DN_QK_HEADS + h)),
                  pl.BlockSpec((DN_ROWS, LANE), lambda h, g: (g, DN_AB_BLK)),
                  pl.BlockSpec((8, LANE), lambda h, g: (0, 0)), pl.BlockSpec((8, LANE), lambda h, g: (0, 0))],
        out_specs=[big, big, big, big,
                   pl.BlockSpec((2, DN_ROWS, C), lambda h, g: (h, g, 0)),
                   pl.BlockSpec((2, DN_GROUP, 8, LANE), lambda h, g: (h, g, 0, 0))],
        out_shape=[wide, wide, wide, wide, jax.ShapeDtypeStruct((DN_V_HEADS, s, C), F32),
                   jax.ShapeDtypeStruct((DN_V_HEADS, nc, 8, LANE), F32)],
        sem=("parallel", "parallel"),
    )(act, act, act, proj, alog, dtb)


def dn_prep_bwd(act, proj, alog, dtb, du0, dw, dqd, dkd, dqk, dcd, exchange=None):
    s = act.shape[0]
    C = DN_CHUNK

    def body(q_ref, k_ref, v_ref, ab_ref, al_ref, dt_ref, du0_ref, dw_ref, dqd_ref, dkd_ref, dqk_ref, dcd_ref,
             dq_ref, dk_ref, dv_ref, dab_ref, dal_ref, ddt_ref):
        g_id = pl.program_id(0)
        qh = pl.program_id(1)
        al = al_ref[0:1, :]
        dt = dt_ref[0:1, :]

        @pl.when(qh == 0)
        def _():
            dab_ref[...] = jnp.zeros_like(dab_ref)

        @pl.when((qh == 0) & (g_id == 0))
        def _():
            dal_ref[...] = jnp.zeros_like(dal_ref)
            ddt_ref[...] = jnp.zeros_like(ddt_ref)

        chunks = lambda x: x.reshape(DN_GROUP, C, x.shape[-1])
        rows = lambda x: x.reshape(DN_ROWS, x.shape[-1])
        qt, kt, ab = chunks(q_ref[...]), chunks(k_ref[...]), chunks(ab_ref[...])
        dq_acc = jnp.zeros((DN_ROWS, LANE), F32)
        dk_acc = jnp.zeros((DN_ROWS, LANE), F32)
        for hv in range(2):
            cs = slice(hv * LANE, (hv + 1) * LANE)
            h = 2 * qh + hv
            f = lambda qt_, kt_, v_, ab_, a_, d_: _dn_chunk(qt_, kt_, v_, ab_, a_, d_, h)
            _, vjp = jax.vjp(f, qt, kt, chunks(v_ref[:, cs]), ab, al, dt)
            dq, dk, dv, dab, dal, ddt = vjp((chunks(du0_ref[:, cs]), chunks(dw_ref[:, cs]), chunks(dqk_ref[hv]),
                                             chunks(dqd_ref[:, cs]), chunks(dkd_ref[:, cs]), dcd_ref[hv]))
            dq_acc = dq_acc + rows(dq)
            dk_acc = dk_acc + rows(dk)
            dv_ref[:, cs] = rows(dv)
            dab_ref[...] += rows(dab)
            dal_ref[0:1, :] += dal
            ddt_ref[0:1, :] += ddt
        dq_ref[...] = dq_acc
        dk_ref[...] = dk_acc

    big = pl.BlockSpec((DN_ROWS, 2 * LANE), lambda g, h: (g, h))
    one = pl.BlockSpec((DN_ROWS, LANE), lambda g, h: (g, h))
    small = pl.BlockSpec((8, LANE), lambda g, h: (0, 0))
    return hosted_call(
        body, exchange, name="dn_prep_bwd", grid=(s // DN_ROWS, DN_QK_HEADS),
        in_specs=[one, pl.BlockSpec((DN_ROWS, LANE), lambda g, h: (g, DN_QK_HEADS + h)),
                  pl.BlockSpec((DN_ROWS, 2 * LANE), lambda g, h: (g, DN_QK_HEADS + h)),
                  pl.BlockSpec((DN_ROWS, LANE), lambda g, h: (g, DN_AB_BLK)), small, small,
                  big, big, big, big,
                  pl.BlockSpec((2, DN_ROWS, C), lambda g, h: (h, g, 0)),
                  pl.BlockSpec((2, DN_GROUP, 8, LANE), lambda g, h: (h, g, 0, 0))],
        out_specs=[one, one, big, pl.BlockSpec((DN_ROWS, LANE), lambda g, h: (g, 0)), small, small],
        out_shape=[jax.ShapeDtypeStruct((s, DN_QK_WIDTH), F32), jax.ShapeDtypeStruct((s, DN_QK_WIDTH), F32),
                   jax.ShapeDtypeStruct((s, MIX_WIDTH), F32), jax.ShapeDtypeStruct((s, LANE), F32),
                   jax.ShapeDtypeStruct((8, LANE), F32), jax.ShapeDtypeStruct((8, LANE), F32)],
        sem=("arbitrary", "arbitrary"),
    )(act, act, act, proj, alog, dtb, du0, dw, dqd, dkd, dqk, dcd)


DN_SCAN_HEADS = 2


def dn_scan_fwd(u0, w, qd, kd, qk, cd, width, exchange=None):
    s = u0.shape[0]
    nc = s // DN_CHUNK
    C = DN_CHUNK

    nh = DN_SCAN_HEADS
    heads = range(nh)
    cols = [slice(h * LANE, (h + 1) * LANE) for h in heads]

    def body(u0_ref, w_ref, qd_ref, kd_ref, qk_ref, cd_ref, o_ref, st_ref):
        def step(c, states):
            rs = pl.ds(pl.multiple_of(c * C, C), C)
            for h in heads:
                st_ref[h, c] = states[h]
            ws = [_dg(w_ref[rs, cols[h]], states[h], _NN) for h in heads]
            us = [u0_ref[rs, cols[h]] - ws[h] for h in heads]
            os_ = [_dg(qd_ref[rs, cols[h]], states[h], _NN) for h in heads]
            for h in heads:
                o_ref[rs, cols[h]] = os_[h] + _dg(qk_ref[h, rs, :], us[h], _NN)
            return tuple(cd_ref[h, c][0:1, :] * states[h] + _dg(kd_ref[rs, cols[h]], us[h], _TN) for h in heads)

        lax.fori_loop(0, nc, step, tuple(jnp.zeros((HEAD_DIM, HEAD_DIM), F32) for _ in heads))

    col = pl.BlockSpec((s, nh * LANE), lambda h: (0, h))
    return hosted_call(
        body, exchange, name="dn_scan_fwd", grid=(DN_V_HEADS // nh,),
        in_specs=[col, col, col, col, pl.BlockSpec((nh, s, C), lambda h: (h, 0, 0)),
                  pl.BlockSpec((nh, nc, 8, LANE), lambda h: (h, 0, 0, 0))],
        out_specs=[col, pl.BlockSpec((nh, nc, HEAD_DIM, HEAD_DIM), lambda h: (h, 0, 0, 0))],
        out_shape=[jax.ShapeDtypeStruct((s, width), F32),
                   jax.ShapeDtypeStruct((DN_V_HEADS, nc, HEAD_DIM, HEAD_DIM), F32)],
        sem=("parallel",),
    )(u0, w, qd, kd, qk, cd)


def dn_scan_bwd(do, u0, w, qd, kd, qk, cd, states, exchange=None):
    s = u0.shape[0]
    nc = s // DN_CHUNK
    C = DN_CHUNK

    nh = DN_SCAN_HEADS
    heads = range(nh)
    cols = [slice(h * LANE, (h + 1) * LANE) for h in heads]

    def body(do_ref, u0_ref, w_ref, qd_ref, kd_ref, qk_ref, cd_ref, st_ref,
             du0_ref, dw_ref, dqd_ref, dkd_ref, dqk_ref, dcd_ref):
        def step(i, dstates):
            c = nc - 1 - i
            rs = pl.ds(pl.multiple_of(c * C, C), C)
            states = [st_ref[h, c] for h in heads]
            gs = [do_ref[rs, cols[h]] for h in heads]
            w_cs = [w_ref[rs, cols[h]] for h in heads]
            qd_cs = [qd_ref[rs, cols[h]] for h in heads]
            cd_rows = [cd_ref[h, c][0:1, :] for h in heads]
            us = [u0_ref[rs, cols[h]] - _dg(w_cs[h], states[h], _NN) for h in heads]
            dus = [_dg(qk_ref[h, rs, :], gs[h], _TN) + _dg(kd_ref[rs, cols[h]], dstates[h], _NN) for h in heads]
            for h in heads:
                du0_ref[rs, cols[h]] = dus[h]
                dw_ref[rs, cols[h]] = -_dg(dus[h], states[h], _NT)
                dqd_ref[rs, cols[h]] = _dg(gs[h], states[h], _NT)
                dkd_ref[rs, cols[h]] = _dg(us[h], dstates[h], _NT)
                dqk_ref[h, rs, :] = _dg(gs[h], us[h], _NT)
                dcd_row = jnp.sum(states[h] * dstates[h], axis=0, keepdims=True)
                dcd_ref[h, c] = jnp.concatenate([dcd_row, jnp.zeros((7, LANE), F32)], axis=0)
            return tuple(cd_rows[h] * dstates[h] + _dg(qd_cs[h], gs[h], _TN) - _dg(w_cs[h], dus[h], _TN)
                         for h in heads)

        lax.fori_loop(0, nc, step, tuple(jnp.zeros((HEAD_DIM, HEAD_DIM), F32) for _ in heads))

    col = pl.BlockSpec((s, nh * LANE), lambda h: (0, h))
    qk_spec = pl.BlockSpec((nh, s, C), lambda h: (h, 0, 0))
    cd_spec = pl.BlockSpec((nh, nc, 8, LANE), lambda h: (h, 0, 0, 0))
    wide = jax.ShapeDtypeStruct((s, MIX_WIDTH), F32)
    return hosted_call(
        body, exchange, name="dn_scan_bwd", grid=(DN_V_HEADS // nh,),
        in_specs=[col, col, col, col, col, qk_spec, cd_spec,
                  pl.BlockSpec((nh, nc, HEAD_DIM, HEAD_DIM), lambda h: (h, 0, 0, 0))],
        out_specs=[col, col, col, col, qk_spec, cd_spec],
        out_shape=[wide, wide, wide, wide, jax.ShapeDtypeStruct((DN_V_HEADS, s, C), F32),
                   jax.ShapeDtypeStruct((DN_V_HEADS, nc, 8, LANE), F32)],
        sem=("parallel",),
    )(do, u0, w, qd, kd, qk, cd, states)


SB_T = 256
SB_GROUPS = (4, 2, 1)


def _sb_scores(q, kbs, diff, lims):
    zs = [_dg(q, kb, _NT) * (HEAD_DIM ** -0.5) for kb in kbs]
    masks = [diff < lim for lim in lims]
    lss = [_log_sigmoid(z) for z in zs]
    lrs = [jnp.where(m, ls - z, 0.0) for m, ls, z in zip(masks, lss, zs)]
    return masks, lss, lrs


def _sb_diff():
    return lax.broadcasted_iota(jnp.int32, (SB_T, SB_T), 1) - lax.broadcasted_iota(jnp.int32, (SB_T, SB_T), 0)


def _sb_loop(n_tiles, step, carry):
    done = 0
    for size in SB_GROUPS:
        groups = (n_tiles - done) // size
        carry = lax.fori_loop(0, groups, lambda p, c, s=size, d=done: step(d + p * s, s, c), carry)
        done = done + groups * size
    return carry


def sb_fwd(qkn, proj, v_blk0, width, exchange=None):
    s = qkn.shape[0]

    def body(q_ref, k_ref, v_ref, o_ref, lt_ref):
        i = pl.program_id(1)
        q = q_ref[...]
        diff = _sb_diff()
        after = (diff < 0).astype(BF16)

        def step(first, n, carry):
            run, acc = carry
            tiles = [first + t for t in range(n)]
            kss = [pl.ds(pl.multiple_of((i - t) * SB_T, SB_T), SB_T) for t in tiles]
            masks, lss, lrs = _sb_scores(q, [k_ref[ks, :] for ks in kss], diff, [t * SB_T for t in tiles])
            within = [_dg_exact_rhs(lr, after, _NN) for lr in lrs]
            sums = [jnp.sum(lr, axis=1, keepdims=True) for lr in lrs]
            for t in range(n):
                wts = jnp.where(masks[t], jnp.exp(lss[t] + (within[t] + run)), 0.0)
                acc = acc + _dg(wts, v_ref[kss[t], :], _NN)
                run = run + sums[t]
            return run, acc

        run, acc = _sb_loop(i + 1, step, (jnp.zeros((SB_T, 1), F32), jnp.zeros((SB_T, HEAD_DIM), F32)))
        o_ref[...] = acc
        lt_ref[0] = run

    return hosted_call(
        body, exchange, name="sb_fwd", grid=(SB_HEADS, s // SB_T),
        in_specs=[pl.BlockSpec((SB_T, LANE), lambda h, i: (i, h)),
                  pl.BlockSpec((s, LANE), lambda h, i: (0, SB_HEADS + h)),
                  pl.BlockSpec((s, LANE), lambda h, i: (0, v_blk0 + h))],
        out_specs=[pl.BlockSpec((SB_T, LANE), lambda h, i: (i, h)), pl.BlockSpec((1, SB_T, 1), lambda h, i: (h, i, 0))],
        out_shape=[jax.ShapeDtypeStruct((s, width), F32), jax.ShapeDtypeStruct((SB_HEADS, s, 1), F32)],
        sem=("parallel", "parallel"),
    )(qkn, qkn, proj)


def sb_bwd(qkn, proj, v_blk0, do, ltot, dproj_in):
    s = qkn.shape[0]

    def body(q_ref, k_ref, v_ref, do_ref, lt_ref, _, dq_ref, dk_ref, dv_ref):
        i = pl.program_id(1)

        @pl.when(i == 0)
        def _():
            dk_ref[...] = jnp.zeros_like(dk_ref)
            dv_ref[...] = jnp.zeros_like(dv_ref)

        q = q_ref[...]
        g = do_ref[...]
        ltot = lt_ref[0]
        diff = _sb_diff()
        upto = (diff >= 0).astype(BF16)
        before = (diff > 0).astype(BF16)

        def step(first, n, carry):
            plr, pdl, dq = carry
            tiles = [first + t for t in range(n)]
            kss = [pl.ds(pl.multiple_of(j * SB_T, SB_T), SB_T) for j in tiles]
            kbs = [k_ref[ks, :] for ks in kss]
            vbs = [v_ref[ks, :] for ks in kss]
            masks, lss, lrs = _sb_scores(q, kbs, diff, [(i - j) * SB_T for j in tiles])
            dwts = [_dg(g, vb, _NT) for vb in vbs]
            within = [_dg_exact_rhs(lr, upto, _NN) for lr in lrs]
            wtss, dls = [], []
            for t in range(n):
                wts = jnp.where(masks[t], jnp.exp(lss[t] + (ltot - (within[t] + plr))), 0.0)
                plr = plr + jnp.sum(lrs[t], axis=1, keepdims=True)
                wtss.append(wts)
                dls.append(dwts[t] * wts)
            dwithin = [_dg_exact_rhs(dl, before, _NN) for dl in dls]
            for t in range(n):
                sz = jnp.exp(lss[t])
                dz = jnp.where(masks[t], dls[t] * (1.0 - sz) - sz * (dwithin[t] + pdl), 0.0) * (HEAD_DIM ** -0.5)
                pdl = pdl + jnp.sum(dls[t], axis=1, keepdims=True)
                dk_ref[kss[t], :] += _dg(dz, q, _TN)
                dv_ref[kss[t], :] += _dg(wtss[t], g, _TN)
                dq = dq + _dg(dz, kbs[t], _NN)
            return plr, pdl, dq

        zero = jnp.zeros((SB_T, 1), F32)
        _, _, dq = _sb_loop(i + 1, step, (zero, zero, jnp.zeros((SB_T, HEAD_DIM), F32)))
        dq_ref[...] = dq

    tile = pl.BlockSpec((SB_T, LANE), lambda h, i: (i, h))
    colspec = pl.BlockSpec((s, LANE), lambda h, i: (0, h))
    return pl.pallas_call(
        body, name="sb_bwd", grid=(SB_HEADS, s // SB_T),
        in_specs=[tile, pl.BlockSpec((s, LANE), lambda h, i: (0, SB_HEADS + h)),
                  pl.BlockSpec((s, LANE), lambda h, i: (0, v_blk0 + h)), tile,
                  pl.BlockSpec((1, SB_T, 1), lambda h, i: (h, i, 0)), pl.BlockSpec(memory_space=pl.ANY)],
        out_specs=[tile, colspec, pl.BlockSpec((s, LANE), lambda h, i: (0, v_blk0 + h))],
        out_shape=[jax.ShapeDtypeStruct((s, MIX_WIDTH), F32), jax.ShapeDtypeStruct((s, MIX_WIDTH), F32),
                   jax.ShapeDtypeStruct(dproj_in.shape, F32)],
        input_output_aliases={5: 2}, compiler_params=_params("parallel", "arbitrary"),
    )(qkn, qkn, proj, do, ltot, dproj_in)


ROW_TILE = 256
HN_HEADS = 8


def head_norm_fwd(x, x_blk0, nblk, gains, out_dtype, out_width, name):
    s = x.shape[0]
    assert x_blk0 % HN_HEADS == 0 and nblk % (HN_HEADS * gains.shape[0]) == 0
    per = nblk // gains.shape[0] // HN_HEADS
    w = HN_HEADS * LANE

    def body(x_ref, g_ref, o_ref):
        gain = g_ref[0, 0:1, :]
        for j in range(HN_HEADS):
            cs = slice(j * LANE, (j + 1) * LANE)
            xv = x_ref[:, cs]
            r = lax.rsqrt(jnp.mean(xv * xv, axis=1, keepdims=True) + EPS)
            o_ref[:, cs] = (xv * r * gain).astype(out_dtype)

    return pl.pallas_call(
        body, name=name, grid=(nblk // HN_HEADS, s // ROW_TILE),
        in_specs=[pl.BlockSpec((ROW_TILE, w), lambda j, t: (t, x_blk0 // HN_HEADS + j)),
                  pl.BlockSpec((1, 8, LANE), lambda j, t: (j // per, 0, 0))],
        out_specs=pl.BlockSpec((ROW_TILE, w), lambda j, t: (t, j)),
        out_shape=jax.ShapeDtypeStruct((s, out_width), out_dtype), compiler_params=_params("parallel", "parallel"),
    )(x, gains)


def head_norm_bwd(dy, dy_blk0, x, x_blk0, nblk, gain, dst, dst_blk0, name):
    s = x.shape[0]

    def body(*refs):
        if dst is not None:
            dy_ref, x_ref, g_ref, _, dx_ref, dg_ref = refs
        else:
            dy_ref, x_ref, g_ref, dx_ref, dg_ref = refs

        @pl.when((pl.program_id(0) == 0) & (pl.program_id(1) == 0))
        def _():
            dg_ref[...] = jnp.zeros_like(dg_ref)

        gain = g_ref[0:1, :]
        dg = jnp.zeros((1, LANE), F32)
        for j in range(HN_HEADS):
            cs = slice(j * LANE, (j + 1) * LANE)
            xv = x_ref[:, cs]
            g = dy_ref[:, cs]
            r = lax.rsqrt(jnp.mean(xv * xv, axis=1, keepdims=True) + EPS)
            gy = g * gain
            dx_ref[:, cs] = r * gy - xv * (r * r * r) * jnp.mean(gy * xv, axis=1, keepdims=True)
            dg = dg + jnp.sum(g * xv * r, axis=0, keepdims=True)
        dg_ref[0:1, :] += dg

    assert dy_blk0 % HN_HEADS == 0 and x_blk0 % HN_HEADS == 0 and dst_blk0 % HN_HEADS == 0 and nblk % HN_HEADS == 0
    w = HN_HEADS * LANE
    in_specs = [pl.BlockSpec((ROW_TILE, w), lambda j, t: (t, dy_blk0 // HN_HEADS + j)),
                pl.BlockSpec((ROW_TILE, w), lambda j, t: (t, x_blk0 // HN_HEADS + j)),
                pl.BlockSpec((8, LANE), lambda j, t: (0, 0))]
    args = [dy, x, gain]
    aliases = {}
    if dst is not None:
        in_specs.append(pl.BlockSpec(memory_space=pl.ANY))
        args.append(dst)
        aliases = {3: 0}
        out0 = jax.ShapeDtypeStruct(dst.shape, F32)
    else:
        out0 = jax.ShapeDtypeStruct((s, (dst_blk0 + nblk) * LANE), F32)
    return pl.pallas_call(
        body, name=name, grid=(nblk // HN_HEADS, s // ROW_TILE), in_specs=in_specs,
        out_specs=[pl.BlockSpec((ROW_TILE, w), lambda j, t: (t, dst_blk0 // HN_HEADS + j)),
                   pl.BlockSpec((8, LANE), lambda j, t: (0, 0))],
        out_shape=[out0, jax.ShapeDtypeStruct((8, LANE), F32)],
        input_output_aliases=aliases, compiler_params=_params("arbitrary", "arbitrary"),
    )(*args)


def _xa_head(xq, kraw, v, qg, kg):
    q = xq * lax.rsqrt(jnp.mean(xq * xq, axis=1, keepdims=True) + EPS) * qg
    k = kraw * lax.rsqrt(jnp.mean(kraw * kraw, axis=1, keepdims=True) + EPS) * kg
    sc = mm_nt(q, k) * (XA_DIM ** -0.5)
    e = jnp.exp(sc - lax.stop_gradient(jnp.max(sc, axis=1, keepdims=True)))
    return mm(e / jnp.sum(e, axis=1, keepdims=True), v)


def xa_fwd(proj, xq_blk0, kv, qg, kg, cat):
    s = proj.shape[0]
    n_mem = kv.shape[0]

    def body(xq_ref, k_ref, v_ref, qg_ref, kg_ref, _, o_ref):
        o_ref[...] = _xa_head(xq_ref[...], k_ref[...], v_ref[...], qg_ref[0:1, :], kg_ref[0:1, :])

    gain = pl.BlockSpec((8, XA_DIM), lambda h, t: (0, 0))
    return pl.pallas_call(
        body, name="xa_fwd", grid=(XA_HEADS, s // ROW_TILE),
        in_specs=[pl.BlockSpec((ROW_TILE, XA_DIM), lambda h, t: (t, xq_blk0 // 2 + h)),
                  pl.BlockSpec((n_mem, XA_DIM), lambda h, t: (0, h)),
                  pl.BlockSpec((n_mem, XA_DIM), lambda h, t: (0, XA_HEADS + h)), gain, gain,
                  pl.BlockSpec(memory_space=pl.ANY)],
        out_specs=pl.BlockSpec((ROW_TILE, XA_DIM), lambda h, t: (t, MIX_WIDTH // XA_DIM + h)),
        out_shape=jax.ShapeDtypeStruct(cat.shape, F32), input_output_aliases={5: 0},
        compiler_params=_params("parallel", "parallel"),
    )(proj, kv, kv, qg, kg, cat)


def xa_bwd(proj, xq_blk0, kv, qg, kg, dcat, dproj_in):
    s = proj.shape[0]
    n_mem = kv.shape[0]

    def body(xq_ref, k_ref, v_ref, qg_ref, kg_ref, do_ref, _, dxq_ref, dk_ref, dv_ref, dqg_ref, dkg_ref):
        h = pl.program_id(0)
        t = pl.program_id(1)

        @pl.when(t == 0)
        def _():
            dk_ref[...] = jnp.zeros_like(dk_ref)
            dv_ref[...] = jnp.zeros_like(dv_ref)

        @pl.when((t == 0) & (h == 0))
        def _():
            dqg_ref[...] = jnp.zeros_like(dqg_ref)
            dkg_ref[...] = jnp.zeros_like(dkg_ref)

        _, vjp = jax.vjp(_xa_head, xq_ref[...], k_ref[...], v_ref[...], qg_ref[0:1, :], kg_ref[0:1, :])
        dxq, dk, dv, dqg, dkg = vjp(do_ref[...])
        dxq_ref[...] = dxq
        dk_ref[...] += dk
        dv_ref[...] += dv
        dqg_ref[0:1, :] += dqg
        dkg_ref[0:1, :] += dkg

    gain = pl.BlockSpec((8, XA_DIM), lambda h, t: (0, 0))
    kspec = pl.BlockSpec((n_mem, XA_DIM), lambda h, t: (0, h))
    vspec = pl.BlockSpec((n_mem, XA_DIM), lambda h, t: (0, XA_HEADS + h))
    return pl.pallas_call(
        body, name="xa_bwd", grid=(XA_HEADS, s // ROW_TILE),
        in_specs=[pl.BlockSpec((ROW_TILE, XA_DIM), lambda h, t: (t, xq_blk0 // 2 + h)), kspec, vspec, gain, gain,
                  pl.BlockSpec((ROW_TILE, XA_DIM), lambda h, t: (t, MIX_WIDTH // XA_DIM + h)),
                  pl.BlockSpec(memory_space=pl.ANY)],
        out_specs=[pl.BlockSpec((ROW_TILE, XA_DIM), lambda h, t: (t, xq_blk0 // 2 + h)), kspec, kspec, gain, gain],
        out_shape=[jax.ShapeDtypeStruct(dproj_in.shape, F32), jax.ShapeDtypeStruct((n_mem, XA_WIDTH), F32),
                   jax.ShapeDtypeStruct((n_mem, XA_WIDTH), F32), jax.ShapeDtypeStruct((8, XA_DIM), F32),
                   jax.ShapeDtypeStruct((8, XA_DIM), F32)],
        input_output_aliases={6: 0}, compiler_params=_params("arbitrary", "arbitrary"),
    )(proj, kv, kv, qg, kg, dcat, dproj_in)


GATE_ROWS = 1024


def gate_fwd(cat, proj, z_blk0):
    s = cat.shape[0]

    def body(c_ref, z_ref, y_ref):
        y_ref[...] = (c_ref[...] * _silu(z_ref[...])).astype(BF16)

    w = 2 * LANE
    rt = min(GATE_ROWS, s)
    return pl.pallas_call(
        body, name="gate_fwd", grid=(INNER // w, s // rt),
        in_specs=[pl.BlockSpec((rt, w), lambda j, t: (t, j)),
                  pl.BlockSpec((rt, w), lambda j, t: (t, z_blk0 // 2 + j))],
        out_specs=pl.BlockSpec((rt, w), lambda j, t: (t, j)),
        out_shape=jax.ShapeDtypeStruct((s, INNER), BF16), compiler_params=_params("parallel", "parallel"),
    )(cat, proj)


def gate_bwd(dy, cat, proj, z_blk0):
    s = cat.shape[0]

    def body(dy_ref, c_ref, z_ref, dc_ref, dz_ref):
        z = z_ref[...]
        g = dy_ref[...]
        dc_ref[...] = g * _silu(z)
        dz_ref[...] = g * c_ref[...] * _silu_grad(z)

    w = 2 * LANE
    rt = min(GATE_ROWS, s)
    tile = pl.BlockSpec((rt, w), lambda j, t: (t, j))
    ztile = pl.BlockSpec((rt, w), lambda j, t: (t, z_blk0 // 2 + j))
    return pl.pallas_call(
        body, name="gate_bwd", grid=(INNER // w, s // rt), in_specs=[tile, tile, ztile],
        out_specs=[tile, ztile],
        out_shape=[jax.ShapeDtypeStruct((s, INNER), F32), jax.ShapeDtypeStruct(proj.shape, F32)],
        compiler_params=_params("parallel", "parallel"),
    )(dy, cat, proj)


NORM_ROWS = 256


def rms_fwd(x, gain, with_transpose=False):
    s, d = x.shape

    def body(x_ref, g_ref, o_ref, *t_ref):
        xv = x_ref[...]
        r = lax.rsqrt(jnp.mean(xv * xv, axis=1, keepdims=True) + EPS)
        y = xv * r * g_ref[0:1, :]
        o_ref[...] = y.astype(BF16)
        if with_transpose:
            t_ref[0][...] = y.T.astype(BF16)

    tile = pl.BlockSpec((NORM_ROWS, d), lambda t: (t, 0))
    res = pl.pallas_call(
        body, name="rms_fwd", grid=(s // NORM_ROWS,),
        in_specs=[tile, pl.BlockSpec((8, d), lambda t: (0, 0))],
        out_specs=[tile] + ([pl.BlockSpec((d, NORM_ROWS), lambda t: (0, t))] if with_transpose else []),
        out_shape=[jax.ShapeDtypeStruct((s, d), BF16)] + ([jax.ShapeDtypeStruct((d, s), BF16)] if with_transpose else []),
        compiler_params=_params("parallel"),
    )(x, gain)
    return res if with_transpose else res[0]


def rms_bwd(dh, x, gain, dres):
    s, d = x.shape

    def body(*refs):
        if dres is not None:
            dh_ref, x_ref, g_ref, dr_ref, dx_ref, dg_ref = refs
        else:
            dh_ref, x_ref, g_ref, dx_ref, dg_ref = refs

        @pl.when(pl.program_id(0) == 0)
        def _():
            dg_ref[...] = jnp.zeros_like(dg_ref)

        xv = x_ref[...]
        g = dh_ref[...]
        r = lax.rsqrt(jnp.mean(xv * xv, axis=1, keepdims=True) + EPS)
        gy = g * g_ref[0:1, :]
        dx = r * gy - xv * (r * r * r) * jnp.mean(gy * xv, axis=1, keepdims=True)
        dx_ref[...] = dx + dr_ref[...] if dres is not None else dx
        dg_ref[0:1, :] += jnp.sum(g * xv * r, axis=0, keepdims=True)

    tile = pl.BlockSpec((NORM_ROWS, d), lambda t: (t, 0))
    gspec = pl.BlockSpec((8, d), lambda t: (0, 0))
    args = [dh, x, gain] + ([dres] if dres is not None else [])
    return pl.pallas_call(
        body, name="rms_bwd", grid=(s // NORM_ROWS,),
        in_specs=[tile, tile, gspec] + ([tile] if dres is not None else []),
        out_specs=[tile, gspec],
        out_shape=[jax.ShapeDtypeStruct((s, d), F32), jax.ShapeDtypeStruct((8, d), F32)],
        compiler_params=_params("arbitrary"),
    )(*args)


def loss_fwd_bwd(y, target):
    s, d = y.shape

    def body(y_ref, t_ref, l_ref, dy_ref):
        @pl.when(pl.program_id(0) == 0)
        def _():
            l_ref[...] = jnp.zeros_like(l_ref)

        err = y_ref[...] - t_ref[...]
        dy_ref[...] = err * (1.0 / d)
        part = 0.5 * jnp.sum(jnp.mean(err * err, axis=1, keepdims=True), axis=0, keepdims=True)
        r = lax.broadcasted_iota(jnp.int32, (8, LANE), 0)
        c = lax.broadcasted_iota(jnp.int32, (8, LANE), 1)
        l_ref[...] += jnp.where((r == 0) & (c == 0), part, 0.0)

    tile = pl.BlockSpec((NORM_ROWS, d), lambda t: (t, 0))
    return pl.pallas_call(
        body, name="loss", grid=(s // NORM_ROWS,), in_specs=[tile, tile],
        out_specs=[pl.BlockSpec((8, LANE), lambda t: (0, 0)), tile],
        out_shape=[jax.ShapeDtypeStruct((8, LANE), F32), jax.ShapeDtypeStruct((s, d), F32)],
        compiler_params=_params("arbitrary"),
    )(y, target)


def matmul(a, b, mode, out_dtype, tm, tn, tk, name, add=None, b_blocked=False, out_blocks=None, exchange=None):
    if b_blocked:
        nb, _, width = b.shape
        bshape = (b.shape[1], nb * width)
    else:
        bshape = b.shape
    if mode == "tn":
        (kdim, m), n = a.shape, bshape[1]
    else:
        (m, kdim), n = a.shape, (bshape[1] if mode == "nn" else bshape[0])
    tm, tn, tk = min(tm, m), min(tn, n), min(tk, kdim)
    assert m % tm == 0 and n % tn == 0 and kdim % tk == 0, (name, m, n, kdim)
    nk = kdim // tk
    dims = {"nn": _NN, "nt": _NT, "tn": _TN}[mode]

    def body(*refs):
        if add is not None:
            a_ref, b_ref, add_ref, o_ref, acc_ref = refs
        else:
            a_ref, b_ref, o_ref, acc_ref = refs
        k = pl.program_id(2)

        @pl.when(k == 0)
        def _():
            acc_ref[...] = jnp.zeros_like(acc_ref)

        acc_ref[...] += _dg(a_ref[...], b_ref[...], dims)

        @pl.when(k == nk - 1)
        def _():
            r = acc_ref[...]
            if add is not None:
                r = r + add_ref[...]
            o_ref[...] = r.astype(out_dtype)

    a_spec = pl.BlockSpec((tk, tm), lambda i, j, k: (k, i)) if mode == "tn" else pl.BlockSpec((tm, tk), lambda i, j, k: (i, k))
    if b_blocked and mode == "nn":
        per = width // tn
        assert width % tn == 0
        b_spec = pl.BlockSpec((None, tk, tn), lambda i, j, k: (j // per, k, j % per))
    elif b_blocked and mode == "nt":
        per = width // tk
        assert width % tk == 0
        b_spec = pl.BlockSpec((None, tn, tk), lambda i, j, k: (k // per, j, k % per))
    elif mode == "nt":
        b_spec = pl.BlockSpec((tn, tk), lambda i, j, k: (j, k))
    else:
        assert not b_blocked
        b_spec = pl.BlockSpec((tk, tn), lambda i, j, k: (k, j))
    add_spec = pl.BlockSpec((tm, tn), lambda i, j, k: (i, j))
    if out_blocks is not None:
        operb = (n // out_blocks) // tn
        assert (n // out_blocks) % tn == 0 and add is None
        o_spec = pl.BlockSpec((None, tm, tn), lambda i, j, k: (j // operb, i, j % operb))
        out_shape = jax.ShapeDtypeStruct((out_blocks, m, n // out_blocks), out_dtype)
    else:
        o_spec = add_spec
        out_shape = jax.ShapeDtypeStruct((m, n), out_dtype)
    res = hosted_call(
        body, exchange, name=name, grid=(m // tm, n // tn, nk),
        in_specs=[a_spec, b_spec] + ([add_spec] if add is not None else []), out_specs=[o_spec],
        out_shape=[out_shape], scratch_shapes=[pltpu.VMEM((tm, tn), F32)],
        sem=("parallel", "parallel", "arbitrary"),
    )(*([a, b] + ([add] if add is not None else [])))
    return res[0] if exchange is None else res


_HBM = pl.BlockSpec(memory_space=pltpu.HBM)


def _me():
    return lax.axis_index("x"), lax.axis_index("y"), lax.axis_index("c")


def _flat(p):
    return 4 * p[0] + 2 * p[1] + p[2]


def _flip(p, r):
    return tuple((1 - v) if (r >> (2 - a)) & 1 else v for a, v in enumerate(p))


class Exchange:
    def __init__(self, srcs, out_shapes, sems, start, finish, alias=None):
        self.srcs, self.out_shapes, self.sems = list(srcs), list(out_shapes), list(sems)
        self.start, self.finish, self.alias = start, finish, dict(alias or {})


def hosted_call(body, exchange, *, name, grid, in_specs, out_specs, out_shape, scratch_shapes=(),
                input_output_aliases=None, sem=()):
    in_specs, out_specs, out_shape = list(in_specs), list(out_specs), list(out_shape)
    scratch_shapes = list(scratch_shapes)
    aliases = input_output_aliases or {}
    if exchange is None:
        call = pl.pallas_call(body, name=name, grid=grid, in_specs=in_specs, out_specs=out_specs, out_shape=out_shape,
                              scratch_shapes=scratch_shapes, input_output_aliases=aliases, compiler_params=_params(*sem))
        return lambda *args: list(call(*args))
    ni, no, ns = len(in_specs), len(out_specs), len(scratch_shapes)
    xi, xo = len(exchange.srcs), len(exchange.out_shapes)

    def wrapped(*refs):
        ins, refs = refs[:ni], refs[ni:]
        xin, refs = refs[:xi], refs[xi:]
        outs, refs = refs[:no], refs[no:]
        xout, refs = refs[:xo], refs[xo:]
        scr, xsem = refs[:ns], refs[ns:]
        first = functools.reduce(lambda p, q: p & q, [pl.program_id(d) == 0 for d in range(len(grid))])
        last = functools.reduce(lambda p, q: p & q, [pl.program_id(d) == grid[d] - 1 for d in range(len(grid))])

        @pl.when(first)
        def _():
            exchange.start(xin, xout, xsem)

        body(*ins, *outs, *scr)

        @pl.when(last)
        def _():
            exchange.finish(xin, xout, xsem)

    call = pl.pallas_call(
        wrapped, name=name, grid=grid, in_specs=in_specs + [_HBM] * xi, out_specs=out_specs + [_HBM] * xo,
        out_shape=out_shape + exchange.out_shapes, scratch_shapes=scratch_shapes + exchange.sems,
        input_output_aliases={**aliases, **{ni + i: no + j for i, j in exchange.alias.items()}},
        compiler_params=_params(*(("arbitrary",) * len(grid))))
    return lambda *args: list(call(*args, *exchange.srcs))


def run_exchange(exchange, name):
    xi, xo = len(exchange.srcs), len(exchange.out_shapes)

    def body(*refs):
        exchange.start(refs[:xi], refs[xi:xi + xo], refs[xi + xo:])
        exchange.finish(refs[:xi], refs[xi:xi + xo], refs[xi + xo:])

    return list(pl.pallas_call(body, name=name, in_specs=[_HBM] * xi, out_specs=[_HBM] * xo,
                               out_shape=exchange.out_shapes, scratch_shapes=exchange.sems,
                               input_output_aliases=exchange.alias)(*exchange.srcs))


def gather_exchange(shards, rows=None, into=None):
    n = len(shards)

    def parts(srcs, outs, sems):
        send_sems, recv_sems, local_sems = sems
        me = _me()
        x, y, c = me
        chips = [(1 - x, y), (x, 1 - y), (1 - x, 1 - y)]

        def place(a, block):
            dst = outs[a].at[_flat(block)]
            return dst if rows is None else dst.at[pl.ds(rows[0], shards[a].shape[0])]

        def copy(a, k, block, to, src=None):
            dst = place(a, block)
            return pltpu.make_async_remote_copy(src_ref=dst if src is None else src, dst_ref=dst,
                                                send_sem=send_sems.at[a, k], recv_sem=recv_sems.at[a, k],
                                                device_id=to, device_id_type=MESH)

        mine = [pltpu.make_async_copy(srcs[a], place(a, me), local_sems.at[a]) for a in range(n)]
        own = []
        for a in range(n):
            own.append(copy(a, 0, me, (x, y, 1 - c), src=srcs[a]))
            own += [copy(a, 1 + j, me, (*chip, c), src=srcs[a]) for j, chip in enumerate(chips)]
        return me, chips, copy, mine, own

    def start(srcs, outs, sems):
        _, _, _, mine, own = parts(srcs, outs, sems)
        for cp in mine + own:
            cp.start()

    def finish(srcs, outs, sems):
        me, chips, copy, mine, own = parts(srcs, outs, sems)
        x, y, c = me
        passed = []
        for j, chip in enumerate(chips):
            for a in range(n):
                copy(a, 1 + j, (*chip, c), me).wait_recv()
                fwd = copy(a, 4 + j, (*chip, c), (x, y, 1 - c))
                fwd.start()
                passed.append(fwd)
        for a in range(n):
            copy(a, 0, (x, y, 1 - c), me).wait_recv()
            for j, chip in enumerate(chips):
                copy(a, 4 + j, (*chip, 1 - c), me).wait_recv()
        for cp in own + passed:
            cp.wait_send()
        for cp in mine:
            cp.wait()

    dma = pltpu.SemaphoreType.DMA
    full = lambda s: s.shape if rows is None else (rows[1],) + s.shape[1:]
    return Exchange(list(shards) + list(into or []), [jax.ShapeDtypeStruct((N_DEV,) + full(s), s.dtype) for s in shards],
                    [dma((n, 7)), dma((n, 7)), dma((n,))], start, finish,
                    alias={n + a: a for a in range(n)} if into else None)


def pair_exchange(srcs):
    n = len(srcs)

    def copies(srcs_, outs, sems):
        send_sems, recv_sems = sems
        x, y, c = _me()
        return [pltpu.make_async_remote_copy(src_ref=srcs_[a].at[:, 1 - c], dst_ref=outs[a], send_sem=send_sems.at[a],
                                             recv_sem=recv_sems.at[a], device_id=(x, y, 1 - c), device_id_type=MESH)
                for a in range(n)]

    def start(srcs_, outs, sems):
        for cp in copies(srcs_, outs, sems):
            cp.start()

    def finish(srcs_, outs, sems):
        for cp in copies(srcs_, outs, sems):
            cp.wait()

    dma = pltpu.SemaphoreType.DMA
    return Exchange(srcs, [jax.ShapeDtypeStruct((4,) + s.shape[2:], s.dtype) for s in srcs], [dma((n,)), dma((n,))],
                    start, finish)


def pair_sum(src, half, name):
    _, _, rows, cols = src.shape
    tr = min(rows, 256)

    def body(x_ref, h_ref, o_ref):
        c = lax.axis_index("c")
        o_ref[0] = (x_ref[0, c].astype(F32) + h_ref[0].astype(F32)).astype(BF16)

    return pl.pallas_call(
        body, name=name, grid=(4, rows // tr),
        in_specs=[pl.BlockSpec((1, 2, tr, cols), lambda ch, t: (ch, 0, t, 0)),
                  pl.BlockSpec((1, tr, cols), lambda ch, t: (ch, t, 0))],
        out_specs=pl.BlockSpec((1, tr, cols), lambda ch, t: (ch, t, 0)),
        out_shape=jax.ShapeDtypeStruct(half.shape, BF16), compiler_params=_params("parallel", "parallel"),
    )(src, half)


def merge_exchanges(a, b):
    ns, no, nm = len(a.srcs), len(a.out_shapes), len(a.sems)

    def start(srcs, outs, sems):
        a.start(srcs[:ns], outs[:no], sems[:nm])
        b.start(srcs[ns:], outs[no:], sems[nm:])

    def finish(srcs, outs, sems):
        a.finish(srcs[:ns], outs[:no], sems[:nm])
        b.finish(srcs[ns:], outs[no:], sems[nm:])

    alias = {**a.alias, **{ns + i: no + j for i, j in b.alias.items()}}
    return Exchange(a.srcs + b.srcs, a.out_shapes + b.out_shapes, a.sems + b.sems, start, finish, alias)


def chip_exchange(parts, slots, recv_shapes, rows=None, into=None):
    n = len(parts)
    win = (lambda ref: ref) if rows is None else (lambda ref: ref.at[pl.ds(rows[0], rows[1])])

    def plan(srcs, outs, sems):
        send_sems, recv_sems, local_sems = sems
        x, y, c = _me()
        chip = 2 * x + y
        mine = [pltpu.make_async_copy(win(srcs[a].at[chip]), win(outs[slots[a][0]].at[chip, slots[a][1]]),
                                      local_sems.at[a]) for a in range(n)]
        sends, arrivals = [], []
        for r in (1, 2, 3):
            px = (1 - x) if r & 2 else x
            py = (1 - y) if r & 1 else y
            for a in range(n):
                ri, layer = slots[a]
                sends.append(pltpu.make_async_remote_copy(
                    src_ref=win(srcs[a].at[2 * px + py]), dst_ref=win(outs[ri].at[chip, layer]),
                    send_sem=send_sems.at[a, r - 1],
                    recv_sem=recv_sems.at[a, r - 1], device_id=(px, py, c), device_id_type=MESH))
                land = win(outs[ri].at[2 * px + py, layer])
                arrivals.append(pltpu.make_async_remote_copy(
                    src_ref=land, dst_ref=land, send_sem=send_sems.at[a, r - 1], recv_sem=recv_sems.at[a, r - 1],
                    device_id=(px, py, c), device_id_type=MESH))
        return mine, sends, arrivals

    def start(srcs, outs, sems):
        mine, sends, _ = plan(srcs, outs, sems)
        for cp in mine + sends:
            cp.start()

    def finish(srcs, outs, sems):
        mine, sends, arrivals = plan(srcs, outs, sems)
        for cp in arrivals:
            cp.wait_recv()
        for cp in sends:
            cp.wait_send()
        for cp in mine:
            cp.wait()

    dma = pltpu.SemaphoreType.DMA
    return Exchange(list(parts) + list(into or []), [jax.ShapeDtypeStruct(s, BF16) for s in recv_shapes],
                    [dma((n, 3)), dma((n, 3)), dma((n,))], start, finish,
                    alias={n + k: k for k in range(len(recv_shapes))} if into else None)


SMALL_ROWS = 24


def all_reduce_small(pack):
    def body(p_ref, o_ref, buf, send_sems, recv_sems):
        me = _me()
        buf[_flat(me)] = p_ref[...]
        sent = []
        for r in range(1, N_DEV):
            peer = _flip(me, r)
            cp = pltpu.make_async_remote_copy(src_ref=p_ref, dst_ref=buf.at[_flat(me)], send_sem=send_sems.at[r - 1],
                                              recv_sem=recv_sems.at[r - 1], device_id=peer, device_id_type=MESH)
            cp.start()
            sent.append(cp)
        for r in range(1, N_DEV):
            peer = _flip(me, r)
            land = buf.at[_flat(peer)]
            pltpu.make_async_remote_copy(src_ref=land, dst_ref=land, send_sem=send_sems.at[r - 1],
                                         recv_sem=recv_sems.at[r - 1], device_id=peer, device_id_type=MESH).wait_recv()
        for cp in sent:
            cp.wait_send()
        acc = buf[0]
        for d in range(1, N_DEV):
            acc = acc + buf[d]
        o_ref[...] = acc

    vm = pl.BlockSpec(memory_space=pltpu.VMEM)
    return pl.pallas_call(
        body, name="all_reduce_small", in_specs=[vm], out_specs=vm,
        out_shape=jax.ShapeDtypeStruct(pack.shape, F32),
        scratch_shapes=[pltpu.VMEM((N_DEV,) + pack.shape, F32), pltpu.SemaphoreType.DMA((7,)),
                        pltpu.SemaphoreType.DMA((7,))],
    )(pack)


def _adamw(w, g, m, v):
    m = ADAM_B1 * m + (1.0 - ADAM_B1) * g
    v = ADAM_B2 * v + (1.0 - ADAM_B2) * (g * g)
    m_hat = m / (1.0 - ADAM_B1 ** ADAM_STEP)
    v_hat = v / (1.0 - ADAM_B2 ** ADAM_STEP)
    delta = -ADAM_LR * (m_hat / (jnp.sqrt(v_hat) + ADAM_EPS) + ADAM_WD * w)
    return delta, m, v


ADAM_ROWS = 256


def reduce_adamw(recv, w, m, v, name):
    nl, rows, cols = w.shape
    nslot, cp = recv.shape[0], recv.shape[3]

    def body(r_ref, w_ref, m_ref, v_ref, g_ref, d_ref, mo_ref, vo_ref):
        g = r_ref[0, 0].astype(F32)
        for slot in range(1, nslot):
            g = g + r_ref[slot, 0].astype(F32)
        if cp != cols:
            g = g[:, :cols]
        delta, m_new, v_new = _adamw(w_ref[0], g, m_ref[0], v_ref[0])
        g_ref[0] = g
        d_ref[0] = delta
        mo_ref[0] = m_new
        vo_ref[0] = v_new

    tile = pl.BlockSpec((1, ADAM_ROWS, cols), lambda l, t: (l, t, 0))
    out = jax.ShapeDtypeStruct(w.shape, F32)
    return pl.pallas_call(
        body, name=name, grid=(nl, rows // ADAM_ROWS),
        in_specs=[pl.BlockSpec((nslot, 1, ADAM_ROWS, cp), lambda l, t: (0, l, t, 0)), tile, tile, tile],
        out_specs=[tile, tile, tile, tile], out_shape=[out, out, out, out],
        compiler_params=_params("parallel", "parallel"),
    )(recv, w, m, v)


def reduce_adamw_t(recv, w_t, m_t, v_t, name):
    cols, rows = w_t.shape
    nslot, cp = recv.shape[0], recv.shape[3]
    tr = 256

    def body(r_ref, w_ref, m_ref, v_ref, g_ref, d_ref, mo_ref, vo_ref):
        g = r_ref[0, 0].astype(F32)
        for slot in range(1, nslot):
            g = g + r_ref[slot, 0].astype(F32)
        g_ref[...] = g.T[:cols, :]
        delta, m_new, v_new = _adamw(w_ref[...], g_ref[...], m_ref[...], v_ref[...])
        d_ref[...] = delta
        mo_ref[...] = m_new
        vo_ref[...] = v_new

    tile = pl.BlockSpec((cols, tr), lambda t: (0, t))
    out = jax.ShapeDtypeStruct((cols, rows), F32)
    return pl.pallas_call(
        body, name=name, grid=(rows // tr,),
        in_specs=[pl.BlockSpec((nslot, 1, tr, cp), lambda t: (0, 0, t, 0)), tile, tile, tile],
        out_specs=[tile, tile, tile, tile], out_shape=[out, out, out, out], compiler_params=_params("parallel"),
    )(recv, w_t, m_t, v_t)


def adamw_small(g, w, m, v):
    def body(g_ref, w_ref, m_ref, v_ref, d_ref, mo_ref, vo_ref):
        d_ref[...], mo_ref[...], vo_ref[...] = _adamw(w_ref[...], g_ref[...], m_ref[...], v_ref[...])

    out = jax.ShapeDtypeStruct(g.shape, F32)
    return pl.pallas_call(body, name="adamw_small", out_shape=[out, out, out])(g, w, m, v)


def _row8(v):
    return jnp.pad(v.reshape(1, -1).astype(F32), ((0, 7), (0, 0)))


def _row8_lanes(v, width=LANE):
    return jnp.pad(v.reshape(1, -1).astype(F32), ((0, 7), (0, width - v.size)))


def _pack_rows(parts):
    rows = []
    for p in parts:
        p = p.reshape(-1).astype(F32)
        nrow = -(-p.size // D_MODEL)
        rows.append(jnp.pad(p, (0, nrow * D_MODEL - p.size)).reshape(nrow, D_MODEL))
    out = jnp.concatenate(rows, axis=0)
    return jnp.pad(out, ((0, SMALL_ROWS - out.shape[0]), (0, 0)))


def _unpack_rows(pack, shapes):
    out, r = [], 0
    for shp in shapes:
        size = 1
        for d in shp:
            size *= d
        nrow = -(-size // D_MODEL)
        out.append(pack[r:r + nrow].reshape(-1)[:size].reshape(shp))
        r += nrow
    return out


def _dn_weight_layout(gathered):
    split = DN_QKV + 2 * DN_V_HEADS
    pieces = []
    for d in range(N_DEV):
        lo, hi = d * DN_SHARD, (d + 1) * DN_SHARD
        if lo < split < hi:
            pieces += [gathered[d, :, :split - lo], jnp.zeros((D_MODEL, DN_COLS - DN_PROJ), gathered.dtype),
                       gathered[d, :, split - lo:DN_SHARD]]
        else:
            pieces.append(gathered[d, :, :DN_SHARD])
    return jnp.concatenate(pieces, axis=1)


def _dn_grad_blocks(dw):
    split = DN_QKV + 2 * DN_V_HEADS
    gap = DN_COLS - DN_PROJ
    local = lambda c: c if c <= split else c + gap
    zeros = jnp.zeros((D_MODEL, DN_SHARD_PAD - DN_SHARD), dw.dtype)
    blocks = []
    for d in range(N_DEV):
        lo, hi = d * DN_SHARD, (d + 1) * DN_SHARD
        if lo < split < hi:
            parts = [dw[:, lo:split], dw[:, split + gap:hi + gap]]
        else:
            parts = [dw[:, local(lo):local(lo) + DN_SHARD]]
        blocks.append(jnp.concatenate(parts + [zeros], axis=1))
    return jnp.stack(blocks)


def kernel(x, mem, norm_g, mem_norm_g, mem_w_kv, xa_q_norm_g, xa_k_norm_g, w_out, dn_w_in, dn_conv_w, dn_a_log, dn_dt_bias, dn_out_norm_g, sb_w_in, sb_q_norm_g, sb_k_norm_g, loss_target, m_norm_g, m_mem_norm_g, m_mem_w_kv, m_xa_q_norm_g, m_xa_k_norm_g, m_w_out, m_dn_w_in, m_dn_conv_w, m_dn_a_log, m_dn_dt_bias, m_dn_out_norm_g, m_sb_w_in, m_sb_q_norm_g, m_sb_k_norm_g, v_norm_g, v_mem_norm_g, v_mem_w_kv, v_xa_q_norm_g, v_xa_k_norm_g, v_w_out, v_dn_w_in, v_dn_conv_w, v_dn_a_log, v_dn_dt_bias, v_dn_out_norm_g, v_sb_w_in, v_sb_q_norm_g, v_sb_k_norm_g):
    x0, memv, target = x[0], mem[0], loss_target[0]
    my_dev = 4 * lax.axis_index("x") + 2 * lax.axis_index("y") + lax.axis_index("c")

    dn_shard = jnp.pad(dn_w_in[0].astype(BF16), ((0, 0), (0, DN_SHARD_PAD - DN_SHARD)))
    w_out_b = [w_out[i].astype(BF16) for i in range(2)]
    w_kv_b = [mem_w_kv[i].astype(BF16) for i in range(2)]
    conv_shard = jnp.pad(dn_conv_w[0], ((0, 4), (0, 0)))
    g_dn, g_conv = run_exchange(gather_exchange([dn_shard, conv_shard]), "gather_first")
    w_dn = _dn_weight_layout(g_dn)
    conv_w = jnp.transpose(g_conv, (1, 0, 2)).reshape(8, DN_QKV)

    ng = [_row8(norm_g[0]), _row8(norm_g[1])]
    mem_g = _row8(mem_norm_g)
    xqg = [_row8(xa_q_norm_g[0]), _row8(xa_q_norm_g[1])]
    xkg = [_row8(xa_k_norm_g[0]), _row8(xa_k_norm_g[1])]
    alog, dtb = _row8_lanes(dn_a_log[0]), _row8_lanes(dn_dt_bias[0])
    out_g, sbq_g, sbk_g = _row8(dn_out_norm_g[0]), _row8(sb_q_norm_g[0]), _row8(sb_k_norm_g[0])

    mem_n = rms_fwd(memv, mem_g)
    h0, h0_t = rms_fwd(x0, ng[0], with_transpose=True)
    proj0, g_wo0 = matmul(h0, w_dn, "nn", F32, 2048, 640, 2048, "proj_dn", exchange=gather_exchange([w_out_b[0]]))
    act, g_kv0 = dn_conv_fwd(proj0, conv_w, exchange=gather_exchange([w_kv_b[0]]))
    sb_shard, half = sb_w_in[0].astype(BF16), 9 * D_MODEL // 16
    u0, w_, qd, kd, qk, cd, w_sb = dn_prep_fwd(
        act, proj0, alog, dtb, exchange=gather_exchange([sb_shard[:half]], rows=(0, D_MODEL)))
    o_raw, states, w_sb = dn_scan_fwd(
        u0, w_, qd, kd, qk, cd, MIX_WIDTH,
        exchange=gather_exchange([sb_shard[half:]], rows=(half, D_MODEL), into=[w_sb]))
    w_o = [g_wo0.reshape(INNER, D_MODEL), None]
    w_kv = [g_kv0.reshape(D_MODEL, 2 * XA_WIDTH), None]
    kv = [matmul(mem_n, w_kv[0], "nn", F32, 256, 1024, 2048, "kv0"), None]
    cat0 = head_norm_fwd(o_raw, 0, DN_V_HEADS, out_g[None], F32, INNER, "dn_out_norm")
    cat0 = xa_fwd(proj0, DN_XQ_BLK, kv[0], xqg[0], xkg[0], cat0)
    y0 = gate_fwd(cat0, proj0, DN_Z_BLK)
    x1 = matmul(y0, w_o[0], "nn", F32, 1024, 1024, 2048, "out_proj0", add=x0)

    h1, h1_t = rms_fwd(x1, ng[1], with_transpose=True)
    proj1 = matmul(h1, w_sb, "nn", F32, 2048, 896, 2048, "proj_sb", b_blocked=True)
    qkn = head_norm_fwd(proj1, 0, 2 * SB_HEADS, jnp.stack([sbq_g, sbk_g]), BF16, 2 * MIX_WIDTH, "sb_qk_norm")
    cat1, ltot, g_wo1, g_kv1 = sb_fwd(qkn, proj1, 2 * SB_HEADS, INNER,
                                      exchange=gather_exchange([w_out_b[1], w_kv_b[1]]))
    w_o[1] = g_wo1.reshape(INNER, D_MODEL)
    w_kv[1] = g_kv1.reshape(D_MODEL, 2 * XA_WIDTH)
    kv[1] = matmul(mem_n, w_kv[1], "nn", F32, 256, 1024, 2048, "kv1")
    cat1 = xa_fwd(proj1, SB_XQ_BLK, kv[1], xqg[1], xkg[1], cat1)
    y1 = gate_fwd(cat1, proj1, SB_Z_BLK)
    x2 = matmul(y1, w_o[1], "nn", F32, 1024, 1024, 2048, "out_proj1", add=x1)
    loss_part, dx2 = loss_fwd_bwd(x2, target)

    dy1 = matmul(dx2, w_o[1], "nt", F32, 1024, 1024, 2048, "d_y1")
    dw_o1 = matmul(y1, dx2, "tn", BF16, 1024, 1024, 2048, "d_w_out1")
    dcat1, dproj1 = gate_bwd(dy1, cat1, proj1, SB_Z_BLK)
    dproj1, dxk1, dxv1, dxqg1, dxkg1 = xa_bwd(proj1, SB_XQ_BLK, kv[1], xqg[1], xkg[1], dcat1, dproj1)
    dqn, dkn, dproj1 = sb_bwd(qkn, proj1, 2 * SB_HEADS, dcat1, ltot, dproj1)
    dproj1, d_sbq = head_norm_bwd(dqn, 0, proj1, 0, SB_HEADS, sbq_g, dproj1, 0, "sb_q_norm_bwd")
    dproj1, d_sbk = head_norm_bwd(dkn, 0, proj1, SB_HEADS, SB_HEADS, sbk_g, dproj1, SB_HEADS, "sb_k_norm_bwd")
    dw_sb = matmul(h1_t, dproj1, "nn", BF16, 1024, 896, 2048, "d_w_sb", out_blocks=N_DEV)
    dh1 = matmul(dproj1, w_sb, "nt", F32, 1024, 1024, 1792, "d_h1", b_blocked=True)
    dx1, d_ng1 = rms_bwd(dh1, x1, ng[1], dx2)

    by_owner = lambda g, rows: g.reshape(4, 2, rows, g.size // (N_DEV * rows))
    sb_grad = dw_sb.reshape(4, 2, D_MODEL, SB_PROJ // N_DEV)
    dy0, sb_half = matmul(dx1, w_o[0], "nt", F32, 1024, 1024, 2048, "d_y0", exchange=pair_exchange([sb_grad]))
    sb_sum = pair_sum(sb_grad, sb_half, "pair_sum_sb")
    dw_o0 = matmul(y0, dx1, "tn", BF16, 1024, 1024, 2048, "d_w_out0")
    dcat0, dproj0 = gate_bwd(dy0, cat0, proj0, DN_Z_BLK)
    dproj0, dxk0, dxv0, dxqg0, dxkg0 = xa_bwd(proj0, DN_XQ_BLK, kv[0], xqg[0], xkg[0], dcat0, dproj0)

    dkv = [jnp.concatenate([dxk0, dxv0], axis=1), jnp.concatenate([dxk1, dxv1], axis=1)]
    dw_kv = [matmul(mem_n, dkv[i], "tn", BF16, 1024, 1024, 256, f"d_w_kv{i}") for i in range(2)]
    dmem_n = matmul(dkv[0], w_kv[0], "nt", F32, 256, 1024, 2048, "d_mem_n0")
    dmem_n = matmul(dkv[1], w_kv[1], "nt", F32, 256, 1024, 2048, "d_mem_n1", add=dmem_n)
    _, d_memg = rms_bwd(dmem_n, memv, mem_g, None)

    grads = [by_owner(dw_o0, INNER // N_DEV), by_owner(dw_o1, INNER // N_DEV),
             by_owner(dw_kv[0], D_MODEL // N_DEV), by_owner(dw_kv[1], D_MODEL // N_DEV)]
    sb_recv_shape = [(4, 1, D_MODEL, SB_PROJ // N_DEV)]
    sb_first = D_MODEL // 2
    do_raw, d_outg = head_norm_bwd(dcat0, 0, o_raw, 0, DN_V_HEADS, out_g, None, 0, "dn_out_norm_bwd")
    du0, dw_, dqd, dkd, dqk, dcd, *halves, recv_sb = dn_scan_bwd(
        do_raw, u0, w_, qd, kd, qk, cd, states,
        exchange=merge_exchanges(pair_exchange(grads),
                                 chip_exchange([sb_sum], [(0, 0)], sb_recv_shape, rows=(0, sb_first))))
    sums = [pair_sum(g, h, f"pair_sum{i}") for i, (g, h) in enumerate(zip(grads, halves))]
    to_chips = merge_exchanges(
        chip_exchange(sums, [(0, 0), (0, 1), (1, 0), (1, 1)],
                      [(4, 2, INNER // N_DEV, D_MODEL), (4, 2, D_MODEL // N_DEV, 2 * XA_WIDTH)]),
        chip_exchange([sb_sum], [(0, 0)], sb_recv_shape, rows=(sb_first, D_MODEL - sb_first), into=[recv_sb]))
    dq_a, dk_a, dv_a, dab, d_alog, d_dtb, recv_wo, recv_kv, recv_sb = dn_prep_bwd(
        act, proj0, alog, dtb, du0, dw_, dqd, dkd, dqk, dcd, exchange=to_chips)
    dproj0, dcw_q = dn_conv_bwd(dq_a, proj0, conv_w, dproj0, None, 0, "dn_conv_bwd_q")
    dproj0, dcw_k = dn_conv_bwd(dk_a, proj0, conv_w, dproj0, None, DN_QK_HEADS, "dn_conv_bwd_k")
    dproj0, dcw_v = dn_conv_bwd(dv_a, proj0, conv_w, dproj0, dab, 2 * DN_QK_HEADS, "dn_conv_bwd_v")
    dw_dn = matmul(h0_t, dproj0, "nn", BF16, 1024, 1152, 2048, "d_w_dn")
    dn_grad = _dn_grad_blocks(dw_dn).reshape(4, 2, D_MODEL, DN_SHARD_PAD)
    dn_half, = run_exchange(pair_exchange([dn_grad]), "pair_dn")
    dn_sum = pair_sum(dn_grad, dn_half, "pair_sum_dn")
    dh0, recv_dn = matmul(dproj0, w_dn, "nt", F32, 1024, 1024, 2304, "d_h0",
                          exchange=chip_exchange([dn_sum], [(0, 0)], [(4, 1, D_MODEL, DN_SHARD_PAD)]))
    grad_x, d_ng0 = rms_bwd(dh0, x0, ng[0], dx1)

    big = {
        "dn_w_in": tuple(jnp.transpose(a)[None] for a in reduce_adamw_t(
            recv_dn, jnp.transpose(dn_w_in[0]), jnp.transpose(m_dn_w_in[0]), jnp.transpose(v_dn_w_in[0]),
            "adamw_dn_w_in")),
        "sb_w_in": reduce_adamw(recv_sb, sb_w_in, m_sb_w_in, v_sb_w_in, "adamw_sb_w_in"),
        "w_out": reduce_adamw(recv_wo, w_out, m_w_out, v_w_out, "adamw_w_out"),
        "mem_w_kv": reduce_adamw(recv_kv, mem_w_kv, m_mem_w_kv, v_mem_w_kv, "adamw_mem_w_kv"),
    }

    dconv = jnp.concatenate([dcw_q, dcw_k, dcw_v], axis=1)[:4]
    small_shapes = [(2, D_MODEL), (D_MODEL,), (2, XA_DIM), (2, XA_DIM), (4, DN_QKV), (1, DN_V_HEADS),
                    (1, DN_V_HEADS), (1, HEAD_DIM), (1, HEAD_DIM), (1, HEAD_DIM), (1,)]
    pack = _pack_rows([jnp.stack([d_ng0[0], d_ng1[0]]), d_memg[0], jnp.stack([dxqg0[0], dxqg1[0]]),
                       jnp.stack([dxkg0[0], dxkg1[0]]), dconv, d_alog[0, :DN_V_HEADS], d_dtb[0, :DN_V_HEADS],
                       d_outg[0], d_sbq[0], d_sbk[0], loss_part[0, :1]])
    total = all_reduce_small(pack)
    (g_norm, g_memn, g_xq, g_xk, g_conv_full, g_alog, g_dtb, g_outn, g_sbq, g_sbk, loss1) = _unpack_rows(total, small_shapes)
    conv_cols = DN_QKV // N_DEV
    g_conv = lax.dynamic_slice(g_conv_full, (0, my_dev * conv_cols), (4, conv_cols))[None]
    names = ["norm_g", "mem_norm_g", "xa_q_norm_g", "xa_k_norm_g", "dn_conv_w", "dn_a_log", "dn_dt_bias",
             "dn_out_norm_g", "sb_q_norm_g", "sb_k_norm_g"]
    grads = [g_norm, g_memn, g_xq, g_xk, g_conv, g_alog, g_dtb, g_outn, g_sbq, g_sbk]
    ws = [norm_g, mem_norm_g, xa_q_norm_g, xa_k_norm_g, dn_conv_w, dn_a_log, dn_dt_bias, dn_out_norm_g, sb_q_norm_g,
          sb_k_norm_g]
    ms = [m_norm_g, m_mem_norm_g, m_xa_q_norm_g, m_xa_k_norm_g, m_dn_conv_w, m_dn_a_log, m_dn_dt_bias,
          m_dn_out_norm_g, m_sb_q_norm_g, m_sb_k_norm_g]
    vs = [v_norm_g, v_mem_norm_g, v_xa_q_norm_g, v_xa_k_norm_g, v_dn_conv_w, v_dn_a_log, v_dn_dt_bias,
          v_dn_out_norm_g, v_sb_q_norm_g, v_sb_k_norm_g]
    shapes = [w.shape for w in ws]
    d_p, m_p, v_p = adamw_small(_pack_rows(grads), _pack_rows(ws), _pack_rows(ms), _pack_rows(vs))
    small = dict(zip(names, zip(grads, _unpack_rows(d_p, shapes), _unpack_rows(m_p, shapes), _unpack_rows(v_p, shapes))))

    order = ["norm_g", "mem_norm_g", "mem_w_kv", "xa_q_norm_g", "xa_k_norm_g", "w_out", "dn_w_in", "dn_conv_w",
             "dn_a_log", "dn_dt_bias", "dn_out_norm_g", "sb_w_in", "sb_q_norm_g", "sb_k_norm_g"]
    res = {**big, **small}
    outs = [loss1.reshape(()), grad_x[None]]
    for k in range(4):
        outs += [res[n][k] for n in order]
    return tuple(outs)
```

```python
import functools

import jax
import jax.numpy as jnp
from jax import lax
from jax.experimental import pallas as pl
from jax.experimental.pallas import tpu as pltpu

F32 = jnp.float32
BF16 = jnp.bfloat16

D_MODEL = 2048
SEQ = 2048
N_MEM = 256
INNER = 4096
XA_HEADS = 4
XA_WIDTH = 1024
XA_DIM = 256
MIX_WIDTH = 3072
HEAD_DIM = 128
DN_V_HEADS = 24
DN_QK_HEADS = 12
DN_QK_WIDTH = 1536
DN_CHUNK = 64
DN_QKV = 2 * DN_QK_WIDTH + MIX_WIDTH
DN_PROJ = 11312
SB_HEADS = 24
SB_PROJ = 14336
EPS = 1e-6
N_DEV = 8
DN_SHARD = DN_PROJ // N_DEV
DN_SHARD_PAD = 1536
LANE = 128
DN_COLS = 90 * LANE
DN_AB_BLK, DN_PAD_BLK, DN_XQ_BLK, DN_Z_BLK = 48, 49, 50, 58
SB_XQ_BLK, SB_Z_BLK = 72, 80

ADAM_LR, ADAM_B1, ADAM_B2, ADAM_EPS, ADAM_WD, ADAM_STEP = 0.001, 0.9, 0.999, 1e-08, 0.01, 10

VMEM_LIMIT = 56 * 1024 * 1024
MESH = pl.DeviceIdType.MESH

_NN, _NT, _TN = "nn", "nt", "tn"


def _dims(mode, rank):
    lhs, rhs = {"nn": (1, 0), "nt": (1, 1), "tn": (0, 0)}[mode]
    if rank == 2:
        return (((lhs,), (rhs,)), ((), ()))
    return (((lhs + 1,), (rhs + 1,)), ((0,), (0,)))


def _params(*sem):
    return pltpu.CompilerParams(dimension_semantics=sem if sem else None, vmem_limit_bytes=VMEM_LIMIT)


def _dot(a, b, mode):
    return lax.dot_general(a, b, _dims(mode, a.ndim), preferred_element_type=F32)


def _dg(a, b, dims):
    return _dot(a.astype(BF16), b.astype(BF16), dims)


@jax.custom_vjp
def mm(a, b):
    return _dg(a, b, _NN)


@jax.custom_vjp
def mm_nt(a, b):
    return _dg(a, b, _NT)


@jax.custom_vjp
def mm_tn(a, b):
    return _dg(a, b, _TN)


mm.defvjp(lambda a, b: (_dg(a, b, _NN), (a, b)), lambda r, g: (mm_nt(g, r[1]), mm_tn(r[0], g)))
mm_nt.defvjp(lambda a, b: (_dg(a, b, _NT), (a, b)), lambda r, g: (mm(g, r[1]), mm_tn(g, r[0])))
mm_tn.defvjp(lambda a, b: (_dg(a, b, _TN), (a, b)), lambda r, g: (mm_nt(r[1], g), mm(r[0], g)))


def _split3(x):
    hi = x.astype(BF16)
    r1 = x - hi.astype(F32)
    mid = r1.astype(BF16)
    lo = (r1 - mid.astype(F32)).astype(BF16)
    return hi, mid, lo


def _dg3(a, b, dims):
    ah, am, _ = _split3(a)
    bh, bm, _ = _split3(b)
    return _dot(ah, bh, dims) + (_dot(ah, bm, dims) + _dot(am, bh, dims))


def _dg_exact_rhs(a, b01, dims):
    ah, am, _ = _split3(a)
    b = b01.astype(BF16)
    return _dot(ah, b, dims) + _dot(am, b, dims)


def _dg_exact_lhs(a01, b, dims):
    bh, bm, bl = _split3(b)
    a = a01.astype(BF16)
    return _dot(a, bh, dims) + (_dot(a, bm, dims) + _dot(a, bl, dims))


@jax.custom_vjp
def mm3(a, b):
    return _dg3(a, b, _NN)


@jax.custom_vjp
def mm3_nt(a, b):
    return _dg3(a, b, _NT)


@jax.custom_vjp
def mm3_tn(a, b):
    return _dg3(a, b, _TN)


mm3.defvjp(lambda a, b: (_dg3(a, b, _NN), (a, b)), lambda r, g: (mm3_nt(g, r[1]), mm3_tn(r[0], g)))
mm3_nt.defvjp(lambda a, b: (_dg3(a, b, _NT), (a, b)), lambda r, g: (mm3(g, r[1]), mm3_tn(g, r[0])))
mm3_tn.defvjp(lambda a, b: (_dg3(a, b, _TN), (a, b)), lambda r, g: (mm3_nt(r[1], g), mm3(r[0], g)))


def _softplus(x):
    return jnp.maximum(x, 0.0) + jnp.log(1.0 + jnp.exp(-jnp.abs(x)))


def _log_sigmoid(x):
    return jnp.minimum(x, 0.0) - jnp.log(1.0 + jnp.exp(-jnp.abs(x)))


def _sigmoid(x):
    return 1.0 / (1.0 + jnp.exp(-x))


def _silu(x):
    return x * _sigmoid(x)


def _silu_grad(x):
    s = _sigmoid(x)
    return s * (1.0 + x * (1.0 - s))


@jax.custom_vjp
def mm01(a01, b):
    return _dg_exact_lhs(a01, b, _NN)


mm01.defvjp(lambda a, b: (_dg_exact_lhs(a, b, _NN), a),
            lambda a, g: (jnp.zeros_like(a), _dg_exact_lhs(a, g, _TN)))


def _lane_pick(x, idx):
    lane = lax.broadcasted_iota(jnp.int32, x.shape, x.ndim - 1)
    return jnp.sum(jnp.where(lane == idx, x, 0.0), axis=-1, keepdims=True)


def _dn_chunk(qt, kt, v, ab, alog, dtb, h):
    B, C = qt.shape[0], DN_CHUNK
    g = -jnp.exp(_lane_pick(alog, h)) * _softplus(_lane_pick(ab, h) + _lane_pick(dtb, h))
    beta = _sigmoid(_lane_pick(ab, h + DN_V_HEADS))
    q = qt * lax.rsqrt(jnp.sum(qt * qt, axis=-1, keepdims=True) + EPS) * (HEAD_DIM ** -0.5)
    k = kt * lax.rsqrt(jnp.sum(kt * kt, axis=-1, keepdims=True) + EPS)
    row = lax.broadcasted_iota(jnp.int32, (B, C, C), 1)
    col = lax.broadcasted_iota(jnp.int32, (B, C, C), 2)
    lower = (row >= col).astype(F32)
    ones = jnp.ones((B, C, C), F32)
    g_wide = jnp.broadcast_to(g, (B, C, LANE))
    g_sq = jnp.broadcast_to(g, (B, C, C))
    gc = mm01(lower, g_wide)
    gc_i = gc[:, :, :C]
    gc_j = mm01(ones, jnp.where(row <= col, g_sq, 0.0))
    g_last = jnp.broadcast_to(gc[:, C - 1:C, :], (B, C, LANE))
    decay = jnp.exp(jnp.where(row >= col, gc_i - gc_j, -1e30))
    eg = jnp.exp(gc)
    kk = mm_nt(k, k)
    a_mat = jnp.where(row > col, jnp.broadcast_to(beta, (B, C, C)) * kk * decay, 0.0)
    eye = (row == col).astype(F32)
    y = -a_mat
    t = eye + y
    for _ in range(5):
        y = mm(y, y)
        t = t + mm(t, y)
    bb = jnp.broadcast_to(beta, (B, C, LANE))
    u0 = mm(t, v * bb)
    w = mm(t, k * (bb * eg))
    qk = mm_nt(q, k) * decay
    q_dec = q * eg
    k_dec = k * jnp.exp(g_last - gc)
    cd = jnp.exp(g_last)[:, :8, :]
    return u0, w, qk, q_dec, k_dec, cd


def _shift_rows(x, j, down):
    if j == 0:
        return x
    n = x.shape[0]
    r = lax.broadcasted_iota(jnp.int32, x.shape, 0)
    if down:
        return jnp.where(r >= j, pltpu.roll(x, j, 0), 0.0)
    return jnp.where(r < n - j, pltpu.roll(x, n - j, 0), 0.0)


def dn_conv_fwd(proj, conv_w, exchange=None):
    s = proj.shape[0]

    def body(x_ref, w_ref, o_ref):
        x = x_ref[...]
        w = w_ref[...]
        pre = x * w[3:4, :]
        for j in (1, 2, 3):
            pre = pre + _shift_rows(x, j, True) * w[3 - j:4 - j, :]
        o_ref[...] = _silu(pre)

    return hosted_call(
        body, exchange, name="dn_conv_fwd", grid=(DN_QKV // LANE,),
        in_specs=[pl.BlockSpec((s, LANE), lambda j: (0, j)), pl.BlockSpec((8, LANE), lambda j: (0, j))],
        out_specs=[pl.BlockSpec((s, LANE), lambda j: (0, j))],
        out_shape=[jax.ShapeDtypeStruct((s, DN_QKV), F32)], sem=("parallel",),
    )(proj, conv_w)


def dn_conv_bwd(dact, proj, conv_w, dproj_in, dab, blk0, name):
    s = proj.shape[0]
    nblk = dact.shape[1] // LANE
    extra = 2 if dab is not None else 0

    def body(*refs):
        if dab is not None:
            da_ref, x_ref, w_ref, _, dab_ref, dp_ref, dw_ref = refs
        else:
            da_ref, x_ref, w_ref, _, dp_ref, dw_ref = refs
        j = pl.program_id(0)

        @pl.when(j < nblk)
        def _():
            x = x_ref[...]
            w = w_ref[...]
            xs = [_shift_rows(x, 3 - kk_, True) for kk_ in range(4)]
            pre = xs[0] * w[0:1, :]
            for kk_ in (1, 2, 3):
                pre = pre + xs[kk_] * w[kk_:kk_ + 1, :]
            dpre = da_ref[...] * _silu_grad(pre)
            dx = dpre * w[3:4, :]
            for jj in (1, 2, 3):
                dx = dx + _shift_rows(dpre, jj, False) * w[3 - jj:4 - jj, :]
            dp_ref[...] = dx
            rows = [jnp.sum(dpre * xs[kk_], axis=0, keepdims=True) for kk_ in range(4)]
            dw_ref[...] = jnp.concatenate(rows + [jnp.zeros((4, LANE), F32)], axis=0)

        if dab is not None:
            @pl.when(j == nblk)
            def _():
                dp_ref[...] = dab_ref[...]

            @pl.when(j == nblk + 1)
            def _():
                dp_ref[...] = jnp.zeros_like(dp_ref)

    cl = lambda j: jnp.minimum(j, nblk - 1)
    in_specs = [pl.BlockSpec((s, LANE), lambda j: (0, cl(j))),
                pl.BlockSpec((s, LANE), lambda j: (0, blk0 + cl(j))),
                pl.BlockSpec((8, LANE), lambda j: (0, blk0 + cl(j))),
                pl.BlockSpec(memory_space=pl.ANY)]
    args = [dact, proj, conv_w, dproj_in]
    if dab is not None:
        in_specs.append(pl.BlockSpec((s, LANE), lambda j: (0, 0)))
        args.append(dab)
    return pl.pallas_call(
        body, name=name, grid=(nblk + extra,), in_specs=in_specs,
        out_specs=[pl.BlockSpec((s, LANE), lambda j: (0, blk0 + j)), pl.BlockSpec((8, LANE), lambda j: (0, cl(j)))],
        out_shape=[jax.ShapeDtypeStruct(dproj_in.shape, F32), jax.ShapeDtypeStruct((8, dact.shape[1]), F32)],
        input_output_aliases={3: 0}, compiler_params=_params("arbitrary"),
    )(*args)


DN_GROUP = 8
DN_ROWS = DN_GROUP * DN_CHUNK


def dn_prep_fwd(act, proj, alog, dtb, exchange=None):
    s = act.shape[0]
    nc = s // DN_CHUNK
    C = DN_CHUNK

    def body(q_ref, k_ref, v_ref, ab_ref, al_ref, dt_ref, u0_ref, w_ref, qd_ref, kd_ref, qk_ref, cd_ref):
        qh = pl.program_id(0)
        al = al_ref[0:1, :]
        dt = dt_ref[0:1, :]
        chunks = lambda x: x.reshape(DN_GROUP, C, x.shape[-1])
        rows = lambda x: x.reshape(DN_ROWS, x.shape[-1])
        qt, kt, ab = chunks(q_ref[...]), chunks(k_ref[...]), chunks(ab_ref[...])
        for hv in range(2):
            cs = slice(hv * LANE, (hv + 1) * LANE)
            u0, w, qk, qd, kd, cd = _dn_chunk(qt, kt, chunks(v_ref[:, cs]), ab, al, dt, 2 * qh + hv)
            u0_ref[:, cs] = rows(u0)
            w_ref[:, cs] = rows(w)
            qd_ref[:, cs] = rows(qd)
            kd_ref[:, cs] = rows(kd)
            qk_ref[hv] = rows(qk)
            cd_ref[hv] = cd

    big = pl.BlockSpec((DN_ROWS, 2 * LANE), lambda h, g: (g, h))
    wide = jax.ShapeDtypeStruct((s, MIX_WIDTH), F32)
    return hosted_call(
        body, exchange, name="dn_prep_fwd", grid=(DN_QK_HEADS, s // DN_ROWS),
        in_specs=[pl.BlockSpec((DN_ROWS, LANE), lambda h, g: (g, h)),
                  pl.BlockSpec((DN_ROWS, LANE), lambda h, g: (g, DN_QK_HEADS + h)),
                  pl.BlockSpec((DN_ROWS, 2 * LANE), lambda h, g: (g, DN_QK_HEADS + h)),
                  pl.BlockSpec((DN_ROWS, LANE), lambda h, g: (g, DN_AB_BLK)),
                  pl.BlockSpec((8, LANE), lambda h, g: (0, 0)), pl.BlockSpec((8, LANE), lambda h, g: (0, 0))],
        out_specs=[big, big, big, big,
                   pl.BlockSpec((2, DN_ROWS, C), lambda h, g: (h, g, 0)),
                   pl.BlockSpec((2, DN_GROUP, 8, LANE), lambda h, g: (h, g, 0, 0))],
        out_shape=[wide, wide, wide, wide, jax.ShapeDtypeStruct((DN_V_HEADS, s, C), F32),
                   jax.ShapeDtypeStruct((DN_V_HEADS, nc, 8, LANE), F32)],
        sem=("parallel", "parallel"),
    )(act, act, act, proj, alog, dtb)


def dn_prep_bwd(act, proj, alog, dtb, du0, dw, dqd, dkd, dqk, dcd, exchange=None):
    s = act.shape[0]
    C = DN_CHUNK

    def body(q_ref, k_ref, v_ref, ab_ref, al_ref, dt_ref, du0_ref, dw_ref, dqd_ref, dkd_ref, dqk_ref, dcd_ref,
             dq_ref, dk_ref, dv_ref, dab_ref, dal_ref, ddt_ref):
        g_id = pl.program_id(0)
        qh = pl.program_id(1)
        al = al_ref[0:1, :]
        dt = dt_ref[0:1, :]

        @pl.when(qh == 0)
        def _():
            dab_ref[...] = jnp.zeros_like(dab_ref)

        @pl.when((qh == 0) & (g_id == 0))
        def _():
            dal_ref[...] = jnp.zeros_like(dal_ref)
            ddt_ref[...] = jnp.zeros_like(ddt_ref)

        chunks = lambda x: x.reshape(DN_GROUP, C, x.shape[-1])
        rows = lambda x: x.reshape(DN_ROWS, x.shape[-1])
        qt, kt, ab = chunks(q_ref[...]), chunks(k_ref[...]), chunks(ab_ref[...])
        dq_acc = jnp.zeros((DN_ROWS, LANE), F32)
        dk_acc = jnp.zeros((DN_ROWS, LANE), F32)
        for hv in range(2):
            cs = slice(hv * LANE, (hv + 1) * LANE)
            h = 2 * qh + hv
            f = lambda qt_, kt_, v_, ab_, a_, d_: _dn_chunk(qt_, kt_, v_, ab_, a_, d_, h)
            _, vjp = jax.vjp(f, qt, kt, chunks(v_ref[:, cs]), ab, al, dt)
            dq, dk, dv, dab, dal, ddt = vjp((chunks(du0_ref[:, cs]), chunks(dw_ref[:, cs]), chunks(dqk_ref[hv]),
                                             chunks(dqd_ref[:, cs]), chunks(dkd_ref[:, cs]), dcd_ref[hv]))
            dq_acc = dq_acc + rows(dq)
            dk_acc = dk_acc + rows(dk)
            dv_ref[:, cs] = rows(dv)
            dab_ref[...] += rows(dab)
            dal_ref[0:1, :] += dal
            ddt_ref[0:1, :] += ddt
        dq_ref[...] = dq_acc
        dk_ref[...] = dk_acc

    big = pl.BlockSpec((DN_ROWS, 2 * LANE), lambda g, h: (g, h))
    one = pl.BlockSpec((DN_ROWS, LANE), lambda g, h: (g, h))
    small = pl.BlockSpec((8, LANE), lambda g, h: (0, 0))
    return hosted_call(
        body, exchange, name="dn_prep_bwd", grid=(s // DN_ROWS, DN_QK_HEADS),
        in_specs=[one, pl.BlockSpec((DN_ROWS, LANE), lambda g, h: (g, DN_QK_HEADS + h)),
                  pl.BlockSpec((DN_ROWS, 2 * LANE), lambda g, h: (g, DN_QK_HEADS + h)),
                  pl.BlockSpec((DN_ROWS, LANE), lambda g, h: (g, DN_AB_BLK)), small, small,
                  big, big, big, big,
                  pl.BlockSpec((2, DN_ROWS, C), lambda g, h: (h, g, 0)),
                  pl.BlockSpec((2, DN_GROUP, 8, LANE), lambda g, h: (h, g, 0, 0))],
        out_specs=[one, one, big, pl.BlockSpec((DN_ROWS, LANE), lambda g, h: (g, 0)), small, small],
        out_shape=[jax.ShapeDtypeStruct((s, DN_QK_WIDTH), F32), jax.ShapeDtypeStruct((s, DN_QK_WIDTH), F32),
                   jax.ShapeDtypeStruct((s, MIX_WIDTH), F32), jax.ShapeDtypeStruct((s, LANE), F32),
                   jax.ShapeDtypeStruct((8, LANE), F32), jax.ShapeDtypeStruct((8, LANE), F32)],
        sem=("arbitrary", "arbitrary"),
    )(act, act, act, proj, alog, dtb, du0, dw, dqd, dkd, dqk, dcd)


DN_SCAN_HEADS = 2


def dn_scan_fwd(u0, w, qd, kd, qk, cd, width, exchange=None):
    s = u0.shape[0]
    nc = s // DN_CHUNK
    C = DN_CHUNK

    nh = DN_SCAN_HEADS
    heads = range(nh)
    cols = [slice(h * LANE, (h + 1) * LANE) for h in heads]

    def body(u0_ref, w_ref, qd_ref, kd_ref, qk_ref, cd_ref, o_ref, st_ref):
        def step(c, states):
            rs = pl.ds(pl.multiple_of(c * C, C), C)
            for h in heads:
                st_ref[h, c] = states[h]
            ws = [_dg(w_ref[rs, cols[h]], states[h], _NN) for h in heads]
            us = [u0_ref[rs, cols[h]] - ws[h] for h in heads]
            os_ = [_dg(qd_ref[rs, cols[h]], states[h], _NN) for h in heads]
            for h in heads:
                o_ref[rs, cols[h]] = os_[h] + _dg(qk_ref[h, rs, :], us[h], _NN)
            return tuple(cd_ref[h, c][0:1, :] * states[h] + _dg(kd_ref[rs, cols[h]], us[h], _TN) for h in heads)

        lax.fori_loop(0, nc, step, tuple(jnp.zeros((HEAD_DIM, HEAD_DIM), F32) for _ in heads))

    col = pl.BlockSpec((s, nh * LANE), lambda h: (0, h))
    return hosted_call(
        body, exchange, name="dn_scan_fwd", grid=(DN_V_HEADS // nh,),
        in_specs=[col, col, col, col, pl.BlockSpec((nh, s, C), lambda h: (h, 0, 0)),
                  pl.BlockSpec((nh, nc, 8, LANE), lambda h: (h, 0, 0, 0))],
        out_specs=[col, pl.BlockSpec((nh, nc, HEAD_DIM, HEAD_DIM), lambda h: (h, 0, 0, 0))],
        out_shape=[jax.ShapeDtypeStruct((s, width), F32),
                   jax.ShapeDtypeStruct((DN_V_HEADS, nc, HEAD_DIM, HEAD_DIM), F32)],
        sem=("parallel",),
    )(u0, w, qd, kd, qk, cd)


def dn_scan_bwd(do, u0, w, qd, kd, qk, cd, states, exchange=None):
    s = u0.shape[0]
    nc = s // DN_CHUNK
    C = DN_CHUNK

    nh = DN_SCAN_HEADS
    heads = range(nh)
    cols = [slice(h * LANE, (h + 1) * LANE) for h in heads]

    def body(do_ref, u0_ref, w_ref, qd_ref, kd_ref, qk_ref, cd_ref, st_ref,
             du0_ref, dw_ref, dqd_ref, dkd_ref, dqk_ref, dcd_ref):
        def step(i, dstates):
            c = nc - 1 - i
            rs = pl.ds(pl.multiple_of(c * C, C), C)
            states = [st_ref[h, c] for h in heads]
            gs = [do_ref[rs, cols[h]] for h in heads]
            w_cs = [w_ref[rs, cols[h]] for h in heads]
            qd_cs = [qd_ref[rs, cols[h]] for h in heads]
            cd_rows = [cd_ref[h, c][0:1, :] for h in heads]
            us = [u0_ref[rs, cols[h]] - _dg(w_cs[h], states[h], _NN) for h in heads]
            dus = [_dg(qk_ref[h, rs, :], gs[h], _TN) + _dg(kd_ref[rs, cols[h]], dstates[h], _NN) for h in heads]
            for h in heads:
                du0_ref[rs, cols[h]] = dus[h]
                dw_ref[rs, cols[h]] = -_dg(dus[h], states[h], _NT)
                dqd_ref[rs, cols[h]] = _dg(gs[h], states[h], _NT)
                dkd_ref[rs, cols[h]] = _dg(us[h], dstates[h], _NT)
                dqk_ref[h, rs, :] = _dg(gs[h], us[h], _NT)
                dcd_row = jnp.sum(states[h] * dstates[h], axis=0, keepdims=True)
                dcd_ref[h, c] = jnp.concatenate([dcd_row, jnp.zeros((7, LANE), F32)], axis=0)
            return tuple(cd_rows[h] * dstates[h] + _dg(qd_cs[h], gs[h], _TN) - _dg(w_cs[h], dus[h], _TN)
                         for h in heads)

        lax.fori_loop(0, nc, step, tuple(jnp.zeros((HEAD_DIM, HEAD_DIM), F32) for _ in heads))

    col = pl.BlockSpec((s, nh * LANE), lambda h: (0, h))
    qk_spec = pl.BlockSpec((nh, s, C), lambda h: (h, 0, 0))
    cd_spec = pl.BlockSpec((nh, nc, 8, LANE), lambda h: (h, 0, 0, 0))
    wide = jax.ShapeDtypeStruct((s, MIX_WIDTH), F32)
    return hosted_call(
        body, exchange, name="dn_scan_bwd", grid=(DN_V_HEADS // nh,),
        in_specs=[col, col, col, col, col, qk_spec, cd_spec,
                  pl.BlockSpec((nh, nc, HEAD_DIM, HEAD_DIM), lambda h: (h, 0, 0, 0))],
        out_specs=[col, col, col, col, qk_spec, cd_spec],
        out_shape=[wide, wide, wide, wide, jax.ShapeDtypeStruct((DN_V_HEADS, s, C), F32),
                   jax.ShapeDtypeStruct((DN_V_HEADS, nc, 8, LANE), F32)],
        sem=("parallel",),
    )(do, u0, w, qd, kd, qk, cd, states)


SB_T = 256
SB_GROUPS = (4, 2, 1)


def _sb_scores(q, kbs, diff, lims):
    zs = [_dg(q, kb, _NT) * (HEAD_DIM ** -0.5) for kb in kbs]
    masks = [diff < lim for lim in lims]
    lss = [_log_sigmoid(z) for z in zs]
    lrs = [jnp.where(m, ls - z, 0.0) for m, ls, z in zip(masks, lss, zs)]
    return masks, lss, lrs


def _sb_diff():
    return lax.broadcasted_iota(jnp.int32, (SB_T, SB_T), 1) - lax.broadcasted_iota(jnp.int32, (SB_T, SB_T), 0)


def _sb_loop(n_tiles, step, carry):
    done = 0
    for size in SB_GROUPS:
        groups = (n_tiles - done) // size
        carry = lax.fori_loop(0, groups, lambda p, c, s=size, d=done: step(d + p * s, s, c), carry)
        done = done + groups * size
    return carry


def sb_fwd(qkn, proj, v_blk0, width, exchange=None):
    s = qkn.shape[0]

    def body(q_ref, k_ref, v_ref, o_ref, lt_ref):
        i = pl.program_id(1)
        q = q_ref[...]
        diff = _sb_diff()
        after = (diff < 0).astype(BF16)

        def step(first, n, carry):
            run, acc = carry
            tiles = [first + t for t in range(n)]
            kss = [pl.ds(pl.multiple_of((i - t) * SB_T, SB_T), SB_T) for t in tiles]
            masks, lss, lrs = _sb_scores(q, [k_ref[ks, :] for ks in kss], diff, [t * SB_T for t in tiles])
            within = [_dg_exact_rhs(lr, after, _NN) for lr in lrs]
            sums = [jnp.sum(lr, axis=1, keepdims=True) for lr in lrs]
            for t in range(n):
                wts = jnp.where(masks[t], jnp.exp(lss[t] + (within[t] + run)), 0.0)
                acc = acc + _dg(wts, v_ref[kss[t], :], _NN)
                run = run + sums[t]
            return run, acc

        run, acc = _sb_loop(i + 1, step, (jnp.zeros((SB_T, 1), F32), jnp.zeros((SB_T, HEAD_DIM), F32)))
        o_ref[...] = acc
        lt_ref[0] = run

    return hosted_call(
        body, exchange, name="sb_fwd", grid=(SB_HEADS, s // SB_T),
        in_specs=[pl.BlockSpec((SB_T, LANE), lambda h, i: (i, h)),
                  pl.BlockSpec((s, LANE), lambda h, i: (0, SB_HEADS + h)),
                  pl.BlockSpec((s, LANE), lambda h, i: (0, v_blk0 + h))],
        out_specs=[pl.BlockSpec((SB_T, LANE), lambda h, i: (i, h)), pl.BlockSpec((1, SB_T, 1), lambda h, i: (h, i, 0))],
        out_shape=[jax.ShapeDtypeStruct((s, width), F32), jax.ShapeDtypeStruct((SB_HEADS, s, 1), F32)],
        sem=("parallel", "parallel"),
    )(qkn, qkn, proj)


def sb_bwd(qkn, proj, v_blk0, do, ltot, dproj_in):
    s = qkn.shape[0]

    def body(q_ref, k_ref, v_ref, do_ref, lt_ref, _, dq_ref, dk_ref, dv_ref):
        i = pl.program_id(1)

        @pl.when(i == 0)
        def _():
            dk_ref[...] = jnp.zeros_like(dk_ref)
            dv_ref[...] = jnp.zeros_like(dv_ref)

        q = q_ref[...]
        g = do_ref[...]
        ltot = lt_ref[0]
        diff = _sb_diff()
        upto = (diff >= 0).astype(BF16)
        before = (diff > 0).astype(BF16)

        def step(first, n, carry):
            plr, pdl, dq = carry
            tiles = [first + t for t in range(n)]
            kss = [pl.ds(pl.multiple_of(j * SB_T, SB_T), SB_T) for j in tiles]
            kbs = [k_ref[ks, :] for ks in kss]
            vbs = [v_ref[ks, :] for ks in kss]
            masks, lss, lrs = _sb_scores(q, kbs, diff, [(i - j) * SB_T for j in tiles])
            dwts = [_dg(g, vb, _NT) for vb in vbs]
            within = [_dg_exact_rhs(lr, upto, _NN) for lr in lrs]
            wtss, dls = [], []
            for t in range(n):
                wts = jnp.where(masks[t], jnp.exp(lss[t] + (ltot - (within[t] + plr))), 0.0)
                plr = plr + jnp.sum(lrs[t], axis=1, keepdims=True)
                wtss.append(wts)
                dls.append(dwts[t] * wts)
            dwithin = [_dg_exact_rhs(dl, before, _NN) for dl in dls]
            for t in range(n):
                sz = jnp.exp(lss[t])
                dz = jnp.where(masks[t], dls[t] * (1.0 - sz) - sz * (dwithin[t] + pdl), 0.0) * (HEAD_DIM ** -0.5)
                pdl = pdl + jnp.sum(dls[t], axis=1, keepdims=True)
                dk_ref[kss[t], :] += _dg(dz, q, _TN)
                dv_ref[kss[t], :] += _dg(wtss[t], g, _TN)
                dq = dq + _dg(dz, kbs[t], _NN)
            return plr, pdl, dq

        zero = jnp.zeros((SB_T, 1), F32)
        _, _, dq = _sb_loop(i + 1, step, (zero, zero, jnp.zeros((SB_T, HEAD_DIM), F32)))
        dq_ref[...] = dq

    tile = pl.BlockSpec((SB_T, LANE), lambda h, i: (i, h))
    colspec = pl.BlockSpec((s, LANE), lambda h, i: (0, h))
    return pl.pallas_call(
        body, name="sb_bwd", grid=(SB_HEADS, s // SB_T),
        in_specs=[tile, pl.BlockSpec((s, LANE), lambda h, i: (0, SB_HEADS + h)),
                  pl.BlockSpec((s, LANE), lambda h, i: (0, v_blk0 + h)), tile,
                  pl.BlockSpec((1, SB_T, 1), lambda h, i: (h, i, 0)), pl.BlockSpec(memory_space=pl.ANY)],
        out_specs=[tile, colspec, pl.BlockSpec((s, LANE), lambda h, i: (0, v_blk0 + h))],
        out_shape=[jax.ShapeDtypeStruct((s, MIX_WIDTH), F32), jax.ShapeDtypeStruct((s, MIX_WIDTH), F32),
                   jax.ShapeDtypeStruct(dproj_in.shape, F32)],
        input_output_aliases={5: 2}, compiler_params=_params("parallel", "arbitrary"),
    )(qkn, qkn, proj, do, ltot, dproj_in)


ROW_TILE = 256
HN_HEADS = 8


def head_norm_fwd(x, x_blk0, nblk, gains, out_dtype, out_width, name):
    s = x.shape[0]
    assert x_blk0 % HN_HEADS == 0 and nblk % (HN_HEADS * gains.shape[0]) == 0
    per = nblk // gains.shape[0] // HN_HEADS
    w = HN_HEADS * LANE

    def body(x_ref, g_ref, o_ref):
        gain = g_ref[0, 0:1, :]
        for j in range(HN_HEADS):
            cs = slice(j * LANE, (j + 1) * LANE)
            xv = x_ref[:, cs]
            r = lax.rsqrt(jnp.mean(xv * xv, axis=1, keepdims=True) + EPS)
            o_ref[:, cs] = (xv * r * gain).astype(out_dtype)

    return pl.pallas_call(
        body, name=name, grid=(nblk // HN_HEADS, s // ROW_TILE),
        in_specs=[pl.BlockSpec((ROW_TILE, w), lambda j, t: (t, x_blk0 // HN_HEADS + j)),
                  pl.BlockSpec((1, 8, LANE), lambda j, t: (j // per, 0, 0))],
        out_specs=pl.BlockSpec((ROW_TILE, w), lambda j, t: (t, j)),
        out_shape=jax.ShapeDtypeStruct((s, out_width), out_dtype), compiler_params=_params("parallel", "parallel"),
    )(x, gains)


def head_norm_bwd(dy, dy_blk0, x, x_blk0, nblk, gain, dst, dst_blk0, name):
    s = x.shape[0]

    def body(*refs):
        if dst is not None:
            dy_ref, x_ref, g_ref, _, dx_ref, dg_ref = refs
        else:
            dy_ref, x_ref, g_ref, dx_ref, dg_ref = refs

        @pl.when((pl.program_id(0) == 0) & (pl.program_id(1) == 0))
        def _():
            dg_ref[...] = jnp.zeros_like(dg_ref)

        gain = g_ref[0:1, :]
        dg = jnp.zeros((1, LANE), F32)
        for j in range(HN_HEADS):
            cs = slice(j * LANE, (j + 1) * LANE)
            xv = x_ref[:, cs]
            g = dy_ref[:, cs]
            r = lax.rsqrt(jnp.mean(xv * xv, axis=1, keepdims=True) + EPS)
            gy = g * gain
            dx_ref[:, cs] = r * gy - xv * (r * r * r) * jnp.mean(gy * xv, axis=1, keepdims=True)
            dg = dg + jnp.sum(g * xv * r, axis=0, keepdims=True)
        dg_ref[0:1, :] += dg

    assert dy_blk0 % HN_HEADS == 0 and x_blk0 % HN_HEADS == 0 and dst_blk0 % HN_HEADS == 0 and nblk % HN_HEADS == 0
    w = HN_HEADS * LANE
    in_specs = [pl.BlockSpec((ROW_TILE, w), lambda j, t: (t, dy_blk0 // HN_HEADS + j)),
                pl.BlockSpec((ROW_TILE, w), lambda j, t: (t, x_blk0 // HN_HEADS + j)),
                pl.BlockSpec((8, LANE), lambda j, t: (0, 0))]
    args = [dy, x, gain]
    aliases = {}
    if dst is not None:
        in_specs.append(pl.BlockSpec(memory_space=pl.ANY))
        args.append(dst)
        aliases = {3: 0}
        out0 = jax.ShapeDtypeStruct(dst.shape, F32)
    else:
        out0 = jax.ShapeDtypeStruct((s, (dst_blk0 + nblk) * LANE), F32)
    return pl.pallas_call(
        body, name=name, grid=(nblk // HN_HEADS, s // ROW_TILE), in_specs=in_specs,
        out_specs=[pl.BlockSpec((ROW_TILE, w), lambda j, t: (t, dst_blk0 // HN_HEADS + j)),
                   pl.BlockSpec((8, LANE), lambda j, t: (0, 0))],
        out_shape=[out0, jax.ShapeDtypeStruct((8, LANE), F32)],
        input_output_aliases=aliases, compiler_params=_params("arbitrary", "arbitrary"),
    )(*args)


def _xa_head(xq, kraw, v, qg, kg):
    q = xq * lax.rsqrt(jnp.mean(xq * xq, axis=1, keepdims=True) + EPS) * qg
    k = kraw * lax.rsqrt(jnp.mean(kraw * kraw, axis=1, keepdims=True) + EPS) * kg
    sc = mm_nt(q, k) * (XA_DIM ** -0.5)
    e = jnp.exp(sc - lax.stop_gradient(jnp.max(sc, axis=1, keepdims=True)))
    return mm(e / jnp.sum(e, axis=1, keepdims=True), v)


def xa_fwd(proj, xq_blk0, kv, qg, kg, cat):
    s = proj.shape[0]
    n_mem = kv.shape[0]

    def body(xq_ref, k_ref, v_ref, qg_ref, kg_ref, _, o_ref):
        o_ref[...] = _xa_head(xq_ref[...], k_ref[...], v_ref[...], qg_ref[0:1, :], kg_ref[0:1, :])

    gain = pl.BlockSpec((8, XA_DIM), lambda h, t: (0, 0))
    return pl.pallas_call(
        body, name="xa_fwd", grid=(XA_HEADS, s // ROW_TILE),
        in_specs=[pl.BlockSpec((ROW_TILE, XA_DIM), lambda h, t: (t, xq_blk0 // 2 + h)),
                  pl.BlockSpec((n_mem, XA_DIM), lambda h, t: (0, h)),
                  pl.BlockSpec((n_mem, XA_DIM), lambda h, t: (0, XA_HEADS + h)), gain, gain,
                  pl.BlockSpec(memory_space=pl.ANY)],
        out_specs=pl.BlockSpec((ROW_TILE, XA_DIM), lambda h, t: (t, MIX_WIDTH // XA_DIM + h)),
        out_shape=jax.ShapeDtypeStruct(cat.shape, F32), input_output_aliases={5: 0},
        compiler_params=_params("parallel", "parallel"),
    )(proj, kv, kv, qg, kg, cat)


def xa_bwd(proj, xq_blk0, kv, qg, kg, dcat, dproj_in):
    s = proj.shape[0]
    n_mem = kv.shape[0]

    def body(xq_ref, k_ref, v_ref, qg_ref, kg_ref, do_ref, _, dxq_ref, dk_ref, dv_ref, dqg_ref, dkg_ref):
        h = pl.program_id(0)
        t = pl.program_id(1)

        @pl.when(t == 0)
        def _():
            dk_ref[...] = jnp.zeros_like(dk_ref)
            dv_ref[...] = jnp.zeros_like(dv_ref)

        @pl.when((t == 0) & (h == 0))
        def _():
            dqg_ref[...] = jnp.zeros_like(dqg_ref)
            dkg_ref[...] = jnp.zeros_like(dkg_ref)

        _, vjp = jax.vjp(_xa_head, xq_ref[...], k_ref[...], v_ref[...], qg_ref[0:1, :], kg_ref[0:1, :])
        dxq, dk, dv, dqg, dkg = vjp(do_ref[...])
        dxq_ref[...] = dxq
        dk_ref[...] += dk
        dv_ref[...] += dv
        dqg_ref[0:1, :] += dqg
        dkg_ref[0:1, :] += dkg

    gain = pl.BlockSpec((8, XA_DIM), lambda h, t: (0, 0))
    kspec = pl.BlockSpec((n_mem, XA_DIM), lambda h, t: (0, h))
    vspec = pl.BlockSpec((n_mem, XA_DIM), lambda h, t: (0, XA_HEADS + h))
    return pl.pallas_call(
        body, name="xa_bwd", grid=(XA_HEADS, s // ROW_TILE),
        in_specs=[pl.BlockSpec((ROW_TILE, XA_DIM), lambda h, t: (t, xq_blk0 // 2 + h)), kspec, vspec, gain, gain,
                  pl.BlockSpec((ROW_TILE, XA_DIM), lambda h, t: (t, MIX_WIDTH // XA_DIM + h)),
                  pl.BlockSpec(memory_space=pl.ANY)],
        out_specs=[pl.BlockSpec((ROW_TILE, XA_DIM), lambda h, t: (t, xq_blk0 // 2 + h)), kspec, kspec, gain, gain],
        out_shape=[jax.ShapeDtypeStruct(dproj_in.shape, F32), jax.ShapeDtypeStruct((n_mem, XA_WIDTH), F32),
                   jax.ShapeDtypeStruct((n_mem, XA_WIDTH), F32), jax.ShapeDtypeStruct((8, XA_DIM), F32),
                   jax.ShapeDtypeStruct((8, XA_DIM), F32)],
        input_output_aliases={6: 0}, compiler_params=_params("arbitrary", "arbitrary"),
    )(proj, kv, kv, qg, kg, dcat, dproj_in)


GATE_ROWS = 1024


def gate_fwd(cat, proj, z_blk0):
    s = cat.shape[0]

    def body(c_ref, z_ref, y_ref):
        y_ref[...] = (c_ref[...] * _silu(z_ref[...])).astype(BF16)

    w = 2 * LANE
    rt = min(GATE_ROWS, s)
    return pl.pallas_call(
        body, name="gate_fwd", grid=(INNER // w, s // rt),
        in_specs=[pl.BlockSpec((rt, w), lambda j, t: (t, j)),
                  pl.BlockSpec((rt, w), lambda j, t: (t, z_blk0 // 2 + j))],
        out_specs=pl.BlockSpec((rt, w), lambda j, t: (t, j)),
        out_shape=jax.ShapeDtypeStruct((s, INNER), BF16), compiler_params=_params("parallel", "parallel"),
    )(cat, proj)


def gate_bwd(dy, cat, proj, z_blk0):
    s = cat.shape[0]

    def body(dy_ref, c_ref, z_ref, dc_ref, dz_ref):
        z = z_ref[...]
        g = dy_ref[...]
        dc_ref[...] = g * _silu(z)
        dz_ref[...] = g * c_ref[...] * _silu_grad(z)

    w = 2 * LANE
    rt = min(GATE_ROWS, s)
    tile = pl.BlockSpec((rt, w), lambda j, t: (t, j))
    ztile = pl.BlockSpec((rt, w), lambda j, t: (t, z_blk0 // 2 + j))
    return pl.pallas_call(
        body, name="gate_bwd", grid=(INNER // w, s // rt), in_specs=[tile, tile, ztile],
        out_specs=[tile, ztile],
        out_shape=[jax.ShapeDtypeStruct((s, INNER), F32), jax.ShapeDtypeStruct(proj.shape, F32)],
        compiler_params=_params("parallel", "parallel"),
    )(dy, cat, proj)


NORM_ROWS = 256


def rms_fwd(x, gain, with_transpose=False, exchange=None):
    s, d = x.shape

    def body(x_ref, g_ref, o_ref, *t_ref):
        xv = x_ref[...]
        r = lax.rsqrt(jnp.mean(xv * xv, axis=1, keepdims=True) + EPS)
        y = xv * r * g_ref[0:1, :]
        o_ref[...] = y.astype(BF16)
        if with_transpose:
            t_ref[0][...] = y.T.astype(BF16)

    tile = pl.BlockSpec((NORM_ROWS, d), lambda t: (t, 0))
    res = hosted_call(
        body, exchange, name="rms_fwd", grid=(s // NORM_ROWS,),
        in_specs=[tile, pl.BlockSpec((8, d), lambda t: (0, 0))],
        out_specs=[tile] + ([pl.BlockSpec((d, NORM_ROWS), lambda t: (0, t))] if with_transpose else []),
        out_shape=[jax.ShapeDtypeStruct((s, d), BF16)] + ([jax.ShapeDtypeStruct((d, s), BF16)] if with_transpose else []),
        sem=("parallel",),
    )(x, gain)
    return res if (with_transpose or exchange is not None) else res[0]


def rms_bwd(dh, x, gain, dres):
    s, d = x.shape

    def body(*refs):
        if dres is not None:
            dh_ref, x_ref, g_ref, dr_ref, dx_ref, dg_ref = refs
        else:
            dh_ref, x_ref, g_ref, dx_ref, dg_ref = refs

        @pl.when(pl.program_id(0) == 0)
        def _():
            dg_ref[...] = jnp.zeros_like(dg_ref)

        xv = x_ref[...]
        g = dh_ref[...]
        r = lax.rsqrt(jnp.mean(xv * xv, axis=1, keepdims=True) + EPS)
        gy = g * g_ref[0:1, :]
        dx = r * gy - xv * (r * r * r) * jnp.mean(gy * xv, axis=1, keepdims=True)
        dx_ref[...] = dx + dr_ref[...] if dres is not None else dx
        dg_ref[0:1, :] += jnp.sum(g * xv * r, axis=0, keepdims=True)

    tile = pl.BlockSpec((NORM_ROWS, d), lambda t: (t, 0))
    gspec = pl.BlockSpec((8, d), lambda t: (0, 0))
    args = [dh, x, gain] + ([dres] if dres is not None else [])
    return pl.pallas_call(
        body, name="rms_bwd", grid=(s // NORM_ROWS,),
        in_specs=[tile, tile, gspec] + ([tile] if dres is not None else []),
        out_specs=[tile, gspec],
        out_shape=[jax.ShapeDtypeStruct((s, d), F32), jax.ShapeDtypeStruct((8, d), F32)],
        compiler_params=_params("arbitrary"),
    )(*args)


def loss_fwd_bwd(y, target):
    s, d = y.shape

    def body(y_ref, t_ref, l_ref, dy_ref):
        @pl.when(pl.program_id(0) == 0)
        def _():
            l_ref[...] = jnp.zeros_like(l_ref)

        err = y_ref[...] - t_ref[...]
        dy_ref[...] = err * (1.0 / d)
        part = 0.5 * jnp.sum(jnp.mean(err * err, axis=1, keepdims=True), axis=0, keepdims=True)
        r = lax.broadcasted_iota(jnp.int32, (8, LANE), 0)
        c = lax.broadcasted_iota(jnp.int32, (8, LANE), 1)
        l_ref[...] += jnp.where((r == 0) & (c == 0), part, 0.0)

    tile = pl.BlockSpec((NORM_ROWS, d), lambda t: (t, 0))
    return pl.pallas_call(
        body, name="loss", grid=(s // NORM_ROWS,), in_specs=[tile, tile],
        out_specs=[pl.BlockSpec((8, LANE), lambda t: (0, 0)), tile],
        out_shape=[jax.ShapeDtypeStruct((8, LANE), F32), jax.ShapeDtypeStruct((s, d), F32)],
        compiler_params=_params("arbitrary"),
    )(y, target)


def matmul(a, b, mode, out_dtype, tm, tn, tk, name, add=None, b_blocked=False, out_blocks=None, exchange=None):
    if b_blocked:
        nb, _, width = b.shape
        bshape = (b.shape[1], nb * width)
    else:
        bshape = b.shape
    if mode == "tn":
        (kdim, m), n = a.shape, bshape[1]
    else:
        (m, kdim), n = a.shape, (bshape[1] if mode == "nn" else bshape[0])
    tm, tn, tk = min(tm, m), min(tn, n), min(tk, kdim)
    assert m % tm == 0 and n % tn == 0 and kdim % tk == 0, (name, m, n, kdim)
    nk = kdim // tk
    dims = {"nn": _NN, "nt": _NT, "tn": _TN}[mode]

    def body(*refs):
        if add is not None:
            a_ref, b_ref, add_ref, o_ref, acc_ref = refs
        else:
            a_ref, b_ref, o_ref, acc_ref = refs
        k = pl.program_id(2)

        @pl.when(k == 0)
        def _():
            acc_ref[...] = jnp.zeros_like(acc_ref)

        acc_ref[...] += _dg(a_ref[...], b_ref[...], dims)

        @pl.when(k == nk - 1)
        def _():
            r = acc_ref[...]
            if add is not None:
                r = r + add_ref[...]
            o_ref[...] = r.astype(out_dtype)

    a_spec = pl.BlockSpec((tk, tm), lambda i, j, k: (k, i)) if mode == "tn" else pl.BlockSpec((tm, tk), lambda i, j, k: (i, k))
    if b_blocked and mode == "nn":
        per = width // tn
        assert width % tn == 0
        b_spec = pl.BlockSpec((None, tk, tn), lambda i, j, k: (j // per, k, j % per))
    elif b_blocked and mode == "nt":
        per = width // tk
        assert width % tk == 0
        b_spec = pl.BlockSpec((None, tn, tk), lambda i, j, k: (k // per, j, k % per))
    elif mode == "nt":
        b_spec = pl.BlockSpec((tn, tk), lambda i, j, k: (j, k))
    else:
        assert not b_blocked
        b_spec = pl.BlockSpec((tk, tn), lambda i, j, k: (k, j))
    add_spec = pl.BlockSpec((tm, tn), lambda i, j, k: (i, j))
    if out_blocks is not None:
        operb = (n // out_blocks) // tn
        assert (n // out_blocks) % tn == 0 and add is None
        o_spec = pl.BlockSpec((None, tm, tn), lambda i, j, k: (j // operb, i, j % operb))
        out_shape = jax.ShapeDtypeStruct((out_blocks, m, n // out_blocks), out_dtype)
    else:
        o_spec = add_spec
        out_shape = jax.ShapeDtypeStruct((m, n), out_dtype)
    res = hosted_call(
        body, exchange, name=name, grid=(m // tm, n // tn, nk),
        in_specs=[a_spec, b_spec] + ([add_spec] if add is not None else []), out_specs=[o_spec],
        out_shape=[out_shape], scratch_shapes=[pltpu.VMEM((tm, tn), F32)],
        sem=("parallel", "parallel", "arbitrary"),
    )(*([a, b] + ([add] if add is not None else [])))
    return res[0] if exchange is None else res


_HBM = pl.BlockSpec(memory_space=pltpu.HBM)


def _me():
    return lax.axis_index("x"), lax.axis_index("y"), lax.axis_index("c")


def _flat(p):
    return 4 * p[0] + 2 * p[1] + p[2]


def _flip(p, r):
    return tuple((1 - v) if (r >> (2 - a)) & 1 else v for a, v in enumerate(p))


class Exchange:
    def __init__(self, srcs, out_shapes, sems, start, finish, alias=None):
        self.srcs, self.out_shapes, self.sems = list(srcs), list(out_shapes), list(sems)
        self.start, self.finish, self.alias = start, finish, dict(alias or {})


def hosted_call(body, exchange, *, name, grid, in_specs, out_specs, out_shape, scratch_shapes=(),
                input_output_aliases=None, sem=()):
    in_specs, out_specs, out_shape = list(in_specs), list(out_specs), list(out_shape)
    scratch_shapes = list(scratch_shapes)
    aliases = input_output_aliases or {}
    if exchange is None:
        call = pl.pallas_call(body, name=name, grid=grid, in_specs=in_specs, out_specs=out_specs, out_shape=out_shape,
                              scratch_shapes=scratch_shapes, input_output_aliases=aliases, compiler_params=_params(*sem))
        return lambda *args: list(call(*args))
    ni, no, ns = len(in_specs), len(out_specs), len(scratch_shapes)
    xi, xo = len(exchange.srcs), len(exchange.out_shapes)

    def wrapped(*refs):
        ins, refs = refs[:ni], refs[ni:]
        xin, refs = refs[:xi], refs[xi:]
        outs, refs = refs[:no], refs[no:]
        xout, refs = refs[:xo], refs[xo:]
        scr, xsem = refs[:ns], refs[ns:]
        first = functools.reduce(lambda p, q: p & q, [pl.program_id(d) == 0 for d in range(len(grid))])
        last = functools.reduce(lambda p, q: p & q, [pl.program_id(d) == grid[d] - 1 for d in range(len(grid))])

        @pl.when(first)
        def _():
            exchange.start(xin, xout, xsem)

        body(*ins, *outs, *scr)

        @pl.when(last)
        def _():
            exchange.finish(xin, xout, xsem)

    call = pl.pallas_call(
        wrapped, name=name, grid=grid, in_specs=in_specs + [_HBM] * xi, out_specs=out_specs + [_HBM] * xo,
        out_shape=out_shape + exchange.out_shapes, scratch_shapes=scratch_shapes + exchange.sems,
        input_output_aliases={**aliases, **{ni + i: no + j for i, j in exchange.alias.items()}},
        compiler_params=_params(*(("arbitrary",) * len(grid))))
    return lambda *args: list(call(*args, *exchange.srcs))


def run_exchange(exchange, name):
    xi, xo = len(exchange.srcs), len(exchange.out_shapes)

    def body(*refs):
        exchange.start(refs[:xi], refs[xi:xi + xo], refs[xi + xo:])
        exchange.finish(refs[:xi], refs[xi:xi + xo], refs[xi + xo:])

    return list(pl.pallas_call(body, name=name, in_specs=[_HBM] * xi, out_specs=[_HBM] * xo,
                               out_shape=exchange.out_shapes, scratch_shapes=exchange.sems,
                               input_output_aliases=exchange.alias)(*exchange.srcs))


def gather_exchange(shards, rows=None, into=None):
    n = len(shards)

    def parts(srcs, outs, sems):
        send_sems, recv_sems, local_sems = sems
        me = _me()
        x, y, c = me
        chips = [(1 - x, y), (x, 1 - y), (1 - x, 1 - y)]

        def place(a, block):
            dst = outs[a].at[_flat(block)]
            return dst if rows is None else dst.at[pl.ds(rows[0], shards[a].shape[0])]

        def copy(a, k, block, to, src=None):
            dst = place(a, block)
            return pltpu.make_async_remote_copy(src_ref=dst if src is None else src, dst_ref=dst,
                                                send_sem=send_sems.at[a, k], recv_sem=recv_sems.at[a, k],
                                                device_id=to, device_id_type=MESH)

        mine = [pltpu.make_async_copy(srcs[a], place(a, me), local_sems.at[a]) for a in range(n)]
        own = []
        for a in range(n):
            own.append(copy(a, 0, me, (x, y, 1 - c), src=srcs[a]))
            own += [copy(a, 1 + j, me, (*chip, c), src=srcs[a]) for j, chip in enumerate(chips)]
        return me, chips, copy, mine, own

    def start(srcs, outs, sems):
        _, _, _, mine, own = parts(srcs, outs, sems)
        for cp in mine + own:
            cp.start()

    def finish(srcs, outs, sems):
        me, chips, copy, mine, own = parts(srcs, outs, sems)
        x, y, c = me
        passed = []
        for j, chip in enumerate(chips):
            for a in range(n):
                copy(a, 1 + j, (*chip, c), me).wait_recv()
                fwd = copy(a, 4 + j, (*chip, c), (x, y, 1 - c))
                fwd.start()
                passed.append(fwd)
        for a in range(n):
            copy(a, 0, (x, y, 1 - c), me).wait_recv()
            for j, chip in enumerate(chips):
                copy(a, 4 + j, (*chip, 1 - c), me).wait_recv()
        for cp in own + passed:
            cp.wait_send()
        for cp in mine:
            cp.wait()

    dma = pltpu.SemaphoreType.DMA
    full = lambda s: s.shape if rows is None else (rows[1],) + s.shape[1:]
    return Exchange(list(shards) + list(into or []), [jax.ShapeDtypeStruct((N_DEV,) + full(s), s.dtype) for s in shards],
                    [dma((n, 7)), dma((n, 7)), dma((n,))], start, finish,
                    alias={n + a: a for a in range(n)} if into else None)


def pair_exchange(srcs):
    n = len(srcs)

    def copies(srcs_, outs, sems):
        send_sems, recv_sems = sems
        x, y, c = _me()
        return [pltpu.make_async_remote_copy(src_ref=srcs_[a].at[:, 1 - c], dst_ref=outs[a], send_sem=send_sems.at[a],
                                             recv_sem=recv_sems.at[a], device_id=(x, y, 1 - c), device_id_type=MESH)
                for a in range(n)]

    def start(srcs_, outs, sems):
        for cp in copies(srcs_, outs, sems):
            cp.start()

    def finish(srcs_, outs, sems):
        for cp in copies(srcs_, outs, sems):
            cp.wait()

    dma = pltpu.SemaphoreType.DMA
    return Exchange(srcs, [jax.ShapeDtypeStruct((4,) + s.shape[2:], s.dtype) for s in srcs], [dma((n,)), dma((n,))],
                    start, finish)


def pair_sum(src, half, name):
    _, _, rows, cols = src.shape
    tr = min(rows, 256)

    def body(x_ref, h_ref, o_ref):
        c = lax.axis_index("c")
        o_ref[0] = (x_ref[0, c].astype(F32) + h_ref[0].astype(F32)).astype(BF16)

    return pl.pallas_call(
        body, name=name, grid=(4, rows // tr),
        in_specs=[pl.BlockSpec((1, 2, tr, cols), lambda ch, t: (ch, 0, t, 0)),
                  pl.BlockSpec((1, tr, cols), lambda ch, t: (ch, t, 0))],
        out_specs=pl.BlockSpec((1, tr, cols), lambda ch, t: (ch, t, 0)),
        out_shape=jax.ShapeDtypeStruct(half.shape, BF16), compiler_params=_params("parallel", "parallel"),
    )(src, half)


def merge_exchanges(a, b):
    ns, no, nm = len(a.srcs), len(a.out_shapes), len(a.sems)

    def start(srcs, outs, sems):
        a.start(srcs[:ns], outs[:no], sems[:nm])
        b.start(srcs[ns:], outs[no:], sems[nm:])

    def finish(srcs, outs, sems):
        a.finish(srcs[:ns], outs[:no], sems[:nm])
        b.finish(srcs[ns:], outs[no:], sems[nm:])

    alias = {**a.alias, **{ns + i: no + j for i, j in b.alias.items()}}
    return Exchange(a.srcs + b.srcs, a.out_shapes + b.out_shapes, a.sems + b.sems, start, finish, alias)


def chip_exchange(parts, slots, recv_shapes, rows=None, into=None):
    n = len(parts)
    win = (lambda ref: ref) if rows is None else (lambda ref: ref.at[pl.ds(rows[0], rows[1])])

    def plan(srcs, outs, sems):
        send_sems, recv_sems, local_sems = sems
        x, y, c = _me()
        chip = 2 * x + y
        mine = [pltpu.make_async_copy(win(srcs[a].at[chip]), win(outs[slots[a][0]].at[chip, slots[a][1]]),
                                      local_sems.at[a]) for a in range(n)]
        sends, arrivals = [], []
        for r in (1, 2, 3):
            px = (1 - x) if r & 2 else x
            py = (1 - y) if r & 1 else y
            for a in range(n):
                ri, layer = slots[a]
                sends.append(pltpu.make_async_remote_copy(
                    src_ref=win(srcs[a].at[2 * px + py]), dst_ref=win(outs[ri].at[chip, layer]),
                    send_sem=send_sems.at[a, r - 1],
                    recv_sem=recv_sems.at[a, r - 1], device_id=(px, py, c), device_id_type=MESH))
                land = win(outs[ri].at[2 * px + py, layer])
                arrivals.append(pltpu.make_async_remote_copy(
                    src_ref=land, dst_ref=land, send_sem=send_sems.at[a, r - 1], recv_sem=recv_sems.at[a, r - 1],
                    device_id=(px, py, c), device_id_type=MESH))
        return mine, sends, arrivals

    def start(srcs, outs, sems):
        mine, sends, _ = plan(srcs, outs, sems)
        for cp in mine + sends:
            cp.start()

    def finish(srcs, outs, sems):
        mine, sends, arrivals = plan(srcs, outs, sems)
        for cp in arrivals:
            cp.wait_recv()
        for cp in sends:
            cp.wait_send()
        for cp in mine:
            cp.wait()

    dma = pltpu.SemaphoreType.DMA
    return Exchange(list(parts) + list(into or []), [jax.ShapeDtypeStruct(s, BF16) for s in recv_shapes],
                    [dma((n, 3)), dma((n, 3)), dma((n,))], start, finish,
                    alias={n + k: k for k in range(len(recv_shapes))} if into else None)


SMALL_ROWS = 24


def all_reduce_small(pack):
    def body(p_ref, o_ref, buf, send_sems, recv_sems):
        me = _me()
        buf[_flat(me)] = p_ref[...]
        sent = []
        for r in range(1, N_DEV):
            peer = _flip(me, r)
            cp = pltpu.make_async_remote_copy(src_ref=p_ref, dst_ref=buf.at[_flat(me)], send_sem=send_sems.at[r - 1],
                                              recv_sem=recv_sems.at[r - 1], device_id=peer, device_id_type=MESH)
            cp.start()
            sent.append(cp)
        for r in range(1, N_DEV):
            peer = _flip(me, r)
            land = buf.at[_flat(peer)]
            pltpu.make_async_remote_copy(src_ref=land, dst_ref=land, send_sem=send_sems.at[r - 1],
                                         recv_sem=recv_sems.at[r - 1], device_id=peer, device_id_type=MESH).wait_recv()
        for cp in sent:
            cp.wait_send()
        acc = buf[0]
        for d in range(1, N_DEV):
            acc = acc + buf[d]
        o_ref[...] = acc

    vm = pl.BlockSpec(memory_space=pltpu.VMEM)
    return pl.pallas_call(
        body, name="all_reduce_small", in_specs=[vm], out_specs=vm,
        out_shape=jax.ShapeDtypeStruct(pack.shape, F32),
        scratch_shapes=[pltpu.VMEM((N_DEV,) + pack.shape, F32), pltpu.SemaphoreType.DMA((7,)),
                        pltpu.SemaphoreType.DMA((7,))],
    )(pack)


def _adamw(w, g, m, v):
    m = ADAM_B1 * m + (1.0 - ADAM_B1) * g
    v = ADAM_B2 * v + (1.0 - ADAM_B2) * (g * g)
    m_hat = m / (1.0 - ADAM_B1 ** ADAM_STEP)
    v_hat = v / (1.0 - ADAM_B2 ** ADAM_STEP)
    delta = -ADAM_LR * (m_hat / (jnp.sqrt(v_hat) + ADAM_EPS) + ADAM_WD * w)
    return delta, m, v


ADAM_ROWS = 256


def reduce_adamw(recv, w, m, v, name):
    nl, rows, cols = w.shape
    nslot, cp = recv.shape[0], recv.shape[3]

    def body(r_ref, w_ref, m_ref, v_ref, g_ref, d_ref, mo_ref, vo_ref):
        g = r_ref[0, 0].astype(F32)
        for slot in range(1, nslot):
            g = g + r_ref[slot, 0].astype(F32)
        if cp != cols:
            g = g[:, :cols]
        delta, m_new, v_new = _adamw(w_ref[0], g, m_ref[0], v_ref[0])
        g_ref[0] = g
        d_ref[0] = delta
        mo_ref[0] = m_new
        vo_ref[0] = v_new

    tile = pl.BlockSpec((1, ADAM_ROWS, cols), lambda l, t: (l, t, 0))
    out = jax.ShapeDtypeStruct(w.shape, F32)
    return pl.pallas_call(
        body, name=name, grid=(nl, rows // ADAM_ROWS),
        in_specs=[pl.BlockSpec((nslot, 1, ADAM_ROWS, cp), lambda l, t: (0, l, t, 0)), tile, tile, tile],
        out_specs=[tile, tile, tile, tile], out_shape=[out, out, out, out],
        compiler_params=_params("parallel", "parallel"),
    )(recv, w, m, v)


def reduce_adamw_t(recv, w_t, m_t, v_t, name):
    cols, rows = w_t.shape
    nslot, cp = recv.shape[0], recv.shape[3]
    tr = 256

    def body(r_ref, w_ref, m_ref, v_ref, g_ref, d_ref, mo_ref, vo_ref):
        g = r_ref[0, 0].astype(F32)
        for slot in range(1, nslot):
            g = g + r_ref[slot, 0].astype(F32)
        g_ref[...] = g.T[:cols, :]
        delta, m_new, v_new = _adamw(w_ref[...], g_ref[...], m_ref[...], v_ref[...])
        d_ref[...] = delta
        mo_ref[...] = m_new
        vo_ref[...] = v_new

    tile = pl.BlockSpec((cols, tr), lambda t: (0, t))
    out = jax.ShapeDtypeStruct((cols, rows), F32)
    return pl.pallas_call(
        body, name=name, grid=(rows // tr,),
        in_specs=[pl.BlockSpec((nslot, 1, tr, cp), lambda t: (0, 0, t, 0)), tile, tile, tile],
        out_specs=[tile, tile, tile, tile], out_shape=[out, out, out, out], compiler_params=_params("parallel"),
    )(recv, w_t, m_t, v_t)


def adamw_small(g, w, m, v):
    def body(g_ref, w_ref, m_ref, v_ref, d_ref, mo_ref, vo_ref):
        d_ref[...], mo_ref[...], vo_ref[...] = _adamw(w_ref[...], g_ref[...], m_ref[...], v_ref[...])

    out = jax.ShapeDtypeStruct(g.shape, F32)
    return pl.pallas_call(body, name="adamw_small", out_shape=[out, out, out])(g, w, m, v)


def _row8(v):
    return jnp.pad(v.reshape(1, -1).astype(F32), ((0, 7), (0, 0)))


def _row8_lanes(v, width=LANE):
    return jnp.pad(v.reshape(1, -1).astype(F32), ((0, 7), (0, width - v.size)))


def _pack_rows(parts):
    rows = []
    for p in parts:
        p = p.reshape(-1).astype(F32)
        nrow = -(-p.size // D_MODEL)
        rows.append(jnp.pad(p, (0, nrow * D_MODEL - p.size)).reshape(nrow, D_MODEL))
    out = jnp.concatenate(rows, axis=0)
    return jnp.pad(out, ((0, SMALL_ROWS - out.shape[0]), (0, 0)))


def _unpack_rows(pack, shapes):
    out, r = [], 0
    for shp in shapes:
        size = 1
        for d in shp:
            size *= d
        nrow = -(-size // D_MODEL)
        out.append(pack[r:r + nrow].reshape(-1)[:size].reshape(shp))
        r += nrow
    return out


def _dn_weight_layout(gathered):
    split = DN_QKV + 2 * DN_V_HEADS
    pieces = []
    for d in range(N_DEV):
        lo, hi = d * DN_SHARD, (d + 1) * DN_SHARD
        if lo < split < hi:
            pieces += [gathered[d, :, :split - lo], jnp.zeros((D_MODEL, DN_COLS - DN_PROJ), gathered.dtype),
                       gathered[d, :, split - lo:DN_SHARD]]
        else:
            pieces.append(gathered[d, :, :DN_SHARD])
    return jnp.concatenate(pieces, axis=1)


def _dn_grad_blocks(dw):
    split = DN_QKV + 2 * DN_V_HEADS
    gap = DN_COLS - DN_PROJ
    local = lambda c: c if c <= split else c + gap
    zeros = jnp.zeros((D_MODEL, DN_SHARD_PAD - DN_SHARD), dw.dtype)
    blocks = []
    for d in range(N_DEV):
        lo, hi = d * DN_SHARD, (d + 1) * DN_SHARD
        if lo < split < hi:
            parts = [dw[:, lo:split], dw[:, split + gap:hi + gap]]
        else:
            parts = [dw[:, local(lo):local(lo) + DN_SHARD]]
        blocks.append(jnp.concatenate(parts + [zeros], axis=1))
    return jnp.stack(blocks)


def kernel(x, mem, norm_g, mem_norm_g, mem_w_kv, xa_q_norm_g, xa_k_norm_g, w_out, dn_w_in, dn_conv_w, dn_a_log, dn_dt_bias, dn_out_norm_g, sb_w_in, sb_q_norm_g, sb_k_norm_g, loss_target, m_norm_g, m_mem_norm_g, m_mem_w_kv, m_xa_q_norm_g, m_xa_k_norm_g, m_w_out, m_dn_w_in, m_dn_conv_w, m_dn_a_log, m_dn_dt_bias, m_dn_out_norm_g, m_sb_w_in, m_sb_q_norm_g, m_sb_k_norm_g, v_norm_g, v_mem_norm_g, v_mem_w_kv, v_xa_q_norm_g, v_xa_k_norm_g, v_w_out, v_dn_w_in, v_dn_conv_w, v_dn_a_log, v_dn_dt_bias, v_dn_out_norm_g, v_sb_w_in, v_sb_q_norm_g, v_sb_k_norm_g):
    x0, memv, target = x[0], mem[0], loss_target[0]
    my_dev = 4 * lax.axis_index("x") + 2 * lax.axis_index("y") + lax.axis_index("c")

    dn_shard = jnp.pad(dn_w_in[0].astype(BF16), ((0, 0), (0, DN_SHARD_PAD - DN_SHARD)))
    w_out_b = [w_out[i].astype(BF16) for i in range(2)]
    w_kv_b = [mem_w_kv[i].astype(BF16) for i in range(2)]
    conv_shard = jnp.pad(dn_conv_w[0], ((0, 4), (0, 0)))
    ng = [_row8(norm_g[0]), _row8(norm_g[1])]
    mem_g = _row8(mem_norm_g)
    xqg = [_row8(xa_q_norm_g[0]), _row8(xa_q_norm_g[1])]
    xkg = [_row8(xa_k_norm_g[0]), _row8(xa_k_norm_g[1])]
    alog, dtb = _row8_lanes(dn_a_log[0]), _row8_lanes(dn_dt_bias[0])
    out_g, sbq_g, sbk_g = _row8(dn_out_norm_g[0]), _row8(sb_q_norm_g[0]), _row8(sb_k_norm_g[0])

    mem_n = rms_fwd(memv, mem_g)
    h0, h0_t, g_dn, g_conv = rms_fwd(x0, ng[0], with_transpose=True,
                                     exchange=gather_exchange([dn_shard, conv_shard]))
    w_dn = _dn_weight_layout(g_dn)
    conv_w = jnp.transpose(g_conv, (1, 0, 2)).reshape(8, DN_QKV)
    proj0, g_wo0 = matmul(h0, w_dn, "nn", F32, 2048, 640, 2048, "proj_dn", exchange=gather_exchange([w_out_b[0]]))
    act, g_kv0 = dn_conv_fwd(proj0, conv_w, exchange=gather_exchange([w_kv_b[0]]))
    sb_shard, half = sb_w_in[0].astype(BF16), 9 * D_MODEL // 16
    u0, w_, qd, kd, qk, cd, w_sb = dn_prep_fwd(
        act, proj0, alog, dtb, exchange=gather_exchange([sb_shard[:half]], rows=(0, D_MODEL)))
    o_raw, states, w_sb = dn_scan_fwd(
        u0, w_, qd, kd, qk, cd, MIX_WIDTH,
        exchange=gather_exchange([sb_shard[half:]], rows=(half, D_MODEL), into=[w_sb]))
    w_o = [g_wo0.reshape(INNER, D_MODEL), None]
    w_kv = [g_kv0.reshape(D_MODEL, 2 * XA_WIDTH), None]
    kv = [matmul(mem_n, w_kv[0], "nn", F32, 256, 1024, 2048, "kv0"), None]
    cat0 = head_norm_fwd(o_raw, 0, DN_V_HEADS, out_g[None], F32, INNER, "dn_out_norm")
    cat0 = xa_fwd(proj0, DN_XQ_BLK, kv[0], xqg[0], xkg[0], cat0)
    y0 = gate_fwd(cat0, proj0, DN_Z_BLK)
    x1 = matmul(y0, w_o[0], "nn", F32, 1024, 1024, 2048, "out_proj0", add=x0)

    h1, h1_t = rms_fwd(x1, ng[1], with_transpose=True)
    proj1 = matmul(h1, w_sb, "nn", F32, 2048, 896, 2048, "proj_sb", b_blocked=True)
    qkn = head_norm_fwd(proj1, 0, 2 * SB_HEADS, jnp.stack([sbq_g, sbk_g]), BF16, 2 * MIX_WIDTH, "sb_qk_norm")
    cat1, ltot, g_wo1, g_kv1 = sb_fwd(qkn, proj1, 2 * SB_HEADS, INNER,
                                      exchange=gather_exchange([w_out_b[1], w_kv_b[1]]))
    w_o[1] = g_wo1.reshape(INNER, D_MODEL)
    w_kv[1] = g_kv1.reshape(D_MODEL, 2 * XA_WIDTH)
    kv[1] = matmul(mem_n, w_kv[1], "nn", F32, 256, 1024, 2048, "kv1")
    cat1 = xa_fwd(proj1, SB_XQ_BLK, kv[1], xqg[1], xkg[1], cat1)
    y1 = gate_fwd(cat1, proj1, SB_Z_BLK)
    x2 = matmul(y1, w_o[1], "nn", F32, 1024, 1024, 2048, "out_proj1", add=x1)
    loss_part, dx2 = loss_fwd_bwd(x2, target)

    dy1 = matmul(dx2, w_o[1], "nt", F32, 1024, 1024, 2048, "d_y1")
    dw_o1 = matmul(y1, dx2, "tn", BF16, 1024, 1024, 2048, "d_w_out1")
    dcat1, dproj1 = gate_bwd(dy1, cat1, proj1, SB_Z_BLK)
    dproj1, dxk1, dxv1, dxqg1, dxkg1 = xa_bwd(proj1, SB_XQ_BLK, kv[1], xqg[1], xkg[1], dcat1, dproj1)
    dqn, dkn, dproj1 = sb_bwd(qkn, proj1, 2 * SB_HEADS, dcat1, ltot, dproj1)
    dproj1, d_sbq = head_norm_bwd(dqn, 0, proj1, 0, SB_HEADS, sbq_g, dproj1, 0, "sb_q_norm_bwd")
    dproj1, d_sbk = head_norm_bwd(dkn, 0, proj1, SB_HEADS, SB_HEADS, sbk_g, dproj1, SB_HEADS, "sb_k_norm_bwd")
    dw_sb = matmul(h1_t, dproj1, "nn", BF16, 1024, 896, 2048, "d_w_sb", out_blocks=N_DEV)
    dh1 = matmul(dproj1, w_sb, "nt", F32, 1024, 1024, 1792, "d_h1", b_blocked=True)
    dx1, d_ng1 = rms_bwd(dh1, x1, ng[1], dx2)

    by_owner = lambda g, rows: g.reshape(4, 2, rows, g.size // (N_DEV * rows))
    sb_grad = dw_sb.reshape(4, 2, D_MODEL, SB_PROJ // N_DEV)
    dy0, sb_half = matmul(dx1, w_o[0], "nt", F32, 1024, 1024, 2048, "d_y0", exchange=pair_exchange([sb_grad]))
    sb_sum = pair_sum(sb_grad, sb_half, "pair_sum_sb")
    dw_o0 = matmul(y0, dx1, "tn", BF16, 1024, 1024, 2048, "d_w_out0")
    dcat0, dproj0 = gate_bwd(dy0, cat0, proj0, DN_Z_BLK)
    dproj0, dxk0, dxv0, dxqg0, dxkg0 = xa_bwd(proj0, DN_XQ_BLK, kv[0], xqg[0], xkg[0], dcat0, dproj0)

    dkv = [jnp.concatenate([dxk0, dxv0], axis=1), jnp.concatenate([dxk1, dxv1], axis=1)]
    dw_kv = [matmul(mem_n, dkv[i], "tn", BF16, 1024, 1024, 256, f"d_w_kv{i}") for i in range(2)]
    dmem_n = matmul(dkv[0], w_kv[0], "nt", F32, 256, 1024, 2048, "d_mem_n0")
    dmem_n = matmul(dkv[1], w_kv[1], "nt", F32, 256, 1024, 2048, "d_mem_n1", add=dmem_n)
    _, d_memg = rms_bwd(dmem_n, memv, mem_g, None)

    grads = [by_owner(dw_o0, INNER // N_DEV), by_owner(dw_o1, INNER // N_DEV),
             by_owner(dw_kv[0], D_MODEL // N_DEV), by_owner(dw_kv[1], D_MODEL // N_DEV)]
    sb_recv_shape = [(4, 1, D_MODEL, SB_PROJ // N_DEV)]
    sb_first = D_MODEL // 2
    do_raw, d_outg = head_norm_bwd(dcat0, 0, o_raw, 0, DN_V_HEADS, out_g, None, 0, "dn_out_norm_bwd")
    du0, dw_, dqd, dkd, dqk, dcd, *halves, recv_sb = dn_scan_bwd(
        do_raw, u0, w_, qd, kd, qk, cd, states,
        exchange=merge_exchanges(pair_exchange(grads),
                                 chip_exchange([sb_sum], [(0, 0)], sb_recv_shape, rows=(0, sb_first))))
    sums = [pair_sum(g, h, f"pair_sum{i}") for i, (g, h) in enumerate(zip(grads, halves))]
    to_chips = merge_exchanges(
        chip_exchange(sums, [(0, 0), (0, 1), (1, 0), (1, 1)],
                      [(4, 2, INNER // N_DEV, D_MODEL), (4, 2, D_MODEL // N_DEV, 2 * XA_WIDTH)]),
        chip_exchange([sb_sum], [(0, 0)], sb_recv_shape, rows=(sb_first, D_MODEL - sb_first), into=[recv_sb]))
    dq_a, dk_a, dv_a, dab, d_alog, d_dtb, recv_wo, recv_kv, recv_sb = dn_prep_bwd(
        act, proj0, alog, dtb, du0, dw_, dqd, dkd, dqk, dcd, exchange=to_chips)
    dproj0, dcw_q = dn_conv_bwd(dq_a, proj0, conv_w, dproj0, None, 0, "dn_conv_bwd_q")
    dproj0, dcw_k = dn_conv_bwd(dk_a, proj0, conv_w, dproj0, None, DN_QK_HEADS, "dn_conv_bwd_k")
    dproj0, dcw_v = dn_conv_bwd(dv_a, proj0, conv_w, dproj0, dab, 2 * DN_QK_HEADS, "dn_conv_bwd_v")
    dw_dn = matmul(h0_t, dproj0, "nn", BF16, 1024, 1152, 2048, "d_w_dn")
    dn_grad = _dn_grad_blocks(dw_dn).reshape(4, 2, D_MODEL, DN_SHARD_PAD)
    dn_half, = run_exchange(pair_exchange([dn_grad]), "pair_dn")
    dn_sum = pair_sum(dn_grad, dn_half, "pair_sum_dn")
    dh0, recv_dn = matmul(dproj0, w_dn, "nt", F32, 1024, 1024, 2304, "d_h0",
                          exchange=chip_exchange([dn_sum], [(0, 0)], [(4, 1, D_MODEL, DN_SHARD_PAD)]))
    grad_x, d_ng0 = rms_bwd(dh0, x0, ng[0], dx1)

    big = {
        "dn_w_in": tuple(jnp.transpose(a)[None] for a in reduce_adamw_t(
            recv_dn, jnp.transpose(dn_w_in[0]), jnp.transpose(m_dn_w_in[0]), jnp.transpose(v_dn_w_in[0]),
            "adamw_dn_w_in")),
        "sb_w_in": reduce_adamw(recv_sb, sb_w_in, m_sb_w_in, v_sb_w_in, "adamw_sb_w_in"),
        "w_out": reduce_adamw(recv_wo, w_out, m_w_out, v_w_out, "adamw_w_out"),
        "mem_w_kv": reduce_adamw(recv_kv, mem_w_kv, m_mem_w_kv, v_mem_w_kv, "adamw_mem_w_kv"),
    }

    dconv = jnp.concatenate([dcw_q, dcw_k, dcw_v], axis=1)[:4]
    small_shapes = [(2, D_MODEL), (D_MODEL,), (2, XA_DIM), (2, XA_DIM), (4, DN_QKV), (1, DN_V_HEADS),
                    (1, DN_V_HEADS), (1, HEAD_DIM), (1, HEAD_DIM), (1, HEAD_DIM), (1,)]
    pack = _pack_rows([jnp.stack([d_ng0[0], d_ng1[0]]), d_memg[0], jnp.stack([dxqg0[0], dxqg1[0]]),
                       jnp.stack([dxkg0[0], dxkg1[0]]), dconv, d_alog[0, :DN_V_HEADS], d_dtb[0, :DN_V_HEADS],
                       d_outg[0], d_sbq[0], d_sbk[0], loss_part[0, :1]])
    total = all_reduce_small(pack)
    (g_norm, g_memn, g_xq, g_xk, g_conv_full, g_alog, g_dtb, g_outn, g_sbq, g_sbk, loss1) = _unpack_rows(total, small_shapes)
    conv_cols = DN_QKV // N_DEV
    g_conv = lax.dynamic_slice(g_conv_full, (0, my_dev * conv_cols), (4, conv_cols))[None]
    names = ["norm_g", "mem_norm_g", "xa_q_norm_g", "xa_k_norm_g", "dn_conv_w", "dn_a_log", "dn_dt_bias",
             "dn_out_norm_g", "sb_q_norm_g", "sb_k_norm_g"]
    grads = [g_norm, g_memn, g_xq, g_xk, g_conv, g_alog, g_dtb, g_outn, g_sbq, g_sbk]
    ws = [norm_g, mem_norm_g, xa_q_norm_g, xa_k_norm_g, dn_conv_w, dn_a_log, dn_dt_bias, dn_out_norm_g, sb_q_norm_g,
          sb_k_norm_g]
    ms = [m_norm_g, m_mem_norm_g, m_xa_q_norm_g, m_xa_k_norm_g, m_dn_conv_w, m_dn_a_log, m_dn_dt_bias,
          m_dn_out_norm_g, m_sb_q_norm_g, m_sb_k_norm_g]
    vs = [v_norm_g, v_mem_norm_g, v_xa_q_norm_g, v_xa_k_norm_g, v_dn_conv_w, v_dn_a_log, v_dn_dt_bias,
          v_dn_out_norm_g, v_sb_q_norm_g, v_sb_k_norm_g]
    shapes = [w.shape for w in ws]
    d_p, m_p, v_p = adamw_small(_pack_rows(grads), _pack_rows(ws), _pack_rows(ms), _pack_rows(vs))
    small = dict(zip(names, zip(grads, _unpack_rows(d_p, shapes), _unpack_rows(m_p, shapes), _unpack_rows(v_p, shapes))))

    order = ["norm_g", "mem_norm_g", "mem_w_kv", "xa_q_norm_g", "xa_k_norm_g", "w_out", "dn_w_in", "dn_conv_w",
             "dn_a_log", "dn_dt_bias", "dn_out_norm_g", "sb_w_in", "sb_q_norm_g", "sb_k_norm_g"]
    res = {**big, **small}
    outs = [loss1.reshape(()), grad_x[None]]
    for k in range(4):
        outs += [res[n][k] for n in order]
    return tuple(outs)
```

```python
import functools

import jax
import jax.numpy as jnp
from jax import lax
from jax.experimental import pallas as pl
from jax.experimental.pallas import tpu as pltpu

F32 = jnp.float32
BF16 = jnp.bfloat16

D_MODEL = 2048
SEQ = 2048
N_MEM = 256
INNER = 4096
XA_HEADS = 4
XA_WIDTH = 1024
XA_DIM = 256
MIX_WIDTH = 3072
HEAD_DIM = 128
DN_V_HEADS = 24
DN_QK_HEADS = 12
DN_QK_WIDTH = 1536
DN_CHUNK = 64
DN_QKV = 2 * DN_QK_WIDTH + MIX_WIDTH
DN_PROJ = 11312
SB_HEADS = 24
SB_PROJ = 14336
EPS = 1e-6
N_DEV = 8
DN_SHARD = DN_PROJ // N_DEV
DN_SHARD_PAD = 1536
LANE = 128
DN_COLS = 90 * LANE
DN_AB_BLK, DN_PAD_BLK, DN_XQ_BLK, DN_Z_BLK = 48, 49, 50, 58
SB_XQ_BLK, SB_Z_BLK = 72, 80

ADAM_LR, ADAM_B1, ADAM_B2, ADAM_EPS, ADAM_WD, ADAM_STEP = 0.001, 0.9, 0.999, 1e-08, 0.01, 10

VMEM_LIMIT = 56 * 1024 * 1024
MESH = pl.DeviceIdType.MESH

_NN, _NT, _TN = "nn", "nt", "tn"


def _dims(mode, rank):
    lhs, rhs = {"nn": (1, 0), "nt": (1, 1), "tn": (0, 0)}[mode]
    if rank == 2:
        return (((lhs,), (rhs,)), ((), ()))
    return (((lhs + 1,), (rhs + 1,)), ((0,), (0,)))


def _params(*sem):
    return pltpu.CompilerParams(dimension_semantics=sem if sem else None, vmem_limit_bytes=VMEM_LIMIT)


def _dot(a, b, mode):
    return lax.dot_general(a, b, _dims(mode, a.ndim), preferred_element_type=F32)


def _dg(a, b, dims):
    return _dot(a.astype(BF16), b.astype(BF16), dims)


@jax.custom_vjp
def mm(a, b):
    return _dg(a, b, _NN)


@jax.custom_vjp
def mm_nt(a, b):
    return _dg(a, b, _NT)


@jax.custom_vjp
def mm_tn(a, b):
    return _dg(a, b, _TN)


mm.defvjp(lambda a, b: (_dg(a, b, _NN), (a, b)), lambda r, g: (mm_nt(g, r[1]), mm_tn(r[0], g)))
mm_nt.defvjp(lambda a, b: (_dg(a, b, _NT), (a, b)), lambda r, g: (mm(g, r[1]), mm_tn(g, r[0])))
mm_tn.defvjp(lambda a, b: (_dg(a, b, _TN), (a, b)), lambda r, g: (mm_nt(r[1], g), mm(r[0], g)))


def _split3(x):
    hi = x.astype(BF16)
    r1 = x - hi.astype(F32)
    mid = r1.astype(BF16)
    lo = (r1 - mid.astype(F32)).astype(BF16)
    return hi, mid, lo


def _dg3(a, b, dims):
    ah, am, _ = _split3(a)
    bh, bm, _ = _split3(b)
    return _dot(ah, bh, dims) + (_dot(ah, bm, dims) + _dot(am, bh, dims))


def _dg_exact_rhs(a, b01, dims):
    ah, am, _ = _split3(a)
    b = b01.astype(BF16)
    return _dot(ah, b, dims) + _dot(am, b, dims)


def _dg_exact_lhs(a01, b, dims):
    bh, bm, bl = _split3(b)
    a = a01.astype(BF16)
    return _dot(a, bh, dims) + (_dot(a, bm, dims) + _dot(a, bl, dims))


@jax.custom_vjp
def mm3(a, b):
    return _dg3(a, b, _NN)


@jax.custom_vjp
def mm3_nt(a, b):
    return _dg3(a, b, _NT)


@jax.custom_vjp
def mm3_tn(a, b):
    return _dg3(a, b, _TN)


mm3.defvjp(lambda a, b: (_dg3(a, b, _NN), (a, b)), lambda r, g: (mm3_nt(g, r[1]), mm3_tn(r[0], g)))
mm3_nt.defvjp(lambda a, b: (_dg3(a, b, _NT), (a, b)), lambda r, g: (mm3(g, r[1]), mm3_tn(g, r[0])))
mm3_tn.defvjp(lambda a, b: (_dg3(a, b, _TN), (a, b)), lambda r, g: (mm3_nt(r[1], g), mm3(r[0], g)))


def _softplus(x):
    return jnp.maximum(x, 0.0) + jnp.log(1.0 + jnp.exp(-jnp.abs(x)))


def _log_sigmoid(x):
    return jnp.minimum(x, 0.0) - jnp.log(1.0 + jnp.exp(-jnp.abs(x)))


def _sigmoid(x):
    return 1.0 / (1.0 + jnp.exp(-x))


def _silu(x):
    return x * _sigmoid(x)


def _silu_grad(x):
    s = _sigmoid(x)
    return s * (1.0 + x * (1.0 - s))


@jax.custom_vjp
def mm01(a01, b):
    return _dg_exact_lhs(a01, b, _NN)


mm01.defvjp(lambda a, b: (_dg_exact_lhs(a, b, _NN), a),
            lambda a, g: (jnp.zeros_like(a), _dg_exact_lhs(a, g, _TN)))


def _lane_pick(x, idx):
    lane = lax.broadcasted_iota(jnp.int32, x.shape, x.ndim - 1)
    return jnp.sum(jnp.where(lane == idx, x, 0.0), axis=-1, keepdims=True)


def _dn_chunk(qt, kt, v, ab, alog, dtb, h):
    B, C = qt.shape[0], DN_CHUNK
    g = -jnp.exp(_lane_pick(alog, h)) * _softplus(_lane_pick(ab, h) + _lane_pick(dtb, h))
    beta = _sigmoid(_lane_pick(ab, h + DN_V_HEADS))
    q = qt * lax.rsqrt(jnp.sum(qt * qt, axis=-1, keepdims=True) + EPS) * (HEAD_DIM ** -0.5)
    k = kt * lax.rsqrt(jnp.sum(kt * kt, axis=-1, keepdims=True) + EPS)
    row = lax.broadcasted_iota(jnp.int32, (B, C, C), 1)
    col = lax.broadcasted_iota(jnp.int32, (B, C, C), 2)
    lower = (row >= col).astype(F32)
    ones = jnp.ones((B, C, C), F32)
    g_wide = jnp.broadcast_to(g, (B, C, LANE))
    g_sq = jnp.broadcast_to(g, (B, C, C))
    gc = mm01(lower, g_wide)
    gc_i = gc[:, :, :C]
    gc_j = mm01(ones, jnp.where(row <= col, g_sq, 0.0))
    g_last = jnp.broadcast_to(gc[:, C - 1:C, :], (B, C, LANE))
    decay = jnp.exp(jnp.where(row >= col, gc_i - gc_j, -1e30))
    eg = jnp.exp(gc)
    kk = mm_nt(k, k)
    a_mat = jnp.where(row > col, jnp.broadcast_to(beta, (B, C, C)) * kk * decay, 0.0)
    eye = (row == col).astype(F32)
    y = -a_mat
    t = eye + y
    for _ in range(5):
        y = mm(y, y)
        t = t + mm(t, y)
    bb = jnp.broadcast_to(beta, (B, C, LANE))
    u0 = mm(t, v * bb)
    w = mm(t, k * (bb * eg))
    qk = mm_nt(q, k) * decay
    q_dec = q * eg
    k_dec = k * jnp.exp(g_last - gc)
    cd = jnp.exp(g_last)[:, :8, :]
    return u0, w, qk, q_dec, k_dec, cd


def _shift_rows(x, j, down):
    if j == 0:
        return x
    n = x.shape[0]
    r = lax.broadcasted_iota(jnp.int32, x.shape, 0)
    if down:
        return jnp.where(r >= j, pltpu.roll(x, j, 0), 0.0)
    return jnp.where(r < n - j, pltpu.roll(x, n - j, 0), 0.0)


def dn_conv_fwd(proj, conv_w, exchange=None):
    s = proj.shape[0]

    def body(x_ref, w_ref, o_ref):
        x = x_ref[...]
        w = w_ref[...]
        pre = x * w[3:4, :]
        for j in (1, 2, 3):
            pre = pre + _shift_rows(x, j, True) * w[3 - j:4 - j, :]
        o_ref[...] = _silu(pre)

    return hosted_call(
        body, exchange, name="dn_conv_fwd", grid=(DN_QKV // LANE,),
        in_specs=[pl.BlockSpec((s, LANE), lambda j: (0, j)), pl.BlockSpec((8, LANE), lambda j: (0, j))],
        out_specs=[pl.BlockSpec((s, LANE), lambda j: (0, j))],
        out_shape=[jax.ShapeDtypeStruct((s, DN_QKV), F32)], sem=("parallel",),
    )(proj, conv_w)


def dn_conv_bwd(dact, proj, conv_w, dproj_in, dab, blk0, name):
    s = proj.shape[0]
    nblk = dact.shape[1] // LANE
    extra = 2 if dab is not None else 0

    def body(*refs):
        if dab is not None:
            da_ref, x_ref, w_ref, _, dab_ref, dp_ref, dw_ref = refs
        else:
            da_ref, x_ref, w_ref, _, dp_ref, dw_ref = refs
        j = pl.program_id(0)

        @pl.when(j < nblk)
        def _():
            x = x_ref[...]
            w = w_ref[...]
            xs = [_shift_rows(x, 3 - kk_, True) for kk_ in range(4)]
            pre = xs[0] * w[0:1, :]
            for kk_ in (1, 2, 3):
                pre = pre + xs[kk_] * w[kk_:kk_ + 1, :]
            dpre = da_ref[...] * _silu_grad(pre)
            dx = dpre * w[3:4, :]
            for jj in (1, 2, 3):
                dx = dx + _shift_rows(dpre, jj, False) * w[3 - jj:4 - jj, :]
            dp_ref[...] = dx
            rows = [jnp.sum(dpre * xs[kk_], axis=0, keepdims=True) for kk_ in range(4)]
            dw_ref[...] = jnp.concatenate(rows + [jnp.zeros((4, LANE), F32)], axis=0)

        if dab is not None:
            @pl.when(j == nblk)
            def _():
                dp_ref[...] = dab_ref[...]

            @pl.when(j == nblk + 1)
            def _():
                dp_ref[...] = jnp.zeros_like(dp_ref)

    cl = lambda j: jnp.minimum(j, nblk - 1)
    in_specs = [pl.BlockSpec((s, LANE), lambda j: (0, cl(j))),
                pl.BlockSpec((s, LANE), lambda j: (0, blk0 + cl(j))),
                pl.BlockSpec((8, LANE), lambda j: (0, blk0 + cl(j))),
                pl.BlockSpec(memory_space=pl.ANY)]
    args = [dact, proj, conv_w, dproj_in]
    if dab is not None:
        in_specs.append(pl.BlockSpec((s, LANE), lambda j: (0, 0)))
        args.append(dab)
    return pl.pallas_call(
        body, name=name, grid=(nblk + extra,), in_specs=in_specs,
        out_specs=[pl.BlockSpec((s, LANE), lambda j: (0, blk0 + j)), pl.BlockSpec((8, LANE), lambda j: (0, cl(j)))],
        out_shape=[jax.ShapeDtypeStruct(dproj_in.shape, F32), jax.ShapeDtypeStruct((8, dact.shape[1]), F32)],
        input_output_aliases={3: 0}, compiler_params=_params("arbitrary"),
    )(*args)


DN_GROUP = 8
DN_ROWS = DN_GROUP * DN_CHUNK


def dn_prep_fwd(act, proj, alog, dtb, exchange=None):
    s = act.shape[0]
    nc = s // DN_CHUNK
    C = DN_CHUNK

    def body(q_ref, k_ref, v_ref, ab_ref, al_ref, dt_ref, u0_ref, w_ref, qd_ref, kd_ref, qk_ref, cd_ref):
        qh = pl.program_id(0)
        al = al_ref[0:1, :]
        dt = dt_ref[0:1, :]
        chunks = lambda x: x.reshape(DN_GROUP, C, x.shape[-1])
        rows = lambda x: x.reshape(DN_ROWS, x.shape[-1])
        qt, kt, ab = chunks(q_ref[...]), chunks(k_ref[...]), chunks(ab_ref[...])
        for hv in range(2):
            cs = slice(hv * LANE, (hv + 1) * LANE)
            u0, w, qk, qd, kd, cd = _dn_chunk(qt, kt, chunks(v_ref[:, cs]), ab, al, dt, 2 * qh + hv)
            u0_ref[:, cs] = rows(u0)
            w_ref[:, cs] = rows(w)
            qd_ref[:, cs] = rows(qd)
            kd_ref[:, cs] = rows(kd)
            qk_ref[hv] = rows(qk)
            cd_ref[hv] = cd

    big = pl.BlockSpec((DN_ROWS, 2 * LANE), lambda h, g: (g, h))
    wide = jax.ShapeDtypeStruct((s, MIX_WIDTH), F32)
    return hosted_call(
        body, exchange, name="dn_prep_fwd", grid=(DN_QK_HEADS, s // DN_ROWS),
        in_specs=[pl.BlockSpec((DN_ROWS, LANE), lambda h, g: (g, h)),
                  pl.BlockSpec((DN_ROWS, LANE), lambda h, g: (g, DN_QK_HEADS + h)),
                  pl.BlockSpec((DN_ROWS, 2 * LANE), lambda h, g: (g, DN_QK_HEADS + h)),
                  pl.BlockSpec((DN_ROWS, LANE), lambda h, g: (g, DN_AB_BLK)),
                  pl.BlockSpec((8, LANE), lambda h, g: (0, 0)), pl.BlockSpec((8, LANE), lambda h, g: (0, 0))],
        out_specs=[big, big, big, big,
                   pl.BlockSpec((2, DN_ROWS, C), lambda h, g: (h, g, 0)),
                   pl.BlockSpec((2, DN_GROUP, 8, LANE), lambda h, g: (h, g, 0, 0))],
        out_shape=[wide, wide, wide, wide, jax.ShapeDtypeStruct((DN_V_HEADS, s, C), F32),
                   jax.ShapeDtypeStruct((DN_V_HEADS, nc, 8, LANE), F32)],
        sem=("parallel", "parallel"),
    )(act, act, act, proj, alog, dtb)


def dn_prep_bwd(act, proj, alog, dtb, du0, dw, dqd, dkd, dqk, dcd, exchange=None):
    s = act.shape[0]
    C = DN_CHUNK

    def body(q_ref, k_ref, v_ref, ab_ref, al_ref, dt_ref, du0_ref, dw_ref, dqd_ref, dkd_ref, dqk_ref, dcd_ref,
             dq_ref, dk_ref, dv_ref, dab_ref, dal_ref, ddt_ref):
        g_id = pl.program_id(0)
        qh = pl.program_id(1)
        al = al_ref[0:1, :]
        dt = dt_ref[0:1, :]

        @pl.when(qh == 0)
        def _():
            dab_ref[...] = jnp.zeros_like(dab_ref)

        @pl.when((qh == 0) & (g_id == 0))
        def _():
            dal_ref[...] = jnp.zeros_like(dal_ref)
            ddt_ref[...] = jnp.zeros_like(ddt_ref)

        chunks = lambda x: x.reshape(DN_GROUP, C, x.shape[-1])
        rows = lambda x: x.reshape(DN_ROWS, x.shape[-1])
        qt, kt, ab = chunks(q_ref[...]), chunks(k_ref[...]), chunks(ab_ref[...])
        dq_acc = jnp.zeros((DN_ROWS, LANE), F32)
        dk_acc = jnp.zeros((DN_ROWS, LANE), F32)
        for hv in range(2):
            cs = slice(hv * LANE, (hv + 1) * LANE)
            h = 2 * qh + hv
            f = lambda qt_, kt_, v_, ab_, a_, d_: _dn_chunk(qt_, kt_, v_, ab_, a_, d_, h)
            _, vjp = jax.vjp(f, qt, kt, chunks(v_ref[:, cs]), ab, al, dt)
            dq, dk, dv, dab, dal, ddt = vjp((chunks(du0_ref[:, cs]), chunks(dw_ref[:, cs]), chunks(dqk_ref[hv]),
                                             chunks(dqd_ref[:, cs]), chunks(dkd_ref[:, cs]), dcd_ref[hv]))
            dq_acc = dq_acc + rows(dq)
            dk_acc = dk_acc + rows(dk)
            dv_ref[:, cs] = rows(dv)
            dab_ref[...] += rows(dab)
            dal_ref[0:1, :] += dal
            ddt_ref[0:1, :] += ddt
        dq_ref[...] = dq_acc
        dk_ref[...] = dk_acc

    big = pl.BlockSpec((DN_ROWS, 2 * LANE), lambda g, h: (g, h))
    one = pl.BlockSpec((DN_ROWS, LANE), lambda g, h: (g, h))
    small = pl.BlockSpec((8, LANE), lambda g, h: (0, 0))
    return hosted_call(
        body, exchange, name="dn_prep_bwd", grid=(s // DN_ROWS, DN_QK_HEADS),
        in_specs=[one, pl.BlockSpec((DN_ROWS, LANE), lambda g, h: (g, DN_QK_HEADS + h)),
                  pl.BlockSpec((DN_ROWS, 2 * LANE), lambda g, h: (g, DN_QK_HEADS + h)),
                  pl.BlockSpec((DN_ROWS, LANE), lambda g, h: (g, DN_AB_BLK)), small, small,
                  big, big, big, big,
                  pl.BlockSpec((2, DN_ROWS, C), lambda g, h: (h, g, 0)),
                  pl.BlockSpec((2, DN_GROUP, 8, LANE), lambda g, h: (h, g, 0, 0))],
        out_specs=[one, one, big, pl.BlockSpec((DN_ROWS, LANE), lambda g, h: (g, 0)), small, small],
        out_shape=[jax.ShapeDtypeStruct((s, DN_QK_WIDTH), F32), jax.ShapeDtypeStruct((s, DN_QK_WIDTH), F32),
                   jax.ShapeDtypeStruct((s, MIX_WIDTH), F32), jax.ShapeDtypeStruct((s, LANE), F32),
                   jax.ShapeDtypeStruct((8, LANE), F32), jax.ShapeDtypeStruct((8, LANE), F32)],
        sem=("arbitrary", "arbitrary"),
    )(act, act, act, proj, alog, dtb, du0, dw, dqd, dkd, dqk, dcd)


DN_SCAN_HEADS = 2


def dn_scan_fwd(u0, w, qd, kd, qk, cd, width, exchange=None):
    s = u0.shape[0]
    nc = s // DN_CHUNK
    C = DN_CHUNK

    nh = DN_SCAN_HEADS
    heads = range(nh)
    cols = [slice(h * LANE, (h + 1) * LANE) for h in heads]

    def body(u0_ref, w_ref, qd_ref, kd_ref, qk_ref, cd_ref, o_ref, st_ref):
        def step(c, states):
            rs = pl.ds(pl.multiple_of(c * C, C), C)
            for h in heads:
                st_ref[h, c] = states[h]
            ws = [_dg(w_ref[rs, cols[h]], states[h], _NN) for h in heads]
            us = [u0_ref[rs, cols[h]] - ws[h] for h in heads]
            os_ = [_dg(qd_ref[rs, cols[h]], states[h], _NN) for h in heads]
            for h in heads:
                o_ref[rs, cols[h]] = os_[h] + _dg(qk_ref[h, rs, :], us[h], _NN)
            return tuple(cd_ref[h, c][0:1, :] * states[h] + _dg(kd_ref[rs, cols[h]], us[h], _TN) for h in heads)

        lax.fori_loop(0, nc, step, tuple(jnp.zeros((HEAD_DIM, HEAD_DIM), F32) for _ in heads))

    col = pl.BlockSpec((s, nh * LANE), lambda h: (0, h))
    return hosted_call(
        body, exchange, name="dn_scan_fwd", grid=(DN_V_HEADS // nh,),
        in_specs=[col, col, col, col, pl.BlockSpec((nh, s, C), lambda h: (h, 0, 0)),
                  pl.BlockSpec((nh, nc, 8, LANE), lambda h: (h, 0, 0, 0))],
        out_specs=[col, pl.BlockSpec((nh, nc, HEAD_DIM, HEAD_DIM), lambda h: (h, 0, 0, 0))],
        out_shape=[jax.ShapeDtypeStruct((s, width), F32),
                   jax.ShapeDtypeStruct((DN_V_HEADS, nc, HEAD_DIM, HEAD_DIM), F32)],
        sem=("parallel",),
    )(u0, w, qd, kd, qk, cd)


def dn_scan_bwd(do, u0, w, qd, kd, qk, cd, states, exchange=None):
    s = u0.shape[0]
    nc = s // DN_CHUNK
    C = DN_CHUNK

    nh = DN_SCAN_HEADS
    heads = range(nh)
    cols = [slice(h * LANE, (h + 1) * LANE) for h in heads]

    def body(do_ref, u0_ref, w_ref, qd_ref, kd_ref, qk_ref, cd_ref, st_ref,
             du0_ref, dw_ref, dqd_ref, dkd_ref, dqk_ref, dcd_ref):
        def step(i, dstates):
            c = nc - 1 - i
            rs = pl.ds(pl.multiple_of(c * C, C), C)
            states = [st_ref[h, c] for h in heads]
            gs = [do_ref[rs, cols[h]] for h in heads]
            w_cs = [w_ref[rs, cols[h]] for h in heads]
            qd_cs = [qd_ref[rs, cols[h]] for h in heads]
            cd_rows = [cd_ref[h, c][0:1, :] for h in heads]
            us = [u0_ref[rs, cols[h]] - _dg(w_cs[h], states[h], _NN) for h in heads]
            dus = [_dg(qk_ref[h, rs, :], gs[h], _TN) + _dg(kd_ref[rs, cols[h]], dstates[h], _NN) for h in heads]
            for h in heads:
                du0_ref[rs, cols[h]] = dus[h]
                dw_ref[rs, cols[h]] = -_dg(dus[h], states[h], _NT)
                dqd_ref[rs, cols[h]] = _dg(gs[h], states[h], _NT)
                dkd_ref[rs, cols[h]] = _dg(us[h], dstates[h], _NT)
                dqk_ref[h, rs, :] = _dg(gs[h], us[h], _NT)
                dcd_row = jnp.sum(states[h] * dstates[h], axis=0, keepdims=True)
                dcd_ref[h, c] = jnp.concatenate([dcd_row, jnp.zeros((7, LANE), F32)], axis=0)
            return tuple(cd_rows[h] * dstates[h] + _dg(qd_cs[h], gs[h], _TN) - _dg(w_cs[h], dus[h], _TN)
                         for h in heads)

        lax.fori_loop(0, nc, step, tuple(jnp.zeros((HEAD_DIM, HEAD_DIM), F32) for _ in heads))

    col = pl.BlockSpec((s, nh * LANE), lambda h: (0, h))
    qk_spec = pl.BlockSpec((nh, s, C), lambda h: (h, 0, 0))
    cd_spec = pl.BlockSpec((nh, nc, 8, LANE), lambda h: (h, 0, 0, 0))
    wide = jax.ShapeDtypeStruct((s, MIX_WIDTH), F32)
    return hosted_call(
        body, exchange, name="dn_scan_bwd", grid=(DN_V_HEADS // nh,),
        in_specs=[col, col, col, col, col, qk_spec, cd_spec,
                  pl.BlockSpec((nh, nc, HEAD_DIM, HEAD_DIM), lambda h: (h, 0, 0, 0))],
        out_specs=[col, col, col, col, qk_spec, cd_spec],
        out_shape=[wide, wide, wide, wide, jax.ShapeDtypeStruct((DN_V_HEADS, s, C), F32),
                   jax.ShapeDtypeStruct((DN_V_HEADS, nc, 8, LANE), F32)],
        sem=("parallel",),
    )(do, u0, w, qd, kd, qk, cd, states)


SB_T = 256
SB_GROUPS = (4, 2, 1)


def _sb_scores(q, kbs, diff, lims):
    zs = [_dg(q, kb, _NT) * (HEAD_DIM ** -0.5) for kb in kbs]
    masks = [diff < lim for lim in lims]
    lss = [_log_sigmoid(z) for z in zs]
    lrs = [jnp.where(m, ls - z, 0.0) for m, ls, z in zip(masks, lss, zs)]
    return masks, lss, lrs


def _sb_diff():
    return lax.broadcasted_iota(jnp.int32, (SB_T, SB_T), 1) - lax.broadcasted_iota(jnp.int32, (SB_T, SB_T), 0)


def _sb_loop(n_tiles, step, carry):
    done = 0
    for size in SB_GROUPS:
        groups = (n_tiles - done) // size
        carry = lax.fori_loop(0, groups, lambda p, c, s=size, d=done: step(d + p * s, s, c), carry)
        done = done + groups * size
    return carry


def sb_fwd(qkn, proj, v_blk0, width, exchange=None):
    s = qkn.shape[0]

    def body(q_ref, k_ref, v_ref, o_ref, lt_ref):
        i = pl.program_id(1)
        q = q_ref[...]
        diff = _sb_diff()
        after = (diff < 0).astype(BF16)

        def step(first, n, carry):
            run, acc = carry
            tiles = [first + t for t in range(n)]
            kss = [pl.ds(pl.multiple_of((i - t) * SB_T, SB_T), SB_T) for t in tiles]
            masks, lss, lrs = _sb_scores(q, [k_ref[ks, :] for ks in kss], diff, [t * SB_T for t in tiles])
            within = [_dg_exact_rhs(lr, after, _NN) for lr in lrs]
            sums = [jnp.sum(lr, axis=1, keepdims=True) for lr in lrs]
            for t in range(n):
                wts = jnp.where(masks[t], jnp.exp(lss[t] + (within[t] + run)), 0.0)
                acc = acc + _dg(wts, v_ref[kss[t], :], _NN)
                run = run + sums[t]
            return run, acc

        run, acc = _sb_loop(i + 1, step, (jnp.zeros((SB_T, 1), F32), jnp.zeros((SB_T, HEAD_DIM), F32)))
        o_ref[...] = acc
        lt_ref[0] = run

    return hosted_call(
        body, exchange, name="sb_fwd", grid=(SB_HEADS, s // SB_T),
        in_specs=[pl.BlockSpec((SB_T, LANE), lambda h, i: (i, h)),
                  pl.BlockSpec((s, LANE), lambda h, i: (0, SB_HEADS + h)),
                  pl.BlockSpec((s, LANE), lambda h, i: (0, v_blk0 + h))],
        out_specs=[pl.BlockSpec((SB_T, LANE), lambda h, i: (i, h)), pl.BlockSpec((1, SB_T, 1), lambda h, i: (h, i, 0))],
        out_shape=[jax.ShapeDtypeStruct((s, width), F32), jax.ShapeDtypeStruct((SB_HEADS, s, 1), F32)],
        sem=("parallel", "parallel"),
    )(qkn, qkn, proj)


def sb_bwd(qkn, proj, v_blk0, do, ltot, dproj_in, v=None):
    s = qkn.shape[0]
    v_src, v_src_blk0 = (proj, v_blk0) if v is None else (v, 0)

    def body(q_ref, k_ref, v_ref, do_ref, lt_ref, _, dq_ref, dk_ref, dv_ref):
        i = pl.program_id(1)

        @pl.when(i == 0)
        def _():
            dk_ref[...] = jnp.zeros_like(dk_ref)
            dv_ref[...] = jnp.zeros_like(dv_ref)

        q = q_ref[...]
        g = do_ref[...].astype(BF16)
        ltot = lt_ref[0]
        diff = _sb_diff()
        upto = (diff >= 0).astype(BF16)
        before = (diff > 0).astype(BF16)

        def step(first, n, carry):
            plr, pdl, dq = carry
            tiles = [first + t for t in range(n)]
            kss = [pl.ds(pl.multiple_of(j * SB_T, SB_T), SB_T) for j in tiles]
            kbs = [k_ref[ks, :] for ks in kss]
            vbs = [v_ref[ks, :] for ks in kss]
            masks, lss, lrs = _sb_scores(q, kbs, diff, [(i - j) * SB_T for j in tiles])
            dwts = [_dg(g, vb, _NT) for vb in vbs]
            within = [_dg_exact_rhs(lr, upto, _NN) for lr in lrs]
            wtss, dls = [], []
            for t in range(n):
                wts = jnp.where(masks[t], jnp.exp(lss[t] + (ltot - (within[t] + plr))), 0.0)
                plr = plr + jnp.sum(lrs[t], axis=1, keepdims=True)
                wtss.append(wts)
                dls.append(dwts[t] * wts)
            dwithin = [_dg_exact_rhs(dl, before, _NN) for dl in dls]
            for t in range(n):
                sz = jnp.exp(lss[t])
                dz = jnp.where(masks[t], dls[t] * (1.0 - sz) - sz * (dwithin[t] + pdl), 0.0) * (HEAD_DIM ** -0.5)
                pdl = pdl + jnp.sum(dls[t], axis=1, keepdims=True)
                dk_ref[kss[t], :] += _dg(dz, q, _TN)
                dv_ref[kss[t], :] += _dg(wtss[t], g, _TN)
                dq = dq + _dg(dz, kbs[t], _NN)
            return plr, pdl, dq

        zero = jnp.zeros((SB_T, 1), F32)
        _, _, dq = _sb_loop(i + 1, step, (zero, zero, jnp.zeros((SB_T, HEAD_DIM), F32)))
        dq_ref[...] = dq

    tile = pl.BlockSpec((SB_T, LANE), lambda h, i: (i, h))
    colspec = pl.BlockSpec((s, LANE), lambda h, i: (0, h))
    return pl.pallas_call(
        body, name="sb_bwd", grid=(SB_HEADS, s // SB_T),
        in_specs=[tile, pl.BlockSpec((s, LANE), lambda h, i: (0, SB_HEADS + h)),
                  pl.BlockSpec((s, LANE), lambda h, i: (0, v_src_blk0 + h)), tile,
                  pl.BlockSpec((1, SB_T, 1), lambda h, i: (h, i, 0)), pl.BlockSpec(memory_space=pl.ANY)],
        out_specs=[tile, colspec, pl.BlockSpec((s, LANE), lambda h, i: (0, v_blk0 + h))],
        out_shape=[jax.ShapeDtypeStruct((s, MIX_WIDTH), F32), jax.ShapeDtypeStruct((s, MIX_WIDTH), F32),
                   jax.ShapeDtypeStruct(dproj_in.shape, F32)],
        input_output_aliases={5: 2}, compiler_params=_params("parallel", "arbitrary"),
    )(qkn, qkn, v_src, do, ltot, dproj_in)


ROW_TILE = 256
HN_HEADS = 8


def head_norm_fwd(x, x_blk0, nblk, gains, out_dtype, out_width, name):
    s = x.shape[0]
    assert x_blk0 % HN_HEADS == 0 and nblk % (HN_HEADS * gains.shape[0]) == 0
    per = nblk // gains.shape[0] // HN_HEADS
    w = HN_HEADS * LANE

    def body(x_ref, g_ref, o_ref):
        gain = g_ref[0, 0:1, :]
        for j in range(HN_HEADS):
            cs = slice(j * LANE, (j + 1) * LANE)
            xv = x_ref[:, cs]
            r = lax.rsqrt(jnp.mean(xv * xv, axis=1, keepdims=True) + EPS)
            o_ref[:, cs] = (xv * r * gain).astype(out_dtype)

    return pl.pallas_call(
        body, name=name, grid=(nblk // HN_HEADS, s // ROW_TILE),
        in_specs=[pl.BlockSpec((ROW_TILE, w), lambda j, t: (t, x_blk0 // HN_HEADS + j)),
                  pl.BlockSpec((1, 8, LANE), lambda j, t: (j // per, 0, 0))],
        out_specs=pl.BlockSpec((ROW_TILE, w), lambda j, t: (t, j)),
        out_shape=jax.ShapeDtypeStruct((s, out_width), out_dtype), compiler_params=_params("parallel", "parallel"),
    )(x, gains)


def head_norm_bwd(dy, dy_blk0, x, x_blk0, nblk, gain, dst, dst_blk0, name):
    s = x.shape[0]

    def body(*refs):
        if dst is not None:
            dy_ref, x_ref, g_ref, _, dx_ref, dg_ref = refs
        else:
            dy_ref, x_ref, g_ref, dx_ref, dg_ref = refs

        @pl.when((pl.program_id(0) == 0) & (pl.program_id(1) == 0))
        def _():
            dg_ref[...] = jnp.zeros_like(dg_ref)

        gain = g_ref[0:1, :]
        dg = jnp.zeros((1, LANE), F32)
        for j in range(HN_HEADS):
            cs = slice(j * LANE, (j + 1) * LANE)
            xv = x_ref[:, cs]
            g = dy_ref[:, cs]
            r = lax.rsqrt(jnp.mean(xv * xv, axis=1, keepdims=True) + EPS)
            gy = g * gain
            dx_ref[:, cs] = r * gy - xv * (r * r * r) * jnp.mean(gy * xv, axis=1, keepdims=True)
            dg = dg + jnp.sum(g * xv * r, axis=0, keepdims=True)
        dg_ref[0:1, :] += dg

    assert dy_blk0 % HN_HEADS == 0 and x_blk0 % HN_HEADS == 0 and dst_blk0 % HN_HEADS == 0 and nblk % HN_HEADS == 0
    w = HN_HEADS * LANE
    in_specs = [pl.BlockSpec((ROW_TILE, w), lambda j, t: (t, dy_blk0 // HN_HEADS + j)),
                pl.BlockSpec((ROW_TILE, w), lambda j, t: (t, x_blk0 // HN_HEADS + j)),
                pl.BlockSpec((8, LANE), lambda j, t: (0, 0))]
    args = [dy, x, gain]
    aliases = {}
    if dst is not None:
        in_specs.append(pl.BlockSpec(memory_space=pl.ANY))
        args.append(dst)
        aliases = {3: 0}
        out0 = jax.ShapeDtypeStruct(dst.shape, F32)
    else:
        out0 = jax.ShapeDtypeStruct((s, (dst_blk0 + nblk) * LANE), F32)
    return pl.pallas_call(
        body, name=name, grid=(nblk // HN_HEADS, s // ROW_TILE), in_specs=in_specs,
        out_specs=[pl.BlockSpec((ROW_TILE, w), lambda j, t: (t, dst_blk0 // HN_HEADS + j)),
                   pl.BlockSpec((8, LANE), lambda j, t: (0, 0))],
        out_shape=[out0, jax.ShapeDtypeStruct((8, LANE), F32)],
        input_output_aliases=aliases, compiler_params=_params("arbitrary", "arbitrary"),
    )(*args)


def _xa_head(xq, kraw, v, qg, kg):
    q = xq * lax.rsqrt(jnp.mean(xq * xq, axis=1, keepdims=True) + EPS) * qg
    k = kraw * lax.rsqrt(jnp.mean(kraw * kraw, axis=1, keepdims=True) + EPS) * kg
    sc = mm_nt(q, k) * (XA_DIM ** -0.5)
    e = jnp.exp(sc - lax.stop_gradient(jnp.max(sc, axis=1, keepdims=True)))
    return mm(e / jnp.sum(e, axis=1, keepdims=True), v)


def xa_fwd(proj, xq_blk0, kv, qg, kg, cat):
    s = proj.shape[0]
    n_mem = kv.shape[0]

    def body(xq_ref, k_ref, v_ref, qg_ref, kg_ref, _, o_ref):
        o_ref[...] = _xa_head(xq_ref[...], k_ref[...], v_ref[...], qg_ref[0:1, :], kg_ref[0:1, :])

    gain = pl.BlockSpec((8, XA_DIM), lambda h, t: (0, 0))
    return pl.pallas_call(
        body, name="xa_fwd", grid=(XA_HEADS, s // ROW_TILE),
        in_specs=[pl.BlockSpec((ROW_TILE, XA_DIM), lambda h, t: (t, xq_blk0 // 2 + h)),
                  pl.BlockSpec((n_mem, XA_DIM), lambda h, t: (0, h)),
                  pl.BlockSpec((n_mem, XA_DIM), lambda h, t: (0, XA_HEADS + h)), gain, gain,
                  pl.BlockSpec(memory_space=pl.ANY)],
        out_specs=pl.BlockSpec((ROW_TILE, XA_DIM), lambda h, t: (t, MIX_WIDTH // XA_DIM + h)),
        out_shape=jax.ShapeDtypeStruct(cat.shape, F32), input_output_aliases={5: 0},
        compiler_params=_params("parallel", "parallel"),
    )(proj, kv, kv, qg, kg, cat)


def xa_bwd(proj, xq_blk0, kv, qg, kg, dcat, dproj_in):
    s = proj.shape[0]
    n_mem = kv.shape[0]

    def body(xq_ref, k_ref, v_ref, qg_ref, kg_ref, do_ref, _, dxq_ref, dk_ref, dv_ref, dqg_ref, dkg_ref):
        h = pl.program_id(0)
        t = pl.program_id(1)

        @pl.when(t == 0)
        def _():
            dk_ref[...] = jnp.zeros_like(dk_ref)
            dv_ref[...] = jnp.zeros_like(dv_ref)

        @pl.when((t == 0) & (h == 0))
        def _():
            dqg_ref[...] = jnp.zeros_like(dqg_ref)
            dkg_ref[...] = jnp.zeros_like(dkg_ref)

        _, vjp = jax.vjp(_xa_head, xq_ref[...], k_ref[...], v_ref[...], qg_ref[0:1, :], kg_ref[0:1, :])
        dxq, dk, dv, dqg, dkg = vjp(do_ref[...])
        dxq_ref[...] = dxq
        dk_ref[...] += dk
        dv_ref[...] += dv
        dqg_ref[0:1, :] += dqg
        dkg_ref[0:1, :] += dkg

    gain = pl.BlockSpec((8, XA_DIM), lambda h, t: (0, 0))
    kspec = pl.BlockSpec((n_mem, XA_DIM), lambda h, t: (0, h))
    vspec = pl.BlockSpec((n_mem, XA_DIM), lambda h, t: (0, XA_HEADS + h))
    return pl.pallas_call(
        body, name="xa_bwd", grid=(XA_HEADS, s // ROW_TILE),
        in_specs=[pl.BlockSpec((ROW_TILE, XA_DIM), lambda h, t: (t, xq_blk0 // 2 + h)), kspec, vspec, gain, gain,
                  pl.BlockSpec((ROW_TILE, XA_DIM), lambda h, t: (t, MIX_WIDTH // XA_DIM + h)),
                  pl.BlockSpec(memory_space=pl.ANY)],
        out_specs=[pl.BlockSpec((ROW_TILE, XA_DIM), lambda h, t: (t, xq_blk0 // 2 + h)), kspec, kspec, gain, gain],
        out_shape=[jax.ShapeDtypeStruct(dproj_in.shape, F32), jax.ShapeDtypeStruct((n_mem, XA_WIDTH), F32),
                   jax.ShapeDtypeStruct((n_mem, XA_WIDTH), F32), jax.ShapeDtypeStruct((8, XA_DIM), F32),
                   jax.ShapeDtypeStruct((8, XA_DIM), F32)],
        input_output_aliases={6: 0}, compiler_params=_params("arbitrary", "arbitrary"),
    )(proj, kv, kv, qg, kg, dcat, dproj_in)


GATE_ROWS = 1024


def gate_fwd(cat, proj, z_blk0):
    s = cat.shape[0]

    def body(c_ref, z_ref, y_ref):
        y_ref[...] = (c_ref[...] * _silu(z_ref[...])).astype(BF16)

    w = 2 * LANE
    rt = min(GATE_ROWS, s)
    return pl.pallas_call(
        body, name="gate_fwd", grid=(INNER // w, s // rt),
        in_specs=[pl.BlockSpec((rt, w), lambda j, t: (t, j)),
                  pl.BlockSpec((rt, w), lambda j, t: (t, z_blk0 // 2 + j))],
        out_specs=pl.BlockSpec((rt, w), lambda j, t: (t, j)),
        out_shape=jax.ShapeDtypeStruct((s, INNER), BF16), compiler_params=_params("parallel", "parallel"),
    )(cat, proj)


def gate_bwd(dy, cat, proj, z_blk0):
    s = cat.shape[0]

    def body(dy_ref, c_ref, z_ref, dc_ref, dz_ref):
        z = z_ref[...]
        g = dy_ref[...]
        dc_ref[...] = g * _silu(z)
        dz_ref[...] = g * c_ref[...] * _silu_grad(z)

    w = 2 * LANE
    rt = min(GATE_ROWS, s)
    tile = pl.BlockSpec((rt, w), lambda j, t: (t, j))
    ztile = pl.BlockSpec((rt, w), lambda j, t: (t, z_blk0 // 2 + j))
    return pl.pallas_call(
        body, name="gate_bwd", grid=(INNER // w, s // rt), in_specs=[tile, tile, ztile],
        out_specs=[tile, ztile],
        out_shape=[jax.ShapeDtypeStruct((s, INNER), F32), jax.ShapeDtypeStruct(proj.shape, F32)],
        compiler_params=_params("parallel", "parallel"),
    )(dy, cat, proj)


NORM_ROWS = 256


def rms_fwd(x, gain, with_transpose=False, exchange=None):
    s, d = x.shape

    def body(x_ref, g_ref, o_ref, *t_ref):
        xv = x_ref[...]
        r = lax.rsqrt(jnp.mean(xv * xv, axis=1, keepdims=True) + EPS)
        y = xv * r * g_ref[0:1, :]
        o_ref[...] = y.astype(BF16)
        if with_transpose:
            t_ref[0][...] = y.T.astype(BF16)

    tile = pl.BlockSpec((NORM_ROWS, d), lambda t: (t, 0))
    res = hosted_call(
        body, exchange, name="rms_fwd", grid=(s // NORM_ROWS,),
        in_specs=[tile, pl.BlockSpec((8, d), lambda t: (0, 0))],
        out_specs=[tile] + ([pl.BlockSpec((d, NORM_ROWS), lambda t: (0, t))] if with_transpose else []),
        out_shape=[jax.ShapeDtypeStruct((s, d), BF16)] + ([jax.ShapeDtypeStruct((d, s), BF16)] if with_transpose else []),
        sem=("parallel",),
    )(x, gain)
    return res if (with_transpose or exchange is not None) else res[0]


def rms_bwd(dh, x, gain, dres):
    s, d = x.shape

    def body(*refs):
        if dres is not None:
            dh_ref, x_ref, g_ref, dr_ref, dx_ref, dg_ref = refs
        else:
            dh_ref, x_ref, g_ref, dx_ref, dg_ref = refs

        @pl.when(pl.program_id(0) == 0)
        def _():
            dg_ref[...] = jnp.zeros_like(dg_ref)

        xv = x_ref[...]
        g = dh_ref[...]
        r = lax.rsqrt(jnp.mean(xv * xv, axis=1, keepdims=True) + EPS)
        gy = g * g_ref[0:1, :]
        dx = r * gy - xv * (r * r * r) * jnp.mean(gy * xv, axis=1, keepdims=True)
        dx_ref[...] = dx + dr_ref[...] if dres is not None else dx
        dg_ref[0:1, :] += jnp.sum(g * xv * r, axis=0, keepdims=True)

    tile = pl.BlockSpec((NORM_ROWS, d), lambda t: (t, 0))
    gspec = pl.BlockSpec((8, d), lambda t: (0, 0))
    args = [dh, x, gain] + ([dres] if dres is not None else [])
    return pl.pallas_call(
        body, name="rms_bwd", grid=(s // NORM_ROWS,),
        in_specs=[tile, tile, gspec] + ([tile] if dres is not None else []),
        out_specs=[tile, gspec],
        out_shape=[jax.ShapeDtypeStruct((s, d), F32), jax.ShapeDtypeStruct((8, d), F32)],
        compiler_params=_params("arbitrary"),
    )(*args)


def loss_fwd_bwd(y, target):
    s, d = y.shape

    def body(y_ref, t_ref, l_ref, dy_ref):
        @pl.when(pl.program_id(0) == 0)
        def _():
            l_ref[...] = jnp.zeros_like(l_ref)

        err = y_ref[...] - t_ref[...]
        dy_ref[...] = err * (1.0 / d)
        part = 0.5 * jnp.sum(jnp.mean(err * err, axis=1, keepdims=True), axis=0, keepdims=True)
        r = lax.broadcasted_iota(jnp.int32, (8, LANE), 0)
        c = lax.broadcasted_iota(jnp.int32, (8, LANE), 1)
        l_ref[...] += jnp.where((r == 0) & (c == 0), part, 0.0)

    tile = pl.BlockSpec((NORM_ROWS, d), lambda t: (t, 0))
    return pl.pallas_call(
        body, name="loss", grid=(s // NORM_ROWS,), in_specs=[tile, tile],
        out_specs=[pl.BlockSpec((8, LANE), lambda t: (0, 0)), tile],
        out_shape=[jax.ShapeDtypeStruct((8, LANE), F32), jax.ShapeDtypeStruct((s, d), F32)],
        compiler_params=_params("arbitrary"),
    )(y, target)


def matmul(a, b, mode, out_dtype, tm, tn, tk, name, add=None, b_blocked=False, out_blocks=None, exchange=None):
    if b_blocked:
        nb, _, width = b.shape
        bshape = (b.shape[1], nb * width)
    else:
        bshape = b.shape
    if mode == "tn":
        (kdim, m), n = a.shape, bshape[1]
    else:
        (m, kdim), n = a.shape, (bshape[1] if mode == "nn" else bshape[0])
    tm, tn, tk = min(tm, m), min(tn, n), min(tk, kdim)
    assert m % tm == 0 and n % tn == 0 and kdim % tk == 0, (name, m, n, kdim)
    nk = kdim // tk
    dims = {"nn": _NN, "nt": _NT, "tn": _TN}[mode]

    def body(*refs):
        if add is not None:
            a_ref, b_ref, add_ref, o_ref, acc_ref = refs
        else:
            a_ref, b_ref, o_ref, acc_ref = refs
        k = pl.program_id(2)

        @pl.when(k == 0)
        def _():
            acc_ref[...] = jnp.zeros_like(acc_ref)

        acc_ref[...] += _dg(a_ref[...], b_ref[...], dims)

        @pl.when(k == nk - 1)
        def _():
            r = acc_ref[...]
            if add is not None:
                r = r + add_ref[...]
            o_ref[...] = r.astype(out_dtype)

    a_spec = pl.BlockSpec((tk, tm), lambda i, j, k: (k, i)) if mode == "tn" else pl.BlockSpec((tm, tk), lambda i, j, k: (i, k))
    if b_blocked and mode == "nn":
        per = width // tn
        assert width % tn == 0
        b_spec = pl.BlockSpec((None, tk, tn), lambda i, j, k: (j // per, k, j % per))
    elif b_blocked and mode == "nt":
        per = width // tk
        assert width % tk == 0
        b_spec = pl.BlockSpec((None, tn, tk), lambda i, j, k: (k // per, j, k % per))
    elif mode == "nt":
        b_spec = pl.BlockSpec((tn, tk), lambda i, j, k: (j, k))
    else:
        assert not b_blocked
        b_spec = pl.BlockSpec((tk, tn), lambda i, j, k: (k, j))
    add_spec = pl.BlockSpec((tm, tn), lambda i, j, k: (i, j))
    if out_blocks is not None:
        operb = (n // out_blocks) // tn
        assert (n // out_blocks) % tn == 0 and add is None
        o_spec = pl.BlockSpec((None, tm, tn), lambda i, j, k: (j // operb, i, j % operb))
        out_shape = jax.ShapeDtypeStruct((out_blocks, m, n // out_blocks), out_dtype)
    else:
        o_spec = add_spec
        out_shape = jax.ShapeDtypeStruct((m, n), out_dtype)
    res = hosted_call(
        body, exchange, name=name, grid=(m // tm, n // tn, nk),
        in_specs=[a_spec, b_spec] + ([add_spec] if add is not None else []), out_specs=[o_spec],
        out_shape=[out_shape], scratch_shapes=[pltpu.VMEM((tm, tn), F32)],
        sem=("parallel", "parallel", "arbitrary"),
    )(*([a, b] + ([add] if add is not None else [])))
    return res[0] if exchange is None else res


_HBM = pl.BlockSpec(memory_space=pltpu.HBM)


def _me():
    return lax.axis_index("x"), lax.axis_index("y"), lax.axis_index("c")


def _flat(p):
    return 4 * p[0] + 2 * p[1] + p[2]


def _flip(p, r):
    return tuple((1 - v) if (r >> (2 - a)) & 1 else v for a, v in enumerate(p))


class Exchange:
    def __init__(self, srcs, out_shapes, sems, start, finish, alias=None):
        self.srcs, self.out_shapes, self.sems = list(srcs), list(out_shapes), list(sems)
        self.start, self.finish, self.alias = start, finish, dict(alias or {})


def hosted_call(body, exchange, *, name, grid, in_specs, out_specs, out_shape, scratch_shapes=(),
                input_output_aliases=None, sem=()):
    in_specs, out_specs, out_shape = list(in_specs), list(out_specs), list(out_shape)
    scratch_shapes = list(scratch_shapes)
    aliases = input_output_aliases or {}
    if exchange is None:
        call = pl.pallas_call(body, name=name, grid=grid, in_specs=in_specs, out_specs=out_specs, out_shape=out_shape,
                              scratch_shapes=scratch_shapes, input_output_aliases=aliases, compiler_params=_params(*sem))
        return lambda *args: list(call(*args))
    ni, no, ns = len(in_specs), len(out_specs), len(scratch_shapes)
    xi, xo = len(exchange.srcs), len(exchange.out_shapes)

    def wrapped(*refs):
        ins, refs = refs[:ni], refs[ni:]
        xin, refs = refs[:xi], refs[xi:]
        outs, refs = refs[:no], refs[no:]
        xout, refs = refs[:xo], refs[xo:]
        scr, xsem = refs[:ns], refs[ns:]
        first = functools.reduce(lambda p, q: p & q, [pl.program_id(d) == 0 for d in range(len(grid))])
        last = functools.reduce(lambda p, q: p & q, [pl.program_id(d) == grid[d] - 1 for d in range(len(grid))])

        @pl.when(first)
        def _():
            exchange.start(xin, xout, xsem)

        body(*ins, *outs, *scr)

        @pl.when(last)
        def _():
            exchange.finish(xin, xout, xsem)

    call = pl.pallas_call(
        wrapped, name=name, grid=grid, in_specs=in_specs + [_HBM] * xi, out_specs=out_specs + [_HBM] * xo,
        out_shape=out_shape + exchange.out_shapes, scratch_shapes=scratch_shapes + exchange.sems,
        input_output_aliases={**aliases, **{ni + i: no + j for i, j in exchange.alias.items()}},
        compiler_params=_params(*(("arbitrary",) * len(grid))))
    return lambda *args: list(call(*args, *exchange.srcs))


def run_exchange(exchange, name):
    xi, xo = len(exchange.srcs), len(exchange.out_shapes)

    def body(*refs):
        exchange.start(refs[:xi], refs[xi:xi + xo], refs[xi + xo:])
        exchange.finish(refs[:xi], refs[xi:xi + xo], refs[xi + xo:])

    return list(pl.pallas_call(body, name=name, in_specs=[_HBM] * xi, out_specs=[_HBM] * xo,
                               out_shape=exchange.out_shapes, scratch_shapes=exchange.sems,
                               input_output_aliases=exchange.alias)(*exchange.srcs))


def gather_exchange(shards, rows=None, into=None):
    n = len(shards)

    def parts(srcs, outs, sems):
        send_sems, recv_sems, local_sems = sems
        me = _me()
        x, y, c = me
        chips = [(1 - x, y), (x, 1 - y), (1 - x, 1 - y)]

        def place(a, block):
            dst = outs[a].at[_flat(block)]
            return dst if rows is None else dst.at[pl.ds(rows[0], shards[a].shape[0])]

        def copy(a, k, block, to, src=None):
            dst = place(a, block)
            return pltpu.make_async_remote_copy(src_ref=dst if src is None else src, dst_ref=dst,
                                                send_sem=send_sems.at[a, k], recv_sem=recv_sems.at[a, k],
                                                device_id=to, device_id_type=MESH)

        mine = [pltpu.make_async_copy(srcs[a], place(a, me), local_sems.at[a]) for a in range(n)]
        own = []
        for a in range(n):
            own.append(copy(a, 0, me, (x, y, 1 - c), src=srcs[a]))
            own += [copy(a, 1 + j, me, (*chip, c), src=srcs[a]) for j, chip in enumerate(chips)]
        return me, chips, copy, mine, own

    def start(srcs, outs, sems):
        _, _, _, mine, own = parts(srcs, outs, sems)
        for cp in mine + own:
            cp.start()

    def finish(srcs, outs, sems):
        me, chips, copy, mine, own = parts(srcs, outs, sems)
        x, y, c = me
        passed = []
        for j, chip in enumerate(chips):
            for a in range(n):
                copy(a, 1 + j, (*chip, c), me).wait_recv()
                fwd = copy(a, 4 + j, (*chip, c), (x, y, 1 - c))
                fwd.start()
                passed.append(fwd)
        for a in range(n):
            copy(a, 0, (x, y, 1 - c), me).wait_recv()
            for j, chip in enumerate(chips):
                copy(a, 4 + j, (*chip, 1 - c), me).wait_recv()
        for cp in own + passed:
            cp.wait_send()
        for cp in mine:
            cp.wait()

    dma = pltpu.SemaphoreType.DMA
    full = lambda s: s.shape if rows is None else (rows[1],) + s.shape[1:]
    return Exchange(list(shards) + list(into or []), [jax.ShapeDtypeStruct((N_DEV,) + full(s), s.dtype) for s in shards],
                    [dma((n, 7)), dma((n, 7)), dma((n,))], start, finish,
                    alias={n + a: a for a in range(n)} if into else None)


def pair_exchange(srcs):
    n = len(srcs)

    def copies(srcs_, outs, sems):
        send_sems, recv_sems = sems
        x, y, c = _me()
        return [pltpu.make_async_remote_copy(src_ref=srcs_[a].at[:, 1 - c], dst_ref=outs[a], send_sem=send_sems.at[a],
                                             recv_sem=recv_sems.at[a], device_id=(x, y, 1 - c), device_id_type=MESH)
                for a in range(n)]

    def start(srcs_, outs, sems):
        for cp in copies(srcs_, outs, sems):
            cp.start()

    def finish(srcs_, outs, sems):
        for cp in copies(srcs_, outs, sems):
            cp.wait()

    dma = pltpu.SemaphoreType.DMA
    return Exchange(srcs, [jax.ShapeDtypeStruct((4,) + s.shape[2:], s.dtype) for s in srcs], [dma((n,)), dma((n,))],
                    start, finish)


def pair_sum(src, half, name):
    _, _, rows, cols = src.shape
    tr = min(rows, 256)

    def body(x_ref, h_ref, o_ref):
        c = lax.axis_index("c")
        o_ref[0] = (x_ref[0, c].astype(F32) + h_ref[0].astype(F32)).astype(BF16)

    return pl.pallas_call(
        body, name=name, grid=(4, rows // tr),
        in_specs=[pl.BlockSpec((1, 2, tr, cols), lambda ch, t: (ch, 0, t, 0)),
                  pl.BlockSpec((1, tr, cols), lambda ch, t: (ch, t, 0))],
        out_specs=pl.BlockSpec((1, tr, cols), lambda ch, t: (ch, t, 0)),
        out_shape=jax.ShapeDtypeStruct(half.shape, BF16), compiler_params=_params("parallel", "parallel"),
    )(src, half)


def merge_exchanges(a, b):
    ns, no, nm = len(a.srcs), len(a.out_shapes), len(a.sems)

    def start(srcs, outs, sems):
        a.start(srcs[:ns], outs[:no], sems[:nm])
        b.start(srcs[ns:], outs[no:], sems[nm:])

    def finish(srcs, outs, sems):
        a.finish(srcs[:ns], outs[:no], sems[:nm])
        b.finish(srcs[ns:], outs[no:], sems[nm:])

    alias = {**a.alias, **{ns + i: no + j for i, j in b.alias.items()}}
    return Exchange(a.srcs + b.srcs, a.out_shapes + b.out_shapes, a.sems + b.sems, start, finish, alias)


def chip_exchange(parts, slots, recv_shapes, rows=None, into=None):
    n = len(parts)
    win = (lambda ref: ref) if rows is None else (lambda ref: ref.at[pl.ds(rows[0], rows[1])])

    def plan(srcs, outs, sems):
        send_sems, recv_sems, local_sems = sems
        x, y, c = _me()
        chip = 2 * x + y
        mine = [pltpu.make_async_copy(win(srcs[a].at[chip]), win(outs[slots[a][0]].at[chip, slots[a][1]]),
                                      local_sems.at[a]) for a in range(n)]
        sends, arrivals = [], []
        for r in (1, 2, 3):
            px = (1 - x) if r & 2 else x
            py = (1 - y) if r & 1 else y
            for a in range(n):
                ri, layer = slots[a]
                sends.append(pltpu.make_async_remote_copy(
                    src_ref=win(srcs[a].at[2 * px + py]), dst_ref=win(outs[ri].at[chip, layer]),
                    send_sem=send_sems.at[a, r - 1],
                    recv_sem=recv_sems.at[a, r - 1], device_id=(px, py, c), device_id_type=MESH))
                land = win(outs[ri].at[2 * px + py, layer])
                arrivals.append(pltpu.make_async_remote_copy(
                    src_ref=land, dst_ref=land, send_sem=send_sems.at[a, r - 1], recv_sem=recv_sems.at[a, r - 1],
                    device_id=(px, py, c), device_id_type=MESH))
        return mine, sends, arrivals

    def start(srcs, outs, sems):
        mine, sends, _ = plan(srcs, outs, sems)
        for cp in mine + sends:
            cp.start()

    def finish(srcs, outs, sems):
        mine, sends, arrivals = plan(srcs, outs, sems)
        for cp in arrivals:
            cp.wait_recv()
        for cp in sends:
            cp.wait_send()
        for cp in mine:
            cp.wait()

    dma = pltpu.SemaphoreType.DMA
    return Exchange(list(parts) + list(into or []), [jax.ShapeDtypeStruct(s, BF16) for s in recv_shapes],
                    [dma((n, 3)), dma((n, 3)), dma((n,))], start, finish,
                    alias={n + k: k for k in range(len(recv_shapes))} if into else None)


SMALL_ROWS = 24


def all_reduce_small(pack):
    def body(p_ref, o_ref, buf, send_sems, recv_sems):
        me = _me()
        buf[_flat(me)] = p_ref[...]
        sent = []
        for r in range(1, N_DEV):
            peer = _flip(me, r)
            cp = pltpu.make_async_remote_copy(src_ref=p_ref, dst_ref=buf.at[_flat(me)], send_sem=send_sems.at[r - 1],
                                              recv_sem=recv_sems.at[r - 1], device_id=peer, device_id_type=MESH)
            cp.start()
            sent.append(cp)
        for r in range(1, N_DEV):
            peer = _flip(me, r)
            land = buf.at[_flat(peer)]
            pltpu.make_async_remote_copy(src_ref=land, dst_ref=land, send_sem=send_sems.at[r - 1],
                                         recv_sem=recv_sems.at[r - 1], device_id=peer, device_id_type=MESH).wait_recv()
        for cp in sent:
            cp.wait_send()
        acc = buf[0]
        for d in range(1, N_DEV):
            acc = acc + buf[d]
        o_ref[...] = acc

    vm = pl.BlockSpec(memory_space=pltpu.VMEM)
    return pl.pallas_call(
        body, name="all_reduce_small", in_specs=[vm], out_specs=vm,
        out_shape=jax.ShapeDtypeStruct(pack.shape, F32),
        scratch_shapes=[pltpu.VMEM((N_DEV,) + pack.shape, F32), pltpu.SemaphoreType.DMA((7,)),
                        pltpu.SemaphoreType.DMA((7,))],
    )(pack)


def _adamw(w, g, m, v):
    m = ADAM_B1 * m + (1.0 - ADAM_B1) * g
    v = ADAM_B2 * v + (1.0 - ADAM_B2) * (g * g)
    m_hat = m / (1.0 - ADAM_B1 ** ADAM_STEP)
    v_hat = v / (1.0 - ADAM_B2 ** ADAM_STEP)
    delta = -ADAM_LR * (m_hat / (jnp.sqrt(v_hat) + ADAM_EPS) + ADAM_WD * w)
    return delta, m, v


ADAM_ROWS = 256


def reduce_adamw(recv, w, m, v, name):
    nl, rows, cols = w.shape
    nslot, cp = recv.shape[0], recv.shape[3]

    def body(r_ref, w_ref, m_ref, v_ref, g_ref, d_ref, mo_ref, vo_ref):
        g = r_ref[0, 0].astype(F32)
        for slot in range(1, nslot):
            g = g + r_ref[slot, 0].astype(F32)
        if cp != cols:
            g = g[:, :cols]
        delta, m_new, v_new = _adamw(w_ref[0], g, m_ref[0], v_ref[0])
        g_ref[0] = g
        d_ref[0] = delta
        mo_ref[0] = m_new
        vo_ref[0] = v_new

    tile = pl.BlockSpec((1, ADAM_ROWS, cols), lambda l, t: (l, t, 0))
    out = jax.ShapeDtypeStruct(w.shape, F32)
    return pl.pallas_call(
        body, name=name, grid=(nl, rows // ADAM_ROWS),
        in_specs=[pl.BlockSpec((nslot, 1, ADAM_ROWS, cp), lambda l, t: (0, l, t, 0)), tile, tile, tile],
        out_specs=[tile, tile, tile, tile], out_shape=[out, out, out, out],
        compiler_params=_params("parallel", "parallel"),
    )(recv, w, m, v)


def reduce_adamw_t(recv, w_t, m_t, v_t, name):
    cols, rows = w_t.shape
    nslot, cp = recv.shape[0], recv.shape[3]
    tr = 256

    def body(r_ref, w_ref, m_ref, v_ref, g_ref, d_ref, mo_ref, vo_ref):
        g = r_ref[0, 0].astype(F32)
        for slot in range(1, nslot):
            g = g + r_ref[slot, 0].astype(F32)
        g_ref[...] = g.T[:cols, :]
        delta, m_new, v_new = _adamw(w_ref[...], g_ref[...], m_ref[...], v_ref[...])
        d_ref[...] = delta
        mo_ref[...] = m_new
        vo_ref[...] = v_new

    tile = pl.BlockSpec((cols, tr), lambda t: (0, t))
    out = jax.ShapeDtypeStruct((cols, rows), F32)
    return pl.pallas_call(
        body, name=name, grid=(rows // tr,),
        in_specs=[pl.BlockSpec((nslot, 1, tr, cp), lambda t: (0, 0, t, 0)), tile, tile, tile],
        out_specs=[tile, tile, tile, tile], out_shape=[out, out, out, out], compiler_params=_params("parallel"),
    )(recv, w_t, m_t, v_t)


def adamw_small(g, w, m, v):
    def body(g_ref, w_ref, m_ref, v_ref, d_ref, mo_ref, vo_ref):
        d_ref[...], mo_ref[...], vo_ref[...] = _adamw(w_ref[...], g_ref[...], m_ref[...], v_ref[...])

    out = jax.ShapeDtypeStruct(g.shape, F32)
    return pl.pallas_call(body, name="adamw_small", out_shape=[out, out, out])(g, w, m, v)


def _row8(v):
    return jnp.pad(v.reshape(1, -1).astype(F32), ((0, 7), (0, 0)))


def _row8_lanes(v, width=LANE):
    return jnp.pad(v.reshape(1, -1).astype(F32), ((0, 7), (0, width - v.size)))


def _pack_rows(parts):
    rows = []
    for p in parts:
        p = p.reshape(-1).astype(F32)
        nrow = -(-p.size // D_MODEL)
        rows.append(jnp.pad(p, (0, nrow * D_MODEL - p.size)).reshape(nrow, D_MODEL))
    out = jnp.concatenate(rows, axis=0)
    return jnp.pad(out, ((0, SMALL_ROWS - out.shape[0]), (0, 0)))


def _unpack_rows(pack, shapes):
    out, r = [], 0
    for shp in shapes:
        size = 1
        for d in shp:
            size *= d
        nrow = -(-size // D_MODEL)
        out.append(pack[r:r + nrow].reshape(-1)[:size].reshape(shp))
        r += nrow
    return out


def _dn_weight_layout(gathered):
    split = DN_QKV + 2 * DN_V_HEADS
    pieces = []
    for d in range(N_DEV):
        lo, hi = d * DN_SHARD, (d + 1) * DN_SHARD
        if lo < split < hi:
            pieces += [gathered[d, :, :split - lo], jnp.zeros((D_MODEL, DN_COLS - DN_PROJ), gathered.dtype),
                       gathered[d, :, split - lo:DN_SHARD]]
        else:
            pieces.append(gathered[d, :, :DN_SHARD])
    return jnp.concatenate(pieces, axis=1)


def _dn_grad_blocks(dw):
    split = DN_QKV + 2 * DN_V_HEADS
    gap = DN_COLS - DN_PROJ
    local = lambda c: c if c <= split else c + gap
    zeros = jnp.zeros((D_MODEL, DN_SHARD_PAD - DN_SHARD), dw.dtype)
    blocks = []
    for d in range(N_DEV):
        lo, hi = d * DN_SHARD, (d + 1) * DN_SHARD
        if lo < split < hi:
            parts = [dw[:, lo:split], dw[:, split + gap:hi + gap]]
        else:
            parts = [dw[:, local(lo):local(lo) + DN_SHARD]]
        blocks.append(jnp.concatenate(parts + [zeros], axis=1))
    return jnp.stack(blocks)


def kernel(x, mem, norm_g, mem_norm_g, mem_w_kv, xa_q_norm_g, xa_k_norm_g, w_out, dn_w_in, dn_conv_w, dn_a_log, dn_dt_bias, dn_out_norm_g, sb_w_in, sb_q_norm_g, sb_k_norm_g, loss_target, m_norm_g, m_mem_norm_g, m_mem_w_kv, m_xa_q_norm_g, m_xa_k_norm_g, m_w_out, m_dn_w_in, m_dn_conv_w, m_dn_a_log, m_dn_dt_bias, m_dn_out_norm_g, m_sb_w_in, m_sb_q_norm_g, m_sb_k_norm_g, v_norm_g, v_mem_norm_g, v_mem_w_kv, v_xa_q_norm_g, v_xa_k_norm_g, v_w_out, v_dn_w_in, v_dn_conv_w, v_dn_a_log, v_dn_dt_bias, v_dn_out_norm_g, v_sb_w_in, v_sb_q_norm_g, v_sb_k_norm_g):
    x0, memv, target = x[0], mem[0], loss_target[0]
    my_dev = 4 * lax.axis_index("x") + 2 * lax.axis_index("y") + lax.axis_index("c")

    dn_shard = jnp.pad(dn_w_in[0].astype(BF16), ((0, 0), (0, DN_SHARD_PAD - DN_SHARD)))
    w_out_b = [w_out[i].astype(BF16) for i in range(2)]
    w_kv_b = [mem_w_kv[i].astype(BF16) for i in range(2)]
    conv_shard = jnp.pad(dn_conv_w[0], ((0, 4), (0, 0)))
    ng = [_row8(norm_g[0]), _row8(norm_g[1])]
    mem_g = _row8(mem_norm_g)
    xqg = [_row8(xa_q_norm_g[0]), _row8(xa_q_norm_g[1])]
    xkg = [_row8(xa_k_norm_g[0]), _row8(xa_k_norm_g[1])]
    alog, dtb = _row8_lanes(dn_a_log[0]), _row8_lanes(dn_dt_bias[0])
    out_g, sbq_g, sbk_g = _row8(dn_out_norm_g[0]), _row8(sb_q_norm_g[0]), _row8(sb_k_norm_g[0])

    mem_n = rms_fwd(memv, mem_g)
    h0, h0_t, g_dn, g_conv = rms_fwd(x0, ng[0], with_transpose=True,
                                     exchange=gather_exchange([dn_shard, conv_shard]))
    w_dn = _dn_weight_layout(g_dn)
    conv_w = jnp.transpose(g_conv, (1, 0, 2)).reshape(8, DN_QKV)
    proj0, g_wo0 = matmul(h0, w_dn, "nn", F32, 2048, 640, 2048, "proj_dn", exchange=gather_exchange([w_out_b[0]]))
    act, g_kv0 = dn_conv_fwd(proj0, conv_w, exchange=gather_exchange([w_kv_b[0]]))
    sb_shard, half = sb_w_in[0].astype(BF16), 9 * D_MODEL // 16
    u0, w_, qd, kd, qk, cd, w_sb = dn_prep_fwd(
        act, proj0, alog, dtb, exchange=gather_exchange([sb_shard[:half]], rows=(0, D_MODEL)))
    o_raw, states, w_sb = dn_scan_fwd(
        u0, w_, qd, kd, qk, cd, MIX_WIDTH,
        exchange=gather_exchange([sb_shard[half:]], rows=(half, D_MODEL), into=[w_sb]))
    w_o = [g_wo0.reshape(INNER, D_MODEL), None]
    w_kv = [g_kv0.reshape(D_MODEL, 2 * XA_WIDTH), None]
    kv = [matmul(mem_n, w_kv[0], "nn", F32, 256, 1024, 2048, "kv0"), None]
    cat0 = head_norm_fwd(o_raw, 0, DN_V_HEADS, out_g[None], F32, INNER, "dn_out_norm")
    cat0 = xa_fwd(proj0, DN_XQ_BLK, kv[0], xqg[0], xkg[0], cat0)
    y0 = gate_fwd(cat0, proj0, DN_Z_BLK)
    x1 = matmul(y0, w_o[0], "nn", F32, 1024, 1024, 2048, "out_proj0", add=x0)

    h1, h1_t = rms_fwd(x1, ng[1], with_transpose=True)
    proj1 = matmul(h1, w_sb, "nn", F32, 2048, 896, 2048, "proj_sb", b_blocked=True)
    qkn = head_norm_fwd(proj1, 0, 2 * SB_HEADS, jnp.stack([sbq_g, sbk_g]), BF16, 2 * MIX_WIDTH, "sb_qk_norm")
    v_sb = proj1[:, 2 * MIX_WIDTH:3 * MIX_WIDTH].astype(BF16)
    cat1, ltot, g_wo1, g_kv1 = sb_fwd(qkn, v_sb, 0, INNER, exchange=gather_exchange([w_out_b[1], w_kv_b[1]]))
    w_o[1] = g_wo1.reshape(INNER, D_MODEL)
    w_kv[1] = g_kv1.reshape(D_MODEL, 2 * XA_WIDTH)
    kv[1] = matmul(mem_n, w_kv[1], "nn", F32, 256, 1024, 2048, "kv1")
    cat1 = xa_fwd(proj1, SB_XQ_BLK, kv[1], xqg[1], xkg[1], cat1)
    y1 = gate_fwd(cat1, proj1, SB_Z_BLK)
    x2 = matmul(y1, w_o[1], "nn", F32, 1024, 1024, 2048, "out_proj1", add=x1)
    loss_part, dx2 = loss_fwd_bwd(x2, target)

    dy1 = matmul(dx2, w_o[1], "nt", F32, 1024, 1024, 2048, "d_y1")
    dw_o1 = matmul(y1, dx2, "tn", BF16, 1024, 1024, 2048, "d_w_out1")
    dcat1, dproj1 = gate_bwd(dy1, cat1, proj1, SB_Z_BLK)
    dproj1, dxk1, dxv1, dxqg1, dxkg1 = xa_bwd(proj1, SB_XQ_BLK, kv[1], xqg[1], xkg[1], dcat1, dproj1)
    dqn, dkn, dproj1 = sb_bwd(qkn, proj1, 2 * SB_HEADS, dcat1, ltot, dproj1, v=v_sb)
    dproj1, d_sbq = head_norm_bwd(dqn, 0, proj1, 0, SB_HEADS, sbq_g, dproj1, 0, "sb_q_norm_bwd")
    dproj1, d_sbk = head_norm_bwd(dkn, 0, proj1, SB_HEADS, SB_HEADS, sbk_g, dproj1, SB_HEADS, "sb_k_norm_bwd")
    dw_sb = matmul(h1_t, dproj1, "nn", BF16, 1024, 896, 2048, "d_w_sb", out_blocks=N_DEV)
    dh1 = matmul(dproj1, w_sb, "nt", F32, 1024, 1024, 1792, "d_h1", b_blocked=True)
    dx1, d_ng1 = rms_bwd(dh1, x1, ng[1], dx2)

    by_owner = lambda g, rows: g.reshape(4, 2, rows, g.size // (N_DEV * rows))
    sb_grad = dw_sb.reshape(4, 2, D_MODEL, SB_PROJ // N_DEV)
    dy0, sb_half = matmul(dx1, w_o[0], "nt", F32, 1024, 1024, 2048, "d_y0", exchange=pair_exchange([sb_grad]))
    sb_sum = pair_sum(sb_grad, sb_half, "pair_sum_sb")
    dw_o0 = matmul(y0, dx1, "tn", BF16, 1024, 1024, 2048, "d_w_out0")
    dcat0, dproj0 = gate_bwd(dy0, cat0, proj0, DN_Z_BLK)
    dproj0, dxk0, dxv0, dxqg0, dxkg0 = xa_bwd(proj0, DN_XQ_BLK, kv[0], xqg[0], xkg[0], dcat0, dproj0)

    dkv = [jnp.concatenate([dxk0, dxv0], axis=1), jnp.concatenate([dxk1, dxv1], axis=1)]
    dw_kv = [matmul(mem_n, dkv[i], "tn", BF16, 1024, 1024, 256, f"d_w_kv{i}") for i in range(2)]
    dmem_n = matmul(dkv[0], w_kv[0], "nt", F32, 256, 1024, 2048, "d_mem_n0")
    dmem_n = matmul(dkv[1], w_kv[1], "nt", F32, 256, 1024, 2048, "d_mem_n1", add=dmem_n)
    _, d_memg = rms_bwd(dmem_n, memv, mem_g, None)

    grads = [by_owner(dw_o0, INNER // N_DEV), by_owner(dw_o1, INNER // N_DEV),
             by_owner(dw_kv[0], D_MODEL // N_DEV), by_owner(dw_kv[1], D_MODEL // N_DEV)]
    sb_recv_shape = [(4, 1, D_MODEL, SB_PROJ // N_DEV)]
    sb_first = D_MODEL // 2
    do_raw, d_outg = head_norm_bwd(dcat0, 0, o_raw, 0, DN_V_HEADS, out_g, None, 0, "dn_out_norm_bwd")
    du0, dw_, dqd, dkd, dqk, dcd, *halves, recv_sb = dn_scan_bwd(
        do_raw, u0, w_, qd, kd, qk, cd, states,
        exchange=merge_exchanges(pair_exchange(grads),
                                 chip_exchange([sb_sum], [(0, 0)], sb_recv_shape, rows=(0, sb_first))))
    sums = [pair_sum(g, h, f"pair_sum{i}") for i, (g, h) in enumerate(zip(grads, halves))]
    to_chips = merge_exchanges(
        chip_exchange(sums, [(0, 0), (0, 1), (1, 0), (1, 1)],
                      [(4, 2, INNER // N_DEV, D_MODEL), (4, 2, D_MODEL // N_DEV, 2 * XA_WIDTH)]),
        chip_exchange([sb_sum], [(0, 0)], sb_recv_shape, rows=(sb_first, D_MODEL - sb_first), into=[recv_sb]))
    dq_a, dk_a, dv_a, dab, d_alog, d_dtb, recv_wo, recv_kv, recv_sb = dn_prep_bwd(
        act, proj0, alog, dtb, du0, dw_, dqd, dkd, dqk, dcd, exchange=to_chips)
    dproj0, dcw_q = dn_conv_bwd(dq_a, proj0, conv_w, dproj0, None, 0, "dn_conv_bwd_q")
    dproj0, dcw_k = dn_conv_bwd(dk_a, proj0, conv_w, dproj0, None, DN_QK_HEADS, "dn_conv_bwd_k")
    dproj0, dcw_v = dn_conv_bwd(dv_a, proj0, conv_w, dproj0, dab, 2 * DN_QK_HEADS, "dn_conv_bwd_v")
    dw_dn = matmul(h0_t, dproj0, "nn", BF16, 1024, 1152, 2048, "d_w_dn")
    dn_grad = _dn_grad_blocks(dw_dn).reshape(4, 2, D_MODEL, DN_SHARD_PAD)
    dn_half, = run_exchange(pair_exchange([dn_grad]), "pair_dn")
    dn_sum = pair_sum(dn_grad, dn_half, "pair_sum_dn")
    dh0, recv_dn = matmul(dproj0, w_dn, "nt", F32, 1024, 1024, 2304, "d_h0",
                          exchange=chip_exchange([dn_sum], [(0, 0)], [(4, 1, D_MODEL, DN_SHARD_PAD)]))
    grad_x, d_ng0 = rms_bwd(dh0, x0, ng[0], dx1)

    big = {
        "dn_w_in": tuple(jnp.transpose(a)[None] for a in reduce_adamw_t(
            recv_dn, jnp.transpose(dn_w_in[0]), jnp.transpose(m_dn_w_in[0]), jnp.transpose(v_dn_w_in[0]),
            "adamw_dn_w_in")),
        "sb_w_in": reduce_adamw(recv_sb, sb_w_in, m_sb_w_in, v_sb_w_in, "adamw_sb_w_in"),
        "w_out": reduce_adamw(recv_wo, w_out, m_w_out, v_w_out, "adamw_w_out"),
        "mem_w_kv": reduce_adamw(recv_kv, mem_w_kv, m_mem_w_kv, v_mem_w_kv, "adamw_mem_w_kv"),
    }

    dconv = jnp.concatenate([dcw_q, dcw_k, dcw_v], axis=1)[:4]
    small_shapes = [(2, D_MODEL), (D_MODEL,), (2, XA_DIM), (2, XA_DIM), (4, DN_QKV), (1, DN_V_HEADS),
                    (1, DN_V_HEADS), (1, HEAD_DIM), (1, HEAD_DIM), (1, HEAD_DIM), (1,)]
    pack = _pack_rows([jnp.stack([d_ng0[0], d_ng1[0]]), d_memg[0], jnp.stack([dxqg0[0], dxqg1[0]]),
                       jnp.stack([dxkg0[0], dxkg1[0]]), dconv, d_alog[0, :DN_V_HEADS], d_dtb[0, :DN_V_HEADS],
                       d_outg[0], d_sbq[0], d_sbk[0], loss_part[0, :1]])
    total = all_reduce_small(pack)
    (g_norm, g_memn, g_xq, g_xk, g_conv_full, g_alog, g_dtb, g_outn, g_sbq, g_sbk, loss1) = _unpack_rows(total, small_shapes)
    conv_cols = DN_QKV // N_DEV
    g_conv = lax.dynamic_slice(g_conv_full, (0, my_dev * conv_cols), (4, conv_cols))[None]
    names = ["norm_g", "mem_norm_g", "xa_q_norm_g", "xa_k_norm_g", "dn_conv_w", "dn_a_log", "dn_dt_bias",
             "dn_out_norm_g", "sb_q_norm_g", "sb_k_norm_g"]
    grads = [g_norm, g_memn, g_xq, g_xk, g_conv, g_alog, g_dtb, g_outn, g_sbq, g_sbk]
    ws = [norm_g, mem_norm_g, xa_q_norm_g, xa_k_norm_g, dn_conv_w, dn_a_log, dn_dt_bias, dn_out_norm_g, sb_q_norm_g,
          sb_k_norm_g]
    ms = [m_norm_g, m_mem_norm_g, m_xa_q_norm_g, m_xa_k_norm_g, m_dn_conv_w, m_dn_a_log, m_dn_dt_bias,
          m_dn_out_norm_g, m_sb_q_norm_g, m_sb_k_norm_g]
    vs = [v_norm_g, v_mem_norm_g, v_xa_q_norm_g, v_xa_k_norm_g, v_dn_conv_w, v_dn_a_log, v_dn_dt_bias,
          v_dn_out_norm_g, v_sb_q_norm_g, v_sb_k_norm_g]
    shapes = [w.shape for w in ws]
    d_p, m_p, v_p = adamw_small(_pack_rows(grads), _pack_rows(ws), _pack_rows(ms), _pack_rows(vs))
    small = dict(zip(names, zip(grads, _unpack_rows(d_p, shapes), _unpack_rows(m_p, shapes), _unpack_rows(v_p, shapes))))

    order = ["norm_g", "mem_norm_g", "mem_w_kv", "xa_q_norm_g", "xa_k_norm_g", "w_out", "dn_w_in", "dn_conv_w",
             "dn_a_log", "dn_dt_bias", "dn_out_norm_g", "sb_w_in", "sb_q_norm_g", "sb_k_norm_g"]
    res = {**big, **small}
    outs = [loss1.reshape(()), grad_x[None]]
    for k in range(4):
        outs += [res[n][k] for n in order]
    return tuple(outs)
```
